```python
import jax
import jax.numpy as jnp
from jax import lax
import numpy as np

D_MODEL = 1024
BATCH = 8
SEQ = 4096
DEPTH = 1

N_META = 16
HEAD_DIM = 64
ATTN_Q_HEADS = 8
ATTN_KV_HEADS = 2
ATTN_GROUP = ATTN_Q_HEADS // ATTN_KV_HEADS
WINDOW = 128
BLOCK = 128
ROPE_THETA = 500000.0
ROPE_DIM = HEAD_DIM // 4
RWKV_HEADS = 8
RWKV_HEAD = 64
RWKV_DIM = RWKV_HEADS * RWKV_HEAD
DECAY_LORA = 64
AAA_LORA = 64
GATE_LORA = 160
RWKV_LN_EPS = 64e-5
D_FF = -(-8 * D_MODEL // (3 * 256)) * 256
Q_W = ATTN_Q_HEADS * HEAD_DIM
KV_W = ATTN_KV_HEADS * HEAD_DIM
ATTN_PROJ = Q_W + 2 * KV_W
RWKV_PROJ = 3 * RWKV_DIM + DECAY_LORA + AAA_LORA + GATE_LORA
D_IN = ATTN_PROJ + RWKV_PROJ + 2 * D_MODEL
RMS_EPS = 1e-6
NEG_INF = -1e30

kernel_name = 'hybrid_swa_sink_rwkv7_gated_block'


def rms_norm(x, g):
    xf = x.astype(jnp.float32)
    y = xf * lax.rsqrt(jnp.mean(xf * xf, axis=-1, keepdims=True) + RMS_EPS)
    return (y * g.astype(jnp.float32)).astype(x.dtype)


def partial_rope(t, pos):
    half = ROPE_DIM // 2
    inv_freq = jnp.power(jnp.float32(ROPE_THETA), -jnp.arange(half, dtype=jnp.float32) * (2.0 / ROPE_DIM))
    ang = pos.astype(jnp.float32)[:, None] * inv_freq[None, :]
    cos = jnp.cos(ang)[None, :, None, :]
    sin = jnp.sin(ang)[None, :, None, :]
    tf = t.astype(jnp.float32)
    t1 = tf[..., :half]
    t2 = tf[..., half:ROPE_DIM]
    out = jnp.concatenate([t1 * cos - t2 * sin, t2 * cos + t1 * sin, tf[..., ROPE_DIM:]], axis=-1)
    return out.astype(t.dtype)


def sliding_window_sink_attention(q, k, v, sinks):
    B, L = q.shape[0], q.shape[1]
    pad = BLOCK - N_META
    n_blk = (L + pad) // BLOCK

    def blockify(t):
        t = jnp.pad(t, ((0, 0), (pad, 0), (0, 0), (0, 0)))
        return t.reshape(B, n_blk, BLOCK, t.shape[2], t.shape[3])

    def prev_block(t):
        return jnp.pad(t, ((0, 0), (1, 0), (0, 0), (0, 0), (0, 0)))[:, :-1]

    qb = blockify(q).reshape(B, n_blk, BLOCK, ATTN_KV_HEADS, ATTN_GROUP, HEAD_DIM)
    kb = blockify(k)
    vb = blockify(v)
    k_band = jnp.concatenate([prev_block(kb), kb], axis=2)
    v_band = jnp.concatenate([prev_block(vb), vb], axis=2)
    k_meta = k[:, :N_META]
    v_meta = v[:, :N_META]

    scale = HEAD_DIM ** -0.5
    s_band = jnp.einsum('bnqhgd,bnkhd->bhgnqk', qb, k_band).astype(jnp.float32) * scale
    s_meta = jnp.einsum('bnqhgd,bmhd->bhgnqm', qb, k_meta).astype(jnp.float32) * scale

    q_pos = jnp.arange(n_blk * BLOCK).reshape(n_blk, BLOCK) - pad
    k_pos = (jnp.arange(n_blk)[:, None] - 1) * BLOCK + jnp.arange(2 * BLOCK)[None, :] - pad
    qp = q_pos[:, :, None]
    kp = k_pos[:, None, :]
    band_ok = (kp >= N_META) & (kp <= qp) & (qp - kp < WINDOW)
    meta_ok = jnp.arange(N_META)[None, None, :] <= qp
    s_band = jnp.where(band_ok[None, None, None], s_band, NEG_INF)
    s_meta = jnp.where(meta_ok[None, None, None], s_meta, NEG_INF)
    sink = sinks.astype(jnp.float32).reshape(ATTN_KV_HEADS, ATTN_GROUP)[None, :, :, None, None, None]
    sink = jnp.broadcast_to(sink, s_band.shape[:-1] + (1,))

    probs = jax.nn.softmax(jnp.concatenate([s_meta, s_band, sink], axis=-1), axis=-1)
    p_meta = probs[..., :N_META].astype(v.dtype)
    p_band = probs[..., N_META:N_META + 2 * BLOCK].astype(v.dtype)
    out = (jnp.einsum('bhgnqm,bmhd->bnqhgd', p_meta, v_meta)
           + jnp.einsum('bhgnqk,bnkhd->bnqhgd', p_band, v_band))
    return out.reshape(B, n_blk * BLOCK, Q_W)[:, pad:]


def token_shift(t):
    return jnp.pad(t, ((0, 0), (1, 0), (0, 0)))[:, :-1]


def wkv7_scan(r, decay, k, v, aa, bb):
    B, L, H, N = r.shape

    def step(S, inp):
        r_t, w_t, k_t, v_t, a_t, b_t = inp
        sa = jnp.einsum('bhvk,bhk->bhv', S, a_t)
        S = S * w_t[:, :, None, :] + sa[..., None] * b_t[:, :, None, :] + v_t[..., None] * k_t[:, :, None, :]
        y = jnp.einsum('bhvk,bhk->bhv', S, r_t)
        return S, y

    xs = (jnp.moveaxis(r, 1, 0), jnp.moveaxis(decay, 1, 0), jnp.moveaxis(k, 1, 0),
          jnp.moveaxis(v, 1, 0), jnp.moveaxis(aa, 1, 0), jnp.moveaxis(bb, 1, 0))
    S0 = jnp.zeros((B, H, N, N), jnp.float32)
    _, ys = lax.scan(step, S0, xs)
    return jnp.moveaxis(ys, 0, 1)


def rwkv7_time_mix(p, mix, w0, w2, a0, a2, g2, k_k, k_a, r_k, ln_w, ln_b):
    B, L = p.shape[0], p.shape[1]
    f32 = jnp.float32
    pf = p.astype(f32)
    pf = pf + (token_shift(pf) - pf) * mix.astype(f32)
    o1, o2, o3 = RWKV_DIM, 2 * RWKV_DIM, 3 * RWKV_DIM
    o4 = o3 + DECAY_LORA
    o5 = o4 + AAA_LORA
    r = pf[..., :o1]
    k = pf[..., o1:o2]
    v = pf[..., o2:o3]
    dw = pf[..., o3:o4]
    da = pf[..., o4:o5]
    dg = pf[..., o5:]
    w = -jax.nn.softplus(-(w0.astype(f32) + jnp.tanh(dw) @ w2.astype(f32))) - 0.5
    a = jax.nn.sigmoid(a0.astype(f32) + da @ a2.astype(f32))
    g = jax.nn.sigmoid(dg) @ g2.astype(f32)
    hs = (B, L, RWKV_HEADS, RWKV_HEAD)
    kk = (k * k_k.astype(f32)).reshape(hs)
    kk = kk / jnp.maximum(jnp.sqrt(jnp.sum(kk * kk, axis=-1, keepdims=True)), 1e-12)
    k = k * (1.0 + (a - 1.0) * k_a.astype(f32))
    r = r.reshape(hs)
    k = k.reshape(hs)
    v = v.reshape(hs)
    a = a.reshape(hs)
    decay = jnp.exp(-jnp.exp(w)).reshape(hs)
    y = wkv7_scan(r, decay, k, v, -kk, kk * a)
    mean = jnp.mean(y, axis=-1, keepdims=True)
    var = jnp.mean(jnp.square(y - mean), axis=-1, keepdims=True)
    y = ((y - mean) * lax.rsqrt(var + RWKV_LN_EPS) * ln_w.astype(f32).reshape(RWKV_HEADS, RWKV_HEAD)
         + ln_b.astype(f32).reshape(RWKV_HEADS, RWKV_HEAD))
    y = y + jnp.sum(r * k * r_k.astype(f32), axis=-1, keepdims=True) * v
    return (y.reshape(B, L, RWKV_DIM) * g).astype(p.dtype)


def _fwd_setup_inputs(seed: int = 0) -> dict:
    key = jax.random.key(seed)
    ks = jax.random.split(key, 25)
    f32 = jnp.float32

    def nrm(k, shape, scale):
        return jax.random.normal(k, shape, f32) * scale

    def unif(k, shape, lo, hi):
        return jax.random.uniform(k, shape, f32, lo, hi)

    Dp = DEPTH
    return {
        'x': nrm(ks[0], (BATCH, SEQ, D_MODEL), 1.0),
        'meta_tokens': nrm(ks[1], (N_META, D_MODEL), 1.0),
        'norm_mix_g': 1.0 + nrm(ks[2], (Dp, D_MODEL), 0.02),
        'w_in': nrm(ks[3], (Dp, D_MODEL, D_IN), D_MODEL ** -0.5),
        'b_in': nrm(ks[4], (Dp, D_IN), 0.02),
        'attn_sinks': nrm(ks[5], (Dp, ATTN_Q_HEADS), 1.0),
        'rwkv_mix': unif(ks[6], (Dp, RWKV_PROJ), 0.0, 1.0),
        'rwkv_w0': unif(ks[7], (Dp, RWKV_DIM), -6.0, -1.0),
        'rwkv_w2': nrm(ks[8], (Dp, DECAY_LORA, RWKV_DIM), 0.1 * DECAY_LORA ** -0.5),
        'rwkv_a0': nrm(ks[9], (Dp, RWKV_DIM), 0.1),
        'rwkv_a2': nrm(ks[10], (Dp, AAA_LORA, RWKV_DIM), 0.5 * AAA_LORA ** -0.5),
        'rwkv_g2': nrm(ks[11], (Dp, GATE_LORA, RWKV_DIM), GATE_LORA ** -0.5),
        'rwkv_k_k': 0.85 + nrm(ks[12], (Dp, RWKV_DIM), 0.02),
        'rwkv_k_a': 1.0 + nrm(ks[13], (Dp, RWKV_DIM), 0.02),
        'rwkv_r_k': -0.04 + nrm(ks[14], (Dp, RWKV_HEADS, RWKV_HEAD), 0.02),
        'rwkv_ln_w': 1.0 + nrm(ks[15], (Dp, RWKV_DIM), 0.02),
        'rwkv_ln_b': nrm(ks[16], (Dp, RWKV_DIM), 0.02),
        'w_br_attn': nrm(ks[17], (Dp, Q_W, D_MODEL), Q_W ** -0.5),
        'w_br_rwkv': nrm(ks[18], (Dp, RWKV_DIM, D_MODEL), RWKV_DIM ** -0.5),
        'w_o': nrm(ks[19], (Dp, D_MODEL, D_MODEL), D_MODEL ** -0.5),
        'norm_ffn_g': 1.0 + nrm(ks[20], (Dp, D_MODEL), 0.02),
        'w_ffn_gate': nrm(ks[21], (Dp, D_MODEL, D_FF), D_MODEL ** -0.5),
        'w_ffn_up': nrm(ks[22], (Dp, D_MODEL, D_FF), D_MODEL ** -0.5),
        'w_ffn_down': nrm(ks[23], (Dp, D_FF, D_MODEL), D_FF ** -0.5),
        'norm_final_g': 1.0 + nrm(ks[24], (D_MODEL,), 0.02),
    }


def _fwd_reference(x, meta_tokens, norm_mix_g, w_in, b_in, attn_sinks, rwkv_mix, rwkv_w0, rwkv_w2,
              rwkv_a0, rwkv_a2, rwkv_g2, rwkv_k_k, rwkv_k_a, rwkv_r_k, rwkv_ln_w, rwkv_ln_b,
              w_br_attn, w_br_rwkv, w_o, norm_ffn_g, w_ffn_gate, w_ffn_up, w_ffn_down,
              norm_final_g):
    B = x.shape[0]
    meta = jnp.broadcast_to(meta_tokens.astype(x.dtype)[None], (B, N_META, D_MODEL))
    h = jnp.concatenate([meta, x], axis=1)
    L = h.shape[1]
    pos = jnp.arange(L, dtype=jnp.int32)
    for layer in range(DEPTH):
        u = rms_norm(h, norm_mix_g[layer])
        proj = u @ w_in[layer] + b_in[layer]
        p_attn = proj[..., :ATTN_PROJ]
        p_rwkv = proj[..., ATTN_PROJ:ATTN_PROJ + RWKV_PROJ]
        gates = jax.nn.sigmoid(proj[..., ATTN_PROJ + RWKV_PROJ:].astype(jnp.float32)).astype(h.dtype)
        q = p_attn[..., :Q_W].reshape(B, L, ATTN_Q_HEADS, HEAD_DIM)
        k = p_attn[..., Q_W:Q_W + KV_W].reshape(B, L, ATTN_KV_HEADS, HEAD_DIM)
        v = p_attn[..., Q_W + KV_W:].reshape(B, L, ATTN_KV_HEADS, HEAD_DIM)
        q = partial_rope(q, pos)
        k = partial_rope(k, pos)
        y_attn = sliding_window_sink_attention(q, k, v, attn_sinks[layer])
        y_rwkv = rwkv7_time_mix(p_rwkv, rwkv_mix[layer], rwkv_w0[layer], rwkv_w2[layer],
                                rwkv_a0[layer], rwkv_a2[layer], rwkv_g2[layer], rwkv_k_k[layer],
                                rwkv_k_a[layer], rwkv_r_k[layer], rwkv_ln_w[layer],
                                rwkv_ln_b[layer])
        merged = (gates[..., :D_MODEL] * (y_attn @ w_br_attn[layer])
                  + gates[..., D_MODEL:] * (y_rwkv @ w_br_rwkv[layer]))
        h = h + merged @ w_o[layer]
        f = rms_norm(h, norm_ffn_g[layer])
        h = h + (jax.nn.silu(f @ w_ffn_gate[layer]) * (f @ w_ffn_up[layer])) @ w_ffn_down[layer]
    return rms_norm(h, norm_final_g)[:, N_META:]


import jax as _jax
import jax.numpy as _jnp

TWIN_FORMAT = 'train_step'
FWD_PARAMS = ['x', 'meta_tokens', 'norm_mix_g', 'w_in', 'b_in', 'attn_sinks', 'rwkv_mix', 'rwkv_w0', 'rwkv_w2', 'rwkv_a0', 'rwkv_a2', 'rwkv_g2', 'rwkv_k_k', 'rwkv_k_a', 'rwkv_r_k', 'rwkv_ln_w', 'rwkv_ln_b', 'w_br_attn', 'w_br_rwkv', 'w_o', 'norm_ffn_g', 'w_ffn_gate', 'w_ffn_up', 'w_ffn_down', 'norm_final_g']
TWIN_WEIGHTS = ['meta_tokens', 'norm_mix_g', 'w_in', 'b_in', 'attn_sinks', 'rwkv_mix', 'rwkv_w0', 'rwkv_w2', 'rwkv_a0', 'rwkv_a2', 'rwkv_g2', 'rwkv_k_k', 'rwkv_k_a', 'rwkv_r_k', 'rwkv_ln_w', 'rwkv_ln_b', 'w_br_attn', 'w_br_rwkv', 'w_o', 'norm_ffn_g', 'w_ffn_gate', 'w_ffn_up', 'w_ffn_down', 'norm_final_g']
TWIN_DIFF_INPUT = 'x'
TWIN_INPUTS = ['x', 'meta_tokens', 'norm_mix_g', 'w_in', 'b_in', 'attn_sinks', 'rwkv_mix', 'rwkv_w0', 'rwkv_w2', 'rwkv_a0', 'rwkv_a2', 'rwkv_g2', 'rwkv_k_k', 'rwkv_k_a', 'rwkv_r_k', 'rwkv_ln_w', 'rwkv_ln_b', 'w_br_attn', 'w_br_rwkv', 'w_o', 'norm_ffn_g', 'w_ffn_gate', 'w_ffn_up', 'w_ffn_down', 'norm_final_g', 'loss_target', 'm_meta_tokens', 'm_norm_mix_g', 'm_w_in', 'm_b_in', 'm_attn_sinks', 'm_rwkv_mix', 'm_rwkv_w0', 'm_rwkv_w2', 'm_rwkv_a0', 'm_rwkv_a2', 'm_rwkv_g2', 'm_rwkv_k_k', 'm_rwkv_k_a', 'm_rwkv_r_k', 'm_rwkv_ln_w', 'm_rwkv_ln_b', 'm_w_br_attn', 'm_w_br_rwkv', 'm_w_o', 'm_norm_ffn_g', 'm_w_ffn_gate', 'm_w_ffn_up', 'm_w_ffn_down', 'm_norm_final_g', 'v_meta_tokens', 'v_norm_mix_g', 'v_w_in', 'v_b_in', 'v_attn_sinks', 'v_rwkv_mix', 'v_rwkv_w0', 'v_rwkv_w2', 'v_rwkv_a0', 'v_rwkv_a2', 'v_rwkv_g2', 'v_rwkv_k_k', 'v_rwkv_k_a', 'v_rwkv_r_k', 'v_rwkv_ln_w', 'v_rwkv_ln_b', 'v_w_br_attn', 'v_w_br_rwkv', 'v_w_o', 'v_norm_ffn_g', 'v_w_ffn_gate', 'v_w_ffn_up', 'v_w_ffn_down', 'v_norm_final_g']
TWIN_OUTPUTS = ['loss', 'grad_x', 'grad_meta_tokens', 'grad_norm_mix_g', 'grad_w_in', 'grad_b_in', 'grad_attn_sinks', 'grad_rwkv_mix', 'grad_rwkv_w0', 'grad_rwkv_w2', 'grad_rwkv_a0', 'grad_rwkv_a2', 'grad_rwkv_g2', 'grad_rwkv_k_k', 'grad_rwkv_k_a', 'grad_rwkv_r_k', 'grad_rwkv_ln_w', 'grad_rwkv_ln_b', 'grad_w_br_attn', 'grad_w_br_rwkv', 'grad_w_o', 'grad_norm_ffn_g', 'grad_w_ffn_gate', 'grad_w_ffn_up', 'grad_w_ffn_down', 'grad_norm_final_g', 'delta_meta_tokens', 'delta_norm_mix_g', 'delta_w_in', 'delta_b_in', 'delta_attn_sinks', 'delta_rwkv_mix', 'delta_rwkv_w0', 'delta_rwkv_w2', 'delta_rwkv_a0', 'delta_rwkv_a2', 'delta_rwkv_g2', 'delta_rwkv_k_k', 'delta_rwkv_k_a', 'delta_rwkv_r_k', 'delta_rwkv_ln_w', 'delta_rwkv_ln_b', 'delta_w_br_attn', 'delta_w_br_rwkv', 'delta_w_o', 'delta_norm_ffn_g', 'delta_w_ffn_gate', 'delta_w_ffn_up', 'delta_w_ffn_down', 'delta_norm_final_g', 'new_m_meta_tokens', 'new_m_norm_mix_g', 'new_m_w_in', 'new_m_b_in', 'new_m_attn_sinks', 'new_m_rwkv_mix', 'new_m_rwkv_w0', 'new_m_rwkv_w2', 'new_m_rwkv_a0', 'new_m_rwkv_a2', 'new_m_rwkv_g2', 'new_m_rwkv_k_k', 'new_m_rwkv_k_a', 'new_m_rwkv_r_k', 'new_m_rwkv_ln_w', 'new_m_rwkv_ln_b', 'new_m_w_br_attn', 'new_m_w_br_rwkv', 'new_m_w_o', 'new_m_norm_ffn_g', 'new_m_w_ffn_gate', 'new_m_w_ffn_up', 'new_m_w_ffn_down', 'new_m_norm_final_g', 'new_v_meta_tokens', 'new_v_norm_mix_g', 'new_v_w_in', 'new_v_b_in', 'new_v_attn_sinks', 'new_v_rwkv_mix', 'new_v_rwkv_w0', 'new_v_rwkv_w2', 'new_v_rwkv_a0', 'new_v_rwkv_a2', 'new_v_rwkv_g2', 'new_v_rwkv_k_k', 'new_v_rwkv_k_a', 'new_v_rwkv_r_k', 'new_v_rwkv_ln_w', 'new_v_rwkv_ln_b', 'new_v_w_br_attn', 'new_v_w_br_rwkv', 'new_v_w_o', 'new_v_norm_ffn_g', 'new_v_w_ffn_gate', 'new_v_w_ffn_up', 'new_v_w_ffn_down', 'new_v_norm_final_g']
TWIN_LEAF_KINDS = {'loss': 'loss', 'grad_x': 'grad_x', 'grad_meta_tokens': 'grad_w', 'grad_norm_mix_g': 'grad_w', 'grad_w_in': 'grad_w', 'grad_b_in': 'grad_w', 'grad_attn_sinks': 'grad_w', 'grad_rwkv_mix': 'grad_w', 'grad_rwkv_w0': 'grad_w', 'grad_rwkv_w2': 'grad_w', 'grad_rwkv_a0': 'grad_w', 'grad_rwkv_a2': 'grad_w', 'grad_rwkv_g2': 'grad_w', 'grad_rwkv_k_k': 'grad_w', 'grad_rwkv_k_a': 'grad_w', 'grad_rwkv_r_k': 'grad_w', 'grad_rwkv_ln_w': 'grad_w', 'grad_rwkv_ln_b': 'grad_w', 'grad_w_br_attn': 'grad_w', 'grad_w_br_rwkv': 'grad_w', 'grad_w_o': 'grad_w', 'grad_norm_ffn_g': 'grad_w', 'grad_w_ffn_gate': 'grad_w', 'grad_w_ffn_up': 'grad_w', 'grad_w_ffn_down': 'grad_w', 'grad_norm_final_g': 'grad_w', 'delta_meta_tokens': 'delta_w', 'delta_norm_mix_g': 'delta_w', 'delta_w_in': 'delta_w', 'delta_b_in': 'delta_w', 'delta_attn_sinks': 'delta_w', 'delta_rwkv_mix': 'delta_w', 'delta_rwkv_w0': 'delta_w', 'delta_rwkv_w2': 'delta_w', 'delta_rwkv_a0': 'delta_w', 'delta_rwkv_a2': 'delta_w', 'delta_rwkv_g2': 'delta_w', 'delta_rwkv_k_k': 'delta_w', 'delta_rwkv_k_a': 'delta_w', 'delta_rwkv_r_k': 'delta_w', 'delta_rwkv_ln_w': 'delta_w', 'delta_rwkv_ln_b': 'delta_w', 'delta_w_br_attn': 'delta_w', 'delta_w_br_rwkv': 'delta_w', 'delta_w_o': 'delta_w', 'delta_norm_ffn_g': 'delta_w', 'delta_w_ffn_gate': 'delta_w', 'delta_w_ffn_up': 'delta_w', 'delta_w_ffn_down': 'delta_w', 'delta_norm_final_g': 'delta_w', 'new_m_meta_tokens': 'new_m', 'new_m_norm_mix_g': 'new_m', 'new_m_w_in': 'new_m', 'new_m_b_in': 'new_m', 'new_m_attn_sinks': 'new_m', 'new_m_rwkv_mix': 'new_m', 'new_m_rwkv_w0': 'new_m', 'new_m_rwkv_w2': 'new_m', 'new_m_rwkv_a0': 'new_m', 'new_m_rwkv_a2': 'new_m', 'new_m_rwkv_g2': 'new_m', 'new_m_rwkv_k_k': 'new_m', 'new_m_rwkv_k_a': 'new_m', 'new_m_rwkv_r_k': 'new_m', 'new_m_rwkv_ln_w': 'new_m', 'new_m_rwkv_ln_b': 'new_m', 'new_m_w_br_attn': 'new_m', 'new_m_w_br_rwkv': 'new_m', 'new_m_w_o': 'new_m', 'new_m_norm_ffn_g': 'new_m', 'new_m_w_ffn_gate': 'new_m', 'new_m_w_ffn_up': 'new_m', 'new_m_w_ffn_down': 'new_m', 'new_m_norm_final_g': 'new_m', 'new_v_meta_tokens': 'new_v', 'new_v_norm_mix_g': 'new_v', 'new_v_w_in': 'new_v', 'new_v_b_in': 'new_v', 'new_v_attn_sinks': 'new_v', 'new_v_rwkv_mix': 'new_v', 'new_v_rwkv_w0': 'new_v', 'new_v_rwkv_w2': 'new_v', 'new_v_rwkv_a0': 'new_v', 'new_v_rwkv_a2': 'new_v', 'new_v_rwkv_g2': 'new_v', 'new_v_rwkv_k_k': 'new_v', 'new_v_rwkv_k_a': 'new_v', 'new_v_rwkv_r_k': 'new_v', 'new_v_rwkv_ln_w': 'new_v', 'new_v_rwkv_ln_b': 'new_v', 'new_v_w_br_attn': 'new_v', 'new_v_w_br_rwkv': 'new_v', 'new_v_w_o': 'new_v', 'new_v_norm_ffn_g': 'new_v', 'new_v_w_ffn_gate': 'new_v', 'new_v_w_ffn_up': 'new_v', 'new_v_w_ffn_down': 'new_v', 'new_v_norm_final_g': 'new_v'}


def _forward(args):
    return _fwd_reference(*[args[k] for k in FWD_PARAMS])


def _output_shape():
    out = _jax.eval_shape(lambda: _forward(_fwd_setup_inputs(0)))
    return out.shape, out.dtype

N_MICROBATCH = 1
ADAM_LR = 0.001
ADAM_B1 = 0.9
ADAM_B2 = 0.999
ADAM_EPS = 1e-08
ADAM_WD = 0.01
ADAM_STEP = 10
PER_EXAMPLE_BATCH_AXIS = {'x': 0, 'loss_target': 0}
SHARED_INPUTS = []
_WEIGHT_DTYPES = {'meta_tokens': _jnp.float32, 'norm_mix_g': _jnp.float32, 'w_in': _jnp.float32, 'b_in': _jnp.float32, 'attn_sinks': _jnp.float32, 'rwkv_mix': _jnp.float32, 'rwkv_w0': _jnp.float32, 'rwkv_w2': _jnp.float32, 'rwkv_a0': _jnp.float32, 'rwkv_a2': _jnp.float32, 'rwkv_g2': _jnp.float32, 'rwkv_k_k': _jnp.float32, 'rwkv_k_a': _jnp.float32, 'rwkv_r_k': _jnp.float32, 'rwkv_ln_w': _jnp.float32, 'rwkv_ln_b': _jnp.float32, 'w_br_attn': _jnp.float32, 'w_br_rwkv': _jnp.float32, 'w_o': _jnp.float32, 'norm_ffn_g': _jnp.float32, 'w_ffn_gate': _jnp.float32, 'w_ffn_up': _jnp.float32, 'w_ffn_down': _jnp.float32, 'norm_final_g': _jnp.float32}
MOMENT_SCALE = {'meta_tokens': 6.574496e-03, 'norm_mix_g': 1.115560e-01, 'w_in': 5.022348e-02, 'b_in': 7.759106e-02, 'attn_sinks': 4.129134e-03, 'rwkv_mix': 1.210959e-01, 'rwkv_w0': 3.136847e-02, 'rwkv_w2': 3.481202e-03, 'rwkv_a0': 3.298211e-02, 'rwkv_a2': 3.064119e-02, 'rwkv_g2': 7.903286e-02, 'rwkv_k_k': 1.127347e-01, 'rwkv_k_a': 8.576206e-02, 'rwkv_r_k': 1.739529e-01, 'rwkv_ln_w': 7.677464e-02, 'rwkv_ln_b': 7.362360e-02, 'w_br_attn': 1.732809e-02, 'w_br_rwkv': 5.403220e-02, 'w_o': 5.685608e-02, 'norm_ffn_g': 1.298560e-01, 'w_ffn_gate': 5.489151e-02, 'w_ffn_up': 5.305953e-02, 'w_ffn_down': 8.840413e-02, 'norm_final_g': 3.201866e+01}


def _to_microbatches(a, axis):
    t = _jnp.moveaxis(a, axis, 0)
    t = t.reshape((N_MICROBATCH, t.shape[0] // N_MICROBATCH) + t.shape[1:])
    return _jnp.moveaxis(t, 1, axis + 1)


def setup_inputs(seed: int = 0) -> dict:
    inp = _fwd_setup_inputs(seed)
    key = _jax.random.fold_in(_jax.random.key(seed), 7919)
    shape, _ = _output_shape()
    out = dict(inp)
    out["loss_target"] = _jax.random.normal(_jax.random.fold_in(key, 0), shape, _jnp.float32)
    for i, name in enumerate(TWIN_WEIGHTS):
        w = inp[name].astype(_jnp.float32)
        if MOMENT_SCALE is None:
            s = _jnp.sqrt(_jnp.mean(_jnp.square(w)) + 1e-30)
        else:
            s = MOMENT_SCALE[name]
        km, kv = _jax.random.split(_jax.random.fold_in(key, i + 1))
        out[name] = w
        out["m_" + name] = s * _jax.random.normal(km, w.shape, _jnp.float32)
        out["v_" + name] = (s * s) * _jax.random.uniform(kv, w.shape, _jnp.float32, 0.5, 1.5)
    if N_MICROBATCH > 1:
        for name, axis in PER_EXAMPLE_BATCH_AXIS.items():
            out[name] = _to_microbatches(out[name], axis)
    return {'x': out['x'], 'meta_tokens': out['meta_tokens'], 'norm_mix_g': out['norm_mix_g'], 'w_in': out['w_in'], 'b_in': out['b_in'], 'attn_sinks': out['attn_sinks'], 'rwkv_mix': out['rwkv_mix'], 'rwkv_w0': out['rwkv_w0'], 'rwkv_w2': out['rwkv_w2'], 'rwkv_a0': out['rwkv_a0'], 'rwkv_a2': out['rwkv_a2'], 'rwkv_g2': out['rwkv_g2'], 'rwkv_k_k': out['rwkv_k_k'], 'rwkv_k_a': out['rwkv_k_a'], 'rwkv_r_k': out['rwkv_r_k'], 'rwkv_ln_w': out['rwkv_ln_w'], 'rwkv_ln_b': out['rwkv_ln_b'], 'w_br_attn': out['w_br_attn'], 'w_br_rwkv': out['w_br_rwkv'], 'w_o': out['w_o'], 'norm_ffn_g': out['norm_ffn_g'], 'w_ffn_gate': out['w_ffn_gate'], 'w_ffn_up': out['w_ffn_up'], 'w_ffn_down': out['w_ffn_down'], 'norm_final_g': out['norm_final_g'], 'loss_target': out['loss_target'], 'm_meta_tokens': out['m_meta_tokens'], 'm_norm_mix_g': out['m_norm_mix_g'], 'm_w_in': out['m_w_in'], 'm_b_in': out['m_b_in'], 'm_attn_sinks': out['m_attn_sinks'], 'm_rwkv_mix': out['m_rwkv_mix'], 'm_rwkv_w0': out['m_rwkv_w0'], 'm_rwkv_w2': out['m_rwkv_w2'], 'm_rwkv_a0': out['m_rwkv_a0'], 'm_rwkv_a2': out['m_rwkv_a2'], 'm_rwkv_g2': out['m_rwkv_g2'], 'm_rwkv_k_k': out['m_rwkv_k_k'], 'm_rwkv_k_a': out['m_rwkv_k_a'], 'm_rwkv_r_k': out['m_rwkv_r_k'], 'm_rwkv_ln_w': out['m_rwkv_ln_w'], 'm_rwkv_ln_b': out['m_rwkv_ln_b'], 'm_w_br_attn': out['m_w_br_attn'], 'm_w_br_rwkv': out['m_w_br_rwkv'], 'm_w_o': out['m_w_o'], 'm_norm_ffn_g': out['m_norm_ffn_g'], 'm_w_ffn_gate': out['m_w_ffn_gate'], 'm_w_ffn_up': out['m_w_ffn_up'], 'm_w_ffn_down': out['m_w_ffn_down'], 'm_norm_final_g': out['m_norm_final_g'], 'v_meta_tokens': out['v_meta_tokens'], 'v_norm_mix_g': out['v_norm_mix_g'], 'v_w_in': out['v_w_in'], 'v_b_in': out['v_b_in'], 'v_attn_sinks': out['v_attn_sinks'], 'v_rwkv_mix': out['v_rwkv_mix'], 'v_rwkv_w0': out['v_rwkv_w0'], 'v_rwkv_w2': out['v_rwkv_w2'], 'v_rwkv_a0': out['v_rwkv_a0'], 'v_rwkv_a2': out['v_rwkv_a2'], 'v_rwkv_g2': out['v_rwkv_g2'], 'v_rwkv_k_k': out['v_rwkv_k_k'], 'v_rwkv_k_a': out['v_rwkv_k_a'], 'v_rwkv_r_k': out['v_rwkv_r_k'], 'v_rwkv_ln_w': out['v_rwkv_ln_w'], 'v_rwkv_ln_b': out['v_rwkv_ln_b'], 'v_w_br_attn': out['v_w_br_attn'], 'v_w_br_rwkv': out['v_w_br_rwkv'], 'v_w_o': out['v_w_o'], 'v_norm_ffn_g': out['v_norm_ffn_g'], 'v_w_ffn_gate': out['v_w_ffn_gate'], 'v_w_ffn_up': out['v_w_ffn_up'], 'v_w_ffn_down': out['v_w_ffn_down'], 'v_norm_final_g': out['v_norm_final_g']}


def _loss(weights, diff, rest, loss_target):
    with _jax.named_scope("forward"):
        args = {**rest, TWIN_DIFF_INPUT: diff, **{k: w.astype(_WEIGHT_DTYPES[k]) for k, w in weights.items()}}
        y = _forward(args)
    with _jax.named_scope("loss_head"):
        err = _jnp.square(y.astype(_jnp.float32) - loss_target)
        return 0.5 * _jnp.sum(_jnp.mean(err, axis=-1)) if err.ndim else 0.5 * err


def _adamw(w, g, m, v):
    m = ADAM_B1 * m + (1.0 - ADAM_B1) * g
    v = ADAM_B2 * v + (1.0 - ADAM_B2) * _jnp.square(g)
    m_hat = m / (1.0 - ADAM_B1 ** ADAM_STEP)
    v_hat = v / (1.0 - ADAM_B2 ** ADAM_STEP)
    delta = -ADAM_LR * (m_hat / (_jnp.sqrt(v_hat) + ADAM_EPS) + ADAM_WD * w)
    return delta, m, v


def reference(x, meta_tokens, norm_mix_g, w_in, b_in, attn_sinks, rwkv_mix, rwkv_w0, rwkv_w2, rwkv_a0, rwkv_a2, rwkv_g2, rwkv_k_k, rwkv_k_a, rwkv_r_k, rwkv_ln_w, rwkv_ln_b, w_br_attn, w_br_rwkv, w_o, norm_ffn_g, w_ffn_gate, w_ffn_up, w_ffn_down, norm_final_g, loss_target, m_meta_tokens, m_norm_mix_g, m_w_in, m_b_in, m_attn_sinks, m_rwkv_mix, m_rwkv_w0, m_rwkv_w2, m_rwkv_a0, m_rwkv_a2, m_rwkv_g2, m_rwkv_k_k, m_rwkv_k_a, m_rwkv_r_k, m_rwkv_ln_w, m_rwkv_ln_b, m_w_br_attn, m_w_br_rwkv, m_w_o, m_norm_ffn_g, m_w_ffn_gate, m_w_ffn_up, m_w_ffn_down, m_norm_final_g, v_meta_tokens, v_norm_mix_g, v_w_in, v_b_in, v_attn_sinks, v_rwkv_mix, v_rwkv_w0, v_rwkv_w2, v_rwkv_a0, v_rwkv_a2, v_rwkv_g2, v_rwkv_k_k, v_rwkv_k_a, v_rwkv_r_k, v_rwkv_ln_w, v_rwkv_ln_b, v_w_br_attn, v_w_br_rwkv, v_w_o, v_norm_ffn_g, v_w_ffn_gate, v_w_ffn_up, v_w_ffn_down, v_norm_final_g):
    given = dict(x=x, meta_tokens=meta_tokens, norm_mix_g=norm_mix_g, w_in=w_in, b_in=b_in, attn_sinks=attn_sinks, rwkv_mix=rwkv_mix, rwkv_w0=rwkv_w0, rwkv_w2=rwkv_w2, rwkv_a0=rwkv_a0, rwkv_a2=rwkv_a2, rwkv_g2=rwkv_g2, rwkv_k_k=rwkv_k_k, rwkv_k_a=rwkv_k_a, rwkv_r_k=rwkv_r_k, rwkv_ln_w=rwkv_ln_w, rwkv_ln_b=rwkv_ln_b, w_br_attn=w_br_attn, w_br_rwkv=w_br_rwkv, w_o=w_o, norm_ffn_g=norm_ffn_g, w_ffn_gate=w_ffn_gate, w_ffn_up=w_ffn_up, w_ffn_down=w_ffn_down, norm_final_g=norm_final_g, loss_target=loss_target, m_meta_tokens=m_meta_tokens, m_norm_mix_g=m_norm_mix_g, m_w_in=m_w_in, m_b_in=m_b_in, m_attn_sinks=m_attn_sinks, m_rwkv_mix=m_rwkv_mix, m_rwkv_w0=m_rwkv_w0, m_rwkv_w2=m_rwkv_w2, m_rwkv_a0=m_rwkv_a0, m_rwkv_a2=m_rwkv_a2, m_rwkv_g2=m_rwkv_g2, m_rwkv_k_k=m_rwkv_k_k, m_rwkv_k_a=m_rwkv_k_a, m_rwkv_r_k=m_rwkv_r_k, m_rwkv_ln_w=m_rwkv_ln_w, m_rwkv_ln_b=m_rwkv_ln_b, m_w_br_attn=m_w_br_attn, m_w_br_rwkv=m_w_br_rwkv, m_w_o=m_w_o, m_norm_ffn_g=m_norm_ffn_g, m_w_ffn_gate=m_w_ffn_gate, m_w_ffn_up=m_w_ffn_up, m_w_ffn_down=m_w_ffn_down, m_norm_final_g=m_norm_final_g, v_meta_tokens=v_meta_tokens, v_norm_mix_g=v_norm_mix_g, v_w_in=v_w_in, v_b_in=v_b_in, v_attn_sinks=v_attn_sinks, v_rwkv_mix=v_rwkv_mix, v_rwkv_w0=v_rwkv_w0, v_rwkv_w2=v_rwkv_w2, v_rwkv_a0=v_rwkv_a0, v_rwkv_a2=v_rwkv_a2, v_rwkv_g2=v_rwkv_g2, v_rwkv_k_k=v_rwkv_k_k, v_rwkv_k_a=v_rwkv_k_a, v_rwkv_r_k=v_rwkv_r_k, v_rwkv_ln_w=v_rwkv_ln_w, v_rwkv_ln_b=v_rwkv_ln_b, v_w_br_attn=v_w_br_attn, v_w_br_rwkv=v_w_br_rwkv, v_w_o=v_w_o, v_norm_ffn_g=v_norm_ffn_g, v_w_ffn_gate=v_w_ffn_gate, v_w_ffn_up=v_w_ffn_up, v_w_ffn_down=v_w_ffn_down, v_norm_final_g=v_norm_final_g)
    weights = {n: given[n] for n in TWIN_WEIGHTS}
    shared = {n: given[n] for n in SHARED_INPUTS}
    per_example = {n: given[n] for n in ['x']}
    grad_fn = _jax.value_and_grad(_loss, argnums=(0, 1))

    def one_microbatch(ex, loss_target):
        ex = dict(ex)
        diff = ex.pop(TWIN_DIFF_INPUT)
        return grad_fn(weights, diff, {**shared, **ex}, loss_target)

    if N_MICROBATCH == 1:
        loss, (grad_w, grad_x) = one_microbatch(per_example, given["loss_target"])
    else:
        def body(carry, xs):
            loss_sum, grad_sum = carry
            l_k, (gw_k, gx_k) = one_microbatch(xs[0], xs[1])
            with _jax.named_scope("update"):
                return (loss_sum + l_k, _jax.tree.map(_jnp.add, grad_sum, gw_k)), gx_k

        init = (_jnp.zeros((), _jnp.float32), _jax.tree.map(_jnp.zeros_like, weights))
        (loss, grad_w), grad_x = _jax.lax.scan(body, init, (per_example, given["loss_target"]))
    with _jax.named_scope("update"):
        delta_w, new_m, new_v = {}, {}, {}
        for n in TWIN_WEIGHTS:
            delta_w[n], new_m[n], new_v[n] = _adamw(weights[n], grad_w[n], given["m_" + n], given["v_" + n])
    return (loss, grad_x, *[grad_w[n] for n in TWIN_WEIGHTS], *[delta_w[n] for n in TWIN_WEIGHTS],
            *[new_m[n] for n in TWIN_WEIGHTS], *[new_v[n] for n in TWIN_WEIGHTS])
```

```python
import functools

import jax
import jax.numpy as jnp
from jax import lax
from jax.experimental import pallas as pl
from jax.experimental.pallas import tpu as pltpu

F32 = jnp.float32
BF16 = jnp.bfloat16
HI = lax.Precision.HIGHEST

N_DEV = 8
D_MODEL = 1024
N_META = 16
BLOCK = 128
PAD_ROWS = BLOCK - N_META
HEAD_DIM = 64
Q_HEADS = 8
KV_HEADS = 2
GROUP = Q_HEADS // KV_HEADS
ROPE_DIM = HEAD_DIM // 4
ROPE_HALF = ROPE_DIM // 2
ROPE_THETA = 500000.0
RW_HEADS = 8
RW_DIM = 512
DECAY_LORA = 64
AAA_LORA = 64
GATE_LORA = 160
D_FF = 2816
D_IN = 4640
RMS_EPS = 1e-6
RWKV_LN_EPS = 64e-5
NEG_INF = -1e30
SCAN_T = 16
LANES = 128
PACK_ROWS = 512

ADAM_LR = 0.001
ADAM_B1 = 0.9
ADAM_B2 = 0.999
ADAM_EPS = 1e-08
ADAM_WD = 0.01
ADAM_STEP = 10

C_R, C_K, C_V, C_Q = 0, 512, 1024, 1536
C_G1, C_G2 = 2048, 3072
C_DG, C_KA, C_VA, C_DW, C_DA = 4096, 4352, 4480, 4608, 4736
NP = 4864

VMEM_LIMIT = 48 * 1024 * 1024


def _cparams(sem):
    return pltpu.CompilerParams(dimension_semantics=sem, vmem_limit_bytes=VMEM_LIMIT)


def _pick(n, cands):
    for c in cands:
        if n % c == 0:
            return c
    raise ValueError(f"no tile for {n}")


def _mm(a, b, *, ta=False, tb=False, bias=None, residual=None, name):
    m = a.shape[1] if ta else a.shape[0]
    k = a.shape[0] if ta else a.shape[1]
    n = b.shape[0] if tb else b.shape[1]
    assert k == (b.shape[1] if tb else b.shape[0]), (a.shape, b.shape, ta, tb)
    tm = _pick(m, (512, 384, 256, 128))
    tn = _pick(n, (512, 256, 128))
    tk = _pick(k, (512, 384, 256, 128))
    nk = k // tk
    has_bias = bias is not None
    has_res = residual is not None
    dn = (((0 if ta else 1,), (1 if tb else 0,)), ((), ()))

    def body(*refs):
        a_ref, b_ref = refs[0], refs[1]
        pos = 2
        bias_ref = res_ref = None
        if has_bias:
            bias_ref = refs[pos]
            pos += 1
        if has_res:
            res_ref = refs[pos]
            pos += 1
        o_ref, acc_ref = refs[pos], refs[pos + 1]
        kk = pl.program_id(2)

        @pl.when(kk == 0)
        def _():
            acc_ref[...] = jnp.zeros_like(acc_ref)

        acc_ref[...] += lax.dot_general(a_ref[...].astype(BF16), b_ref[...].astype(BF16), dn,
                                        preferred_element_type=F32)

        @pl.when(kk == nk - 1)
        def _():
            out = acc_ref[...]
            if has_bias:
                out = out + bias_ref[...]
            if has_res:
                out = out + res_ref[...]
            o_ref[...] = out

    in_specs = [
        pl.BlockSpec((tk, tm), lambda i, j, kk: (kk, i)) if ta else pl.BlockSpec((tm, tk), lambda i, j, kk: (i, kk)),
        pl.BlockSpec((tn, tk), lambda i, j, kk: (j, kk)) if tb else pl.BlockSpec((tk, tn), lambda i, j, kk: (kk, j)),
    ]
    args = [a, b]
    if has_bias:
        in_specs.append(pl.BlockSpec((1, tn), lambda i, j, kk: (0, j)))
        args.append(bias)
    if has_res:
        in_specs.append(pl.BlockSpec((tm, tn), lambda i, j, kk: (i, j)))
        args.append(residual)
    return pl.pallas_call(
        body, name=name, grid=(m // tm, n // tn, nk),
        in_specs=in_specs, out_specs=pl.BlockSpec((tm, tn), lambda i, j, kk: (i, j)),
        out_shape=jax.ShapeDtypeStruct((m, n), F32),
        scratch_shapes=[pltpu.VMEM((tm, tn), F32)],
        compiler_params=_cparams(("parallel", "parallel", "arbitrary")),
    )(*args)


def _colsum(x, name):
    m, n = x.shape
    tm = BLOCK

    def body(x_ref, o_ref):
        i = pl.program_id(0)
        s = jnp.sum(x_ref[...], axis=0, keepdims=True)

        @pl.when(i == 0)
        def _():
            o_ref[...] = s

        @pl.when(i > 0)
        def _():
            o_ref[...] += s

    return pl.pallas_call(
        body, name=name, grid=(m // tm,),
        in_specs=[pl.BlockSpec((tm, n), lambda i: (i, 0))],
        out_specs=pl.BlockSpec((1, n), lambda i: (0, 0)),
        out_shape=jax.ShapeDtypeStruct((1, n), F32),
        compiler_params=_cparams(("arbitrary",)),
    )(x)


def _view(arr, width=None, col=0, off=0):
    return (arr, arr.shape[1] if width is None else width, col, off)


def _row_spec(view):
    _, width, col, off = view
    return pl.BlockSpec((BLOCK, width), lambda i, col=col, off=off: (i + off, col))


def _const_spec(arr):
    return pl.BlockSpec(arr.shape, lambda i: (0,) * arr.ndim)


def _rows_fwd(fn, rows, consts, out_widths, *, nblk, name):
    nr, nc = len(rows), len(consts)

    def body(*refs):
        i = pl.program_id(0)
        vals = [r[...] for r in refs[:nr + nc]]
        outs = fn(i, *vals)
        for o_ref, o in zip(refs[nr + nc:], outs):
            o_ref[...] = o

    return pl.pallas_call(
        body, name=name, grid=(nblk,),
        in_specs=[_row_spec(v) for v in rows] + [_const_spec(c) for c in consts],
        out_specs=[pl.BlockSpec((BLOCK, w), lambda i: (i, 0)) for w in out_widths],
        out_shape=[jax.ShapeDtypeStruct((nblk * BLOCK, w), F32) for w in out_widths],
        compiler_params=_cparams(("parallel",)),
    )(*[v[0] for v in rows], *consts)


def _rows_bwd(fn, rows, consts, cts, *, nblk, name, diff_rows, diff_consts, acc=None, fwd_widths=()):
    nr, nc = len(rows), len(consts)
    ct_views = [c for c in cts if c is not None]
    acc = acc or [None] * len(diff_rows)
    acc_views = [a for a in acc if a is not None]
    n_in = nr + nc + len(ct_views) + len(acc_views)
    n_fwd = len(fwd_widths)

    def body(*refs):
        i = pl.program_id(0)
        row_vals = [r[...] for r in refs[:nr]]
        const_vals = [r[...] for r in refs[nr:nr + nc]]
        ct_vals = [r[...] for r in refs[nr + nc:nr + nc + len(ct_views)]]
        acc_vals = [r[...] for r in refs[nr + nc + len(ct_views):n_in]]
        out_refs = refs[n_in:]

        def f(*dargs):
            rv = list(row_vals)
            cv = list(const_vals)
            for pos, idx in enumerate(diff_rows):
                rv[idx] = dargs[pos]
            for pos, idx in enumerate(diff_consts):
                cv[idx] = dargs[len(diff_rows) + pos]
            return tuple(fn(i, *rv, *cv))

        primals = [row_vals[idx] for idx in diff_rows] + [const_vals[idx] for idx in diff_consts]
        outs, pull = jax.vjp(f, *primals)
        full_ct, ci = [], 0
        for o, c in zip(outs, cts):
            if c is None:
                full_ct.append(jnp.zeros_like(o))
            else:
                full_ct.append(ct_vals[ci])
                ci += 1
        grads = pull(tuple(full_ct))
        for o_ref, o in zip(out_refs[:n_fwd], outs):
            o_ref[...] = o
        ai = 0
        for pos in range(len(diff_rows)):
            g = grads[pos]
            if acc[pos] is not None:
                g = g + acc_vals[ai]
                ai += 1
            out_refs[n_fwd + pos][...] = g
        for pos in range(len(diff_consts)):
            g = grads[len(diff_rows) + pos]
            o_ref = out_refs[n_fwd + len(diff_rows) + pos]

            @pl.when(i == 0)
            def _(o_ref=o_ref, g=g):
                o_ref[...] = g

            @pl.when(i > 0)
            def _(o_ref=o_ref, g=g):
                o_ref[...] += g

    out_specs = [pl.BlockSpec((BLOCK, w), lambda i: (i, 0)) for w in fwd_widths]
    out_shape = [jax.ShapeDtypeStruct((nblk * BLOCK, w), F32) for w in fwd_widths]
    for idx in diff_rows:
        out_specs.append(pl.BlockSpec((BLOCK, rows[idx][1]), lambda i: (i, 0)))
        out_shape.append(jax.ShapeDtypeStruct((nblk * BLOCK, rows[idx][1]), F32))
    for idx in diff_consts:
        out_specs.append(_const_spec(consts[idx]))
        out_shape.append(jax.ShapeDtypeStruct(consts[idx].shape, F32))
    return pl.pallas_call(
        body, name=name, grid=(nblk,),
        in_specs=([_row_spec(v) for v in rows] + [_const_spec(c) for c in consts]
                  + [_row_spec(v) for v in ct_views] + [_row_spec(v) for v in acc_views]),
        out_specs=out_specs, out_shape=out_shape,
        compiler_params=_cparams(("arbitrary",)),
    )(*[v[0] for v in rows], *consts, *[v[0] for v in ct_views], *[v[0] for v in acc_views])


def _rms_fn(i, x, g):
    return (x * lax.rsqrt(jnp.mean(x * x, axis=-1, keepdims=True) + RMS_EPS) * g,)


def _sigmoid(x):
    return 1.0 / (1.0 + jnp.exp(-x))


def _softplus(x):
    return jnp.maximum(x, 0.0) + jnp.log(1.0 + jnp.exp(-jnp.abs(x)))


def _head_ones(n=RW_DIM, head=HEAD_DIM):
    r = lax.broadcasted_iota(jnp.int32, (n, n), 0) // head
    c = lax.broadcasted_iota(jnp.int32, (n, n), 1) // head
    return (r == c).astype(F32)


def _head_sum(x, ones):
    return jnp.dot(x, ones, precision=HI, preferred_element_type=F32)


def _rwkv_pre_fn(i, r, k, v, dw, da, dg, r_p, k_p, v_p, dw_p, da_p, dg_p,
                 mix_r, mix_k, mix_v, mix_dw, mix_da, mix_dg, w0, w2, a0, a2, g2, k_k, k_a):
    row = i * BLOCK + lax.broadcasted_iota(jnp.int32, (BLOCK, 1), 0)
    live = row >= PAD_ROWS
    live_prev = row >= PAD_ROWS + 1

    def shift(cur, prev, mix):
        cur = jnp.where(live, cur, 0.0)
        prev = jnp.where(live_prev, prev, 0.0)
        return cur + (prev - cur) * mix

    r = shift(r, r_p, mix_r)
    k = shift(k, k_p, mix_k)
    v = shift(v, v_p, mix_v)
    dw = shift(dw, dw_p, mix_dw)
    da = shift(da, da_p, mix_da)
    dg = shift(dg, dg_p, mix_dg)
    ones = _head_ones()
    wlog = -_softplus(-(w0 + jnp.dot(jnp.tanh(dw), w2, preferred_element_type=F32))) - 0.5
    decay = jnp.exp(-jnp.exp(wlog))
    a = _sigmoid(a0 + jnp.dot(da, a2, preferred_element_type=F32))
    g = jnp.dot(_sigmoid(dg), g2, preferred_element_type=F32)
    kk = k * k_k
    norm_sq = jnp.where(live, _head_sum(kk * kk, ones), 1.0)
    kk = kk / jnp.maximum(jnp.sqrt(norm_sq), 1e-12)
    k_mod = k * (1.0 + (a - 1.0) * k_a)
    return r, decay, k_mod, v, -kk, kk * a, g


def _rwkv_post_fn(i, ys, r, k_mod, v, g, ln_w, ln_b, r_k):
    ones = _head_ones()
    mean = _head_sum(ys, ones) * (1.0 / HEAD_DIM)
    d = ys - mean
    var = _head_sum(d * d, ones) * (1.0 / HEAD_DIM)
    yn = d * lax.rsqrt(var + RWKV_LN_EPS) * ln_w + ln_b
    bonus = _head_sum(r * k_mod * r_k, ones) * v
    return ((yn + bonus) * g,)


def _merge_fn(i, ya, yr, g1, g2):
    return (_sigmoid(g1) * ya + _sigmoid(g2) * yr,)


def _swiglu_fn(i, gate, up):
    return (gate * _sigmoid(gate) * up,)


def _loss_fn(i, h, tgt, g):
    y = h * lax.rsqrt(jnp.mean(h * h, axis=-1, keepdims=True) + RMS_EPS) * g
    err = y - tgt
    return (0.5 * jnp.mean(err * err, axis=-1, keepdims=True),)


def _rope_tables(lp):
    pos = (jnp.arange(lp, dtype=jnp.int32) - PAD_ROWS).astype(F32)
    inv_freq = jnp.power(jnp.float32(ROPE_THETA), -jnp.arange(ROPE_HALF, dtype=F32) * (2.0 / ROPE_DIM))
    ang = pos[:, None] * inv_freq[None, :]
    cos, sin = jnp.cos(ang), jnp.sin(ang)
    one = jnp.ones((lp, HEAD_DIM - ROPE_DIM), F32)
    zero_h = jnp.zeros((lp, ROPE_HALF), F32)
    zero_r = jnp.zeros((lp, HEAD_DIM - ROPE_DIM), F32)
    c = jnp.concatenate([cos, cos, one], axis=1)
    s1 = jnp.concatenate([-sin, zero_h, zero_r], axis=1)
    s2 = jnp.concatenate([zero_h, sin, zero_r], axis=1)
    return tuple(jnp.tile(t, (1, Q_HEADS)) for t in (c, s1, s2))


def _rope_fwd_fn(i, x, c, s1, s2):
    n = x.shape[1]
    return (x * c + pltpu.roll(x, n - ROPE_HALF, 1) * s1 + pltpu.roll(x, ROPE_HALF, 1) * s2,)


def _rope_bwd_fn(i, dy, c, s1, s2):
    n = dy.shape[1]
    return (dy * c + pltpu.roll(dy * s1, ROPE_HALF, 1) + pltpu.roll(dy * s2, n - ROPE_HALF, 1),)


def _attn_mask(i):
    r = lax.broadcasted_iota(jnp.int32, (BLOCK, 3 * BLOCK), 0)
    c = lax.broadcasted_iota(jnp.int32, (BLOCK, 3 * BLOCK), 1)
    meta = (c < BLOCK) & (c >= PAD_ROWS)
    prev = (c >= BLOCK) & (c < 2 * BLOCK) & ((c - BLOCK) > r) & (i >= 1)
    cur = (c >= 2 * BLOCK) & ((c - 2 * BLOCK) <= r)
    return meta | prev | cur


def _attn_fwd(q, k, v, sinks, *, nblk, name):
    scale = HEAD_DIM ** -0.5

    def body(q_ref, km_ref, kp_ref, kc_ref, vm_ref, vp_ref, vc_ref, s_ref, o_ref, lse_ref):
        i = pl.program_id(0)
        valid = _attn_mask(i)
        for h in range(Q_HEADS):
            g = h // GROUP
            kcat = jnp.concatenate([km_ref[g], kp_ref[g], kc_ref[g]], axis=0).astype(BF16)
            vcat = jnp.concatenate([vm_ref[g], vp_ref[g], vc_ref[g]], axis=0).astype(BF16)
            s = lax.dot_general(q_ref[h].astype(BF16), kcat, (((1,), (1,)), ((), ())),
                                preferred_element_type=F32) * scale
            s = jnp.where(valid, s, NEG_INF)
            sink = s_ref[0:1, h:h + 1]
            m = jnp.maximum(jnp.max(s, axis=-1, keepdims=True), sink)
            p = jnp.exp(s - m)
            den = jnp.sum(p, axis=-1, keepdims=True) + jnp.exp(sink - m)
            o = jnp.dot(p.astype(BF16), vcat, preferred_element_type=F32)
            o_ref[h] = o / den
            lse_ref[h] = m + jnp.log(den)

    kv = lambda f: pl.BlockSpec((KV_HEADS, BLOCK, HEAD_DIM), f)
    return pl.pallas_call(
        body, name=name, grid=(nblk,),
        in_specs=[pl.BlockSpec((Q_HEADS, BLOCK, HEAD_DIM), lambda i: (0, i + 1, 0)),
                  kv(lambda i: (0, 0, 0)), kv(lambda i: (0, i, 0)), kv(lambda i: (0, i + 1, 0)),
                  kv(lambda i: (0, 0, 0)), kv(lambda i: (0, i, 0)), kv(lambda i: (0, i + 1, 0)),
                  pl.BlockSpec((1, Q_HEADS), lambda i: (0, 0))],
        out_specs=[pl.BlockSpec((Q_HEADS, BLOCK, HEAD_DIM), lambda i: (0, i, 0)),
                   pl.BlockSpec((Q_HEADS, BLOCK, 1), lambda i: (0, i, 0))],
        out_shape=[jax.ShapeDtypeStruct((Q_HEADS, nblk * BLOCK, HEAD_DIM), F32),
                   jax.ShapeDtypeStruct((Q_HEADS, nblk * BLOCK, 1), F32)],
        compiler_params=_cparams(("parallel",)),
    )(q, k, k, k, v, v, v, sinks)


def _attn_bwd(q, k, v, sinks, o, lse, do, *, nblk, name):
    scale = HEAD_DIM ** -0.5
    lp = k.shape[1]

    def body(q_ref, km_ref, kp_ref, kc_ref, vm_ref, vp_ref, vc_ref, s_ref, o_ref, lse_ref, do_ref,
             dq_ref, dk_ref, dv_ref, ds_ref):
        i = pl.program_id(0)

        @pl.when(i == 0)
        def _():
            dk_ref[...] = jnp.zeros_like(dk_ref)
            dv_ref[...] = jnp.zeros_like(dv_ref)
            ds_ref[...] = jnp.zeros_like(ds_ref)

        valid = _attn_mask(i)
        lane = lax.broadcasted_iota(jnp.int32, (1, Q_HEADS), 1)
        prev_rows = pl.ds(pl.multiple_of(i * BLOCK, BLOCK), BLOCK)
        cur_rows = pl.ds(pl.multiple_of((i + 1) * BLOCK, BLOCK), BLOCK)
        for h in range(Q_HEADS):
            g = h // GROUP
            qh = q_ref[h].astype(BF16)
            doh = do_ref[h]
            kcat = jnp.concatenate([km_ref[g], kp_ref[g], kc_ref[g]], axis=0).astype(BF16)
            vcat = jnp.concatenate([vm_ref[g], vp_ref[g], vc_ref[g]], axis=0).astype(BF16)
            s = lax.dot_general(qh, kcat, (((1,), (1,)), ((), ())), preferred_element_type=F32) * scale
            s = jnp.where(valid, s, NEG_INF)
            lse_h = lse_ref[h]
            p = jnp.exp(s - lse_h)
            delta = jnp.sum(doh * o_ref[h], axis=-1, keepdims=True)
            dp = lax.dot_general(doh.astype(BF16), vcat, (((1,), (1,)), ((), ())), preferred_element_type=F32)
            dsc = (p * (dp - delta) * scale).astype(BF16)
            dq_ref[h] = jnp.dot(dsc, kcat, preferred_element_type=F32)
            dk_all = lax.dot_general(dsc, qh, (((0,), (0,)), ((), ())), preferred_element_type=F32)
            dv_all = lax.dot_general(p.astype(BF16), doh.astype(BF16), (((0,), (0,)), ((), ())),
                                     preferred_element_type=F32)
            dk_ref[g, 0:BLOCK, :] += dk_all[0:BLOCK]
            dk_ref[g, prev_rows, :] += dk_all[BLOCK:2 * BLOCK]
            dk_ref[g, cur_rows, :] += dk_all[2 * BLOCK:]
            dv_ref[g, 0:BLOCK, :] += dv_all[0:BLOCK]
            dv_ref[g, prev_rows, :] += dv_all[BLOCK:2 * BLOCK]
            dv_ref[g, cur_rows, :] += dv_all[2 * BLOCK:]
            p_sink = jnp.exp(s_ref[0:1, h:h + 1] - lse_h)
            dsink = -jnp.sum(p_sink * delta, axis=0, keepdims=True)
            ds_ref[...] += jnp.where(lane == h, dsink, 0.0)

    kv = lambda f: pl.BlockSpec((KV_HEADS, BLOCK, HEAD_DIM), f)
    qspec = pl.BlockSpec((Q_HEADS, BLOCK, HEAD_DIM), lambda i: (0, i, 0))
    whole = pl.BlockSpec((KV_HEADS, lp, HEAD_DIM), lambda i: (0, 0, 0))
    return pl.pallas_call(
        body, name=name, grid=(nblk,),
        in_specs=[pl.BlockSpec((Q_HEADS, BLOCK, HEAD_DIM), lambda i: (0, i + 1, 0)),
                  kv(lambda i: (0, 0, 0)), kv(lambda i: (0, i, 0)), kv(lambda i: (0, i + 1, 0)),
                  kv(lambda i: (0, 0, 0)), kv(lambda i: (0, i, 0)), kv(lambda i: (0, i + 1, 0)),
                  pl.BlockSpec((1, Q_HEADS), lambda i: (0, 0)),
                  qspec, pl.BlockSpec((Q_HEADS, BLOCK, 1), lambda i: (0, i, 0)), qspec],
        out_specs=[qspec, whole, whole, pl.BlockSpec((1, Q_HEADS), lambda i: (0, 0))],
        out_shape=[jax.ShapeDtypeStruct((Q_HEADS, nblk * BLOCK, HEAD_DIM), F32),
                   jax.ShapeDtypeStruct((KV_HEADS, lp, HEAD_DIM), F32),
                   jax.ShapeDtypeStruct((KV_HEADS, lp, HEAD_DIM), F32),
                   jax.ShapeDtypeStruct((1, Q_HEADS), F32)],
        compiler_params=_cparams(("arbitrary",)),
    )(q, k, k, k, v, v, v, sinks, o, lse, do)


def _wkv_fwd(r, w, k, a, b, v, name):
    nch = r.shape[0]
    t_steps = SCAN_T

    def body(r_ref, w_ref, k_ref, a_ref, b_ref, v_ref, y_ref, hist_ref, st_ref):
        @pl.when(pl.program_id(0) == 0)
        def _():
            st_ref[...] = jnp.zeros_like(st_ref)

        st = st_ref[...]
        for j in range(t_steps):
            hist_ref[0, j] = st
            col = lambda ref: ref[0, :, :, j:j + 1]
            v_row = v_ref[0, :, j:j + 1, :]
            sa = jnp.sum(st * col(a_ref), axis=1, keepdims=True)
            st = st * col(w_ref) + col(b_ref) * sa + col(k_ref) * v_row
            y_ref[0, :, j:j + 1, :] = jnp.sum(st * col(r_ref), axis=1, keepdims=True)
        st_ref[...] = st

    colspec = pl.BlockSpec((1, RW_HEADS, HEAD_DIM, t_steps), lambda c: (c, 0, 0, 0))
    rowspec = pl.BlockSpec((1, RW_HEADS, t_steps, HEAD_DIM), lambda c: (c, 0, 0, 0))
    return pl.pallas_call(
        body, name=name, grid=(nch,),
        in_specs=[colspec] * 5 + [rowspec],
        out_specs=[rowspec, pl.BlockSpec((1, t_steps, RW_HEADS, HEAD_DIM, HEAD_DIM), lambda c: (c, 0, 0, 0, 0))],
        out_shape=[jax.ShapeDtypeStruct((nch, RW_HEADS, t_steps, HEAD_DIM), F32),
                   jax.ShapeDtypeStruct((nch, t_steps, RW_HEADS, HEAD_DIM, HEAD_DIM), F32)],
        scratch_shapes=[pltpu.VMEM((RW_HEADS, HEAD_DIM, HEAD_DIM), F32)],
        compiler_params=_cparams(("arbitrary",)),
    )(r, w, k, a, b, v)


def _wkv_bwd(r, w, k, a, b, v, hist, dy, name):
    nch = r.shape[0]
    t_steps = SCAN_T

    def body(r_ref, w_ref, k_ref, a_ref, b_ref, v_ref, hist_ref, dy_ref,
             dr_ref, dw_ref, dk_ref, da_ref, db_ref, dv_ref, g_ref):
        @pl.when(pl.program_id(0) == 0)
        def _():
            g_ref[...] = jnp.zeros_like(g_ref)

        gst = g_ref[...]
        for j in reversed(range(t_steps)):
            col = lambda ref: ref[0, :, :, j:j + 1]
            prev = hist_ref[0, j]
            v_row = v_ref[0, :, j:j + 1, :]
            dy_row = dy_ref[0, :, j:j + 1, :]
            a_c, b_c, w_c, k_c = col(a_ref), col(b_ref), col(w_ref), col(k_ref)
            sa = jnp.sum(prev * a_c, axis=1, keepdims=True)
            nxt = prev * w_c + b_c * sa + k_c * v_row
            gst = gst + col(r_ref) * dy_row
            dr_ref[0, :, :, j:j + 1] = jnp.sum(nxt * dy_row, axis=2, keepdims=True)
            dv_ref[0, :, j:j + 1, :] = jnp.sum(gst * k_c, axis=1, keepdims=True)
            dk_ref[0, :, :, j:j + 1] = jnp.sum(gst * v_row, axis=2, keepdims=True)
            dsa = jnp.sum(gst * b_c, axis=1, keepdims=True)
            db_ref[0, :, :, j:j + 1] = jnp.sum(gst * sa, axis=2, keepdims=True)
            dw_ref[0, :, :, j:j + 1] = jnp.sum(gst * prev, axis=2, keepdims=True)
            da_ref[0, :, :, j:j + 1] = jnp.sum(prev * dsa, axis=2, keepdims=True)
            gst = gst * w_c + a_c * dsa
        g_ref[...] = gst

    rev = lambda c: (nch - 1 - c, 0, 0, 0)
    colspec = pl.BlockSpec((1, RW_HEADS, HEAD_DIM, t_steps), rev)
    rowspec = pl.BlockSpec((1, RW_HEADS, t_steps, HEAD_DIM), rev)
    col_shape = jax.ShapeDtypeStruct((nch, RW_HEADS, HEAD_DIM, t_steps), F32)
    return pl.pallas_call(
        body, name=name, grid=(nch,),
        in_specs=[colspec] * 5 + [rowspec,
                                  pl.BlockSpec((1, t_steps, RW_HEADS, HEAD_DIM, HEAD_DIM),
                                               lambda c: (nch - 1 - c, 0, 0, 0, 0)),
                                  rowspec],
        out_specs=[colspec] * 5 + [rowspec],
        out_shape=[col_shape] * 5 + [jax.ShapeDtypeStruct((nch, RW_HEADS, t_steps, HEAD_DIM), F32)],
        scratch_shapes=[pltpu.VMEM((RW_HEADS, HEAD_DIM, HEAD_DIM), F32)],
        compiler_params=_cparams(("arbitrary",)),
    )(r, w, k, a, b, v, hist, dy)


def _to_cols(x, nch):
    return x.reshape(nch, SCAN_T, RW_HEADS, HEAD_DIM).transpose(0, 2, 3, 1)


def _from_cols(x):
    return x.transpose(0, 3, 1, 2).reshape(-1, RW_DIM)


def _to_rows(x, nch):
    return x.reshape(nch, SCAN_T, RW_HEADS, HEAD_DIM).transpose(0, 2, 1, 3)


def _from_rows(x):
    return x.transpose(0, 2, 1, 3).reshape(-1, RW_DIM)


MESH = pl.DeviceIdType.MESH
ANY = pl.BlockSpec(memory_space=pl.ANY)


def _all_gather(x, name):
    rows, n = x.shape

    def body(x_ref, out_ref, send_sems, recv_sems, local_sem):
        xi, yi, ci = lax.axis_index("x"), lax.axis_index("y"), lax.axis_index("c")
        me, sibling = (xi, yi, ci), (xi, yi, 1 - ci)
        chips = [(1 - xi, yi), (xi, 1 - yi), (1 - xi, 1 - yi)]

        def slot(px, py, pc):
            return out_ref.at[4 * px + 2 * py + pc]

        def copy(sem, block, to, src=None):
            return pltpu.make_async_remote_copy(
                src_ref=slot(*block) if src is None else src, dst_ref=slot(*block),
                send_sem=send_sems.at[sem], recv_sem=recv_sems.at[sem], device_id=to, device_id_type=MESH)

        mine = pltpu.make_async_copy(x_ref, slot(*me), local_sem)
        mine.start()
        first = [copy(0, me, sibling, src=x_ref)]
        first += [copy(1 + j, me, (*chip, ci), src=x_ref) for j, chip in enumerate(chips)]
        for cp in first:
            cp.start()
        passed = [copy(4 + j, (*chip, ci), sibling) for j, chip in enumerate(chips)]
        for j, chip in enumerate(chips):
            copy(1 + j, (*chip, ci), me).wait_recv()
            passed[j].start()
        copy(0, sibling, me).wait_recv()
        for j, chip in enumerate(chips):
            copy(4 + j, (*chip, 1 - ci), me).wait_recv()
        for cp in first + passed:
            cp.wait_send()
        mine.wait()

    return pl.pallas_call(
        body, name=name, out_shape=jax.ShapeDtypeStruct((N_DEV, rows, n), x.dtype),
        in_specs=[ANY], out_specs=ANY,
        scratch_shapes=[pltpu.SemaphoreType.DMA((7,)), pltpu.SemaphoreType.DMA((7,)), pltpu.SemaphoreType.DMA(())],
    )(x)


def _exchange(g_big, g_small, name):
    def body(gb_ref, gs_ref, rb_ref, rs_ref, send_b, recv_b, send_s, recv_s, local_sems):
        xi, yi, ci = lax.axis_index("x"), lax.axis_index("y"), lax.axis_index("c")
        me = 4 * xi + 2 * yi + ci
        copies = []
        for d in range(1, N_DEV):
            px = 1 - xi if d & 4 else xi
            py = 1 - yi if d & 2 else yi
            pc = 1 - ci if d & 1 else ci
            peer = 4 * px + 2 * py + pc
            copies.append(pltpu.make_async_remote_copy(
                src_ref=gb_ref.at[peer], dst_ref=rb_ref.at[me], send_sem=send_b.at[d - 1], recv_sem=recv_b.at[d - 1],
                device_id=(px, py, pc), device_id_type=MESH))
            copies.append(pltpu.make_async_remote_copy(
                src_ref=gs_ref, dst_ref=rs_ref.at[me], send_sem=send_s.at[d - 1], recv_sem=recv_s.at[d - 1],
                device_id=(px, py, pc), device_id_type=MESH))
        own_b = pltpu.make_async_copy(gb_ref.at[me], rb_ref.at[me], local_sems.at[0])
        own_s = pltpu.make_async_copy(gs_ref, rs_ref.at[me], local_sems.at[1])
        for cp in copies:
            cp.start()
        own_b.start()
        own_s.start()
        for cp in copies:
            cp.wait()
        own_b.wait()
        own_s.wait()

    dma7 = pltpu.SemaphoreType.DMA((7,))
    return pl.pallas_call(
        body, name=name,
        out_shape=[jax.ShapeDtypeStruct(g_big.shape, F32), jax.ShapeDtypeStruct((N_DEV,) + g_small.shape, F32)],
        in_specs=[ANY, ANY], out_specs=[ANY, ANY],
        scratch_shapes=[dma7, dma7, dma7, dma7, pltpu.SemaphoreType.DMA((2,))],
    )(g_big, g_small)


def _adamw(parts, w, m, v, name):
    rows = w.shape[0]
    tile = _pick(rows, (PACK_ROWS, 104, 8))

    def body(p_ref, w_ref, m_ref, v_ref, g_out, d_out, m_out, v_out):
        g = p_ref[0]
        for s in range(1, N_DEV):
            g = g + p_ref[s]
        m_new = ADAM_B1 * m_ref[...] + (1.0 - ADAM_B1) * g
        v_new = ADAM_B2 * v_ref[...] + (1.0 - ADAM_B2) * (g * g)
        m_hat = m_new / (1.0 - ADAM_B1 ** ADAM_STEP)
        v_hat = v_new / (1.0 - ADAM_B2 ** ADAM_STEP)
        g_out[...] = g
        d_out[...] = -ADAM_LR * (m_hat / (jnp.sqrt(v_hat) + ADAM_EPS) + ADAM_WD * w_ref[...])
        m_out[...] = m_new
        v_out[...] = v_new

    spec = pl.BlockSpec((tile, LANES), lambda i: (i, 0))
    return pl.pallas_call(
        body, name=name, grid=(rows // tile,),
        in_specs=[pl.BlockSpec((N_DEV, tile, LANES), lambda i: (0, i, 0)), spec, spec, spec],
        out_specs=[spec] * 4, out_shape=[jax.ShapeDtypeStruct((rows, LANES), F32)] * 4,
        compiler_params=_cparams(("parallel",)),
    )(parts, w, m, v)


SHARDED = [
    ("meta_tokens", 1), ("w_in", 1), ("rwkv_w2", 1), ("rwkv_a2", 1), ("rwkv_g2", 1), ("w_br_attn", 1),
    ("w_br_rwkv", 1), ("w_o", 0), ("w_ffn_gate", 1), ("w_ffn_up", 1), ("w_ffn_down", 0)]
REPLICATED = ["norm_mix_g", "b_in", "attn_sinks", "rwkv_mix", "rwkv_w0", "rwkv_a0", "rwkv_k_k", "rwkv_k_a",
              "rwkv_r_k", "rwkv_ln_w", "rwkv_ln_b", "norm_ffn_g", "norm_final_g"]
WEIGHTS = ["meta_tokens", "norm_mix_g", "w_in", "b_in", "attn_sinks", "rwkv_mix", "rwkv_w0", "rwkv_w2", "rwkv_a0",
           "rwkv_a2", "rwkv_g2", "rwkv_k_k", "rwkv_k_a", "rwkv_r_k", "rwkv_ln_w", "rwkv_ln_b", "w_br_attn",
           "w_br_rwkv", "w_o", "norm_ffn_g", "w_ffn_gate", "w_ffn_up", "w_ffn_down", "norm_final_g"]


def _pack(arrays, row_multiple):
    flat = jnp.concatenate([a.reshape(-1) for a in arrays])
    per = row_multiple * LANES
    total = -(-flat.shape[0] // per) * per
    return jnp.pad(flat, (0, total - flat.shape[0])).reshape(-1, LANES)


def _unpack(buf, shapes):
    flat = buf.reshape(-1)
    out, pos = [], 0
    for s in shapes:
        n = 1
        for d in s:
            n *= d
        out.append(flat[pos:pos + n].reshape(s))
        pos += n
    return out


def _strip(name, a):
    return a if name in ("meta_tokens", "norm_final_g") else a[0]


def _w_in_padded(w, z):
    return jnp.concatenate([
        w[:, 768:2304], w[:, 0:512], w[:, 2592:4640], w[:, 2432:2592], z(256 - GATE_LORA), w[:, 512:768],
        w[:, 2304:2368], z(128 - DECAY_LORA), w[:, 2368:2432], z(128 - AAA_LORA)], axis=1)


def _w_in_unpadded(wp):
    return jnp.concatenate([
        wp[:, C_Q:C_Q + 512], wp[:, C_KA:C_KA + 256], wp[:, C_R:C_R + 1536], wp[:, C_DW:C_DW + DECAY_LORA],
        wp[:, C_DA:C_DA + AAA_LORA], wp[:, C_DG:C_DG + GATE_LORA], wp[:, C_G1:C_G1 + 2048]], axis=1)


def _pad_rows(a, n):
    return jnp.pad(a, ((0, n - a.shape[0]), (0, 0)))


def _device_step(x, tgt, full):
    seq = x.shape[0]
    nblk = seq // BLOCK
    lp = seq + BLOCK
    nall = nblk + 1
    nch = lp // SCAN_T
    zc = lambda rows: (lambda n: jnp.zeros((rows, n), F32))

    w_in_p = _w_in_padded(full["w_in"], zc(D_MODEL))
    b_in_p = _w_in_padded(full["b_in"][None], zc(1))
    mix = full["rwkv_mix"][None]
    mix_r, mix_k, mix_v = mix[:, 0:512], mix[:, 512:1024], mix[:, 1024:1536]
    mix_dw = jnp.pad(mix[:, 1536:1600], ((0, 0), (0, 64)))
    mix_da = jnp.pad(mix[:, 1600:1664], ((0, 0), (0, 64)))
    mix_dg = jnp.pad(mix[:, 1664:1824], ((0, 0), (0, 96)))
    w2_p = _pad_rows(full["rwkv_w2"], 128)
    a2_p = _pad_rows(full["rwkv_a2"], 128)
    g2_p = _pad_rows(full["rwkv_g2"], 256)
    row = lambda name: full[name].reshape(1, -1)
    sinks = row("attn_sinks")
    rope_c, rope_s1, rope_s2 = _rope_tables(lp)

    hpad = jnp.concatenate([jnp.zeros((PAD_ROWS, D_MODEL), F32), full["meta_tokens"], x], axis=0)
    (u,) = _rows_fwd(_rms_fn, [_view(hpad)], [row("norm_mix_g")], [D_MODEL], nblk=nall, name="norm_mix")
    proj = _mm(u, w_in_p, bias=b_in_p, name="in_proj")
    (q_r,) = _rows_fwd(_rope_fwd_fn, [_view(proj, 512, C_Q // 512), _view(rope_c), _view(rope_s1), _view(rope_s2)],
                       [], [512], nblk=nall, name="rope_q")
    (k_r,) = _rows_fwd(_rope_fwd_fn, [_view(proj, 128, C_KA // 128), _view(rope_c, 128), _view(rope_s1, 128),
                                      _view(rope_s2, 128)], [], [128], nblk=nall, name="rope_k")
    heads = lambda a, nh: a.reshape(a.shape[0], nh, HEAD_DIM).transpose(1, 0, 2)
    unheads = lambda a: a.transpose(1, 0, 2).reshape(a.shape[1], -1)
    q_h, k_h = heads(q_r, Q_HEADS), heads(k_r, KV_HEADS)
    v_h = heads(proj[:, C_VA:C_VA + 128], KV_HEADS)
    o_h, lse = _attn_fwd(q_h, k_h, v_h, sinks, nblk=nblk, name="attn_fwd")
    y_attn = unheads(o_h)

    rw_cols = jnp.concatenate([proj[:, C_R:C_R + 1536], proj[:, C_DG:C_DG + 256], proj[:, C_DW:C_DW + 256]], axis=1)
    rw_prev = jnp.pad(rw_cols[:-1], ((1, 0), (0, 0)))
    pre_rows = [_view(proj, 512, 0), _view(proj, 512, 1), _view(proj, 512, 2), _view(proj, 128, C_DW // 128),
                _view(proj, 128, C_DA // 128), _view(proj, 256, C_DG // 256),
                _view(rw_prev, 512, 0), _view(rw_prev, 512, 1), _view(rw_prev, 512, 2), _view(rw_prev, 128, 14),
                _view(rw_prev, 128, 15), _view(rw_prev, 256, 6)]
    pre_consts = [mix_r, mix_k, mix_v, mix_dw, mix_da, mix_dg, row("rwkv_w0"), w2_p, row("rwkv_a0"), a2_p, g2_p,
                  row("rwkv_k_k"), row("rwkv_k_a")]
    r_t, decay, k_mod, v_t, a_neg, b_t, gate = _rows_fwd(_rwkv_pre_fn, pre_rows, pre_consts, [RW_DIM] * 7,
                                                         nblk=nall, name="rwkv_pre")
    cols = [_to_cols(t, nch) for t in (r_t, decay, k_mod, a_neg, b_t)]
    v_rows = _to_rows(v_t, nch)
    y_rows, hist = _wkv_fwd(*cols, v_rows, name="wkv_fwd")
    y_scan = _from_rows(y_rows)
    post_rows = [_view(y_scan, off=1), _view(r_t, off=1), _view(k_mod, off=1), _view(v_t, off=1), _view(gate, off=1)]
    post_consts = [row("rwkv_ln_w"), row("rwkv_ln_b"), row("rwkv_r_k")]
    (y_rwkv,) = _rows_fwd(_rwkv_post_fn, post_rows, post_consts, [RW_DIM], nblk=nblk, name="rwkv_post")

    ya = _mm(y_attn, full["w_br_attn"], name="br_attn")
    yr = _mm(y_rwkv, full["w_br_rwkv"], name="br_rwkv")
    merge_rows = [_view(ya), _view(yr), _view(proj, 1024, C_G1 // 1024, 1), _view(proj, 1024, C_G2 // 1024, 1)]
    (merged,) = _rows_fwd(_merge_fn, merge_rows, [], [D_MODEL], nblk=nblk, name="merge")
    h1 = _mm(merged, full["w_o"], residual=x, name="out_proj")
    (f,) = _rows_fwd(_rms_fn, [_view(h1)], [row("norm_ffn_g")], [D_MODEL], nblk=nblk, name="norm_ffn")
    ff_gate = _mm(f, full["w_ffn_gate"], name="ffn_gate")
    ff_up = _mm(f, full["w_ffn_up"], name="ffn_up")
    (act,) = _rows_fwd(_swiglu_fn, [_view(ff_gate), _view(ff_up)], [], [D_FF], nblk=nblk, name="swiglu")
    h2 = _mm(act, full["w_ffn_down"], residual=h1, name="ffn_down")

    grads = {}
    ones_col = jnp.ones((seq, 1), F32)
    loss_rows, dh2, grads["norm_final_g"] = _rows_bwd(
        _loss_fn, [_view(h2), _view(tgt)], [row("norm_final_g")], [_view(ones_col)], nblk=nblk, name="loss",
        diff_rows=[0], diff_consts=[0], fwd_widths=[1])
    loss = jnp.sum(loss_rows)

    dact = _mm(dh2, full["w_ffn_down"], tb=True, name="d_act")
    grads["w_ffn_down"] = _mm(act, dh2, ta=True, name="dw_ffn_down")
    dgate, dup = _rows_bwd(_swiglu_fn, [_view(ff_gate), _view(ff_up)], [], [_view(dact)], nblk=nblk,
                           name="swiglu_bwd", diff_rows=[0, 1], diff_consts=[])
    grads["w_ffn_gate"] = _mm(f, dgate, ta=True, name="dw_ffn_gate")
    grads["w_ffn_up"] = _mm(f, dup, ta=True, name="dw_ffn_up")
    df = _mm(dgate, full["w_ffn_gate"], tb=True, name="df_gate")
    df = _mm(dup, full["w_ffn_up"], tb=True, residual=df, name="df_up")
    dh1, grads["norm_ffn_g"] = _rows_bwd(_rms_fn, [_view(h1)], [row("norm_ffn_g")], [_view(df)], nblk=nblk,
                                         name="norm_ffn_bwd", diff_rows=[0], diff_consts=[0], acc=[_view(dh2)])
    dmerged = _mm(dh1, full["w_o"], tb=True, name="d_merged")
    grads["w_o"] = _mm(merged, dh1, ta=True, name="dw_o")
    dya, dyr, dg1, dg2 = _rows_bwd(_merge_fn, merge_rows, [], [_view(dmerged)], nblk=nblk, name="merge_bwd",
                                   diff_rows=[0, 1, 2, 3], diff_consts=[])
    grads["w_br_attn"] = _mm(y_attn, dya, ta=True, name="dw_br_attn")
    grads["w_br_rwkv"] = _mm(y_rwkv, dyr, ta=True, name="dw_br_rwkv")
    dy_attn = _mm(dya, full["w_br_attn"], tb=True, name="d_y_attn")
    dy_rwkv = _mm(dyr, full["w_br_rwkv"], tb=True, name="d_y_rwkv")

    post = _rows_bwd(_rwkv_post_fn, post_rows, post_consts, [_view(dy_rwkv)], nblk=nblk, name="rwkv_post_bwd",
                     diff_rows=[0, 1, 2, 3, 4], diff_consts=[0, 1, 2])
    dys, dr_post, dk_post, dv_post, dgate_post = [jnp.pad(t, ((BLOCK, 0), (0, 0))) for t in post[:5]]
    grads["rwkv_ln_w"], grads["rwkv_ln_b"], grads["rwkv_r_k"] = post[5:]
    scan = _wkv_bwd(*cols, v_rows, hist, _to_rows(dys, nch), name="wkv_bwd")
    dr_s, dw_s, dk_s, da_s, db_s = [_from_cols(t) for t in scan[:5]]
    dv_s = _from_rows(scan[5])
    pre_cts = [_view(dr_s + dr_post), _view(dw_s), _view(dk_s + dk_post), _view(dv_s + dv_post), _view(da_s),
               _view(db_s), _view(dgate_post)]
    pre = _rows_bwd(_rwkv_pre_fn, pre_rows, pre_consts, pre_cts, nblk=nall, name="rwkv_pre_bwd",
                    diff_rows=list(range(12)), diff_consts=list(range(13)))
    d_cur, d_prev, d_par = pre[0:6], pre[6:12], pre[12:]
    up = lambda t: jnp.pad(t[1:], ((0, 1), (0, 0)))
    d_rw = [c + up(p) for c, p in zip(d_cur, d_prev)]
    grads["rwkv_mix"] = jnp.concatenate([d_par[0], d_par[1], d_par[2], d_par[3][:, :DECAY_LORA],
                                         d_par[4][:, :AAA_LORA], d_par[5][:, :GATE_LORA]], axis=1)
    grads["rwkv_w0"], grads["rwkv_w2"] = d_par[6], d_par[7][:DECAY_LORA]
    grads["rwkv_a0"], grads["rwkv_a2"] = d_par[8], d_par[9][:AAA_LORA]
    grads["rwkv_g2"] = d_par[10][:GATE_LORA]
    grads["rwkv_k_k"], grads["rwkv_k_a"] = d_par[11], d_par[12]

    do_h = heads(dy_attn, Q_HEADS)
    dq_h, dk_h, dv_h, grads["attn_sinks"] = _attn_bwd(q_h, k_h, v_h, sinks, o_h, lse, do_h, nblk=nblk,
                                                      name="attn_bwd")
    dq_r = jnp.pad(unheads(dq_h), ((BLOCK, 0), (0, 0)))
    (dq,) = _rows_fwd(_rope_bwd_fn, [_view(dq_r), _view(rope_c), _view(rope_s1), _view(rope_s2)], [], [512],
                      nblk=nall, name="rope_q_bwd")
    (dka,) = _rows_fwd(_rope_bwd_fn, [_view(unheads(dk_h)), _view(rope_c, 128), _view(rope_s1, 128),
                                      _view(rope_s2, 128)], [], [128], nblk=nall, name="rope_k_bwd")
    dva = unheads(dv_h)

    lead = lambda t: jnp.pad(t, ((BLOCK, 0), (0, 0)))
    dproj = jnp.concatenate([d_rw[0], d_rw[1], d_rw[2], dq, lead(dg1), lead(dg2), d_rw[5], dka, dva, d_rw[3],
                             d_rw[4]], axis=1)
    grads["w_in"] = _w_in_unpadded(_mm(u, dproj, ta=True, name="dw_in"))
    grads["b_in"] = _w_in_unpadded(_colsum(dproj, name="db_in"))
    du = _mm(dproj, w_in_p, tb=True, name="d_u")
    dh, grads["norm_mix_g"] = _rows_bwd(_rms_fn, [_view(hpad)], [row("norm_mix_g")], [_view(du)], nblk=nall,
                                        name="norm_mix_bwd", diff_rows=[0], diff_consts=[0], acc=[_view(lead(dh1))])
    grads["meta_tokens"] = dh[PAD_ROWS:BLOCK]
    return loss, dh[BLOCK:], grads


def kernel(x, meta_tokens, norm_mix_g, w_in, b_in, attn_sinks, rwkv_mix, rwkv_w0, rwkv_w2, rwkv_a0, rwkv_a2, rwkv_g2, rwkv_k_k, rwkv_k_a, rwkv_r_k, rwkv_ln_w, rwkv_ln_b, w_br_attn, w_br_rwkv, w_o, norm_ffn_g, w_ffn_gate, w_ffn_up, w_ffn_down, norm_final_g, loss_target, m_meta_tokens, m_norm_mix_g, m_w_in, m_b_in, m_attn_sinks, m_rwkv_mix, m_rwkv_w0, m_rwkv_w2, m_rwkv_a0, m_rwkv_a2, m_rwkv_g2, m_rwkv_k_k, m_rwkv_k_a, m_rwkv_r_k, m_rwkv_ln_w, m_rwkv_ln_b, m_w_br_attn, m_w_br_rwkv, m_w_o, m_norm_ffn_g, m_w_ffn_gate, m_w_ffn_up, m_w_ffn_down, m_norm_final_g, v_meta_tokens, v_norm_mix_g, v_w_in, v_b_in, v_attn_sinks, v_rwkv_mix, v_rwkv_w0, v_rwkv_w2, v_rwkv_a0, v_rwkv_a2, v_rwkv_g2, v_rwkv_k_k, v_rwkv_k_a, v_rwkv_r_k, v_rwkv_ln_w, v_rwkv_ln_b, v_w_br_attn, v_w_br_rwkv, v_w_o, v_norm_ffn_g, v_w_ffn_gate, v_w_ffn_up, v_w_ffn_down, v_norm_final_g):
    given = dict(locals())
    wts = {n: _strip(n, given[n]) for n in WEIGHTS}
    mom = {n: _strip(n, given["m_" + n]) for n in WEIGHTS}
    var = {n: _strip(n, given["v_" + n]) for n in WEIGHTS}
    shard_shapes = [wts[n].shape for n, _ in SHARDED]
    small_shapes = [wts[n].shape for n in REPLICATED]

    gathered = _all_gather(_pack([wts[n] for n, _ in SHARDED], PACK_ROWS), name="gather_weights")
    full = {n: wts[n] for n in REPLICATED}
    per_dev = [_unpack(gathered[p], shard_shapes) for p in range(N_DEV)]
    for j, (n, axis) in enumerate(SHARDED):
        full[n] = jnp.concatenate([per_dev[p][j] for p in range(N_DEV)], axis=axis)

    loss_part, grad_x, grads = _device_step(x[0], loss_target[0], full)

    def shard_of(n, axis, p):
        g = grads[n]
        size = g.shape[axis] // N_DEV
        return lax.slice_in_dim(g, p * size, (p + 1) * size, axis=axis)

    g_big = jnp.stack([_pack([shard_of(n, axis, p) for n, axis in SHARDED], PACK_ROWS) for p in range(N_DEV)])
    g_small = _pack([grads[n].reshape(wts[n].shape) for n in REPLICATED], 8)
    parts_big, parts_small = _exchange(g_big, g_small, name="exchange_grads")
    big = _adamw(parts_big, _pack([wts[n] for n, _ in SHARDED], PACK_ROWS),
                 _pack([mom[n] for n, _ in SHARDED], PACK_ROWS), _pack([var[n] for n, _ in SHARDED], PACK_ROWS),
                 name="adamw_sharded")
    small = _adamw(parts_small, _pack([wts[n] for n in REPLICATED], 8), _pack([mom[n] for n in REPLICATED], 8),
                   _pack([var[n] for n in REPLICATED], 8), name="adamw_replicated")

    results = [{}, {}, {}, {}]
    for kind in range(4):
        for (n, _), a in zip(SHARDED, _unpack(big[kind], shard_shapes)):
            results[kind][n] = a
        for n, a in zip(REPLICATED, _unpack(small[kind], small_shapes)):
            results[kind][n] = a
    loss = lax.psum(loss_part, ("x", "y", "c"))
    out = [loss, grad_x[None]]
    for kind in range(4):
        out += [results[kind][n].reshape(given[n].shape) for n in WEIGHTS]
    return tuple(out)
```

```python
import functools

import jax
import jax.numpy as jnp
from jax import lax
from jax.experimental import pallas as pl
from jax.experimental.pallas import tpu as pltpu

F32 = jnp.float32
BF16 = jnp.bfloat16
HI = lax.Precision.HIGHEST

N_DEV = 8
D_MODEL = 1024
N_META = 16
BLOCK = 128
PAD_ROWS = BLOCK - N_META
HEAD_DIM = 64
Q_HEADS = 8
KV_HEADS = 2
GROUP = Q_HEADS // KV_HEADS
ROPE_DIM = HEAD_DIM // 4
ROPE_HALF = ROPE_DIM // 2
ROPE_THETA = 500000.0
RW_HEADS = 8
RW_DIM = 512
DECAY_LORA = 64
AAA_LORA = 64
GATE_LORA = 160
D_FF = 2816
D_IN = 4640
RMS_EPS = 1e-6
RWKV_LN_EPS = 64e-5
NEG_INF = -1e30
SCAN_T = 16
LANES = 128
PACK_ROWS = 512

ADAM_LR = 0.001
ADAM_B1 = 0.9
ADAM_B2 = 0.999
ADAM_EPS = 1e-08
ADAM_WD = 0.01
ADAM_STEP = 10

C_R, C_K, C_V, C_Q = 0, 512, 1024, 1536
C_G1, C_G2 = 2048, 3072
C_DG, C_KA, C_VA, C_DW, C_DA = 4096, 4352, 4480, 4608, 4736
NP = 5120

VMEM_LIMIT = 48 * 1024 * 1024


def _cparams(sem):
    return pltpu.CompilerParams(dimension_semantics=sem, vmem_limit_bytes=VMEM_LIMIT)


def _pick(n, cands):
    for c in cands:
        if n % c == 0:
            return c
    raise ValueError(f"no tile for {n}")


def _mm(a, b, *, ta=False, tb=False, bias=None, residual=None, name):
    m = a.shape[1] if ta else a.shape[0]
    k = a.shape[0] if ta else a.shape[1]
    n = b.shape[0] if tb else b.shape[1]
    assert k == (b.shape[1] if tb else b.shape[0]), (a.shape, b.shape, ta, tb)
    tm = _pick(m, (512, 1408, 256, 128) if ta else (512, 528, 384, 256, 128))
    tn = _pick(n, (512, 1408, 256, 128))
    if k <= 1024:
        tk = k
    else:
        tk = _pick(k, (1024, 1056, 528, 512) if (ta and not tb) else (1024, 1408, 512, 256, 128))
    nk = k // tk
    has_bias = bias is not None
    has_res = residual is not None
    dn = (((0 if ta else 1,), (1 if tb else 0,)), ((), ()))

    def body(*refs):
        a_ref, b_ref = refs[0], refs[1]
        pos = 2
        bias_ref = res_ref = None
        if has_bias:
            bias_ref = refs[pos]
            pos += 1
        if has_res:
            res_ref = refs[pos]
            pos += 1
        o_ref, acc_ref = refs[pos], refs[pos + 1]
        kk = pl.program_id(2)
        part = lax.dot_general(a_ref[...].astype(BF16), b_ref[...].astype(BF16), dn, preferred_element_type=F32)

        def finish(out):
            if has_bias:
                out = out + bias_ref[...]
            if has_res:
                out = out + res_ref[...]
            o_ref[...] = out

        if nk == 1:
            finish(part)
        else:
            @pl.when(kk == 0)
            def _():
                acc_ref[...] = part

            @pl.when((kk > 0) & (kk < nk - 1))
            def _():
                acc_ref[...] += part

            @pl.when(kk == nk - 1)
            def _():
                finish(acc_ref[...] + part)

    in_specs = [
        pl.BlockSpec((tk, tm), lambda i, j, kk: (kk, i)) if ta else pl.BlockSpec((tm, tk), lambda i, j, kk: (i, kk)),
        pl.BlockSpec((tn, tk), lambda i, j, kk: (j, kk)) if tb else pl.BlockSpec((tk, tn), lambda i, j, kk: (kk, j)),
    ]
    args = [a, b]
    if has_bias:
        in_specs.append(pl.BlockSpec((1, tn), lambda i, j, kk: (0, j)))
        args.append(bias)
    if has_res:
        in_specs.append(pl.BlockSpec((tm, tn), lambda i, j, kk: (i, j)))
        args.append(residual)
    return pl.pallas_call(
        body, name=name, grid=(m // tm, n // tn, nk),
        in_specs=in_specs, out_specs=pl.BlockSpec((tm, tn), lambda i, j, kk: (i, j)),
        out_shape=jax.ShapeDtypeStruct((m, n), F32),
        scratch_shapes=[pltpu.VMEM((tm, tn) if nk > 1 else (8, LANES), F32)],
        compiler_params=_cparams(("parallel", "parallel", "arbitrary")),
    )(*args)


def _colsum(x, name):
    m, n = x.shape
    tm = BLOCK

    def body(x_ref, o_ref):
        i = pl.program_id(0)
        s = jnp.sum(x_ref[...], axis=0, keepdims=True)

        @pl.when(i == 0)
        def _():
            o_ref[...] = s

        @pl.when(i > 0)
        def _():
            o_ref[...] += s

    return pl.pallas_call(
        body, name=name, grid=(m // tm,),
        in_specs=[pl.BlockSpec((tm, n), lambda i: (i, 0))],
        out_specs=pl.BlockSpec((1, n), lambda i: (0, 0)),
        out_shape=jax.ShapeDtypeStruct((1, n), F32),
        compiler_params=_cparams(("arbitrary",)),
    )(x)


def _view(arr, width=None, col=0, off=0):
    return (arr, arr.shape[1] if width is None else width, col, off)


def _row_spec(view):
    _, width, col, off = view
    return pl.BlockSpec((BLOCK, width), lambda i, col=col, off=off: (i + off, col))


def _const_spec(arr):
    return pl.BlockSpec(arr.shape, lambda i: (0,) * arr.ndim)


def _rows_fwd(fn, rows, consts, out_widths, *, nblk, name):
    nr, nc = len(rows), len(consts)

    def body(*refs):
        i = pl.program_id(0)
        vals = [r[...] for r in refs[:nr + nc]]
        outs = fn(i, *vals)
        for o_ref, o in zip(refs[nr + nc:], outs):
            o_ref[...] = o

    return pl.pallas_call(
        body, name=name, grid=(nblk,),
        in_specs=[_row_spec(v) for v in rows] + [_const_spec(c) for c in consts],
        out_specs=[pl.BlockSpec((BLOCK, w), lambda i: (i, 0)) for w in out_widths],
        out_shape=[jax.ShapeDtypeStruct((nblk * BLOCK, w), F32) for w in out_widths],
        compiler_params=_cparams(("parallel",)),
    )(*[v[0] for v in rows], *consts)


def _rows_bwd(fn, rows, consts, cts, *, nblk, name, diff_rows, diff_consts, acc=None, fwd_widths=()):
    nr, nc = len(rows), len(consts)
    ct_views = [c for c in cts if c is not None]
    acc = acc or [None] * len(diff_rows)
    acc_views = [a for a in acc if a is not None]
    n_in = nr + nc + len(ct_views) + len(acc_views)
    n_fwd = len(fwd_widths)

    def body(*refs):
        i = pl.program_id(0)
        row_vals = [r[...] for r in refs[:nr]]
        const_vals = [r[...] for r in refs[nr:nr + nc]]
        ct_vals = [r[...] for r in refs[nr + nc:nr + nc + len(ct_views)]]
        acc_vals = [r[...] for r in refs[nr + nc + len(ct_views):n_in]]
        out_refs = refs[n_in:]

        def f(*dargs):
            rv = list(row_vals)
            cv = list(const_vals)
            for pos, idx in enumerate(diff_rows):
                rv[idx] = dargs[pos]
            for pos, idx in enumerate(diff_consts):
                cv[idx] = dargs[len(diff_rows) + pos]
            return tuple(fn(i, *rv, *cv))

        primals = [row_vals[idx] for idx in diff_rows] + [const_vals[idx] for idx in diff_consts]
        outs, pull = jax.vjp(f, *primals)
        full_ct, ci = [], 0
        for o, c in zip(outs, cts):
            if c is None:
                full_ct.append(jnp.zeros_like(o))
            else:
                full_ct.append(ct_vals[ci])
                ci += 1
        grads = pull(tuple(full_ct))
        for o_ref, o in zip(out_refs[:n_fwd], outs):
            o_ref[...] = o
        ai = 0
        for pos in range(len(diff_rows)):
            g = grads[pos]
            if acc[pos] is not None:
                g = g + acc_vals[ai]
                ai += 1
            out_refs[n_fwd + pos][...] = g
        for pos in range(len(diff_consts)):
            g = grads[len(diff_rows) + pos]
            o_ref = out_refs[n_fwd + len(diff_rows) + pos]

            @pl.when(i == 0)
            def _(o_ref=o_ref, g=g):
                o_ref[...] = g

            @pl.when(i > 0)
            def _(o_ref=o_ref, g=g):
                o_ref[...] += g

    out_specs = [pl.BlockSpec((BLOCK, w), lambda i: (i, 0)) for w in fwd_widths]
    out_shape = [jax.ShapeDtypeStruct((nblk * BLOCK, w), F32) for w in fwd_widths]
    for idx in diff_rows:
        out_specs.append(pl.BlockSpec((BLOCK, rows[idx][1]), lambda i: (i, 0)))
        out_shape.append(jax.ShapeDtypeStruct((nblk * BLOCK, rows[idx][1]), F32))
    for idx in diff_consts:
        out_specs.append(_const_spec(consts[idx]))
        out_shape.append(jax.ShapeDtypeStruct(consts[idx].shape, F32))
    return pl.pallas_call(
        body, name=name, grid=(nblk,),
        in_specs=([_row_spec(v) for v in rows] + [_const_spec(c) for c in consts]
                  + [_row_spec(v) for v in ct_views] + [_row_spec(v) for v in acc_views]),
        out_specs=out_specs, out_shape=out_shape,
        compiler_params=_cparams(("arbitrary",)),
    )(*[v[0] for v in rows], *consts, *[v[0] for v in ct_views], *[v[0] for v in acc_views])


def _rms_fn(i, x, g):
    return (x * lax.rsqrt(jnp.mean(x * x, axis=-1, keepdims=True) + RMS_EPS) * g,)


def _sigmoid(x):
    return 1.0 / (1.0 + jnp.exp(-x))


def _softplus(x):
    return jnp.maximum(x, 0.0) + jnp.log(1.0 + jnp.exp(-jnp.abs(x)))


def _head_ones(n=RW_DIM, head=HEAD_DIM):
    r = lax.broadcasted_iota(jnp.int32, (n, n), 0) // head
    c = lax.broadcasted_iota(jnp.int32, (n, n), 1) // head
    return (r == c).astype(F32)


def _head_sum(x, ones):
    return jnp.dot(x, ones, precision=HI, preferred_element_type=F32)


def _rwkv_pre_fn(i, r, k, v, dw, da, dg, r_p, k_p, v_p, dw_p, da_p, dg_p,
                 mix_r, mix_k, mix_v, mix_dw, mix_da, mix_dg, w0, w2, a0, a2, g2, k_k, k_a):
    row = i * BLOCK + lax.broadcasted_iota(jnp.int32, (BLOCK, 1), 0)
    live = row >= PAD_ROWS
    live_prev = row >= PAD_ROWS + 1

    def shift(cur, prev, mix):
        cur = jnp.where(live, cur, 0.0)
        prev = jnp.where(live_prev, prev, 0.0)
        return cur + (prev - cur) * mix

    r = shift(r, r_p, mix_r)
    k = shift(k, k_p, mix_k)
    v = shift(v, v_p, mix_v)
    dw = shift(dw, dw_p, mix_dw)
    da = shift(da, da_p, mix_da)
    dg = shift(dg, dg_p, mix_dg)
    ones = _head_ones()
    wlog = -_softplus(-(w0 + jnp.dot(jnp.tanh(dw), w2, preferred_element_type=F32))) - 0.5
    decay = jnp.exp(-jnp.exp(wlog))
    a = _sigmoid(a0 + jnp.dot(da, a2, preferred_element_type=F32))
    g = jnp.dot(_sigmoid(dg), g2, preferred_element_type=F32)
    kk = k * k_k
    norm_sq = jnp.where(live, _head_sum(kk * kk, ones), 1.0)
    kk = kk / jnp.maximum(jnp.sqrt(norm_sq), 1e-12)
    k_mod = k * (1.0 + (a - 1.0) * k_a)
    return r, decay, k_mod, v, -kk, kk * a, g


def _rwkv_post_fn(i, ys, r, k_mod, v, g, ln_w, ln_b, r_k):
    ones = _head_ones()
    mean = _head_sum(ys, ones) * (1.0 / HEAD_DIM)
    d = ys - mean
    var = _head_sum(d * d, ones) * (1.0 / HEAD_DIM)
    yn = d * lax.rsqrt(var + RWKV_LN_EPS) * ln_w + ln_b
    bonus = _head_sum(r * k_mod * r_k, ones) * v
    return ((yn + bonus) * g,)


def _merge_fn(i, ya, yr, g1, g2):
    return (_sigmoid(g1) * ya + _sigmoid(g2) * yr,)


def _swiglu_fn(i, gate, up):
    return (gate * _sigmoid(gate) * up,)


def _loss_fn(i, h, tgt, g):
    y = h * lax.rsqrt(jnp.mean(h * h, axis=-1, keepdims=True) + RMS_EPS) * g
    err = y - tgt
    return (0.5 * jnp.mean(err * err, axis=-1, keepdims=True),)


def _rope_tables(lp):
    pos = (jnp.arange(lp, dtype=jnp.int32) - PAD_ROWS).astype(F32)
    inv_freq = jnp.power(jnp.float32(ROPE_THETA), -jnp.arange(ROPE_HALF, dtype=F32) * (2.0 / ROPE_DIM))
    ang = pos[:, None] * inv_freq[None, :]
    cos, sin = jnp.cos(ang), jnp.sin(ang)
    one = jnp.ones((lp, HEAD_DIM - ROPE_DIM), F32)
    zero_h = jnp.zeros((lp, ROPE_HALF), F32)
    zero_r = jnp.zeros((lp, HEAD_DIM - ROPE_DIM), F32)
    c = jnp.concatenate([cos, cos, one], axis=1)
    s1 = jnp.concatenate([-sin, zero_h, zero_r], axis=1)
    s2 = jnp.concatenate([zero_h, sin, zero_r], axis=1)
    return tuple(jnp.tile(t, (1, Q_HEADS)) for t in (c, s1, s2))


def _rope_fwd_fn(i, x, c, s1, s2):
    n = x.shape[1]
    return (x * c + pltpu.roll(x, n - ROPE_HALF, 1) * s1 + pltpu.roll(x, ROPE_HALF, 1) * s2,)


def _rope_bwd_fn(i, dy, c, s1, s2):
    n = dy.shape[1]
    return (dy * c + pltpu.roll(dy * s1, ROPE_HALF, 1) + pltpu.roll(dy * s2, n - ROPE_HALF, 1),)


def _attn_mask(i):
    r = lax.broadcasted_iota(jnp.int32, (BLOCK, 3 * BLOCK), 0)
    c = lax.broadcasted_iota(jnp.int32, (BLOCK, 3 * BLOCK), 1)
    meta = (c < BLOCK) & (c >= PAD_ROWS)
    prev = (c >= BLOCK) & (c < 2 * BLOCK) & ((c - BLOCK) > r) & (i >= 1)
    cur = (c >= 2 * BLOCK) & ((c - 2 * BLOCK) <= r)
    return meta | prev | cur


def _attn_fwd(q, k, v, sinks, *, nblk, name):
    scale = HEAD_DIM ** -0.5

    def body(q_ref, km_ref, kp_ref, kc_ref, vm_ref, vp_ref, vc_ref, s_ref, o_ref, lse_ref):
        i = pl.program_id(0)
        valid = _attn_mask(i)
        for h in range(Q_HEADS):
            g = h // GROUP
            kcat = jnp.concatenate([km_ref[g], kp_ref[g], kc_ref[g]], axis=0).astype(BF16)
            vcat = jnp.concatenate([vm_ref[g], vp_ref[g], vc_ref[g]], axis=0).astype(BF16)
            s = lax.dot_general(q_ref[h].astype(BF16), kcat, (((1,), (1,)), ((), ())),
                                preferred_element_type=F32) * scale
            s = jnp.where(valid, s, NEG_INF)
            sink = s_ref[0:1, h:h + 1]
            m = jnp.maximum(jnp.max(s, axis=-1, keepdims=True), sink)
            p = jnp.exp(s - m)
            den = jnp.sum(p, axis=-1, keepdims=True) + jnp.exp(sink - m)
            o = jnp.dot(p.astype(BF16), vcat, preferred_element_type=F32)
            o_ref[h] = o / den
            lse_ref[h] = m + jnp.log(den)

    kv = lambda f: pl.BlockSpec((KV_HEADS, BLOCK, HEAD_DIM), f)
    return pl.pallas_call(
        body, name=name, grid=(nblk,),
        in_specs=[pl.BlockSpec((Q_HEADS, BLOCK, HEAD_DIM), lambda i: (0, i + 1, 0)),
                  kv(lambda i: (0, 0, 0)), kv(lambda i: (0, i, 0)), kv(lambda i: (0, i + 1, 0)),
                  kv(lambda i: (0, 0, 0)), kv(lambda i: (0, i, 0)), kv(lambda i: (0, i + 1, 0)),
                  pl.BlockSpec((1, Q_HEADS), lambda i: (0, 0))],
        out_specs=[pl.BlockSpec((Q_HEADS, BLOCK, HEAD_DIM), lambda i: (0, i, 0)),
                   pl.BlockSpec((Q_HEADS, BLOCK, 1), lambda i: (0, i, 0))],
        out_shape=[jax.ShapeDtypeStruct((Q_HEADS, nblk * BLOCK, HEAD_DIM), F32),
                   jax.ShapeDtypeStruct((Q_HEADS, nblk * BLOCK, 1), F32)],
        compiler_params=_cparams(("parallel",)),
    )(q, k, k, k, v, v, v, sinks)


def _attn_bwd(q, k, v, sinks, o, lse, do, *, nblk, name):
    scale = HEAD_DIM ** -0.5
    lp = k.shape[1]

    def body(q_ref, km_ref, kp_ref, kc_ref, vm_ref, vp_ref, vc_ref, s_ref, o_ref, lse_ref, do_ref,
             dq_ref, dk_ref, dv_ref, ds_ref):
        i = pl.program_id(0)

        @pl.when(i == 0)
        def _():
            dk_ref[...] = jnp.zeros_like(dk_ref)
            dv_ref[...] = jnp.zeros_like(dv_ref)
            ds_ref[...] = jnp.zeros_like(ds_ref)

        valid = _attn_mask(i)
        lane = lax.broadcasted_iota(jnp.int32, (1, Q_HEADS), 1)
        prev_rows = pl.ds(pl.multiple_of(i * BLOCK, BLOCK), BLOCK)
        cur_rows = pl.ds(pl.multiple_of((i + 1) * BLOCK, BLOCK), BLOCK)
        for h in range(Q_HEADS):
            g = h // GROUP
            qh = q_ref[h].astype(BF16)
            doh = do_ref[h]
            kcat = jnp.concatenate([km_ref[g], kp_ref[g], kc_ref[g]], axis=0).astype(BF16)
            vcat = jnp.concatenate([vm_ref[g], vp_ref[g], vc_ref[g]], axis=0).astype(BF16)
            s = lax.dot_general(qh, kcat, (((1,), (1,)), ((), ())), preferred_element_type=F32) * scale
            s = jnp.where(valid, s, NEG_INF)
            lse_h = lse_ref[h]
            p = jnp.exp(s - lse_h)
            delta = jnp.sum(doh * o_ref[h], axis=-1, keepdims=True)
            dp = lax.dot_general(doh.astype(BF16), vcat, (((1,), (1,)), ((), ())), preferred_element_type=F32)
            dsc = (p * (dp - delta) * scale).astype(BF16)
            dq_ref[h] = jnp.dot(dsc, kcat, preferred_element_type=F32)
            dk_all = lax.dot_general(dsc, qh, (((0,), (0,)), ((), ())), preferred_element_type=F32)
            dv_all = lax.dot_general(p.astype(BF16), doh.astype(BF16), (((0,), (0,)), ((), ())),
                                     preferred_element_type=F32)
            dk_ref[g, 0:BLOCK, :] += dk_all[0:BLOCK]
            dk_ref[g, prev_rows, :] += dk_all[BLOCK:2 * BLOCK]
            dk_ref[g, cur_rows, :] += dk_all[2 * BLOCK:]
            dv_ref[g, 0:BLOCK, :] += dv_all[0:BLOCK]
            dv_ref[g, prev_rows, :] += dv_all[BLOCK:2 * BLOCK]
            dv_ref[g, cur_rows, :] += dv_all[2 * BLOCK:]
            p_sink = jnp.exp(s_ref[0:1, h:h + 1] - lse_h)
            dsink = -jnp.sum(p_sink * delta, axis=0, keepdims=True)
            ds_ref[...] += jnp.where(lane == h, dsink, 0.0)

    kv = lambda f: pl.BlockSpec((KV_HEADS, BLOCK, HEAD_DIM), f)
    qspec = pl.BlockSpec((Q_HEADS, BLOCK, HEAD_DIM), lambda i: (0, i, 0))
    whole = pl.BlockSpec((KV_HEADS, lp, HEAD_DIM), lambda i: (0, 0, 0))
    return pl.pallas_call(
        body, name=name, grid=(nblk,),
        in_specs=[pl.BlockSpec((Q_HEADS, BLOCK, HEAD_DIM), lambda i: (0, i + 1, 0)),
                  kv(lambda i: (0, 0, 0)), kv(lambda i: (0, i, 0)), kv(lambda i: (0, i + 1, 0)),
                  kv(lambda i: (0, 0, 0)), kv(lambda i: (0, i, 0)), kv(lambda i: (0, i + 1, 0)),
                  pl.BlockSpec((1, Q_HEADS), lambda i: (0, 0)),
                  qspec, pl.BlockSpec((Q_HEADS, BLOCK, 1), lambda i: (0, i, 0)), qspec],
        out_specs=[qspec, whole, whole, pl.BlockSpec((1, Q_HEADS), lambda i: (0, 0))],
        out_shape=[jax.ShapeDtypeStruct((Q_HEADS, nblk * BLOCK, HEAD_DIM), F32),
                   jax.ShapeDtypeStruct((KV_HEADS, lp, HEAD_DIM), F32),
                   jax.ShapeDtypeStruct((KV_HEADS, lp, HEAD_DIM), F32),
                   jax.ShapeDtypeStruct((1, Q_HEADS), F32)],
        compiler_params=_cparams(("arbitrary",)),
    )(q, k, k, k, v, v, v, sinks, o, lse, do)


N_VEC = 5
VEC_ROWS = N_VEC * HEAD_DIM


def _selectors():
    t = SCAN_T
    shape = (t, 3 * LANES, RW_DIM)
    step, src, dst = [lax.broadcasted_iota(jnp.int32, shape, d) for d in range(3)]
    src = src % LANES
    spread = ((src // t == dst // HEAD_DIM) & (src % t == step)).astype(BF16)
    shape = (t, 2 * RW_DIM, LANES)
    step, src, dst = [lax.broadcasted_iota(jnp.int32, shape, d) for d in range(3)]
    src = src % RW_DIM
    collect = ((src // HEAD_DIM == dst // t) & (dst % t == step)).astype(BF16)
    return spread, collect


def _split3(x):
    hi = x.astype(BF16)
    rest = x - hi.astype(F32)
    mid = rest.astype(BF16)
    lo = (rest - mid.astype(F32)).astype(BF16)
    return jnp.concatenate([hi, mid, lo], axis=1)


def _split2(x):
    hi = x.astype(BF16)
    mid = (x - hi.astype(F32)).astype(BF16)
    return jnp.concatenate([hi, mid], axis=1)


def _wkv_fwd(xt, v, spread, name):
    nch = xt.shape[0]
    t_steps = SCAN_T

    def body(xt_ref, v_ref, sel_ref, y_ref, hist_ref, st_ref):
        @pl.when(pl.program_id(0) == 0)
        def _():
            st_ref[...] = jnp.zeros_like(st_ref)

        x3 = _split3(xt_ref[0])
        st = st_ref[...]
        for j in range(t_steps):
            cols = jnp.dot(x3, sel_ref[j], preferred_element_type=F32)
            a_c, w_c, b_c, k_c, r_c = [cols[n * HEAD_DIM:(n + 1) * HEAD_DIM] for n in range(N_VEC)]
            hist_ref[j] = st
            sa = jnp.sum(st * a_c, axis=0, keepdims=True)
            st = st * w_c + b_c * sa + k_c * v_ref[j:j + 1, :]
            y_ref[j:j + 1, :] = jnp.sum(st * r_c, axis=0, keepdims=True)
        st_ref[...] = st

    return pl.pallas_call(
        body, name=name, grid=(nch,),
        in_specs=[pl.BlockSpec((1, VEC_ROWS, LANES), lambda c: (c, 0, 0)),
                  pl.BlockSpec((t_steps, RW_DIM), lambda c: (c, 0)),
                  pl.BlockSpec(spread.shape, lambda c: (0, 0, 0))],
        out_specs=[pl.BlockSpec((t_steps, RW_DIM), lambda c: (c, 0)),
                   pl.BlockSpec((t_steps, HEAD_DIM, RW_DIM), lambda c: (c, 0, 0))],
        out_shape=[jax.ShapeDtypeStruct((nch * t_steps, RW_DIM), F32),
                   jax.ShapeDtypeStruct((nch * t_steps, HEAD_DIM, RW_DIM), F32)],
        scratch_shapes=[pltpu.VMEM((HEAD_DIM, RW_DIM), F32)],
        compiler_params=_cparams(("arbitrary",)),
    )(xt, v, spread)


def _wkv_bwd(xt, v, hist, dy, spread, collect, name):
    nch = xt.shape[0]
    t_steps = SCAN_T

    def body(xt_ref, v_ref, hist_ref, dy_ref, sel_ref, col_ref, dxt_ref, dv_ref, g_ref):
        @pl.when(pl.program_id(0) == 0)
        def _():
            g_ref[...] = jnp.zeros_like(g_ref)

        x3 = _split3(xt_ref[0])
        gst = g_ref[...]
        acc = jnp.zeros((VEC_ROWS, LANES), F32)
        for j in reversed(range(t_steps)):
            cols = jnp.dot(x3, sel_ref[j], preferred_element_type=F32)
            a_c, w_c, b_c, k_c, r_c = [cols[n * HEAD_DIM:(n + 1) * HEAD_DIM] for n in range(N_VEC)]
            prev = hist_ref[j]
            v_row = v_ref[j:j + 1, :]
            dy_row = dy_ref[j:j + 1, :]
            sa = jnp.sum(prev * a_c, axis=0, keepdims=True)
            nxt = prev * w_c + b_c * sa + k_c * v_row
            gst = gst + r_c * dy_row
            dv_ref[j:j + 1, :] = jnp.sum(gst * k_c, axis=0, keepdims=True)
            dsa = jnp.sum(gst * b_c, axis=0, keepdims=True)
            prods = jnp.concatenate([prev * dsa, gst * prev, gst * sa, gst * v_row, nxt * dy_row], axis=0)
            acc = acc + jnp.dot(_split2(prods), col_ref[j], preferred_element_type=F32)
            gst = gst * w_c + a_c * dsa
        dxt_ref[0] = acc
        g_ref[...] = gst

    rev3 = lambda c: (nch - 1 - c, 0, 0)
    rev2 = lambda c: (nch - 1 - c, 0)
    rowspec = pl.BlockSpec((t_steps, RW_DIM), rev2)
    return pl.pallas_call(
        body, name=name, grid=(nch,),
        in_specs=[pl.BlockSpec((1, VEC_ROWS, LANES), rev3), rowspec,
                  pl.BlockSpec((t_steps, HEAD_DIM, RW_DIM), rev3), rowspec,
                  pl.BlockSpec(spread.shape, lambda c: (0, 0, 0)), pl.BlockSpec(collect.shape, lambda c: (0, 0, 0))],
        out_specs=[pl.BlockSpec((1, VEC_ROWS, LANES), rev3), rowspec],
        out_shape=[jax.ShapeDtypeStruct((nch, VEC_ROWS, LANES), F32),
                   jax.ShapeDtypeStruct((nch * t_steps, RW_DIM), F32)],
        scratch_shapes=[pltpu.VMEM((HEAD_DIM, RW_DIM), F32)],
        compiler_params=_cparams(("arbitrary",)),
    )(xt, v, hist, dy, spread, collect)


def _to_xt(vecs, nch):
    parts = [x.reshape(nch, SCAN_T, RW_HEADS, HEAD_DIM).transpose(0, 3, 2, 1).reshape(nch, HEAD_DIM, LANES)
             for x in vecs]
    return jnp.concatenate(parts, axis=1)


def _from_xt(xt):
    nch = xt.shape[0]
    x = xt.reshape(nch, N_VEC, HEAD_DIM, RW_HEADS, SCAN_T).transpose(1, 0, 4, 3, 2)
    return [x[n].reshape(nch * SCAN_T, RW_DIM) for n in range(N_VEC)]


MESH = pl.DeviceIdType.MESH
ANY = pl.BlockSpec(memory_space=pl.ANY)


def _all_gather(x, name):
    rows, n = x.shape

    def body(x_ref, out_ref, send_sems, recv_sems, local_sem):
        xi, yi, ci = lax.axis_index("x"), lax.axis_index("y"), lax.axis_index("c")
        me, sibling = (xi, yi, ci), (xi, yi, 1 - ci)
        chips = [(1 - xi, yi), (xi, 1 - yi), (1 - xi, 1 - yi)]

        def slot(px, py, pc):
            return out_ref.at[4 * px + 2 * py + pc]

        def copy(sem, block, to, src=None):
            return pltpu.make_async_remote_copy(
                src_ref=slot(*block) if src is None else src, dst_ref=slot(*block),
                send_sem=send_sems.at[sem], recv_sem=recv_sems.at[sem], device_id=to, device_id_type=MESH)

        mine = pltpu.make_async_copy(x_ref, slot(*me), local_sem)
        mine.start()
        first = [copy(0, me, sibling, src=x_ref)]
        first += [copy(1 + j, me, (*chip, ci), src=x_ref) for j, chip in enumerate(chips)]
        for cp in first:
            cp.start()
        passed = [copy(4 + j, (*chip, ci), sibling) for j, chip in enumerate(chips)]
        for j, chip in enumerate(chips):
            copy(1 + j, (*chip, ci), me).wait_recv()
            passed[j].start()
        copy(0, sibling, me).wait_recv()
        for j, chip in enumerate(chips):
            copy(4 + j, (*chip, 1 - ci), me).wait_recv()
        for cp in first + passed:
            cp.wait_send()
        mine.wait()

    return pl.pallas_call(
        body, name=name, out_shape=jax.ShapeDtypeStruct((N_DEV, rows, n), x.dtype),
        in_specs=[ANY], out_specs=ANY,
        scratch_shapes=[pltpu.SemaphoreType.DMA((7,)), pltpu.SemaphoreType.DMA((7,)), pltpu.SemaphoreType.DMA(())],
    )(x)


def _exchange(g_big, g_small, name):
    def body(gb_ref, gs_ref, rb_ref, rs_ref, send_b, recv_b, send_s, recv_s, local_sems):
        xi, yi, ci = lax.axis_index("x"), lax.axis_index("y"), lax.axis_index("c")
        me = 4 * xi + 2 * yi + ci
        copies = []
        for d in range(1, N_DEV):
            px = 1 - xi if d & 4 else xi
            py = 1 - yi if d & 2 else yi
            pc = 1 - ci if d & 1 else ci
            peer = 4 * px + 2 * py + pc
            copies.append(pltpu.make_async_remote_copy(
                src_ref=gb_ref.at[peer], dst_ref=rb_ref.at[me], send_sem=send_b.at[d - 1], recv_sem=recv_b.at[d - 1],
                device_id=(px, py, pc), device_id_type=MESH))
            copies.append(pltpu.make_async_remote_copy(
                src_ref=gs_ref, dst_ref=rs_ref.at[me], send_sem=send_s.at[d - 1], recv_sem=recv_s.at[d - 1],
                device_id=(px, py, pc), device_id_type=MESH))
        own_b = pltpu.make_async_copy(gb_ref.at[me], rb_ref.at[me], local_sems.at[0])
        own_s = pltpu.make_async_copy(gs_ref, rs_ref.at[me], local_sems.at[1])
        for cp in copies:
            cp.start()
        own_b.start()
        own_s.start()
        for cp in copies:
            cp.wait()
        own_b.wait()
        own_s.wait()

    dma7 = pltpu.SemaphoreType.DMA((7,))
    return pl.pallas_call(
        body, name=name,
        out_shape=[jax.ShapeDtypeStruct(g_big.shape, g_big.dtype),
                   jax.ShapeDtypeStruct((N_DEV,) + g_small.shape, g_small.dtype)],
        in_specs=[ANY, ANY], out_specs=[ANY, ANY],
        scratch_shapes=[dma7, dma7, dma7, dma7, pltpu.SemaphoreType.DMA((2,))],
    )(g_big, g_small)


def _adamw(parts, w, m, v, name):
    rows = w.shape[0]
    tile = _pick(rows, (PACK_ROWS, 104, 8))

    def body(p_ref, w_ref, m_ref, v_ref, g_out, d_out, m_out, v_out):
        g = p_ref[0].astype(F32)
        for s in range(1, N_DEV):
            g = g + p_ref[s].astype(F32)
        m_new = ADAM_B1 * m_ref[...] + (1.0 - ADAM_B1) * g
        v_new = ADAM_B2 * v_ref[...] + (1.0 - ADAM_B2) * (g * g)
        m_hat = m_new / (1.0 - ADAM_B1 ** ADAM_STEP)
        v_hat = v_new / (1.0 - ADAM_B2 ** ADAM_STEP)
        g_out[...] = g
        d_out[...] = -ADAM_LR * (m_hat / (jnp.sqrt(v_hat) + ADAM_EPS) + ADAM_WD * w_ref[...])
        m_out[...] = m_new
        v_out[...] = v_new

    spec = pl.BlockSpec((tile, LANES), lambda i: (i, 0))
    return pl.pallas_call(
        body, name=name, grid=(rows // tile,),
        in_specs=[pl.BlockSpec((N_DEV, tile, LANES), lambda i: (0, i, 0)), spec, spec, spec],
        out_specs=[spec] * 4, out_shape=[jax.ShapeDtypeStruct((rows, LANES), F32)] * 4,
        compiler_params=_cparams(("parallel",)),
    )(parts, w, m, v)


SHARDED = [
    ("meta_tokens", 1), ("w_in", 1), ("rwkv_w2", 1), ("rwkv_a2", 1), ("rwkv_g2", 1), ("w_br_attn", 1),
    ("w_br_rwkv", 1), ("w_o", 0), ("w_ffn_gate", 1), ("w_ffn_up", 1), ("w_ffn_down", 0)]
REPLICATED = ["norm_mix_g", "b_in", "attn_sinks", "rwkv_mix", "rwkv_w0", "rwkv_a0", "rwkv_k_k", "rwkv_k_a",
              "rwkv_r_k", "rwkv_ln_w", "rwkv_ln_b", "norm_ffn_g", "norm_final_g"]
WEIGHTS = ["meta_tokens", "norm_mix_g", "w_in", "b_in", "attn_sinks", "rwkv_mix", "rwkv_w0", "rwkv_w2", "rwkv_a0",
           "rwkv_a2", "rwkv_g2", "rwkv_k_k", "rwkv_k_a", "rwkv_r_k", "rwkv_ln_w", "rwkv_ln_b", "w_br_attn",
           "w_br_rwkv", "w_o", "norm_ffn_g", "w_ffn_gate", "w_ffn_up", "w_ffn_down", "norm_final_g"]


def _pack(arrays, row_multiple):
    flat = jnp.concatenate([a.reshape(-1) for a in arrays])
    per = row_multiple * LANES
    total = -(-flat.shape[0] // per) * per
    return jnp.pad(flat, (0, total - flat.shape[0])).reshape(-1, LANES)


def _unpack(buf, shapes):
    flat = buf.reshape(-1)
    out, pos = [], 0
    for s in shapes:
        n = 1
        for d in s:
            n *= d
        out.append(flat[pos:pos + n].reshape(s))
        pos += n
    return out


def _strip(name, a):
    return a if name in ("meta_tokens", "norm_final_g") else a[0]


def _shard_offsets(shapes):
    offs, pos = [], 0
    for r, c in shapes:
        offs.append(pos)
        pos += r * c
    return offs


def _assemble(gathered, shapes):
    flat = gathered.reshape(N_DEV, -1)
    out = {}
    for (name, axis), (r, c), off in zip(SHARDED, shapes, _shard_offsets(shapes)):
        piece = flat[:, off:off + r * c]
        if axis == 0:
            out[name] = piece.reshape(N_DEV * r, c)
        else:
            out[name] = piece.reshape(N_DEV, r, c).transpose(1, 0, 2).reshape(r, N_DEV * c)
    return out


def _disassemble(grads, shapes, rows):
    pieces = []
    for (name, axis), (r, c) in zip(SHARDED, shapes):
        g = grads[name]
        if axis == 0:
            pieces.append(g.reshape(N_DEV, r * c))
        else:
            pieces.append(g.reshape(r, N_DEV, c).transpose(1, 0, 2).reshape(N_DEV, r * c))
    flat = jnp.concatenate(pieces, axis=1)
    return jnp.pad(flat, ((0, 0), (0, rows * LANES - flat.shape[1]))).reshape(N_DEV, rows, LANES)


def _w_in_padded(w):
    z = lambda n: jnp.zeros((w.shape[0], n), w.dtype)
    return jnp.concatenate([
        w[:, 768:2304], w[:, 0:512], w[:, 2592:4640], w[:, 2432:2592], z(256 - GATE_LORA), w[:, 512:768],
        w[:, 2304:2368], z(128 - DECAY_LORA), w[:, 2368:2432], z(128 - AAA_LORA), z(NP - C_DA - 128)], axis=1)


def _w_in_unpadded(wp):
    return jnp.concatenate([
        wp[:, C_Q:C_Q + 512], wp[:, C_KA:C_KA + 256], wp[:, C_R:C_R + 1536], wp[:, C_DW:C_DW + DECAY_LORA],
        wp[:, C_DA:C_DA + AAA_LORA], wp[:, C_DG:C_DG + GATE_LORA], wp[:, C_G1:C_G1 + 2048]], axis=1)


def _pad_rows(a, n):
    return jnp.pad(a, ((0, n - a.shape[0]), (0, 0)))


def _device_step(x, tgt, full):
    seq = x.shape[0]
    nblk = seq // BLOCK
    lp = seq + BLOCK
    nall = nblk + 1
    nch = lp // SCAN_T

    w_in_p = _w_in_padded(full["w_in"])
    b_in_p = _w_in_padded(full["b_in"][None])
    mix = full["rwkv_mix"][None]
    mix_r, mix_k, mix_v = mix[:, 0:512], mix[:, 512:1024], mix[:, 1024:1536]
    mix_dw = jnp.pad(mix[:, 1536:1600], ((0, 0), (0, 64)))
    mix_da = jnp.pad(mix[:, 1600:1664], ((0, 0), (0, 64)))
    mix_dg = jnp.pad(mix[:, 1664:1824], ((0, 0), (0, 96)))
    w2_p = _pad_rows(full["rwkv_w2"].astype(F32), 128)
    a2_p = _pad_rows(full["rwkv_a2"].astype(F32), 128)
    g2_p = _pad_rows(full["rwkv_g2"].astype(F32), 256)
    row = lambda name: full[name].reshape(1, -1)
    sinks = row("attn_sinks")
    rope_c, rope_s1, rope_s2 = _rope_tables(lp)

    hpad = jnp.concatenate([jnp.zeros((PAD_ROWS, D_MODEL), F32), full["meta_tokens"].astype(F32), x], axis=0)
    (u,) = _rows_fwd(_rms_fn, [_view(hpad)], [row("norm_mix_g")], [D_MODEL], nblk=nall, name="norm_mix")
    proj = _mm(u, w_in_p, bias=b_in_p, name="in_proj")
    (q_r,) = _rows_fwd(_rope_fwd_fn, [_view(proj, 512, C_Q // 512), _view(rope_c), _view(rope_s1), _view(rope_s2)],
                       [], [512], nblk=nall, name="rope_q")
    (k_r,) = _rows_fwd(_rope_fwd_fn, [_view(proj, 128, C_KA // 128), _view(rope_c, 128), _view(rope_s1, 128),
                                      _view(rope_s2, 128)], [], [128], nblk=nall, name="rope_k")
    heads = lambda a, nh: a.reshape(a.shape[0], nh, HEAD_DIM).transpose(1, 0, 2)
    unheads = lambda a: a.transpose(1, 0, 2).reshape(a.shape[1], -1)
    q_h, k_h = heads(q_r, Q_HEADS), heads(k_r, KV_HEADS)
    v_h = heads(proj[:, C_VA:C_VA + 128], KV_HEADS)
    o_h, lse = _attn_fwd(q_h, k_h, v_h, sinks, nblk=nblk, name="attn_fwd")
    y_attn = unheads(o_h)

    rw_cols = jnp.concatenate([proj[:, C_R:C_R + 1536], proj[:, C_DG:C_DG + 256], proj[:, C_DW:C_DW + 256]], axis=1)
    rw_prev = jnp.pad(rw_cols[:-1], ((1, 0), (0, 0)))
    pre_rows = [_view(proj, 512, 0), _view(proj, 512, 1), _view(proj, 512, 2), _view(proj, 128, C_DW // 128),
                _view(proj, 128, C_DA // 128), _view(proj, 256, C_DG // 256),
                _view(rw_prev, 512, 0), _view(rw_prev, 512, 1), _view(rw_prev, 512, 2), _view(rw_prev, 128, 14),
                _view(rw_prev, 128, 15), _view(rw_prev, 256, 6)]
    pre_consts = [mix_r, mix_k, mix_v, mix_dw, mix_da, mix_dg, row("rwkv_w0"), w2_p, row("rwkv_a0"), a2_p, g2_p,
                  row("rwkv_k_k"), row("rwkv_k_a")]
    r_t, decay, k_mod, v_t, a_neg, b_t, gate = _rows_fwd(_rwkv_pre_fn, pre_rows, pre_consts, [RW_DIM] * 7,
                                                         nblk=nall, name="rwkv_pre")
    spread, collect = _selectors()
    xt = _to_xt((a_neg, decay, b_t, k_mod, r_t), nch)
    y_scan, hist = _wkv_fwd(xt, v_t, spread, name="wkv_fwd")
    post_rows = [_view(y_scan, off=1), _view(r_t, off=1), _view(k_mod, off=1), _view(v_t, off=1), _view(gate, off=1)]
    post_consts = [row("rwkv_ln_w"), row("rwkv_ln_b"), row("rwkv_r_k")]
    (y_rwkv,) = _rows_fwd(_rwkv_post_fn, post_rows, post_consts, [RW_DIM], nblk=nblk, name="rwkv_post")

    ya = _mm(y_attn, full["w_br_attn"], name="br_attn")
    yr = _mm(y_rwkv, full["w_br_rwkv"], name="br_rwkv")
    merge_rows = [_view(ya), _view(yr), _view(proj, 1024, C_G1 // 1024, 1), _view(proj, 1024, C_G2 // 1024, 1)]
    (merged,) = _rows_fwd(_merge_fn, merge_rows, [], [D_MODEL], nblk=nblk, name="merge")
    h1 = _mm(merged, full["w_o"], residual=x, name="out_proj")
    (f,) = _rows_fwd(_rms_fn, [_view(h1)], [row("norm_ffn_g")], [D_MODEL], nblk=nblk, name="norm_ffn")
    ff_gate = _mm(f, full["w_ffn_gate"], name="ffn_gate")
    ff_up = _mm(f, full["w_ffn_up"], name="ffn_up")
    (act,) = _rows_fwd(_swiglu_fn, [_view(ff_gate), _view(ff_up)], [], [D_FF], nblk=nblk, name="swiglu")
    h2 = _mm(act, full["w_ffn_down"], residual=h1, name="ffn_down")

    grads = {}
    ones_col = jnp.ones((seq, 1), F32)
    loss_rows, dh2, grads["norm_final_g"] = _rows_bwd(
        _loss_fn, [_view(h2), _view(tgt)], [row("norm_final_g")], [_view(ones_col)], nblk=nblk, name="loss",
        diff_rows=[0], diff_consts=[0], fwd_widths=[1])
    loss = jnp.sum(loss_rows)

    dact = _mm(dh2, full["w_ffn_down"], tb=True, name="d_act")
    grads["w_ffn_down"] = _mm(act, dh2, ta=True, name="dw_ffn_down")
    dgate, dup = _rows_bwd(_swiglu_fn, [_view(ff_gate), _view(ff_up)], [], [_view(dact)], nblk=nblk,
                           name="swiglu_bwd", diff_rows=[0, 1], diff_consts=[])
    grads["w_ffn_gate"] = _mm(f, dgate, ta=True, name="dw_ffn_gate")
    grads["w_ffn_up"] = _mm(f, dup, ta=True, name="dw_ffn_up")
    df = _mm(dgate, full["w_ffn_gate"], tb=True, name="df_gate")
    df = _mm(dup, full["w_ffn_up"], tb=True, residual=df, name="df_up")
    dh1, grads["norm_ffn_g"] = _rows_bwd(_rms_fn, [_view(h1)], [row("norm_ffn_g")], [_view(df)], nblk=nblk,
                                         name="norm_ffn_bwd", diff_rows=[0], diff_consts=[0], acc=[_view(dh2)])
    dmerged = _mm(dh1, full["w_o"], tb=True, name="d_merged")
    grads["w_o"] = _mm(merged, dh1, ta=True, name="dw_o")
    dya, dyr, dg1, dg2 = _rows_bwd(_merge_fn, merge_rows, [], [_view(dmerged)], nblk=nblk, name="merge_bwd",
                                   diff_rows=[0, 1, 2, 3], diff_consts=[])
    grads["w_br_attn"] = _mm(y_attn, dya, ta=True, name="dw_br_attn")
    grads["w_br_rwkv"] = _mm(y_rwkv, dyr, ta=True, name="dw_br_rwkv")
    dy_attn = _mm(dya, full["w_br_attn"], tb=True, name="d_y_attn")
    dy_rwkv = _mm(dyr, full["w_br_rwkv"], tb=True, name="d_y_rwkv")

    post = _rows_bwd(_rwkv_post_fn, post_rows, post_consts, [_view(dy_rwkv)], nblk=nblk, name="rwkv_post_bwd",
                     diff_rows=[0, 1, 2, 3, 4], diff_consts=[0, 1, 2])
    dys, dr_post, dk_post, dv_post, dgate_post = [jnp.pad(t, ((BLOCK, 0), (0, 0))) for t in post[:5]]
    grads["rwkv_ln_w"], grads["rwkv_ln_b"], grads["rwkv_r_k"] = post[5:]
    dxt, dv_s = _wkv_bwd(xt, v_t, hist, dys, spread, collect, name="wkv_bwd")
    da_s, dw_s, db_s, dk_s, dr_s = _from_xt(dxt)
    pre_cts = [_view(dr_s + dr_post), _view(dw_s), _view(dk_s + dk_post), _view(dv_s + dv_post), _view(da_s),
               _view(db_s), _view(dgate_post)]
    pre = _rows_bwd(_rwkv_pre_fn, pre_rows, pre_consts, pre_cts, nblk=nall, name="rwkv_pre_bwd",
                    diff_rows=list(range(12)), diff_consts=list(range(13)))
    d_cur, d_prev, d_par = pre[0:6], pre[6:12], pre[12:]
    up = lambda t: jnp.pad(t[1:], ((0, 1), (0, 0)))
    d_rw = [c + up(p) for c, p in zip(d_cur, d_prev)]
    grads["rwkv_mix"] = jnp.concatenate([d_par[0], d_par[1], d_par[2], d_par[3][:, :DECAY_LORA],
                                         d_par[4][:, :AAA_LORA], d_par[5][:, :GATE_LORA]], axis=1)
    grads["rwkv_w0"], grads["rwkv_w2"] = d_par[6], d_par[7][:DECAY_LORA]
    grads["rwkv_a0"], grads["rwkv_a2"] = d_par[8], d_par[9][:AAA_LORA]
    grads["rwkv_g2"] = d_par[10][:GATE_LORA]
    grads["rwkv_k_k"], grads["rwkv_k_a"] = d_par[11], d_par[12]

    do_h = heads(dy_attn, Q_HEADS)
    dq_h, dk_h, dv_h, grads["attn_sinks"] = _attn_bwd(q_h, k_h, v_h, sinks, o_h, lse, do_h, nblk=nblk,
                                                      name="attn_bwd")
    dq_r = jnp.pad(unheads(dq_h), ((BLOCK, 0), (0, 0)))
    (dq,) = _rows_fwd(_rope_bwd_fn, [_view(dq_r), _view(rope_c), _view(rope_s1), _view(rope_s2)], [], [512],
                      nblk=nall, name="rope_q_bwd")
    (dka,) = _rows_fwd(_rope_bwd_fn, [_view(unheads(dk_h)), _view(rope_c, 128), _view(rope_s1, 128),
                                      _view(rope_s2, 128)], [], [128], nblk=nall, name="rope_k_bwd")
    dva = unheads(dv_h)

    lead = lambda t: jnp.pad(t, ((BLOCK, 0), (0, 0)))
    dproj = jnp.concatenate([d_rw[0], d_rw[1], d_rw[2], dq, lead(dg1), lead(dg2), d_rw[5], dka, dva, d_rw[3],
                             d_rw[4], jnp.zeros((lp, NP - C_DA - 128), F32)], axis=1)
    grads["w_in"] = _w_in_unpadded(_mm(u, dproj, ta=True, name="dw_in"))
    grads["b_in"] = _w_in_unpadded(_colsum(dproj, name="db_in"))
    du = _mm(dproj, w_in_p, tb=True, name="d_u")
    dh, grads["norm_mix_g"] = _rows_bwd(_rms_fn, [_view(hpad)], [row("norm_mix_g")], [_view(du)], nblk=nall,
                                        name="norm_mix_bwd", diff_rows=[0], diff_consts=[0], acc=[_view(lead(dh1))])
    grads["meta_tokens"] = dh[PAD_ROWS:BLOCK]
    return loss, dh[BLOCK:], grads


def kernel(x, meta_tokens, norm_mix_g, w_in, b_in, attn_sinks, rwkv_mix, rwkv_w0, rwkv_w2, rwkv_a0, rwkv_a2, rwkv_g2, rwkv_k_k, rwkv_k_a, rwkv_r_k, rwkv_ln_w, rwkv_ln_b, w_br_attn, w_br_rwkv, w_o, norm_ffn_g, w_ffn_gate, w_ffn_up, w_ffn_down, norm_final_g, loss_target, m_meta_tokens, m_norm_mix_g, m_w_in, m_b_in, m_attn_sinks, m_rwkv_mix, m_rwkv_w0, m_rwkv_w2, m_rwkv_a0, m_rwkv_a2, m_rwkv_g2, m_rwkv_k_k, m_rwkv_k_a, m_rwkv_r_k, m_rwkv_ln_w, m_rwkv_ln_b, m_w_br_attn, m_w_br_rwkv, m_w_o, m_norm_ffn_g, m_w_ffn_gate, m_w_ffn_up, m_w_ffn_down, m_norm_final_g, v_meta_tokens, v_norm_mix_g, v_w_in, v_b_in, v_attn_sinks, v_rwkv_mix, v_rwkv_w0, v_rwkv_w2, v_rwkv_a0, v_rwkv_a2, v_rwkv_g2, v_rwkv_k_k, v_rwkv_k_a, v_rwkv_r_k, v_rwkv_ln_w, v_rwkv_ln_b, v_w_br_attn, v_w_br_rwkv, v_w_o, v_norm_ffn_g, v_w_ffn_gate, v_w_ffn_up, v_w_ffn_down, v_norm_final_g):
    given = dict(locals())
    wts = {n: _strip(n, given[n]) for n in WEIGHTS}
    mom = {n: _strip(n, given["m_" + n]) for n in WEIGHTS}
    var = {n: _strip(n, given["v_" + n]) for n in WEIGHTS}
    shard_shapes = [wts[n].shape for n, _ in SHARDED]
    small_shapes = [wts[n].shape for n in REPLICATED]

    w_pack = _pack([wts[n] for n, _ in SHARDED], PACK_ROWS)
    gathered = _all_gather(w_pack.astype(BF16), name="gather_weights")
    full = {n: wts[n] for n in REPLICATED}
    full.update(_assemble(gathered, shard_shapes))

    loss_part, grad_x, grads = _device_step(x[0], loss_target[0], full)

    g_big = _disassemble(grads, shard_shapes, w_pack.shape[0]).astype(BF16)
    zero = jnp.zeros((1,), F32)
    g_small = _pack([grads[n].reshape(wts[n].shape) for n in REPLICATED] + [loss_part.reshape(1)], 8)
    parts_big, parts_small = _exchange(g_big, g_small, name="exchange_grads")
    big = _adamw(parts_big, w_pack, _pack([mom[n] for n, _ in SHARDED], PACK_ROWS),
                 _pack([var[n] for n, _ in SHARDED], PACK_ROWS), name="adamw_sharded")
    small = _adamw(parts_small, _pack([wts[n] for n in REPLICATED] + [zero], 8),
                   _pack([mom[n] for n in REPLICATED] + [zero], 8), _pack([var[n] for n in REPLICATED] + [zero], 8),
                   name="adamw_replicated")

    results = [{}, {}, {}, {}]
    for kind in range(4):
        for (n, _), a in zip(SHARDED, _unpack(big[kind], shard_shapes)):
            results[kind][n] = a
        for n, a in zip(REPLICATED, _unpack(small[kind], small_shapes)):
            results[kind][n] = a
    loss = _unpack(small[0], small_shapes + [(1,)])[-1][0]
    out = [loss, grad_x[None]]
    for kind in range(4):
        out += [results[kind][n].reshape(given[n].shape) for n in WEIGHTS]
    return tuple(out)
```

```python
import functools

import jax
import jax.numpy as jnp
from jax import lax
from jax.experimental import pallas as pl
from jax.experimental.pallas import tpu as pltpu

F32 = jnp.float32
BF16 = jnp.bfloat16
HI = lax.Precision.HIGHEST

N_DEV = 8
D_MODEL = 1024
N_META = 16
BLOCK = 128
PAD_ROWS = BLOCK - N_META
HEAD_DIM = 64
Q_HEADS = 8
KV_HEADS = 2
GROUP = Q_HEADS // KV_HEADS
ROPE_DIM = HEAD_DIM // 4
ROPE_HALF = ROPE_DIM // 2
ROPE_THETA = 500000.0
RW_HEADS = 8
RW_DIM = 512
DECAY_LORA = 64
AAA_LORA = 64
GATE_LORA = 160
D_FF = 2816
D_IN = 4640
RMS_EPS = 1e-6
RWKV_LN_EPS = 64e-5
NEG_INF = -1e30
SCAN_T = 16
LANES = 128
PACK_ROWS = 256

ADAM_LR = 0.001
ADAM_B1 = 0.9
ADAM_B2 = 0.999
ADAM_EPS = 1e-08
ADAM_WD = 0.01
ADAM_STEP = 10

C_R, C_K, C_V, C_Q = 0, 512, 1024, 1536
C_G1, C_G2 = 2048, 3072
C_DG, C_KA, C_VA, C_DW, C_DA = 4096, 4352, 4480, 4608, 4736
NP = 5120

VMEM_LIMIT = 48 * 1024 * 1024


def _cparams(sem):
    return pltpu.CompilerParams(dimension_semantics=sem, vmem_limit_bytes=VMEM_LIMIT)


def _pick(n, cands):
    for c in cands:
        if n % c == 0:
            return c
    raise ValueError(f"no tile for {n}")


def _mm(a, b, *, ta=False, tb=False, bias=None, residual=None, name):
    m = a.shape[1] if ta else a.shape[0]
    k = a.shape[0] if ta else a.shape[1]
    n = b.shape[0] if tb else b.shape[1]
    assert k == (b.shape[1] if tb else b.shape[0]), (a.shape, b.shape, ta, tb)
    tm = _pick(m, (512, 1408, 256, 128) if ta else (512, 528, 384, 256, 128))
    tn = _pick(n, (512, 1408, 256, 128))
    if k <= 1024:
        tk = k
    else:
        tk = _pick(k, (1024, 1056, 528, 512) if (ta and not tb) else (1024, 1408, 512, 256, 128))
    nk = k // tk
    has_bias = bias is not None
    has_res = residual is not None
    dn = (((0 if ta else 1,), (1 if tb else 0,)), ((), ()))

    def body(*refs):
        a_ref, b_ref = refs[0], refs[1]
        pos = 2
        bias_ref = res_ref = None
        if has_bias:
            bias_ref = refs[pos]
            pos += 1
        if has_res:
            res_ref = refs[pos]
            pos += 1
        o_ref, acc_ref = refs[pos], refs[pos + 1]
        kk = pl.program_id(2)
        part = lax.dot_general(a_ref[...].astype(BF16), b_ref[...].astype(BF16), dn, preferred_element_type=F32)

        def finish(out):
            if has_bias:
                out = out + bias_ref[...]
            if has_res:
                out = out + res_ref[...]
            o_ref[...] = out

        if nk == 1:
            finish(part)
        else:
            @pl.when(kk == 0)
            def _():
                acc_ref[...] = part

            @pl.when((kk > 0) & (kk < nk - 1))
            def _():
                acc_ref[...] += part

            @pl.when(kk == nk - 1)
            def _():
                finish(acc_ref[...] + part)

    in_specs = [
        pl.BlockSpec((tk, tm), lambda i, j, kk: (kk, i)) if ta else pl.BlockSpec((tm, tk), lambda i, j, kk: (i, kk)),
        pl.BlockSpec((tn, tk), lambda i, j, kk: (j, kk)) if tb else pl.BlockSpec((tk, tn), lambda i, j, kk: (kk, j)),
    ]
    args = [a, b]
    if has_bias:
        in_specs.append(pl.BlockSpec((1, tn), lambda i, j, kk: (0, j)))
        args.append(bias)
    if has_res:
        in_specs.append(pl.BlockSpec((tm, tn), lambda i, j, kk: (i, j)))
        args.append(residual)
    return pl.pallas_call(
        body, name=name, grid=(m // tm, n // tn, nk),
        in_specs=in_specs, out_specs=pl.BlockSpec((tm, tn), lambda i, j, kk: (i, j)),
        out_shape=jax.ShapeDtypeStruct((m, n), F32),
        scratch_shapes=[pltpu.VMEM((tm, tn) if nk > 1 else (8, LANES), F32)],
        compiler_params=_cparams(("parallel", "parallel", "arbitrary")),
    )(*args)


def _colsum(x, name):
    m, n = x.shape
    tm = BLOCK

    def body(x_ref, o_ref):
        i = pl.program_id(0)
        s = jnp.sum(x_ref[...], axis=0, keepdims=True)

        @pl.when(i == 0)
        def _():
            o_ref[...] = s

        @pl.when(i > 0)
        def _():
            o_ref[...] += s

    return pl.pallas_call(
        body, name=name, grid=(m // tm,),
        in_specs=[pl.BlockSpec((tm, n), lambda i: (i, 0))],
        out_specs=pl.BlockSpec((1, n), lambda i: (0, 0)),
        out_shape=jax.ShapeDtypeStruct((1, n), F32),
        compiler_params=_cparams(("arbitrary",)),
    )(x)


def _view(arr, width=None, col=0, off=0):
    return (arr, arr.shape[1] if width is None else width, col, off)


def _row_spec(view):
    _, width, col, off = view
    return pl.BlockSpec((BLOCK, width), lambda i, col=col, off=off: (i + off, col))


def _const_spec(arr):
    return pl.BlockSpec(arr.shape, lambda i: (0,) * arr.ndim)


def _rows_fwd(fn, rows, consts, out_widths, *, nblk, name):
    nr, nc = len(rows), len(consts)

    def body(*refs):
        i = pl.program_id(0)
        vals = [r[...] for r in refs[:nr + nc]]
        outs = fn(i, *vals)
        for o_ref, o in zip(refs[nr + nc:], outs):
            o_ref[...] = o

    return pl.pallas_call(
        body, name=name, grid=(nblk,),
        in_specs=[_row_spec(v) for v in rows] + [_const_spec(c) for c in consts],
        out_specs=[pl.BlockSpec((BLOCK, w), lambda i: (i, 0)) for w in out_widths],
        out_shape=[jax.ShapeDtypeStruct((nblk * BLOCK, w), F32) for w in out_widths],
        compiler_params=_cparams(("parallel",)),
    )(*[v[0] for v in rows], *consts)


def _rows_bwd(fn, rows, consts, cts, *, nblk, name, diff_rows, diff_consts, acc=None, fwd_widths=()):
    nr, nc = len(rows), len(consts)
    ct_views = [c for c in cts if c is not None]
    acc = acc or [None] * len(diff_rows)
    acc_views = [a for a in acc if a is not None]
    n_in = nr + nc + len(ct_views) + len(acc_views)
    n_fwd = len(fwd_widths)

    def body(*refs):
        i = pl.program_id(0)
        row_vals = [r[...] for r in refs[:nr]]
        const_vals = [r[...] for r in refs[nr:nr + nc]]
        ct_vals = [r[...] for r in refs[nr + nc:nr + nc + len(ct_views)]]
        acc_vals = [r[...] for r in refs[nr + nc + len(ct_views):n_in]]
        out_refs = refs[n_in:]

        def f(*dargs):
            rv = list(row_vals)
            cv = list(const_vals)
            for pos, idx in enumerate(diff_rows):
                rv[idx] = dargs[pos]
            for pos, idx in enumerate(diff_consts):
                cv[idx] = dargs[len(diff_rows) + pos]
            return tuple(fn(i, *rv, *cv))

        primals = [row_vals[idx] for idx in diff_rows] + [const_vals[idx] for idx in diff_consts]
        outs, pull = jax.vjp(f, *primals)
        full_ct, ci = [], 0
        for o, c in zip(outs, cts):
            if c is None:
                full_ct.append(jnp.zeros_like(o))
            else:
                full_ct.append(ct_vals[ci])
                ci += 1
        grads = pull(tuple(full_ct))
        for o_ref, o in zip(out_refs[:n_fwd], outs):
            o_ref[...] = o
        ai = 0
        for pos in range(len(diff_rows)):
            g = grads[pos]
            if acc[pos] is not None:
                g = g + acc_vals[ai]
                ai += 1
            out_refs[n_fwd + pos][...] = g
        for pos in range(len(diff_consts)):
            g = grads[len(diff_rows) + pos]
            o_ref = out_refs[n_fwd + len(diff_rows) + pos]

            @pl.when(i == 0)
            def _(o_ref=o_ref, g=g):
                o_ref[...] = g

            @pl.when(i > 0)
            def _(o_ref=o_ref, g=g):
                o_ref[...] += g

    out_specs = [pl.BlockSpec((BLOCK, w), lambda i: (i, 0)) for w in fwd_widths]
    out_shape = [jax.ShapeDtypeStruct((nblk * BLOCK, w), F32) for w in fwd_widths]
    for idx in diff_rows:
        out_specs.append(pl.BlockSpec((BLOCK, rows[idx][1]), lambda i: (i, 0)))
        out_shape.append(jax.ShapeDtypeStruct((nblk * BLOCK, rows[idx][1]), F32))
    for idx in diff_consts:
        out_specs.append(_const_spec(consts[idx]))
        out_shape.append(jax.ShapeDtypeStruct(consts[idx].shape, F32))
    return pl.pallas_call(
        body, name=name, grid=(nblk,),
        in_specs=([_row_spec(v) for v in rows] + [_const_spec(c) for c in consts]
                  + [_row_spec(v) for v in ct_views] + [_row_spec(v) for v in acc_views]),
        out_specs=out_specs, out_shape=out_shape,
        compiler_params=_cparams(("arbitrary",)),
    )(*[v[0] for v in rows], *consts, *[v[0] for v in ct_views], *[v[0] for v in acc_views])


def _rms_fn(i, x, g):
    return (x * lax.rsqrt(jnp.mean(x * x, axis=-1, keepdims=True) + RMS_EPS) * g,)


def _sigmoid(x):
    return 1.0 / (1.0 + jnp.exp(-x))


def _softplus(x):
    return jnp.maximum(x, 0.0) + jnp.log(1.0 + jnp.exp(-jnp.abs(x)))


def _head_ones(n=RW_DIM, head=HEAD_DIM):
    r = lax.broadcasted_iota(jnp.int32, (n, n), 0) // head
    c = lax.broadcasted_iota(jnp.int32, (n, n), 1) // head
    return (r == c).astype(F32)


def _head_sum(x, ones):
    return jnp.dot(x, ones, precision=HI, preferred_element_type=F32)


def _rwkv_pre_fn(i, r, k, v, dw, da, dg, r_p, k_p, v_p, dw_p, da_p, dg_p,
                 mix_r, mix_k, mix_v, mix_dw, mix_da, mix_dg, w0, w2, a0, a2, g2, k_k, k_a):
    row = i * BLOCK + lax.broadcasted_iota(jnp.int32, (BLOCK, 1), 0)
    live = row >= PAD_ROWS
    live_prev = row >= PAD_ROWS + 1

    def shift(cur, prev, mix):
        cur = jnp.where(live, cur, 0.0)
        prev = jnp.where(live_prev, prev, 0.0)
        return cur + (prev - cur) * mix

    r = shift(r, r_p, mix_r)
    k = shift(k, k_p, mix_k)
    v = shift(v, v_p, mix_v)
    dw = shift(dw, dw_p, mix_dw)
    da = shift(da, da_p, mix_da)
    dg = shift(dg, dg_p, mix_dg)
    ones = _head_ones()
    wlog = -_softplus(-(w0 + jnp.dot(jnp.tanh(dw), w2, preferred_element_type=F32))) - 0.5
    decay = jnp.exp(-jnp.exp(wlog))
    a = _sigmoid(a0 + jnp.dot(da, a2, preferred_element_type=F32))
    g = jnp.dot(_sigmoid(dg), g2, preferred_element_type=F32)
    kk = k * k_k
    norm_sq = jnp.where(live, _head_sum(kk * kk, ones), 1.0)
    kk = kk / jnp.maximum(jnp.sqrt(norm_sq), 1e-12)
    k_mod = k * (1.0 + (a - 1.0) * k_a)
    return r, decay, k_mod, v, -kk, kk * a, g


def _rwkv_post_fn(i, ys, r, k_mod, v, g, ln_w, ln_b, r_k):
    ones = _head_ones()
    mean = _head_sum(ys, ones) * (1.0 / HEAD_DIM)
    d = ys - mean
    var = _head_sum(d * d, ones) * (1.0 / HEAD_DIM)
    yn = d * lax.rsqrt(var + RWKV_LN_EPS) * ln_w + ln_b
    bonus = _head_sum(r * k_mod * r_k, ones) * v
    return ((yn + bonus) * g,)


def _merge_fn(i, ya, yr, g1, g2):
    return (_sigmoid(g1) * ya + _sigmoid(g2) * yr,)


def _swiglu_fn(i, gate, up):
    return (gate * _sigmoid(gate) * up,)


def _loss_fn(i, h, tgt, g):
    y = h * lax.rsqrt(jnp.mean(h * h, axis=-1, keepdims=True) + RMS_EPS) * g
    err = y - tgt
    return (0.5 * jnp.mean(err * err, axis=-1, keepdims=True),)


def _rope_tables(lp):
    pos = (jnp.arange(lp, dtype=jnp.int32) - PAD_ROWS).astype(F32)
    inv_freq = jnp.power(jnp.float32(ROPE_THETA), -jnp.arange(ROPE_HALF, dtype=F32) * (2.0 / ROPE_DIM))
    ang = pos[:, None] * inv_freq[None, :]
    cos, sin = jnp.cos(ang), jnp.sin(ang)
    one = jnp.ones((lp, HEAD_DIM - ROPE_DIM), F32)
    zero_h = jnp.zeros((lp, ROPE_HALF), F32)
    zero_r = jnp.zeros((lp, HEAD_DIM - ROPE_DIM), F32)
    c = jnp.concatenate([cos, cos, one], axis=1)
    s1 = jnp.concatenate([-sin, zero_h, zero_r], axis=1)
    s2 = jnp.concatenate([zero_h, sin, zero_r], axis=1)
    return tuple(jnp.tile(t, (1, Q_HEADS)) for t in (c, s1, s2))


def _rope_fwd_fn(i, x, c, s1, s2):
    n = x.shape[1]
    return (x * c + pltpu.roll(x, n - ROPE_HALF, 1) * s1 + pltpu.roll(x, ROPE_HALF, 1) * s2,)


def _rope_bwd_fn(i, dy, c, s1, s2):
    n = dy.shape[1]
    return (dy * c + pltpu.roll(dy * s1, ROPE_HALF, 1) + pltpu.roll(dy * s2, n - ROPE_HALF, 1),)


def _attn_mask(i):
    r = lax.broadcasted_iota(jnp.int32, (BLOCK, 3 * BLOCK), 0)
    c = lax.broadcasted_iota(jnp.int32, (BLOCK, 3 * BLOCK), 1)
    meta = (c < BLOCK) & (c >= PAD_ROWS)
    prev = (c >= BLOCK) & (c < 2 * BLOCK) & ((c - BLOCK) > r) & (i >= 1)
    cur = (c >= 2 * BLOCK) & ((c - 2 * BLOCK) <= r)
    return meta | prev | cur


def _attn_fwd(q, k, v, sinks, *, nblk, name):
    scale = HEAD_DIM ** -0.5

    def body(q_ref, km_ref, kp_ref, kc_ref, vm_ref, vp_ref, vc_ref, s_ref, o_ref, lse_ref):
        i = pl.program_id(0)
        valid = _attn_mask(i)
        for h in range(Q_HEADS):
            g = h // GROUP
            kcat = jnp.concatenate([km_ref[g], kp_ref[g], kc_ref[g]], axis=0).astype(BF16)
            vcat = jnp.concatenate([vm_ref[g], vp_ref[g], vc_ref[g]], axis=0).astype(BF16)
            s = lax.dot_general(q_ref[h].astype(BF16), kcat, (((1,), (1,)), ((), ())),
                                preferred_element_type=F32) * scale
            s = jnp.where(valid, s, NEG_INF)
            sink = s_ref[0:1, h:h + 1]
            m = jnp.maximum(jnp.max(s, axis=-1, keepdims=True), sink)
            p = jnp.exp(s - m)
            den = jnp.sum(p, axis=-1, keepdims=True) + jnp.exp(sink - m)
            o = jnp.dot(p.astype(BF16), vcat, preferred_element_type=F32)
            o_ref[h] = o / den
            lse_ref[h] = m + jnp.log(den)

    kv = lambda f: pl.BlockSpec((KV_HEADS, BLOCK, HEAD_DIM), f)
    return pl.pallas_call(
        body, name=name, grid=(nblk,),
        in_specs=[pl.BlockSpec((Q_HEADS, BLOCK, HEAD_DIM), lambda i: (0, i + 1, 0)),
                  kv(lambda i: (0, 0, 0)), kv(lambda i: (0, i, 0)), kv(lambda i: (0, i + 1, 0)),
                  kv(lambda i: (0, 0, 0)), kv(lambda i: (0, i, 0)), kv(lambda i: (0, i + 1, 0)),
                  pl.BlockSpec((1, Q_HEADS), lambda i: (0, 0))],
        out_specs=[pl.BlockSpec((Q_HEADS, BLOCK, HEAD_DIM), lambda i: (0, i, 0)),
                   pl.BlockSpec((Q_HEADS, BLOCK, 1), lambda i: (0, i, 0))],
        out_shape=[jax.ShapeDtypeStruct((Q_HEADS, nblk * BLOCK, HEAD_DIM), F32),
                   jax.ShapeDtypeStruct((Q_HEADS, nblk * BLOCK, 1), F32)],
        compiler_params=_cparams(("parallel",)),
    )(q, k, k, k, v, v, v, sinks)


def _attn_bwd(q, k, v, sinks, o, lse, do, *, nblk, name):
    scale = HEAD_DIM ** -0.5
    lp = k.shape[1]

    def body(q_ref, km_ref, kp_ref, kc_ref, vm_ref, vp_ref, vc_ref, s_ref, o_ref, lse_ref, do_ref,
             dq_ref, dk_ref, dv_ref, ds_ref):
        i = pl.program_id(0)

        @pl.when(i == 0)
        def _():
            dk_ref[...] = jnp.zeros_like(dk_ref)
            dv_ref[...] = jnp.zeros_like(dv_ref)
            ds_ref[...] = jnp.zeros_like(ds_ref)

        valid = _attn_mask(i)
        lane = lax.broadcasted_iota(jnp.int32, (1, Q_HEADS), 1)
        prev_rows = pl.ds(pl.multiple_of(i * BLOCK, BLOCK), BLOCK)
        cur_rows = pl.ds(pl.multiple_of((i + 1) * BLOCK, BLOCK), BLOCK)
        for h in range(Q_HEADS):
            g = h // GROUP
            qh = q_ref[h].astype(BF16)
            doh = do_ref[h]
            kcat = jnp.concatenate([km_ref[g], kp_ref[g], kc_ref[g]], axis=0).astype(BF16)
            vcat = jnp.concatenate([vm_ref[g], vp_ref[g], vc_ref[g]], axis=0).astype(BF16)
            s = lax.dot_general(qh, kcat, (((1,), (1,)), ((), ())), preferred_element_type=F32) * scale
            s = jnp.where(valid, s, NEG_INF)
            lse_h = lse_ref[h]
            p = jnp.exp(s - lse_h)
            delta = jnp.sum(doh * o_ref[h], axis=-1, keepdims=True)
            dp = lax.dot_general(doh.astype(BF16), vcat, (((1,), (1,)), ((), ())), preferred_element_type=F32)
            dsc = (p * (dp - delta) * scale).astype(BF16)
            dq_ref[h] = jnp.dot(dsc, kcat, preferred_element_type=F32)
            dk_all = lax.dot_general(dsc, qh, (((0,), (0,)), ((), ())), preferred_element_type=F32)
            dv_all = lax.dot_general(p.astype(BF16), doh.astype(BF16), (((0,), (0,)), ((), ())),
                                     preferred_element_type=F32)
            dk_ref[g, 0:BLOCK, :] += dk_all[0:BLOCK]
            dk_ref[g, prev_rows, :] += dk_all[BLOCK:2 * BLOCK]
            dk_ref[g, cur_rows, :] += dk_all[2 * BLOCK:]
            dv_ref[g, 0:BLOCK, :] += dv_all[0:BLOCK]
            dv_ref[g, prev_rows, :] += dv_all[BLOCK:2 * BLOCK]
            dv_ref[g, cur_rows, :] += dv_all[2 * BLOCK:]
            p_sink = jnp.exp(s_ref[0:1, h:h + 1] - lse_h)
            dsink = -jnp.sum(p_sink * delta, axis=0, keepdims=True)
            ds_ref[...] += jnp.where(lane == h, dsink, 0.0)

    kv = lambda f: pl.BlockSpec((KV_HEADS, BLOCK, HEAD_DIM), f)
    qspec = pl.BlockSpec((Q_HEADS, BLOCK, HEAD_DIM), lambda i: (0, i, 0))
    whole = pl.BlockSpec((KV_HEADS, lp, HEAD_DIM), lambda i: (0, 0, 0))
    return pl.pallas_call(
        body, name=name, grid=(nblk,),
        in_specs=[pl.BlockSpec((Q_HEADS, BLOCK, HEAD_DIM), lambda i: (0, i + 1, 0)),
                  kv(lambda i: (0, 0, 0)), kv(lambda i: (0, i, 0)), kv(lambda i: (0, i + 1, 0)),
                  kv(lambda i: (0, 0, 0)), kv(lambda i: (0, i, 0)), kv(lambda i: (0, i + 1, 0)),
                  pl.BlockSpec((1, Q_HEADS), lambda i: (0, 0)),
                  qspec, pl.BlockSpec((Q_HEADS, BLOCK, 1), lambda i: (0, i, 0)), qspec],
        out_specs=[qspec, whole, whole, pl.BlockSpec((1, Q_HEADS), lambda i: (0, 0))],
        out_shape=[jax.ShapeDtypeStruct((Q_HEADS, nblk * BLOCK, HEAD_DIM), F32),
                   jax.ShapeDtypeStruct((KV_HEADS, lp, HEAD_DIM), F32),
                   jax.ShapeDtypeStruct((KV_HEADS, lp, HEAD_DIM), F32),
                   jax.ShapeDtypeStruct((1, Q_HEADS), F32)],
        compiler_params=_cparams(("arbitrary",)),
    )(q, k, k, k, v, v, v, sinks, o, lse, do)


N_VEC = 5
VEC_ROWS = N_VEC * HEAD_DIM


def _selectors():
    t = SCAN_T
    shape = (t, 2 * LANES, RW_DIM)
    step, src, dst = [lax.broadcasted_iota(jnp.int32, shape, d) for d in range(3)]
    src = src % LANES
    spread = ((src // t == dst // HEAD_DIM) & (src % t == step)).astype(BF16)
    shape = (t, RW_DIM, LANES)
    step, src, dst = [lax.broadcasted_iota(jnp.int32, shape, d) for d in range(3)]
    collect = ((src // HEAD_DIM == dst // t) & (dst % t == step)).astype(BF16)
    return spread, collect


def _split2(x):
    hi = x.astype(BF16)
    lo = (x - hi.astype(F32)).astype(BF16)
    return jnp.concatenate([hi, lo], axis=1)


def _wkv_fwd(xt, v, spread, name):
    nch = xt.shape[0]
    t_steps = SCAN_T

    def body(xt_ref, v_ref, sel_ref, y_ref, hist_ref, st_ref):
        @pl.when(pl.program_id(0) == 0)
        def _():
            st_ref[...] = jnp.zeros_like(st_ref)

        x2 = _split2(xt_ref[0])
        st = st_ref[...]
        for j in range(t_steps):
            cols = jnp.dot(x2, sel_ref[j], preferred_element_type=F32)
            a_c, w_c, b_c, k_c, r_c = [cols[n * HEAD_DIM:(n + 1) * HEAD_DIM] for n in range(N_VEC)]
            hist_ref[j] = st
            sa = jnp.sum(st * a_c, axis=0, keepdims=True)
            st = st * w_c + b_c * sa + k_c * v_ref[j:j + 1, :]
            y_ref[j:j + 1, :] = jnp.sum(st * r_c, axis=0, keepdims=True)
        st_ref[...] = st

    return pl.pallas_call(
        body, name=name, grid=(nch,),
        in_specs=[pl.BlockSpec((1, VEC_ROWS, LANES), lambda c: (c, 0, 0)),
                  pl.BlockSpec((t_steps, RW_DIM), lambda c: (c, 0)),
                  pl.BlockSpec(spread.shape, lambda c: (0, 0, 0))],
        out_specs=[pl.BlockSpec((t_steps, RW_DIM), lambda c: (c, 0)),
                   pl.BlockSpec((t_steps, HEAD_DIM, RW_DIM), lambda c: (c, 0, 0))],
        out_shape=[jax.ShapeDtypeStruct((nch * t_steps, RW_DIM), F32),
                   jax.ShapeDtypeStruct((nch * t_steps, HEAD_DIM, RW_DIM), F32)],
        scratch_shapes=[pltpu.VMEM((HEAD_DIM, RW_DIM), F32)],
        compiler_params=_cparams(("arbitrary",)),
    )(xt, v, spread)


def _wkv_bwd(xt, v, hist, dy, spread, collect, name):
    nch = xt.shape[0]
    t_steps = SCAN_T

    def body(xt_ref, v_ref, hist_ref, dy_ref, sel_ref, col_ref, dxt_ref, dv_ref, g_ref):
        @pl.when(pl.program_id(0) == 0)
        def _():
            g_ref[...] = jnp.zeros_like(g_ref)

        x2 = _split2(xt_ref[0])
        gst = g_ref[...]
        acc = jnp.zeros((VEC_ROWS, LANES), F32)
        nxt = None
        for j in reversed(range(t_steps)):
            cols = jnp.dot(x2, sel_ref[j], preferred_element_type=F32)
            a_c, w_c, b_c, k_c, r_c = [cols[n * HEAD_DIM:(n + 1) * HEAD_DIM] for n in range(N_VEC)]
            prev = hist_ref[j]
            v_row = v_ref[j:j + 1, :]
            dy_row = dy_ref[j:j + 1, :]
            sa = jnp.sum(prev * a_c, axis=0, keepdims=True)
            if nxt is None:
                nxt = prev * w_c + b_c * sa + k_c * v_row
            gst = gst + r_c * dy_row
            dv_ref[j:j + 1, :] = jnp.sum(gst * k_c, axis=0, keepdims=True)
            dsa = jnp.sum(gst * b_c, axis=0, keepdims=True)
            prods = jnp.concatenate([p.astype(BF16) for p in
                                     (prev * dsa, gst * prev, gst * sa, gst * v_row, nxt * dy_row)], axis=0)
            acc = acc + jnp.dot(prods, col_ref[j], preferred_element_type=F32)
            gst = gst * w_c + a_c * dsa
            nxt = prev
        dxt_ref[0] = acc
        g_ref[...] = gst

    rev3 = lambda c: (nch - 1 - c, 0, 0)
    rev2 = lambda c: (nch - 1 - c, 0)
    rowspec = pl.BlockSpec((t_steps, RW_DIM), rev2)
    return pl.pallas_call(
        body, name=name, grid=(nch,),
        in_specs=[pl.BlockSpec((1, VEC_ROWS, LANES), rev3), rowspec,
                  pl.BlockSpec((t_steps, HEAD_DIM, RW_DIM), rev3), rowspec,
                  pl.BlockSpec(spread.shape, lambda c: (0, 0, 0)), pl.BlockSpec(collect.shape, lambda c: (0, 0, 0))],
        out_specs=[pl.BlockSpec((1, VEC_ROWS, LANES), rev3), rowspec],
        out_shape=[jax.ShapeDtypeStruct((nch, VEC_ROWS, LANES), F32),
                   jax.ShapeDtypeStruct((nch * t_steps, RW_DIM), F32)],
        scratch_shapes=[pltpu.VMEM((HEAD_DIM, RW_DIM), F32)],
        compiler_params=_cparams(("arbitrary",)),
    )(xt, v, hist, dy, spread, collect)


def _to_xt(vecs, nch):
    parts = [x.reshape(nch, SCAN_T, RW_HEADS, HEAD_DIM).transpose(0, 3, 2, 1).reshape(nch, HEAD_DIM, LANES)
             for x in vecs]
    return jnp.concatenate(parts, axis=1)


def _from_xt(xt):
    nch = xt.shape[0]
    x = xt.reshape(nch, N_VEC, HEAD_DIM, RW_HEADS, SCAN_T).transpose(1, 0, 4, 3, 2)
    return [x[n].reshape(nch * SCAN_T, RW_DIM) for n in range(N_VEC)]


MESH = pl.DeviceIdType.MESH
ANY = pl.BlockSpec(memory_space=pl.ANY)


def _all_gather(arrays, name):
    n_arr = len(arrays)
    per = N_DEV - 1

    def body(*refs):
        x_refs, out_refs = refs[:n_arr], refs[n_arr:2 * n_arr]
        send_sems, recv_sems, local_sems = refs[2 * n_arr:]
        xi, yi, ci = lax.axis_index("x"), lax.axis_index("y"), lax.axis_index("c")
        me, sibling = (xi, yi, ci), (xi, yi, 1 - ci)
        chips = [(1 - xi, yi), (xi, 1 - yi), (1 - xi, 1 - yi)]

        def slot(a, px, py, pc):
            return out_refs[a].at[4 * px + 2 * py + pc]

        def copy(a, sem, block, to, src=None):
            return pltpu.make_async_remote_copy(
                src_ref=slot(a, *block) if src is None else src, dst_ref=slot(a, *block),
                send_sem=send_sems.at[per * a + sem], recv_sem=recv_sems.at[per * a + sem],
                device_id=to, device_id_type=MESH)

        mine = [pltpu.make_async_copy(x_refs[a], slot(a, *me), local_sems.at[a]) for a in range(n_arr)]
        for cp in mine:
            cp.start()
        sent = []
        for a in range(n_arr):
            sent.append(copy(a, 0, me, sibling, src=x_refs[a]))
            sent += [copy(a, 1 + j, me, (*chip, ci), src=x_refs[a]) for j, chip in enumerate(chips)]
        for cp in sent:
            cp.start()
        for j, chip in enumerate(chips):
            for a in range(n_arr):
                copy(a, 1 + j, (*chip, ci), me).wait_recv()
                onward = copy(a, 4 + j, (*chip, ci), sibling)
                onward.start()
                sent.append(onward)
        for a in range(n_arr):
            copy(a, 0, sibling, me).wait_recv()
        for j, chip in enumerate(chips):
            for a in range(n_arr):
                copy(a, 4 + j, (*chip, 1 - ci), me).wait_recv()
        for cp in sent:
            cp.wait_send()
        for cp in mine:
            cp.wait()

    sems = pltpu.SemaphoreType.DMA((per * n_arr,))
    return pl.pallas_call(
        body, name=name, out_shape=[jax.ShapeDtypeStruct((N_DEV,) + a.shape, a.dtype) for a in arrays],
        in_specs=[ANY] * n_arr, out_specs=[ANY] * n_arr,
        scratch_shapes=[sems, sems, pltpu.SemaphoreType.DMA((n_arr,))],
    )(*arrays)


def _exchange(scattered, shared, name):
    n_sc, n_sh = len(scattered), len(shared)
    n_arr = n_sc + n_sh
    per = N_DEV - 1

    def body(*refs):
        in_refs, out_refs = refs[:n_arr], refs[n_arr:2 * n_arr]
        send_sems, recv_sems, local_sems = refs[2 * n_arr:]
        xi, yi, ci = lax.axis_index("x"), lax.axis_index("y"), lax.axis_index("c")
        me = 4 * xi + 2 * yi + ci
        src_of = lambda a, peer: in_refs[a].at[peer] if a < n_sc else in_refs[a]
        copies = []
        for d in range(1, N_DEV):
            px = 1 - xi if d & 4 else xi
            py = 1 - yi if d & 2 else yi
            pc = 1 - ci if d & 1 else ci
            for a in range(n_arr):
                copies.append(pltpu.make_async_remote_copy(
                    src_ref=src_of(a, 4 * px + 2 * py + pc), dst_ref=out_refs[a].at[me],
                    send_sem=send_sems.at[per * a + d - 1], recv_sem=recv_sems.at[per * a + d - 1],
                    device_id=(px, py, pc), device_id_type=MESH))
        own = [pltpu.make_async_copy(src_of(a, me), out_refs[a].at[me], local_sems.at[a]) for a in range(n_arr)]
        for cp in copies + own:
            cp.start()
        for cp in copies + own:
            cp.wait()

    sems = pltpu.SemaphoreType.DMA((per * n_arr,))
    return pl.pallas_call(
        body, name=name,
        out_shape=([jax.ShapeDtypeStruct(a.shape, a.dtype) for a in scattered]
                   + [jax.ShapeDtypeStruct((N_DEV,) + a.shape, a.dtype) for a in shared]),
        in_specs=[ANY] * n_arr, out_specs=[ANY] * n_arr,
        scratch_shapes=[sems, sems, pltpu.SemaphoreType.DMA((n_arr,))],
    )(*scattered, *shared)


def _adamw(parts, w, m, v, name):
    rows, cols = w.shape
    tile = _pick(rows, (PACK_ROWS, 104, 8))

    def body(p_ref, w_ref, m_ref, v_ref, g_out, d_out, m_out, v_out):
        g = p_ref[0].astype(F32)
        for s in range(1, N_DEV):
            g = g + p_ref[s].astype(F32)
        m_new = ADAM_B1 * m_ref[...] + (1.0 - ADAM_B1) * g
        v_new = ADAM_B2 * v_ref[...] + (1.0 - ADAM_B2) * (g * g)
        m_hat = m_new / (1.0 - ADAM_B1 ** ADAM_STEP)
        v_hat = v_new / (1.0 - ADAM_B2 ** ADAM_STEP)
        g_out[...] = g
        d_out[...] = -ADAM_LR * (m_hat / (jnp.sqrt(v_hat) + ADAM_EPS) + ADAM_WD * w_ref[...])
        m_out[...] = m_new
        v_out[...] = v_new

    spec = pl.BlockSpec((tile, cols), lambda i: (i, 0))
    return pl.pallas_call(
        body, name=name, grid=(rows // tile,),
        in_specs=[pl.BlockSpec((N_DEV, tile, cols), lambda i: (0, i, 0)), spec, spec, spec],
        out_specs=[spec] * 4, out_shape=[jax.ShapeDtypeStruct((rows, cols), F32)] * 4,
        compiler_params=_cparams(("parallel",)),
    )(parts, w, m, v)


SHARDED = [
    ("meta_tokens", 1), ("w_in", 1), ("rwkv_w2", 1), ("rwkv_a2", 1), ("rwkv_g2", 1), ("w_br_attn", 1),
    ("w_br_rwkv", 1), ("w_o", 0), ("w_ffn_gate", 1), ("w_ffn_up", 1), ("w_ffn_down", 0)]
PACKED = [s for s in SHARDED if s[0] != "w_in"]
REPLICATED = ["norm_mix_g", "b_in", "attn_sinks", "rwkv_mix", "rwkv_w0", "rwkv_a0", "rwkv_k_k", "rwkv_k_a",
              "rwkv_r_k", "rwkv_ln_w", "rwkv_ln_b", "norm_ffn_g", "norm_final_g"]
WEIGHTS = ["meta_tokens", "norm_mix_g", "w_in", "b_in", "attn_sinks", "rwkv_mix", "rwkv_w0", "rwkv_w2", "rwkv_a0",
           "rwkv_a2", "rwkv_g2", "rwkv_k_k", "rwkv_k_a", "rwkv_r_k", "rwkv_ln_w", "rwkv_ln_b", "w_br_attn",
           "w_br_rwkv", "w_o", "norm_ffn_g", "w_ffn_gate", "w_ffn_up", "w_ffn_down", "norm_final_g"]


def _pack(arrays, row_multiple):
    flat = jnp.concatenate([a.reshape(-1) for a in arrays])
    per = row_multiple * LANES
    total = -(-flat.shape[0] // per) * per
    return jnp.pad(flat, (0, total - flat.shape[0])).reshape(-1, LANES)


def _unpack(buf, shapes):
    flat = buf.reshape(-1)
    out, pos = [], 0
    for s in shapes:
        n = 1
        for d in s:
            n *= d
        out.append(flat[pos:pos + n].reshape(s))
        pos += n
    return out


def _strip(name, a):
    return a if name in ("meta_tokens", "norm_final_g") else a[0]


def _shard_offsets(shapes):
    offs, pos = [], 0
    for r, c in shapes:
        offs.append(pos)
        pos += r * c
    return offs


def _assemble(gathered, shapes):
    flat = gathered.reshape(N_DEV, -1)
    out = {}
    for (name, axis), (r, c), off in zip(PACKED, shapes, _shard_offsets(shapes)):
        piece = flat[:, off:off + r * c]
        if axis == 0:
            out[name] = piece.reshape(N_DEV * r, c)
        else:
            out[name] = piece.reshape(N_DEV, r, c).transpose(1, 0, 2).reshape(r, N_DEV * c)
    return out


def _disassemble(grads, shapes, rows):
    pieces = []
    for (name, axis), (r, c) in zip(PACKED, shapes):
        g = grads[name]
        if axis == 0:
            pieces.append(g.reshape(N_DEV, r * c))
        else:
            pieces.append(g.reshape(r, N_DEV, c).transpose(1, 0, 2).reshape(N_DEV, r * c))
    flat = jnp.concatenate(pieces, axis=1)
    return jnp.pad(flat, ((0, 0), (0, rows * LANES - flat.shape[1]))).reshape(N_DEV, rows, LANES)


W_IN_LAYOUT = [(768, 2304), (0, 512), (2592, 4640), (2432, 2592), 256 - GATE_LORA, (512, 768), (2304, 2368),
               128 - DECAY_LORA, (2368, 2432), 128 - AAA_LORA, NP - C_DA - 128]


def _w_in_padded(w, shard_width=None):
    rows = w.shape[-2]
    width = D_IN if shard_width is None else shard_width
    parts = []
    for seg in W_IN_LAYOUT:
        if isinstance(seg, int):
            parts.append(jnp.zeros((rows, seg), w.dtype))
            continue
        lo, stop = seg
        while lo < stop:
            p = lo // width
            hi = min(stop, (p + 1) * width)
            src = w if shard_width is None else w[p]
            parts.append(src[:, lo - p * width:hi - p * width])
            lo = hi
    return jnp.concatenate(parts, axis=1)


def _w_in_unpadded(wp, lo=0, stop=D_IN):
    spans, pos = [], 0
    for seg in W_IN_LAYOUT:
        if isinstance(seg, int):
            pos += seg
        else:
            spans.append((seg[0], seg[1], pos))
            pos += seg[1] - seg[0]
    parts = []
    for a, b, at in sorted(spans):
        c, d = max(a, lo), min(b, stop)
        if c < d:
            parts.append(wp[:, at + c - a:at + d - a])
    return jnp.concatenate(parts, axis=1)


def _pad_rows(a, n):
    return jnp.pad(a, ((0, n - a.shape[0]), (0, 0)))


def _device_step(x, tgt, full):
    seq = x.shape[0]
    nblk = seq // BLOCK
    lp = seq + BLOCK
    nall = nblk + 1
    nch = lp // SCAN_T

    w_in_p = full["w_in_p"]
    b_in_p = _w_in_padded(full["b_in"][None])
    mix = full["rwkv_mix"][None]
    mix_r, mix_k, mix_v = mix[:, 0:512], mix[:, 512:1024], mix[:, 1024:1536]
    mix_dw = jnp.pad(mix[:, 1536:1600], ((0, 0), (0, 64)))
    mix_da = jnp.pad(mix[:, 1600:1664], ((0, 0), (0, 64)))
    mix_dg = jnp.pad(mix[:, 1664:1824], ((0, 0), (0, 96)))
    w2_p = _pad_rows(full["rwkv_w2"].astype(F32), 128)
    a2_p = _pad_rows(full["rwkv_a2"].astype(F32), 128)
    g2_p = _pad_rows(full["rwkv_g2"].astype(F32), 256)
    row = lambda name: full[name].reshape(1, -1)
    sinks = row("attn_sinks")
    rope_c, rope_s1, rope_s2 = _rope_tables(lp)

    hpad = jnp.concatenate([jnp.zeros((PAD_ROWS, D_MODEL), F32), full["meta_tokens"].astype(F32), x], axis=0)
    (u,) = _rows_fwd(_rms_fn, [_view(hpad)], [row("norm_mix_g")], [D_MODEL], nblk=nall, name="norm_mix")
    proj = _mm(u, w_in_p, bias=b_in_p, name="in_proj")
    (q_r,) = _rows_fwd(_rope_fwd_fn, [_view(proj, 512, C_Q // 512), _view(rope_c), _view(rope_s1), _view(rope_s2)],
                       [], [512], nblk=nall, name="rope_q")
    (k_r,) = _rows_fwd(_rope_fwd_fn, [_view(proj, 128, C_KA // 128), _view(rope_c, 128), _view(rope_s1, 128),
                                      _view(rope_s2, 128)], [], [128], nblk=nall, name="rope_k")
    heads = lambda a, nh: a.reshape(a.shape[0], nh, HEAD_DIM).transpose(1, 0, 2)
    unheads = lambda a: a.transpose(1, 0, 2).reshape(a.shape[1], -1)
    q_h, k_h = heads(q_r, Q_HEADS), heads(k_r, KV_HEADS)
    v_h = heads(proj[:, C_VA:C_VA + 128], KV_HEADS)
    o_h, lse = _attn_fwd(q_h, k_h, v_h, sinks, nblk=nblk, name="attn_fwd")
    y_attn = unheads(o_h)

    rw_cols = jnp.concatenate([proj[:, C_R:C_R + 1536], proj[:, C_DG:C_DG + 256], proj[:, C_DW:C_DW + 256]], axis=1)
    rw_prev = jnp.pad(rw_cols[:-1], ((1, 0), (0, 0)))
    pre_rows = [_view(proj, 512, 0), _view(proj, 512, 1), _view(proj, 512, 2), _view(proj, 128, C_DW // 128),
                _view(proj, 128, C_DA // 128), _view(proj, 256, C_DG // 256),
                _view(rw_prev, 512, 0), _view(rw_prev, 512, 1), _view(rw_prev, 512, 2), _view(rw_prev, 128, 14),
                _view(rw_prev, 128, 15), _view(rw_prev, 256, 6)]
    pre_consts = [mix_r, mix_k, mix_v, mix_dw, mix_da, mix_dg, row("rwkv_w0"), w2_p, row("rwkv_a0"), a2_p, g2_p,
                  row("rwkv_k_k"), row("rwkv_k_a")]
    r_t, decay, k_mod, v_t, a_neg, b_t, gate = _rows_fwd(_rwkv_pre_fn, pre_rows, pre_consts, [RW_DIM] * 7,
                                                         nblk=nall, name="rwkv_pre")
    spread, collect = _selectors()
    xt = _to_xt((a_neg, decay, b_t, k_mod, r_t), nch)
    y_scan, hist = _wkv_fwd(xt, v_t, spread, name="wkv_fwd")
    post_rows = [_view(y_scan, off=1), _view(r_t, off=1), _view(k_mod, off=1), _view(v_t, off=1), _view(gate, off=1)]
    post_consts = [row("rwkv_ln_w"), row("rwkv_ln_b"), row("rwkv_r_k")]
    (y_rwkv,) = _rows_fwd(_rwkv_post_fn, post_rows, post_consts, [RW_DIM], nblk=nblk, name="rwkv_post")

    ya = _mm(y_attn, full["w_br_attn"], name="br_attn")
    yr = _mm(y_rwkv, full["w_br_rwkv"], name="br_rwkv")
    merge_rows = [_view(ya), _view(yr), _view(proj, 1024, C_G1 // 1024, 1), _view(proj, 1024, C_G2 // 1024, 1)]
    (merged,) = _rows_fwd(_merge_fn, merge_rows, [], [D_MODEL], nblk=nblk, name="merge")
    h1 = _mm(merged, full["w_o"], residual=x, name="out_proj")
    (f,) = _rows_fwd(_rms_fn, [_view(h1)], [row("norm_ffn_g")], [D_MODEL], nblk=nblk, name="norm_ffn")
    ff_gate = _mm(f, full["w_ffn_gate"], name="ffn_gate")
    ff_up = _mm(f, full["w_ffn_up"], name="ffn_up")
    (act,) = _rows_fwd(_swiglu_fn, [_view(ff_gate), _view(ff_up)], [], [D_FF], nblk=nblk, name="swiglu")
    h2 = _mm(act, full["w_ffn_down"], residual=h1, name="ffn_down")

    grads = {}
    ones_col = jnp.ones((seq, 1), F32)
    loss_rows, dh2, grads["norm_final_g"] = _rows_bwd(
        _loss_fn, [_view(h2), _view(tgt)], [row("norm_final_g")], [_view(ones_col)], nblk=nblk, name="loss",
        diff_rows=[0], diff_consts=[0], fwd_widths=[1])
    loss = jnp.sum(loss_rows)

    dact = _mm(dh2, full["w_ffn_down"], tb=True, name="d_act")
    grads["w_ffn_down"] = _mm(act, dh2, ta=True, name="dw_ffn_down")
    dgate, dup = _rows_bwd(_swiglu_fn, [_view(ff_gate), _view(ff_up)], [], [_view(dact)], nblk=nblk,
                           name="swiglu_bwd", diff_rows=[0, 1], diff_consts=[])
    grads["w_ffn_gate"] = _mm(f, dgate, ta=True, name="dw_ffn_gate")
    grads["w_ffn_up"] = _mm(f, dup, ta=True, name="dw_ffn_up")
    df = _mm(dgate, full["w_ffn_gate"], tb=True, name="df_gate")
    df = _mm(dup, full["w_ffn_up"], tb=True, residual=df, name="df_up")
    dh1, grads["norm_ffn_g"] = _rows_bwd(_rms_fn, [_view(h1)], [row("norm_ffn_g")], [_view(df)], nblk=nblk,
                                         name="norm_ffn_bwd", diff_rows=[0], diff_consts=[0], acc=[_view(dh2)])
    dmerged = _mm(dh1, full["w_o"], tb=True, name="d_merged")
    grads["w_o"] = _mm(merged, dh1, ta=True, name="dw_o")
    dya, dyr, dg1, dg2 = _rows_bwd(_merge_fn, merge_rows, [], [_view(dmerged)], nblk=nblk, name="merge_bwd",
                                   diff_rows=[0, 1, 2, 3], diff_consts=[])
    grads["w_br_attn"] = _mm(y_attn, dya, ta=True, name="dw_br_attn")
    grads["w_br_rwkv"] = _mm(y_rwkv, dyr, ta=True, name="dw_br_rwkv")
    dy_attn = _mm(dya, full["w_br_attn"], tb=True, name="d_y_attn")
    dy_rwkv = _mm(dyr, full["w_br_rwkv"], tb=True, name="d_y_rwkv")

    post = _rows_bwd(_rwkv_post_fn, post_rows, post_consts, [_view(dy_rwkv)], nblk=nblk, name="rwkv_post_bwd",
                     diff_rows=[0, 1, 2, 3, 4], diff_consts=[0, 1, 2])
    dys, dr_post, dk_post, dv_post, dgate_post = [jnp.pad(t, ((BLOCK, 0), (0, 0))) for t in post[:5]]
    grads["rwkv_ln_w"], grads["rwkv_ln_b"], grads["rwkv_r_k"] = post[5:]
    dxt, dv_s = _wkv_bwd(xt, v_t, hist, dys, spread, collect, name="wkv_bwd")
    da_s, dw_s, db_s, dk_s, dr_s = _from_xt(dxt)
    pre_cts = [_view(dr_s + dr_post), _view(dw_s), _view(dk_s + dk_post), _view(dv_s + dv_post), _view(da_s),
               _view(db_s), _view(dgate_post)]
    pre = _rows_bwd(_rwkv_pre_fn, pre_rows, pre_consts, pre_cts, nblk=nall, name="rwkv_pre_bwd",
                    diff_rows=list(range(12)), diff_consts=list(range(13)))
    d_cur, d_prev, d_par = pre[0:6], pre[6:12], pre[12:]
    up = lambda t: jnp.pad(t[1:], ((0, 1), (0, 0)))
    d_rw = [c + up(p) for c, p in zip(d_cur, d_prev)]
    grads["rwkv_mix"] = jnp.concatenate([d_par[0], d_par[1], d_par[2], d_par[3][:, :DECAY_LORA],
                                         d_par[4][:, :AAA_LORA], d_par[5][:, :GATE_LORA]], axis=1)
    grads["rwkv_w0"], grads["rwkv_w2"] = d_par[6], d_par[7][:DECAY_LORA]
    grads["rwkv_a0"], grads["rwkv_a2"] = d_par[8], d_par[9][:AAA_LORA]
    grads["rwkv_g2"] = d_par[10][:GATE_LORA]
    grads["rwkv_k_k"], grads["rwkv_k_a"] = d_par[11], d_par[12]

    do_h = heads(dy_attn, Q_HEADS)
    dq_h, dk_h, dv_h, grads["attn_sinks"] = _attn_bwd(q_h, k_h, v_h, sinks, o_h, lse, do_h, nblk=nblk,
                                                      name="attn_bwd")
    dq_r = jnp.pad(unheads(dq_h), ((BLOCK, 0), (0, 0)))
    (dq,) = _rows_fwd(_rope_bwd_fn, [_view(dq_r), _view(rope_c), _view(rope_s1), _view(rope_s2)], [], [512],
                      nblk=nall, name="rope_q_bwd")
    (dka,) = _rows_fwd(_rope_bwd_fn, [_view(unheads(dk_h)), _view(rope_c, 128), _view(rope_s1, 128),
                                      _view(rope_s2, 128)], [], [128], nblk=nall, name="rope_k_bwd")
    dva = unheads(dv_h)

    lead = lambda t: jnp.pad(t, ((BLOCK, 0), (0, 0)))
    dproj = jnp.concatenate([d_rw[0], d_rw[1], d_rw[2], dq, lead(dg1), lead(dg2), d_rw[5], dka, dva, d_rw[3],
                             d_rw[4], jnp.zeros((lp, NP - C_DA - 128), F32)], axis=1)
    grads["w_in_p"] = _mm(u, dproj, ta=True, name="dw_in")
    grads["b_in"] = _w_in_unpadded(_colsum(dproj, name="db_in"))
    du = _mm(dproj, w_in_p, tb=True, name="d_u")
    dh, grads["norm_mix_g"] = _rows_bwd(_rms_fn, [_view(hpad)], [row("norm_mix_g")], [_view(du)], nblk=nall,
                                        name="norm_mix_bwd", diff_rows=[0], diff_consts=[0], acc=[_view(lead(dh1))])
    grads["meta_tokens"] = dh[PAD_ROWS:BLOCK]
    return loss, dh[BLOCK:], grads


def kernel(x, meta_tokens, norm_mix_g, w_in, b_in, attn_sinks, rwkv_mix, rwkv_w0, rwkv_w2, rwkv_a0, rwkv_a2, rwkv_g2, rwkv_k_k, rwkv_k_a, rwkv_r_k, rwkv_ln_w, rwkv_ln_b, w_br_attn, w_br_rwkv, w_o, norm_ffn_g, w_ffn_gate, w_ffn_up, w_ffn_down, norm_final_g, loss_target, m_meta_tokens, m_norm_mix_g, m_w_in, m_b_in, m_attn_sinks, m_rwkv_mix, m_rwkv_w0, m_rwkv_w2, m_rwkv_a0, m_rwkv_a2, m_rwkv_g2, m_rwkv_k_k, m_rwkv_k_a, m_rwkv_r_k, m_rwkv_ln_w, m_rwkv_ln_b, m_w_br_attn, m_w_br_rwkv, m_w_o, m_norm_ffn_g, m_w_ffn_gate, m_w_ffn_up, m_w_ffn_down, m_norm_final_g, v_meta_tokens, v_norm_mix_g, v_w_in, v_b_in, v_attn_sinks, v_rwkv_mix, v_rwkv_w0, v_rwkv_w2, v_rwkv_a0, v_rwkv_a2, v_rwkv_g2, v_rwkv_k_k, v_rwkv_k_a, v_rwkv_r_k, v_rwkv_ln_w, v_rwkv_ln_b, v_w_br_attn, v_w_br_rwkv, v_w_o, v_norm_ffn_g, v_w_ffn_gate, v_w_ffn_up, v_w_ffn_down, v_norm_final_g):
    given = dict(locals())
    wts = {n: _strip(n, given[n]) for n in WEIGHTS}
    mom = {n: _strip(n, given["m_" + n]) for n in WEIGHTS}
    var = {n: _strip(n, given["v_" + n]) for n in WEIGHTS}
    shard_shapes = [wts[n].shape for n, _ in PACKED]
    small_shapes = [wts[n].shape for n in REPLICATED]
    width = wts["w_in"].shape[1]

    w_pack = _pack([wts[n] for n, _ in PACKED], PACK_ROWS)
    w_in_all, gathered = _all_gather([wts["w_in"].astype(BF16), w_pack.astype(BF16)], name="gather_weights")
    full = {n: wts[n] for n in REPLICATED}
    full.update(_assemble(gathered, shard_shapes))
    full["w_in_p"] = _w_in_padded(w_in_all, shard_width=width)

    loss_part, grad_x, grads = _device_step(x[0], loss_target[0], full)

    g_w_in = jnp.stack([_w_in_unpadded(grads["w_in_p"], p * width, (p + 1) * width) for p in range(N_DEV)])
    g_big = _disassemble(grads, shard_shapes, w_pack.shape[0])
    zero = jnp.zeros((1,), F32)
    g_small = _pack([grads[n].reshape(wts[n].shape) for n in REPLICATED] + [loss_part.reshape(1)], 8)
    parts_w_in, parts_big, parts_small = _exchange([g_w_in.astype(BF16), g_big.astype(BF16)], [g_small],
                                                   name="exchange_grads")
    w_in_out = _adamw(parts_w_in, wts["w_in"], mom["w_in"], var["w_in"], name="adamw_w_in")
    big = _adamw(parts_big, w_pack, _pack([mom[n] for n, _ in PACKED], PACK_ROWS),
                 _pack([var[n] for n, _ in PACKED], PACK_ROWS), name="adamw_sharded")
    small = _adamw(parts_small, _pack([wts[n] for n in REPLICATED] + [zero], 8),
                   _pack([mom[n] for n in REPLICATED] + [zero], 8), _pack([var[n] for n in REPLICATED] + [zero], 8),
                   name="adamw_replicated")

    results = [{}, {}, {}, {}]
    for kind in range(4):
        results[kind]["w_in"] = w_in_out[kind]
        for (n, _), a in zip(PACKED, _unpack(big[kind], shard_shapes)):
            results[kind][n] = a
        for n, a in zip(REPLICATED, _unpack(small[kind], small_shapes)):
            results[kind][n] = a
    loss = _unpack(small[0], small_shapes + [(1,)])[-1][0]
    out = [loss, grad_x[None]]
    for kind in range(4):
        out += [results[kind][n].reshape(given[n].shape) for n in WEIGHTS]
    return tuple(out)
```

```python
import functools

import jax
import jax.numpy as jnp
from jax import lax
from jax.experimental import pallas as pl
from jax.experimental.pallas import tpu as pltpu

F32 = jnp.float32
BF16 = jnp.bfloat16

N_DEV = 8
D_MODEL = 1024
N_META = 16
BLOCK = 128
PAD_ROWS = BLOCK - N_META
HEAD_DIM = 64
Q_HEADS = 8
KV_HEADS = 2
GROUP = Q_HEADS // KV_HEADS
ROPE_DIM = HEAD_DIM // 4
ROPE_HALF = ROPE_DIM // 2
ROPE_THETA = 500000.0
RW_HEADS = 8
RW_DIM = 512
DECAY_LORA = 64
AAA_LORA = 64
GATE_LORA = 160
D_FF = 2816
D_IN = 4640
RMS_EPS = 1e-6
RWKV_LN_EPS = 64e-5
NEG_INF = -1e30
SCAN_T = 16
LANES = 128
PACK_ROWS = 256

ADAM_LR = 0.001
ADAM_B1 = 0.9
ADAM_B2 = 0.999
ADAM_EPS = 1e-08
ADAM_WD = 0.01
ADAM_STEP = 10

C_R, C_K, C_V, C_Q = 0, 512, 1024, 1536
C_G1, C_G2 = 2048, 3072
C_DG, C_KA, C_VA, C_DW, C_DA = 4096, 4352, 4480, 4608, 4736
NP = 5120

VMEM_LIMIT = 48 * 1024 * 1024


def _cparams(sem):
    return pltpu.CompilerParams(dimension_semantics=sem, vmem_limit_bytes=VMEM_LIMIT)


def _pick(n, cands):
    for c in cands:
        if n % c == 0:
            return c
    raise ValueError(f"no tile for {n}")


def _mm(a, b, *, ta=False, tb=False, bias=None, residual=None, name):
    m = a.shape[1] if ta else a.shape[0]
    k = a.shape[0] if ta else a.shape[1]
    n = b.shape[0] if tb else b.shape[1]
    assert k == (b.shape[1] if tb else b.shape[0]), (a.shape, b.shape, ta, tb)
    tm = _pick(m, (512, 1408, 256, 128) if ta else (512, 528, 384, 256, 128))
    tn = _pick(n, (512, 1408, 256, 128))
    if k <= 1024:
        tk = k
    else:
        tk = _pick(k, (1024, 1056, 528, 512) if (ta and not tb) else (1024, 1408, 512, 256, 128))
    nk = k // tk
    has_bias = bias is not None
    has_res = residual is not None
    dn = (((0 if ta else 1,), (1 if tb else 0,)), ((), ()))

    def body(*refs):
        a_ref, b_ref = refs[0], refs[1]
        pos = 2
        bias_ref = res_ref = None
        if has_bias:
            bias_ref = refs[pos]
            pos += 1
        if has_res:
            res_ref = refs[pos]
            pos += 1
        o_ref, acc_ref = refs[pos], refs[pos + 1]
        kk = pl.program_id(2)
        part = lax.dot_general(a_ref[...].astype(BF16), b_ref[...].astype(BF16), dn, preferred_element_type=F32)

        def finish(out):
            if has_bias:
                out = out + bias_ref[...]
            if has_res:
                out = out + res_ref[...]
            o_ref[...] = out

        if nk == 1:
            finish(part)
        else:
            @pl.when(kk == 0)
            def _():
                acc_ref[...] = part

            @pl.when((kk > 0) & (kk < nk - 1))
            def _():
                acc_ref[...] += part

            @pl.when(kk == nk - 1)
            def _():
                finish(acc_ref[...] + part)

    in_specs = [
        pl.BlockSpec((tk, tm), lambda i, j, kk: (kk, i)) if ta else pl.BlockSpec((tm, tk), lambda i, j, kk: (i, kk)),
        pl.BlockSpec((tn, tk), lambda i, j, kk: (j, kk)) if tb else pl.BlockSpec((tk, tn), lambda i, j, kk: (kk, j)),
    ]
    args = [a, b]
    if has_bias:
        in_specs.append(pl.BlockSpec((1, tn), lambda i, j, kk: (0, j)))
        args.append(bias)
    if has_res:
        in_specs.append(pl.BlockSpec((tm, tn), lambda i, j, kk: (i, j)))
        args.append(residual)
    return pl.pallas_call(
        body, name=name, grid=(m // tm, n // tn, nk),
        in_specs=in_specs, out_specs=pl.BlockSpec((tm, tn), lambda i, j, kk: (i, j)),
        out_shape=jax.ShapeDtypeStruct((m, n), F32),
        scratch_shapes=[pltpu.VMEM((tm, tn) if nk > 1 else (8, LANES), F32)],
        compiler_params=_cparams(("parallel", "parallel", "arbitrary")),
    )(*args)


def _colsum(x, name):
    m, n = x.shape
    tm = BLOCK

    def body(x_ref, o_ref):
        i = pl.program_id(0)
        s = jnp.sum(x_ref[...], axis=0, keepdims=True)

        @pl.when(i == 0)
        def _():
            o_ref[...] = s

        @pl.when(i > 0)
        def _():
            o_ref[...] += s

    return pl.pallas_call(
        body, name=name, grid=(m // tm,),
        in_specs=[pl.BlockSpec((tm, n), lambda i: (i, 0))],
        out_specs=pl.BlockSpec((1, n), lambda i: (0, 0)),
        out_shape=jax.ShapeDtypeStruct((1, n), F32),
        compiler_params=_cparams(("arbitrary",)),
    )(x)


def _view(arr, width=None, col=0, off=0):
    return (arr, arr.shape[1] if width is None else width, col, off)


def _row_spec(view):
    _, width, col, off = view
    return pl.BlockSpec((BLOCK, width), lambda i, col=col, off=off: (i + off, col))


def _const_spec(arr):
    return pl.BlockSpec(arr.shape, lambda i: (0,) * arr.ndim)


def _rows_fwd(fn, rows, consts, out_widths, *, nblk, name):
    nr, nc = len(rows), len(consts)

    def body(*refs):
        i = pl.program_id(0)
        vals = [r[...] for r in refs[:nr + nc]]
        outs = fn(i, *vals)
        for o_ref, o in zip(refs[nr + nc:], outs):
            o_ref[...] = o

    return pl.pallas_call(
        body, name=name, grid=(nblk,),
        in_specs=[_row_spec(v) for v in rows] + [_const_spec(c) for c in consts],
        out_specs=[pl.BlockSpec((BLOCK, w), lambda i: (i, 0)) for w in out_widths],
        out_shape=[jax.ShapeDtypeStruct((nblk * BLOCK, w), F32) for w in out_widths],
        compiler_params=_cparams(("parallel",)),
    )(*[v[0] for v in rows], *consts)


def _rows_bwd(fn, rows, consts, cts, *, nblk, name, diff_rows, diff_consts, acc=None, fwd_widths=()):
    nr, nc = len(rows), len(consts)
    ct_views = [c for c in cts if c is not None]
    acc = acc or [None] * len(diff_rows)
    acc_views = [a for a in acc if a is not None]
    n_in = nr + nc + len(ct_views) + len(acc_views)
    n_fwd = len(fwd_widths)

    def body(*refs):
        i = pl.program_id(0)
        row_vals = [r[...] for r in refs[:nr]]
        const_vals = [r[...] for r in refs[nr:nr + nc]]
        ct_vals = [r[...] for r in refs[nr + nc:nr + nc + len(ct_views)]]
        acc_vals = [r[...] for r in refs[nr + nc + len(ct_views):n_in]]
        out_refs = refs[n_in:]

        def f(*dargs):
            rv = list(row_vals)
            cv = list(const_vals)
            for pos, idx in enumerate(diff_rows):
                rv[idx] = dargs[pos]
            for pos, idx in enumerate(diff_consts):
                cv[idx] = dargs[len(diff_rows) + pos]
            return tuple(fn(i, *rv, *cv))

        primals = [row_vals[idx] for idx in diff_rows] + [const_vals[idx] for idx in diff_consts]
        outs, pull = jax.vjp(f, *primals)
        full_ct, ci = [], 0
        for o, c in zip(outs, cts):
            if c is None:
                full_ct.append(jnp.zeros_like(o))
            else:
                full_ct.append(ct_vals[ci])
                ci += 1
        grads = pull(tuple(full_ct))
        for o_ref, o in zip(out_refs[:n_fwd], outs):
            o_ref[...] = o
        ai = 0
        for pos in range(len(diff_rows)):
            g = grads[pos]
            if acc[pos] is not None:
                g = g + acc_vals[ai]
                ai += 1
            out_refs[n_fwd + pos][...] = g
        for pos in range(len(diff_consts)):
            g = grads[len(diff_rows) + pos]
            o_ref = out_refs[n_fwd + len(diff_rows) + pos]

            @pl.when(i == 0)
            def _(o_ref=o_ref, g=g):
                o_ref[...] = g

            @pl.when(i > 0)
            def _(o_ref=o_ref, g=g):
                o_ref[...] += g

    out_specs = [pl.BlockSpec((BLOCK, w), lambda i: (i, 0)) for w in fwd_widths]
    out_shape = [jax.ShapeDtypeStruct((nblk * BLOCK, w), F32) for w in fwd_widths]
    for idx in diff_rows:
        out_specs.append(pl.BlockSpec((BLOCK, rows[idx][1]), lambda i: (i, 0)))
        out_shape.append(jax.ShapeDtypeStruct((nblk * BLOCK, rows[idx][1]), F32))
    for idx in diff_consts:
        out_specs.append(_const_spec(consts[idx]))
        out_shape.append(jax.ShapeDtypeStruct(consts[idx].shape, F32))
    return pl.pallas_call(
        body, name=name, grid=(nblk,),
        in_specs=([_row_spec(v) for v in rows] + [_const_spec(c) for c in consts]
                  + [_row_spec(v) for v in ct_views] + [_row_spec(v) for v in acc_views]),
        out_specs=out_specs, out_shape=out_shape,
        compiler_params=_cparams(("arbitrary",)),
    )(*[v[0] for v in rows], *consts, *[v[0] for v in ct_views], *[v[0] for v in acc_views])


def _rms_fn(i, x, g):
    return (x * lax.rsqrt(jnp.mean(x * x, axis=-1, keepdims=True) + RMS_EPS) * g,)


def _sigmoid(x):
    return 1.0 / (1.0 + jnp.exp(-x))


def _softplus(x):
    return jnp.maximum(x, 0.0) + jnp.log(1.0 + jnp.exp(-jnp.abs(x)))


def _split2(x):
    hi = x.astype(BF16)
    lo = (x - hi.astype(F32)).astype(BF16)
    return jnp.concatenate([hi, lo], axis=1)


@jax.custom_vjp
def _head_sum(x):
    r = lax.broadcasted_iota(jnp.int32, (2 * RW_DIM, RW_DIM), 0) % RW_DIM // HEAD_DIM
    c = lax.broadcasted_iota(jnp.int32, (2 * RW_DIM, RW_DIM), 1) // HEAD_DIM
    return jnp.dot(_split2(x), (r == c).astype(BF16), preferred_element_type=F32)


_head_sum.defvjp(lambda x: (_head_sum(x), None), lambda _, ct: (_head_sum(ct),))


@jax.custom_vjp
def _dot_bf16(x, w):
    return jnp.dot(x.astype(BF16), w.astype(BF16), preferred_element_type=F32)


def _dot_bf16_bwd(res, ct):
    x, w = res
    ct = ct.astype(BF16)
    dx = lax.dot_general(ct, w.astype(BF16), (((1,), (1,)), ((), ())), preferred_element_type=F32)
    dw = lax.dot_general(x.astype(BF16), ct, (((0,), (0,)), ((), ())), preferred_element_type=F32)
    return dx, dw


_dot_bf16.defvjp(lambda x, w: (_dot_bf16(x, w), (x, w)), _dot_bf16_bwd)


def _rwkv_pre_fn(i, r, k, v, dw, da, dg, r_p, k_p, v_p, dw_p, da_p, dg_p,
                 mix_r, mix_k, mix_v, mix_dw, mix_da, mix_dg, w0, w2, a0, a2, g2, k_k, k_a):
    row = i * BLOCK + lax.broadcasted_iota(jnp.int32, (BLOCK, 1), 0)
    live = row >= PAD_ROWS
    live_prev = row >= PAD_ROWS + 1

    def shift(cur, prev, mix):
        cur = jnp.where(live, cur, 0.0)
        prev = jnp.where(live_prev, prev, 0.0)
        return cur + (prev - cur) * mix

    r = shift(r, r_p, mix_r)
    k = shift(k, k_p, mix_k)
    v = shift(v, v_p, mix_v)
    dw = shift(dw, dw_p, mix_dw)
    da = shift(da, da_p, mix_da)
    dg = shift(dg, dg_p, mix_dg)
    wlog = -_softplus(-(w0 + _dot_bf16(jnp.tanh(dw), w2))) - 0.5
    decay = jnp.exp(-jnp.exp(wlog))
    a = _sigmoid(a0 + _dot_bf16(da, a2))
    g = _dot_bf16(_sigmoid(dg), g2)
    kk = k * k_k
    norm_sq = jnp.where(live, _head_sum(kk * kk), 1.0)
    kk = kk / jnp.maximum(jnp.sqrt(norm_sq), 1e-12)
    k_mod = k * (1.0 + (a - 1.0) * k_a)
    return r, decay, k_mod, v, -kk, kk * a, g


def _rwkv_post_fn(i, ys, r, k_mod, v, g, ln_w, ln_b, r_k):
    mean = _head_sum(ys) * (1.0 / HEAD_DIM)
    d = ys - mean
    var = _head_sum(d * d) * (1.0 / HEAD_DIM)
    yn = d * lax.rsqrt(var + RWKV_LN_EPS) * ln_w + ln_b
    bonus = _head_sum(r * k_mod * r_k) * v
    return ((yn + bonus) * g,)


def _merge_fn(i, ya, yr, g1, g2):
    return (_sigmoid(g1) * ya + _sigmoid(g2) * yr,)


def _swiglu_fn(i, gate, up):
    return (gate * _sigmoid(gate) * up,)


def _loss_fn(i, h, tgt, g):
    y = h * lax.rsqrt(jnp.mean(h * h, axis=-1, keepdims=True) + RMS_EPS) * g
    err = y - tgt
    return (0.5 * jnp.mean(err * err, axis=-1, keepdims=True),)


def _rope_tables(lp):
    pos = (jnp.arange(lp, dtype=jnp.int32) - PAD_ROWS).astype(F32)
    inv_freq = jnp.power(jnp.float32(ROPE_THETA), -jnp.arange(ROPE_HALF, dtype=F32) * (2.0 / ROPE_DIM))
    ang = pos[:, None] * inv_freq[None, :]
    cos, sin = jnp.cos(ang), jnp.sin(ang)
    one = jnp.ones((lp, HEAD_DIM - ROPE_DIM), F32)
    zero_h = jnp.zeros((lp, ROPE_HALF), F32)
    zero_r = jnp.zeros((lp, HEAD_DIM - ROPE_DIM), F32)
    c = jnp.concatenate([cos, cos, one], axis=1)
    s1 = jnp.concatenate([-sin, zero_h, zero_r], axis=1)
    s2 = jnp.concatenate([zero_h, sin, zero_r], axis=1)
    return tuple(jnp.tile(t, (1, Q_HEADS)) for t in (c, s1, s2))


def _rope_fwd_fn(i, x, c, s1, s2):
    n = x.shape[1]
    return (x * c + pltpu.roll(x, n - ROPE_HALF, 1) * s1 + pltpu.roll(x, ROPE_HALF, 1) * s2,)


def _rope_bwd_fn(i, dy, c, s1, s2):
    n = dy.shape[1]
    return (dy * c + pltpu.roll(dy * s1, ROPE_HALF, 1) + pltpu.roll(dy * s2, n - ROPE_HALF, 1),)


def _attn_mask(i):
    r = lax.broadcasted_iota(jnp.int32, (BLOCK, 3 * BLOCK), 0)
    c = lax.broadcasted_iota(jnp.int32, (BLOCK, 3 * BLOCK), 1)
    meta = (c < BLOCK) & (c >= PAD_ROWS)
    prev = (c >= BLOCK) & (c < 2 * BLOCK) & ((c - BLOCK) > r) & (i >= 1)
    cur = (c >= 2 * BLOCK) & ((c - 2 * BLOCK) <= r)
    return meta | prev | cur


def _attn_fwd(q, k, v, sinks, *, nblk, name):
    scale = HEAD_DIM ** -0.5

    def body(q_ref, km_ref, kp_ref, kc_ref, vm_ref, vp_ref, vc_ref, s_ref, o_ref, lse_ref):
        i = pl.program_id(0)
        valid = _attn_mask(i)
        for h in range(Q_HEADS):
            g = h // GROUP
            kcat = jnp.concatenate([km_ref[g], kp_ref[g], kc_ref[g]], axis=0).astype(BF16)
            vcat = jnp.concatenate([vm_ref[g], vp_ref[g], vc_ref[g]], axis=0).astype(BF16)
            s = lax.dot_general(q_ref[h].astype(BF16), kcat, (((1,), (1,)), ((), ())),
                                preferred_element_type=F32) * scale
            s = jnp.where(valid, s, NEG_INF)
            sink = s_ref[0:1, h:h + 1]
            m = jnp.maximum(jnp.max(s, axis=-1, keepdims=True), sink)
            p = jnp.exp(s - m)
            den = jnp.sum(p, axis=-1, keepdims=True) + jnp.exp(sink - m)
            o = jnp.dot(p.astype(BF16), vcat, preferred_element_type=F32)
            o_ref[h] = o / den
            lse_ref[h] = m + jnp.log(den)

    kv = lambda f: pl.BlockSpec((KV_HEADS, BLOCK, HEAD_DIM), f)
    return pl.pallas_call(
        body, name=name, grid=(nblk,),
        in_specs=[pl.BlockSpec((Q_HEADS, BLOCK, HEAD_DIM), lambda i: (0, i + 1, 0)),
                  kv(lambda i: (0, 0, 0)), kv(lambda i: (0, i, 0)), kv(lambda i: (0, i + 1, 0)),
                  kv(lambda i: (0, 0, 0)), kv(lambda i: (0, i, 0)), kv(lambda i: (0, i + 1, 0)),
                  pl.BlockSpec((1, Q_HEADS), lambda i: (0, 0))],
        out_specs=[pl.BlockSpec((Q_HEADS, BLOCK, HEAD_DIM), lambda i: (0, i, 0)),
                   pl.BlockSpec((Q_HEADS, BLOCK, 1), lambda i: (0, i, 0))],
        out_shape=[jax.ShapeDtypeStruct((Q_HEADS, nblk * BLOCK, HEAD_DIM), F32),
                   jax.ShapeDtypeStruct((Q_HEADS, nblk * BLOCK, 1), F32)],
        compiler_params=_cparams(("parallel",)),
    )(q, k, k, k, v, v, v, sinks)


def _attn_bwd(q, k, v, sinks, o, lse, do, *, nblk, name):
    scale = HEAD_DIM ** -0.5
    lp = k.shape[1]

    def body(q_ref, km_ref, kp_ref, kc_ref, vm_ref, vp_ref, vc_ref, s_ref, o_ref, lse_ref, do_ref,
             dq_ref, dk_ref, dv_ref, ds_ref):
        i = pl.program_id(0)

        @pl.when(i == 0)
        def _():
            dk_ref[...] = jnp.zeros_like(dk_ref)
            dv_ref[...] = jnp.zeros_like(dv_ref)
            ds_ref[...] = jnp.zeros_like(ds_ref)

        valid = _attn_mask(i)
        lane = lax.broadcasted_iota(jnp.int32, (1, Q_HEADS), 1)
        prev_rows = pl.ds(pl.multiple_of(i * BLOCK, BLOCK), BLOCK)
        cur_rows = pl.ds(pl.multiple_of((i + 1) * BLOCK, BLOCK), BLOCK)
        for h in range(Q_HEADS):
            g = h // GROUP
            qh = q_ref[h].astype(BF16)
            doh = do_ref[h]
            kcat = jnp.concatenate([km_ref[g], kp_ref[g], kc_ref[g]], axis=0).astype(BF16)
            vcat = jnp.concatenate([vm_ref[g], vp_ref[g], vc_ref[g]], axis=0).astype(BF16)
            s = lax.dot_general(qh, kcat, (((1,), (1,)), ((), ())), preferred_element_type=F32) * scale
            s = jnp.where(valid, s, NEG_INF)
            lse_h = lse_ref[h]
            p = jnp.exp(s - lse_h)
            delta = jnp.sum(doh * o_ref[h], axis=-1, keepdims=True)
            dp = lax.dot_general(doh.astype(BF16), vcat, (((1,), (1,)), ((), ())), preferred_element_type=F32)
            dsc = (p * (dp - delta) * scale).astype(BF16)
            dq_ref[h] = jnp.dot(dsc, kcat, preferred_element_type=F32)
            dk_all = lax.dot_general(dsc, qh, (((0,), (0,)), ((), ())), preferred_element_type=F32)
            dv_all = lax.dot_general(p.astype(BF16), doh.astype(BF16), (((0,), (0,)), ((), ())),
                                     preferred_element_type=F32)
            dk_ref[g, 0:BLOCK, :] += dk_all[0:BLOCK]
            dk_ref[g, prev_rows, :] += dk_all[BLOCK:2 * BLOCK]
            dk_ref[g, cur_rows, :] += dk_all[2 * BLOCK:]
            dv_ref[g, 0:BLOCK, :] += dv_all[0:BLOCK]
            dv_ref[g, prev_rows, :] += dv_all[BLOCK:2 * BLOCK]
            dv_ref[g, cur_rows, :] += dv_all[2 * BLOCK:]
            p_sink = jnp.exp(s_ref[0:1, h:h + 1] - lse_h)
            dsink = -jnp.sum(p_sink * delta, axis=0, keepdims=True)
            ds_ref[...] += jnp.where(lane == h, dsink, 0.0)

    kv = lambda f: pl.BlockSpec((KV_HEADS, BLOCK, HEAD_DIM), f)
    qspec = pl.BlockSpec((Q_HEADS, BLOCK, HEAD_DIM), lambda i: (0, i, 0))
    whole = pl.BlockSpec((KV_HEADS, lp, HEAD_DIM), lambda i: (0, 0, 0))
    return pl.pallas_call(
        body, name=name, grid=(nblk,),
        in_specs=[pl.BlockSpec((Q_HEADS, BLOCK, HEAD_DIM), lambda i: (0, i + 1, 0)),
                  kv(lambda i: (0, 0, 0)), kv(lambda i: (0, i, 0)), kv(lambda i: (0, i + 1, 0)),
                  kv(lambda i: (0, 0, 0)), kv(lambda i: (0, i, 0)), kv(lambda i: (0, i + 1, 0)),
                  pl.BlockSpec((1, Q_HEADS), lambda i: (0, 0)),
                  qspec, pl.BlockSpec((Q_HEADS, BLOCK, 1), lambda i: (0, i, 0)), qspec],
        out_specs=[qspec, whole, whole, pl.BlockSpec((1, Q_HEADS), lambda i: (0, 0))],
        out_shape=[jax.ShapeDtypeStruct((Q_HEADS, nblk * BLOCK, HEAD_DIM), F32),
                   jax.ShapeDtypeStruct((KV_HEADS, lp, HEAD_DIM), F32),
                   jax.ShapeDtypeStruct((KV_HEADS, lp, HEAD_DIM), F32),
                   jax.ShapeDtypeStruct((1, Q_HEADS), F32)],
        compiler_params=_cparams(("arbitrary",)),
    )(q, k, k, k, v, v, v, sinks, o, lse, do)


N_VEC = 5
VEC_ROWS = N_VEC * HEAD_DIM
XT_PAD = 384


def _selectors():
    t = SCAN_T
    shape = (t, 2 * LANES, RW_DIM)
    step, src, dst = [lax.broadcasted_iota(jnp.int32, shape, d) for d in range(3)]
    src = src % LANES
    spread = ((src // t == dst // HEAD_DIM) & (src % t == step)).astype(BF16)
    shape = (t, RW_DIM, LANES)
    step, src, dst = [lax.broadcasted_iota(jnp.int32, shape, d) for d in range(3)]
    collect = ((src // HEAD_DIM == dst // t) & (dst % t == step)).astype(BF16)
    return spread, collect


def _with_exchange(compute, n_in, n_out, n_scratch, scattered, shared, last_step):
    n_sc = len(scattered)
    n_x = n_sc + len(shared)
    if n_x == 0:
        return compute

    def body(*refs):
        ins, x_in = refs[:n_in], refs[n_in:n_in + n_x]
        outs, x_out = refs[n_in + n_x:n_in + n_x + n_out], refs[n_in + n_x + n_out:n_in + 2 * n_x + n_out]
        scratch = refs[n_in + 2 * n_x + n_out:n_in + 2 * n_x + n_out + n_scratch]
        sems = refs[n_in + 2 * n_x + n_out + n_scratch:]

        @pl.when(pl.program_id(0) == 0)
        def _():
            for cp in _exchange_copies(x_in, x_out, n_sc, *sems):
                cp.start()

        compute(*ins, *outs, *scratch)

        @pl.when(pl.program_id(0) == last_step)
        def _():
            for cp in _exchange_copies(x_in, x_out, n_sc, *sems):
                cp.wait()

    return body


def _wkv_fwd(xt, v, spread, name, shared=()):
    nch = xt.shape[0]
    t_steps = SCAN_T
    n_x = len(shared)

    def compute(xt_ref, v_ref, sel_ref, y_ref, hist_ref, st_ref):
        @pl.when(pl.program_id(0) == 0)
        def _():
            st_ref[...] = jnp.zeros_like(st_ref)

        x2 = _split2(xt_ref[0])
        st = st_ref[...]
        for j in range(t_steps):
            cols = jnp.dot(x2, sel_ref[j], preferred_element_type=F32)
            a_c, w_c, b_c, k_c, r_c = [cols[n * HEAD_DIM:(n + 1) * HEAD_DIM] for n in range(N_VEC)]
            hist_ref[j] = st
            sa = jnp.sum(st * a_c, axis=0, keepdims=True)
            st = st * w_c + b_c * sa + k_c * v_ref[j:j + 1, :]
            y_ref[j:j + 1, :] = jnp.sum(st * r_c, axis=0, keepdims=True)
        st_ref[...] = st

    return pl.pallas_call(
        _with_exchange(compute, 3, 2, 1, (), shared, nch - 1), name=name, grid=(nch,),
        in_specs=[pl.BlockSpec((1, VEC_ROWS, LANES), lambda c: (c, 0, 0)),
                  pl.BlockSpec((t_steps, RW_DIM), lambda c: (c, 0)),
                  pl.BlockSpec(spread.shape, lambda c: (0, 0, 0))] + [ANY] * n_x,
        out_specs=[pl.BlockSpec((t_steps, RW_DIM), lambda c: (c, 0)),
                   pl.BlockSpec((t_steps, HEAD_DIM, RW_DIM), lambda c: (c, 0, 0))] + [ANY] * n_x,
        out_shape=[jax.ShapeDtypeStruct((nch * t_steps, RW_DIM), F32),
                   jax.ShapeDtypeStruct((nch * t_steps, HEAD_DIM, RW_DIM), F32)] + _exchange_shapes((), shared),
        scratch_shapes=[pltpu.VMEM((HEAD_DIM, RW_DIM), F32)] + (_exchange_sems(n_x) if n_x else []),
        compiler_params=_cparams(("arbitrary",)),
    )(xt, v, spread, *shared)


def _wkv_bwd(xt, v, hist, dy, spread, collect, name, scattered=()):
    nch = xt.shape[0]
    t_steps = SCAN_T
    n_x = len(scattered)

    def compute(xt_ref, v_ref, hist_ref, dy_ref, sel_ref, col_ref, dxt_ref, dv_ref, g_ref):
        @pl.when(pl.program_id(0) == 0)
        def _():
            g_ref[...] = jnp.zeros_like(g_ref)

        x2 = _split2(xt_ref[0])
        gst = g_ref[...]
        acc = jnp.zeros((VEC_ROWS, LANES), F32)
        nxt = None
        for j in reversed(range(t_steps)):
            cols = jnp.dot(x2, sel_ref[j], preferred_element_type=F32)
            a_c, w_c, b_c, k_c, r_c = [cols[n * HEAD_DIM:(n + 1) * HEAD_DIM] for n in range(N_VEC)]
            prev = hist_ref[j]
            v_row = v_ref[j:j + 1, :]
            dy_row = dy_ref[j:j + 1, :]
            sa = jnp.sum(prev * a_c, axis=0, keepdims=True)
            if nxt is None:
                nxt = prev * w_c + b_c * sa + k_c * v_row
            gst = gst + r_c * dy_row
            dv_ref[j:j + 1, :] = jnp.sum(gst * k_c, axis=0, keepdims=True)
            dsa = jnp.sum(gst * b_c, axis=0, keepdims=True)
            prods = jnp.concatenate([p.astype(BF16) for p in
                                     (prev * dsa, gst * prev, gst * sa, gst * v_row, nxt * dy_row)], axis=0)
            acc = acc + jnp.dot(prods, col_ref[j], preferred_element_type=F32)
            gst = gst * w_c + a_c * dsa
            nxt = prev
        dxt_ref[0] = jnp.concatenate([acc, jnp.zeros((XT_PAD - VEC_ROWS, LANES), F32)], axis=0).T
        g_ref[...] = gst

    rev3 = lambda c: (nch - 1 - c, 0, 0)
    rev2 = lambda c: (nch - 1 - c, 0)
    rowspec = pl.BlockSpec((t_steps, RW_DIM), rev2)
    return pl.pallas_call(
        _with_exchange(compute, 6, 2, 1, scattered, (), nch - 1), name=name, grid=(nch,),
        in_specs=[pl.BlockSpec((1, VEC_ROWS, LANES), rev3), rowspec,
                  pl.BlockSpec((t_steps, HEAD_DIM, RW_DIM), rev3), rowspec,
                  pl.BlockSpec(spread.shape, lambda c: (0, 0, 0)),
                  pl.BlockSpec(collect.shape, lambda c: (0, 0, 0))] + [ANY] * n_x,
        out_specs=[pl.BlockSpec((1, LANES, XT_PAD), rev3), rowspec] + [ANY] * n_x,
        out_shape=[jax.ShapeDtypeStruct((nch, LANES, XT_PAD), F32),
                   jax.ShapeDtypeStruct((nch * t_steps, RW_DIM), F32)] + _exchange_shapes(scattered, ()),
        scratch_shapes=[pltpu.VMEM((HEAD_DIM, RW_DIM), F32)] + (_exchange_sems(n_x) if n_x else []),
        compiler_params=_cparams(("arbitrary",)),
    )(xt, v, hist, dy, spread, collect, *scattered)


def _to_xt(vecs, nch):
    parts = [x.reshape(nch, SCAN_T, RW_HEADS, HEAD_DIM).transpose(0, 3, 2, 1).reshape(nch, HEAD_DIM, LANES)
             for x in vecs]
    return jnp.concatenate(parts, axis=1)


def _from_xt_t(xt_t):
    nch = xt_t.shape[0]
    x = xt_t.reshape(nch, RW_HEADS, SCAN_T, XT_PAD // HEAD_DIM, HEAD_DIM).transpose(3, 0, 2, 1, 4)
    return [x[n].reshape(nch * SCAN_T, RW_DIM) for n in range(N_VEC)]


MESH = pl.DeviceIdType.MESH
ANY = pl.BlockSpec(memory_space=pl.ANY)


def _all_gather(arrays, name):
    n_arr = len(arrays)
    per = N_DEV - 1

    def body(*refs):
        x_refs, out_refs = refs[:n_arr], refs[n_arr:2 * n_arr]
        send_sems, recv_sems, local_sems = refs[2 * n_arr:]
        xi, yi, ci = lax.axis_index("x"), lax.axis_index("y"), lax.axis_index("c")
        me, sibling = (xi, yi, ci), (xi, yi, 1 - ci)
        chips = [(1 - xi, yi), (xi, 1 - yi), (1 - xi, 1 - yi)]

        def slot(a, px, py, pc):
            return out_refs[a].at[4 * px + 2 * py + pc]

        def copy(a, sem, block, to, src=None):
            return pltpu.make_async_remote_copy(
                src_ref=slot(a, *block) if src is None else src, dst_ref=slot(a, *block),
                send_sem=send_sems.at[per * a + sem], recv_sem=recv_sems.at[per * a + sem],
                device_id=to, device_id_type=MESH)

        mine = [pltpu.make_async_copy(x_refs[a], slot(a, *me), local_sems.at[a]) for a in range(n_arr)]
        for cp in mine:
            cp.start()
        sent = []
        for a in range(n_arr):
            sent.append(copy(a, 0, me, sibling, src=x_refs[a]))
            sent += [copy(a, 1 + j, me, (*chip, ci), src=x_refs[a]) for j, chip in enumerate(chips)]
        for cp in sent:
            cp.start()
        for j, chip in enumerate(chips):
            for a in range(n_arr):
                copy(a, 1 + j, (*chip, ci), me).wait_recv()
                onward = copy(a, 4 + j, (*chip, ci), sibling)
                onward.start()
                sent.append(onward)
        for a in range(n_arr):
            copy(a, 0, sibling, me).wait_recv()
        for j, chip in enumerate(chips):
            for a in range(n_arr):
                copy(a, 4 + j, (*chip, 1 - ci), me).wait_recv()
        for cp in sent:
            cp.wait_send()
        for cp in mine:
            cp.wait()

    sems = pltpu.SemaphoreType.DMA((per * n_arr,))
    return pl.pallas_call(
        body, name=name, out_shape=[jax.ShapeDtypeStruct((N_DEV,) + a.shape, a.dtype) for a in arrays],
        in_specs=[ANY] * n_arr, out_specs=[ANY] * n_arr,
        scratch_shapes=[sems, sems, pltpu.SemaphoreType.DMA((n_arr,))],
    )(*arrays)


def _exchange_copies(in_refs, out_refs, n_scattered, send_sems, recv_sems, local_sems):
    n_arr = len(in_refs)
    per = N_DEV - 1
    xi, yi, ci = lax.axis_index("x"), lax.axis_index("y"), lax.axis_index("c")
    me = 4 * xi + 2 * yi + ci
    src_of = lambda a, peer: in_refs[a].at[peer] if a < n_scattered else in_refs[a]
    copies = []
    for d in range(1, N_DEV):
        px = 1 - xi if d & 4 else xi
        py = 1 - yi if d & 2 else yi
        pc = 1 - ci if d & 1 else ci
        for a in range(n_arr):
            copies.append(pltpu.make_async_remote_copy(
                src_ref=src_of(a, 4 * px + 2 * py + pc), dst_ref=out_refs[a].at[me],
                send_sem=send_sems.at[per * a + d - 1], recv_sem=recv_sems.at[per * a + d - 1],
                device_id=(px, py, pc), device_id_type=MESH))
    own = [pltpu.make_async_copy(src_of(a, me), out_refs[a].at[me], local_sems.at[a]) for a in range(n_arr)]
    return copies + own


def _exchange_shapes(scattered, shared):
    return ([jax.ShapeDtypeStruct(a.shape, a.dtype) for a in scattered]
            + [jax.ShapeDtypeStruct((N_DEV,) + a.shape, a.dtype) for a in shared])


def _exchange_sems(n_arr):
    sems = pltpu.SemaphoreType.DMA(((N_DEV - 1) * n_arr,))
    return [sems, sems, pltpu.SemaphoreType.DMA((n_arr,))]


def _exchange(scattered, shared, name):
    n_sc = len(scattered)
    n_arr = n_sc + len(shared)

    def body(*refs):
        copies = _exchange_copies(refs[:n_arr], refs[n_arr:2 * n_arr], n_sc, *refs[2 * n_arr:])
        for cp in copies:
            cp.start()
        for cp in copies:
            cp.wait()

    return pl.pallas_call(
        body, name=name, out_shape=_exchange_shapes(scattered, shared),
        in_specs=[ANY] * n_arr, out_specs=[ANY] * n_arr, scratch_shapes=_exchange_sems(n_arr),
    )(*scattered, *shared)


def _adamw(parts, w, m, v, name):
    rows, cols = w.shape
    tile = rows if rows <= PACK_ROWS else _pick(rows, (PACK_ROWS,))

    def body(p_ref, w_ref, m_ref, v_ref, g_out, d_out, m_out, v_out):
        g = p_ref[0].astype(F32)
        for s in range(1, N_DEV):
            g = g + p_ref[s].astype(F32)
        m_new = ADAM_B1 * m_ref[...] + (1.0 - ADAM_B1) * g
        v_new = ADAM_B2 * v_ref[...] + (1.0 - ADAM_B2) * (g * g)
        m_hat = m_new / (1.0 - ADAM_B1 ** ADAM_STEP)
        v_hat = v_new / (1.0 - ADAM_B2 ** ADAM_STEP)
        g_out[...] = g
        d_out[...] = -ADAM_LR * (m_hat / (jnp.sqrt(v_hat) + ADAM_EPS) + ADAM_WD * w_ref[...])
        m_out[...] = m_new
        v_out[...] = v_new

    spec = pl.BlockSpec((tile, cols), lambda i: (i, 0))
    return pl.pallas_call(
        body, name=name, grid=(rows // tile,),
        in_specs=[pl.BlockSpec((N_DEV, tile, cols), lambda i: (0, i, 0)), spec, spec, spec],
        out_specs=[spec] * 4, out_shape=[jax.ShapeDtypeStruct((rows, cols), F32)] * 4,
        compiler_params=_cparams(("parallel",)),
    )(parts, w, m, v)


EARLY = [("meta_tokens", 1), ("rwkv_w2", 1), ("rwkv_a2", 1), ("rwkv_g2", 1)]
LATE = [("w_br_attn", 1), ("w_br_rwkv", 1), ("w_o", 0), ("w_ffn_gate", 1), ("w_ffn_up", 1), ("w_ffn_down", 0)]
REPLICATED = ["norm_mix_g", "b_in", "attn_sinks", "rwkv_mix", "rwkv_w0", "rwkv_a0", "rwkv_k_k", "rwkv_k_a",
              "rwkv_r_k", "rwkv_ln_w", "rwkv_ln_b", "norm_ffn_g", "norm_final_g"]
WEIGHTS = ["meta_tokens", "norm_mix_g", "w_in", "b_in", "attn_sinks", "rwkv_mix", "rwkv_w0", "rwkv_w2", "rwkv_a0",
           "rwkv_a2", "rwkv_g2", "rwkv_k_k", "rwkv_k_a", "rwkv_r_k", "rwkv_ln_w", "rwkv_ln_b", "w_br_attn",
           "w_br_rwkv", "w_o", "norm_ffn_g", "w_ffn_gate", "w_ffn_up", "w_ffn_down", "norm_final_g"]


def _pack(arrays, row_multiple):
    flat = jnp.concatenate([a.reshape(-1) for a in arrays])
    per = row_multiple * LANES
    total = -(-flat.shape[0] // per) * per
    return jnp.pad(flat, (0, total - flat.shape[0])).reshape(-1, LANES)


def _unpack(buf, shapes):
    flat = buf.reshape(-1)
    out, pos = [], 0
    for s in shapes:
        n = 1
        for d in s:
            n *= d
        out.append(flat[pos:pos + n].reshape(s))
        pos += n
    return out


def _strip(name, a):
    return a if name in ("meta_tokens", "norm_final_g") else a[0]


def _shard_offsets(shapes):
    offs, pos = [], 0
    for r, c in shapes:
        offs.append(pos)
        pos += r * c
    return offs


def _assemble(gathered, table, shapes):
    flat = gathered.reshape(N_DEV, -1)
    out = {}
    for (name, axis), (r, c), off in zip(table, shapes, _shard_offsets(shapes)):
        piece = flat[:, off:off + r * c]
        if axis == 0:
            out[name] = piece.reshape(N_DEV * r, c)
        else:
            out[name] = piece.reshape(N_DEV, r, c).transpose(1, 0, 2).reshape(r, N_DEV * c)
    return out


def _disassemble(grads, table, shapes, rows):
    pieces = []
    for (name, axis), (r, c) in zip(table, shapes):
        g = grads[name]
        if axis == 0:
            pieces.append(g.reshape(N_DEV, r * c))
        else:
            pieces.append(g.reshape(r, N_DEV, c).transpose(1, 0, 2).reshape(N_DEV, r * c))
    flat = jnp.concatenate(pieces, axis=1)
    return jnp.pad(flat, ((0, 0), (0, rows * LANES - flat.shape[1]))).reshape(N_DEV, rows, LANES)


W_IN_LAYOUT = [(768, 2304), (0, 512), (2592, 4640), (2432, 2592), 256 - GATE_LORA, (512, 768), (2304, 2368),
               128 - DECAY_LORA, (2368, 2432), 128 - AAA_LORA, NP - C_DA - 128]


def _w_in_padded(w, shard_width=None):
    rows = w.shape[-2]
    width = D_IN if shard_width is None else shard_width
    parts = []
    for seg in W_IN_LAYOUT:
        if isinstance(seg, int):
            parts.append(jnp.zeros((rows, seg), w.dtype))
            continue
        lo, stop = seg
        while lo < stop:
            p = lo // width
            hi = min(stop, (p + 1) * width)
            src = w if shard_width is None else w[p]
            parts.append(src[:, lo - p * width:hi - p * width])
            lo = hi
    return jnp.concatenate(parts, axis=1)


def _w_in_unpadded(wp, lo=0, stop=D_IN):
    spans, pos = [], 0
    for seg in W_IN_LAYOUT:
        if isinstance(seg, int):
            pos += seg
        else:
            spans.append((seg[0], seg[1], pos))
            pos += seg[1] - seg[0]
    parts = []
    for a, b, at in sorted(spans):
        c, d = max(a, lo), min(b, stop)
        if c < d:
            parts.append(wp[:, at + c - a:at + d - a])
    return jnp.concatenate(parts, axis=1)


def _pad_rows(a, n):
    return jnp.pad(a, ((0, n - a.shape[0]), (0, 0)))


def _device_step(x, tgt, full, gather_late=None, scatter_early=None):
    seq = x.shape[0]
    nblk = seq // BLOCK
    lp = seq + BLOCK
    nall = nblk + 1
    nch = lp // SCAN_T

    w_in_p = full["w_in_p"]
    b_in_p = _w_in_padded(full["b_in"][None])
    mix = full["rwkv_mix"][None]
    mix_r, mix_k, mix_v = mix[:, 0:512], mix[:, 512:1024], mix[:, 1024:1536]
    mix_dw = jnp.pad(mix[:, 1536:1600], ((0, 0), (0, 64)))
    mix_da = jnp.pad(mix[:, 1600:1664], ((0, 0), (0, 64)))
    mix_dg = jnp.pad(mix[:, 1664:1824], ((0, 0), (0, 96)))
    w2_p = _pad_rows(full["rwkv_w2"].astype(F32), 128)
    a2_p = _pad_rows(full["rwkv_a2"].astype(F32), 128)
    g2_p = _pad_rows(full["rwkv_g2"].astype(F32), 256)
    row = lambda name: full[name].reshape(1, -1)
    sinks = row("attn_sinks")
    rope_c, rope_s1, rope_s2 = _rope_tables(lp)

    hpad = jnp.concatenate([jnp.zeros((PAD_ROWS, D_MODEL), F32), full["meta_tokens"].astype(F32), x], axis=0)
    (u,) = _rows_fwd(_rms_fn, [_view(hpad)], [row("norm_mix_g")], [D_MODEL], nblk=nall, name="norm_mix")
    proj = _mm(u, w_in_p, bias=b_in_p, name="in_proj")
    (q_r,) = _rows_fwd(_rope_fwd_fn, [_view(proj, 512, C_Q // 512), _view(rope_c), _view(rope_s1), _view(rope_s2)],
                       [], [512], nblk=nall, name="rope_q")
    (k_r,) = _rows_fwd(_rope_fwd_fn, [_view(proj, 128, C_KA // 128), _view(rope_c, 128), _view(rope_s1, 128),
                                      _view(rope_s2, 128)], [], [128], nblk=nall, name="rope_k")
    heads = lambda a, nh: a.reshape(a.shape[0], nh, HEAD_DIM).transpose(1, 0, 2)
    unheads = lambda a: a.transpose(1, 0, 2).reshape(a.shape[1], -1)
    q_h, k_h = heads(q_r, Q_HEADS), heads(k_r, KV_HEADS)
    v_h = heads(proj[:, C_VA:C_VA + 128], KV_HEADS)
    o_h, lse = _attn_fwd(q_h, k_h, v_h, sinks, nblk=nblk, name="attn_fwd")
    y_attn = unheads(o_h)

    rw_cols = jnp.concatenate([proj[:, C_R:C_R + 1536], proj[:, C_DG:C_DG + 256], proj[:, C_DW:C_DW + 256]], axis=1)
    rw_prev = jnp.pad(rw_cols[:-1], ((1, 0), (0, 0)))
    pre_rows = [_view(proj, 512, 0), _view(proj, 512, 1), _view(proj, 512, 2), _view(proj, 128, C_DW // 128),
                _view(proj, 128, C_DA // 128), _view(proj, 256, C_DG // 256),
                _view(rw_prev, 512, 0), _view(rw_prev, 512, 1), _view(rw_prev, 512, 2), _view(rw_prev, 128, 14),
                _view(rw_prev, 128, 15), _view(rw_prev, 256, 6)]
    pre_consts = [mix_r, mix_k, mix_v, mix_dw, mix_da, mix_dg, row("rwkv_w0"), w2_p, row("rwkv_a0"), a2_p, g2_p,
                  row("rwkv_k_k"), row("rwkv_k_a")]
    r_t, decay, k_mod, v_t, a_neg, b_t, gate = _rows_fwd(_rwkv_pre_fn, pre_rows, pre_consts, [RW_DIM] * 7,
                                                         nblk=nall, name="rwkv_pre")
    spread, collect = _selectors()
    xt = _to_xt((a_neg, decay, b_t, k_mod, r_t), nch)
    y_scan, hist, *late = _wkv_fwd(xt, v_t, spread, name="wkv_fwd", shared=gather_late[0] if gather_late else ())
    if gather_late:
        full = {**full, **gather_late[1](late)}
    post_rows = [_view(y_scan, off=1), _view(r_t, off=1), _view(k_mod, off=1), _view(v_t, off=1), _view(gate, off=1)]
    post_consts = [row("rwkv_ln_w"), row("rwkv_ln_b"), row("rwkv_r_k")]
    (y_rwkv,) = _rows_fwd(_rwkv_post_fn, post_rows, post_consts, [RW_DIM], nblk=nblk, name="rwkv_post")

    ya = _mm(y_attn, full["w_br_attn"], name="br_attn")
    yr = _mm(y_rwkv, full["w_br_rwkv"], name="br_rwkv")
    merge_rows = [_view(ya), _view(yr), _view(proj, 1024, C_G1 // 1024, 1), _view(proj, 1024, C_G2 // 1024, 1)]
    (merged,) = _rows_fwd(_merge_fn, merge_rows, [], [D_MODEL], nblk=nblk, name="merge")
    h1 = _mm(merged, full["w_o"], residual=x, name="out_proj")
    (f,) = _rows_fwd(_rms_fn, [_view(h1)], [row("norm_ffn_g")], [D_MODEL], nblk=nblk, name="norm_ffn")
    ff_gate = _mm(f, full["w_ffn_gate"], name="ffn_gate")
    ff_up = _mm(f, full["w_ffn_up"], name="ffn_up")
    (act,) = _rows_fwd(_swiglu_fn, [_view(ff_gate), _view(ff_up)], [], [D_FF], nblk=nblk, name="swiglu")
    h2 = _mm(act, full["w_ffn_down"], residual=h1, name="ffn_down")

    grads = {}
    ones_col = jnp.ones((seq, 1), F32)
    loss_rows, dh2, grads["norm_final_g"] = _rows_bwd(
        _loss_fn, [_view(h2), _view(tgt)], [row("norm_final_g")], [_view(ones_col)], nblk=nblk, name="loss",
        diff_rows=[0], diff_consts=[0], fwd_widths=[1])
    loss = jnp.sum(loss_rows)

    dact = _mm(dh2, full["w_ffn_down"], tb=True, name="d_act")
    grads["w_ffn_down"] = _mm(act, dh2, ta=True, name="dw_ffn_down")
    dgate, dup = _rows_bwd(_swiglu_fn, [_view(ff_gate), _view(ff_up)], [], [_view(dact)], nblk=nblk,
                           name="swiglu_bwd", diff_rows=[0, 1], diff_consts=[])
    grads["w_ffn_gate"] = _mm(f, dgate, ta=True, name="dw_ffn_gate")
    grads["w_ffn_up"] = _mm(f, dup, ta=True, name="dw_ffn_up")
    df = _mm(dgate, full["w_ffn_gate"], tb=True, name="df_gate")
    df = _mm(dup, full["w_ffn_up"], tb=True, residual=df, name="df_up")
    dh1, grads["norm_ffn_g"] = _rows_bwd(_rms_fn, [_view(h1)], [row("norm_ffn_g")], [_view(df)], nblk=nblk,
                                         name="norm_ffn_bwd", diff_rows=[0], diff_consts=[0], acc=[_view(dh2)])
    dmerged = _mm(dh1, full["w_o"], tb=True, name="d_merged")
    grads["w_o"] = _mm(merged, dh1, ta=True, name="dw_o")
    dya, dyr, dg1, dg2 = _rows_bwd(_merge_fn, merge_rows, [], [_view(dmerged)], nblk=nblk, name="merge_bwd",
                                   diff_rows=[0, 1, 2, 3], diff_consts=[])
    grads["w_br_attn"] = _mm(y_attn, dya, ta=True, name="dw_br_attn")
    grads["w_br_rwkv"] = _mm(y_rwkv, dyr, ta=True, name="dw_br_rwkv")
    dy_attn = _mm(dya, full["w_br_attn"], tb=True, name="d_y_attn")
    dy_rwkv = _mm(dyr, full["w_br_rwkv"], tb=True, name="d_y_rwkv")

    post = _rows_bwd(_rwkv_post_fn, post_rows, post_consts, [_view(dy_rwkv)], nblk=nblk, name="rwkv_post_bwd",
                     diff_rows=[0, 1, 2, 3, 4], diff_consts=[0, 1, 2])
    dys, dr_post, dk_post, dv_post, dgate_post = [jnp.pad(t, ((BLOCK, 0), (0, 0))) for t in post[:5]]
    grads["rwkv_ln_w"], grads["rwkv_ln_b"], grads["rwkv_r_k"] = post[5:]
    dxt, dv_s, *early_parts = _wkv_bwd(xt, v_t, hist, dys, spread, collect, name="wkv_bwd",
                                       scattered=scatter_early(grads) if scatter_early else ())
    da_s, dw_s, db_s, dk_s, dr_s = _from_xt_t(dxt)
    pre_cts = [_view(dr_s + dr_post), _view(dw_s), _view(dk_s + dk_post), _view(dv_s + dv_post), _view(da_s),
               _view(db_s), _view(dgate_post)]
    pre = _rows_bwd(_rwkv_pre_fn, pre_rows, pre_consts, pre_cts, nblk=nall, name="rwkv_pre_bwd",
                    diff_rows=list(range(12)), diff_consts=list(range(13)))
    d_cur, d_prev, d_par = pre[0:6], pre[6:12], pre[12:]
    up = lambda t: jnp.pad(t[1:], ((0, 1), (0, 0)))
    d_rw = [c + up(p) for c, p in zip(d_cur, d_prev)]
    grads["rwkv_mix"] = jnp.concatenate([d_par[0], d_par[1], d_par[2], d_par[3][:, :DECAY_LORA],
                                         d_par[4][:, :AAA_LORA], d_par[5][:, :GATE_LORA]], axis=1)
    grads["rwkv_w0"], grads["rwkv_w2"] = d_par[6], d_par[7][:DECAY_LORA]
    grads["rwkv_a0"], grads["rwkv_a2"] = d_par[8], d_par[9][:AAA_LORA]
    grads["rwkv_g2"] = d_par[10][:GATE_LORA]
    grads["rwkv_k_k"], grads["rwkv_k_a"] = d_par[11], d_par[12]

    do_h = heads(dy_attn, Q_HEADS)
    dq_h, dk_h, dv_h, grads["attn_sinks"] = _attn_bwd(q_h, k_h, v_h, sinks, o_h, lse, do_h, nblk=nblk,
                                                      name="attn_bwd")
    dq_r = jnp.pad(unheads(dq_h), ((BLOCK, 0), (0, 0)))
    (dq,) = _rows_fwd(_rope_bwd_fn, [_view(dq_r), _view(rope_c), _view(rope_s1), _view(rope_s2)], [], [512],
                      nblk=nall, name="rope_q_bwd")
    (dka,) = _rows_fwd(_rope_bwd_fn, [_view(unheads(dk_h)), _view(rope_c, 128), _view(rope_s1, 128),
                                      _view(rope_s2, 128)], [], [128], nblk=nall, name="rope_k_bwd")
    dva = unheads(dv_h)

    lead = lambda t: jnp.pad(t, ((BLOCK, 0), (0, 0)))
    dproj = jnp.concatenate([d_rw[0], d_rw[1], d_rw[2], dq, lead(dg1), lead(dg2), d_rw[5], dka, dva, d_rw[3],
                             d_rw[4], jnp.zeros((lp, NP - C_DA - 128), F32)], axis=1)
    grads["w_in_p"] = _mm(u, dproj, ta=True, name="dw_in")
    grads["b_in"] = _w_in_unpadded(_colsum(dproj, name="db_in"))
    du = _mm(dproj, w_in_p, tb=True, name="d_u")
    dh, grads["norm_mix_g"] = _rows_bwd(_rms_fn, [_view(hpad)], [row("norm_mix_g")], [_view(du)], nblk=nall,
                                        name="norm_mix_bwd", diff_rows=[0], diff_consts=[0], acc=[_view(lead(dh1))])
    grads["meta_tokens"] = dh[PAD_ROWS:BLOCK]
    return loss, dh[BLOCK:], grads, early_parts


def kernel(x, meta_tokens, norm_mix_g, w_in, b_in, attn_sinks, rwkv_mix, rwkv_w0, rwkv_w2, rwkv_a0, rwkv_a2, rwkv_g2, rwkv_k_k, rwkv_k_a, rwkv_r_k, rwkv_ln_w, rwkv_ln_b, w_br_attn, w_br_rwkv, w_o, norm_ffn_g, w_ffn_gate, w_ffn_up, w_ffn_down, norm_final_g, loss_target, m_meta_tokens, m_norm_mix_g, m_w_in, m_b_in, m_attn_sinks, m_rwkv_mix, m_rwkv_w0, m_rwkv_w2, m_rwkv_a0, m_rwkv_a2, m_rwkv_g2, m_rwkv_k_k, m_rwkv_k_a, m_rwkv_r_k, m_rwkv_ln_w, m_rwkv_ln_b, m_w_br_attn, m_w_br_rwkv, m_w_o, m_norm_ffn_g, m_w_ffn_gate, m_w_ffn_up, m_w_ffn_down, m_norm_final_g, v_meta_tokens, v_norm_mix_g, v_w_in, v_b_in, v_attn_sinks, v_rwkv_mix, v_rwkv_w0, v_rwkv_w2, v_rwkv_a0, v_rwkv_a2, v_rwkv_g2, v_rwkv_k_k, v_rwkv_k_a, v_rwkv_r_k, v_rwkv_ln_w, v_rwkv_ln_b, v_w_br_attn, v_w_br_rwkv, v_w_o, v_norm_ffn_g, v_w_ffn_gate, v_w_ffn_up, v_w_ffn_down, v_norm_final_g):
    given = dict(locals())
    wts = {n: _strip(n, given[n]) for n in WEIGHTS}
    mom = {n: _strip(n, given["m_" + n]) for n in WEIGHTS}
    var = {n: _strip(n, given["v_" + n]) for n in WEIGHTS}
    early_shapes = [wts[n].shape for n, _ in EARLY]
    late_shapes = [wts[n].shape for n, _ in LATE]
    small_shapes = [wts[n].shape for n in REPLICATED]
    width = wts["w_in"].shape[1]

    early_pack = _pack([wts[n] for n, _ in EARLY], 16)
    late_pack = _pack([wts[n] for n, _ in LATE], PACK_ROWS)
    w_in_all, gathered = _all_gather([wts["w_in"].astype(BF16), early_pack.astype(BF16)], name="gather_weights")
    full = {n: wts[n] for n in REPLICATED}
    full.update(_assemble(gathered, EARLY, early_shapes))
    full["w_in_p"] = _w_in_padded(w_in_all, shard_width=width)
    gather_late = ([late_pack.astype(BF16)], lambda got: _assemble(got[0], LATE, late_shapes))
    scatter_early = lambda g: [_disassemble(g, LATE, late_shapes, late_pack.shape[0]).astype(BF16)]

    loss_part, grad_x, grads, (parts_late,) = _device_step(x[0], loss_target[0], full, gather_late, scatter_early)

    g_w_in = jnp.stack([_w_in_unpadded(grads["w_in_p"], p * width, (p + 1) * width) for p in range(N_DEV)])
    g_early = _disassemble(grads, EARLY, early_shapes, early_pack.shape[0])
    zero = jnp.zeros((1,), F32)
    g_small = _pack([grads[n].reshape(wts[n].shape) for n in REPLICATED] + [loss_part.reshape(1)], 8)
    parts_w_in, parts_early, parts_small = _exchange([g_w_in.astype(BF16), g_early.astype(BF16)], [g_small],
                                                     name="exchange_grads")
    w_in_out = _adamw(parts_w_in, wts["w_in"], mom["w_in"], var["w_in"], name="adamw_w_in")
    early = _adamw(parts_early, early_pack, _pack([mom[n] for n, _ in EARLY], 16),
                   _pack([var[n] for n, _ in EARLY], 16), name="adamw_early")
    late = _adamw(parts_late, late_pack, _pack([mom[n] for n, _ in LATE], PACK_ROWS),
                  _pack([var[n] for n, _ in LATE], PACK_ROWS), name="adamw_late")
    small = _adamw(parts_small, _pack([wts[n] for n in REPLICATED] + [zero], 8),
                   _pack([mom[n] for n in REPLICATED] + [zero], 8), _pack([var[n] for n in REPLICATED] + [zero], 8),
                   name="adamw_replicated")

    results = [{}, {}, {}, {}]
    for kind in range(4):
        results[kind]["w_in"] = w_in_out[kind]
        for (n, _), a in zip(EARLY, _unpack(early[kind], early_shapes)):
            results[kind][n] = a
        for (n, _), a in zip(LATE, _unpack(late[kind], late_shapes)):
            results[kind][n] = a
        for n, a in zip(REPLICATED, _unpack(small[kind], small_shapes)):
            results[kind][n] = a
    loss = _unpack(small[0], small_shapes + [(1,)])[-1][0]
    out = [loss, grad_x[None]]
    for kind in range(4):
        out += [results[kind][n].reshape(given[n].shape) for n in WEIGHTS]
    return tuple(out)
```

```python
import functools

import jax
import jax.numpy as jnp
from jax import lax
from jax.experimental import pallas as pl
from jax.experimental.pallas import tpu as pltpu

F32 = jnp.float32
BF16 = jnp.bfloat16

N_DEV = 8
D_MODEL = 1024
N_META = 16
BLOCK = 128
PAD_ROWS = BLOCK - N_META
HEAD_DIM = 64
Q_HEADS = 8
KV_HEADS = 2
GROUP = Q_HEADS // KV_HEADS
ROPE_DIM = HEAD_DIM // 4
ROPE_HALF = ROPE_DIM // 2
ROPE_THETA = 500000.0
RW_HEADS = 8
RW_DIM = 512
DECAY_LORA = 64
AAA_LORA = 64
GATE_LORA = 160
D_FF = 2816
D_IN = 4640
RMS_EPS = 1e-6
RWKV_LN_EPS = 64e-5
NEG_INF = -1e30
SCAN_T = 16
LANES = 128
PACK_ROWS = 256

ADAM_LR = 0.001
ADAM_B1 = 0.9
ADAM_B2 = 0.999
ADAM_EPS = 1e-08
ADAM_WD = 0.01
ADAM_STEP = 10

C_R, C_K, C_V, C_Q = 0, 512, 1024, 1536
C_G1, C_G2 = 2048, 3072
C_DG, C_KA, C_VA, C_DW, C_DA = 4096, 4352, 4480, 4608, 4736
NP = 5120

VMEM_LIMIT = 48 * 1024 * 1024


def _cparams(sem):
    return pltpu.CompilerParams(dimension_semantics=sem, vmem_limit_bytes=VMEM_LIMIT)


def _pick(n, cands):
    for c in cands:
        if n % c == 0:
            return c
    raise ValueError(f"no tile for {n}")


def _mm(a, b, *, ta=False, tb=False, bias=None, residual=None, name):
    m = a.shape[1] if ta else a.shape[0]
    k = a.shape[0] if ta else a.shape[1]
    n = b.shape[0] if tb else b.shape[1]
    assert k == (b.shape[1] if tb else b.shape[0]), (a.shape, b.shape, ta, tb)
    tm = _pick(m, (512, 1408, 256, 128) if ta else (512, 528, 384, 256, 128))
    tn = _pick(n, (512, 1408, 256, 128))
    if k <= 1024:
        tk = k
    else:
        tk = _pick(k, (1024, 1056, 528, 512) if (ta and not tb) else (1024, 1408, 512, 256, 128))
    nk = k // tk
    has_bias = bias is not None
    has_res = residual is not None
    dn = (((0 if ta else 1,), (1 if tb else 0,)), ((), ()))

    def body(*refs):
        a_ref, b_ref = refs[0], refs[1]
        pos = 2
        bias_ref = res_ref = None
        if has_bias:
            bias_ref = refs[pos]
            pos += 1
        if has_res:
            res_ref = refs[pos]
            pos += 1
        o_ref, acc_ref = refs[pos], refs[pos + 1]
        kk = pl.program_id(2)
        part = lax.dot_general(a_ref[...].astype(BF16), b_ref[...].astype(BF16), dn, preferred_element_type=F32)

        def finish(out):
            if has_bias:
                out = out + bias_ref[...]
            if has_res:
                out = out + res_ref[...]
            o_ref[...] = out

        if nk == 1:
            finish(part)
        else:
            @pl.when(kk == 0)
            def _():
                acc_ref[...] = part

            @pl.when((kk > 0) & (kk < nk - 1))
            def _():
                acc_ref[...] += part

            @pl.when(kk == nk - 1)
            def _():
                finish(acc_ref[...] + part)

    in_specs = [
        pl.BlockSpec((tk, tm), lambda i, j, kk: (kk, i)) if ta else pl.BlockSpec((tm, tk), lambda i, j, kk: (i, kk)),
        pl.BlockSpec((tn, tk), lambda i, j, kk: (j, kk)) if tb else pl.BlockSpec((tk, tn), lambda i, j, kk: (kk, j)),
    ]
    args = [a, b]
    if has_bias:
        in_specs.append(pl.BlockSpec((1, tn), lambda i, j, kk: (0, j)))
        args.append(bias)
    if has_res:
        in_specs.append(pl.BlockSpec((tm, tn), lambda i, j, kk: (i, j)))
        args.append(residual)
    return pl.pallas_call(
        body, name=name, grid=(m // tm, n // tn, nk),
        in_specs=in_specs, out_specs=pl.BlockSpec((tm, tn), lambda i, j, kk: (i, j)),
        out_shape=jax.ShapeDtypeStruct((m, n), F32),
        scratch_shapes=[pltpu.VMEM((tm, tn) if nk > 1 else (8, LANES), F32)],
        compiler_params=_cparams(("parallel", "parallel", "arbitrary")),
    )(*args)


def _colsum(x, name):
    m, n = x.shape
    tm = BLOCK

    def body(x_ref, o_ref):
        i = pl.program_id(0)
        s = jnp.sum(x_ref[...], axis=0, keepdims=True)

        @pl.when(i == 0)
        def _():
            o_ref[...] = s

        @pl.when(i > 0)
        def _():
            o_ref[...] += s

    return pl.pallas_call(
        body, name=name, grid=(m // tm,),
        in_specs=[pl.BlockSpec((tm, n), lambda i: (i, 0))],
        out_specs=pl.BlockSpec((1, n), lambda i: (0, 0)),
        out_shape=jax.ShapeDtypeStruct((1, n), F32),
        compiler_params=_cparams(("arbitrary",)),
    )(x)


def _view(arr, width=None, col=0, off=0):
    return (arr, arr.shape[1] if width is None else width, col, off)


def _row_spec(view):
    _, width, col, off = view
    return pl.BlockSpec((BLOCK, width), lambda i, col=col, off=off: (i + off, col))


def _const_spec(arr):
    return pl.BlockSpec(arr.shape, lambda i: (0,) * arr.ndim)


def _rows_fwd(fn, rows, consts, out_widths, *, nblk, name):
    nr, nc = len(rows), len(consts)

    def body(*refs):
        i = pl.program_id(0)
        vals = [r[...] for r in refs[:nr + nc]]
        outs = fn(i, *vals)
        for o_ref, o in zip(refs[nr + nc:], outs):
            o_ref[...] = o

    return pl.pallas_call(
        body, name=name, grid=(nblk,),
        in_specs=[_row_spec(v) for v in rows] + [_const_spec(c) for c in consts],
        out_specs=[pl.BlockSpec((BLOCK, w), lambda i: (i, 0)) for w in out_widths],
        out_shape=[jax.ShapeDtypeStruct((nblk * BLOCK, w), F32) for w in out_widths],
        compiler_params=_cparams(("parallel",)),
    )(*[v[0] for v in rows], *consts)


def _rows_bwd(fn, rows, consts, cts, *, nblk, name, diff_rows, diff_consts, acc=None, fwd_widths=()):
    nr, nc = len(rows), len(consts)
    ct_views = [c for c in cts if c is not None]
    acc = acc or [None] * len(diff_rows)
    acc_views = [a for a in acc if a is not None]
    n_in = nr + nc + len(ct_views) + len(acc_views)
    n_fwd = len(fwd_widths)

    def body(*refs):
        i = pl.program_id(0)
        row_vals = [r[...] for r in refs[:nr]]
        const_vals = [r[...] for r in refs[nr:nr + nc]]
        ct_vals = [r[...] for r in refs[nr + nc:nr + nc + len(ct_views)]]
        acc_vals = [r[...] for r in refs[nr + nc + len(ct_views):n_in]]
        out_refs = refs[n_in:]

        def f(*dargs):
            rv = list(row_vals)
            cv = list(const_vals)
            for pos, idx in enumerate(diff_rows):
                rv[idx] = dargs[pos]
            for pos, idx in enumerate(diff_consts):
                cv[idx] = dargs[len(diff_rows) + pos]
            return tuple(fn(i, *rv, *cv))

        primals = [row_vals[idx] for idx in diff_rows] + [const_vals[idx] for idx in diff_consts]
        outs, pull = jax.vjp(f, *primals)
        full_ct, ci = [], 0
        for o, c in zip(outs, cts):
            if c is None:
                full_ct.append(jnp.zeros_like(o))
            else:
                full_ct.append(ct_vals[ci])
                ci += 1
        grads = pull(tuple(full_ct))
        for o_ref, o in zip(out_refs[:n_fwd], outs):
            o_ref[...] = o
        ai = 0
        for pos in range(len(diff_rows)):
            g = grads[pos]
            if acc[pos] is not None:
                g = g + acc_vals[ai]
                ai += 1
            out_refs[n_fwd + pos][...] = g
        for pos in range(len(diff_consts)):
            g = grads[len(diff_rows) + pos]
            o_ref = out_refs[n_fwd + len(diff_rows) + pos]

            @pl.when(i == 0)
            def _(o_ref=o_ref, g=g):
                o_ref[...] = g

            @pl.when(i > 0)
            def _(o_ref=o_ref, g=g):
                o_ref[...] += g

    out_specs = [pl.BlockSpec((BLOCK, w), lambda i: (i, 0)) for w in fwd_widths]
    out_shape = [jax.ShapeDtypeStruct((nblk * BLOCK, w), F32) for w in fwd_widths]
    for idx in diff_rows:
        out_specs.append(pl.BlockSpec((BLOCK, rows[idx][1]), lambda i: (i, 0)))
        out_shape.append(jax.ShapeDtypeStruct((nblk * BLOCK, rows[idx][1]), F32))
    for idx in diff_consts:
        out_specs.append(_const_spec(consts[idx]))
        out_shape.append(jax.ShapeDtypeStruct(consts[idx].shape, F32))
    return pl.pallas_call(
        body, name=name, grid=(nblk,),
        in_specs=([_row_spec(v) for v in rows] + [_const_spec(c) for c in consts]
                  + [_row_spec(v) for v in ct_views] + [_row_spec(v) for v in acc_views]),
        out_specs=out_specs, out_shape=out_shape,
        compiler_params=_cparams(("arbitrary",)),
    )(*[v[0] for v in rows], *consts, *[v[0] for v in ct_views], *[v[0] for v in acc_views])


def _rms_fn(i, x, g):
    return (x * lax.rsqrt(jnp.mean(x * x, axis=-1, keepdims=True) + RMS_EPS) * g,)


def _sigmoid(x):
    return 1.0 / (1.0 + jnp.exp(-x))


def _softplus(x):
    return jnp.maximum(x, 0.0) + jnp.log(1.0 + jnp.exp(-jnp.abs(x)))


def _split2(x):
    hi = x.astype(BF16)
    lo = (x - hi.astype(F32)).astype(BF16)
    return jnp.concatenate([hi, lo], axis=1)


@jax.custom_vjp
def _head_sum(x):
    r = lax.broadcasted_iota(jnp.int32, (2 * RW_DIM, RW_DIM), 0) % RW_DIM // HEAD_DIM
    c = lax.broadcasted_iota(jnp.int32, (2 * RW_DIM, RW_DIM), 1) // HEAD_DIM
    return jnp.dot(_split2(x), (r == c).astype(BF16), preferred_element_type=F32)


_head_sum.defvjp(lambda x: (_head_sum(x), None), lambda _, ct: (_head_sum(ct),))


@jax.custom_vjp
def _dot_bf16(x, w):
    return jnp.dot(x.astype(BF16), w.astype(BF16), preferred_element_type=F32)


def _dot_bf16_bwd(res, ct):
    x, w = res
    ct = ct.astype(BF16)
    dx = lax.dot_general(ct, w.astype(BF16), (((1,), (1,)), ((), ())), preferred_element_type=F32)
    dw = lax.dot_general(x.astype(BF16), ct, (((0,), (0,)), ((), ())), preferred_element_type=F32)
    return dx, dw


_dot_bf16.defvjp(lambda x, w: (_dot_bf16(x, w), (x, w)), _dot_bf16_bwd)


def _rwkv_pre_fn(i, r, k, v, dw, da, dg, r_p, k_p, v_p, dw_p, da_p, dg_p,
                 mix_r, mix_k, mix_v, mix_dw, mix_da, mix_dg, w0, w2, a0, a2, g2, k_k, k_a):
    row = i * BLOCK + lax.broadcasted_iota(jnp.int32, (BLOCK, 1), 0)
    live = row >= PAD_ROWS
    live_prev = row >= PAD_ROWS + 1

    def shift(cur, prev, mix):
        cur = jnp.where(live, cur, 0.0)
        prev = jnp.where(live_prev, prev, 0.0)
        return cur + (prev - cur) * mix

    r = shift(r, r_p, mix_r)
    k = shift(k, k_p, mix_k)
    v = shift(v, v_p, mix_v)
    dw = shift(dw, dw_p, mix_dw)
    da = shift(da, da_p, mix_da)
    dg = shift(dg, dg_p, mix_dg)
    wlog = -_softplus(-(w0 + _dot_bf16(jnp.tanh(dw), w2))) - 0.5
    decay = jnp.exp(-jnp.exp(wlog))
    a = _sigmoid(a0 + _dot_bf16(da, a2))
    g = _dot_bf16(_sigmoid(dg), g2)
    kk = k * k_k
    norm_sq = jnp.where(live, _head_sum(kk * kk), 1.0)
    kk = kk / jnp.maximum(jnp.sqrt(norm_sq), 1e-12)
    k_mod = k * (1.0 + (a - 1.0) * k_a)
    return r, decay, k_mod, v, -kk, kk * a, g


def _rwkv_post_fn(i, ys, r, k_mod, v, g, ln_w, ln_b, r_k):
    mean = _head_sum(ys) * (1.0 / HEAD_DIM)
    d = ys - mean
    var = _head_sum(d * d) * (1.0 / HEAD_DIM)
    yn = d * lax.rsqrt(var + RWKV_LN_EPS) * ln_w + ln_b
    bonus = _head_sum(r * k_mod * r_k) * v
    return ((yn + bonus) * g,)


def _merge_fn(i, ya, yr, g1, g2):
    return (_sigmoid(g1) * ya + _sigmoid(g2) * yr,)


def _swiglu_fn(i, gate, up):
    return (gate * _sigmoid(gate) * up,)


def _loss_fn(i, h, tgt, g):
    y = h * lax.rsqrt(jnp.mean(h * h, axis=-1, keepdims=True) + RMS_EPS) * g
    err = y - tgt
    return (0.5 * jnp.mean(err * err, axis=-1, keepdims=True),)


def _rope_tables(lp):
    pos = (jnp.arange(lp, dtype=jnp.int32) - PAD_ROWS).astype(F32)
    inv_freq = jnp.power(jnp.float32(ROPE_THETA), -jnp.arange(ROPE_HALF, dtype=F32) * (2.0 / ROPE_DIM))
    ang = pos[:, None] * inv_freq[None, :]
    cos, sin = jnp.cos(ang), jnp.sin(ang)
    one = jnp.ones((lp, HEAD_DIM - ROPE_DIM), F32)
    zero_h = jnp.zeros((lp, ROPE_HALF), F32)
    zero_r = jnp.zeros((lp, HEAD_DIM - ROPE_DIM), F32)
    c = jnp.concatenate([cos, cos, one], axis=1)
    s1 = jnp.concatenate([-sin, zero_h, zero_r], axis=1)
    s2 = jnp.concatenate([zero_h, sin, zero_r], axis=1)
    return tuple(jnp.tile(t, (1, LANES // HEAD_DIM)) for t in (c, s1, s2))


def _rope_fwd_fn(i, x, c, s1, s2):
    n = x.shape[1]
    c, s1, s2 = [jnp.tile(t, (1, n // LANES)) for t in (c, s1, s2)]
    return (x * c + pltpu.roll(x, n - ROPE_HALF, 1) * s1 + pltpu.roll(x, ROPE_HALF, 1) * s2,)


def _rope_bwd_fn(i, dy, c, s1, s2):
    n = dy.shape[1]
    c, s1, s2 = [jnp.tile(t, (1, n // LANES)) for t in (c, s1, s2)]
    return (dy * c + pltpu.roll(dy * s1, ROPE_HALF, 1) + pltpu.roll(dy * s2, n - ROPE_HALF, 1),)


def _attn_mask(i):
    r = lax.broadcasted_iota(jnp.int32, (BLOCK, 3 * BLOCK), 0)
    c = lax.broadcasted_iota(jnp.int32, (BLOCK, 3 * BLOCK), 1)
    meta = (c < BLOCK) & (c >= PAD_ROWS)
    prev = (c >= BLOCK) & (c < 2 * BLOCK) & ((c - BLOCK) > r) & (i >= 1)
    cur = (c >= 2 * BLOCK) & ((c - 2 * BLOCK) <= r)
    return meta | prev | cur


def _attn_fwd(q, k, v, sinks, *, nblk, name):
    scale = HEAD_DIM ** -0.5

    def body(q_ref, km_ref, kp_ref, kc_ref, vm_ref, vp_ref, vc_ref, s_ref, o_ref, lse_ref):
        i = pl.program_id(0)
        valid = _attn_mask(i)
        for h in range(Q_HEADS):
            g = h // GROUP
            kcat = jnp.concatenate([km_ref[g], kp_ref[g], kc_ref[g]], axis=0).astype(BF16)
            vcat = jnp.concatenate([vm_ref[g], vp_ref[g], vc_ref[g]], axis=0).astype(BF16)
            s = lax.dot_general(q_ref[h].astype(BF16), kcat, (((1,), (1,)), ((), ())),
                                preferred_element_type=F32) * scale
            s = jnp.where(valid, s, NEG_INF)
            sink = s_ref[0:1, h:h + 1]
            m = jnp.maximum(jnp.max(s, axis=-1, keepdims=True), sink)
            p = jnp.exp(s - m)
            den = jnp.sum(p, axis=-1, keepdims=True) + jnp.exp(sink - m)
            o = jnp.dot(p.astype(BF16), vcat, preferred_element_type=F32)
            o_ref[h] = o / den
            lse_ref[h] = m + jnp.log(den)

    kv = lambda f: pl.BlockSpec((KV_HEADS, BLOCK, HEAD_DIM), f)
    return pl.pallas_call(
        body, name=name, grid=(nblk,),
        in_specs=[pl.BlockSpec((Q_HEADS, BLOCK, HEAD_DIM), lambda i: (0, i + 1, 0)),
                  kv(lambda i: (0, 0, 0)), kv(lambda i: (0, i, 0)), kv(lambda i: (0, i + 1, 0)),
                  kv(lambda i: (0, 0, 0)), kv(lambda i: (0, i, 0)), kv(lambda i: (0, i + 1, 0)),
                  pl.BlockSpec((1, Q_HEADS), lambda i: (0, 0))],
        out_specs=[pl.BlockSpec((Q_HEADS, BLOCK, HEAD_DIM), lambda i: (0, i, 0)),
                   pl.BlockSpec((Q_HEADS, BLOCK, 1), lambda i: (0, i, 0))],
        out_shape=[jax.ShapeDtypeStruct((Q_HEADS, nblk * BLOCK, HEAD_DIM), F32),
                   jax.ShapeDtypeStruct((Q_HEADS, nblk * BLOCK, 1), F32)],
        compiler_params=_cparams(("parallel",)),
    )(q, k, k, k, v, v, v, sinks)


def _attn_bwd(q, k, v, sinks, o, lse, do, *, nblk, name):
    scale = HEAD_DIM ** -0.5
    lp = k.shape[1]

    def body(q_ref, km_ref, kp_ref, kc_ref, vm_ref, vp_ref, vc_ref, s_ref, o_ref, lse_ref, do_ref,
             dq_ref, dk_ref, dv_ref, ds_ref):
        i = pl.program_id(0)

        @pl.when(i == 0)
        def _():
            dk_ref[...] = jnp.zeros_like(dk_ref)
            dv_ref[...] = jnp.zeros_like(dv_ref)
            ds_ref[...] = jnp.zeros_like(ds_ref)

        valid = _attn_mask(i)
        lane = lax.broadcasted_iota(jnp.int32, (1, Q_HEADS), 1)
        prev_rows = pl.ds(pl.multiple_of(i * BLOCK, BLOCK), BLOCK)
        cur_rows = pl.ds(pl.multiple_of((i + 1) * BLOCK, BLOCK), BLOCK)
        for h in range(Q_HEADS):
            g = h // GROUP
            qh = q_ref[h].astype(BF16)
            doh = do_ref[h]
            kcat = jnp.concatenate([km_ref[g], kp_ref[g], kc_ref[g]], axis=0).astype(BF16)
            vcat = jnp.concatenate([vm_ref[g], vp_ref[g], vc_ref[g]], axis=0).astype(BF16)
            s = lax.dot_general(qh, kcat, (((1,), (1,)), ((), ())), preferred_element_type=F32) * scale
            s = jnp.where(valid, s, NEG_INF)
            lse_h = lse_ref[h]
            p = jnp.exp(s - lse_h)
            delta = jnp.sum(doh * o_ref[h], axis=-1, keepdims=True)
            dp = lax.dot_general(doh.astype(BF16), vcat, (((1,), (1,)), ((), ())), preferred_element_type=F32)
            dsc = (p * (dp - delta) * scale).astype(BF16)
            dq_ref[h] = jnp.dot(dsc, kcat, preferred_element_type=F32)
            dk_all = lax.dot_general(dsc, qh, (((0,), (0,)), ((), ())), preferred_element_type=F32)
            dv_all = lax.dot_general(p.astype(BF16), doh.astype(BF16), (((0,), (0,)), ((), ())),
                                     preferred_element_type=F32)
            dk_ref[g, 0:BLOCK, :] += dk_all[0:BLOCK]
            dk_ref[g, prev_rows, :] += dk_all[BLOCK:2 * BLOCK]
            dk_ref[g, cur_rows, :] += dk_all[2 * BLOCK:]
            dv_ref[g, 0:BLOCK, :] += dv_all[0:BLOCK]
            dv_ref[g, prev_rows, :] += dv_all[BLOCK:2 * BLOCK]
            dv_ref[g, cur_rows, :] += dv_all[2 * BLOCK:]
            p_sink = jnp.exp(s_ref[0:1, h:h + 1] - lse_h)
            dsink = -jnp.sum(p_sink * delta, axis=0, keepdims=True)
            ds_ref[...] += jnp.where(lane == h, dsink, 0.0)

    kv = lambda f: pl.BlockSpec((KV_HEADS, BLOCK, HEAD_DIM), f)
    qspec = pl.BlockSpec((Q_HEADS, BLOCK, HEAD_DIM), lambda i: (0, i, 0))
    whole = pl.BlockSpec((KV_HEADS, lp, HEAD_DIM), lambda i: (0, 0, 0))
    return pl.pallas_call(
        body, name=name, grid=(nblk,),
        in_specs=[pl.BlockSpec((Q_HEADS, BLOCK, HEAD_DIM), lambda i: (0, i + 1, 0)),
                  kv(lambda i: (0, 0, 0)), kv(lambda i: (0, i, 0)), kv(lambda i: (0, i + 1, 0)),
                  kv(lambda i: (0, 0, 0)), kv(lambda i: (0, i, 0)), kv(lambda i: (0, i + 1, 0)),
                  pl.BlockSpec((1, Q_HEADS), lambda i: (0, 0)),
                  qspec, pl.BlockSpec((Q_HEADS, BLOCK, 1), lambda i: (0, i, 0)), qspec],
        out_specs=[qspec, whole, whole, pl.BlockSpec((1, Q_HEADS), lambda i: (0, 0))],
        out_shape=[jax.ShapeDtypeStruct((Q_HEADS, nblk * BLOCK, HEAD_DIM), F32),
                   jax.ShapeDtypeStruct((KV_HEADS, lp, HEAD_DIM), F32),
                   jax.ShapeDtypeStruct((KV_HEADS, lp, HEAD_DIM), F32),
                   jax.ShapeDtypeStruct((1, Q_HEADS), F32)],
        compiler_params=_cparams(("arbitrary",)),
    )(q, k, k, k, v, v, v, sinks, o, lse, do)


N_VEC = 5
VEC_ROWS = N_VEC * HEAD_DIM


def _selectors():
    t = SCAN_T
    shape = (t, 2 * LANES, RW_DIM)
    step, src, dst = [lax.broadcasted_iota(jnp.int32, shape, d) for d in range(3)]
    src = src % LANES
    spread = ((src // t == dst // HEAD_DIM) & (src % t == step)).astype(BF16)
    shape = (t, RW_DIM, LANES)
    step, src, dst = [lax.broadcasted_iota(jnp.int32, shape, d) for d in range(3)]
    collect = ((src // HEAD_DIM == dst // t) & (dst % t == step)).astype(BF16)
    return spread, collect


def _with_exchange(compute, n_in, n_out, n_scratch, scattered, shared, last_step):
    n_sc = len(scattered)
    n_x = n_sc + len(shared)
    if n_x == 0:
        return compute

    def body(*refs):
        ins, x_in = refs[:n_in], refs[n_in:n_in + n_x]
        outs, x_out = refs[n_in + n_x:n_in + n_x + n_out], refs[n_in + n_x + n_out:n_in + 2 * n_x + n_out]
        scratch = refs[n_in + 2 * n_x + n_out:n_in + 2 * n_x + n_out + n_scratch]
        sems = refs[n_in + 2 * n_x + n_out + n_scratch:]

        @pl.when(pl.program_id(0) == 0)
        def _():
            for cp in _exchange_copies(x_in, x_out, n_sc, *sems):
                cp.start()

        compute(*ins, *outs, *scratch)

        @pl.when(pl.program_id(0) == last_step)
        def _():
            for cp in _exchange_copies(x_in, x_out, n_sc, *sems):
                cp.wait()

    return body


def _wkv_fwd(xt, v, spread, name, shared=()):
    nch = xt.shape[0]
    t_steps = SCAN_T
    n_x = len(shared)

    def compute(xt_ref, v_ref, sel_ref, y_ref, hist_ref, st_ref):
        @pl.when(pl.program_id(0) == 0)
        def _():
            st_ref[...] = jnp.zeros_like(st_ref)

        x2 = _split2(xt_ref[0])
        st = st_ref[...]
        for j in range(t_steps):
            cols = jnp.dot(x2, sel_ref[j], preferred_element_type=F32)
            a_c, w_c, b_c, k_c, r_c = [cols[n * HEAD_DIM:(n + 1) * HEAD_DIM] for n in range(N_VEC)]
            hist_ref[j] = st
            sa = jnp.sum(st * a_c, axis=0, keepdims=True)
            st = st * w_c + b_c * sa + k_c * v_ref[j:j + 1, :]
            y_ref[j:j + 1, :] = jnp.sum(st * r_c, axis=0, keepdims=True)
        st_ref[...] = st

    return pl.pallas_call(
        _with_exchange(compute, 3, 2, 1, (), shared, nch - 1), name=name, grid=(nch,),
        in_specs=[pl.BlockSpec((1, VEC_ROWS, LANES), lambda c: (c, 0, 0)),
                  pl.BlockSpec((t_steps, RW_DIM), lambda c: (c, 0)),
                  pl.BlockSpec(spread.shape, lambda c: (0, 0, 0))] + [ANY] * n_x,
        out_specs=[pl.BlockSpec((t_steps, RW_DIM), lambda c: (c, 0)),
                   pl.BlockSpec((t_steps, HEAD_DIM, RW_DIM), lambda c: (c, 0, 0))] + [ANY] * n_x,
        out_shape=[jax.ShapeDtypeStruct((nch * t_steps, RW_DIM), F32),
                   jax.ShapeDtypeStruct((nch * t_steps, HEAD_DIM, RW_DIM), F32)] + _exchange_shapes((), shared),
        scratch_shapes=[pltpu.VMEM((HEAD_DIM, RW_DIM), F32)] + (_exchange_sems(n_x) if n_x else []),
        compiler_params=_cparams(("arbitrary",)),
    )(xt, v, spread, *shared)


def _wkv_bwd(xt, v, hist, dy, spread, collect, name, scattered=()):
    nch = xt.shape[0]
    t_steps = SCAN_T
    n_x = len(scattered)

    def compute(xt_ref, v_ref, hist_ref, dy_ref, sel_ref, col_ref, *rest):
        d_refs, dv_ref, g_ref = rest[:N_VEC], rest[N_VEC], rest[N_VEC + 1]
        @pl.when(pl.program_id(0) == 0)
        def _():
            g_ref[...] = jnp.zeros_like(g_ref)

        x2 = _split2(xt_ref[0])
        gst = g_ref[...]
        acc = jnp.zeros((VEC_ROWS, LANES), F32)
        nxt = None
        for j in reversed(range(t_steps)):
            cols = jnp.dot(x2, sel_ref[j], preferred_element_type=F32)
            a_c, w_c, b_c, k_c, r_c = [cols[n * HEAD_DIM:(n + 1) * HEAD_DIM] for n in range(N_VEC)]
            prev = hist_ref[j]
            v_row = v_ref[j:j + 1, :]
            dy_row = dy_ref[j:j + 1, :]
            sa = jnp.sum(prev * a_c, axis=0, keepdims=True)
            if nxt is None:
                nxt = prev * w_c + b_c * sa + k_c * v_row
            gst = gst + r_c * dy_row
            dv_ref[j:j + 1, :] = jnp.sum(gst * k_c, axis=0, keepdims=True)
            dsa = jnp.sum(gst * b_c, axis=0, keepdims=True)
            prods = jnp.concatenate([p.astype(BF16) for p in
                                     (prev * dsa, gst * prev, gst * sa, gst * v_row, nxt * dy_row)], axis=0)
            acc = acc + jnp.dot(prods, col_ref[j], preferred_element_type=F32)
            gst = gst * w_c + a_c * dsa
            nxt = prev
        t = t_steps
        tile_k = (lax.broadcasted_iota(jnp.int32, (2 * HEAD_DIM, RW_DIM), 0) % HEAD_DIM
                  == lax.broadcasted_iota(jnp.int32, (2 * HEAD_DIM, RW_DIM), 1) % HEAD_DIM).astype(BF16)
        own_head = (lax.broadcasted_iota(jnp.int32, (LANES, RW_DIM), 0) // t
                    == lax.broadcasted_iota(jnp.int32, (LANES, RW_DIM), 1) // HEAD_DIM).astype(F32)
        pick_j = (lax.broadcasted_iota(jnp.int32, (t, 2 * LANES), 0)
                  == lax.broadcasted_iota(jnp.int32, (t, 2 * LANES), 1) % t).astype(BF16)
        for n in range(N_VEC):
            a_n = acc[n * HEAD_DIM:(n + 1) * HEAD_DIM]
            hi = a_n.astype(BF16)
            both = jnp.concatenate([hi, (a_n - hi.astype(F32)).astype(BF16)], axis=0)
            wide = lax.dot_general(both, tile_k, (((0,), (0,)), ((), ())), preferred_element_type=F32) * own_head
            hi = wide.astype(BF16)
            both = jnp.concatenate([hi, (wide - hi.astype(F32)).astype(BF16)], axis=0)
            d_refs[n][...] = jnp.dot(pick_j, both, preferred_element_type=F32)
        g_ref[...] = gst

    rev3 = lambda c: (nch - 1 - c, 0, 0)
    rev2 = lambda c: (nch - 1 - c, 0)
    rowspec = pl.BlockSpec((t_steps, RW_DIM), rev2)
    return pl.pallas_call(
        _with_exchange(compute, 6, N_VEC + 1, 1, scattered, (), nch - 1), name=name, grid=(nch,),
        in_specs=[pl.BlockSpec((1, VEC_ROWS, LANES), rev3), rowspec,
                  pl.BlockSpec((t_steps, HEAD_DIM, RW_DIM), rev3), rowspec,
                  pl.BlockSpec(spread.shape, lambda c: (0, 0, 0)),
                  pl.BlockSpec(collect.shape, lambda c: (0, 0, 0))] + [ANY] * n_x,
        out_specs=[rowspec] * (N_VEC + 1) + [ANY] * n_x,
        out_shape=([jax.ShapeDtypeStruct((nch * t_steps, RW_DIM), F32)] * (N_VEC + 1)
                   + _exchange_shapes(scattered, ())),
        scratch_shapes=[pltpu.VMEM((HEAD_DIM, RW_DIM), F32)] + (_exchange_sems(n_x) if n_x else []),
        compiler_params=_cparams(("arbitrary",)),
    )(xt, v, hist, dy, spread, collect, *scattered)


def _to_xt(vecs, nch):
    parts = [x.reshape(nch, SCAN_T, RW_HEADS, HEAD_DIM).transpose(0, 3, 2, 1).reshape(nch, HEAD_DIM, LANES)
             for x in vecs]
    return jnp.concatenate(parts, axis=1)


MESH = pl.DeviceIdType.MESH
ANY = pl.BlockSpec(memory_space=pl.ANY)


def _all_gather(arrays, name):
    n_arr = len(arrays)
    per = N_DEV - 1

    def body(*refs):
        x_refs, out_refs = refs[:n_arr], refs[n_arr:2 * n_arr]
        send_sems, recv_sems, local_sems = refs[2 * n_arr:]
        xi, yi, ci = lax.axis_index("x"), lax.axis_index("y"), lax.axis_index("c")
        me, sibling = (xi, yi, ci), (xi, yi, 1 - ci)
        chips = [(1 - xi, yi), (xi, 1 - yi), (1 - xi, 1 - yi)]

        def slot(a, px, py, pc):
            return out_refs[a].at[4 * px + 2 * py + pc]

        def copy(a, sem, block, to, src=None):
            return pltpu.make_async_remote_copy(
                src_ref=slot(a, *block) if src is None else src, dst_ref=slot(a, *block),
                send_sem=send_sems.at[per * a + sem], recv_sem=recv_sems.at[per * a + sem],
                device_id=to, device_id_type=MESH)

        mine = [pltpu.make_async_copy(x_refs[a], slot(a, *me), local_sems.at[a]) for a in range(n_arr)]
        for cp in mine:
            cp.start()
        sent = []
        for a in range(n_arr):
            sent.append(copy(a, 0, me, sibling, src=x_refs[a]))
            sent += [copy(a, 1 + j, me, (*chip, ci), src=x_refs[a]) for j, chip in enumerate(chips)]
        for cp in sent:
            cp.start()
        for j, chip in enumerate(chips):
            for a in range(n_arr):
                copy(a, 1 + j, (*chip, ci), me).wait_recv()
                onward = copy(a, 4 + j, (*chip, ci), sibling)
                onward.start()
                sent.append(onward)
        for a in range(n_arr):
            copy(a, 0, sibling, me).wait_recv()
        for j, chip in enumerate(chips):
            for a in range(n_arr):
                copy(a, 4 + j, (*chip, 1 - ci), me).wait_recv()
        for cp in sent:
            cp.wait_send()
        for cp in mine:
            cp.wait()

    sems = pltpu.SemaphoreType.DMA((per * n_arr,))
    return pl.pallas_call(
        body, name=name, out_shape=[jax.ShapeDtypeStruct((N_DEV,) + a.shape, a.dtype) for a in arrays],
        in_specs=[ANY] * n_arr, out_specs=[ANY] * n_arr,
        scratch_shapes=[sems, sems, pltpu.SemaphoreType.DMA((n_arr,))],
    )(*arrays)


def _exchange_copies(in_refs, out_refs, n_scattered, send_sems, recv_sems, local_sems):
    n_arr = len(in_refs)
    per = N_DEV - 1
    xi, yi, ci = lax.axis_index("x"), lax.axis_index("y"), lax.axis_index("c")
    me = 4 * xi + 2 * yi + ci
    src_of = lambda a, peer: in_refs[a].at[peer] if a < n_scattered else in_refs[a]
    copies = []
    for d in range(1, N_DEV):
        px = 1 - xi if d & 4 else xi
        py = 1 - yi if d & 2 else yi
        pc = 1 - ci if d & 1 else ci
        for a in range(n_arr):
            copies.append(pltpu.make_async_remote_copy(
                src_ref=src_of(a, 4 * px + 2 * py + pc), dst_ref=out_refs[a].at[me],
                send_sem=send_sems.at[per * a + d - 1], recv_sem=recv_sems.at[per * a + d - 1],
                device_id=(px, py, pc), device_id_type=MESH))
    own = [pltpu.make_async_copy(src_of(a, me), out_refs[a].at[me], local_sems.at[a]) for a in range(n_arr)]
    return copies + own


def _exchange_shapes(scattered, shared):
    return ([jax.ShapeDtypeStruct(a.shape, a.dtype) for a in scattered]
            + [jax.ShapeDtypeStruct((N_DEV,) + a.shape, a.dtype) for a in shared])


def _exchange_sems(n_arr):
    sems = pltpu.SemaphoreType.DMA(((N_DEV - 1) * n_arr,))
    return [sems, sems, pltpu.SemaphoreType.DMA((n_arr,))]


def _exchange(scattered, shared, name):
    n_sc = len(scattered)
    n_arr = n_sc + len(shared)

    def body(*refs):
        copies = _exchange_copies(refs[:n_arr], refs[n_arr:2 * n_arr], n_sc, *refs[2 * n_arr:])
        for cp in copies:
            cp.start()
        for cp in copies:
            cp.wait()

    return pl.pallas_call(
        body, name=name, out_shape=_exchange_shapes(scattered, shared),
        in_specs=[ANY] * n_arr, out_specs=[ANY] * n_arr, scratch_shapes=_exchange_sems(n_arr),
    )(*scattered, *shared)


def _adamw(parts, w, m, v, name):
    rows, cols = w.shape
    tile = PACK_ROWS if rows % PACK_ROWS == 0 else rows

    def body(p_ref, w_ref, m_ref, v_ref, g_out, d_out, m_out, v_out):
        g = p_ref[0].astype(F32)
        for s in range(1, N_DEV):
            g = g + p_ref[s].astype(F32)
        m_new = ADAM_B1 * m_ref[...] + (1.0 - ADAM_B1) * g
        v_new = ADAM_B2 * v_ref[...] + (1.0 - ADAM_B2) * (g * g)
        m_hat = m_new / (1.0 - ADAM_B1 ** ADAM_STEP)
        v_hat = v_new / (1.0 - ADAM_B2 ** ADAM_STEP)
        g_out[...] = g
        d_out[...] = -ADAM_LR * (m_hat / (jnp.sqrt(v_hat) + ADAM_EPS) + ADAM_WD * w_ref[...])
        m_out[...] = m_new
        v_out[...] = v_new

    spec = pl.BlockSpec((tile, cols), lambda i: (i, 0))
    return pl.pallas_call(
        body, name=name, grid=(rows // tile,),
        in_specs=[pl.BlockSpec((N_DEV, tile, cols), lambda i: (0, i, 0)), spec, spec, spec],
        out_specs=[spec] * 4, out_shape=[jax.ShapeDtypeStruct((rows, cols), F32)] * 4,
        compiler_params=_cparams(("parallel",)),
    )(parts, w, m, v)


EARLY = [("meta_tokens", 1), ("rwkv_w2", 1), ("rwkv_a2", 1), ("rwkv_g2", 1)]
LATE = [("w_br_attn", 1), ("w_br_rwkv", 1), ("w_o", 0), ("w_ffn_gate", 1), ("w_ffn_up", 1), ("w_ffn_down", 0)]
REPLICATED = ["norm_mix_g", "b_in", "attn_sinks", "rwkv_mix", "rwkv_w0", "rwkv_a0", "rwkv_k_k", "rwkv_k_a",
              "rwkv_r_k", "rwkv_ln_w", "rwkv_ln_b", "norm_ffn_g", "norm_final_g"]
WEIGHTS = ["meta_tokens", "norm_mix_g", "w_in", "b_in", "attn_sinks", "rwkv_mix", "rwkv_w0", "rwkv_w2", "rwkv_a0",
           "rwkv_a2", "rwkv_g2", "rwkv_k_k", "rwkv_k_a", "rwkv_r_k", "rwkv_ln_w", "rwkv_ln_b", "w_br_attn",
           "w_br_rwkv", "w_o", "norm_ffn_g", "w_ffn_gate", "w_ffn_up", "w_ffn_down", "norm_final_g"]


def _pack(arrays, row_multiple):
    flat = jnp.concatenate([a.reshape(-1) for a in arrays])
    per = row_multiple * LANES
    total = -(-flat.shape[0] // per) * per
    return jnp.pad(flat, (0, total - flat.shape[0])).reshape(-1, LANES)


def _unpack(buf, shapes):
    flat = buf.reshape(-1)
    out, pos = [], 0
    for s in shapes:
        n = 1
        for d in s:
            n *= d
        out.append(flat[pos:pos + n].reshape(s))
        pos += n
    return out


def _strip(name, a):
    return a if name in ("meta_tokens", "norm_final_g") else a[0]


def _join(gathered, axis):
    if axis == 0:
        return gathered.reshape(-1, gathered.shape[2])
    return gathered.transpose(1, 0, 2).reshape(gathered.shape[1], -1)


def _split(g, axis):
    if axis == 0:
        return g.reshape(N_DEV, -1, g.shape[1])
    return g.reshape(g.shape[0], N_DEV, -1).transpose(1, 0, 2)


W_IN_LAYOUT = [(768, 2304), (0, 512), (2592, 4640), (2432, 2592), 256 - GATE_LORA, (512, 768), (2304, 2368),
               128 - DECAY_LORA, (2368, 2432), 128 - AAA_LORA, NP - C_DA - 128]


def _w_in_padded(w, shard_width=None):
    rows = w.shape[-2]
    width = D_IN if shard_width is None else shard_width
    parts = []
    for seg in W_IN_LAYOUT:
        if isinstance(seg, int):
            parts.append(jnp.zeros((rows, seg), w.dtype))
            continue
        lo, stop = seg
        while lo < stop:
            p = lo // width
            hi = min(stop, (p + 1) * width)
            src = w if shard_width is None else w[p]
            parts.append(src[:, lo - p * width:hi - p * width])
            lo = hi
    return jnp.concatenate(parts, axis=1)


def _w_in_unpadded(wp, lo=0, stop=D_IN):
    spans, pos = [], 0
    for seg in W_IN_LAYOUT:
        if isinstance(seg, int):
            pos += seg
        else:
            spans.append((seg[0], seg[1], pos))
            pos += seg[1] - seg[0]
    parts = []
    for a, b, at in sorted(spans):
        c, d = max(a, lo), min(b, stop)
        if c < d:
            parts.append(wp[:, at + c - a:at + d - a])
    return jnp.concatenate(parts, axis=1)


def _pad_rows(a, n):
    return jnp.pad(a, ((0, n - a.shape[0]), (0, 0)))


def _device_step(x, tgt, full, gather_late=None, scatter_early=None):
    seq = x.shape[0]
    nblk = seq // BLOCK
    lp = seq + BLOCK
    nall = nblk + 1
    nch = lp // SCAN_T

    w_in_p = full["w_in_p"]
    b_in_p = _w_in_padded(full["b_in"][None])
    mix = full["rwkv_mix"][None]
    mix_r, mix_k, mix_v = mix[:, 0:512], mix[:, 512:1024], mix[:, 1024:1536]
    mix_dw = jnp.pad(mix[:, 1536:1600], ((0, 0), (0, 64)))
    mix_da = jnp.pad(mix[:, 1600:1664], ((0, 0), (0, 64)))
    mix_dg = jnp.pad(mix[:, 1664:1824], ((0, 0), (0, 96)))
    w2_p = _pad_rows(full["rwkv_w2"].astype(F32), 128)
    a2_p = _pad_rows(full["rwkv_a2"].astype(F32), 128)
    g2_p = _pad_rows(full["rwkv_g2"].astype(F32), 256)
    row = lambda name: full[name].reshape(1, -1)
    sinks = row("attn_sinks")
    rope_c, rope_s1, rope_s2 = _rope_tables(lp)

    hpad = jnp.concatenate([jnp.zeros((PAD_ROWS, D_MODEL), F32), full["meta_tokens"].astype(F32), x], axis=0)
    (u,) = _rows_fwd(_rms_fn, [_view(hpad)], [row("norm_mix_g")], [D_MODEL], nblk=nall, name="norm_mix")
    proj = _mm(u, w_in_p, bias=b_in_p, name="in_proj")
    (q_r,) = _rows_fwd(_rope_fwd_fn, [_view(proj, 512, C_Q // 512), _view(rope_c), _view(rope_s1), _view(rope_s2)],
                       [], [512], nblk=nall, name="rope_q")
    (k_r,) = _rows_fwd(_rope_fwd_fn, [_view(proj, 128, C_KA // 128), _view(rope_c), _view(rope_s1),
                                      _view(rope_s2)], [], [128], nblk=nall, name="rope_k")
    heads = lambda a, nh: a.reshape(a.shape[0], nh, HEAD_DIM).transpose(1, 0, 2)
    unheads = lambda a: a.transpose(1, 0, 2).reshape(a.shape[1], -1)
    q_h, k_h = heads(q_r, Q_HEADS), heads(k_r, KV_HEADS)
    v_h = heads(proj[:, C_VA:C_VA + 128], KV_HEADS)
    o_h, lse = _attn_fwd(q_h, k_h, v_h, sinks, nblk=nblk, name="attn_fwd")
    y_attn = unheads(o_h)

    rw_cols = jnp.concatenate([proj[:, C_R:C_R + 1536], proj[:, C_DG:C_DG + 256], proj[:, C_DW:C_DW + 256]], axis=1)
    rw_prev = jnp.pad(rw_cols[:-1], ((1, 0), (0, 0)))
    pre_rows = [_view(proj, 512, 0), _view(proj, 512, 1), _view(proj, 512, 2), _view(proj, 128, C_DW // 128),
                _view(proj, 128, C_DA // 128), _view(proj, 256, C_DG // 256),
                _view(rw_prev, 512, 0), _view(rw_prev, 512, 1), _view(rw_prev, 512, 2), _view(rw_prev, 128, 14),
                _view(rw_prev, 128, 15), _view(rw_prev, 256, 6)]
    pre_consts = [mix_r, mix_k, mix_v, mix_dw, mix_da, mix_dg, row("rwkv_w0"), w2_p, row("rwkv_a0"), a2_p, g2_p,
                  row("rwkv_k_k"), row("rwkv_k_a")]
    r_t, decay, k_mod, v_t, a_neg, b_t, gate = _rows_fwd(_rwkv_pre_fn, pre_rows, pre_consts, [RW_DIM] * 7,
                                                         nblk=nall, name="rwkv_pre")
    spread, collect = _selectors()
    xt = _to_xt((a_neg, decay, b_t, k_mod, r_t), nch)
    y_scan, hist, *late = _wkv_fwd(xt, v_t, spread, name="wkv_fwd", shared=gather_late[0] if gather_late else ())
    if gather_late:
        full = {**full, **gather_late[1](late)}
    post_rows = [_view(y_scan, off=1), _view(r_t, off=1), _view(k_mod, off=1), _view(v_t, off=1), _view(gate, off=1)]
    post_consts = [row("rwkv_ln_w"), row("rwkv_ln_b"), row("rwkv_r_k")]
    (y_rwkv,) = _rows_fwd(_rwkv_post_fn, post_rows, post_consts, [RW_DIM], nblk=nblk, name="rwkv_post")

    ya = _mm(y_attn, full["w_br_attn"], name="br_attn")
    yr = _mm(y_rwkv, full["w_br_rwkv"], name="br_rwkv")
    merge_rows = [_view(ya), _view(yr), _view(proj, 1024, C_G1 // 1024, 1), _view(proj, 1024, C_G2 // 1024, 1)]
    (merged,) = _rows_fwd(_merge_fn, merge_rows, [], [D_MODEL], nblk=nblk, name="merge")
    h1 = _mm(merged, full["w_o"], residual=x, name="out_proj")
    (f,) = _rows_fwd(_rms_fn, [_view(h1)], [row("norm_ffn_g")], [D_MODEL], nblk=nblk, name="norm_ffn")
    ff_gate = _mm(f, full["w_ffn_gate"], name="ffn_gate")
    ff_up = _mm(f, full["w_ffn_up"], name="ffn_up")
    (act,) = _rows_fwd(_swiglu_fn, [_view(ff_gate), _view(ff_up)], [], [D_FF], nblk=nblk, name="swiglu")
    h2 = _mm(act, full["w_ffn_down"], residual=h1, name="ffn_down")

    grads = {}
    ones_col = jnp.ones((seq, 1), F32)
    loss_rows, dh2, grads["norm_final_g"] = _rows_bwd(
        _loss_fn, [_view(h2), _view(tgt)], [row("norm_final_g")], [_view(ones_col)], nblk=nblk, name="loss",
        diff_rows=[0], diff_consts=[0], fwd_widths=[1])
    loss = jnp.sum(loss_rows)

    dact = _mm(dh2, full["w_ffn_down"], tb=True, name="d_act")
    grads["w_ffn_down"] = _mm(act, dh2, ta=True, name="dw_ffn_down")
    dgate, dup = _rows_bwd(_swiglu_fn, [_view(ff_gate), _view(ff_up)], [], [_view(dact)], nblk=nblk,
                           name="swiglu_bwd", diff_rows=[0, 1], diff_consts=[])
    grads["w_ffn_gate"] = _mm(f, dgate, ta=True, name="dw_ffn_gate")
    grads["w_ffn_up"] = _mm(f, dup, ta=True, name="dw_ffn_up")
    df = _mm(dgate, full["w_ffn_gate"], tb=True, name="df_gate")
    df = _mm(dup, full["w_ffn_up"], tb=True, residual=df, name="df_up")
    dh1, grads["norm_ffn_g"] = _rows_bwd(_rms_fn, [_view(h1)], [row("norm_ffn_g")], [_view(df)], nblk=nblk,
                                         name="norm_ffn_bwd", diff_rows=[0], diff_consts=[0], acc=[_view(dh2)])
    dmerged = _mm(dh1, full["w_o"], tb=True, name="d_merged")
    grads["w_o"] = _mm(merged, dh1, ta=True, name="dw_o")
    dya, dyr, dg1, dg2 = _rows_bwd(_merge_fn, merge_rows, [], [_view(dmerged)], nblk=nblk, name="merge_bwd",
                                   diff_rows=[0, 1, 2, 3], diff_consts=[])
    grads["w_br_attn"] = _mm(y_attn, dya, ta=True, name="dw_br_attn")
    grads["w_br_rwkv"] = _mm(y_rwkv, dyr, ta=True, name="dw_br_rwkv")
    dy_attn = _mm(dya, full["w_br_attn"], tb=True, name="d_y_attn")
    dy_rwkv = _mm(dyr, full["w_br_rwkv"], tb=True, name="d_y_rwkv")

    post = _rows_bwd(_rwkv_post_fn, post_rows, post_consts, [_view(dy_rwkv)], nblk=nblk, name="rwkv_post_bwd",
                     diff_rows=[0, 1, 2, 3, 4], diff_consts=[0, 1, 2])
    dys, dr_post, dk_post, dv_post, dgate_post = [jnp.pad(t, ((BLOCK, 0), (0, 0))) for t in post[:5]]
    grads["rwkv_ln_w"], grads["rwkv_ln_b"], grads["rwkv_r_k"] = post[5:]
    da_s, dw_s, db_s, dk_s, dr_s, dv_s, *early_parts = _wkv_bwd(
        xt, v_t, hist, dys, spread, collect, name="wkv_bwd", scattered=scatter_early(grads) if scatter_early else ())
    pre_cts = [_view(dr_s + dr_post), _view(dw_s), _view(dk_s + dk_post), _view(dv_s + dv_post), _view(da_s),
               _view(db_s), _view(dgate_post)]
    pre = _rows_bwd(_rwkv_pre_fn, pre_rows, pre_consts, pre_cts, nblk=nall, name="rwkv_pre_bwd",
                    diff_rows=list(range(12)), diff_consts=list(range(13)))
    d_cur, d_prev, d_par = pre[0:6], pre[6:12], pre[12:]
    up = lambda t: jnp.pad(t[1:], ((0, 1), (0, 0)))
    d_rw = [c + up(p) for c, p in zip(d_cur, d_prev)]
    grads["rwkv_mix"] = jnp.concatenate([d_par[0], d_par[1], d_par[2], d_par[3][:, :DECAY_LORA],
                                         d_par[4][:, :AAA_LORA], d_par[5][:, :GATE_LORA]], axis=1)
    grads["rwkv_w0"], grads["rwkv_w2"] = d_par[6], d_par[7][:DECAY_LORA]
    grads["rwkv_a0"], grads["rwkv_a2"] = d_par[8], d_par[9][:AAA_LORA]
    grads["rwkv_g2"] = d_par[10][:GATE_LORA]
    grads["rwkv_k_k"], grads["rwkv_k_a"] = d_par[11], d_par[12]

    do_h = heads(dy_attn, Q_HEADS)
    dq_h, dk_h, dv_h, grads["attn_sinks"] = _attn_bwd(q_h, k_h, v_h, sinks, o_h, lse, do_h, nblk=nblk,
                                                      name="attn_bwd")
    dq_r = jnp.pad(unheads(dq_h), ((BLOCK, 0), (0, 0)))
    (dq,) = _rows_fwd(_rope_bwd_fn, [_view(dq_r), _view(rope_c), _view(rope_s1), _view(rope_s2)], [], [512],
                      nblk=nall, name="rope_q_bwd")
    (dka,) = _rows_fwd(_rope_bwd_fn, [_view(unheads(dk_h)), _view(rope_c), _view(rope_s1),
                                      _view(rope_s2)], [], [128], nblk=nall, name="rope_k_bwd")
    dva = unheads(dv_h)

    lead = lambda t: jnp.pad(t, ((BLOCK, 0), (0, 0)))
    dproj = jnp.concatenate([d_rw[0], d_rw[1], d_rw[2], dq, lead(dg1), lead(dg2), d_rw[5], dka, dva, d_rw[3],
                             d_rw[4], jnp.zeros((lp, NP - C_DA - 128), F32)], axis=1)
    grads["w_in_p"] = _mm(u, dproj, ta=True, name="dw_in")
    grads["b_in"] = _w_in_unpadded(_colsum(dproj, name="db_in"))
    du = _mm(dproj, w_in_p, tb=True, name="d_u")
    dh, grads["norm_mix_g"] = _rows_bwd(_rms_fn, [_view(hpad)], [row("norm_mix_g")], [_view(du)], nblk=nall,
                                        name="norm_mix_bwd", diff_rows=[0], diff_consts=[0], acc=[_view(lead(dh1))])
    grads["meta_tokens"] = dh[PAD_ROWS:BLOCK]
    return loss, dh[BLOCK:], grads, early_parts


def kernel(x, meta_tokens, norm_mix_g, w_in, b_in, attn_sinks, rwkv_mix, rwkv_w0, rwkv_w2, rwkv_a0, rwkv_a2, rwkv_g2, rwkv_k_k, rwkv_k_a, rwkv_r_k, rwkv_ln_w, rwkv_ln_b, w_br_attn, w_br_rwkv, w_o, norm_ffn_g, w_ffn_gate, w_ffn_up, w_ffn_down, norm_final_g, loss_target, m_meta_tokens, m_norm_mix_g, m_w_in, m_b_in, m_attn_sinks, m_rwkv_mix, m_rwkv_w0, m_rwkv_w2, m_rwkv_a0, m_rwkv_a2, m_rwkv_g2, m_rwkv_k_k, m_rwkv_k_a, m_rwkv_r_k, m_rwkv_ln_w, m_rwkv_ln_b, m_w_br_attn, m_w_br_rwkv, m_w_o, m_norm_ffn_g, m_w_ffn_gate, m_w_ffn_up, m_w_ffn_down, m_norm_final_g, v_meta_tokens, v_norm_mix_g, v_w_in, v_b_in, v_attn_sinks, v_rwkv_mix, v_rwkv_w0, v_rwkv_w2, v_rwkv_a0, v_rwkv_a2, v_rwkv_g2, v_rwkv_k_k, v_rwkv_k_a, v_rwkv_r_k, v_rwkv_ln_w, v_rwkv_ln_b, v_w_br_attn, v_w_br_rwkv, v_w_o, v_norm_ffn_g, v_w_ffn_gate, v_w_ffn_up, v_w_ffn_down, v_norm_final_g):
    given = dict(locals())
    wts = {n: _strip(n, given[n]) for n in WEIGHTS}
    mom = {n: _strip(n, given["m_" + n]) for n in WEIGHTS}
    var = {n: _strip(n, given["v_" + n]) for n in WEIGHTS}
    small_shapes = [wts[n].shape for n in REPLICATED]
    width = wts["w_in"].shape[1]
    wire = lambda table: [wts[n].astype(BF16) for n, _ in table]

    w_in_all, *early_all = _all_gather([wts["w_in"].astype(BF16)] + wire(EARLY), name="gather_weights")
    full = {n: wts[n] for n in REPLICATED}
    full.update({n: _join(g, axis) for (n, axis), g in zip(EARLY, early_all)})
    full["w_in_p"] = _w_in_padded(w_in_all, shard_width=width)
    gather_late = (wire(LATE), lambda got: {n: _join(g, axis) for (n, axis), g in zip(LATE, got)})
    scatter_early = lambda g: [_split(g[n], axis).astype(BF16) for n, axis in LATE]

    loss_part, grad_x, grads, parts_late = _device_step(x[0], loss_target[0], full, gather_late, scatter_early)

    g_w_in = jnp.stack([_w_in_unpadded(grads["w_in_p"], p * width, (p + 1) * width) for p in range(N_DEV)])
    g_early = [_split(grads[n], axis).astype(BF16) for n, axis in EARLY]
    zero = jnp.zeros((1,), F32)
    g_small = _pack([grads[n].reshape(wts[n].shape) for n in REPLICATED] + [loss_part.reshape(1)], 8)
    parts_w_in, *parts_early, parts_small = _exchange([g_w_in.astype(BF16)] + g_early, [g_small],
                                                      name="exchange_grads")
    results = [{}, {}, {}, {}]
    for (n, _), parts in zip([("w_in", 1)] + EARLY + LATE, [parts_w_in] + parts_early + list(parts_late)):
        for kind, a in enumerate(_adamw(parts, wts[n], mom[n], var[n], name="adamw_" + n)):
            results[kind][n] = a
    small = _adamw(parts_small, _pack([wts[n] for n in REPLICATED] + [zero], 8),
                   _pack([mom[n] for n in REPLICATED] + [zero], 8), _pack([var[n] for n in REPLICATED] + [zero], 8),
                   name="adamw_replicated")
    for kind in range(4):
        for n, a in zip(REPLICATED, _unpack(small[kind], small_shapes)):
            results[kind][n] = a
    loss = _unpack(small[0], small_shapes + [(1,)])[-1][0]
    out = [loss, grad_x[None]]
    for kind in range(4):
        out += [results[kind][n].reshape(given[n].shape) for n in WEIGHTS]
    return tuple(out)
```

```python
import functools

import jax
import jax.numpy as jnp
from jax import lax
from jax.experimental import pallas as pl
from jax.experimental.pallas import tpu as pltpu

F32 = jnp.float32
BF16 = jnp.bfloat16

N_DEV = 8
D_MODEL = 1024
N_META = 16
BLOCK = 128
PAD_ROWS = BLOCK - N_META
HEAD_DIM = 64
Q_HEADS = 8
KV_HEADS = 2
GROUP = Q_HEADS // KV_HEADS
ROPE_DIM = HEAD_DIM // 4
ROPE_HALF = ROPE_DIM // 2
ROPE_THETA = 500000.0
RW_HEADS = 8
RW_DIM = 512
DECAY_LORA = 64
AAA_LORA = 64
GATE_LORA = 160
D_FF = 2816
D_IN = 4640
RMS_EPS = 1e-6
RWKV_LN_EPS = 64e-5
NEG_INF = -1e30
SCAN_T = 16
LANES = 128
PACK_ROWS = 256

ADAM_LR = 0.001
ADAM_B1 = 0.9
ADAM_B2 = 0.999
ADAM_EPS = 1e-08
ADAM_WD = 0.01
ADAM_STEP = 10

C_R, C_K, C_V, C_Q = 0, 512, 1024, 1536
C_G1, C_G2 = 2048, 3072
C_DG, C_KA, C_VA, C_DW, C_DA = 4096, 4352, 4480, 4608, 4736
NP = 5120

VMEM_LIMIT = 48 * 1024 * 1024


def _cparams(sem):
    return pltpu.CompilerParams(dimension_semantics=sem, vmem_limit_bytes=VMEM_LIMIT)


def _pick(n, cands):
    for c in cands:
        if n % c == 0:
            return c
    raise ValueError(f"no tile for {n}")


def _mm(a, b, *, ta=False, tb=False, bias=None, residual=None, name):
    m = a.shape[1] if ta else a.shape[0]
    k = a.shape[0] if ta else a.shape[1]
    n = b.shape[0] if tb else b.shape[1]
    assert k == (b.shape[1] if tb else b.shape[0]), (a.shape, b.shape, ta, tb)
    tm = _pick(m, (512, 1408, 256, 128) if ta else (512, 528, 384, 256, 128))
    tn = _pick(n, (512, 1408, 256, 128))
    if k <= 1024:
        tk = k
    else:
        tk = _pick(k, (1024, 1056, 528, 512) if (ta and not tb) else (1024, 1408, 512, 256, 128))
    nk = k // tk
    has_bias = bias is not None
    has_res = residual is not None
    dn = (((0 if ta else 1,), (1 if tb else 0,)), ((), ()))

    def body(*refs):
        a_ref, b_ref = refs[0], refs[1]
        pos = 2
        bias_ref = res_ref = None
        if has_bias:
            bias_ref = refs[pos]
            pos += 1
        if has_res:
            res_ref = refs[pos]
            pos += 1
        o_ref, acc_ref = refs[pos], refs[pos + 1]
        kk = pl.program_id(2)
        part = lax.dot_general(a_ref[...].astype(BF16), b_ref[...].astype(BF16), dn, preferred_element_type=F32)

        def finish(out):
            if has_bias:
                out = out + bias_ref[...]
            if has_res:
                out = out + res_ref[...]
            o_ref[...] = out

        if nk == 1:
            finish(part)
        else:
            @pl.when(kk == 0)
            def _():
                acc_ref[...] = part

            @pl.when((kk > 0) & (kk < nk - 1))
            def _():
                acc_ref[...] += part

            @pl.when(kk == nk - 1)
            def _():
                finish(acc_ref[...] + part)

    in_specs = [
        pl.BlockSpec((tk, tm), lambda i, j, kk: (kk, i)) if ta else pl.BlockSpec((tm, tk), lambda i, j, kk: (i, kk)),
        pl.BlockSpec((tn, tk), lambda i, j, kk: (j, kk)) if tb else pl.BlockSpec((tk, tn), lambda i, j, kk: (kk, j)),
    ]
    args = [a, b]
    if has_bias:
        in_specs.append(pl.BlockSpec((1, tn), lambda i, j, kk: (0, j)))
        args.append(bias)
    if has_res:
        in_specs.append(pl.BlockSpec((tm, tn), lambda i, j, kk: (i, j)))
        args.append(residual)
    return pl.pallas_call(
        body, name=name, grid=(m // tm, n // tn, nk),
        in_specs=in_specs, out_specs=pl.BlockSpec((tm, tn), lambda i, j, kk: (i, j)),
        out_shape=jax.ShapeDtypeStruct((m, n), F32),
        scratch_shapes=[pltpu.VMEM((tm, tn) if nk > 1 else (8, LANES), F32)],
        compiler_params=_cparams(("parallel", "parallel", "arbitrary")),
    )(*args)


def _colsum(x, name):
    m, n = x.shape
    tm = BLOCK

    def body(x_ref, o_ref):
        i = pl.program_id(0)
        s = jnp.sum(x_ref[...].astype(F32), axis=0, keepdims=True)

        @pl.when(i == 0)
        def _():
            o_ref[...] = s

        @pl.when(i > 0)
        def _():
            o_ref[...] += s

    return pl.pallas_call(
        body, name=name, grid=(m // tm,),
        in_specs=[pl.BlockSpec((tm, n), lambda i: (i, 0))],
        out_specs=pl.BlockSpec((1, n), lambda i: (0, 0)),
        out_shape=jax.ShapeDtypeStruct((1, n), F32),
        compiler_params=_cparams(("arbitrary",)),
    )(x)


def _view(arr, width=None, col=0, off=0):
    return (arr, arr.shape[1] if width is None else width, col, off)


def _row_spec(view):
    _, width, col, off = view
    return pl.BlockSpec((BLOCK, width), lambda i, col=col, off=off: (i + off, col))


def _const_spec(arr):
    return pl.BlockSpec(arr.shape, lambda i: (0,) * arr.ndim)


def _rows_fwd(fn, rows, consts, out_widths, *, nblk, name, out_dtype=F32):
    nr, nc = len(rows), len(consts)

    def body(*refs):
        i = pl.program_id(0)
        vals = [r[...] for r in refs[:nr + nc]]
        outs = fn(i, *vals)
        for o_ref, o in zip(refs[nr + nc:], outs):
            o_ref[...] = o.astype(o_ref.dtype)

    return pl.pallas_call(
        body, name=name, grid=(nblk,),
        in_specs=[_row_spec(v) for v in rows] + [_const_spec(c) for c in consts],
        out_specs=[pl.BlockSpec((BLOCK, w), lambda i: (i, 0)) for w in out_widths],
        out_shape=[jax.ShapeDtypeStruct((nblk * BLOCK, w), out_dtype) for w in out_widths],
        compiler_params=_cparams(("parallel",)),
    )(*[v[0] for v in rows], *consts)


def _rows_bwd(fn, rows, consts, cts, *, nblk, name, diff_rows, diff_consts, acc=None, fwd_widths=(), row_dtype=F32):
    nr, nc = len(rows), len(consts)
    ct_views = [c for c in cts if c is not None]
    acc = acc or [None] * len(diff_rows)
    acc_views = [a for a in acc if a is not None]
    n_in = nr + nc + len(ct_views) + len(acc_views)
    n_fwd = len(fwd_widths)

    def body(*refs):
        i = pl.program_id(0)
        row_vals = [r[...] for r in refs[:nr]]
        const_vals = [r[...] for r in refs[nr:nr + nc]]
        ct_vals = [r[...] for r in refs[nr + nc:nr + nc + len(ct_views)]]
        acc_vals = [r[...] for r in refs[nr + nc + len(ct_views):n_in]]
        out_refs = refs[n_in:]

        def f(*dargs):
            rv = list(row_vals)
            cv = list(const_vals)
            for pos, idx in enumerate(diff_rows):
                rv[idx] = dargs[pos]
            for pos, idx in enumerate(diff_consts):
                cv[idx] = dargs[len(diff_rows) + pos]
            return tuple(fn(i, *rv, *cv))

        primals = [row_vals[idx] for idx in diff_rows] + [const_vals[idx] for idx in diff_consts]
        outs, pull = jax.vjp(f, *primals)
        full_ct, ci = [], 0
        for o, c in zip(outs, cts):
            if c is None:
                full_ct.append(jnp.zeros_like(o))
            else:
                full_ct.append(ct_vals[ci])
                ci += 1
        grads = pull(tuple(full_ct))
        for o_ref, o in zip(out_refs[:n_fwd], outs):
            o_ref[...] = o
        ai = 0
        for pos in range(len(diff_rows)):
            g = grads[pos]
            if acc[pos] is not None:
                g = g + acc_vals[ai]
                ai += 1
            out_refs[n_fwd + pos][...] = g.astype(row_dtype)
        for pos in range(len(diff_consts)):
            g = grads[len(diff_rows) + pos]
            o_ref = out_refs[n_fwd + len(diff_rows) + pos]

            @pl.when(i == 0)
            def _(o_ref=o_ref, g=g):
                o_ref[...] = g

            @pl.when(i > 0)
            def _(o_ref=o_ref, g=g):
                o_ref[...] += g

    out_specs = [pl.BlockSpec((BLOCK, w), lambda i: (i, 0)) for w in fwd_widths]
    out_shape = [jax.ShapeDtypeStruct((nblk * BLOCK, w), F32) for w in fwd_widths]
    for idx in diff_rows:
        out_specs.append(pl.BlockSpec((BLOCK, rows[idx][1]), lambda i: (i, 0)))
        out_shape.append(jax.ShapeDtypeStruct((nblk * BLOCK, rows[idx][1]), row_dtype))
    for idx in diff_consts:
        out_specs.append(_const_spec(consts[idx]))
        out_shape.append(jax.ShapeDtypeStruct(consts[idx].shape, F32))
    return pl.pallas_call(
        body, name=name, grid=(nblk,),
        in_specs=([_row_spec(v) for v in rows] + [_const_spec(c) for c in consts]
                  + [_row_spec(v) for v in ct_views] + [_row_spec(v) for v in acc_views]),
        out_specs=out_specs, out_shape=out_shape,
        compiler_params=_cparams(("arbitrary",)),
    )(*[v[0] for v in rows], *consts, *[v[0] for v in ct_views], *[v[0] for v in acc_views])


def _rms_fn(i, x, g):
    return (x * lax.rsqrt(jnp.mean(x * x, axis=-1, keepdims=True) + RMS_EPS) * g,)


def _sigmoid(x):
    return 1.0 / (1.0 + jnp.exp(-x))


def _softplus(x):
    return jnp.maximum(x, 0.0) + jnp.log(1.0 + jnp.exp(-jnp.abs(x)))


def _split2(x):
    hi = x.astype(BF16)
    lo = (x - hi.astype(F32)).astype(BF16)
    return jnp.concatenate([hi, lo], axis=1)


@jax.custom_vjp
def _head_sum(x):
    r = lax.broadcasted_iota(jnp.int32, (2 * RW_DIM, RW_DIM), 0) % RW_DIM // HEAD_DIM
    c = lax.broadcasted_iota(jnp.int32, (2 * RW_DIM, RW_DIM), 1) // HEAD_DIM
    return jnp.dot(_split2(x), (r == c).astype(BF16), preferred_element_type=F32)


_head_sum.defvjp(lambda x: (_head_sum(x), None), lambda _, ct: (_head_sum(ct),))


@jax.custom_vjp
def _dot_bf16(x, w):
    return jnp.dot(x.astype(BF16), w.astype(BF16), preferred_element_type=F32)


def _dot_bf16_bwd(res, ct):
    x, w = res
    ct = ct.astype(BF16)
    dx = lax.dot_general(ct, w.astype(BF16), (((1,), (1,)), ((), ())), preferred_element_type=F32)
    dw = lax.dot_general(x.astype(BF16), ct, (((0,), (0,)), ((), ())), preferred_element_type=F32)
    return dx, dw


_dot_bf16.defvjp(lambda x, w: (_dot_bf16(x, w), (x, w)), _dot_bf16_bwd)


def _rwkv_pre_fn(i, r, k, v, dw, da, dg, r_p, k_p, v_p, dw_p, da_p, dg_p,
                 mix_r, mix_k, mix_v, mix_dw, mix_da, mix_dg, w0, w2, a0, a2, g2, k_k, k_a):
    row = i * BLOCK + lax.broadcasted_iota(jnp.int32, (BLOCK, 1), 0)
    live = row >= PAD_ROWS
    live_prev = row >= PAD_ROWS + 1

    def shift(cur, prev, mix):
        cur = jnp.where(live, cur, 0.0)
        prev = jnp.where(live_prev, prev, 0.0)
        return cur + (prev - cur) * mix

    r = shift(r, r_p, mix_r)
    k = shift(k, k_p, mix_k)
    v = shift(v, v_p, mix_v)
    dw = shift(dw, dw_p, mix_dw)
    da = shift(da, da_p, mix_da)
    dg = shift(dg, dg_p, mix_dg)
    wlog = -_softplus(-(w0 + _dot_bf16(jnp.tanh(dw), w2))) - 0.5
    decay = jnp.exp(-jnp.exp(wlog))
    a = _sigmoid(a0 + _dot_bf16(da, a2))
    g = _dot_bf16(_sigmoid(dg), g2)
    kk = k * k_k
    norm_sq = jnp.where(live, _head_sum(kk * kk), 1.0)
    kk = kk / jnp.maximum(jnp.sqrt(norm_sq), 1e-12)
    k_mod = k * (1.0 + (a - 1.0) * k_a)
    return r, decay, k_mod, v, -kk, kk * a, g


def _rwkv_post_fn(i, ys, r, k_mod, v, g, ln_w, ln_b, r_k):
    mean = _head_sum(ys) * (1.0 / HEAD_DIM)
    d = ys - mean
    var = _head_sum(d * d) * (1.0 / HEAD_DIM)
    yn = d * lax.rsqrt(var + RWKV_LN_EPS) * ln_w + ln_b
    bonus = _head_sum(r * k_mod * r_k) * v
    return ((yn + bonus) * g,)


def _merge_fn(i, ya, yr, g1, g2):
    return (_sigmoid(g1) * ya + _sigmoid(g2) * yr,)


def _swiglu_fn(i, gate, up):
    return (gate * _sigmoid(gate) * up,)


def _loss_fn(i, h, tgt, g):
    y = h * lax.rsqrt(jnp.mean(h * h, axis=-1, keepdims=True) + RMS_EPS) * g
    err = y - tgt
    return (0.5 * jnp.mean(err * err, axis=-1, keepdims=True),)


def _rope_tables(lp):
    pos = (jnp.arange(lp, dtype=jnp.int32) - PAD_ROWS).astype(F32)
    inv_freq = jnp.power(jnp.float32(ROPE_THETA), -jnp.arange(ROPE_HALF, dtype=F32) * (2.0 / ROPE_DIM))
    ang = pos[:, None] * inv_freq[None, :]
    cos, sin = jnp.cos(ang), jnp.sin(ang)
    one = jnp.ones((lp, HEAD_DIM - ROPE_DIM), F32)
    zero_h = jnp.zeros((lp, ROPE_HALF), F32)
    zero_r = jnp.zeros((lp, HEAD_DIM - ROPE_DIM), F32)
    c = jnp.concatenate([cos, cos, one], axis=1)
    s1 = jnp.concatenate([-sin, zero_h, zero_r], axis=1)
    s2 = jnp.concatenate([zero_h, sin, zero_r], axis=1)
    return tuple(jnp.tile(t, (1, LANES // HEAD_DIM)) for t in (c, s1, s2))


def _rope_fwd_fn(i, x, c, s1, s2):
    n = x.shape[1]
    c, s1, s2 = [jnp.tile(t, (1, n // LANES)) for t in (c, s1, s2)]
    return (x * c + pltpu.roll(x, n - ROPE_HALF, 1) * s1 + pltpu.roll(x, ROPE_HALF, 1) * s2,)


def _rope_bwd_fn(i, dy, c, s1, s2):
    n = dy.shape[1]
    c, s1, s2 = [jnp.tile(t, (1, n // LANES)) for t in (c, s1, s2)]
    return (dy * c + pltpu.roll(dy * s1, ROPE_HALF, 1) + pltpu.roll(dy * s2, n - ROPE_HALF, 1),)


def _attn_mask(i):
    r = lax.broadcasted_iota(jnp.int32, (BLOCK, 3 * BLOCK), 0)
    c = lax.broadcasted_iota(jnp.int32, (BLOCK, 3 * BLOCK), 1)
    meta = (c < BLOCK) & (c >= PAD_ROWS)
    prev = (c >= BLOCK) & (c < 2 * BLOCK) & ((c - BLOCK) > r) & (i >= 1)
    cur = (c >= 2 * BLOCK) & ((c - 2 * BLOCK) <= r)
    return meta | prev | cur


def _attn_fwd(q, k, v, sinks, *, nblk, name):
    scale = HEAD_DIM ** -0.5

    def body(q_ref, km_ref, kp_ref, kc_ref, vm_ref, vp_ref, vc_ref, s_ref, o_ref, lse_ref):
        i = pl.program_id(0)
        valid = _attn_mask(i)
        for h in range(Q_HEADS):
            g = h // GROUP
            kcat = jnp.concatenate([km_ref[g], kp_ref[g], kc_ref[g]], axis=0).astype(BF16)
            vcat = jnp.concatenate([vm_ref[g], vp_ref[g], vc_ref[g]], axis=0).astype(BF16)
            s = lax.dot_general(q_ref[h].astype(BF16), kcat, (((1,), (1,)), ((), ())),
                                preferred_element_type=F32) * scale
            s = jnp.where(valid, s, NEG_INF)
            sink = s_ref[0:1, h:h + 1]
            m = jnp.maximum(jnp.max(s, axis=-1, keepdims=True), sink)
            p = jnp.exp(s - m)
            den = jnp.sum(p, axis=-1, keepdims=True) + jnp.exp(sink - m)
            o = jnp.dot(p.astype(BF16), vcat, preferred_element_type=F32)
            o_ref[h] = o / den
            lse_ref[h] = m + jnp.log(den)

    kv = lambda f: pl.BlockSpec((KV_HEADS, BLOCK, HEAD_DIM), f)
    return pl.pallas_call(
        body, name=name, grid=(nblk,),
        in_specs=[pl.BlockSpec((Q_HEADS, BLOCK, HEAD_DIM), lambda i: (0, i + 1, 0)),
                  kv(lambda i: (0, 0, 0)), kv(lambda i: (0, i, 0)), kv(lambda i: (0, i + 1, 0)),
                  kv(lambda i: (0, 0, 0)), kv(lambda i: (0, i, 0)), kv(lambda i: (0, i + 1, 0)),
                  pl.BlockSpec((1, Q_HEADS), lambda i: (0, 0))],
        out_specs=[pl.BlockSpec((Q_HEADS, BLOCK, HEAD_DIM), lambda i: (0, i, 0)),
                   pl.BlockSpec((Q_HEADS, BLOCK, 1), lambda i: (0, i, 0))],
        out_shape=[jax.ShapeDtypeStruct((Q_HEADS, nblk * BLOCK, HEAD_DIM), F32),
                   jax.ShapeDtypeStruct((Q_HEADS, nblk * BLOCK, 1), F32)],
        compiler_params=_cparams(("parallel",)),
    )(q, k, k, k, v, v, v, sinks)


def _attn_bwd(q, k, v, sinks, o, lse, do, *, nblk, name):
    scale = HEAD_DIM ** -0.5
    lp = k.shape[1]

    def body(q_ref, km_ref, kp_ref, kc_ref, vm_ref, vp_ref, vc_ref, s_ref, o_ref, lse_ref, do_ref,
             dq_ref, dk_ref, dv_ref, ds_ref):
        i = pl.program_id(0)

        @pl.when(i == 0)
        def _():
            dk_ref[...] = jnp.zeros_like(dk_ref)
            dv_ref[...] = jnp.zeros_like(dv_ref)
            ds_ref[...] = jnp.zeros_like(ds_ref)

        valid = _attn_mask(i)
        lane = lax.broadcasted_iota(jnp.int32, (1, Q_HEADS), 1)
        prev_rows = pl.ds(pl.multiple_of(i * BLOCK, BLOCK), BLOCK)
        cur_rows = pl.ds(pl.multiple_of((i + 1) * BLOCK, BLOCK), BLOCK)
        for h in range(Q_HEADS):
            g = h // GROUP
            qh = q_ref[h].astype(BF16)
            doh = do_ref[h]
            kcat = jnp.concatenate([km_ref[g], kp_ref[g], kc_ref[g]], axis=0).astype(BF16)
            vcat = jnp.concatenate([vm_ref[g], vp_ref[g], vc_ref[g]], axis=0).astype(BF16)
            s = lax.dot_general(qh, kcat, (((1,), (1,)), ((), ())), preferred_element_type=F32) * scale
            s = jnp.where(valid, s, NEG_INF)
            lse_h = lse_ref[h]
            p = jnp.exp(s - lse_h)
            delta = jnp.sum(doh * o_ref[h], axis=-1, keepdims=True)
            dp = lax.dot_general(doh.astype(BF16), vcat, (((1,), (1,)), ((), ())), preferred_element_type=F32)
            dsc = (p * (dp - delta) * scale).astype(BF16)
            dq_ref[h] = jnp.dot(dsc, kcat, preferred_element_type=F32)
            dk_all = lax.dot_general(dsc, qh, (((0,), (0,)), ((), ())), preferred_element_type=F32)
            dv_all = lax.dot_general(p.astype(BF16), doh.astype(BF16), (((0,), (0,)), ((), ())),
                                     preferred_element_type=F32)
            dk_ref[g, 0:BLOCK, :] += dk_all[0:BLOCK]
            dk_ref[g, prev_rows, :] += dk_all[BLOCK:2 * BLOCK]
            dk_ref[g, cur_rows, :] += dk_all[2 * BLOCK:]
            dv_ref[g, 0:BLOCK, :] += dv_all[0:BLOCK]
            dv_ref[g, prev_rows, :] += dv_all[BLOCK:2 * BLOCK]
            dv_ref[g, cur_rows, :] += dv_all[2 * BLOCK:]
            p_sink = jnp.exp(s_ref[0:1, h:h + 1] - lse_h)
            dsink = -jnp.sum(p_sink * delta, axis=0, keepdims=True)
            ds_ref[...] += jnp.where(lane == h, dsink, 0.0)

    kv = lambda f: pl.BlockSpec((KV_HEADS, BLOCK, HEAD_DIM), f)
    qspec = pl.BlockSpec((Q_HEADS, BLOCK, HEAD_DIM), lambda i: (0, i, 0))
    whole = pl.BlockSpec((KV_HEADS, lp, HEAD_DIM), lambda i: (0, 0, 0))
    return pl.pallas_call(
        body, name=name, grid=(nblk,),
        in_specs=[pl.BlockSpec((Q_HEADS, BLOCK, HEAD_DIM), lambda i: (0, i + 1, 0)),
                  kv(lambda i: (0, 0, 0)), kv(lambda i: (0, i, 0)), kv(lambda i: (0, i + 1, 0)),
                  kv(lambda i: (0, 0, 0)), kv(lambda i: (0, i, 0)), kv(lambda i: (0, i + 1, 0)),
                  pl.BlockSpec((1, Q_HEADS), lambda i: (0, 0)),
                  qspec, pl.BlockSpec((Q_HEADS, BLOCK, 1), lambda i: (0, i, 0)), qspec],
        out_specs=[qspec, whole, whole, pl.BlockSpec((1, Q_HEADS), lambda i: (0, 0))],
        out_shape=[jax.ShapeDtypeStruct((Q_HEADS, nblk * BLOCK, HEAD_DIM), F32),
                   jax.ShapeDtypeStruct((KV_HEADS, lp, HEAD_DIM), F32),
                   jax.ShapeDtypeStruct((KV_HEADS, lp, HEAD_DIM), F32),
                   jax.ShapeDtypeStruct((1, Q_HEADS), F32)],
        compiler_params=_cparams(("arbitrary",)),
    )(q, k, k, k, v, v, v, sinks, o, lse, do)


N_VEC = 5
VEC_ROWS = N_VEC * HEAD_DIM


def _selectors():
    t = SCAN_T
    shape = (t, 2 * LANES, RW_DIM)
    step, src, dst = [lax.broadcasted_iota(jnp.int32, shape, d) for d in range(3)]
    src = src % LANES
    spread = ((src // t == dst // HEAD_DIM) & (src % t == step)).astype(BF16)
    shape = (t, RW_DIM, LANES)
    step, src, dst = [lax.broadcasted_iota(jnp.int32, shape, d) for d in range(3)]
    collect = ((src // HEAD_DIM == dst // t) & (dst % t == step)).astype(BF16)
    return spread, collect


def _rows_to_xt(x):
    low = lax.broadcasted_iota(jnp.int32, (SCAN_T, LANES), 1) < HEAD_DIM
    pieces = []
    for m in range(RW_HEADS // 2):
        pair = x[:, m * LANES:(m + 1) * LANES]
        pieces += [jnp.where(low, pair, 0.0), jnp.where(low, pltpu.roll(pair, HEAD_DIM, 1), 0.0)]
    return jnp.concatenate(pieces, axis=0).T[:HEAD_DIM]


def _xt_to_rows(a):
    t = SCAN_T
    a_t = jnp.concatenate([a, jnp.zeros_like(a)], axis=0).T
    pairs = [a_t[2 * m * t:(2 * m + 1) * t] + pltpu.roll(a_t[(2 * m + 1) * t:(2 * m + 2) * t], HEAD_DIM, 1)
             for m in range(RW_HEADS // 2)]
    return jnp.concatenate(pairs, axis=1)


def _with_exchange(compute, n_in, n_out, n_scratch, scattered, shared, last_step):
    n_sc = len(scattered)
    n_x = n_sc + len(shared)
    if n_x == 0:
        return compute

    def body(*refs):
        ins, x_in = refs[:n_in], refs[n_in:n_in + n_x]
        outs, x_out = refs[n_in + n_x:n_in + n_x + n_out], refs[n_in + n_x + n_out:n_in + 2 * n_x + n_out]
        scratch = refs[n_in + 2 * n_x + n_out:n_in + 2 * n_x + n_out + n_scratch]
        sems = refs[n_in + 2 * n_x + n_out + n_scratch:]

        @pl.when(pl.program_id(0) == 0)
        def _():
            for cp in _exchange_copies(x_in, x_out, n_sc, *sems):
                cp.start()

        compute(*ins, *outs, *scratch)

        @pl.when(pl.program_id(0) == last_step)
        def _():
            for cp in _exchange_copies(x_in, x_out, n_sc, *sems):
                cp.wait()

    return body


def _wkv_fwd(vecs, v, spread, name, shared=()):
    t_steps = SCAN_T
    nch = v.shape[0] // t_steps
    n_x = len(shared)

    def compute(*refs):
        vec_refs, (v_ref, sel_ref, y_ref, hist_ref, st_ref) = refs[:N_VEC], refs[N_VEC:]

        @pl.when(pl.program_id(0) == 0)
        def _():
            st_ref[...] = jnp.zeros_like(st_ref)

        x2 = _split2(jnp.concatenate([_rows_to_xt(ref[...]) for ref in vec_refs], axis=0))
        st = st_ref[...]
        for j in range(t_steps):
            cols = jnp.dot(x2, sel_ref[j], preferred_element_type=F32)
            a_c, w_c, b_c, k_c, r_c = [cols[n * HEAD_DIM:(n + 1) * HEAD_DIM] for n in range(N_VEC)]
            hist_ref[j] = st
            sa = jnp.sum(st * a_c, axis=0, keepdims=True)
            st = st * w_c + b_c * sa + k_c * v_ref[j:j + 1, :]
            y_ref[j:j + 1, :] = jnp.sum(st * r_c, axis=0, keepdims=True)
        st_ref[...] = st

    return pl.pallas_call(
        _with_exchange(compute, N_VEC + 2, 2, 1, (), shared, nch - 1), name=name, grid=(nch,),
        in_specs=[pl.BlockSpec((t_steps, RW_DIM), lambda c: (c, 0))] * (N_VEC + 1)
                 + [pl.BlockSpec(spread.shape, lambda c: (0, 0, 0))] + [ANY] * n_x,
        out_specs=[pl.BlockSpec((t_steps, RW_DIM), lambda c: (c, 0)),
                   pl.BlockSpec((t_steps, HEAD_DIM, RW_DIM), lambda c: (c, 0, 0))] + [ANY] * n_x,
        out_shape=[jax.ShapeDtypeStruct((nch * t_steps, RW_DIM), F32),
                   jax.ShapeDtypeStruct((nch * t_steps, HEAD_DIM, RW_DIM), F32)] + _exchange_shapes((), shared),
        scratch_shapes=[pltpu.VMEM((HEAD_DIM, RW_DIM), F32)] + (_exchange_sems(n_x) if n_x else []),
        compiler_params=_cparams(("arbitrary",)),
    )(*vecs, v, spread, *shared)


def _wkv_bwd(vecs, v, hist, dy, spread, collect, name, scattered=()):
    t_steps = SCAN_T
    nch = v.shape[0] // t_steps
    n_x = len(scattered)

    def compute(*refs):
        vec_refs, (v_ref, hist_ref, dy_ref, sel_ref, col_ref) = refs[:N_VEC], refs[N_VEC:N_VEC + 5]
        d_refs, dv_ref, g_ref = refs[N_VEC + 5:2 * N_VEC + 5], refs[2 * N_VEC + 5], refs[2 * N_VEC + 6]
        @pl.when(pl.program_id(0) == 0)
        def _():
            g_ref[...] = jnp.zeros_like(g_ref)

        x2 = _split2(jnp.concatenate([_rows_to_xt(ref[...]) for ref in vec_refs], axis=0))
        gst = g_ref[...]
        acc = jnp.zeros((VEC_ROWS, LANES), F32)
        nxt = None
        for j in reversed(range(t_steps)):
            cols = jnp.dot(x2, sel_ref[j], preferred_element_type=F32)
            a_c, w_c, b_c, k_c, r_c = [cols[n * HEAD_DIM:(n + 1) * HEAD_DIM] for n in range(N_VEC)]
            prev = hist_ref[j]
            v_row = v_ref[j:j + 1, :]
            dy_row = dy_ref[j:j + 1, :]
            sa = jnp.sum(prev * a_c, axis=0, keepdims=True)
            if nxt is None:
                nxt = prev * w_c + b_c * sa + k_c * v_row
            gst = gst + r_c * dy_row
            dv_ref[j:j + 1, :] = jnp.sum(gst * k_c, axis=0, keepdims=True)
            dsa = jnp.sum(gst * b_c, axis=0, keepdims=True)
            prods = jnp.concatenate([p.astype(BF16) for p in
                                     (prev * dsa, gst * prev, gst * sa, gst * v_row, nxt * dy_row)], axis=0)
            acc = acc + jnp.dot(prods, col_ref[j], preferred_element_type=F32)
            gst = gst * w_c + a_c * dsa
            nxt = prev
        for n in range(N_VEC):
            d_refs[n][...] = _xt_to_rows(acc[n * HEAD_DIM:(n + 1) * HEAD_DIM])
        g_ref[...] = gst

    rev3 = lambda c: (nch - 1 - c, 0, 0)
    rev2 = lambda c: (nch - 1 - c, 0)
    rowspec = pl.BlockSpec((t_steps, RW_DIM), rev2)
    return pl.pallas_call(
        _with_exchange(compute, N_VEC + 5, N_VEC + 1, 1, scattered, (), nch - 1), name=name, grid=(nch,),
        in_specs=[rowspec] * (N_VEC + 1) + [
                  pl.BlockSpec((t_steps, HEAD_DIM, RW_DIM), rev3), rowspec,
                  pl.BlockSpec(spread.shape, lambda c: (0, 0, 0)),
                  pl.BlockSpec(collect.shape, lambda c: (0, 0, 0))] + [ANY] * n_x,
        out_specs=[rowspec] * (N_VEC + 1) + [ANY] * n_x,
        out_shape=([jax.ShapeDtypeStruct((nch * t_steps, RW_DIM), F32)] * (N_VEC + 1)
                   + _exchange_shapes(scattered, ())),
        scratch_shapes=[pltpu.VMEM((HEAD_DIM, RW_DIM), F32)] + (_exchange_sems(n_x) if n_x else []),
        compiler_params=_cparams(("arbitrary",)),
    )(*vecs, v, hist, dy, spread, collect, *scattered)


MESH = pl.DeviceIdType.MESH
ANY = pl.BlockSpec(memory_space=pl.ANY)


def _all_gather(arrays, name):
    n_arr = len(arrays)
    per = N_DEV - 1

    def body(*refs):
        x_refs, out_refs = refs[:n_arr], refs[n_arr:2 * n_arr]
        send_sems, recv_sems, local_sems = refs[2 * n_arr:]
        xi, yi, ci = lax.axis_index("x"), lax.axis_index("y"), lax.axis_index("c")
        me, sibling = (xi, yi, ci), (xi, yi, 1 - ci)
        chips = [(1 - xi, yi), (xi, 1 - yi), (1 - xi, 1 - yi)]

        def slot(a, px, py, pc):
            return out_refs[a].at[4 * px + 2 * py + pc]

        def copy(a, sem, block, to, src=None):
            return pltpu.make_async_remote_copy(
                src_ref=slot(a, *block) if src is None else src, dst_ref=slot(a, *block),
                send_sem=send_sems.at[per * a + sem], recv_sem=recv_sems.at[per * a + sem],
                device_id=to, device_id_type=MESH)

        mine = [pltpu.make_async_copy(x_refs[a], slot(a, *me), local_sems.at[a]) for a in range(n_arr)]
        for cp in mine:
            cp.start()
        sent = []
        for a in range(n_arr):
            sent.append(copy(a, 0, me, sibling, src=x_refs[a]))
            sent += [copy(a, 1 + j, me, (*chip, ci), src=x_refs[a]) for j, chip in enumerate(chips)]
        for cp in sent:
            cp.start()
        for j, chip in enumerate(chips):
            for a in range(n_arr):
                copy(a, 1 + j, (*chip, ci), me).wait_recv()
                onward = copy(a, 4 + j, (*chip, ci), sibling)
                onward.start()
                sent.append(onward)
        for a in range(n_arr):
            copy(a, 0, sibling, me).wait_recv()
        for j, chip in enumerate(chips):
            for a in range(n_arr):
                copy(a, 4 + j, (*chip, 1 - ci), me).wait_recv()
        for cp in sent:
            cp.wait_send()
        for cp in mine:
            cp.wait()

    sems = pltpu.SemaphoreType.DMA((per * n_arr,))
    return pl.pallas_call(
        body, name=name, out_shape=[jax.ShapeDtypeStruct((N_DEV,) + a.shape, a.dtype) for a in arrays],
        in_specs=[ANY] * n_arr, out_specs=[ANY] * n_arr,
        scratch_shapes=[sems, sems, pltpu.SemaphoreType.DMA((n_arr,))],
    )(*arrays)


def _exchange_copies(in_refs, out_refs, n_scattered, send_sems, recv_sems, local_sems):
    n_arr = len(in_refs)
    per = N_DEV - 1
    xi, yi, ci = lax.axis_index("x"), lax.axis_index("y"), lax.axis_index("c")
    me = 4 * xi + 2 * yi + ci
    src_of = lambda a, peer: in_refs[a].at[peer] if a < n_scattered else in_refs[a]
    copies = []
    for d in range(1, N_DEV):
        px = 1 - xi if d & 4 else xi
        py = 1 - yi if d & 2 else yi
        pc = 1 - ci if d & 1 else ci
        for a in range(n_arr):
            copies.append(pltpu.make_async_remote_copy(
                src_ref=src_of(a, 4 * px + 2 * py + pc), dst_ref=out_refs[a].at[me],
                send_sem=send_sems.at[per * a + d - 1], recv_sem=recv_sems.at[per * a + d - 1],
                device_id=(px, py, pc), device_id_type=MESH))
    own = [pltpu.make_async_copy(src_of(a, me), out_refs[a].at[me], local_sems.at[a]) for a in range(n_arr)]
    return copies + own


def _exchange_shapes(scattered, shared):
    return ([jax.ShapeDtypeStruct(a.shape, a.dtype) for a in scattered]
            + [jax.ShapeDtypeStruct((N_DEV,) + a.shape, a.dtype) for a in shared])


def _exchange_sems(n_arr):
    sems = pltpu.SemaphoreType.DMA(((N_DEV - 1) * n_arr,))
    return [sems, sems, pltpu.SemaphoreType.DMA((n_arr,))]


def _exchange(scattered, shared, name):
    n_sc = len(scattered)
    n_arr = n_sc + len(shared)

    def body(*refs):
        copies = _exchange_copies(refs[:n_arr], refs[n_arr:2 * n_arr], n_sc, *refs[2 * n_arr:])
        for cp in copies:
            cp.start()
        for cp in copies:
            cp.wait()

    return pl.pallas_call(
        body, name=name, out_shape=_exchange_shapes(scattered, shared),
        in_specs=[ANY] * n_arr, out_specs=[ANY] * n_arr, scratch_shapes=_exchange_sems(n_arr),
    )(*scattered, *shared)


def _adamw(parts, w, m, v, name):
    rows, cols = w.shape
    tile = PACK_ROWS if rows % PACK_ROWS == 0 else rows

    def body(p_ref, w_ref, m_ref, v_ref, g_out, d_out, m_out, v_out):
        g = p_ref[0].astype(F32)
        for s in range(1, N_DEV):
            g = g + p_ref[s].astype(F32)
        m_new = ADAM_B1 * m_ref[...] + (1.0 - ADAM_B1) * g
        v_new = ADAM_B2 * v_ref[...] + (1.0 - ADAM_B2) * (g * g)
        m_hat = m_new / (1.0 - ADAM_B1 ** ADAM_STEP)
        v_hat = v_new / (1.0 - ADAM_B2 ** ADAM_STEP)
        g_out[...] = g
        d_out[...] = -ADAM_LR * (m_hat / (jnp.sqrt(v_hat) + ADAM_EPS) + ADAM_WD * w_ref[...])
        m_out[...] = m_new
        v_out[...] = v_new

    spec = pl.BlockSpec((tile, cols), lambda i: (i, 0))
    return pl.pallas_call(
        body, name=name, grid=(rows // tile,),
        in_specs=[pl.BlockSpec((N_DEV, tile, cols), lambda i: (0, i, 0)), spec, spec, spec],
        out_specs=[spec] * 4, out_shape=[jax.ShapeDtypeStruct((rows, cols), F32)] * 4,
        compiler_params=_cparams(("parallel",)),
    )(parts, w, m, v)


EARLY = [("meta_tokens", 1), ("rwkv_w2", 1), ("rwkv_a2", 1), ("rwkv_g2", 1)]
LATE = [("w_br_attn", 1), ("w_br_rwkv", 1), ("w_o", 0), ("w_ffn_gate", 1), ("w_ffn_up", 1), ("w_ffn_down", 0)]
REPLICATED = ["norm_mix_g", "b_in", "attn_sinks", "rwkv_mix", "rwkv_w0", "rwkv_a0", "rwkv_k_k", "rwkv_k_a",
              "rwkv_r_k", "rwkv_ln_w", "rwkv_ln_b", "norm_ffn_g", "norm_final_g"]
WEIGHTS = ["meta_tokens", "norm_mix_g", "w_in", "b_in", "attn_sinks", "rwkv_mix", "rwkv_w0", "rwkv_w2", "rwkv_a0",
           "rwkv_a2", "rwkv_g2", "rwkv_k_k", "rwkv_k_a", "rwkv_r_k", "rwkv_ln_w", "rwkv_ln_b", "w_br_attn",
           "w_br_rwkv", "w_o", "norm_ffn_g", "w_ffn_gate", "w_ffn_up", "w_ffn_down", "norm_final_g"]


def _pack(arrays, row_multiple):
    flat = jnp.concatenate([a.reshape(-1) for a in arrays])
    per = row_multiple * LANES
    total = -(-flat.shape[0] // per) * per
    return jnp.pad(flat, (0, total - flat.shape[0])).reshape(-1, LANES)


def _unpack(buf, shapes):
    flat = buf.reshape(-1)
    out, pos = [], 0
    for s in shapes:
        n = 1
        for d in s:
            n *= d
        out.append(flat[pos:pos + n].reshape(s))
        pos += n
    return out


def _strip(name, a):
    return a if name in ("meta_tokens", "norm_final_g") else a[0]


def _join(gathered, axis):
    if axis == 0:
        return gathered.reshape(-1, gathered.shape[2])
    return gathered.transpose(1, 0, 2).reshape(gathered.shape[1], -1)


def _split(g, axis):
    if axis == 0:
        return g.reshape(N_DEV, -1, g.shape[1])
    return g.reshape(g.shape[0], N_DEV, -1).transpose(1, 0, 2)


W_IN_LAYOUT = [(768, 2304), (0, 512), (2592, 4640), (2432, 2592), 256 - GATE_LORA, (512, 768), (2304, 2368),
               128 - DECAY_LORA, (2368, 2432), 128 - AAA_LORA, NP - C_DA - 128]


def _w_in_padded(w, shard_width=None):
    rows = w.shape[-2]
    width = D_IN if shard_width is None else shard_width
    parts = []
    for seg in W_IN_LAYOUT:
        if isinstance(seg, int):
            parts.append(jnp.zeros((rows, seg), w.dtype))
            continue
        lo, stop = seg
        while lo < stop:
            p = lo // width
            hi = min(stop, (p + 1) * width)
            src = w if shard_width is None else w[p]
            parts.append(src[:, lo - p * width:hi - p * width])
            lo = hi
    return jnp.concatenate(parts, axis=1)


def _w_in_unpadded(wp, lo=0, stop=D_IN):
    spans, pos = [], 0
    for seg in W_IN_LAYOUT:
        if isinstance(seg, int):
            pos += seg
        else:
            spans.append((seg[0], seg[1], pos))
            pos += seg[1] - seg[0]
    parts = []
    for a, b, at in sorted(spans):
        c, d = max(a, lo), min(b, stop)
        if c < d:
            parts.append(wp[:, at + c - a:at + d - a])
    return jnp.concatenate(parts, axis=1)


def _pad_rows(a, n):
    return jnp.pad(a, ((0, n - a.shape[0]), (0, 0)))


def _device_step(x, tgt, full, gather_late=None, scatter_early=None):
    seq = x.shape[0]
    nblk = seq // BLOCK
    lp = seq + BLOCK
    nall = nblk + 1

    w_in_p = full["w_in_p"]
    b_in_p = _w_in_padded(full["b_in"][None])
    mix = full["rwkv_mix"][None]
    mix_r, mix_k, mix_v = mix[:, 0:512], mix[:, 512:1024], mix[:, 1024:1536]
    mix_dw = jnp.pad(mix[:, 1536:1600], ((0, 0), (0, 64)))
    mix_da = jnp.pad(mix[:, 1600:1664], ((0, 0), (0, 64)))
    mix_dg = jnp.pad(mix[:, 1664:1824], ((0, 0), (0, 96)))
    w2_p = _pad_rows(full["rwkv_w2"].astype(F32), 128)
    a2_p = _pad_rows(full["rwkv_a2"].astype(F32), 128)
    g2_p = _pad_rows(full["rwkv_g2"].astype(F32), 256)
    row = lambda name: full[name].reshape(1, -1)
    sinks = row("attn_sinks")
    rope_c, rope_s1, rope_s2 = _rope_tables(lp)

    hpad = jnp.concatenate([jnp.zeros((PAD_ROWS, D_MODEL), F32), full["meta_tokens"].astype(F32), x], axis=0)
    (u,) = _rows_fwd(_rms_fn, [_view(hpad)], [row("norm_mix_g")], [D_MODEL], nblk=nall, name="norm_mix",
                     out_dtype=BF16)
    proj = _mm(u, w_in_p, bias=b_in_p, name="in_proj")
    (q_r,) = _rows_fwd(_rope_fwd_fn, [_view(proj, 512, C_Q // 512), _view(rope_c), _view(rope_s1), _view(rope_s2)],
                       [], [512], nblk=nall, name="rope_q")
    (k_r,) = _rows_fwd(_rope_fwd_fn, [_view(proj, 128, C_KA // 128), _view(rope_c), _view(rope_s1),
                                      _view(rope_s2)], [], [128], nblk=nall, name="rope_k")
    heads = lambda a, nh: a.reshape(a.shape[0], nh, HEAD_DIM).transpose(1, 0, 2)
    unheads = lambda a: a.transpose(1, 0, 2).reshape(a.shape[1], -1)
    q_h, k_h = heads(q_r, Q_HEADS), heads(k_r, KV_HEADS)
    v_h = heads(proj[:, C_VA:C_VA + 128], KV_HEADS)
    o_h, lse = _attn_fwd(q_h, k_h, v_h, sinks, nblk=nblk, name="attn_fwd")
    y_attn = unheads(o_h).astype(BF16)

    rw_cols = jnp.concatenate([proj[:, C_R:C_R + 1536], proj[:, C_DG:C_DG + 256], proj[:, C_DW:C_DW + 256]], axis=1)
    rw_prev = jnp.pad(rw_cols[:-1], ((1, 0), (0, 0)))
    pre_rows = [_view(proj, 512, 0), _view(proj, 512, 1), _view(proj, 512, 2), _view(proj, 128, C_DW // 128),
                _view(proj, 128, C_DA // 128), _view(proj, 256, C_DG // 256),
                _view(rw_prev, 512, 0), _view(rw_prev, 512, 1), _view(rw_prev, 512, 2), _view(rw_prev, 128, 14),
                _view(rw_prev, 128, 15), _view(rw_prev, 256, 6)]
    pre_consts = [mix_r, mix_k, mix_v, mix_dw, mix_da, mix_dg, row("rwkv_w0"), w2_p, row("rwkv_a0"), a2_p, g2_p,
                  row("rwkv_k_k"), row("rwkv_k_a")]
    r_t, decay, k_mod, v_t, a_neg, b_t, gate = _rows_fwd(_rwkv_pre_fn, pre_rows, pre_consts, [RW_DIM] * 7,
                                                         nblk=nall, name="rwkv_pre")
    spread, collect = _selectors()
    vecs = (a_neg, decay, b_t, k_mod, r_t)
    y_scan, hist, *late = _wkv_fwd(vecs, v_t, spread, name="wkv_fwd", shared=gather_late[0] if gather_late else ())
    if gather_late:
        full = {**full, **gather_late[1](late)}
    post_rows = [_view(y_scan, off=1), _view(r_t, off=1), _view(k_mod, off=1), _view(v_t, off=1), _view(gate, off=1)]
    post_consts = [row("rwkv_ln_w"), row("rwkv_ln_b"), row("rwkv_r_k")]
    (y_rwkv,) = _rows_fwd(_rwkv_post_fn, post_rows, post_consts, [RW_DIM], nblk=nblk, name="rwkv_post",
                          out_dtype=BF16)

    ya = _mm(y_attn, full["w_br_attn"], name="br_attn")
    yr = _mm(y_rwkv, full["w_br_rwkv"], name="br_rwkv")
    merge_rows = [_view(ya), _view(yr), _view(proj, 1024, C_G1 // 1024, 1), _view(proj, 1024, C_G2 // 1024, 1)]
    (merged,) = _rows_fwd(_merge_fn, merge_rows, [], [D_MODEL], nblk=nblk, name="merge", out_dtype=BF16)
    h1 = _mm(merged, full["w_o"], residual=x, name="out_proj")
    (f,) = _rows_fwd(_rms_fn, [_view(h1)], [row("norm_ffn_g")], [D_MODEL], nblk=nblk, name="norm_ffn",
                     out_dtype=BF16)
    ff_gate = _mm(f, full["w_ffn_gate"], name="ffn_gate")
    ff_up = _mm(f, full["w_ffn_up"], name="ffn_up")
    (act,) = _rows_fwd(_swiglu_fn, [_view(ff_gate), _view(ff_up)], [], [D_FF], nblk=nblk, name="swiglu",
                       out_dtype=BF16)
    h2 = _mm(act, full["w_ffn_down"], residual=h1, name="ffn_down")

    grads = {}
    ones_col = jnp.ones((seq, 1), F32)
    loss_rows, dh2, grads["norm_final_g"] = _rows_bwd(
        _loss_fn, [_view(h2), _view(tgt)], [row("norm_final_g")], [_view(ones_col)], nblk=nblk, name="loss",
        diff_rows=[0], diff_consts=[0], fwd_widths=[1])
    loss = jnp.sum(loss_rows)

    dact = _mm(dh2, full["w_ffn_down"], tb=True, name="d_act")
    grads["w_ffn_down"] = _mm(act, dh2, ta=True, name="dw_ffn_down")
    dgate, dup = _rows_bwd(_swiglu_fn, [_view(ff_gate), _view(ff_up)], [], [_view(dact)], nblk=nblk,
                           name="swiglu_bwd", diff_rows=[0, 1], diff_consts=[], row_dtype=BF16)
    grads["w_ffn_gate"] = _mm(f, dgate, ta=True, name="dw_ffn_gate")
    grads["w_ffn_up"] = _mm(f, dup, ta=True, name="dw_ffn_up")
    df = _mm(dgate, full["w_ffn_gate"], tb=True, name="df_gate")
    df = _mm(dup, full["w_ffn_up"], tb=True, residual=df, name="df_up")
    dh1, grads["norm_ffn_g"] = _rows_bwd(_rms_fn, [_view(h1)], [row("norm_ffn_g")], [_view(df)], nblk=nblk,
                                         name="norm_ffn_bwd", diff_rows=[0], diff_consts=[0], acc=[_view(dh2)])
    dmerged = _mm(dh1, full["w_o"], tb=True, name="d_merged")
    grads["w_o"] = _mm(merged, dh1, ta=True, name="dw_o")
    dya, dyr, dg1, dg2 = _rows_bwd(_merge_fn, merge_rows, [], [_view(dmerged)], nblk=nblk, name="merge_bwd",
                                   diff_rows=[0, 1, 2, 3], diff_consts=[], row_dtype=BF16)
    grads["w_br_attn"] = _mm(y_attn, dya, ta=True, name="dw_br_attn")
    grads["w_br_rwkv"] = _mm(y_rwkv, dyr, ta=True, name="dw_br_rwkv")
    dy_attn = _mm(dya, full["w_br_attn"], tb=True, name="d_y_attn")
    dy_rwkv = _mm(dyr, full["w_br_rwkv"], tb=True, name="d_y_rwkv")

    post = _rows_bwd(_rwkv_post_fn, post_rows, post_consts, [_view(dy_rwkv)], nblk=nblk, name="rwkv_post_bwd",
                     diff_rows=[0, 1, 2, 3, 4], diff_consts=[0, 1, 2])
    dys, dr_post, dk_post, dv_post, dgate_post = [jnp.pad(t, ((BLOCK, 0), (0, 0))) for t in post[:5]]
    grads["rwkv_ln_w"], grads["rwkv_ln_b"], grads["rwkv_r_k"] = post[5:]
    da_s, dw_s, db_s, dk_s, dr_s, dv_s, *early_parts = _wkv_bwd(
        vecs, v_t, hist, dys, spread, collect, name="wkv_bwd", scattered=scatter_early(grads) if scatter_early else ())
    pre_cts = [_view(dr_s + dr_post), _view(dw_s), _view(dk_s + dk_post), _view(dv_s + dv_post), _view(da_s),
               _view(db_s), _view(dgate_post)]
    pre = _rows_bwd(_rwkv_pre_fn, pre_rows, pre_consts, pre_cts, nblk=nall, name="rwkv_pre_bwd",
                    diff_rows=list(range(12)), diff_consts=list(range(13)))
    d_cur, d_prev, d_par = pre[0:6], pre[6:12], pre[12:]
    up = lambda t: jnp.pad(t[1:], ((0, 1), (0, 0)))
    d_rw = [c + up(p) for c, p in zip(d_cur, d_prev)]
    grads["rwkv_mix"] = jnp.concatenate([d_par[0], d_par[1], d_par[2], d_par[3][:, :DECAY_LORA],
                                         d_par[4][:, :AAA_LORA], d_par[5][:, :GATE_LORA]], axis=1)
    grads["rwkv_w0"], grads["rwkv_w2"] = d_par[6], d_par[7][:DECAY_LORA]
    grads["rwkv_a0"], grads["rwkv_a2"] = d_par[8], d_par[9][:AAA_LORA]
    grads["rwkv_g2"] = d_par[10][:GATE_LORA]
    grads["rwkv_k_k"], grads["rwkv_k_a"] = d_par[11], d_par[12]

    do_h = heads(dy_attn, Q_HEADS)
    dq_h, dk_h, dv_h, grads["attn_sinks"] = _attn_bwd(q_h, k_h, v_h, sinks, o_h, lse, do_h, nblk=nblk,
                                                      name="attn_bwd")
    dq_r = jnp.pad(unheads(dq_h), ((BLOCK, 0), (0, 0)))
    (dq,) = _rows_fwd(_rope_bwd_fn, [_view(dq_r), _view(rope_c), _view(rope_s1), _view(rope_s2)], [], [512],
                      nblk=nall, name="rope_q_bwd", out_dtype=BF16)
    (dka,) = _rows_fwd(_rope_bwd_fn, [_view(unheads(dk_h)), _view(rope_c), _view(rope_s1),
                                      _view(rope_s2)], [], [128], nblk=nall, name="rope_k_bwd", out_dtype=BF16)
    dva = unheads(dv_h)

    lead = lambda t: jnp.pad(t, ((BLOCK, 0), (0, 0)))
    pieces = [d_rw[0], d_rw[1], d_rw[2], dq, lead(dg1), lead(dg2), d_rw[5], dka, dva, d_rw[3], d_rw[4],
              jnp.zeros((lp, NP - C_DA - 128), BF16)]
    dproj = jnp.concatenate([p.astype(BF16) for p in pieces], axis=1)
    grads["w_in_p"] = _mm(u, dproj, ta=True, name="dw_in")
    grads["b_in"] = _w_in_unpadded(_colsum(dproj, name="db_in"))
    du = _mm(dproj, w_in_p, tb=True, name="d_u")
    dh, grads["norm_mix_g"] = _rows_bwd(_rms_fn, [_view(hpad)], [row("norm_mix_g")], [_view(du)], nblk=nall,
                                        name="norm_mix_bwd", diff_rows=[0], diff_consts=[0], acc=[_view(lead(dh1))])
    grads["meta_tokens"] = dh[PAD_ROWS:BLOCK]
    return loss, dh[BLOCK:], grads, early_parts


def kernel(x, meta_tokens, norm_mix_g, w_in, b_in, attn_sinks, rwkv_mix, rwkv_w0, rwkv_w2, rwkv_a0, rwkv_a2, rwkv_g2, rwkv_k_k, rwkv_k_a, rwkv_r_k, rwkv_ln_w, rwkv_ln_b, w_br_attn, w_br_rwkv, w_o, norm_ffn_g, w_ffn_gate, w_ffn_up, w_ffn_down, norm_final_g, loss_target, m_meta_tokens, m_norm_mix_g, m_w_in, m_b_in, m_attn_sinks, m_rwkv_mix, m_rwkv_w0, m_rwkv_w2, m_rwkv_a0, m_rwkv_a2, m_rwkv_g2, m_rwkv_k_k, m_rwkv_k_a, m_rwkv_r_k, m_rwkv_ln_w, m_rwkv_ln_b, m_w_br_attn, m_w_br_rwkv, m_w_o, m_norm_ffn_g, m_w_ffn_gate, m_w_ffn_up, m_w_ffn_down, m_norm_final_g, v_meta_tokens, v_norm_mix_g, v_w_in, v_b_in, v_attn_sinks, v_rwkv_mix, v_rwkv_w0, v_rwkv_w2, v_rwkv_a0, v_rwkv_a2, v_rwkv_g2, v_rwkv_k_k, v_rwkv_k_a, v_rwkv_r_k, v_rwkv_ln_w, v_rwkv_ln_b, v_w_br_attn, v_w_br_rwkv, v_w_o, v_norm_ffn_g, v_w_ffn_gate, v_w_ffn_up, v_w_ffn_down, v_norm_final_g):
    given = dict(locals())
    wts = {n: _strip(n, given[n]) for n in WEIGHTS}
    mom = {n: _strip(n, given["m_" + n]) for n in WEIGHTS}
    var = {n: _strip(n, given["v_" + n]) for n in WEIGHTS}
    small_shapes = [wts[n].shape for n in REPLICATED]
    width = wts["w_in"].shape[1]
    wire = lambda table: [wts[n].astype(BF16) for n, _ in table]

    w_in_all, *early_all = _all_gather([wts["w_in"].astype(BF16)] + wire(EARLY), name="gather_weights")
    full = {n: wts[n] for n in REPLICATED}
    full.update({n: _join(g, axis) for (n, axis), g in zip(EARLY, early_all)})
    full["w_in_p"] = _w_in_padded(w_in_all, shard_width=width)
    gather_late = (wire(LATE), lambda got: {n: _join(g, axis) for (n, axis), g in zip(LATE, got)})
    scatter_early = lambda g: [_split(g[n], axis).astype(BF16) for n, axis in LATE]

    loss_part, grad_x, grads, parts_late = _device_step(x[0], loss_target[0], full, gather_late, scatter_early)

    g_w_in = jnp.stack([_w_in_unpadded(grads["w_in_p"], p * width, (p + 1) * width) for p in range(N_DEV)])
    g_early = [_split(grads[n], axis).astype(BF16) for n, axis in EARLY]
    zero = jnp.zeros((1,), F32)
    g_small = _pack([grads[n].reshape(wts[n].shape) for n in REPLICATED] + [loss_part.reshape(1)], 8)
    parts_w_in, *parts_early, parts_small = _exchange([g_w_in.astype(BF16)] + g_early, [g_small],
                                                      name="exchange_grads")
    results = [{}, {}, {}, {}]
    for (n, _), parts in zip([("w_in", 1)] + EARLY + LATE, [parts_w_in] + parts_early + list(parts_late)):
        for kind, a in enumerate(_adamw(parts, wts[n], mom[n], var[n], name="adamw_" + n)):
            results[kind][n] = a
    small = _adamw(parts_small, _pack([wts[n] for n in REPLICATED] + [zero], 8),
                   _pack([mom[n] for n in REPLICATED] + [zero], 8), _pack([var[n] for n in REPLICATED] + [zero], 8),
                   name="adamw_replicated")
    for kind in range(4):
        for n, a in zip(REPLICATED, _unpack(small[kind], small_shapes)):
            results[kind][n] = a
    loss = _unpack(small[0], small_shapes + [(1,)])[-1][0]
    out = [loss, grad_x[None]]
    for kind in range(4):
        out += [results[kind][n].reshape(given[n].shape) for n in WEIGHTS]
    return tuple(out)
```

```python
import functools

import jax
import jax.numpy as jnp
from jax import lax
from jax.experimental import pallas as pl
from jax.experimental.pallas import tpu as pltpu

F32 = jnp.float32
BF16 = jnp.bfloat16

N_DEV = 8
D_MODEL = 1024
N_META = 16
BLOCK = 128
PAD_ROWS = BLOCK - N_META
HEAD_DIM = 64
Q_HEADS = 8
KV_HEADS = 2
GROUP = Q_HEADS // KV_HEADS
ROPE_DIM = HEAD_DIM // 4
ROPE_HALF = ROPE_DIM // 2
ROPE_THETA = 500000.0
RW_HEADS = 8
RW_DIM = 512
DECAY_LORA = 64
AAA_LORA = 64
GATE_LORA = 160
D_FF = 2816
D_IN = 4640
RMS_EPS = 1e-6
RWKV_LN_EPS = 64e-5
NEG_INF = -1e30
SCAN_T = 16
LANES = 128
PACK_ROWS = 256

ADAM_LR = 0.001
ADAM_B1 = 0.9
ADAM_B2 = 0.999
ADAM_EPS = 1e-08
ADAM_WD = 0.01
ADAM_STEP = 10

C_R, C_K, C_V, C_Q = 0, 512, 1024, 1536
C_G1, C_G2 = 2048, 3072
C_DG, C_KA, C_VA, C_DW, C_DA = 4096, 4352, 4480, 4608, 4736
NP = 5120

VMEM_LIMIT = 48 * 1024 * 1024


def _cparams(sem):
    return pltpu.CompilerParams(dimension_semantics=sem, vmem_limit_bytes=VMEM_LIMIT)


def _pick(n, cands):
    for c in cands:
        if n % c == 0:
            return c
    raise ValueError(f"no tile for {n}")


def _mm(a, b, *, ta=False, tb=False, bias=None, residual=None, name):
    m = a.shape[1] if ta else a.shape[0]
    k = a.shape[0] if ta else a.shape[1]
    n = b.shape[0] if tb else b.shape[1]
    assert k == (b.shape[1] if tb else b.shape[0]), (a.shape, b.shape, ta, tb)
    tm = _pick(m, (512, 1408, 256, 128) if ta else (512, 528, 384, 256, 128))
    tn = _pick(n, (512, 1408, 256, 128))
    if k <= 1024:
        tk = k
    else:
        tk = _pick(k, (1024, 1056, 528, 512) if (ta and not tb) else (1024, 1408, 512, 256, 128))
    nk = k // tk
    has_bias = bias is not None
    has_res = residual is not None
    dn = (((0 if ta else 1,), (1 if tb else 0,)), ((), ()))

    def body(*refs):
        a_ref, b_ref = refs[0], refs[1]
        pos = 2
        bias_ref = res_ref = None
        if has_bias:
            bias_ref = refs[pos]
            pos += 1
        if has_res:
            res_ref = refs[pos]
            pos += 1
        o_ref, acc_ref = refs[pos], refs[pos + 1]
        kk = pl.program_id(2)
        part = lax.dot_general(a_ref[...].astype(BF16), b_ref[...].astype(BF16), dn, preferred_element_type=F32)

        def finish(out):
            if has_bias:
                out = out + bias_ref[...]
            if has_res:
                out = out + res_ref[...]
            o_ref[...] = out

        if nk == 1:
            finish(part)
        else:
            @pl.when(kk == 0)
            def _():
                acc_ref[...] = part

            @pl.when((kk > 0) & (kk < nk - 1))
            def _():
                acc_ref[...] += part

            @pl.when(kk == nk - 1)
            def _():
                finish(acc_ref[...] + part)

    in_specs = [
        pl.BlockSpec((tk, tm), lambda i, j, kk: (kk, i)) if ta else pl.BlockSpec((tm, tk), lambda i, j, kk: (i, kk)),
        pl.BlockSpec((tn, tk), lambda i, j, kk: (j, kk)) if tb else pl.BlockSpec((tk, tn), lambda i, j, kk: (kk, j)),
    ]
    args = [a, b]
    if has_bias:
        in_specs.append(pl.BlockSpec((1, tn), lambda i, j, kk: (0, j)))
        args.append(bias)
    if has_res:
        in_specs.append(pl.BlockSpec((tm, tn), lambda i, j, kk: (i, j)))
        args.append(residual)
    return pl.pallas_call(
        body, name=name, grid=(m // tm, n // tn, nk),
        in_specs=in_specs, out_specs=pl.BlockSpec((tm, tn), lambda i, j, kk: (i, j)),
        out_shape=jax.ShapeDtypeStruct((m, n), F32),
        scratch_shapes=[pltpu.VMEM((tm, tn) if nk > 1 else (8, LANES), F32)],
        compiler_params=_cparams(("parallel", "parallel", "arbitrary")),
    )(*args)


def _colsum(x, name):
    m, n = x.shape
    tm = BLOCK

    def body(x_ref, o_ref):
        i = pl.program_id(0)
        s = jnp.sum(x_ref[...].astype(F32), axis=0, keepdims=True)

        @pl.when(i == 0)
        def _():
            o_ref[...] = s

        @pl.when(i > 0)
        def _():
            o_ref[...] += s

    return pl.pallas_call(
        body, name=name, grid=(m // tm,),
        in_specs=[pl.BlockSpec((tm, n), lambda i: (i, 0))],
        out_specs=pl.BlockSpec((1, n), lambda i: (0, 0)),
        out_shape=jax.ShapeDtypeStruct((1, n), F32),
        compiler_params=_cparams(("arbitrary",)),
    )(x)


def _view(arr, width=None, col=0, off=0, rows=BLOCK):
    return (arr, arr.shape[1] if width is None else width, col, off, rows)


def _row_spec(view):
    _, width, col, off, rows = view
    if off < 0:
        return pl.BlockSpec((rows, width), lambda i, col=col, off=off: (jnp.maximum(i + off, 0), col))
    return pl.BlockSpec((rows, width), lambda i, col=col, off=off: (i + off, col))


def _const_spec(arr):
    return pl.BlockSpec(arr.shape, lambda i: (0,) * arr.ndim)


def _rows_fwd(fn, rows, consts, out_widths, *, nblk, name, out_dtype=F32):
    nr, nc = len(rows), len(consts)
    out_blocks = [(BLOCK, w) if isinstance(w, int) else w for w in out_widths]

    def body(*refs):
        i = pl.program_id(0)
        vals = [r[...] for r in refs[:nr + nc]]
        outs = fn(i, *vals)
        for o_ref, o in zip(refs[nr + nc:], outs):
            o_ref[...] = o.astype(o_ref.dtype)

    return pl.pallas_call(
        body, name=name, grid=(nblk,),
        in_specs=[_row_spec(v) for v in rows] + [_const_spec(c) for c in consts],
        out_specs=[pl.BlockSpec(b, lambda i: (i, 0)) for b in out_blocks],
        out_shape=[jax.ShapeDtypeStruct((nblk * r, w), out_dtype) for r, w in out_blocks],
        compiler_params=_cparams(("parallel",)),
    )(*[v[0] for v in rows], *consts)


def _rows_bwd(fn, rows, consts, cts, *, nblk, name, diff_rows, diff_consts, acc=None, fwd_widths=(), row_dtype=F32,
              ct_map=None):
    nr, nc = len(rows), len(consts)
    ct_views = [c for c in cts if c is not None]
    acc = acc or [None] * len(diff_rows)
    acc_views = [a for a in acc if a is not None]
    n_in = nr + nc + len(ct_views) + len(acc_views)
    n_fwd = len(fwd_widths)

    def body(*refs):
        i = pl.program_id(0)
        row_vals = [r[...] for r in refs[:nr]]
        const_vals = [r[...] for r in refs[nr:nr + nc]]
        ct_vals = [r[...] for r in refs[nr + nc:nr + nc + len(ct_views)]]
        acc_vals = [r[...] for r in refs[nr + nc + len(ct_views):n_in]]
        out_refs = refs[n_in:]

        def f(*dargs):
            rv = list(row_vals)
            cv = list(const_vals)
            for pos, idx in enumerate(diff_rows):
                rv[idx] = dargs[pos]
            for pos, idx in enumerate(diff_consts):
                cv[idx] = dargs[len(diff_rows) + pos]
            return tuple(fn(i, *rv, *cv))

        primals = [row_vals[idx] for idx in diff_rows] + [const_vals[idx] for idx in diff_consts]
        outs, pull = jax.vjp(f, *primals)
        full_ct, ci = [], 0
        if ct_map is not None:
            full_ct = ct_map(i, *ct_vals)
        else:
            for o, c in zip(outs, cts):
                if c is None:
                    full_ct.append(jnp.zeros_like(o))
                else:
                    full_ct.append(ct_vals[ci])
                    ci += 1
        grads = pull(tuple(full_ct))
        for o_ref, o in zip(out_refs[:n_fwd], outs):
            o_ref[...] = o
        ai = 0
        for pos in range(len(diff_rows)):
            g = grads[pos]
            if acc[pos] is not None:
                g = g + acc_vals[ai]
                ai += 1
            out_refs[n_fwd + pos][...] = g.astype(row_dtype)
        for pos in range(len(diff_consts)):
            g = grads[len(diff_rows) + pos]
            o_ref = out_refs[n_fwd + len(diff_rows) + pos]

            @pl.when(i == 0)
            def _(o_ref=o_ref, g=g):
                o_ref[...] = g

            @pl.when(i > 0)
            def _(o_ref=o_ref, g=g):
                o_ref[...] += g

    out_specs = [pl.BlockSpec((BLOCK, w), lambda i: (i, 0)) for w in fwd_widths]
    out_shape = [jax.ShapeDtypeStruct((nblk * BLOCK, w), F32) for w in fwd_widths]
    for idx in diff_rows:
        out_specs.append(pl.BlockSpec((BLOCK, rows[idx][1]), lambda i: (i, 0)))
        out_shape.append(jax.ShapeDtypeStruct((nblk * BLOCK, rows[idx][1]), row_dtype))
    for idx in diff_consts:
        out_specs.append(_const_spec(consts[idx]))
        out_shape.append(jax.ShapeDtypeStruct(consts[idx].shape, F32))
    return pl.pallas_call(
        body, name=name, grid=(nblk,),
        in_specs=([_row_spec(v) for v in rows] + [_const_spec(c) for c in consts]
                  + [_row_spec(v) for v in ct_views] + [_row_spec(v) for v in acc_views]),
        out_specs=out_specs, out_shape=out_shape,
        compiler_params=_cparams(("arbitrary",)),
    )(*[v[0] for v in rows], *consts, *[v[0] for v in ct_views], *[v[0] for v in acc_views])


def _rms_fn(i, x, g):
    return (x * lax.rsqrt(jnp.mean(x * x, axis=-1, keepdims=True) + RMS_EPS) * g,)


def _sigmoid(x):
    return 1.0 / (1.0 + jnp.exp(-x))


def _softplus(x):
    return jnp.maximum(x, 0.0) + jnp.log(1.0 + jnp.exp(-jnp.abs(x)))


def _split2(x):
    hi = x.astype(BF16)
    lo = (x - hi.astype(F32)).astype(BF16)
    return jnp.concatenate([hi, lo], axis=1)


@jax.custom_vjp
def _head_sum(x):
    r = lax.broadcasted_iota(jnp.int32, (2 * RW_DIM, RW_DIM), 0) % RW_DIM // HEAD_DIM
    c = lax.broadcasted_iota(jnp.int32, (2 * RW_DIM, RW_DIM), 1) // HEAD_DIM
    return jnp.dot(_split2(x), (r == c).astype(BF16), preferred_element_type=F32)


_head_sum.defvjp(lambda x: (_head_sum(x), None), lambda _, ct: (_head_sum(ct),))


@jax.custom_vjp
def _dot_bf16(x, w):
    return jnp.dot(x.astype(BF16), w.astype(BF16), preferred_element_type=F32)


def _dot_bf16_bwd(res, ct):
    x, w = res
    ct = ct.astype(BF16)
    dx = lax.dot_general(ct, w.astype(BF16), (((1,), (1,)), ((), ())), preferred_element_type=F32)
    dw = lax.dot_general(x.astype(BF16), ct, (((0,), (0,)), ((), ())), preferred_element_type=F32)
    return dx, dw


_dot_bf16.defvjp(lambda x, w: (_dot_bf16(x, w), (x, w)), _dot_bf16_bwd)


def _rwkv_pre_fn(i, r, k, v, dw, da, dg, r_p, k_p, v_p, dw_p, da_p, dg_p,
                 mix_r, mix_k, mix_v, mix_dw, mix_da, mix_dg, w0, w2, a0, a2, g2, k_k, k_a):
    row = i * BLOCK + lax.broadcasted_iota(jnp.int32, (BLOCK, 1), 0)
    live = row >= PAD_ROWS
    live_prev = row >= PAD_ROWS + 1

    def shift(cur, prev, mix):
        cur = jnp.where(live, cur, 0.0)
        prev = jnp.where(live_prev, prev, 0.0)
        return cur + (prev - cur) * mix

    r = shift(r, r_p, mix_r)
    k = shift(k, k_p, mix_k)
    v = shift(v, v_p, mix_v)
    dw = shift(dw, dw_p, mix_dw)
    da = shift(da, da_p, mix_da)
    dg = shift(dg, dg_p, mix_dg)
    wlog = -_softplus(-(w0 + _dot_bf16(jnp.tanh(dw), w2))) - 0.5
    decay = jnp.exp(-jnp.exp(wlog))
    a = _sigmoid(a0 + _dot_bf16(da, a2))
    g = _dot_bf16(_sigmoid(dg), g2)
    kk = k * k_k
    norm_sq = jnp.where(live, _head_sum(kk * kk), 1.0)
    kk = kk / jnp.maximum(jnp.sqrt(norm_sq), 1e-12)
    k_mod = k * (1.0 + (a - 1.0) * k_a)
    return r, decay, k_mod, v, -kk, kk * a, g


def _rwkv_pre_xt_fn(i, *args):
    r, decay, k_mod, v, a_neg, b, g = _rwkv_pre_fn(i, *args)
    t = SCAN_T
    xt = jnp.concatenate([_rows_to_xt(x[c * t:(c + 1) * t]) for c in range(BLOCK // t)
                          for x in (a_neg, decay, b, k_mod, r)], axis=0)
    return r, k_mod, v, g, xt


def _rwkv_pre_cts(i, dxt, dv_s, dr_p, dk_p, dv_p, dg_p):
    t = SCAN_T
    d_a, d_w, d_b, d_k, d_r = [
        jnp.concatenate([_xt_to_rows(dxt[c * VEC_ROWS + n * HEAD_DIM:c * VEC_ROWS + (n + 1) * HEAD_DIM])
                         for c in range(BLOCK // t)], axis=0) for n in range(N_VEC)]
    dr_p, dk_p, dv_p, dg_p = [jnp.where(i > 0, x, 0.0) for x in (dr_p, dk_p, dv_p, dg_p)]
    return d_r + dr_p, d_w, d_k + dk_p, dv_s + dv_p, d_a, d_b, dg_p


def _rwkv_post_fn(i, ys, r, k_mod, v, g, ln_w, ln_b, r_k):
    mean = _head_sum(ys) * (1.0 / HEAD_DIM)
    d = ys - mean
    var = _head_sum(d * d) * (1.0 / HEAD_DIM)
    yn = d * lax.rsqrt(var + RWKV_LN_EPS) * ln_w + ln_b
    bonus = _head_sum(r * k_mod * r_k) * v
    return ((yn + bonus) * g,)


def _merge_fn(i, ya, yr, g1, g2):
    return (_sigmoid(g1) * ya + _sigmoid(g2) * yr,)


def _swiglu_fn(i, gate, up):
    return (gate * _sigmoid(gate) * up,)


def _loss_fn(i, h, tgt, g):
    y = h * lax.rsqrt(jnp.mean(h * h, axis=-1, keepdims=True) + RMS_EPS) * g
    err = y - tgt
    return (0.5 * jnp.mean(err * err, axis=-1, keepdims=True),)


def _rope_tables(lp):
    pos = (jnp.arange(lp, dtype=jnp.int32) - PAD_ROWS).astype(F32)
    inv_freq = jnp.power(jnp.float32(ROPE_THETA), -jnp.arange(ROPE_HALF, dtype=F32) * (2.0 / ROPE_DIM))
    ang = pos[:, None] * inv_freq[None, :]
    cos, sin = jnp.cos(ang), jnp.sin(ang)
    one = jnp.ones((lp, HEAD_DIM - ROPE_DIM), F32)
    zero_h = jnp.zeros((lp, ROPE_HALF), F32)
    zero_r = jnp.zeros((lp, HEAD_DIM - ROPE_DIM), F32)
    c = jnp.concatenate([cos, cos, one], axis=1)
    s1 = jnp.concatenate([-sin, zero_h, zero_r], axis=1)
    s2 = jnp.concatenate([zero_h, sin, zero_r], axis=1)
    return tuple(jnp.tile(t, (1, LANES // HEAD_DIM)) for t in (c, s1, s2))


def _rope_fwd_fn(i, x, c, s1, s2):
    n = x.shape[1]
    c, s1, s2 = [jnp.tile(t, (1, n // LANES)) for t in (c, s1, s2)]
    return (x * c + pltpu.roll(x, n - ROPE_HALF, 1) * s1 + pltpu.roll(x, ROPE_HALF, 1) * s2,)


def _rope_bwd_fn(i, dy, c, s1, s2):
    n = dy.shape[1]
    c, s1, s2 = [jnp.tile(t, (1, n // LANES)) for t in (c, s1, s2)]
    return (dy * c + pltpu.roll(dy * s1, ROPE_HALF, 1) + pltpu.roll(dy * s2, n - ROPE_HALF, 1),)


def _attn_mask(i):
    r = lax.broadcasted_iota(jnp.int32, (BLOCK, 3 * BLOCK), 0)
    c = lax.broadcasted_iota(jnp.int32, (BLOCK, 3 * BLOCK), 1)
    meta = (c < BLOCK) & (c >= PAD_ROWS)
    prev = (c >= BLOCK) & (c < 2 * BLOCK) & ((c - BLOCK) > r) & (i >= 1)
    cur = (c >= 2 * BLOCK) & ((c - 2 * BLOCK) <= r)
    return meta | prev | cur


def _attn_fwd(q, k, v, sinks, *, nblk, name):
    scale = HEAD_DIM ** -0.5

    def body(q_ref, km_ref, kp_ref, kc_ref, vm_ref, vp_ref, vc_ref, s_ref, o_ref, lse_ref):
        i = pl.program_id(0)
        valid = _attn_mask(i)
        for h in range(Q_HEADS):
            g = h // GROUP
            kcat = jnp.concatenate([km_ref[g], kp_ref[g], kc_ref[g]], axis=0).astype(BF16)
            vcat = jnp.concatenate([vm_ref[g], vp_ref[g], vc_ref[g]], axis=0).astype(BF16)
            s = lax.dot_general(q_ref[h].astype(BF16), kcat, (((1,), (1,)), ((), ())),
                                preferred_element_type=F32) * scale
            s = jnp.where(valid, s, NEG_INF)
            sink = s_ref[0:1, h:h + 1]
            m = jnp.maximum(jnp.max(s, axis=-1, keepdims=True), sink)
            p = jnp.exp(s - m)
            den = jnp.sum(p, axis=-1, keepdims=True) + jnp.exp(sink - m)
            o = jnp.dot(p.astype(BF16), vcat, preferred_element_type=F32)
            o_ref[h] = o / den
            lse_ref[h] = m + jnp.log(den)

    kv = lambda f: pl.BlockSpec((KV_HEADS, BLOCK, HEAD_DIM), f)
    return pl.pallas_call(
        body, name=name, grid=(nblk,),
        in_specs=[pl.BlockSpec((Q_HEADS, BLOCK, HEAD_DIM), lambda i: (0, i + 1, 0)),
                  kv(lambda i: (0, 0, 0)), kv(lambda i: (0, i, 0)), kv(lambda i: (0, i + 1, 0)),
                  kv(lambda i: (0, 0, 0)), kv(lambda i: (0, i, 0)), kv(lambda i: (0, i + 1, 0)),
                  pl.BlockSpec((1, Q_HEADS), lambda i: (0, 0))],
        out_specs=[pl.BlockSpec((Q_HEADS, BLOCK, HEAD_DIM), lambda i: (0, i, 0)),
                   pl.BlockSpec((Q_HEADS, BLOCK, 1), lambda i: (0, i, 0))],
        out_shape=[jax.ShapeDtypeStruct((Q_HEADS, nblk * BLOCK, HEAD_DIM), F32),
                   jax.ShapeDtypeStruct((Q_HEADS, nblk * BLOCK, 1), F32)],
        compiler_params=_cparams(("parallel",)),
    )(q, k, k, k, v, v, v, sinks)


def _attn_bwd(q, k, v, sinks, o, lse, do, *, nblk, name):
    scale = HEAD_DIM ** -0.5
    lp = k.shape[1]

    def body(q_ref, km_ref, kp_ref, kc_ref, vm_ref, vp_ref, vc_ref, s_ref, o_ref, lse_ref, do_ref,
             dq_ref, dk_ref, dv_ref, ds_ref):
        i = pl.program_id(0)

        @pl.when(i == 0)
        def _():
            dk_ref[...] = jnp.zeros_like(dk_ref)
            dv_ref[...] = jnp.zeros_like(dv_ref)
            ds_ref[...] = jnp.zeros_like(ds_ref)

        valid = _attn_mask(i)
        lane = lax.broadcasted_iota(jnp.int32, (1, Q_HEADS), 1)
        prev_rows = pl.ds(pl.multiple_of(i * BLOCK, BLOCK), BLOCK)
        cur_rows = pl.ds(pl.multiple_of((i + 1) * BLOCK, BLOCK), BLOCK)
        for h in range(Q_HEADS):
            g = h // GROUP
            qh = q_ref[h].astype(BF16)
            doh = do_ref[h]
            kcat = jnp.concatenate([km_ref[g], kp_ref[g], kc_ref[g]], axis=0).astype(BF16)
            vcat = jnp.concatenate([vm_ref[g], vp_ref[g], vc_ref[g]], axis=0).astype(BF16)
            s = lax.dot_general(qh, kcat, (((1,), (1,)), ((), ())), preferred_element_type=F32) * scale
            s = jnp.where(valid, s, NEG_INF)
            lse_h = lse_ref[h]
            p = jnp.exp(s - lse_h)
            delta = jnp.sum(doh * o_ref[h], axis=-1, keepdims=True)
            dp = lax.dot_general(doh.astype(BF16), vcat, (((1,), (1,)), ((), ())), preferred_element_type=F32)
            dsc = (p * (dp - delta) * scale).astype(BF16)
            dq_ref[h] = jnp.dot(dsc, kcat, preferred_element_type=F32)
            dk_all = lax.dot_general(dsc, qh, (((0,), (0,)), ((), ())), preferred_element_type=F32)
            dv_all = lax.dot_general(p.astype(BF16), doh.astype(BF16), (((0,), (0,)), ((), ())),
                                     preferred_element_type=F32)
            dk_ref[g, 0:BLOCK, :] += dk_all[0:BLOCK]
            dk_ref[g, prev_rows, :] += dk_all[BLOCK:2 * BLOCK]
            dk_ref[g, cur_rows, :] += dk_all[2 * BLOCK:]
            dv_ref[g, 0:BLOCK, :] += dv_all[0:BLOCK]
            dv_ref[g, prev_rows, :] += dv_all[BLOCK:2 * BLOCK]
            dv_ref[g, cur_rows, :] += dv_all[2 * BLOCK:]
            p_sink = jnp.exp(s_ref[0:1, h:h + 1] - lse_h)
            dsink = -jnp.sum(p_sink * delta, axis=0, keepdims=True)
            ds_ref[...] += jnp.where(lane == h, dsink, 0.0)

    kv = lambda f: pl.BlockSpec((KV_HEADS, BLOCK, HEAD_DIM), f)
    qspec = pl.BlockSpec((Q_HEADS, BLOCK, HEAD_DIM), lambda i: (0, i, 0))
    whole = pl.BlockSpec((KV_HEADS, lp, HEAD_DIM), lambda i: (0, 0, 0))
    return pl.pallas_call(
        body, name=name, grid=(nblk,),
        in_specs=[pl.BlockSpec((Q_HEADS, BLOCK, HEAD_DIM), lambda i: (0, i + 1, 0)),
                  kv(lambda i: (0, 0, 0)), kv(lambda i: (0, i, 0)), kv(lambda i: (0, i + 1, 0)),
                  kv(lambda i: (0, 0, 0)), kv(lambda i: (0, i, 0)), kv(lambda i: (0, i + 1, 0)),
                  pl.BlockSpec((1, Q_HEADS), lambda i: (0, 0)),
                  qspec, pl.BlockSpec((Q_HEADS, BLOCK, 1), lambda i: (0, i, 0)), qspec],
        out_specs=[qspec, whole, whole, pl.BlockSpec((1, Q_HEADS), lambda i: (0, 0))],
        out_shape=[jax.ShapeDtypeStruct((Q_HEADS, nblk * BLOCK, HEAD_DIM), F32),
                   jax.ShapeDtypeStruct((KV_HEADS, lp, HEAD_DIM), F32),
                   jax.ShapeDtypeStruct((KV_HEADS, lp, HEAD_DIM), F32),
                   jax.ShapeDtypeStruct((1, Q_HEADS), F32)],
        compiler_params=_cparams(("arbitrary",)),
    )(q, k, k, k, v, v, v, sinks, o, lse, do)


N_VEC = 5
VEC_ROWS = N_VEC * HEAD_DIM


def _selectors():
    t = SCAN_T
    shape = (t, 2 * LANES, RW_DIM)
    step, src, dst = [lax.broadcasted_iota(jnp.int32, shape, d) for d in range(3)]
    src = src % LANES
    spread = ((src // t == dst // HEAD_DIM) & (src % t == step)).astype(BF16)
    shape = (t, RW_DIM, LANES)
    step, src, dst = [lax.broadcasted_iota(jnp.int32, shape, d) for d in range(3)]
    collect = ((src // HEAD_DIM == dst // t) & (dst % t == step)).astype(BF16)
    return spread, collect


def _rows_to_xt(x):
    low = lax.broadcasted_iota(jnp.int32, (SCAN_T, LANES), 1) < HEAD_DIM
    pieces = []
    for m in range(RW_HEADS // 2):
        pair = x[:, m * LANES:(m + 1) * LANES]
        pieces += [jnp.where(low, pair, 0.0), jnp.where(low, pltpu.roll(pair, HEAD_DIM, 1), 0.0)]
    return jnp.concatenate(pieces, axis=0).T[:HEAD_DIM]


def _xt_to_rows(a):
    t = SCAN_T
    a_t = jnp.concatenate([a, jnp.zeros_like(a)], axis=0).T
    pairs = [a_t[2 * m * t:(2 * m + 1) * t] + pltpu.roll(a_t[(2 * m + 1) * t:(2 * m + 2) * t], HEAD_DIM, 1)
             for m in range(RW_HEADS // 2)]
    return jnp.concatenate(pairs, axis=1)


def _with_exchange(compute, n_in, n_out, n_scratch, scattered, shared, last_step):
    n_sc = len(scattered)
    n_x = n_sc + len(shared)
    if n_x == 0:
        return compute

    def body(*refs):
        ins, x_in = refs[:n_in], refs[n_in:n_in + n_x]
        outs, x_out = refs[n_in + n_x:n_in + n_x + n_out], refs[n_in + n_x + n_out:n_in + 2 * n_x + n_out]
        scratch = refs[n_in + 2 * n_x + n_out:n_in + 2 * n_x + n_out + n_scratch]
        sems = refs[n_in + 2 * n_x + n_out + n_scratch:]

        @pl.when(pl.program_id(0) == 0)
        def _():
            for cp in _exchange_copies(x_in, x_out, n_sc, *sems):
                cp.start()

        compute(*ins, *outs, *scratch)

        @pl.when(pl.program_id(0) == last_step)
        def _():
            for cp in _exchange_copies(x_in, x_out, n_sc, *sems):
                cp.wait()

    return body


def _wkv_fwd(xt, v, spread, name, shared=()):
    t_steps = SCAN_T
    nch = xt.shape[0]
    n_x = len(shared)

    def compute(xt_ref, v_ref, sel_ref, y_ref, hist_ref, st_ref):
        @pl.when(pl.program_id(0) == 0)
        def _():
            st_ref[...] = jnp.zeros_like(st_ref)

        x2 = _split2(xt_ref[0])
        st = st_ref[...]
        for j in range(t_steps):
            cols = jnp.dot(x2, sel_ref[j], preferred_element_type=F32)
            a_c, w_c, b_c, k_c, r_c = [cols[n * HEAD_DIM:(n + 1) * HEAD_DIM] for n in range(N_VEC)]
            hist_ref[j] = st
            sa = jnp.sum(st * a_c, axis=0, keepdims=True)
            st = st * w_c + b_c * sa + k_c * v_ref[j:j + 1, :]
            y_ref[j:j + 1, :] = jnp.sum(st * r_c, axis=0, keepdims=True)
        st_ref[...] = st

    return pl.pallas_call(
        _with_exchange(compute, 3, 2, 1, (), shared, nch - 1), name=name, grid=(nch,),
        in_specs=[pl.BlockSpec((1, VEC_ROWS, LANES), lambda c: (c, 0, 0)),
                  pl.BlockSpec((t_steps, RW_DIM), lambda c: (c, 0)),
                  pl.BlockSpec(spread.shape, lambda c: (0, 0, 0))] + [ANY] * n_x,
        out_specs=[pl.BlockSpec((t_steps, RW_DIM), lambda c: (c, 0)),
                   pl.BlockSpec((t_steps, HEAD_DIM, RW_DIM), lambda c: (c, 0, 0))] + [ANY] * n_x,
        out_shape=[jax.ShapeDtypeStruct((nch * t_steps, RW_DIM), F32),
                   jax.ShapeDtypeStruct((nch * t_steps, HEAD_DIM, RW_DIM), F32)] + _exchange_shapes((), shared),
        scratch_shapes=[pltpu.VMEM((HEAD_DIM, RW_DIM), F32)] + (_exchange_sems(n_x) if n_x else []),
        compiler_params=_cparams(("arbitrary",)),
    )(xt, v, spread, *shared)


def _wkv_bwd(xt, v, hist, dy, spread, collect, name, scattered=()):
    t_steps = SCAN_T
    nch = xt.shape[0]
    n_x = len(scattered)
    lead = BLOCK // t_steps

    def compute(xt_ref, v_ref, hist_ref, dy_ref, sel_ref, col_ref, dxt_ref, dv_ref, g_ref):
        @pl.when(pl.program_id(0) == 0)
        def _():
            g_ref[...] = jnp.zeros_like(g_ref)

        x2 = _split2(xt_ref[0])
        has_dy = nch - 1 - pl.program_id(0) >= lead
        gst = g_ref[...]
        acc = jnp.zeros((VEC_ROWS, LANES), F32)
        nxt = None
        for j in reversed(range(t_steps)):
            cols = jnp.dot(x2, sel_ref[j], preferred_element_type=F32)
            a_c, w_c, b_c, k_c, r_c = [cols[n * HEAD_DIM:(n + 1) * HEAD_DIM] for n in range(N_VEC)]
            prev = hist_ref[j]
            v_row = v_ref[j:j + 1, :]
            dy_row = jnp.where(has_dy, dy_ref[j:j + 1, :], 0.0)
            sa = jnp.sum(prev * a_c, axis=0, keepdims=True)
            if nxt is None:
                nxt = prev * w_c + b_c * sa + k_c * v_row
            gst = gst + r_c * dy_row
            dv_ref[j:j + 1, :] = jnp.sum(gst * k_c, axis=0, keepdims=True)
            dsa = jnp.sum(gst * b_c, axis=0, keepdims=True)
            prods = jnp.concatenate([p.astype(BF16) for p in
                                     (prev * dsa, gst * prev, gst * sa, gst * v_row, nxt * dy_row)], axis=0)
            acc = acc + jnp.dot(prods, col_ref[j], preferred_element_type=F32)
            gst = gst * w_c + a_c * dsa
            nxt = prev
        dxt_ref[0] = acc
        g_ref[...] = gst

    rev3 = lambda c: (nch - 1 - c, 0, 0)
    rev2 = lambda c: (nch - 1 - c, 0)
    rowspec = pl.BlockSpec((t_steps, RW_DIM), rev2)
    return pl.pallas_call(
        _with_exchange(compute, 6, 2, 1, scattered, (), nch - 1), name=name, grid=(nch,),
        in_specs=[pl.BlockSpec((1, VEC_ROWS, LANES), rev3), rowspec,
                  pl.BlockSpec((t_steps, HEAD_DIM, RW_DIM), rev3),
                  pl.BlockSpec((t_steps, RW_DIM), lambda c: (jnp.maximum(nch - 1 - c - lead, 0), 0)),
                  pl.BlockSpec(spread.shape, lambda c: (0, 0, 0)),
                  pl.BlockSpec(collect.shape, lambda c: (0, 0, 0))] + [ANY] * n_x,
        out_specs=[pl.BlockSpec((1, VEC_ROWS, LANES), rev3), rowspec] + [ANY] * n_x,
        out_shape=[jax.ShapeDtypeStruct((nch, VEC_ROWS, LANES), F32),
                   jax.ShapeDtypeStruct((nch * t_steps, RW_DIM), F32)] + _exchange_shapes(scattered, ()),
        scratch_shapes=[pltpu.VMEM((HEAD_DIM, RW_DIM), F32)] + (_exchange_sems(n_x) if n_x else []),
        compiler_params=_cparams(("arbitrary",)),
    )(xt, v, hist, dy, spread, collect, *scattered)


MESH = pl.DeviceIdType.MESH
ANY = pl.BlockSpec(memory_space=pl.ANY)


def _all_gather(arrays, name):
    n_arr = len(arrays)
    per = N_DEV - 1

    def body(*refs):
        x_refs, out_refs = refs[:n_arr], refs[n_arr:2 * n_arr]
        send_sems, recv_sems, local_sems = refs[2 * n_arr:]
        xi, yi, ci = lax.axis_index("x"), lax.axis_index("y"), lax.axis_index("c")
        me, sibling = (xi, yi, ci), (xi, yi, 1 - ci)
        chips = [(1 - xi, yi), (xi, 1 - yi), (1 - xi, 1 - yi)]

        def slot(a, px, py, pc):
            return out_refs[a].at[4 * px + 2 * py + pc]

        def copy(a, sem, block, to, src=None):
            return pltpu.make_async_remote_copy(
                src_ref=slot(a, *block) if src is None else src, dst_ref=slot(a, *block),
                send_sem=send_sems.at[per * a + sem], recv_sem=recv_sems.at[per * a + sem],
                device_id=to, device_id_type=MESH)

        mine = [pltpu.make_async_copy(x_refs[a], slot(a, *me), local_sems.at[a]) for a in range(n_arr)]
        for cp in mine:
            cp.start()
        sent = []
        for a in range(n_arr):
            sent.append(copy(a, 0, me, sibling, src=x_refs[a]))
            sent += [copy(a, 1 + j, me, (*chip, ci), src=x_refs[a]) for j, chip in enumerate(chips)]
        for cp in sent:
            cp.start()
        for j, chip in enumerate(chips):
            for a in range(n_arr):
                copy(a, 1 + j, (*chip, ci), me).wait_recv()
                onward = copy(a, 4 + j, (*chip, ci), sibling)
                onward.start()
                sent.append(onward)
        for a in range(n_arr):
            copy(a, 0, sibling, me).wait_recv()
        for j, chip in enumerate(chips):
            for a in range(n_arr):
                copy(a, 4 + j, (*chip, 1 - ci), me).wait_recv()
        for cp in sent:
            cp.wait_send()
        for cp in mine:
            cp.wait()

    sems = pltpu.SemaphoreType.DMA((per * n_arr,))
    return pl.pallas_call(
        body, name=name, out_shape=[jax.ShapeDtypeStruct((N_DEV,) + a.shape, a.dtype) for a in arrays],
        in_specs=[ANY] * n_arr, out_specs=[ANY] * n_arr,
        scratch_shapes=[sems, sems, pltpu.SemaphoreType.DMA((n_arr,))],
    )(*arrays)


def _exchange_copies(in_refs, out_refs, n_scattered, send_sems, recv_sems, local_sems):
    n_arr = len(in_refs)
    per = N_DEV - 1
    xi, yi, ci = lax.axis_index("x"), lax.axis_index("y"), lax.axis_index("c")
    me = 4 * xi + 2 * yi + ci
    src_of = lambda a, peer: in_refs[a].at[peer] if a < n_scattered else in_refs[a]
    copies = []
    for d in range(1, N_DEV):
        px = 1 - xi if d & 4 else xi
        py = 1 - yi if d & 2 else yi
        pc = 1 - ci if d & 1 else ci
        for a in range(n_arr):
            copies.append(pltpu.make_async_remote_copy(
                src_ref=src_of(a, 4 * px + 2 * py + pc), dst_ref=out_refs[a].at[me],
                send_sem=send_sems.at[per * a + d - 1], recv_sem=recv_sems.at[per * a + d - 1],
                device_id=(px, py, pc), device_id_type=MESH))
    own = [pltpu.make_async_copy(src_of(a, me), out_refs[a].at[me], local_sems.at[a]) for a in range(n_arr)]
    return copies + own


def _exchange_shapes(scattered, shared):
    return ([jax.ShapeDtypeStruct(a.shape, a.dtype) for a in scattered]
            + [jax.ShapeDtypeStruct((N_DEV,) + a.shape, a.dtype) for a in shared])


def _exchange_sems(n_arr):
    sems = pltpu.SemaphoreType.DMA(((N_DEV - 1) * n_arr,))
    return [sems, sems, pltpu.SemaphoreType.DMA((n_arr,))]


def _exchange(scattered, shared, name):
    n_sc = len(scattered)
    n_arr = n_sc + len(shared)

    def body(*refs):
        copies = _exchange_copies(refs[:n_arr], refs[n_arr:2 * n_arr], n_sc, *refs[2 * n_arr:])
        for cp in copies:
            cp.start()
        for cp in copies:
            cp.wait()

    return pl.pallas_call(
        body, name=name, out_shape=_exchange_shapes(scattered, shared),
        in_specs=[ANY] * n_arr, out_specs=[ANY] * n_arr, scratch_shapes=_exchange_sems(n_arr),
    )(*scattered, *shared)


def _adamw(parts, w, m, v, name):
    rows, cols = w.shape
    tile = PACK_ROWS if rows % PACK_ROWS == 0 else rows

    def body(p_ref, w_ref, m_ref, v_ref, g_out, d_out, m_out, v_out):
        g = p_ref[0].astype(F32)
        for s in range(1, N_DEV):
            g = g + p_ref[s].astype(F32)
        m_new = ADAM_B1 * m_ref[...] + (1.0 - ADAM_B1) * g
        v_new = ADAM_B2 * v_ref[...] + (1.0 - ADAM_B2) * (g * g)
        m_hat = m_new / (1.0 - ADAM_B1 ** ADAM_STEP)
        v_hat = v_new / (1.0 - ADAM_B2 ** ADAM_STEP)
        g_out[...] = g
        d_out[...] = -ADAM_LR * (m_hat / (jnp.sqrt(v_hat) + ADAM_EPS) + ADAM_WD * w_ref[...])
        m_out[...] = m_new
        v_out[...] = v_new

    spec = pl.BlockSpec((tile, cols), lambda i: (i, 0))
    return pl.pallas_call(
        body, name=name, grid=(rows // tile,),
        in_specs=[pl.BlockSpec((N_DEV, tile, cols), lambda i: (0, i, 0)), spec, spec, spec],
        out_specs=[spec] * 4, out_shape=[jax.ShapeDtypeStruct((rows, cols), F32)] * 4,
        compiler_params=_cparams(("parallel",)),
    )(parts, w, m, v)


EARLY = [("meta_tokens", 1), ("rwkv_w2", 1), ("rwkv_a2", 1), ("rwkv_g2", 1)]
LATE = [("w_br_attn", 1), ("w_br_rwkv", 1), ("w_o", 0), ("w_ffn_gate", 1), ("w_ffn_up", 1), ("w_ffn_down", 0)]
REPLICATED = ["norm_mix_g", "b_in", "attn_sinks", "rwkv_mix", "rwkv_w0", "rwkv_a0", "rwkv_k_k", "rwkv_k_a",
              "rwkv_r_k", "rwkv_ln_w", "rwkv_ln_b", "norm_ffn_g", "norm_final_g"]
WEIGHTS = ["meta_tokens", "norm_mix_g", "w_in", "b_in", "attn_sinks", "rwkv_mix", "rwkv_w0", "rwkv_w2", "rwkv_a0",
           "rwkv_a2", "rwkv_g2", "rwkv_k_k", "rwkv_k_a", "rwkv_r_k", "rwkv_ln_w", "rwkv_ln_b", "w_br_attn",
           "w_br_rwkv", "w_o", "norm_ffn_g", "w_ffn_gate", "w_ffn_up", "w_ffn_down", "norm_final_g"]


def _pack(arrays, row_multiple):
    flat = jnp.concatenate([a.reshape(-1) for a in arrays])
    per = row_multiple * LANES
    total = -(-flat.shape[0] // per) * per
    return jnp.pad(flat, (0, total - flat.shape[0])).reshape(-1, LANES)


def _unpack(buf, shapes):
    flat = buf.reshape(-1)
    out, pos = [], 0
    for s in shapes:
        n = 1
        for d in s:
            n *= d
        out.append(flat[pos:pos + n].reshape(s))
        pos += n
    return out


def _strip(name, a):
    return a if name in ("meta_tokens", "norm_final_g") else a[0]


def _join(gathered, axis):
    if axis == 0:
        return gathered.reshape(-1, gathered.shape[2])
    return gathered.transpose(1, 0, 2).reshape(gathered.shape[1], -1)


def _split(g, axis):
    if axis == 0:
        return g.reshape(N_DEV, -1, g.shape[1])
    return g.reshape(g.shape[0], N_DEV, -1).transpose(1, 0, 2)


W_IN_LAYOUT = [(768, 2304), (0, 512), (2592, 4640), (2432, 2592), 256 - GATE_LORA, (512, 768), (2304, 2368),
               128 - DECAY_LORA, (2368, 2432), 128 - AAA_LORA, NP - C_DA - 128]


def _w_in_padded(w, shard_width=None):
    rows = w.shape[-2]
    width = D_IN if shard_width is None else shard_width
    parts = []
    for seg in W_IN_LAYOUT:
        if isinstance(seg, int):
            parts.append(jnp.zeros((rows, seg), w.dtype))
            continue
        lo, stop = seg
        while lo < stop:
            p = lo // width
            hi = min(stop, (p + 1) * width)
            src = w if shard_width is None else w[p]
            parts.append(src[:, lo - p * width:hi - p * width])
            lo = hi
    return jnp.concatenate(parts, axis=1)


def _w_in_unpadded(wp, lo=0, stop=D_IN):
    spans, pos = [], 0
    for seg in W_IN_LAYOUT:
        if isinstance(seg, int):
            pos += seg
        else:
            spans.append((seg[0], seg[1], pos))
            pos += seg[1] - seg[0]
    parts = []
    for a, b, at in sorted(spans):
        c, d = max(a, lo), min(b, stop)
        if c < d:
            parts.append(wp[:, at + c - a:at + d - a])
    return jnp.concatenate(parts, axis=1)


def _pad_rows(a, n):
    return jnp.pad(a, ((0, n - a.shape[0]), (0, 0)))


def _device_step(x, tgt, full, gather_late=None, scatter_early=None):
    seq = x.shape[0]
    nblk = seq // BLOCK
    lp = seq + BLOCK
    nall = nblk + 1

    w_in_p = full["w_in_p"]
    b_in_p = _w_in_padded(full["b_in"][None])
    mix = full["rwkv_mix"][None]
    mix_r, mix_k, mix_v = mix[:, 0:512], mix[:, 512:1024], mix[:, 1024:1536]
    mix_dw = jnp.pad(mix[:, 1536:1600], ((0, 0), (0, 64)))
    mix_da = jnp.pad(mix[:, 1600:1664], ((0, 0), (0, 64)))
    mix_dg = jnp.pad(mix[:, 1664:1824], ((0, 0), (0, 96)))
    w2_p = _pad_rows(full["rwkv_w2"].astype(F32), 128)
    a2_p = _pad_rows(full["rwkv_a2"].astype(F32), 128)
    g2_p = _pad_rows(full["rwkv_g2"].astype(F32), 256)
    row = lambda name: full[name].reshape(1, -1)
    sinks = row("attn_sinks")
    rope_c, rope_s1, rope_s2 = _rope_tables(lp)

    hpad = jnp.concatenate([jnp.zeros((PAD_ROWS, D_MODEL), F32), full["meta_tokens"].astype(F32), x], axis=0)
    (u,) = _rows_fwd(_rms_fn, [_view(hpad)], [row("norm_mix_g")], [D_MODEL], nblk=nall, name="norm_mix",
                     out_dtype=BF16)
    proj = _mm(u, w_in_p, bias=b_in_p, name="in_proj")
    (q_r,) = _rows_fwd(_rope_fwd_fn, [_view(proj, 512, C_Q // 512), _view(rope_c), _view(rope_s1), _view(rope_s2)],
                       [], [512], nblk=nall, name="rope_q")
    (k_r,) = _rows_fwd(_rope_fwd_fn, [_view(proj, 128, C_KA // 128), _view(rope_c), _view(rope_s1),
                                      _view(rope_s2)], [], [128], nblk=nall, name="rope_k")
    heads = lambda a, nh: a.reshape(a.shape[0], nh, HEAD_DIM).transpose(1, 0, 2)
    unheads = lambda a: a.transpose(1, 0, 2).reshape(a.shape[1], -1)
    q_h, k_h = heads(q_r, Q_HEADS), heads(k_r, KV_HEADS)
    v_h = heads(proj[:, C_VA:C_VA + 128], KV_HEADS)
    o_h, lse = _attn_fwd(q_h, k_h, v_h, sinks, nblk=nblk, name="attn_fwd")
    y_attn = unheads(o_h).astype(BF16)

    rw_cols = jnp.concatenate([proj[:, C_R:C_R + 1536], proj[:, C_DG:C_DG + 256], proj[:, C_DW:C_DW + 256]], axis=1)
    rw_prev = jnp.pad(rw_cols[:-1], ((1, 0), (0, 0)))
    pre_rows = [_view(proj, 512, 0), _view(proj, 512, 1), _view(proj, 512, 2), _view(proj, 128, C_DW // 128),
                _view(proj, 128, C_DA // 128), _view(proj, 256, C_DG // 256),
                _view(rw_prev, 512, 0), _view(rw_prev, 512, 1), _view(rw_prev, 512, 2), _view(rw_prev, 128, 14),
                _view(rw_prev, 128, 15), _view(rw_prev, 256, 6)]
    pre_consts = [mix_r, mix_k, mix_v, mix_dw, mix_da, mix_dg, row("rwkv_w0"), w2_p, row("rwkv_a0"), a2_p, g2_p,
                  row("rwkv_k_k"), row("rwkv_k_a")]
    xt_block = (BLOCK // SCAN_T * VEC_ROWS, LANES)
    r_t, k_mod, v_t, gate, xt = _rows_fwd(_rwkv_pre_xt_fn, pre_rows, pre_consts, [RW_DIM] * 4 + [xt_block],
                                          nblk=nall, name="rwkv_pre")
    xt = xt.reshape(-1, VEC_ROWS, LANES)
    spread, collect = _selectors()
    y_scan, hist, *late = _wkv_fwd(xt, v_t, spread, name="wkv_fwd", shared=gather_late[0] if gather_late else ())
    if gather_late:
        full = {**full, **gather_late[1](late)}
    post_rows = [_view(y_scan, off=1), _view(r_t, off=1), _view(k_mod, off=1), _view(v_t, off=1), _view(gate, off=1)]
    post_consts = [row("rwkv_ln_w"), row("rwkv_ln_b"), row("rwkv_r_k")]
    (y_rwkv,) = _rows_fwd(_rwkv_post_fn, post_rows, post_consts, [RW_DIM], nblk=nblk, name="rwkv_post",
                          out_dtype=BF16)

    ya = _mm(y_attn, full["w_br_attn"], name="br_attn")
    yr = _mm(y_rwkv, full["w_br_rwkv"], name="br_rwkv")
    merge_rows = [_view(ya), _view(yr), _view(proj, 1024, C_G1 // 1024, 1), _view(proj, 1024, C_G2 // 1024, 1)]
    (merged,) = _rows_fwd(_merge_fn, merge_rows, [], [D_MODEL], nblk=nblk, name="merge", out_dtype=BF16)
    h1 = _mm(merged, full["w_o"], residual=x, name="out_proj")
    (f,) = _rows_fwd(_rms_fn, [_view(h1)], [row("norm_ffn_g")], [D_MODEL], nblk=nblk, name="norm_ffn",
                     out_dtype=BF16)
    ff_gate = _mm(f, full["w_ffn_gate"], name="ffn_gate")
    ff_up = _mm(f, full["w_ffn_up"], name="ffn_up")
    (act,) = _rows_fwd(_swiglu_fn, [_view(ff_gate), _view(ff_up)], [], [D_FF], nblk=nblk, name="swiglu",
                       out_dtype=BF16)
    h2 = _mm(act, full["w_ffn_down"], residual=h1, name="ffn_down")

    grads = {}
    ones_col = jnp.ones((seq, 1), F32)
    loss_rows, dh2, grads["norm_final_g"] = _rows_bwd(
        _loss_fn, [_view(h2), _view(tgt)], [row("norm_final_g")], [_view(ones_col)], nblk=nblk, name="loss",
        diff_rows=[0], diff_consts=[0], fwd_widths=[1])
    loss = jnp.sum(loss_rows)

    dact = _mm(dh2, full["w_ffn_down"], tb=True, name="d_act")
    grads["w_ffn_down"] = _mm(act, dh2, ta=True, name="dw_ffn_down")
    dgate, dup = _rows_bwd(_swiglu_fn, [_view(ff_gate), _view(ff_up)], [], [_view(dact)], nblk=nblk,
                           name="swiglu_bwd", diff_rows=[0, 1], diff_consts=[], row_dtype=BF16)
    grads["w_ffn_gate"] = _mm(f, dgate, ta=True, name="dw_ffn_gate")
    grads["w_ffn_up"] = _mm(f, dup, ta=True, name="dw_ffn_up")
    df = _mm(dgate, full["w_ffn_gate"], tb=True, name="df_gate")
    df = _mm(dup, full["w_ffn_up"], tb=True, residual=df, name="df_up")
    dh1, grads["norm_ffn_g"] = _rows_bwd(_rms_fn, [_view(h1)], [row("norm_ffn_g")], [_view(df)], nblk=nblk,
                                         name="norm_ffn_bwd", diff_rows=[0], diff_consts=[0], acc=[_view(dh2)])
    dmerged = _mm(dh1, full["w_o"], tb=True, name="d_merged")
    grads["w_o"] = _mm(merged, dh1, ta=True, name="dw_o")
    dya, dyr, dg1, dg2 = _rows_bwd(_merge_fn, merge_rows, [], [_view(dmerged)], nblk=nblk, name="merge_bwd",
                                   diff_rows=[0, 1, 2, 3], diff_consts=[], row_dtype=BF16)
    grads["w_br_attn"] = _mm(y_attn, dya, ta=True, name="dw_br_attn")
    grads["w_br_rwkv"] = _mm(y_rwkv, dyr, ta=True, name="dw_br_rwkv")
    dy_attn = _mm(dya, full["w_br_attn"], tb=True, name="d_y_attn")
    dy_rwkv = _mm(dyr, full["w_br_rwkv"], tb=True, name="d_y_rwkv")

    post = _rows_bwd(_rwkv_post_fn, post_rows, post_consts, [_view(dy_rwkv)], nblk=nblk, name="rwkv_post_bwd",
                     diff_rows=[0, 1, 2, 3, 4], diff_consts=[0, 1, 2])
    dys, dr_post, dk_post, dv_post, dgate_post = post[:5]
    grads["rwkv_ln_w"], grads["rwkv_ln_b"], grads["rwkv_r_k"] = post[5:]
    dxt, dv_s, *early_parts = _wkv_bwd(xt, v_t, hist, dys, spread, collect, name="wkv_bwd",
                                       scattered=scatter_early(grads) if scatter_early else ())
    pre_cts = [_view(dxt.reshape(-1, LANES), rows=xt_block[0]), _view(dv_s)] + [
        _view(t, off=-1) for t in (dr_post, dk_post, dv_post, dgate_post)]
    pre = _rows_bwd(_rwkv_pre_fn, pre_rows, pre_consts, pre_cts, nblk=nall, name="rwkv_pre_bwd",
                    diff_rows=list(range(12)), diff_consts=list(range(13)), ct_map=_rwkv_pre_cts)
    d_cur, d_prev, d_par = pre[0:6], pre[6:12], pre[12:]
    up = lambda t: jnp.pad(t[1:], ((0, 1), (0, 0)))
    d_rw = [c + up(p) for c, p in zip(d_cur, d_prev)]
    grads["rwkv_mix"] = jnp.concatenate([d_par[0], d_par[1], d_par[2], d_par[3][:, :DECAY_LORA],
                                         d_par[4][:, :AAA_LORA], d_par[5][:, :GATE_LORA]], axis=1)
    grads["rwkv_w0"], grads["rwkv_w2"] = d_par[6], d_par[7][:DECAY_LORA]
    grads["rwkv_a0"], grads["rwkv_a2"] = d_par[8], d_par[9][:AAA_LORA]
    grads["rwkv_g2"] = d_par[10][:GATE_LORA]
    grads["rwkv_k_k"], grads["rwkv_k_a"] = d_par[11], d_par[12]

    do_h = heads(dy_attn, Q_HEADS)
    dq_h, dk_h, dv_h, grads["attn_sinks"] = _attn_bwd(q_h, k_h, v_h, sinks, o_h, lse, do_h, nblk=nblk,
                                                      name="attn_bwd")
    dq_r = jnp.pad(unheads(dq_h), ((BLOCK, 0), (0, 0)))
    (dq,) = _rows_fwd(_rope_bwd_fn, [_view(dq_r), _view(rope_c), _view(rope_s1), _view(rope_s2)], [], [512],
                      nblk=nall, name="rope_q_bwd", out_dtype=BF16)
    (dka,) = _rows_fwd(_rope_bwd_fn, [_view(unheads(dk_h)), _view(rope_c), _view(rope_s1),
                                      _view(rope_s2)], [], [128], nblk=nall, name="rope_k_bwd", out_dtype=BF16)
    dva = unheads(dv_h)

    lead = lambda t: jnp.pad(t, ((BLOCK, 0), (0, 0)))
    pieces = [d_rw[0], d_rw[1], d_rw[2], dq, lead(dg1), lead(dg2), d_rw[5], dka, dva, d_rw[3], d_rw[4],
              jnp.zeros((lp, NP - C_DA - 128), BF16)]
    dproj = jnp.concatenate([p.astype(BF16) for p in pieces], axis=1)
    grads["w_in_p"] = _mm(u, dproj, ta=True, name="dw_in")
    grads["b_in"] = _w_in_unpadded(_colsum(dproj, name="db_in"))
    du = _mm(dproj, w_in_p, tb=True, name="d_u")
    dh, grads["norm_mix_g"] = _rows_bwd(_rms_fn, [_view(hpad)], [row("norm_mix_g")], [_view(du)], nblk=nall,
                                        name="norm_mix_bwd", diff_rows=[0], diff_consts=[0], acc=[_view(lead(dh1))])
    grads["meta_tokens"] = dh[PAD_ROWS:BLOCK]
    return loss, dh[BLOCK:], grads, early_parts


def kernel(x, meta_tokens, norm_mix_g, w_in, b_in, attn_sinks, rwkv_mix, rwkv_w0, rwkv_w2, rwkv_a0, rwkv_a2, rwkv_g2, rwkv_k_k, rwkv_k_a, rwkv_r_k, rwkv_ln_w, rwkv_ln_b, w_br_attn, w_br_rwkv, w_o, norm_ffn_g, w_ffn_gate, w_ffn_up, w_ffn_down, norm_final_g, loss_target, m_meta_tokens, m_norm_mix_g, m_w_in, m_b_in, m_attn_sinks, m_rwkv_mix, m_rwkv_w0, m_rwkv_w2, m_rwkv_a0, m_rwkv_a2, m_rwkv_g2, m_rwkv_k_k, m_rwkv_k_a, m_rwkv_r_k, m_rwkv_ln_w, m_rwkv_ln_b, m_w_br_attn, m_w_br_rwkv, m_w_o, m_norm_ffn_g, m_w_ffn_gate, m_w_ffn_up, m_w_ffn_down, m_norm_final_g, v_meta_tokens, v_norm_mix_g, v_w_in, v_b_in, v_attn_sinks, v_rwkv_mix, v_rwkv_w0, v_rwkv_w2, v_rwkv_a0, v_rwkv_a2, v_rwkv_g2, v_rwkv_k_k, v_rwkv_k_a, v_rwkv_r_k, v_rwkv_ln_w, v_rwkv_ln_b, v_w_br_attn, v_w_br_rwkv, v_w_o, v_norm_ffn_g, v_w_ffn_gate, v_w_ffn_up, v_w_ffn_down, v_norm_final_g):
    given = dict(locals())
    wts = {n: _strip(n, given[n]) for n in WEIGHTS}
    mom = {n: _strip(n, given["m_" + n]) for n in WEIGHTS}
    var = {n: _strip(n, given["v_" + n]) for n in WEIGHTS}
    small_shapes = [wts[n].shape for n in REPLICATED]
    width = wts["w_in"].shape[1]
    wire = lambda table: [wts[n].astype(BF16) for n, _ in table]

    w_in_all, *early_all = _all_gather([wts["w_in"].astype(BF16)] + wire(EARLY), name="gather_weights")
    full = {n: wts[n] for n in REPLICATED}
    full.update({n: _join(g, axis) for (n, axis), g in zip(EARLY, early_all)})
    full["w_in_p"] = _w_in_padded(w_in_all, shard_width=width)
    gather_late = (wire(LATE), lambda got: {n: _join(g, axis) for (n, axis), g in zip(LATE, got)})
    scatter_early = lambda g: [_split(g[n], axis).astype(BF16) for n, axis in LATE]

    loss_part, grad_x, grads, parts_late = _device_step(x[0], loss_target[0], full, gather_late, scatter_early)

    g_w_in = jnp.stack([_w_in_unpadded(grads["w_in_p"], p * width, (p + 1) * width) for p in range(N_DEV)])
    g_early = [_split(grads[n], axis).astype(BF16) for n, axis in EARLY]
    zero = jnp.zeros((1,), F32)
    g_small = _pack([grads[n].reshape(wts[n].shape) for n in REPLICATED] + [loss_part.reshape(1)], 8)
    parts_w_in, *parts_early, parts_small = _exchange([g_w_in.astype(BF16)] + g_early, [g_small],
                                                      name="exchange_grads")
    results = [{}, {}, {}, {}]
    for (n, _), parts in zip([("w_in", 1)] + EARLY + LATE, [parts_w_in] + parts_early + list(parts_late)):
        for kind, a in enumerate(_adamw(parts, wts[n], mom[n], var[n], name="adamw_" + n)):
            results[kind][n] = a
    small = _adamw(parts_small, _pack([wts[n] for n in REPLICATED] + [zero], 8),
                   _pack([mom[n] for n in REPLICATED] + [zero], 8), _pack([var[n] for n in REPLICATED] + [zero], 8),
                   name="adamw_replicated")
    for kind in range(4):
        for n, a in zip(REPLICATED, _unpack(small[kind], small_shapes)):
            results[kind][n] = a
    loss = _unpack(small[0], small_shapes + [(1,)])[-1][0]
    out = [loss, grad_x[None]]
    for kind in range(4):
        out += [results[kind][n].reshape(given[n].shape) for n in WEIGHTS]
    return tuple(out)
```

```python
import functools

import jax
import jax.numpy as jnp
from jax import lax
from jax.experimental import pallas as pl
from jax.experimental.pallas import tpu as pltpu

F32 = jnp.float32
BF16 = jnp.bfloat16

N_DEV = 8
D_MODEL = 1024
N_META = 16
BLOCK = 128
PAD_ROWS = BLOCK - N_META
HEAD_DIM = 64
Q_HEADS = 8
KV_HEADS = 2
GROUP = Q_HEADS // KV_HEADS
ROPE_DIM = HEAD_DIM // 4
ROPE_HALF = ROPE_DIM // 2
ROPE_THETA = 500000.0
RW_HEADS = 8
RW_DIM = 512
DECAY_LORA = 64
AAA_LORA = 64
GATE_LORA = 160
D_FF = 2816
D_IN = 4640
RMS_EPS = 1e-6
RWKV_LN_EPS = 64e-5
NEG_INF = -1e30
SCAN_T = 16
LANES = 128
PACK_ROWS = 256

ADAM_LR = 0.001
ADAM_B1 = 0.9
ADAM_B2 = 0.999
ADAM_EPS = 1e-08
ADAM_WD = 0.01
ADAM_STEP = 10

C_R, C_K, C_V, C_Q = 0, 512, 1024, 1536
C_G1, C_G2 = 2048, 3072
C_DG, C_KA, C_VA, C_DW, C_DA = 4096, 4352, 4480, 4608, 4736
NP = 5120

VMEM_LIMIT = 48 * 1024 * 1024


def _cparams(sem):
    return pltpu.CompilerParams(dimension_semantics=sem, vmem_limit_bytes=VMEM_LIMIT)


def _pick(n, cands):
    for c in cands:
        if n % c == 0:
            return c
    raise ValueError(f"no tile for {n}")


def _mm(a, b, *, ta=False, tb=False, bias=None, residual=None, name, scattered=()):
    m = a.shape[1] if ta else a.shape[0]
    k = a.shape[0] if ta else a.shape[1]
    n = b.shape[0] if tb else b.shape[1]
    assert k == (b.shape[1] if tb else b.shape[0]), (a.shape, b.shape, ta, tb)
    tm = _pick(m, (512, 1408, 256, 128) if ta else (1024, 528, 512, 384, 256, 128))
    tn = _pick(n, (1024, 512, 1408, 256, 128))
    if k <= 1024:
        tk = k
    else:
        tk = _pick(k, (1024, 1056, 528, 512) if (ta and not tb) else (1024, 1408, 512, 256, 128))
    nk = k // tk
    has_bias = bias is not None
    has_res = residual is not None
    dn = (((0 if ta else 1,), (1 if tb else 0,)), ((), ()))

    def body(*refs):
        a_ref, b_ref = refs[0], refs[1]
        pos = 2
        bias_ref = res_ref = None
        if has_bias:
            bias_ref = refs[pos]
            pos += 1
        if has_res:
            res_ref = refs[pos]
            pos += 1
        o_ref, acc_ref = refs[pos], refs[pos + 1]
        kk = pl.program_id(2)
        part = lax.dot_general(a_ref[...].astype(BF16), b_ref[...].astype(BF16), dn, preferred_element_type=F32)

        def finish(out):
            if has_bias:
                out = out + bias_ref[...]
            if has_res:
                out = out + res_ref[...]
            o_ref[...] = out

        if nk == 1:
            finish(part)
        else:
            @pl.when(kk == 0)
            def _():
                acc_ref[...] = part

            @pl.when((kk > 0) & (kk < nk - 1))
            def _():
                acc_ref[...] += part

            @pl.when(kk == nk - 1)
            def _():
                finish(acc_ref[...] + part)

    in_specs = [
        pl.BlockSpec((tk, tm), lambda i, j, kk: (kk, i)) if ta else pl.BlockSpec((tm, tk), lambda i, j, kk: (i, kk)),
        pl.BlockSpec((tn, tk), lambda i, j, kk: (j, kk)) if tb else pl.BlockSpec((tk, tn), lambda i, j, kk: (kk, j)),
    ]
    args = [a, b]
    if has_bias:
        in_specs.append(pl.BlockSpec((1, tn), lambda i, j, kk: (0, j)))
        args.append(bias)
    if has_res:
        in_specs.append(pl.BlockSpec((tm, tn), lambda i, j, kk: (i, j)))
        args.append(residual)
    grid = (m // tm, n // tn, nk)
    n_x = len(scattered)
    out = pl.pallas_call(
        _with_exchange(body, len(args), 1, 1, scattered, (), grid), name=name, grid=grid,
        in_specs=in_specs + [ANY] * n_x,
        out_specs=[pl.BlockSpec((tm, tn), lambda i, j, kk: (i, j))] + [ANY] * n_x,
        out_shape=[jax.ShapeDtypeStruct((m, n), F32)] + _exchange_shapes(scattered, ()),
        scratch_shapes=[pltpu.VMEM((tm, tn) if nk > 1 else (8, LANES), F32)] + (_exchange_sems(n_x) if n_x else []),
        compiler_params=_cparams(("arbitrary",) * 3 if n_x else ("parallel", "parallel", "arbitrary")),
    )(*args, *scattered)
    return out if n_x else out[0]


def _colsum(x, name):
    m, n = x.shape
    tm = BLOCK

    def body(x_ref, o_ref):
        i = pl.program_id(0)
        s = jnp.sum(x_ref[...].astype(F32), axis=0, keepdims=True)

        @pl.when(i == 0)
        def _():
            o_ref[...] = s

        @pl.when(i > 0)
        def _():
            o_ref[...] += s

    return pl.pallas_call(
        body, name=name, grid=(m // tm,),
        in_specs=[pl.BlockSpec((tm, n), lambda i: (i, 0))],
        out_specs=pl.BlockSpec((1, n), lambda i: (0, 0)),
        out_shape=jax.ShapeDtypeStruct((1, n), F32),
        compiler_params=_cparams(("arbitrary",)),
    )(x)


def _view(arr, width=None, col=0, off=0, rows=BLOCK):
    return (arr, arr.shape[1] if width is None else width, col, off, rows)


def _row_spec(view):
    _, width, col, off, rows = view
    if off < 0:
        return pl.BlockSpec((rows, width), lambda i, col=col, off=off: (jnp.maximum(i + off, 0), col))
    return pl.BlockSpec((rows, width), lambda i, col=col, off=off: (i + off, col))


def _const_spec(arr):
    return pl.BlockSpec(arr.shape, lambda i: (0,) * arr.ndim)


def _rows_fwd(fn, rows, consts, out_widths, *, nblk, name, out_dtype=F32):
    nr, nc = len(rows), len(consts)
    out_blocks = [(BLOCK, w) if isinstance(w, int) else w for w in out_widths]

    def body(*refs):
        i = pl.program_id(0)
        vals = [r[...] for r in refs[:nr + nc]]
        outs = fn(i, *vals)
        for o_ref, o in zip(refs[nr + nc:], outs):
            o_ref[...] = o.astype(o_ref.dtype)

    return pl.pallas_call(
        body, name=name, grid=(nblk,),
        in_specs=[_row_spec(v) for v in rows] + [_const_spec(c) for c in consts],
        out_specs=[pl.BlockSpec(b, lambda i: (i, 0)) for b in out_blocks],
        out_shape=[jax.ShapeDtypeStruct((nblk * r, w), out_dtype) for r, w in out_blocks],
        compiler_params=_cparams(("parallel",)),
    )(*[v[0] for v in rows], *consts)


def _rows_bwd(fn, rows, consts, cts, *, nblk, name, diff_rows, diff_consts, acc=None, fwd_widths=(), row_dtype=F32,
              ct_map=None):
    nr, nc = len(rows), len(consts)
    ct_views = [c for c in cts if c is not None]
    acc = acc or [None] * len(diff_rows)
    acc_views = [a for a in acc if a is not None]
    n_in = nr + nc + len(ct_views) + len(acc_views)
    n_fwd = len(fwd_widths)

    def body(*refs):
        i = pl.program_id(0)
        row_vals = [r[...] for r in refs[:nr]]
        const_vals = [r[...] for r in refs[nr:nr + nc]]
        ct_vals = [r[...] for r in refs[nr + nc:nr + nc + len(ct_views)]]
        acc_vals = [r[...] for r in refs[nr + nc + len(ct_views):n_in]]
        out_refs = refs[n_in:]

        def f(*dargs):
            rv = list(row_vals)
            cv = list(const_vals)
            for pos, idx in enumerate(diff_rows):
                rv[idx] = dargs[pos]
            for pos, idx in enumerate(diff_consts):
                cv[idx] = dargs[len(diff_rows) + pos]
            return tuple(fn(i, *rv, *cv))

        primals = [row_vals[idx] for idx in diff_rows] + [const_vals[idx] for idx in diff_consts]
        outs, pull = jax.vjp(f, *primals)
        full_ct, ci = [], 0
        if ct_map is not None:
            full_ct = ct_map(i, *ct_vals)
        else:
            for o, c in zip(outs, cts):
                if c is None:
                    full_ct.append(jnp.zeros_like(o))
                else:
                    full_ct.append(ct_vals[ci])
                    ci += 1
        grads = pull(tuple(full_ct))
        for o_ref, o in zip(out_refs[:n_fwd], outs):
            o_ref[...] = o
        ai = 0
        for pos in range(len(diff_rows)):
            g = grads[pos]
            if acc[pos] is not None:
                g = g + acc_vals[ai]
                ai += 1
            out_refs[n_fwd + pos][...] = g.astype(row_dtype)
        for pos in range(len(diff_consts)):
            g = grads[len(diff_rows) + pos]
            o_ref = out_refs[n_fwd + len(diff_rows) + pos]

            @pl.when(i == 0)
            def _(o_ref=o_ref, g=g):
                o_ref[...] = g

            @pl.when(i > 0)
            def _(o_ref=o_ref, g=g):
                o_ref[...] += g

    out_specs = [pl.BlockSpec((BLOCK, w), lambda i: (i, 0)) for w in fwd_widths]
    out_shape = [jax.ShapeDtypeStruct((nblk * BLOCK, w), F32) for w in fwd_widths]
    for idx in diff_rows:
        out_specs.append(pl.BlockSpec((BLOCK, rows[idx][1]), lambda i: (i, 0)))
        out_shape.append(jax.ShapeDtypeStruct((nblk * BLOCK, rows[idx][1]), row_dtype))
    for idx in diff_consts:
        out_specs.append(_const_spec(consts[idx]))
        out_shape.append(jax.ShapeDtypeStruct(consts[idx].shape, F32))
    return pl.pallas_call(
        body, name=name, grid=(nblk,),
        in_specs=([_row_spec(v) for v in rows] + [_const_spec(c) for c in consts]
                  + [_row_spec(v) for v in ct_views] + [_row_spec(v) for v in acc_views]),
        out_specs=out_specs, out_shape=out_shape,
        compiler_params=_cparams(("arbitrary",)),
    )(*[v[0] for v in rows], *consts, *[v[0] for v in ct_views], *[v[0] for v in acc_views])


def _rms_fn(i, x, g):
    return (x * lax.rsqrt(jnp.mean(x * x, axis=-1, keepdims=True) + RMS_EPS) * g,)


def _sigmoid(x):
    return 1.0 / (1.0 + jnp.exp(-x))


def _softplus(x):
    return jnp.maximum(x, 0.0) + jnp.log(1.0 + jnp.exp(-jnp.abs(x)))


def _split2(x):
    hi = x.astype(BF16)
    lo = (x - hi.astype(F32)).astype(BF16)
    return jnp.concatenate([hi, lo], axis=1)


@jax.custom_vjp
def _head_sum(x):
    r = lax.broadcasted_iota(jnp.int32, (2 * RW_DIM, RW_DIM), 0) % RW_DIM // HEAD_DIM
    c = lax.broadcasted_iota(jnp.int32, (2 * RW_DIM, RW_DIM), 1) // HEAD_DIM
    return jnp.dot(_split2(x), (r == c).astype(BF16), preferred_element_type=F32)


_head_sum.defvjp(lambda x: (_head_sum(x), None), lambda _, ct: (_head_sum(ct),))


@jax.custom_vjp
def _dot_bf16(x, w):
    return jnp.dot(x.astype(BF16), w.astype(BF16), preferred_element_type=F32)


def _dot_bf16_bwd(res, ct):
    x, w = res
    ct = ct.astype(BF16)
    dx = lax.dot_general(ct, w.astype(BF16), (((1,), (1,)), ((), ())), preferred_element_type=F32)
    dw = lax.dot_general(x.astype(BF16), ct, (((0,), (0,)), ((), ())), preferred_element_type=F32)
    return dx, dw


_dot_bf16.defvjp(lambda x, w: (_dot_bf16(x, w), (x, w)), _dot_bf16_bwd)


def _rwkv_pre_fn(i, r, k, v, dw, da, dg, r_p, k_p, v_p, dw_p, da_p, dg_p,
                 mix_r, mix_k, mix_v, mix_dw, mix_da, mix_dg, w0, w2, a0, a2, g2, k_k, k_a):
    row = i * BLOCK + lax.broadcasted_iota(jnp.int32, (BLOCK, 1), 0)
    live = row >= PAD_ROWS
    live_prev = row >= PAD_ROWS + 1

    def shift(cur, prev, mix):
        cur = jnp.where(live, cur, 0.0)
        prev = jnp.where(live_prev, prev, 0.0)
        return cur + (prev - cur) * mix

    r = shift(r, r_p, mix_r)
    k = shift(k, k_p, mix_k)
    v = shift(v, v_p, mix_v)
    dw = shift(dw, dw_p, mix_dw)
    da = shift(da, da_p, mix_da)
    dg = shift(dg, dg_p, mix_dg)
    wlog = -_softplus(-(w0 + _dot_bf16(jnp.tanh(dw), w2))) - 0.5
    decay = jnp.exp(-jnp.exp(wlog))
    a = _sigmoid(a0 + _dot_bf16(da, a2))
    g = _dot_bf16(_sigmoid(dg), g2)
    kk = k * k_k
    norm_sq = jnp.where(live, _head_sum(kk * kk), 1.0)
    kk = kk / jnp.maximum(jnp.sqrt(norm_sq), 1e-12)
    k_mod = k * (1.0 + (a - 1.0) * k_a)
    return r, decay, k_mod, v, -kk, kk * a, g


def _rwkv_pre_xt_fn(i, *args):
    r, decay, k_mod, v, a_neg, b, g = _rwkv_pre_fn(i, *args)
    t = SCAN_T
    xt = jnp.concatenate([_rows_to_xt(x[c * t:(c + 1) * t]) for c in range(BLOCK // t)
                          for x in (a_neg, decay, b, k_mod, r)], axis=0)
    return r, k_mod, v, g, xt


def _rwkv_pre_cts(i, dxt, dv_s, dr_p, dk_p, dv_p, dg_p):
    t = SCAN_T
    d_a, d_w, d_b, d_k, d_r = [
        jnp.concatenate([_xt_to_rows(dxt[c * VEC_ROWS + n * HEAD_DIM:c * VEC_ROWS + (n + 1) * HEAD_DIM])
                         for c in range(BLOCK // t)], axis=0) for n in range(N_VEC)]
    dr_p, dk_p, dv_p, dg_p = [jnp.where(i > 0, x, 0.0) for x in (dr_p, dk_p, dv_p, dg_p)]
    return d_r + dr_p, d_w, d_k + dk_p, dv_s + dv_p, d_a, d_b, dg_p


def _rwkv_post_fn(i, ys, r, k_mod, v, g, ln_w, ln_b, r_k):
    mean = _head_sum(ys) * (1.0 / HEAD_DIM)
    d = ys - mean
    var = _head_sum(d * d) * (1.0 / HEAD_DIM)
    yn = d * lax.rsqrt(var + RWKV_LN_EPS) * ln_w + ln_b
    bonus = _head_sum(r * k_mod * r_k) * v
    return ((yn + bonus) * g,)


def _merge_fn(i, ya, yr, g1, g2):
    return (_sigmoid(g1) * ya + _sigmoid(g2) * yr,)


def _swiglu_fn(i, gate, up):
    return (gate * _sigmoid(gate) * up,)


def _loss_fn(i, h, tgt, g):
    y = h * lax.rsqrt(jnp.mean(h * h, axis=-1, keepdims=True) + RMS_EPS) * g
    err = y - tgt
    return (0.5 * jnp.mean(err * err, axis=-1, keepdims=True),)


def _rope_tables(lp):
    pos = (jnp.arange(lp, dtype=jnp.int32) - PAD_ROWS).astype(F32)
    inv_freq = jnp.power(jnp.float32(ROPE_THETA), -jnp.arange(ROPE_HALF, dtype=F32) * (2.0 / ROPE_DIM))
    ang = pos[:, None] * inv_freq[None, :]
    cos, sin = jnp.cos(ang), jnp.sin(ang)
    one = jnp.ones((lp, HEAD_DIM - ROPE_DIM), F32)
    zero_h = jnp.zeros((lp, ROPE_HALF), F32)
    zero_r = jnp.zeros((lp, HEAD_DIM - ROPE_DIM), F32)
    c = jnp.concatenate([cos, cos, one], axis=1)
    s1 = jnp.concatenate([-sin, zero_h, zero_r], axis=1)
    s2 = jnp.concatenate([zero_h, sin, zero_r], axis=1)
    return tuple(jnp.tile(t, (1, LANES // HEAD_DIM)) for t in (c, s1, s2))


def _rope_fwd_fn(i, x, c, s1, s2):
    n = x.shape[1]
    c, s1, s2 = [jnp.tile(t, (1, n // LANES)) for t in (c, s1, s2)]
    return (x * c + pltpu.roll(x, n - ROPE_HALF, 1) * s1 + pltpu.roll(x, ROPE_HALF, 1) * s2,)


def _rope_bwd_fn(i, dy, c, s1, s2):
    n = dy.shape[1]
    c, s1, s2 = [jnp.tile(t, (1, n // LANES)) for t in (c, s1, s2)]
    return (dy * c + pltpu.roll(dy * s1, ROPE_HALF, 1) + pltpu.roll(dy * s2, n - ROPE_HALF, 1),)


def _attn_mask(i):
    r = lax.broadcasted_iota(jnp.int32, (BLOCK, 3 * BLOCK), 0)
    c = lax.broadcasted_iota(jnp.int32, (BLOCK, 3 * BLOCK), 1)
    meta = (c < BLOCK) & (c >= PAD_ROWS)
    prev = (c >= BLOCK) & (c < 2 * BLOCK) & ((c - BLOCK) > r) & (i >= 1)
    cur = (c >= 2 * BLOCK) & ((c - 2 * BLOCK) <= r)
    return meta | prev | cur


def _attn_fwd(q, k, v, sinks, *, nblk, name):
    scale = HEAD_DIM ** -0.5

    def body(q_ref, km_ref, kp_ref, kc_ref, vm_ref, vp_ref, vc_ref, s_ref, o_ref, lse_ref):
        i = pl.program_id(0)
        valid = _attn_mask(i)
        for h in range(Q_HEADS):
            g = h // GROUP
            kcat = jnp.concatenate([km_ref[g], kp_ref[g], kc_ref[g]], axis=0).astype(BF16)
            vcat = jnp.concatenate([vm_ref[g], vp_ref[g], vc_ref[g]], axis=0).astype(BF16)
            s = lax.dot_general(q_ref[h].astype(BF16), kcat, (((1,), (1,)), ((), ())),
                                preferred_element_type=F32) * scale
            s = jnp.where(valid, s, NEG_INF)
            sink = s_ref[0:1, h:h + 1]
            m = jnp.maximum(jnp.max(s, axis=-1, keepdims=True), sink)
            p = jnp.exp(s - m)
            den = jnp.sum(p, axis=-1, keepdims=True) + jnp.exp(sink - m)
            o = jnp.dot(p.astype(BF16), vcat, preferred_element_type=F32)
            o_ref[h] = o / den
            lse_ref[h] = m + jnp.log(den)

    kv = lambda f: pl.BlockSpec((KV_HEADS, BLOCK, HEAD_DIM), f)
    return pl.pallas_call(
        body, name=name, grid=(nblk,),
        in_specs=[pl.BlockSpec((Q_HEADS, BLOCK, HEAD_DIM), lambda i: (0, i + 1, 0)),
                  kv(lambda i: (0, 0, 0)), kv(lambda i: (0, i, 0)), kv(lambda i: (0, i + 1, 0)),
                  kv(lambda i: (0, 0, 0)), kv(lambda i: (0, i, 0)), kv(lambda i: (0, i + 1, 0)),
                  pl.BlockSpec((1, Q_HEADS), lambda i: (0, 0))],
        out_specs=[pl.BlockSpec((Q_HEADS, BLOCK, HEAD_DIM), lambda i: (0, i, 0)),
                   pl.BlockSpec((Q_HEADS, BLOCK, 1), lambda i: (0, i, 0))],
        out_shape=[jax.ShapeDtypeStruct((Q_HEADS, nblk * BLOCK, HEAD_DIM), F32),
                   jax.ShapeDtypeStruct((Q_HEADS, nblk * BLOCK, 1), F32)],
        compiler_params=_cparams(("parallel",)),
    )(q, k, k, k, v, v, v, sinks)


def _attn_bwd(q, k, v, sinks, o, lse, do, *, nblk, name):
    scale = HEAD_DIM ** -0.5
    lp = k.shape[1]

    def body(q_ref, km_ref, kp_ref, kc_ref, vm_ref, vp_ref, vc_ref, s_ref, o_ref, lse_ref, do_ref,
             dq_ref, dk_ref, dv_ref, ds_ref):
        i = pl.program_id(0)

        @pl.when(i == 0)
        def _():
            dk_ref[...] = jnp.zeros_like(dk_ref)
            dv_ref[...] = jnp.zeros_like(dv_ref)
            ds_ref[...] = jnp.zeros_like(ds_ref)

        valid = _attn_mask(i)
        lane = lax.broadcasted_iota(jnp.int32, (1, Q_HEADS), 1)
        prev_rows = pl.ds(pl.multiple_of(i * BLOCK, BLOCK), BLOCK)
        cur_rows = pl.ds(pl.multiple_of((i + 1) * BLOCK, BLOCK), BLOCK)
        for h in range(Q_HEADS):
            g = h // GROUP
            qh = q_ref[h].astype(BF16)
            doh = do_ref[h]
            kcat = jnp.concatenate([km_ref[g], kp_ref[g], kc_ref[g]], axis=0).astype(BF16)
            vcat = jnp.concatenate([vm_ref[g], vp_ref[g], vc_ref[g]], axis=0).astype(BF16)
            s = lax.dot_general(qh, kcat, (((1,), (1,)), ((), ())), preferred_element_type=F32) * scale
            s = jnp.where(valid, s, NEG_INF)
            lse_h = lse_ref[h]
            p = jnp.exp(s - lse_h)
            delta = jnp.sum(doh * o_ref[h], axis=-1, keepdims=True)
            dp = lax.dot_general(doh.astype(BF16), vcat, (((1,), (1,)), ((), ())), preferred_element_type=F32)
            dsc = (p * (dp - delta) * scale).astype(BF16)
            dq_ref[h] = jnp.dot(dsc, kcat, preferred_element_type=F32)
            dk_all = lax.dot_general(dsc, qh, (((0,), (0,)), ((), ())), preferred_element_type=F32)
            dv_all = lax.dot_general(p.astype(BF16), doh.astype(BF16), (((0,), (0,)), ((), ())),
                                     preferred_element_type=F32)
            dk_ref[g, 0:BLOCK, :] += dk_all[0:BLOCK]
            dk_ref[g, prev_rows, :] += dk_all[BLOCK:2 * BLOCK]
            dk_ref[g, cur_rows, :] += dk_all[2 * BLOCK:]
            dv_ref[g, 0:BLOCK, :] += dv_all[0:BLOCK]
            dv_ref[g, prev_rows, :] += dv_all[BLOCK:2 * BLOCK]
            dv_ref[g, cur_rows, :] += dv_all[2 * BLOCK:]
            p_sink = jnp.exp(s_ref[0:1, h:h + 1] - lse_h)
            dsink = -jnp.sum(p_sink * delta, axis=0, keepdims=True)
            ds_ref[...] += jnp.where(lane == h, dsink, 0.0)

    kv = lambda f: pl.BlockSpec((KV_HEADS, BLOCK, HEAD_DIM), f)
    qspec = pl.BlockSpec((Q_HEADS, BLOCK, HEAD_DIM), lambda i: (0, i, 0))
    whole = pl.BlockSpec((KV_HEADS, lp, HEAD_DIM), lambda i: (0, 0, 0))
    return pl.pallas_call(
        body, name=name, grid=(nblk,),
        in_specs=[pl.BlockSpec((Q_HEADS, BLOCK, HEAD_DIM), lambda i: (0, i + 1, 0)),
                  kv(lambda i: (0, 0, 0)), kv(lambda i: (0, i, 0)), kv(lambda i: (0, i + 1, 0)),
                  kv(lambda i: (0, 0, 0)), kv(lambda i: (0, i, 0)), kv(lambda i: (0, i + 1, 0)),
                  pl.BlockSpec((1, Q_HEADS), lambda i: (0, 0)),
                  qspec, pl.BlockSpec((Q_HEADS, BLOCK, 1), lambda i: (0, i, 0)), qspec],
        out_specs=[qspec, whole, whole, pl.BlockSpec((1, Q_HEADS), lambda i: (0, 0))],
        out_shape=[jax.ShapeDtypeStruct((Q_HEADS, nblk * BLOCK, HEAD_DIM), F32),
                   jax.ShapeDtypeStruct((KV_HEADS, lp, HEAD_DIM), F32),
                   jax.ShapeDtypeStruct((KV_HEADS, lp, HEAD_DIM), F32),
                   jax.ShapeDtypeStruct((1, Q_HEADS), F32)],
        compiler_params=_cparams(("arbitrary",)),
    )(q, k, k, k, v, v, v, sinks, o, lse, do)


N_VEC = 5
VEC_ROWS = N_VEC * HEAD_DIM


def _selectors():
    t = SCAN_T
    shape = (t, 2 * LANES, RW_DIM)
    step, src, dst = [lax.broadcasted_iota(jnp.int32, shape, d) for d in range(3)]
    src = src % LANES
    spread = ((src // t == dst // HEAD_DIM) & (src % t == step)).astype(BF16)
    shape = (t, RW_DIM, LANES)
    step, src, dst = [lax.broadcasted_iota(jnp.int32, shape, d) for d in range(3)]
    collect = ((src // HEAD_DIM == dst // t) & (dst % t == step)).astype(BF16)
    return spread, collect


def _rows_to_xt(x):
    low = lax.broadcasted_iota(jnp.int32, (SCAN_T, LANES), 1) < HEAD_DIM
    pieces = []
    for m in range(RW_HEADS // 2):
        pair = x[:, m * LANES:(m + 1) * LANES]
        pieces += [jnp.where(low, pair, 0.0), jnp.where(low, pltpu.roll(pair, HEAD_DIM, 1), 0.0)]
    return jnp.concatenate(pieces, axis=0).T[:HEAD_DIM]


def _xt_to_rows(a):
    t = SCAN_T
    a_t = jnp.concatenate([a, jnp.zeros_like(a)], axis=0).T
    pairs = [a_t[2 * m * t:(2 * m + 1) * t] + pltpu.roll(a_t[(2 * m + 1) * t:(2 * m + 2) * t], HEAD_DIM, 1)
             for m in range(RW_HEADS // 2)]
    return jnp.concatenate(pairs, axis=1)


def _with_exchange(compute, n_in, n_out, n_scratch, scattered, shared, grid):
    n_sc = len(scattered)
    n_x = n_sc + len(shared)
    if n_x == 0:
        return compute

    def body(*refs):
        ins, x_in = refs[:n_in], refs[n_in:n_in + n_x]
        outs, x_out = refs[n_in + n_x:n_in + n_x + n_out], refs[n_in + n_x + n_out:n_in + 2 * n_x + n_out]
        scratch = refs[n_in + 2 * n_x + n_out:n_in + 2 * n_x + n_out + n_scratch]
        sems = refs[n_in + 2 * n_x + n_out + n_scratch:]

        first = last = True
        for d, size in enumerate(grid):
            first = first & (pl.program_id(d) == 0)
            last = last & (pl.program_id(d) == size - 1)

        @pl.when(first)
        def _():
            for cp in _exchange_copies(x_in, x_out, n_sc, *sems):
                cp.start()

        compute(*ins, *outs, *scratch)

        @pl.when(last)
        def _():
            for cp in _exchange_copies(x_in, x_out, n_sc, *sems):
                cp.wait()

    return body


def _wkv_fwd(xt, v, spread, name, shared=()):
    t_steps = SCAN_T
    nch = xt.shape[0]
    n_x = len(shared)

    def compute(xt_ref, v_ref, sel_ref, y_ref, hist_ref, st_ref):
        @pl.when(pl.program_id(0) == 0)
        def _():
            st_ref[...] = jnp.zeros_like(st_ref)

        x2 = _split2(xt_ref[0])
        st = st_ref[...]
        for j in range(t_steps):
            cols = jnp.dot(x2, sel_ref[j], preferred_element_type=F32)
            a_c, w_c, b_c, k_c, r_c = [cols[n * HEAD_DIM:(n + 1) * HEAD_DIM] for n in range(N_VEC)]
            hist_ref[j] = st
            sa = jnp.sum(st * a_c, axis=0, keepdims=True)
            st = st * w_c + b_c * sa + k_c * v_ref[j:j + 1, :]
            y_ref[j:j + 1, :] = jnp.sum(st * r_c, axis=0, keepdims=True)
        st_ref[...] = st

    return pl.pallas_call(
        _with_exchange(compute, 3, 2, 1, (), shared, (nch,)), name=name, grid=(nch,),
        in_specs=[pl.BlockSpec((1, VEC_ROWS, LANES), lambda c: (c, 0, 0)),
                  pl.BlockSpec((t_steps, RW_DIM), lambda c: (c, 0)),
                  pl.BlockSpec(spread.shape, lambda c: (0, 0, 0))] + [ANY] * n_x,
        out_specs=[pl.BlockSpec((t_steps, RW_DIM), lambda c: (c, 0)),
                   pl.BlockSpec((t_steps, HEAD_DIM, RW_DIM), lambda c: (c, 0, 0))] + [ANY] * n_x,
        out_shape=[jax.ShapeDtypeStruct((nch * t_steps, RW_DIM), F32),
                   jax.ShapeDtypeStruct((nch * t_steps, HEAD_DIM, RW_DIM), F32)] + _exchange_shapes((), shared),
        scratch_shapes=[pltpu.VMEM((HEAD_DIM, RW_DIM), F32)] + (_exchange_sems(n_x) if n_x else []),
        compiler_params=_cparams(("arbitrary",)),
    )(xt, v, spread, *shared)


def _wkv_bwd(xt, v, hist, dy, spread, collect, name, scattered=()):
    t_steps = SCAN_T
    nch = xt.shape[0]
    n_x = len(scattered)
    lead = BLOCK // t_steps

    def compute(xt_ref, v_ref, hist_ref, dy_ref, sel_ref, col_ref, dxt_ref, dv_ref, g_ref):
        @pl.when(pl.program_id(0) == 0)
        def _():
            g_ref[...] = jnp.zeros_like(g_ref)

        x2 = _split2(xt_ref[0])
        has_dy = nch - 1 - pl.program_id(0) >= lead
        gst = g_ref[...]
        acc = jnp.zeros((VEC_ROWS, LANES), F32)
        nxt = None
        for j in reversed(range(t_steps)):
            cols = jnp.dot(x2, sel_ref[j], preferred_element_type=F32)
            a_c, w_c, b_c, k_c, r_c = [cols[n * HEAD_DIM:(n + 1) * HEAD_DIM] for n in range(N_VEC)]
            prev = hist_ref[j]
            v_row = v_ref[j:j + 1, :]
            dy_row = jnp.where(has_dy, dy_ref[j:j + 1, :], 0.0)
            sa = jnp.sum(prev * a_c, axis=0, keepdims=True)
            if nxt is None:
                nxt = prev * w_c + b_c * sa + k_c * v_row
            gst = gst + r_c * dy_row
            dv_ref[j:j + 1, :] = jnp.sum(gst * k_c, axis=0, keepdims=True)
            dsa = jnp.sum(gst * b_c, axis=0, keepdims=True)
            prods = jnp.concatenate([p.astype(BF16) for p in
                                     (prev * dsa, gst * prev, gst * sa, gst * v_row, nxt * dy_row)], axis=0)
            acc = acc + jnp.dot(prods, col_ref[j], preferred_element_type=F32)
            gst = gst * w_c + a_c * dsa
            nxt = prev
        dxt_ref[0] = acc
        g_ref[...] = gst

    rev3 = lambda c: (nch - 1 - c, 0, 0)
    rev2 = lambda c: (nch - 1 - c, 0)
    rowspec = pl.BlockSpec((t_steps, RW_DIM), rev2)
    return pl.pallas_call(
        _with_exchange(compute, 6, 2, 1, scattered, (), (nch,)), name=name, grid=(nch,),
        in_specs=[pl.BlockSpec((1, VEC_ROWS, LANES), rev3), rowspec,
                  pl.BlockSpec((t_steps, HEAD_DIM, RW_DIM), rev3),
                  pl.BlockSpec((t_steps, RW_DIM), lambda c: (jnp.maximum(nch - 1 - c - lead, 0), 0)),
                  pl.BlockSpec(spread.shape, lambda c: (0, 0, 0)),
                  pl.BlockSpec(collect.shape, lambda c: (0, 0, 0))] + [ANY] * n_x,
        out_specs=[pl.BlockSpec((1, VEC_ROWS, LANES), rev3), rowspec] + [ANY] * n_x,
        out_shape=[jax.ShapeDtypeStruct((nch, VEC_ROWS, LANES), F32),
                   jax.ShapeDtypeStruct((nch * t_steps, RW_DIM), F32)] + _exchange_shapes(scattered, ()),
        scratch_shapes=[pltpu.VMEM((HEAD_DIM, RW_DIM), F32)] + (_exchange_sems(n_x) if n_x else []),
        compiler_params=_cparams(("arbitrary",)),
    )(xt, v, hist, dy, spread, collect, *scattered)


MESH = pl.DeviceIdType.MESH
ANY = pl.BlockSpec(memory_space=pl.ANY)


def _all_gather(arrays, name):
    n_arr = len(arrays)
    per = N_DEV - 1

    def body(*refs):
        x_refs, out_refs = refs[:n_arr], refs[n_arr:2 * n_arr]
        send_sems, recv_sems, local_sems = refs[2 * n_arr:]
        xi, yi, ci = lax.axis_index("x"), lax.axis_index("y"), lax.axis_index("c")
        me, sibling = (xi, yi, ci), (xi, yi, 1 - ci)
        chips = [(1 - xi, yi), (xi, 1 - yi), (1 - xi, 1 - yi)]

        def slot(a, px, py, pc):
            return out_refs[a].at[4 * px + 2 * py + pc]

        def copy(a, sem, block, to, src=None):
            return pltpu.make_async_remote_copy(
                src_ref=slot(a, *block) if src is None else src, dst_ref=slot(a, *block),
                send_sem=send_sems.at[per * a + sem], recv_sem=recv_sems.at[per * a + sem],
                device_id=to, device_id_type=MESH)

        mine = [pltpu.make_async_copy(x_refs[a], slot(a, *me), local_sems.at[a]) for a in range(n_arr)]
        for cp in mine:
            cp.start()
        sent = []
        for a in range(n_arr):
            sent.append(copy(a, 0, me, sibling, src=x_refs[a]))
            sent += [copy(a, 1 + j, me, (*chip, ci), src=x_refs[a]) for j, chip in enumerate(chips)]
        for cp in sent:
            cp.start()
        for j, chip in enumerate(chips):
            for a in range(n_arr):
                copy(a, 1 + j, (*chip, ci), me).wait_recv()
                onward = copy(a, 4 + j, (*chip, ci), sibling)
                onward.start()
                sent.append(onward)
        for a in range(n_arr):
            copy(a, 0, sibling, me).wait_recv()
        for j, chip in enumerate(chips):
            for a in range(n_arr):
                copy(a, 4 + j, (*chip, 1 - ci), me).wait_recv()
        for cp in sent:
            cp.wait_send()
        for cp in mine:
            cp.wait()

    sems = pltpu.SemaphoreType.DMA((per * n_arr,))
    return pl.pallas_call(
        body, name=name, out_shape=[jax.ShapeDtypeStruct((N_DEV,) + a.shape, a.dtype) for a in arrays],
        in_specs=[ANY] * n_arr, out_specs=[ANY] * n_arr,
        scratch_shapes=[sems, sems, pltpu.SemaphoreType.DMA((n_arr,))],
    )(*arrays)


def _exchange_copies(in_refs, out_refs, n_scattered, send_sems, recv_sems, local_sems):
    n_arr = len(in_refs)
    per = N_DEV - 1
    xi, yi, ci = lax.axis_index("x"), lax.axis_index("y"), lax.axis_index("c")
    me = 4 * xi + 2 * yi + ci
    src_of = lambda a, peer: in_refs[a].at[peer] if a < n_scattered else in_refs[a]
    copies = []
    for d in range(1, N_DEV):
        px = 1 - xi if d & 4 else xi
        py = 1 - yi if d & 2 else yi
        pc = 1 - ci if d & 1 else ci
        for a in range(n_arr):
            copies.append(pltpu.make_async_remote_copy(
                src_ref=src_of(a, 4 * px + 2 * py + pc), dst_ref=out_refs[a].at[me],
                send_sem=send_sems.at[per * a + d - 1], recv_sem=recv_sems.at[per * a + d - 1],
                device_id=(px, py, pc), device_id_type=MESH))
    own = [pltpu.make_async_copy(src_of(a, me), out_refs[a].at[me], local_sems.at[a]) for a in range(n_arr)]
    return copies + own


def _exchange_shapes(scattered, shared):
    return ([jax.ShapeDtypeStruct(a.shape, a.dtype) for a in scattered]
            + [jax.ShapeDtypeStruct((N_DEV,) + a.shape, a.dtype) for a in shared])


def _exchange_sems(n_arr):
    sems = pltpu.SemaphoreType.DMA(((N_DEV - 1) * n_arr,))
    return [sems, sems, pltpu.SemaphoreType.DMA((n_arr,))]


def _exchange(scattered, shared, name):
    n_sc = len(scattered)
    n_arr = n_sc + len(shared)

    def body(*refs):
        copies = _exchange_copies(refs[:n_arr], refs[n_arr:2 * n_arr], n_sc, *refs[2 * n_arr:])
        for cp in copies:
            cp.start()
        for cp in copies:
            cp.wait()

    return pl.pallas_call(
        body, name=name, out_shape=_exchange_shapes(scattered, shared),
        in_specs=[ANY] * n_arr, out_specs=[ANY] * n_arr, scratch_shapes=_exchange_sems(n_arr),
    )(*scattered, *shared)


def _adamw(parts, w, m, v, name):
    rows, cols = w.shape
    tile = PACK_ROWS if rows % PACK_ROWS == 0 else rows

    def body(p_ref, w_ref, m_ref, v_ref, g_out, d_out, m_out, v_out):
        g = p_ref[0].astype(F32)
        for s in range(1, N_DEV):
            g = g + p_ref[s].astype(F32)
        m_new = ADAM_B1 * m_ref[...] + (1.0 - ADAM_B1) * g
        v_new = ADAM_B2 * v_ref[...] + (1.0 - ADAM_B2) * (g * g)
        m_hat = m_new / (1.0 - ADAM_B1 ** ADAM_STEP)
        v_hat = v_new / (1.0 - ADAM_B2 ** ADAM_STEP)
        g_out[...] = g
        d_out[...] = -ADAM_LR * (m_hat / (jnp.sqrt(v_hat) + ADAM_EPS) + ADAM_WD * w_ref[...])
        m_out[...] = m_new
        v_out[...] = v_new

    spec = pl.BlockSpec((tile, cols), lambda i: (i, 0))
    return pl.pallas_call(
        body, name=name, grid=(rows // tile,),
        in_specs=[pl.BlockSpec((N_DEV, tile, cols), lambda i: (0, i, 0)), spec, spec, spec],
        out_specs=[spec] * 4, out_shape=[jax.ShapeDtypeStruct((rows, cols), F32)] * 4,
        compiler_params=_cparams(("parallel",)),
    )(parts, w, m, v)


EARLY = [("meta_tokens", 1), ("rwkv_w2", 1), ("rwkv_a2", 1), ("rwkv_g2", 1)]
LATE = [("w_br_attn", 1), ("w_br_rwkv", 1), ("w_o", 0), ("w_ffn_gate", 1), ("w_ffn_up", 1), ("w_ffn_down", 0)]
REPLICATED = ["norm_mix_g", "b_in", "attn_sinks", "rwkv_mix", "rwkv_w0", "rwkv_a0", "rwkv_k_k", "rwkv_k_a",
              "rwkv_r_k", "rwkv_ln_w", "rwkv_ln_b", "norm_ffn_g", "norm_final_g"]
WEIGHTS = ["meta_tokens", "norm_mix_g", "w_in", "b_in", "attn_sinks", "rwkv_mix", "rwkv_w0", "rwkv_w2", "rwkv_a0",
           "rwkv_a2", "rwkv_g2", "rwkv_k_k", "rwkv_k_a", "rwkv_r_k", "rwkv_ln_w", "rwkv_ln_b", "w_br_attn",
           "w_br_rwkv", "w_o", "norm_ffn_g", "w_ffn_gate", "w_ffn_up", "w_ffn_down", "norm_final_g"]


def _pack(arrays, row_multiple):
    flat = jnp.concatenate([a.reshape(-1) for a in arrays])
    per = row_multiple * LANES
    total = -(-flat.shape[0] // per) * per
    return jnp.pad(flat, (0, total - flat.shape[0])).reshape(-1, LANES)


def _unpack(buf, shapes):
    flat = buf.reshape(-1)
    out, pos = [], 0
    for s in shapes:
        n = 1
        for d in s:
            n *= d
        out.append(flat[pos:pos + n].reshape(s))
        pos += n
    return out


def _strip(name, a):
    return a if name in ("meta_tokens", "norm_final_g") else a[0]


def _join(gathered, axis):
    if axis == 0:
        return gathered.reshape(-1, gathered.shape[2])
    return gathered.transpose(1, 0, 2).reshape(gathered.shape[1], -1)


def _split(g, axis):
    if axis == 0:
        return g.reshape(N_DEV, -1, g.shape[1])
    return g.reshape(g.shape[0], N_DEV, -1).transpose(1, 0, 2)


W_IN_LAYOUT = [(768, 2304), (0, 512), (2592, 4640), (2432, 2592), 256 - GATE_LORA, (512, 768), (2304, 2368),
               128 - DECAY_LORA, (2368, 2432), 128 - AAA_LORA, NP - C_DA - 128]


def _w_in_padded(w, shard_width=None):
    rows = w.shape[-2]
    width = D_IN if shard_width is None else shard_width
    parts = []
    for seg in W_IN_LAYOUT:
        if isinstance(seg, int):
            parts.append(jnp.zeros((rows, seg), w.dtype))
            continue
        lo, stop = seg
        while lo < stop:
            p = lo // width
            hi = min(stop, (p + 1) * width)
            src = w if shard_width is None else w[p]
            parts.append(src[:, lo - p * width:hi - p * width])
            lo = hi
    return jnp.concatenate(parts, axis=1)


def _w_in_unpadded(wp, lo=0, stop=D_IN):
    spans, pos = [], 0
    for seg in W_IN_LAYOUT:
        if isinstance(seg, int):
            pos += seg
        else:
            spans.append((seg[0], seg[1], pos))
            pos += seg[1] - seg[0]
    parts = []
    for a, b, at in sorted(spans):
        c, d = max(a, lo), min(b, stop)
        if c < d:
            parts.append(wp[:, at + c - a:at + d - a])
    return jnp.concatenate(parts, axis=1)


def _pad_rows(a, n):
    return jnp.pad(a, ((0, n - a.shape[0]), (0, 0)))


def _device_step(x, tgt, full, gather_late=None, scatter_early=None, scatter_last=None):
    seq = x.shape[0]
    nblk = seq // BLOCK
    lp = seq + BLOCK
    nall = nblk + 1

    w_in_p = full["w_in_p"]
    b_in_p = _w_in_padded(full["b_in"][None])
    mix = full["rwkv_mix"][None]
    mix_r, mix_k, mix_v = mix[:, 0:512], mix[:, 512:1024], mix[:, 1024:1536]
    mix_dw = jnp.pad(mix[:, 1536:1600], ((0, 0), (0, 64)))
    mix_da = jnp.pad(mix[:, 1600:1664], ((0, 0), (0, 64)))
    mix_dg = jnp.pad(mix[:, 1664:1824], ((0, 0), (0, 96)))
    w2_p = _pad_rows(full["rwkv_w2"].astype(F32), 128)
    a2_p = _pad_rows(full["rwkv_a2"].astype(F32), 128)
    g2_p = _pad_rows(full["rwkv_g2"].astype(F32), 256)
    row = lambda name: full[name].reshape(1, -1)
    sinks = row("attn_sinks")
    rope_c, rope_s1, rope_s2 = _rope_tables(lp)

    hpad = jnp.concatenate([jnp.zeros((PAD_ROWS, D_MODEL), F32), full["meta_tokens"].astype(F32), x], axis=0)
    (u,) = _rows_fwd(_rms_fn, [_view(hpad)], [row("norm_mix_g")], [D_MODEL], nblk=nall, name="norm_mix",
                     out_dtype=BF16)
    proj = _mm(u, w_in_p, bias=b_in_p, name="in_proj")
    (q_r,) = _rows_fwd(_rope_fwd_fn, [_view(proj, 512, C_Q // 512), _view(rope_c), _view(rope_s1), _view(rope_s2)],
                       [], [512], nblk=nall, name="rope_q")
    (k_r,) = _rows_fwd(_rope_fwd_fn, [_view(proj, 128, C_KA // 128), _view(rope_c), _view(rope_s1),
                                      _view(rope_s2)], [], [128], nblk=nall, name="rope_k")
    heads = lambda a, nh: a.reshape(a.shape[0], nh, HEAD_DIM).transpose(1, 0, 2)
    unheads = lambda a: a.transpose(1, 0, 2).reshape(a.shape[1], -1)
    q_h, k_h = heads(q_r, Q_HEADS), heads(k_r, KV_HEADS)
    v_h = heads(proj[:, C_VA:C_VA + 128], KV_HEADS)
    o_h, lse = _attn_fwd(q_h, k_h, v_h, sinks, nblk=nblk, name="attn_fwd")
    y_attn = unheads(o_h).astype(BF16)

    rw_cols = jnp.concatenate([proj[:, C_R:C_R + 1536], proj[:, C_DG:C_DG + 256], proj[:, C_DW:C_DW + 256]], axis=1)
    rw_prev = jnp.pad(rw_cols[:-1], ((1, 0), (0, 0)))
    pre_rows = [_view(proj, 512, 0), _view(proj, 512, 1), _view(proj, 512, 2), _view(proj, 128, C_DW // 128),
                _view(proj, 128, C_DA // 128), _view(proj, 256, C_DG // 256),
                _view(rw_prev, 512, 0), _view(rw_prev, 512, 1), _view(rw_prev, 512, 2), _view(rw_prev, 128, 14),
                _view(rw_prev, 128, 15), _view(rw_prev, 256, 6)]
    pre_consts = [mix_r, mix_k, mix_v, mix_dw, mix_da, mix_dg, row("rwkv_w0"), w2_p, row("rwkv_a0"), a2_p, g2_p,
                  row("rwkv_k_k"), row("rwkv_k_a")]
    xt_block = (BLOCK // SCAN_T * VEC_ROWS, LANES)
    r_t, k_mod, v_t, gate, xt = _rows_fwd(_rwkv_pre_xt_fn, pre_rows, pre_consts, [RW_DIM] * 4 + [xt_block],
                                          nblk=nall, name="rwkv_pre")
    xt = xt.reshape(-1, VEC_ROWS, LANES)
    spread, collect = _selectors()
    y_scan, hist, *late = _wkv_fwd(xt, v_t, spread, name="wkv_fwd", shared=gather_late[0] if gather_late else ())
    if gather_late:
        full = {**full, **gather_late[1](late)}
    post_rows = [_view(y_scan, off=1), _view(r_t, off=1), _view(k_mod, off=1), _view(v_t, off=1), _view(gate, off=1)]
    post_consts = [row("rwkv_ln_w"), row("rwkv_ln_b"), row("rwkv_r_k")]
    (y_rwkv,) = _rows_fwd(_rwkv_post_fn, post_rows, post_consts, [RW_DIM], nblk=nblk, name="rwkv_post",
                          out_dtype=BF16)

    ya = _mm(y_attn, full["w_br_attn"], name="br_attn")
    yr = _mm(y_rwkv, full["w_br_rwkv"], name="br_rwkv")
    merge_rows = [_view(ya), _view(yr), _view(proj, 1024, C_G1 // 1024, 1), _view(proj, 1024, C_G2 // 1024, 1)]
    (merged,) = _rows_fwd(_merge_fn, merge_rows, [], [D_MODEL], nblk=nblk, name="merge", out_dtype=BF16)
    h1 = _mm(merged, full["w_o"], residual=x, name="out_proj")
    (f,) = _rows_fwd(_rms_fn, [_view(h1)], [row("norm_ffn_g")], [D_MODEL], nblk=nblk, name="norm_ffn",
                     out_dtype=BF16)
    ff_gate = _mm(f, full["w_ffn_gate"], name="ffn_gate")
    ff_up = _mm(f, full["w_ffn_up"], name="ffn_up")
    (act,) = _rows_fwd(_swiglu_fn, [_view(ff_gate), _view(ff_up)], [], [D_FF], nblk=nblk, name="swiglu",
                       out_dtype=BF16)
    h2 = _mm(act, full["w_ffn_down"], residual=h1, name="ffn_down")

    grads = {}
    ones_col = jnp.ones((seq, 1), F32)
    loss_rows, dh2, grads["norm_final_g"] = _rows_bwd(
        _loss_fn, [_view(h2), _view(tgt)], [row("norm_final_g")], [_view(ones_col)], nblk=nblk, name="loss",
        diff_rows=[0], diff_consts=[0], fwd_widths=[1])
    loss = jnp.sum(loss_rows)

    dact = _mm(dh2, full["w_ffn_down"], tb=True, name="d_act")
    grads["w_ffn_down"] = _mm(act, dh2, ta=True, name="dw_ffn_down")
    dgate, dup = _rows_bwd(_swiglu_fn, [_view(ff_gate), _view(ff_up)], [], [_view(dact)], nblk=nblk,
                           name="swiglu_bwd", diff_rows=[0, 1], diff_consts=[], row_dtype=BF16)
    grads["w_ffn_gate"] = _mm(f, dgate, ta=True, name="dw_ffn_gate")
    grads["w_ffn_up"] = _mm(f, dup, ta=True, name="dw_ffn_up")
    df = _mm(dgate, full["w_ffn_gate"], tb=True, name="df_gate")
    df = _mm(dup, full["w_ffn_up"], tb=True, residual=df, name="df_up")
    dh1, grads["norm_ffn_g"] = _rows_bwd(_rms_fn, [_view(h1)], [row("norm_ffn_g")], [_view(df)], nblk=nblk,
                                         name="norm_ffn_bwd", diff_rows=[0], diff_consts=[0], acc=[_view(dh2)])
    dmerged = _mm(dh1, full["w_o"], tb=True, name="d_merged")
    grads["w_o"] = _mm(merged, dh1, ta=True, name="dw_o")
    dya, dyr, dg1, dg2 = _rows_bwd(_merge_fn, merge_rows, [], [_view(dmerged)], nblk=nblk, name="merge_bwd",
                                   diff_rows=[0, 1, 2, 3], diff_consts=[], row_dtype=BF16)
    grads["w_br_attn"] = _mm(y_attn, dya, ta=True, name="dw_br_attn")
    grads["w_br_rwkv"] = _mm(y_rwkv, dyr, ta=True, name="dw_br_rwkv")
    dy_attn = _mm(dya, full["w_br_attn"], tb=True, name="d_y_attn")
    dy_rwkv = _mm(dyr, full["w_br_rwkv"], tb=True, name="d_y_rwkv")

    post = _rows_bwd(_rwkv_post_fn, post_rows, post_consts, [_view(dy_rwkv)], nblk=nblk, name="rwkv_post_bwd",
                     diff_rows=[0, 1, 2, 3, 4], diff_consts=[0, 1, 2])
    dys, dr_post, dk_post, dv_post, dgate_post = post[:5]
    grads["rwkv_ln_w"], grads["rwkv_ln_b"], grads["rwkv_r_k"] = post[5:]
    dxt, dv_s, *early_parts = _wkv_bwd(xt, v_t, hist, dys, spread, collect, name="wkv_bwd",
                                       scattered=scatter_early(grads) if scatter_early else ())
    pre_cts = [_view(dxt.reshape(-1, LANES), rows=xt_block[0]), _view(dv_s)] + [
        _view(t, off=-1) for t in (dr_post, dk_post, dv_post, dgate_post)]
    pre = _rows_bwd(_rwkv_pre_fn, pre_rows, pre_consts, pre_cts, nblk=nall, name="rwkv_pre_bwd",
                    diff_rows=list(range(12)), diff_consts=list(range(13)), ct_map=_rwkv_pre_cts)
    d_cur, d_prev, d_par = pre[0:6], pre[6:12], pre[12:]
    up = lambda t: jnp.pad(t[1:], ((0, 1), (0, 0)))
    d_rw = [c + up(p) for c, p in zip(d_cur, d_prev)]
    grads["rwkv_mix"] = jnp.concatenate([d_par[0], d_par[1], d_par[2], d_par[3][:, :DECAY_LORA],
                                         d_par[4][:, :AAA_LORA], d_par[5][:, :GATE_LORA]], axis=1)
    grads["rwkv_w0"], grads["rwkv_w2"] = d_par[6], d_par[7][:DECAY_LORA]
    grads["rwkv_a0"], grads["rwkv_a2"] = d_par[8], d_par[9][:AAA_LORA]
    grads["rwkv_g2"] = d_par[10][:GATE_LORA]
    grads["rwkv_k_k"], grads["rwkv_k_a"] = d_par[11], d_par[12]

    do_h = heads(dy_attn, Q_HEADS)
    dq_h, dk_h, dv_h, grads["attn_sinks"] = _attn_bwd(q_h, k_h, v_h, sinks, o_h, lse, do_h, nblk=nblk,
                                                      name="attn_bwd")
    dq_r = jnp.pad(unheads(dq_h), ((BLOCK, 0), (0, 0)))
    (dq,) = _rows_fwd(_rope_bwd_fn, [_view(dq_r), _view(rope_c), _view(rope_s1), _view(rope_s2)], [], [512],
                      nblk=nall, name="rope_q_bwd", out_dtype=BF16)
    (dka,) = _rows_fwd(_rope_bwd_fn, [_view(unheads(dk_h)), _view(rope_c), _view(rope_s1),
                                      _view(rope_s2)], [], [128], nblk=nall, name="rope_k_bwd", out_dtype=BF16)
    dva = unheads(dv_h)

    lead = lambda t: jnp.pad(t, ((BLOCK, 0), (0, 0)))
    pieces = [d_rw[0], d_rw[1], d_rw[2], dq, lead(dg1), lead(dg2), d_rw[5], dka, dva, d_rw[3], d_rw[4],
              jnp.zeros((lp, NP - C_DA - 128), BF16)]
    dproj = jnp.concatenate([p.astype(BF16) for p in pieces], axis=1)
    grads["w_in_p"] = _mm(u, dproj, ta=True, name="dw_in")
    grads["b_in"] = _w_in_unpadded(_colsum(dproj, name="db_in"))
    du, *last_parts = _mm(dproj, w_in_p, tb=True, name="d_u", scattered=scatter_last(grads)) if scatter_last else (
        _mm(dproj, w_in_p, tb=True, name="d_u"),)
    dh, grads["norm_mix_g"] = _rows_bwd(_rms_fn, [_view(hpad)], [row("norm_mix_g")], [_view(du)], nblk=nall,
                                        name="norm_mix_bwd", diff_rows=[0], diff_consts=[0], acc=[_view(lead(dh1))])
    grads["meta_tokens"] = dh[PAD_ROWS:BLOCK]
    return loss, dh[BLOCK:], grads, early_parts, last_parts


def kernel(x, meta_tokens, norm_mix_g, w_in, b_in, attn_sinks, rwkv_mix, rwkv_w0, rwkv_w2, rwkv_a0, rwkv_a2, rwkv_g2, rwkv_k_k, rwkv_k_a, rwkv_r_k, rwkv_ln_w, rwkv_ln_b, w_br_attn, w_br_rwkv, w_o, norm_ffn_g, w_ffn_gate, w_ffn_up, w_ffn_down, norm_final_g, loss_target, m_meta_tokens, m_norm_mix_g, m_w_in, m_b_in, m_attn_sinks, m_rwkv_mix, m_rwkv_w0, m_rwkv_w2, m_rwkv_a0, m_rwkv_a2, m_rwkv_g2, m_rwkv_k_k, m_rwkv_k_a, m_rwkv_r_k, m_rwkv_ln_w, m_rwkv_ln_b, m_w_br_attn, m_w_br_rwkv, m_w_o, m_norm_ffn_g, m_w_ffn_gate, m_w_ffn_up, m_w_ffn_down, m_norm_final_g, v_meta_tokens, v_norm_mix_g, v_w_in, v_b_in, v_attn_sinks, v_rwkv_mix, v_rwkv_w0, v_rwkv_w2, v_rwkv_a0, v_rwkv_a2, v_rwkv_g2, v_rwkv_k_k, v_rwkv_k_a, v_rwkv_r_k, v_rwkv_ln_w, v_rwkv_ln_b, v_w_br_attn, v_w_br_rwkv, v_w_o, v_norm_ffn_g, v_w_ffn_gate, v_w_ffn_up, v_w_ffn_down, v_norm_final_g):
    given = dict(locals())
    wts = {n: _strip(n, given[n]) for n in WEIGHTS}
    mom = {n: _strip(n, given["m_" + n]) for n in WEIGHTS}
    var = {n: _strip(n, given["v_" + n]) for n in WEIGHTS}
    small_shapes = [wts[n].shape for n in REPLICATED]
    width = wts["w_in"].shape[1]
    wire = lambda table: [wts[n].astype(BF16) for n, _ in table]

    w_in_all, *early_all = _all_gather([wts["w_in"].astype(BF16)] + wire(EARLY), name="gather_weights")
    full = {n: wts[n] for n in REPLICATED}
    full.update({n: _join(g, axis) for (n, axis), g in zip(EARLY, early_all)})
    full["w_in_p"] = _w_in_padded(w_in_all, shard_width=width)
    gather_late = (wire(LATE), lambda got: {n: _join(g, axis) for (n, axis), g in zip(LATE, got)})
    scatter_early = lambda g: [_split(g[n], axis).astype(BF16) for n, axis in LATE]
    scatter_last = lambda g: [jnp.stack([_w_in_unpadded(g["w_in_p"], p * width, (p + 1) * width)
                                         for p in range(N_DEV)]).astype(BF16)]

    loss_part, grad_x, grads, parts_late, (parts_w_in,) = _device_step(
        x[0], loss_target[0], full, gather_late, scatter_early, scatter_last)

    g_early = [_split(grads[n], axis).astype(BF16) for n, axis in EARLY]
    zero = jnp.zeros((1,), F32)
    g_small = _pack([grads[n].reshape(wts[n].shape) for n in REPLICATED] + [loss_part.reshape(1)], 8)
    *parts_early, parts_small = _exchange(g_early, [g_small], name="exchange_grads")
    results = [{}, {}, {}, {}]
    for (n, _), parts in zip([("w_in", 1)] + EARLY + LATE, [parts_w_in] + parts_early + list(parts_late)):
        for kind, a in enumerate(_adamw(parts, wts[n], mom[n], var[n], name="adamw_" + n)):
            results[kind][n] = a
    small = _adamw(parts_small, _pack([wts[n] for n in REPLICATED] + [zero], 8),
                   _pack([mom[n] for n in REPLICATED] + [zero], 8), _pack([var[n] for n in REPLICATED] + [zero], 8),
                   name="adamw_replicated")
    for kind in range(4):
        for n, a in zip(REPLICATED, _unpack(small[kind], small_shapes)):
            results[kind][n] = a
    loss = _unpack(small[0], small_shapes + [(1,)])[-1][0]
    out = [loss, grad_x[None]]
    for kind in range(4):
        out += [results[kind][n].reshape(given[n].shape) for n in WEIGHTS]
    return tuple(out)
```

```python
import functools

import jax
import jax.numpy as jnp
from jax import lax
from jax.experimental import pallas as pl
from jax.experimental.pallas import tpu as pltpu

F32 = jnp.float32
BF16 = jnp.bfloat16

N_DEV = 8
D_MODEL = 1024
N_META = 16
BLOCK = 128
PAD_ROWS = BLOCK - N_META
HEAD_DIM = 64
Q_HEADS = 8
KV_HEADS = 2
GROUP = Q_HEADS // KV_HEADS
ROPE_DIM = HEAD_DIM // 4
ROPE_HALF = ROPE_DIM // 2
ROPE_THETA = 500000.0
RW_HEADS = 8
RW_DIM = 512
DECAY_LORA = 64
AAA_LORA = 64
GATE_LORA = 160
D_FF = 2816
D_IN = 4640
RMS_EPS = 1e-6
RWKV_LN_EPS = 64e-5
NEG_INF = -1e30
SCAN_T = 16
LANES = 128
PACK_ROWS = 256

ADAM_LR = 0.001
ADAM_B1 = 0.9
ADAM_B2 = 0.999
ADAM_EPS = 1e-08
ADAM_WD = 0.01
ADAM_STEP = 10

C_R, C_K, C_V, C_Q = 0, 512, 1024, 1536
C_G1, C_G2 = 2048, 3072
C_DG, C_KA, C_VA, C_DW, C_DA = 4096, 4352, 4480, 4608, 4736
NP = 5120

VMEM_LIMIT = 48 * 1024 * 1024


def _cparams(sem):
    return pltpu.CompilerParams(dimension_semantics=sem, vmem_limit_bytes=VMEM_LIMIT)


def _pick(n, cands):
    for c in cands:
        if n % c == 0:
            return c
    raise ValueError(f"no tile for {n}")


def _mm(a, b, *, ta=False, tb=False, bias=None, residual=None, name, scattered=()):
    m = a.shape[1] if ta else a.shape[0]
    k = a.shape[0] if ta else a.shape[1]
    n = b.shape[0] if tb else b.shape[1]
    assert k == (b.shape[1] if tb else b.shape[0]), (a.shape, b.shape, ta, tb)
    tm = _pick(m, (512, 1408, 256, 128) if ta else (1024, 528, 512, 384, 256, 128))
    tn = _pick(n, (1024, 512, 1408, 256, 128))
    if k <= 1024:
        tk = k
    else:
        tk = _pick(k, (1024, 1056, 528, 512) if (ta and not tb) else (1024, 1408, 512, 256, 128))
    nk = k // tk
    has_bias = bias is not None
    has_res = residual is not None
    dn = (((0 if ta else 1,), (1 if tb else 0,)), ((), ()))

    def body(*refs):
        a_ref, b_ref = refs[0], refs[1]
        pos = 2
        bias_ref = res_ref = None
        if has_bias:
            bias_ref = refs[pos]
            pos += 1
        if has_res:
            res_ref = refs[pos]
            pos += 1
        o_ref, acc_ref = refs[pos], refs[pos + 1]
        kk = pl.program_id(2)
        part = lax.dot_general(a_ref[...].astype(BF16), b_ref[...].astype(BF16), dn, preferred_element_type=F32)

        def finish(out):
            if has_bias:
                out = out + bias_ref[...]
            if has_res:
                out = out + res_ref[...]
            o_ref[...] = out

        if nk == 1:
            finish(part)
        else:
            @pl.when(kk == 0)
            def _():
                acc_ref[...] = part

            @pl.when((kk > 0) & (kk < nk - 1))
            def _():
                acc_ref[...] += part

            @pl.when(kk == nk - 1)
            def _():
                finish(acc_ref[...] + part)

    in_specs = [
        pl.BlockSpec((tk, tm), lambda i, j, kk: (kk, i)) if ta else pl.BlockSpec((tm, tk), lambda i, j, kk: (i, kk)),
        pl.BlockSpec((tn, tk), lambda i, j, kk: (j, kk)) if tb else pl.BlockSpec((tk, tn), lambda i, j, kk: (kk, j)),
    ]
    args = [a, b]
    if has_bias:
        in_specs.append(pl.BlockSpec((1, tn), lambda i, j, kk: (0, j)))
        args.append(bias)
    if has_res:
        in_specs.append(pl.BlockSpec((tm, tn), lambda i, j, kk: (i, j)))
        args.append(residual)
    grid = (m // tm, n // tn, nk)
    n_x = len(scattered)
    out = pl.pallas_call(
        _with_exchange(body, len(args), 1, 1, scattered, (), grid), name=name, grid=grid,
        in_specs=in_specs + [ANY] * n_x,
        out_specs=[pl.BlockSpec((tm, tn), lambda i, j, kk: (i, j))] + [ANY] * n_x,
        out_shape=[jax.ShapeDtypeStruct((m, n), F32)] + _exchange_shapes(scattered, ()),
        scratch_shapes=[pltpu.VMEM((tm, tn) if nk > 1 else (8, LANES), F32)] + (_exchange_sems(n_x) if n_x else []),
        compiler_params=_cparams(("arbitrary",) * 3 if n_x else ("parallel", "parallel", "arbitrary")),
    )(*args, *scattered)
    return out if n_x else out[0]


def _colsum(x, name):
    m, n = x.shape
    tm = BLOCK

    def body(x_ref, o_ref):
        i = pl.program_id(0)
        s = jnp.sum(x_ref[...].astype(F32), axis=0, keepdims=True)

        @pl.when(i == 0)
        def _():
            o_ref[...] = s

        @pl.when(i > 0)
        def _():
            o_ref[...] += s

    return pl.pallas_call(
        body, name=name, grid=(m // tm,),
        in_specs=[pl.BlockSpec((tm, n), lambda i: (i, 0))],
        out_specs=pl.BlockSpec((1, n), lambda i: (0, 0)),
        out_shape=jax.ShapeDtypeStruct((1, n), F32),
        compiler_params=_cparams(("arbitrary",)),
    )(x)


def _view(arr, width=None, col=0, off=0, rows=BLOCK):
    return (arr, arr.shape[1] if width is None else width, col, off, rows)


def _row_spec(view):
    _, width, col, off, rows = view
    if off < 0:
        return pl.BlockSpec((rows, width), lambda i, col=col, off=off: (jnp.maximum(i + off, 0), col))
    return pl.BlockSpec((rows, width), lambda i, col=col, off=off: (i + off, col))


def _const_spec(arr):
    return pl.BlockSpec(arr.shape, lambda i: (0,) * arr.ndim)


def _rows_fwd(fn, rows, consts, out_widths, *, nblk, name, out_dtype=F32):
    nr, nc = len(rows), len(consts)
    out_blocks = [(BLOCK, w) if isinstance(w, int) else w for w in out_widths]

    def body(*refs):
        i = pl.program_id(0)
        vals = [r[...] for r in refs[:nr + nc]]
        outs = fn(i, *vals)
        for o_ref, o in zip(refs[nr + nc:], outs):
            o_ref[...] = o.astype(o_ref.dtype)

    return pl.pallas_call(
        body, name=name, grid=(nblk,),
        in_specs=[_row_spec(v) for v in rows] + [_const_spec(c) for c in consts],
        out_specs=[pl.BlockSpec(b, lambda i: (i, 0)) for b in out_blocks],
        out_shape=[jax.ShapeDtypeStruct((nblk * r, w), out_dtype) for r, w in out_blocks],
        compiler_params=_cparams(("parallel",)),
    )(*[v[0] for v in rows], *consts)


def _rows_bwd(fn, rows, consts, cts, *, nblk, name, diff_rows, diff_consts, acc=None, fwd_widths=(), row_dtype=F32,
              ct_map=None):
    nr, nc = len(rows), len(consts)
    ct_views = [c for c in cts if c is not None]
    acc = acc or [None] * len(diff_rows)
    acc_views = [a for a in acc if a is not None]
    n_in = nr + nc + len(ct_views) + len(acc_views)
    n_fwd = len(fwd_widths)

    def body(*refs):
        i = pl.program_id(0)
        row_vals = [r[...] for r in refs[:nr]]
        const_vals = [r[...] for r in refs[nr:nr + nc]]
        ct_vals = [r[...] for r in refs[nr + nc:nr + nc + len(ct_views)]]
        acc_vals = [r[...] for r in refs[nr + nc + len(ct_views):n_in]]
        out_refs = refs[n_in:]

        def f(*dargs):
            rv = list(row_vals)
            cv = list(const_vals)
            for pos, idx in enumerate(diff_rows):
                rv[idx] = dargs[pos]
            for pos, idx in enumerate(diff_consts):
                cv[idx] = dargs[len(diff_rows) + pos]
            return tuple(fn(i, *rv, *cv))

        primals = [row_vals[idx] for idx in diff_rows] + [const_vals[idx] for idx in diff_consts]
        outs, pull = jax.vjp(f, *primals)
        full_ct, ci = [], 0
        if ct_map is not None:
            full_ct = ct_map(i, *ct_vals)
        else:
            for o, c in zip(outs, cts):
                if c is None:
                    full_ct.append(jnp.zeros_like(o))
                else:
                    full_ct.append(ct_vals[ci])
                    ci += 1
        grads = pull(tuple(full_ct))
        for o_ref, o in zip(out_refs[:n_fwd], outs):
            o_ref[...] = o
        ai = 0
        for pos in range(len(diff_rows)):
            g = grads[pos]
            if acc[pos] is not None:
                g = g + acc_vals[ai]
                ai += 1
            out_refs[n_fwd + pos][...] = g.astype(row_dtype)
        for pos in range(len(diff_consts)):
            g = grads[len(diff_rows) + pos]
            o_ref = out_refs[n_fwd + len(diff_rows) + pos]

            @pl.when(i == 0)
            def _(o_ref=o_ref, g=g):
                o_ref[...] = g

            @pl.when(i > 0)
            def _(o_ref=o_ref, g=g):
                o_ref[...] += g

    out_specs = [pl.BlockSpec((BLOCK, w), lambda i: (i, 0)) for w in fwd_widths]
    out_shape = [jax.ShapeDtypeStruct((nblk * BLOCK, w), F32) for w in fwd_widths]
    for idx in diff_rows:
        out_specs.append(pl.BlockSpec((BLOCK, rows[idx][1]), lambda i: (i, 0)))
        out_shape.append(jax.ShapeDtypeStruct((nblk * BLOCK, rows[idx][1]), row_dtype))
    for idx in diff_consts:
        out_specs.append(_const_spec(consts[idx]))
        out_shape.append(jax.ShapeDtypeStruct(consts[idx].shape, F32))
    return pl.pallas_call(
        body, name=name, grid=(nblk,),
        in_specs=([_row_spec(v) for v in rows] + [_const_spec(c) for c in consts]
                  + [_row_spec(v) for v in ct_views] + [_row_spec(v) for v in acc_views]),
        out_specs=out_specs, out_shape=out_shape,
        compiler_params=_cparams(("arbitrary",)),
    )(*[v[0] for v in rows], *consts, *[v[0] for v in ct_views], *[v[0] for v in acc_views])


def _rms_fn(i, x, g):
    return (x * lax.rsqrt(jnp.mean(x * x, axis=-1, keepdims=True) + RMS_EPS) * g,)


def _sigmoid(x):
    return 1.0 / (1.0 + jnp.exp(-x))


def _softplus(x):
    return jnp.maximum(x, 0.0) + jnp.log(1.0 + jnp.exp(-jnp.abs(x)))


def _split2(x):
    hi = x.astype(BF16)
    lo = (x - hi.astype(F32)).astype(BF16)
    return jnp.concatenate([hi, lo], axis=1)


@jax.custom_vjp
def _head_sum(x):
    r = lax.broadcasted_iota(jnp.int32, (2 * RW_DIM, RW_DIM), 0) % RW_DIM // HEAD_DIM
    c = lax.broadcasted_iota(jnp.int32, (2 * RW_DIM, RW_DIM), 1) // HEAD_DIM
    return jnp.dot(_split2(x), (r == c).astype(BF16), preferred_element_type=F32)


_head_sum.defvjp(lambda x: (_head_sum(x), None), lambda _, ct: (_head_sum(ct),))


@jax.custom_vjp
def _dot_bf16(x, w):
    return jnp.dot(x.astype(BF16), w.astype(BF16), preferred_element_type=F32)


def _dot_bf16_bwd(res, ct):
    x, w = res
    ct = ct.astype(BF16)
    dx = lax.dot_general(ct, w.astype(BF16), (((1,), (1,)), ((), ())), preferred_element_type=F32)
    dw = lax.dot_general(x.astype(BF16), ct, (((0,), (0,)), ((), ())), preferred_element_type=F32)
    return dx, dw


_dot_bf16.defvjp(lambda x, w: (_dot_bf16(x, w), (x, w)), _dot_bf16_bwd)


def _rwkv_pre_fn(i, r, k, v, dw, da, dg, r_p, k_p, v_p, dw_p, da_p, dg_p,
                 mix_r, mix_k, mix_v, mix_dw, mix_da, mix_dg, w0, w2, a0, a2, g2, k_k, k_a):
    row = i * BLOCK + lax.broadcasted_iota(jnp.int32, (BLOCK, 1), 0)
    live = row >= PAD_ROWS
    live_prev = row >= PAD_ROWS + 1

    def shift(cur, prev, mix):
        cur = jnp.where(live, cur, 0.0)
        prev = jnp.where(live_prev, prev, 0.0)
        return cur + (prev - cur) * mix

    r = shift(r, r_p, mix_r)
    k = shift(k, k_p, mix_k)
    v = shift(v, v_p, mix_v)
    dw = shift(dw, dw_p, mix_dw)
    da = shift(da, da_p, mix_da)
    dg = shift(dg, dg_p, mix_dg)
    wlog = -_softplus(-(w0 + _dot_bf16(jnp.tanh(dw), w2))) - 0.5
    decay = jnp.exp(-jnp.exp(wlog))
    a = _sigmoid(a0 + _dot_bf16(da, a2))
    g = _dot_bf16(_sigmoid(dg), g2)
    kk = k * k_k
    norm_sq = jnp.where(live, _head_sum(kk * kk), 1.0)
    kk = kk / jnp.maximum(jnp.sqrt(norm_sq), 1e-12)
    k_mod = k * (1.0 + (a - 1.0) * k_a)
    return r, decay, k_mod, v, -kk, kk * a, g


def _rwkv_pre_xt_fn(i, *args):
    r, decay, k_mod, v, a_neg, b, g = _rwkv_pre_fn(i, *args)
    t = SCAN_T
    xt = jnp.concatenate([_rows_to_xt(x[c * t:(c + 1) * t]) for c in range(BLOCK // t)
                          for x in (a_neg, decay, b, k_mod, r)], axis=0)
    return r, k_mod, v, g, xt


def _rwkv_pre_cts(i, dxt, dv_s, dr_p, dk_p, dv_p, dg_p):
    t = SCAN_T
    d_a, d_w, d_b, d_k, d_r = [
        jnp.concatenate([_xt_to_rows(dxt[c * VEC_ROWS + n * HEAD_DIM:c * VEC_ROWS + (n + 1) * HEAD_DIM])
                         for c in range(BLOCK // t)], axis=0) for n in range(N_VEC)]
    dr_p, dk_p, dv_p, dg_p = [jnp.where(i > 0, x, 0.0) for x in (dr_p, dk_p, dv_p, dg_p)]
    return d_r + dr_p, d_w, d_k + dk_p, dv_s + dv_p, d_a, d_b, dg_p


def _rwkv_post_fn(i, ys, r, k_mod, v, g, ln_w, ln_b, r_k):
    mean = _head_sum(ys) * (1.0 / HEAD_DIM)
    d = ys - mean
    var = _head_sum(d * d) * (1.0 / HEAD_DIM)
    yn = d * lax.rsqrt(var + RWKV_LN_EPS) * ln_w + ln_b
    bonus = _head_sum(r * k_mod * r_k) * v
    return ((yn + bonus) * g,)


def _merge_fn(i, ya, yr, g1, g2):
    return (_sigmoid(g1) * ya + _sigmoid(g2) * yr,)


def _swiglu_fn(i, gate, up):
    return (gate * _sigmoid(gate) * up,)


def _loss_fn(i, h, tgt, g):
    y = h * lax.rsqrt(jnp.mean(h * h, axis=-1, keepdims=True) + RMS_EPS) * g
    err = y - tgt
    return (0.5 * jnp.mean(err * err, axis=-1, keepdims=True),)


def _rope_tables(lp):
    pos = (jnp.arange(lp, dtype=jnp.int32) - PAD_ROWS).astype(F32)
    inv_freq = jnp.power(jnp.float32(ROPE_THETA), -jnp.arange(ROPE_HALF, dtype=F32) * (2.0 / ROPE_DIM))
    ang = pos[:, None] * inv_freq[None, :]
    cos, sin = jnp.cos(ang), jnp.sin(ang)
    one = jnp.ones((lp, HEAD_DIM - ROPE_DIM), F32)
    zero_h = jnp.zeros((lp, ROPE_HALF), F32)
    zero_r = jnp.zeros((lp, HEAD_DIM - ROPE_DIM), F32)
    c = jnp.concatenate([cos, cos, one], axis=1)
    s1 = jnp.concatenate([-sin, zero_h, zero_r], axis=1)
    s2 = jnp.concatenate([zero_h, sin, zero_r], axis=1)
    return tuple(jnp.tile(t, (1, LANES // HEAD_DIM)) for t in (c, s1, s2))


def _rope_fwd_fn(i, x, c, s1, s2):
    n = x.shape[1]
    c, s1, s2 = [jnp.tile(t, (1, n // LANES)) for t in (c, s1, s2)]
    return (x * c + pltpu.roll(x, n - ROPE_HALF, 1) * s1 + pltpu.roll(x, ROPE_HALF, 1) * s2,)


def _rope_bwd_fn(i, dy, c, s1, s2):
    n = dy.shape[1]
    c, s1, s2 = [jnp.tile(t, (1, n // LANES)) for t in (c, s1, s2)]
    return (dy * c + pltpu.roll(dy * s1, ROPE_HALF, 1) + pltpu.roll(dy * s2, n - ROPE_HALF, 1),)


def _attn_mask(i):
    r = lax.broadcasted_iota(jnp.int32, (BLOCK, 3 * BLOCK), 0)
    c = lax.broadcasted_iota(jnp.int32, (BLOCK, 3 * BLOCK), 1)
    meta = (c < BLOCK) & (c >= PAD_ROWS)
    prev = (c >= BLOCK) & (c < 2 * BLOCK) & ((c - BLOCK) > r) & (i >= 1)
    cur = (c >= 2 * BLOCK) & ((c - 2 * BLOCK) <= r)
    return meta | prev | cur


def _attn_rows(ref, g):
    return ref[:, g * HEAD_DIM:(g + 1) * HEAD_DIM]


def _attn_group(i, g, q_all, k_refs, v_refs, s_ref):
    heads = range(g * GROUP, (g + 1) * GROUP)
    kcat = jnp.concatenate([_attn_rows(r, g) for r in k_refs], axis=0).astype(BF16)
    vcat = jnp.concatenate([_attn_rows(r, g) for r in v_refs], axis=0).astype(BF16)
    qg = jnp.concatenate([q_all[:, h * HEAD_DIM:(h + 1) * HEAD_DIM] for h in heads], axis=0).astype(BF16)
    sink = jnp.concatenate([jnp.broadcast_to(s_ref[0:1, h:h + 1], (BLOCK, 1)) for h in heads], axis=0)
    s = lax.dot_general(qg, kcat, (((1,), (1,)), ((), ())), preferred_element_type=F32) * (HEAD_DIM ** -0.5)
    valid = jnp.concatenate([_attn_mask(i)] * GROUP, axis=0)
    return heads, qg, kcat, vcat, sink, jnp.where(valid, s, NEG_INF)


def _attn_specs(v_col):
    blk = lambda f: pl.BlockSpec((BLOCK, 2 * HEAD_DIM), f)
    keys = [blk(lambda i: (0, 0)), blk(lambda i: (i, 0)), blk(lambda i: (i + 1, 0))]
    vals = [blk(lambda i: (0, v_col)), blk(lambda i: (i, v_col)), blk(lambda i: (i + 1, v_col))]
    return keys + vals + [pl.BlockSpec((1, Q_HEADS), lambda i: (0, 0))]


def _attn_fwd(q, k, v, v_col, sinks, *, nblk, name):
    def body(q_ref, km_ref, kp_ref, kc_ref, vm_ref, vp_ref, vc_ref, s_ref, o_ref, lse_ref):
        i = pl.program_id(0)
        q_all = q_ref[...]
        for g in range(KV_HEADS):
            heads, _, _, vcat, sink, s = _attn_group(i, g, q_all, (km_ref, kp_ref, kc_ref), (vm_ref, vp_ref, vc_ref),
                                                     s_ref)
            m = jnp.maximum(jnp.max(s, axis=-1, keepdims=True), sink)
            p = jnp.exp(s - m)
            den = jnp.sum(p, axis=-1, keepdims=True) + jnp.exp(sink - m)
            o = jnp.dot(p.astype(BF16), vcat, preferred_element_type=F32) / den
            lse = m + jnp.log(den)
            for n, h in enumerate(heads):
                o_ref[:, h * HEAD_DIM:(h + 1) * HEAD_DIM] = o[n * BLOCK:(n + 1) * BLOCK]
                lse_ref[:, h:h + 1] = lse[n * BLOCK:(n + 1) * BLOCK]

    return pl.pallas_call(
        body, name=name, grid=(nblk,),
        in_specs=[pl.BlockSpec((BLOCK, Q_HEADS * HEAD_DIM), lambda i: (i + 1, 0))] + _attn_specs(v_col),
        out_specs=[pl.BlockSpec((BLOCK, Q_HEADS * HEAD_DIM), lambda i: (i, 0)),
                   pl.BlockSpec((BLOCK, Q_HEADS), lambda i: (i, 0))],
        out_shape=[jax.ShapeDtypeStruct((nblk * BLOCK, Q_HEADS * HEAD_DIM), F32),
                   jax.ShapeDtypeStruct((nblk * BLOCK, Q_HEADS), F32)],
        compiler_params=_cparams(("parallel",)),
    )(q, k, k, k, v, v, v, sinks)


def _attn_bwd(q, k, v, v_col, sinks, o, lse, do, *, nblk, name):
    lp = k.shape[0]

    def body(q_ref, km_ref, kp_ref, kc_ref, vm_ref, vp_ref, vc_ref, s_ref, o_ref, lse_ref, do_ref,
             dq_ref, dk_ref, dv_ref, ds_ref):
        i = pl.program_id(0)

        @pl.when(i == 0)
        def _():
            dk_ref[...] = jnp.zeros_like(dk_ref)
            dv_ref[...] = jnp.zeros_like(dv_ref)
            ds_ref[...] = jnp.zeros_like(ds_ref)

        lane = lax.broadcasted_iota(jnp.int32, (1, Q_HEADS), 1)
        prev_rows = pl.ds(pl.multiple_of(i * BLOCK, BLOCK), BLOCK)
        cur_rows = pl.ds(pl.multiple_of((i + 1) * BLOCK, BLOCK), BLOCK)
        q_all, o_all, do_all, lse_all = q_ref[...], o_ref[...], do_ref[...], lse_ref[...]
        for g in range(KV_HEADS):
            heads, qg, kcat, vcat, sink, s = _attn_group(i, g, q_all, (km_ref, kp_ref, kc_ref),
                                                         (vm_ref, vp_ref, vc_ref), s_ref)
            stack = lambda x: jnp.concatenate([x[:, h * HEAD_DIM:(h + 1) * HEAD_DIM] for h in heads], axis=0)
            lse_g = jnp.concatenate([lse_all[:, h:h + 1] for h in heads], axis=0)
            do_g = stack(do_all)
            p = jnp.exp(s - lse_g)
            delta = jnp.sum(do_g * stack(o_all), axis=-1, keepdims=True)
            dp = lax.dot_general(do_g.astype(BF16), vcat, (((1,), (1,)), ((), ())), preferred_element_type=F32)
            dsc = (p * (dp - delta) * (HEAD_DIM ** -0.5)).astype(BF16)
            dq = jnp.dot(dsc, kcat, preferred_element_type=F32)
            dk_all = lax.dot_general(dsc, qg, (((0,), (0,)), ((), ())), preferred_element_type=F32)
            dv_all = lax.dot_general(p.astype(BF16), do_g.astype(BF16), (((0,), (0,)), ((), ())),
                                     preferred_element_type=F32)
            cols = slice(g * HEAD_DIM, (g + 1) * HEAD_DIM)
            for ref, full in ((dk_ref, dk_all), (dv_ref, dv_all)):
                ref[0:BLOCK, cols] += full[0:BLOCK]
                ref[prev_rows, cols] += full[BLOCK:2 * BLOCK]
                ref[cur_rows, cols] += full[2 * BLOCK:]
            sink_part = jnp.exp(sink - lse_g) * delta
            for n, h in enumerate(heads):
                dq_ref[:, h * HEAD_DIM:(h + 1) * HEAD_DIM] = dq[n * BLOCK:(n + 1) * BLOCK]
                dsink = -jnp.sum(sink_part[n * BLOCK:(n + 1) * BLOCK], axis=0, keepdims=True)
                ds_ref[...] += jnp.where(lane == h, dsink, 0.0)

    qspec = pl.BlockSpec((BLOCK, Q_HEADS * HEAD_DIM), lambda i: (i, 0))
    whole = pl.BlockSpec((lp, 2 * HEAD_DIM), lambda i: (0, 0))
    return pl.pallas_call(
        body, name=name, grid=(nblk,),
        in_specs=([pl.BlockSpec((BLOCK, Q_HEADS * HEAD_DIM), lambda i: (i + 1, 0))] + _attn_specs(v_col)
                  + [qspec, pl.BlockSpec((BLOCK, Q_HEADS), lambda i: (i, 0)), qspec]),
        out_specs=[qspec, whole, whole, pl.BlockSpec((1, Q_HEADS), lambda i: (0, 0))],
        out_shape=[jax.ShapeDtypeStruct((nblk * BLOCK, Q_HEADS * HEAD_DIM), F32),
                   jax.ShapeDtypeStruct((lp, 2 * HEAD_DIM), F32), jax.ShapeDtypeStruct((lp, 2 * HEAD_DIM), F32),
                   jax.ShapeDtypeStruct((1, Q_HEADS), F32)],
        compiler_params=_cparams(("arbitrary",)),
    )(q, k, k, k, v, v, v, sinks, o, lse, do)


N_VEC = 5
VEC_ROWS = N_VEC * HEAD_DIM


def _selectors():
    t = SCAN_T
    shape = (t, 2 * LANES, RW_DIM)
    step, src, dst = [lax.broadcasted_iota(jnp.int32, shape, d) for d in range(3)]
    src = src % LANES
    spread = ((src // t == dst // HEAD_DIM) & (src % t == step)).astype(BF16)
    shape = (t, RW_DIM, LANES)
    step, src, dst = [lax.broadcasted_iota(jnp.int32, shape, d) for d in range(3)]
    collect = ((src // HEAD_DIM == dst // t) & (dst % t == step)).astype(BF16)
    return spread, collect


def _rows_to_xt(x):
    low = lax.broadcasted_iota(jnp.int32, (SCAN_T, LANES), 1) < HEAD_DIM
    pieces = []
    for m in range(RW_HEADS // 2):
        pair = x[:, m * LANES:(m + 1) * LANES]
        pieces += [jnp.where(low, pair, 0.0), jnp.where(low, pltpu.roll(pair, HEAD_DIM, 1), 0.0)]
    return jnp.concatenate(pieces, axis=0).T[:HEAD_DIM]


def _xt_to_rows(a):
    t = SCAN_T
    a_t = jnp.concatenate([a, jnp.zeros_like(a)], axis=0).T
    pairs = [a_t[2 * m * t:(2 * m + 1) * t] + pltpu.roll(a_t[(2 * m + 1) * t:(2 * m + 2) * t], HEAD_DIM, 1)
             for m in range(RW_HEADS // 2)]
    return jnp.concatenate(pairs, axis=1)


def _with_exchange(compute, n_in, n_out, n_scratch, scattered, shared, grid):
    n_sc = len(scattered)
    n_x = n_sc + len(shared)
    if n_x == 0:
        return compute

    def body(*refs):
        ins, x_in = refs[:n_in], refs[n_in:n_in + n_x]
        outs, x_out = refs[n_in + n_x:n_in + n_x + n_out], refs[n_in + n_x + n_out:n_in + 2 * n_x + n_out]
        scratch = refs[n_in + 2 * n_x + n_out:n_in + 2 * n_x + n_out + n_scratch]
        sems = refs[n_in + 2 * n_x + n_out + n_scratch:]

        first = last = True
        for d, size in enumerate(grid):
            first = first & (pl.program_id(d) == 0)
            last = last & (pl.program_id(d) == size - 1)

        @pl.when(first)
        def _():
            for cp in _exchange_copies(x_in, x_out, n_sc, *sems):
                cp.start()

        compute(*ins, *outs, *scratch)

        @pl.when(last)
        def _():
            for cp in _exchange_copies(x_in, x_out, n_sc, *sems):
                cp.wait()

    return body


def _wkv_fwd(xt, v, spread, name, shared=()):
    t_steps = SCAN_T
    nch = xt.shape[0]
    n_x = len(shared)

    def compute(xt_ref, v_ref, sel_ref, y_ref, hist_ref, st_ref):
        @pl.when(pl.program_id(0) == 0)
        def _():
            st_ref[...] = jnp.zeros_like(st_ref)

        x2 = _split2(xt_ref[0])
        st = st_ref[...]
        for j in range(t_steps):
            cols = jnp.dot(x2, sel_ref[j], preferred_element_type=F32)
            a_c, w_c, b_c, k_c, r_c = [cols[n * HEAD_DIM:(n + 1) * HEAD_DIM] for n in range(N_VEC)]
            hist_ref[j] = st
            sa = jnp.sum(st * a_c, axis=0, keepdims=True)
            st = st * w_c + b_c * sa + k_c * v_ref[j:j + 1, :]
            y_ref[j:j + 1, :] = jnp.sum(st * r_c, axis=0, keepdims=True)
        st_ref[...] = st

    return pl.pallas_call(
        _with_exchange(compute, 3, 2, 1, (), shared, (nch,)), name=name, grid=(nch,),
        in_specs=[pl.BlockSpec((1, VEC_ROWS, LANES), lambda c: (c, 0, 0)),
                  pl.BlockSpec((t_steps, RW_DIM), lambda c: (c, 0)),
                  pl.BlockSpec(spread.shape, lambda c: (0, 0, 0))] + [ANY] * n_x,
        out_specs=[pl.BlockSpec((t_steps, RW_DIM), lambda c: (c, 0)),
                   pl.BlockSpec((t_steps, HEAD_DIM, RW_DIM), lambda c: (c, 0, 0))] + [ANY] * n_x,
        out_shape=[jax.ShapeDtypeStruct((nch * t_steps, RW_DIM), F32),
                   jax.ShapeDtypeStruct((nch * t_steps, HEAD_DIM, RW_DIM), F32)] + _exchange_shapes((), shared),
        scratch_shapes=[pltpu.VMEM((HEAD_DIM, RW_DIM), F32)] + (_exchange_sems(n_x) if n_x else []),
        compiler_params=_cparams(("arbitrary",)),
    )(xt, v, spread, *shared)


def _wkv_bwd(xt, v, hist, dy, spread, collect, name, scattered=()):
    t_steps = SCAN_T
    nch = xt.shape[0]
    n_x = len(scattered)
    lead = BLOCK // t_steps

    def compute(xt_ref, v_ref, hist_ref, dy_ref, sel_ref, col_ref, dxt_ref, dv_ref, g_ref):
        @pl.when(pl.program_id(0) == 0)
        def _():
            g_ref[...] = jnp.zeros_like(g_ref)

        x2 = _split2(xt_ref[0])
        has_dy = nch - 1 - pl.program_id(0) >= lead
        gst = g_ref[...]
        acc = jnp.zeros((VEC_ROWS, LANES), F32)
        nxt = None
        for j in reversed(range(t_steps)):
            cols = jnp.dot(x2, sel_ref[j], preferred_element_type=F32)
            a_c, w_c, b_c, k_c, r_c = [cols[n * HEAD_DIM:(n + 1) * HEAD_DIM] for n in range(N_VEC)]
            prev = hist_ref[j]
            v_row = v_ref[j:j + 1, :]
            dy_row = jnp.where(has_dy, dy_ref[j:j + 1, :], 0.0)
            sa = jnp.sum(prev * a_c, axis=0, keepdims=True)
            if nxt is None:
                nxt = prev * w_c + b_c * sa + k_c * v_row
            gst = gst + r_c * dy_row
            dv_ref[j:j + 1, :] = jnp.sum(gst * k_c, axis=0, keepdims=True)
            dsa = jnp.sum(gst * b_c, axis=0, keepdims=True)
            prods = jnp.concatenate([p.astype(BF16) for p in
                                     (prev * dsa, gst * prev, gst * sa, gst * v_row, nxt * dy_row)], axis=0)
            acc = acc + jnp.dot(prods, col_ref[j], preferred_element_type=F32)
            gst = gst * w_c + a_c * dsa
            nxt = prev
        dxt_ref[0] = acc
        g_ref[...] = gst

    rev3 = lambda c: (nch - 1 - c, 0, 0)
    rev2 = lambda c: (nch - 1 - c, 0)
    rowspec = pl.BlockSpec((t_steps, RW_DIM), rev2)
    return pl.pallas_call(
        _with_exchange(compute, 6, 2, 1, scattered, (), (nch,)), name=name, grid=(nch,),
        in_specs=[pl.BlockSpec((1, VEC_ROWS, LANES), rev3), rowspec,
                  pl.BlockSpec((t_steps, HEAD_DIM, RW_DIM), rev3),
                  pl.BlockSpec((t_steps, RW_DIM), lambda c: (jnp.maximum(nch - 1 - c - lead, 0), 0)),
                  pl.BlockSpec(spread.shape, lambda c: (0, 0, 0)),
                  pl.BlockSpec(collect.shape, lambda c: (0, 0, 0))] + [ANY] * n_x,
        out_specs=[pl.BlockSpec((1, VEC_ROWS, LANES), rev3), rowspec] + [ANY] * n_x,
        out_shape=[jax.ShapeDtypeStruct((nch, VEC_ROWS, LANES), F32),
                   jax.ShapeDtypeStruct((nch * t_steps, RW_DIM), F32)] + _exchange_shapes(scattered, ()),
        scratch_shapes=[pltpu.VMEM((HEAD_DIM, RW_DIM), F32)] + (_exchange_sems(n_x) if n_x else []),
        compiler_params=_cparams(("arbitrary",)),
    )(xt, v, hist, dy, spread, collect, *scattered)


MESH = pl.DeviceIdType.MESH
ANY = pl.BlockSpec(memory_space=pl.ANY)


def _all_gather(arrays, name):
    n_arr = len(arrays)
    per = N_DEV - 1

    def body(*refs):
        x_refs, out_refs = refs[:n_arr], refs[n_arr:2 * n_arr]
        send_sems, recv_sems, local_sems = refs[2 * n_arr:]
        xi, yi, ci = lax.axis_index("x"), lax.axis_index("y"), lax.axis_index("c")
        me, sibling = (xi, yi, ci), (xi, yi, 1 - ci)
        chips = [(1 - xi, yi), (xi, 1 - yi), (1 - xi, 1 - yi)]

        def slot(a, px, py, pc):
            return out_refs[a].at[4 * px + 2 * py + pc]

        def copy(a, sem, block, to, src=None):
            return pltpu.make_async_remote_copy(
                src_ref=slot(a, *block) if src is None else src, dst_ref=slot(a, *block),
                send_sem=send_sems.at[per * a + sem], recv_sem=recv_sems.at[per * a + sem],
                device_id=to, device_id_type=MESH)

        mine = [pltpu.make_async_copy(x_refs[a], slot(a, *me), local_sems.at[a]) for a in range(n_arr)]
        for cp in mine:
            cp.start()
        sent = []
        for a in range(n_arr):
            sent.append(copy(a, 0, me, sibling, src=x_refs[a]))
            sent += [copy(a, 1 + j, me, (*chip, ci), src=x_refs[a]) for j, chip in enumerate(chips)]
        for cp in sent:
            cp.start()
        for j, chip in enumerate(chips):
            for a in range(n_arr):
                copy(a, 1 + j, (*chip, ci), me).wait_recv()
                onward = copy(a, 4 + j, (*chip, ci), sibling)
                onward.start()
                sent.append(onward)
        for a in range(n_arr):
            copy(a, 0, sibling, me).wait_recv()
        for j, chip in enumerate(chips):
            for a in range(n_arr):
                copy(a, 4 + j, (*chip, 1 - ci), me).wait_recv()
        for cp in sent:
            cp.wait_send()
        for cp in mine:
            cp.wait()

    sems = pltpu.SemaphoreType.DMA((per * n_arr,))
    return pl.pallas_call(
        body, name=name, out_shape=[jax.ShapeDtypeStruct((N_DEV,) + a.shape, a.dtype) for a in arrays],
        in_specs=[ANY] * n_arr, out_specs=[ANY] * n_arr,
        scratch_shapes=[sems, sems, pltpu.SemaphoreType.DMA((n_arr,))],
    )(*arrays)


def _exchange_copies(in_refs, out_refs, n_scattered, send_sems, recv_sems, local_sems):
    n_arr = len(in_refs)
    per = N_DEV - 1
    xi, yi, ci = lax.axis_index("x"), lax.axis_index("y"), lax.axis_index("c")
    me = 4 * xi + 2 * yi + ci
    src_of = lambda a, peer: in_refs[a].at[peer] if a < n_scattered else in_refs[a]
    copies = []
    for d in range(1, N_DEV):
        px = 1 - xi if d & 4 else xi
        py = 1 - yi if d & 2 else yi
        pc = 1 - ci if d & 1 else ci
        for a in range(n_arr):
            copies.append(pltpu.make_async_remote_copy(
                src_ref=src_of(a, 4 * px + 2 * py + pc), dst_ref=out_refs[a].at[me],
                send_sem=send_sems.at[per * a + d - 1], recv_sem=recv_sems.at[per * a + d - 1],
                device_id=(px, py, pc), device_id_type=MESH))
    own = [pltpu.make_async_copy(src_of(a, me), out_refs[a].at[me], local_sems.at[a]) for a in range(n_arr)]
    return copies + own


def _exchange_shapes(scattered, shared):
    return ([jax.ShapeDtypeStruct(a.shape, a.dtype) for a in scattered]
            + [jax.ShapeDtypeStruct((N_DEV,) + a.shape, a.dtype) for a in shared])


def _exchange_sems(n_arr):
    sems = pltpu.SemaphoreType.DMA(((N_DEV - 1) * n_arr,))
    return [sems, sems, pltpu.SemaphoreType.DMA((n_arr,))]


def _exchange(scattered, shared, name):
    n_sc = len(scattered)
    n_arr = n_sc + len(shared)

    def body(*refs):
        copies = _exchange_copies(refs[:n_arr], refs[n_arr:2 * n_arr], n_sc, *refs[2 * n_arr:])
        for cp in copies:
            cp.start()
        for cp in copies:
            cp.wait()

    return pl.pallas_call(
        body, name=name, out_shape=_exchange_shapes(scattered, shared),
        in_specs=[ANY] * n_arr, out_specs=[ANY] * n_arr, scratch_shapes=_exchange_sems(n_arr),
    )(*scattered, *shared)


def _adamw(parts, w, m, v, name):
    rows, cols = w.shape
    tile = PACK_ROWS if rows % PACK_ROWS == 0 else rows

    def body(p_ref, w_ref, m_ref, v_ref, g_out, d_out, m_out, v_out):
        g = p_ref[0].astype(F32)
        for s in range(1, N_DEV):
            g = g + p_ref[s].astype(F32)
        m_new = ADAM_B1 * m_ref[...] + (1.0 - ADAM_B1) * g
        v_new = ADAM_B2 * v_ref[...] + (1.0 - ADAM_B2) * (g * g)
        m_hat = m_new / (1.0 - ADAM_B1 ** ADAM_STEP)
        v_hat = v_new / (1.0 - ADAM_B2 ** ADAM_STEP)
        g_out[...] = g
        d_out[...] = -ADAM_LR * (m_hat / (jnp.sqrt(v_hat) + ADAM_EPS) + ADAM_WD * w_ref[...])
        m_out[...] = m_new
        v_out[...] = v_new

    spec = pl.BlockSpec((tile, cols), lambda i: (i, 0))
    return pl.pallas_call(
        body, name=name, grid=(rows // tile,),
        in_specs=[pl.BlockSpec((N_DEV, tile, cols), lambda i: (0, i, 0)), spec, spec, spec],
        out_specs=[spec] * 4, out_shape=[jax.ShapeDtypeStruct((rows, cols), F32)] * 4,
        compiler_params=_cparams(("parallel",)),
    )(parts, w, m, v)


EARLY = [("meta_tokens", 1), ("rwkv_w2", 1), ("rwkv_a2", 1), ("rwkv_g2", 1)]
LATE = [("w_br_attn", 1), ("w_br_rwkv", 1), ("w_o", 0), ("w_ffn_gate", 1), ("w_ffn_up", 1), ("w_ffn_down", 0)]
REPLICATED = ["norm_mix_g", "b_in", "attn_sinks", "rwkv_mix", "rwkv_w0", "rwkv_a0", "rwkv_k_k", "rwkv_k_a",
              "rwkv_r_k", "rwkv_ln_w", "rwkv_ln_b", "norm_ffn_g", "norm_final_g"]
WEIGHTS = ["meta_tokens", "norm_mix_g", "w_in", "b_in", "attn_sinks", "rwkv_mix", "rwkv_w0", "rwkv_w2", "rwkv_a0",
           "rwkv_a2", "rwkv_g2", "rwkv_k_k", "rwkv_k_a", "rwkv_r_k", "rwkv_ln_w", "rwkv_ln_b", "w_br_attn",
           "w_br_rwkv", "w_o", "norm_ffn_g", "w_ffn_gate", "w_ffn_up", "w_ffn_down", "norm_final_g"]


def _pack(arrays, row_multiple):
    flat = jnp.concatenate([a.reshape(-1) for a in arrays])
    per = row_multiple * LANES
    total = -(-flat.shape[0] // per) * per
    return jnp.pad(flat, (0, total - flat.shape[0])).reshape(-1, LANES)


def _unpack(buf, shapes):
    flat = buf.reshape(-1)
    out, pos = [], 0
    for s in shapes:
        n = 1
        for d in s:
            n *= d
        out.append(flat[pos:pos + n].reshape(s))
        pos += n
    return out


def _strip(name, a):
    return a if name in ("meta_tokens", "norm_final_g") else a[0]


def _join(gathered, axis):
    if axis == 0:
        return gathered.reshape(-1, gathered.shape[2])
    return gathered.transpose(1, 0, 2).reshape(gathered.shape[1], -1)


def _split(g, axis):
    if axis == 0:
        return g.reshape(N_DEV, -1, g.shape[1])
    return g.reshape(g.shape[0], N_DEV, -1).transpose(1, 0, 2)


W_IN_LAYOUT = [(768, 2304), (0, 512), (2592, 4640), (2432, 2592), 256 - GATE_LORA, (512, 768), (2304, 2368),
               128 - DECAY_LORA, (2368, 2432), 128 - AAA_LORA, NP - C_DA - 128]


def _w_in_padded(w, shard_width=None):
    rows = w.shape[-2]
    width = D_IN if shard_width is None else shard_width
    parts = []
    for seg in W_IN_LAYOUT:
        if isinstance(seg, int):
            parts.append(jnp.zeros((rows, seg), w.dtype))
            continue
        lo, stop = seg
        while lo < stop:
            p = lo // width
            hi = min(stop, (p + 1) * width)
            src = w if shard_width is None else w[p]
            parts.append(src[:, lo - p * width:hi - p * width])
            lo = hi
    return jnp.concatenate(parts, axis=1)


def _w_in_unpadded(wp, lo=0, stop=D_IN):
    spans, pos = [], 0
    for seg in W_IN_LAYOUT:
        if isinstance(seg, int):
            pos += seg
        else:
            spans.append((seg[0], seg[1], pos))
            pos += seg[1] - seg[0]
    parts = []
    for a, b, at in sorted(spans):
        c, d = max(a, lo), min(b, stop)
        if c < d:
            parts.append(wp[:, at + c - a:at + d - a])
    return jnp.concatenate(parts, axis=1)


def _pad_rows(a, n):
    return jnp.pad(a, ((0, n - a.shape[0]), (0, 0)))


def _device_step(x, tgt, full, gather_late=None, scatter_early=None, scatter_last=None):
    seq = x.shape[0]
    nblk = seq // BLOCK
    lp = seq + BLOCK
    nall = nblk + 1

    w_in_p = full["w_in_p"]
    b_in_p = _w_in_padded(full["b_in"][None])
    mix = full["rwkv_mix"][None]
    mix_r, mix_k, mix_v = mix[:, 0:512], mix[:, 512:1024], mix[:, 1024:1536]
    mix_dw = jnp.pad(mix[:, 1536:1600], ((0, 0), (0, 64)))
    mix_da = jnp.pad(mix[:, 1600:1664], ((0, 0), (0, 64)))
    mix_dg = jnp.pad(mix[:, 1664:1824], ((0, 0), (0, 96)))
    w2_p = _pad_rows(full["rwkv_w2"].astype(F32), 128)
    a2_p = _pad_rows(full["rwkv_a2"].astype(F32), 128)
    g2_p = _pad_rows(full["rwkv_g2"].astype(F32), 256)
    row = lambda name: full[name].reshape(1, -1)
    sinks = row("attn_sinks")
    rope_c, rope_s1, rope_s2 = _rope_tables(lp)

    hpad = jnp.concatenate([jnp.zeros((PAD_ROWS, D_MODEL), F32), full["meta_tokens"].astype(F32), x], axis=0)
    (u,) = _rows_fwd(_rms_fn, [_view(hpad)], [row("norm_mix_g")], [D_MODEL], nblk=nall, name="norm_mix",
                     out_dtype=BF16)
    proj = _mm(u, w_in_p, bias=b_in_p, name="in_proj")
    (q_r,) = _rows_fwd(_rope_fwd_fn, [_view(proj, 512, C_Q // 512), _view(rope_c), _view(rope_s1), _view(rope_s2)],
                       [], [512], nblk=nall, name="rope_q")
    (k_r,) = _rows_fwd(_rope_fwd_fn, [_view(proj, 128, C_KA // 128), _view(rope_c), _view(rope_s1),
                                      _view(rope_s2)], [], [128], nblk=nall, name="rope_k")
    o_attn, lse = _attn_fwd(q_r, k_r, proj, C_VA // 128, sinks, nblk=nblk, name="attn_fwd")

    rw_cols = jnp.concatenate([proj[:, C_R:C_R + 1536], proj[:, C_DG:C_DG + 256], proj[:, C_DW:C_DW + 256]], axis=1)
    rw_prev = jnp.pad(rw_cols[:-1], ((1, 0), (0, 0)))
    pre_rows = [_view(proj, 512, 0), _view(proj, 512, 1), _view(proj, 512, 2), _view(proj, 128, C_DW // 128),
                _view(proj, 128, C_DA // 128), _view(proj, 256, C_DG // 256),
                _view(rw_prev, 512, 0), _view(rw_prev, 512, 1), _view(rw_prev, 512, 2), _view(rw_prev, 128, 14),
                _view(rw_prev, 128, 15), _view(rw_prev, 256, 6)]
    pre_consts = [mix_r, mix_k, mix_v, mix_dw, mix_da, mix_dg, row("rwkv_w0"), w2_p, row("rwkv_a0"), a2_p, g2_p,
                  row("rwkv_k_k"), row("rwkv_k_a")]
    xt_block = (BLOCK // SCAN_T * VEC_ROWS, LANES)
    r_t, k_mod, v_t, gate, xt = _rows_fwd(_rwkv_pre_xt_fn, pre_rows, pre_consts, [RW_DIM] * 4 + [xt_block],
                                          nblk=nall, name="rwkv_pre")
    xt = xt.reshape(-1, VEC_ROWS, LANES)
    spread, collect = _selectors()
    y_scan, hist, *late = _wkv_fwd(xt, v_t, spread, name="wkv_fwd", shared=gather_late[0] if gather_late else ())
    if gather_late:
        full = {**full, **gather_late[1](late)}
    post_rows = [_view(y_scan, off=1), _view(r_t, off=1), _view(k_mod, off=1), _view(v_t, off=1), _view(gate, off=1)]
    post_consts = [row("rwkv_ln_w"), row("rwkv_ln_b"), row("rwkv_r_k")]
    (y_rwkv,) = _rows_fwd(_rwkv_post_fn, post_rows, post_consts, [RW_DIM], nblk=nblk, name="rwkv_post",
                          out_dtype=BF16)

    ya = _mm(o_attn, full["w_br_attn"], name="br_attn")
    yr = _mm(y_rwkv, full["w_br_rwkv"], name="br_rwkv")
    merge_rows = [_view(ya), _view(yr), _view(proj, 1024, C_G1 // 1024, 1), _view(proj, 1024, C_G2 // 1024, 1)]
    (merged,) = _rows_fwd(_merge_fn, merge_rows, [], [D_MODEL], nblk=nblk, name="merge", out_dtype=BF16)
    h1 = _mm(merged, full["w_o"], residual=x, name="out_proj")
    (f,) = _rows_fwd(_rms_fn, [_view(h1)], [row("norm_ffn_g")], [D_MODEL], nblk=nblk, name="norm_ffn",
                     out_dtype=BF16)
    ff_gate = _mm(f, full["w_ffn_gate"], name="ffn_gate")
    ff_up = _mm(f, full["w_ffn_up"], name="ffn_up")
    (act,) = _rows_fwd(_swiglu_fn, [_view(ff_gate), _view(ff_up)], [], [D_FF], nblk=nblk, name="swiglu",
                       out_dtype=BF16)
    h2 = _mm(act, full["w_ffn_down"], residual=h1, name="ffn_down")

    grads = {}
    ones_col = jnp.ones((seq, 1), F32)
    loss_rows, dh2, grads["norm_final_g"] = _rows_bwd(
        _loss_fn, [_view(h2), _view(tgt)], [row("norm_final_g")], [_view(ones_col)], nblk=nblk, name="loss",
        diff_rows=[0], diff_consts=[0], fwd_widths=[1])
    loss = jnp.sum(loss_rows)

    dact = _mm(dh2, full["w_ffn_down"], tb=True, name="d_act")
    grads["w_ffn_down"] = _mm(act, dh2, ta=True, name="dw_ffn_down")
    dgate, dup = _rows_bwd(_swiglu_fn, [_view(ff_gate), _view(ff_up)], [], [_view(dact)], nblk=nblk,
                           name="swiglu_bwd", diff_rows=[0, 1], diff_consts=[], row_dtype=BF16)
    grads["w_ffn_gate"] = _mm(f, dgate, ta=True, name="dw_ffn_gate")
    grads["w_ffn_up"] = _mm(f, dup, ta=True, name="dw_ffn_up")
    df = _mm(dgate, full["w_ffn_gate"], tb=True, name="df_gate")
    df = _mm(dup, full["w_ffn_up"], tb=True, residual=df, name="df_up")
    dh1, grads["norm_ffn_g"] = _rows_bwd(_rms_fn, [_view(h1)], [row("norm_ffn_g")], [_view(df)], nblk=nblk,
                                         name="norm_ffn_bwd", diff_rows=[0], diff_consts=[0], acc=[_view(dh2)])
    dmerged = _mm(dh1, full["w_o"], tb=True, name="d_merged")
    grads["w_o"] = _mm(merged, dh1, ta=True, name="dw_o")
    dya, dyr, dg1, dg2 = _rows_bwd(_merge_fn, merge_rows, [], [_view(dmerged)], nblk=nblk, name="merge_bwd",
                                   diff_rows=[0, 1, 2, 3], diff_consts=[], row_dtype=BF16)
    grads["w_br_attn"] = _mm(o_attn, dya, ta=True, name="dw_br_attn")
    grads["w_br_rwkv"] = _mm(y_rwkv, dyr, ta=True, name="dw_br_rwkv")
    dy_attn = _mm(dya, full["w_br_attn"], tb=True, name="d_y_attn")
    dy_rwkv = _mm(dyr, full["w_br_rwkv"], tb=True, name="d_y_rwkv")

    post = _rows_bwd(_rwkv_post_fn, post_rows, post_consts, [_view(dy_rwkv)], nblk=nblk, name="rwkv_post_bwd",
                     diff_rows=[0, 1, 2, 3, 4], diff_consts=[0, 1, 2])
    dys, dr_post, dk_post, dv_post, dgate_post = post[:5]
    grads["rwkv_ln_w"], grads["rwkv_ln_b"], grads["rwkv_r_k"] = post[5:]
    dxt, dv_s, *early_parts = _wkv_bwd(xt, v_t, hist, dys, spread, collect, name="wkv_bwd",
                                       scattered=scatter_early(grads) if scatter_early else ())
    pre_cts = [_view(dxt.reshape(-1, LANES), rows=xt_block[0]), _view(dv_s)] + [
        _view(t, off=-1) for t in (dr_post, dk_post, dv_post, dgate_post)]
    pre = _rows_bwd(_rwkv_pre_fn, pre_rows, pre_consts, pre_cts, nblk=nall, name="rwkv_pre_bwd",
                    diff_rows=list(range(12)), diff_consts=list(range(13)), ct_map=_rwkv_pre_cts)
    d_cur, d_prev, d_par = pre[0:6], pre[6:12], pre[12:]
    up = lambda t: jnp.pad(t[1:], ((0, 1), (0, 0)))
    d_rw = [c + up(p) for c, p in zip(d_cur, d_prev)]
    grads["rwkv_mix"] = jnp.concatenate([d_par[0], d_par[1], d_par[2], d_par[3][:, :DECAY_LORA],
                                         d_par[4][:, :AAA_LORA], d_par[5][:, :GATE_LORA]], axis=1)
    grads["rwkv_w0"], grads["rwkv_w2"] = d_par[6], d_par[7][:DECAY_LORA]
    grads["rwkv_a0"], grads["rwkv_a2"] = d_par[8], d_par[9][:AAA_LORA]
    grads["rwkv_g2"] = d_par[10][:GATE_LORA]
    grads["rwkv_k_k"], grads["rwkv_k_a"] = d_par[11], d_par[12]

    dq_real, dk_r, dva, grads["attn_sinks"] = _attn_bwd(q_r, k_r, proj, C_VA // 128, sinks, o_attn, lse, dy_attn,
                                                        nblk=nblk, name="attn_bwd")
    dq_r = jnp.pad(dq_real, ((BLOCK, 0), (0, 0)))
    (dq,) = _rows_fwd(_rope_bwd_fn, [_view(dq_r), _view(rope_c), _view(rope_s1), _view(rope_s2)], [], [512],
                      nblk=nall, name="rope_q_bwd", out_dtype=BF16)
    (dka,) = _rows_fwd(_rope_bwd_fn, [_view(dk_r), _view(rope_c), _view(rope_s1),
                                      _view(rope_s2)], [], [128], nblk=nall, name="rope_k_bwd", out_dtype=BF16)

    lead = lambda t: jnp.pad(t, ((BLOCK, 0), (0, 0)))
    pieces = [d_rw[0], d_rw[1], d_rw[2], dq, lead(dg1), lead(dg2), d_rw[5], dka, dva, d_rw[3], d_rw[4],
              jnp.zeros((lp, NP - C_DA - 128), BF16)]
    dproj = jnp.concatenate([p.astype(BF16) for p in pieces], axis=1)
    grads["w_in_p"] = _mm(u, dproj, ta=True, name="dw_in")
    grads["b_in"] = _w_in_unpadded(_colsum(dproj, name="db_in"))
    du, *last_parts = _mm(dproj, w_in_p, tb=True, name="d_u", scattered=scatter_last(grads)) if scatter_last else (
        _mm(dproj, w_in_p, tb=True, name="d_u"),)
    dh, grads["norm_mix_g"] = _rows_bwd(_rms_fn, [_view(hpad)], [row("norm_mix_g")], [_view(du)], nblk=nall,
                                        name="norm_mix_bwd", diff_rows=[0], diff_consts=[0], acc=[_view(lead(dh1))])
    grads["meta_tokens"] = dh[PAD_ROWS:BLOCK]
    return loss, dh[BLOCK:], grads, early_parts, last_parts


def kernel(x, meta_tokens, norm_mix_g, w_in, b_in, attn_sinks, rwkv_mix, rwkv_w0, rwkv_w2, rwkv_a0, rwkv_a2, rwkv_g2, rwkv_k_k, rwkv_k_a, rwkv_r_k, rwkv_ln_w, rwkv_ln_b, w_br_attn, w_br_rwkv, w_o, norm_ffn_g, w_ffn_gate, w_ffn_up, w_ffn_down, norm_final_g, loss_target, m_meta_tokens, m_norm_mix_g, m_w_in, m_b_in, m_attn_sinks, m_rwkv_mix, m_rwkv_w0, m_rwkv_w2, m_rwkv_a0, m_rwkv_a2, m_rwkv_g2, m_rwkv_k_k, m_rwkv_k_a, m_rwkv_r_k, m_rwkv_ln_w, m_rwkv_ln_b, m_w_br_attn, m_w_br_rwkv, m_w_o, m_norm_ffn_g, m_w_ffn_gate, m_w_ffn_up, m_w_ffn_down, m_norm_final_g, v_meta_tokens, v_norm_mix_g, v_w_in, v_b_in, v_attn_sinks, v_rwkv_mix, v_rwkv_w0, v_rwkv_w2, v_rwkv_a0, v_rwkv_a2, v_rwkv_g2, v_rwkv_k_k, v_rwkv_k_a, v_rwkv_r_k, v_rwkv_ln_w, v_rwkv_ln_b, v_w_br_attn, v_w_br_rwkv, v_w_o, v_norm_ffn_g, v_w_ffn_gate, v_w_ffn_up, v_w_ffn_down, v_norm_final_g):
    given = dict(locals())
    wts = {n: _strip(n, given[n]) for n in WEIGHTS}
    mom = {n: _strip(n, given["m_" + n]) for n in WEIGHTS}
    var = {n: _strip(n, given["v_" + n]) for n in WEIGHTS}
    small_shapes = [wts[n].shape for n in REPLICATED]
    width = wts["w_in"].shape[1]
    wire = lambda table: [wts[n].astype(BF16) for n, _ in table]

    w_in_all, *early_all = _all_gather([wts["w_in"].astype(BF16)] + wire(EARLY), name="gather_weights")
    full = {n: wts[n] for n in REPLICATED}
    full.update({n: _join(g, axis) for (n, axis), g in zip(EARLY, early_all)})
    full["w_in_p"] = _w_in_padded(w_in_all, shard_width=width)
    gather_late = (wire(LATE), lambda got: {n: _join(g, axis) for (n, axis), g in zip(LATE, got)})
    scatter_early = lambda g: [_split(g[n], axis).astype(BF16) for n, axis in LATE]
    scatter_last = lambda g: [jnp.stack([_w_in_unpadded(g["w_in_p"], p * width, (p + 1) * width)
                                         for p in range(N_DEV)]).astype(BF16)]

    loss_part, grad_x, grads, parts_late, (parts_w_in,) = _device_step(
        x[0], loss_target[0], full, gather_late, scatter_early, scatter_last)

    g_early = [_split(grads[n], axis).astype(BF16) for n, axis in EARLY]
    zero = jnp.zeros((1,), F32)
    g_small = _pack([grads[n].reshape(wts[n].shape) for n in REPLICATED] + [loss_part.reshape(1)], 8)
    *parts_early, parts_small = _exchange(g_early, [g_small], name="exchange_grads")
    results = [{}, {}, {}, {}]
    for (n, _), parts in zip([("w_in", 1)] + EARLY + LATE, [parts_w_in] + parts_early + list(parts_late)):
        for kind, a in enumerate(_adamw(parts, wts[n], mom[n], var[n], name="adamw_" + n)):
            results[kind][n] = a
    small = _adamw(parts_small, _pack([wts[n] for n in REPLICATED] + [zero], 8),
                   _pack([mom[n] for n in REPLICATED] + [zero], 8), _pack([var[n] for n in REPLICATED] + [zero], 8),
                   name="adamw_replicated")
    for kind in range(4):
        for n, a in zip(REPLICATED, _unpack(small[kind], small_shapes)):
            results[kind][n] = a
    loss = _unpack(small[0], small_shapes + [(1,)])[-1][0]
    out = [loss, grad_x[None]]
    for kind in range(4):
        out += [results[kind][n].reshape(given[n].shape) for n in WEIGHTS]
    return tuple(out)
```

```python
import functools

import jax
import jax.numpy as jnp
from jax import lax
from jax.experimental import pallas as pl
from jax.experimental.pallas import tpu as pltpu

F32 = jnp.float32
BF16 = jnp.bfloat16

N_DEV = 8
D_MODEL = 1024
N_META = 16
BLOCK = 128
PAD_ROWS = BLOCK - N_META
HEAD_DIM = 64
Q_HEADS = 8
KV_HEADS = 2
GROUP = Q_HEADS // KV_HEADS
ROPE_DIM = HEAD_DIM // 4
ROPE_HALF = ROPE_DIM // 2
ROPE_THETA = 500000.0
RW_HEADS = 8
RW_DIM = 512
DECAY_LORA = 64
AAA_LORA = 64
GATE_LORA = 160
D_FF = 2816
D_IN = 4640
RMS_EPS = 1e-6
RWKV_LN_EPS = 64e-5
NEG_INF = -1e30
SCAN_T = 16
SCAN_CHUNKS = 4
LANES = 128
PACK_ROWS = 256

ADAM_LR = 0.001
ADAM_B1 = 0.9
ADAM_B2 = 0.999
ADAM_EPS = 1e-08
ADAM_WD = 0.01
ADAM_STEP = 10

C_R, C_K, C_V, C_Q = 0, 512, 1024, 1536
C_G1, C_G2 = 2048, 3072
C_DG, C_KA, C_VA, C_DW, C_DA = 4096, 4352, 4480, 4608, 4736
NP = 5120

VMEM_LIMIT = 48 * 1024 * 1024


def _cparams(sem):
    return pltpu.CompilerParams(dimension_semantics=sem, vmem_limit_bytes=VMEM_LIMIT)


def _pick(n, cands):
    for c in cands:
        if n % c == 0:
            return c
    raise ValueError(f"no tile for {n}")


def _mm(a, b, *, ta=False, tb=False, bias=None, residual=None, name, scattered=()):
    m = a.shape[1] if ta else a.shape[0]
    k = a.shape[0] if ta else a.shape[1]
    n = b.shape[0] if tb else b.shape[1]
    assert k == (b.shape[1] if tb else b.shape[0]), (a.shape, b.shape, ta, tb)
    tm = _pick(m, (512, 1408, 256, 128) if ta else (1024, 528, 512, 384, 256, 128))
    tn = _pick(n, (1024, 512, 1408, 256, 128))
    if k <= 1024:
        tk = k
    else:
        tk = _pick(k, (1024, 1056, 528, 512) if (ta and not tb) else (1024, 1408, 512, 256, 128))
    nk = k // tk
    has_bias = bias is not None
    has_res = residual is not None
    dn = (((0 if ta else 1,), (1 if tb else 0,)), ((), ()))

    def body(*refs):
        a_ref, b_ref = refs[0], refs[1]
        pos = 2
        bias_ref = res_ref = None
        if has_bias:
            bias_ref = refs[pos]
            pos += 1
        if has_res:
            res_ref = refs[pos]
            pos += 1
        o_ref, acc_ref = refs[pos], refs[pos + 1]
        kk = pl.program_id(2)
        part = lax.dot_general(a_ref[...].astype(BF16), b_ref[...].astype(BF16), dn, preferred_element_type=F32)

        def finish(out):
            if has_bias:
                out = out + bias_ref[...]
            if has_res:
                out = out + res_ref[...]
            o_ref[...] = out

        if nk == 1:
            finish(part)
        else:
            @pl.when(kk == 0)
            def _():
                acc_ref[...] = part

            @pl.when((kk > 0) & (kk < nk - 1))
            def _():
                acc_ref[...] += part

            @pl.when(kk == nk - 1)
            def _():
                finish(acc_ref[...] + part)

    in_specs = [
        pl.BlockSpec((tk, tm), lambda i, j, kk: (kk, i)) if ta else pl.BlockSpec((tm, tk), lambda i, j, kk: (i, kk)),
        pl.BlockSpec((tn, tk), lambda i, j, kk: (j, kk)) if tb else pl.BlockSpec((tk, tn), lambda i, j, kk: (kk, j)),
    ]
    args = [a, b]
    if has_bias:
        in_specs.append(pl.BlockSpec((1, tn), lambda i, j, kk: (0, j)))
        args.append(bias)
    if has_res:
        in_specs.append(pl.BlockSpec((tm, tn), lambda i, j, kk: (i, j)))
        args.append(residual)
    grid = (m // tm, n // tn, nk)
    n_x = len(scattered)
    out = pl.pallas_call(
        _with_exchange(body, len(args), 1, 1, scattered, (), grid), name=name, grid=grid,
        in_specs=in_specs + [ANY] * n_x,
        out_specs=[pl.BlockSpec((tm, tn), lambda i, j, kk: (i, j))] + [ANY] * n_x,
        out_shape=[jax.ShapeDtypeStruct((m, n), F32)] + _exchange_shapes(scattered, ()),
        scratch_shapes=[pltpu.VMEM((tm, tn) if nk > 1 else (8, LANES), F32)] + (_exchange_sems(n_x) if n_x else []),
        compiler_params=_cparams(("arbitrary",) * 3 if n_x else ("parallel", "parallel", "arbitrary")),
    )(*args, *scattered)
    return out if n_x else out[0]


def _colsum(x, name):
    m, n = x.shape
    tm = BLOCK

    def body(x_ref, o_ref):
        i = pl.program_id(0)
        s = jnp.sum(x_ref[...].astype(F32), axis=0, keepdims=True)

        @pl.when(i == 0)
        def _():
            o_ref[...] = s

        @pl.when(i > 0)
        def _():
            o_ref[...] += s

    return pl.pallas_call(
        body, name=name, grid=(m // tm,),
        in_specs=[pl.BlockSpec((tm, n), lambda i: (i, 0))],
        out_specs=pl.BlockSpec((1, n), lambda i: (0, 0)),
        out_shape=jax.ShapeDtypeStruct((1, n), F32),
        compiler_params=_cparams(("arbitrary",)),
    )(x)


def _view(arr, width=None, col=0, off=0, rows=BLOCK):
    return (arr, arr.shape[1] if width is None else width, col, off, rows)


def _row_spec(view):
    _, width, col, off, rows = view
    if off < 0:
        return pl.BlockSpec((rows, width), lambda i, col=col, off=off: (jnp.maximum(i + off, 0), col))
    return pl.BlockSpec((rows, width), lambda i, col=col, off=off: (i + off, col))


def _const_spec(arr):
    return pl.BlockSpec(arr.shape, lambda i: (0,) * arr.ndim)


def _rows_fwd(fn, rows, consts, out_widths, *, nblk, name, out_dtype=F32):
    nr, nc = len(rows), len(consts)
    out_blocks = [(BLOCK, w) if isinstance(w, int) else w for w in out_widths]

    def body(*refs):
        i = pl.program_id(0)
        vals = [r[...] for r in refs[:nr + nc]]
        outs = fn(i, *vals)
        for o_ref, o in zip(refs[nr + nc:], outs):
            o_ref[...] = o.astype(o_ref.dtype)

    return pl.pallas_call(
        body, name=name, grid=(nblk,),
        in_specs=[_row_spec(v) for v in rows] + [_const_spec(c) for c in consts],
        out_specs=[pl.BlockSpec(b, lambda i: (i, 0)) for b in out_blocks],
        out_shape=[jax.ShapeDtypeStruct((nblk * r, w), out_dtype) for r, w in out_blocks],
        compiler_params=_cparams(("parallel",)),
    )(*[v[0] for v in rows], *consts)


def _rows_bwd(fn, rows, consts, cts, *, nblk, name, diff_rows, diff_consts, acc=None, fwd_widths=(), row_dtype=F32,
              ct_map=None):
    nr, nc = len(rows), len(consts)
    ct_views = [c for c in cts if c is not None]
    acc = acc or [None] * len(diff_rows)
    acc_views = [a for a in acc if a is not None]
    n_in = nr + nc + len(ct_views) + len(acc_views)
    n_fwd = len(fwd_widths)

    def body(*refs):
        i = pl.program_id(0)
        row_vals = [r[...] for r in refs[:nr]]
        const_vals = [r[...] for r in refs[nr:nr + nc]]
        ct_vals = [r[...] for r in refs[nr + nc:nr + nc + len(ct_views)]]
        acc_vals = [r[...] for r in refs[nr + nc + len(ct_views):n_in]]
        out_refs = refs[n_in:]

        def f(*dargs):
            rv = list(row_vals)
            cv = list(const_vals)
            for pos, idx in enumerate(diff_rows):
                rv[idx] = dargs[pos]
            for pos, idx in enumerate(diff_consts):
                cv[idx] = dargs[len(diff_rows) + pos]
            return tuple(fn(i, *rv, *cv))

        primals = [row_vals[idx] for idx in diff_rows] + [const_vals[idx] for idx in diff_consts]
        outs, pull = jax.vjp(f, *primals)
        full_ct, ci = [], 0
        if ct_map is not None:
            full_ct = ct_map(i, *ct_vals)
        else:
            for o, c in zip(outs, cts):
                if c is None:
                    full_ct.append(jnp.zeros_like(o))
                else:
                    full_ct.append(ct_vals[ci])
                    ci += 1
        grads = pull(tuple(full_ct))
        for o_ref, o in zip(out_refs[:n_fwd], outs):
            o_ref[...] = o
        ai = 0
        for pos in range(len(diff_rows)):
            g = grads[pos]
            if acc[pos] is not None:
                g = g + acc_vals[ai]
                ai += 1
            out_refs[n_fwd + pos][...] = g.astype(row_dtype)
        for pos in range(len(diff_consts)):
            g = grads[len(diff_rows) + pos]
            o_ref = out_refs[n_fwd + len(diff_rows) + pos]

            @pl.when(i == 0)
            def _(o_ref=o_ref, g=g):
                o_ref[...] = g

            @pl.when(i > 0)
            def _(o_ref=o_ref, g=g):
                o_ref[...] += g

    out_specs = [pl.BlockSpec((BLOCK, w), lambda i: (i, 0)) for w in fwd_widths]
    out_shape = [jax.ShapeDtypeStruct((nblk * BLOCK, w), F32) for w in fwd_widths]
    for idx in diff_rows:
        out_specs.append(pl.BlockSpec((BLOCK, rows[idx][1]), lambda i: (i, 0)))
        out_shape.append(jax.ShapeDtypeStruct((nblk * BLOCK, rows[idx][1]), row_dtype))
    for idx in diff_consts:
        out_specs.append(_const_spec(consts[idx]))
        out_shape.append(jax.ShapeDtypeStruct(consts[idx].shape, F32))
    return pl.pallas_call(
        body, name=name, grid=(nblk,),
        in_specs=([_row_spec(v) for v in rows] + [_const_spec(c) for c in consts]
                  + [_row_spec(v) for v in ct_views] + [_row_spec(v) for v in acc_views]),
        out_specs=out_specs, out_shape=out_shape,
        compiler_params=_cparams(("arbitrary",)),
    )(*[v[0] for v in rows], *consts, *[v[0] for v in ct_views], *[v[0] for v in acc_views])


def _rms_fn(i, x, g):
    return (x * lax.rsqrt(jnp.mean(x * x, axis=-1, keepdims=True) + RMS_EPS) * g,)


def _sigmoid(x):
    return 1.0 / (1.0 + jnp.exp(-x))


def _softplus(x):
    return jnp.maximum(x, 0.0) + jnp.log(1.0 + jnp.exp(-jnp.abs(x)))


def _split2(x):
    hi = x.astype(BF16)
    lo = (x - hi.astype(F32)).astype(BF16)
    return jnp.concatenate([hi, lo], axis=1)


@jax.custom_vjp
def _head_sum(x):
    r = lax.broadcasted_iota(jnp.int32, (2 * RW_DIM, RW_DIM), 0) % RW_DIM // HEAD_DIM
    c = lax.broadcasted_iota(jnp.int32, (2 * RW_DIM, RW_DIM), 1) // HEAD_DIM
    return jnp.dot(_split2(x), (r == c).astype(BF16), preferred_element_type=F32)


_head_sum.defvjp(lambda x: (_head_sum(x), None), lambda _, ct: (_head_sum(ct),))


@jax.custom_vjp
def _dot_bf16(x, w):
    return jnp.dot(x.astype(BF16), w.astype(BF16), preferred_element_type=F32)


def _dot_bf16_bwd(res, ct):
    x, w = res
    ct = ct.astype(BF16)
    dx = lax.dot_general(ct, w.astype(BF16), (((1,), (1,)), ((), ())), preferred_element_type=F32)
    dw = lax.dot_general(x.astype(BF16), ct, (((0,), (0,)), ((), ())), preferred_element_type=F32)
    return dx, dw


_dot_bf16.defvjp(lambda x, w: (_dot_bf16(x, w), (x, w)), _dot_bf16_bwd)


def _rwkv_pre_fn(i, r, k, v, dw, da, dg, r_p, k_p, v_p, dw_p, da_p, dg_p,
                 mix_r, mix_k, mix_v, mix_dw, mix_da, mix_dg, w0, w2, a0, a2, g2, k_k, k_a):
    row = i * BLOCK + lax.broadcasted_iota(jnp.int32, (BLOCK, 1), 0)
    live = row >= PAD_ROWS
    live_prev = row >= PAD_ROWS + 1

    def shift(cur, prev, mix):
        cur = jnp.where(live, cur, 0.0)
        prev = jnp.where(live_prev, prev, 0.0)
        return cur + (prev - cur) * mix

    r = shift(r, r_p, mix_r)
    k = shift(k, k_p, mix_k)
    v = shift(v, v_p, mix_v)
    dw = shift(dw, dw_p, mix_dw)
    da = shift(da, da_p, mix_da)
    dg = shift(dg, dg_p, mix_dg)
    wlog = -_softplus(-(w0 + _dot_bf16(jnp.tanh(dw), w2))) - 0.5
    decay = jnp.exp(-jnp.exp(wlog))
    a = _sigmoid(a0 + _dot_bf16(da, a2))
    g = _dot_bf16(_sigmoid(dg), g2)
    kk = k * k_k
    norm_sq = jnp.where(live, _head_sum(kk * kk), 1.0)
    kk = kk / jnp.maximum(jnp.sqrt(norm_sq), 1e-12)
    k_mod = k * (1.0 + (a - 1.0) * k_a)
    return r, decay, k_mod, v, -kk, kk * a, g


def _rwkv_pre_xt_fn(i, *args):
    r, decay, k_mod, v, a_neg, b, g = _rwkv_pre_fn(i, *args)
    t = SCAN_T
    xt = jnp.concatenate([_rows_to_xt(x[c * t:(c + 1) * t]) for c in range(BLOCK // t)
                          for x in (a_neg, decay, b, k_mod, r)], axis=0)
    return r, k_mod, v, g, xt


def _rwkv_pre_cts(i, dxt, dv_s, dr_p, dk_p, dv_p, dg_p):
    t = SCAN_T
    d_a, d_w, d_b, d_k, d_r = [
        jnp.concatenate([_xt_to_rows(dxt[c * VEC_ROWS + n * HEAD_DIM:c * VEC_ROWS + (n + 1) * HEAD_DIM])
                         for c in range(BLOCK // t)], axis=0) for n in range(N_VEC)]
    dr_p, dk_p, dv_p, dg_p = [jnp.where(i > 0, x, 0.0) for x in (dr_p, dk_p, dv_p, dg_p)]
    return d_r + dr_p, d_w, d_k + dk_p, dv_s + dv_p, d_a, d_b, dg_p


def _rwkv_post_fn(i, ys, r, k_mod, v, g, ln_w, ln_b, r_k):
    mean = _head_sum(ys) * (1.0 / HEAD_DIM)
    d = ys - mean
    var = _head_sum(d * d) * (1.0 / HEAD_DIM)
    yn = d * lax.rsqrt(var + RWKV_LN_EPS) * ln_w + ln_b
    bonus = _head_sum(r * k_mod * r_k) * v
    return ((yn + bonus) * g,)


def _merge_fn(i, ya, yr, g1, g2):
    return (_sigmoid(g1) * ya + _sigmoid(g2) * yr,)


def _swiglu_fn(i, gate, up):
    return (gate * _sigmoid(gate) * up,)


def _loss_fn(i, h, tgt, g):
    y = h * lax.rsqrt(jnp.mean(h * h, axis=-1, keepdims=True) + RMS_EPS) * g
    err = y - tgt
    return (0.5 * jnp.mean(err * err, axis=-1, keepdims=True),)


def _rope_tables(lp):
    pos = (jnp.arange(lp, dtype=jnp.int32) - PAD_ROWS).astype(F32)
    inv_freq = jnp.power(jnp.float32(ROPE_THETA), -jnp.arange(ROPE_HALF, dtype=F32) * (2.0 / ROPE_DIM))
    ang = pos[:, None] * inv_freq[None, :]
    cos, sin = jnp.cos(ang), jnp.sin(ang)
    one = jnp.ones((lp, HEAD_DIM - ROPE_DIM), F32)
    zero_h = jnp.zeros((lp, ROPE_HALF), F32)
    zero_r = jnp.zeros((lp, HEAD_DIM - ROPE_DIM), F32)
    c = jnp.concatenate([cos, cos, one], axis=1)
    s1 = jnp.concatenate([-sin, zero_h, zero_r], axis=1)
    s2 = jnp.concatenate([zero_h, sin, zero_r], axis=1)
    return tuple(jnp.tile(t, (1, LANES // HEAD_DIM)) for t in (c, s1, s2))


def _rope_fwd_fn(i, x, c, s1, s2):
    n = x.shape[1]
    c, s1, s2 = [jnp.tile(t, (1, n // LANES)) for t in (c, s1, s2)]
    return (x * c + pltpu.roll(x, n - ROPE_HALF, 1) * s1 + pltpu.roll(x, ROPE_HALF, 1) * s2,)


def _rope_bwd_fn(i, dy, c, s1, s2):
    n = dy.shape[1]
    c, s1, s2 = [jnp.tile(t, (1, n // LANES)) for t in (c, s1, s2)]
    return (dy * c + pltpu.roll(dy * s1, ROPE_HALF, 1) + pltpu.roll(dy * s2, n - ROPE_HALF, 1),)


def _attn_mask(i):
    r = lax.broadcasted_iota(jnp.int32, (BLOCK, 3 * BLOCK), 0)
    c = lax.broadcasted_iota(jnp.int32, (BLOCK, 3 * BLOCK), 1)
    meta = (c < BLOCK) & (c >= PAD_ROWS)
    prev = (c >= BLOCK) & (c < 2 * BLOCK) & ((c - BLOCK) > r) & (i >= 1)
    cur = (c >= 2 * BLOCK) & ((c - 2 * BLOCK) <= r)
    return meta | prev | cur


def _attn_rows(ref, g):
    return ref[:, g * HEAD_DIM:(g + 1) * HEAD_DIM]


def _attn_group(i, g, q_all, k_refs, v_refs, s_ref):
    heads = range(g * GROUP, (g + 1) * GROUP)
    kcat = jnp.concatenate([_attn_rows(r, g) for r in k_refs], axis=0).astype(BF16)
    vcat = jnp.concatenate([_attn_rows(r, g) for r in v_refs], axis=0).astype(BF16)
    qg = jnp.concatenate([q_all[:, h * HEAD_DIM:(h + 1) * HEAD_DIM] for h in heads], axis=0).astype(BF16)
    sink = jnp.concatenate([jnp.broadcast_to(s_ref[0:1, h:h + 1], (BLOCK, 1)) for h in heads], axis=0)
    s = lax.dot_general(qg, kcat, (((1,), (1,)), ((), ())), preferred_element_type=F32) * (HEAD_DIM ** -0.5)
    valid = jnp.concatenate([_attn_mask(i)] * GROUP, axis=0)
    return heads, qg, kcat, vcat, sink, jnp.where(valid, s, NEG_INF)


def _attn_specs(v_col):
    blk = lambda f: pl.BlockSpec((BLOCK, 2 * HEAD_DIM), f)
    keys = [blk(lambda i: (0, 0)), blk(lambda i: (i, 0)), blk(lambda i: (i + 1, 0))]
    vals = [blk(lambda i: (0, v_col)), blk(lambda i: (i, v_col)), blk(lambda i: (i + 1, v_col))]
    return keys + vals + [pl.BlockSpec((1, Q_HEADS), lambda i: (0, 0))]


def _attn_fwd(q, k, v, v_col, sinks, *, nblk, name):
    def body(q_ref, km_ref, kp_ref, kc_ref, vm_ref, vp_ref, vc_ref, s_ref, o_ref, lse_ref):
        i = pl.program_id(0)
        q_all = q_ref[...]
        for g in range(KV_HEADS):
            heads, _, _, vcat, sink, s = _attn_group(i, g, q_all, (km_ref, kp_ref, kc_ref), (vm_ref, vp_ref, vc_ref),
                                                     s_ref)
            m = jnp.maximum(jnp.max(s, axis=-1, keepdims=True), sink)
            p = jnp.exp(s - m)
            den = jnp.sum(p, axis=-1, keepdims=True) + jnp.exp(sink - m)
            o = jnp.dot(p.astype(BF16), vcat, preferred_element_type=F32) / den
            lse = m + jnp.log(den)
            for n, h in enumerate(heads):
                o_ref[:, h * HEAD_DIM:(h + 1) * HEAD_DIM] = o[n * BLOCK:(n + 1) * BLOCK]
                lse_ref[:, h:h + 1] = lse[n * BLOCK:(n + 1) * BLOCK]

    return pl.pallas_call(
        body, name=name, grid=(nblk,),
        in_specs=[pl.BlockSpec((BLOCK, Q_HEADS * HEAD_DIM), lambda i: (i + 1, 0))] + _attn_specs(v_col),
        out_specs=[pl.BlockSpec((BLOCK, Q_HEADS * HEAD_DIM), lambda i: (i, 0)),
                   pl.BlockSpec((BLOCK, Q_HEADS), lambda i: (i, 0))],
        out_shape=[jax.ShapeDtypeStruct((nblk * BLOCK, Q_HEADS * HEAD_DIM), F32),
                   jax.ShapeDtypeStruct((nblk * BLOCK, Q_HEADS), F32)],
        compiler_params=_cparams(("parallel",)),
    )(q, k, k, k, v, v, v, sinks)


def _attn_bwd(q, k, v, v_col, sinks, o, lse, do, *, nblk, name):
    lp = k.shape[0]

    def body(q_ref, km_ref, kp_ref, kc_ref, vm_ref, vp_ref, vc_ref, s_ref, o_ref, lse_ref, do_ref,
             dq_ref, dk_ref, dv_ref, ds_ref):
        i = pl.program_id(0)

        @pl.when(i == 0)
        def _():
            dk_ref[...] = jnp.zeros_like(dk_ref)
            dv_ref[...] = jnp.zeros_like(dv_ref)
            ds_ref[...] = jnp.zeros_like(ds_ref)

        lane = lax.broadcasted_iota(jnp.int32, (1, Q_HEADS), 1)
        prev_rows = pl.ds(pl.multiple_of(i * BLOCK, BLOCK), BLOCK)
        cur_rows = pl.ds(pl.multiple_of((i + 1) * BLOCK, BLOCK), BLOCK)
        q_all, o_all, do_all, lse_all = q_ref[...], o_ref[...], do_ref[...], lse_ref[...]
        for g in range(KV_HEADS):
            heads, qg, kcat, vcat, sink, s = _attn_group(i, g, q_all, (km_ref, kp_ref, kc_ref),
                                                         (vm_ref, vp_ref, vc_ref), s_ref)
            stack = lambda x: jnp.concatenate([x[:, h * HEAD_DIM:(h + 1) * HEAD_DIM] for h in heads], axis=0)
            lse_g = jnp.concatenate([lse_all[:, h:h + 1] for h in heads], axis=0)
            do_g = stack(do_all)
            p = jnp.exp(s - lse_g)
            delta = jnp.sum(do_g * stack(o_all), axis=-1, keepdims=True)
            dp = lax.dot_general(do_g.astype(BF16), vcat, (((1,), (1,)), ((), ())), preferred_element_type=F32)
            dsc = (p * (dp - delta) * (HEAD_DIM ** -0.5)).astype(BF16)
            dq = jnp.dot(dsc, kcat, preferred_element_type=F32)
            dk_all = lax.dot_general(dsc, qg, (((0,), (0,)), ((), ())), preferred_element_type=F32)
            dv_all = lax.dot_general(p.astype(BF16), do_g.astype(BF16), (((0,), (0,)), ((), ())),
                                     preferred_element_type=F32)
            cols = slice(g * HEAD_DIM, (g + 1) * HEAD_DIM)
            for ref, full in ((dk_ref, dk_all), (dv_ref, dv_all)):
                ref[0:BLOCK, cols] += full[0:BLOCK]
                ref[prev_rows, cols] += full[BLOCK:2 * BLOCK]
                ref[cur_rows, cols] += full[2 * BLOCK:]
            sink_part = jnp.exp(sink - lse_g) * delta
            for n, h in enumerate(heads):
                dq_ref[:, h * HEAD_DIM:(h + 1) * HEAD_DIM] = dq[n * BLOCK:(n + 1) * BLOCK]
                dsink = -jnp.sum(sink_part[n * BLOCK:(n + 1) * BLOCK], axis=0, keepdims=True)
                ds_ref[...] += jnp.where(lane == h, dsink, 0.0)

    qspec = pl.BlockSpec((BLOCK, Q_HEADS * HEAD_DIM), lambda i: (i, 0))
    whole = pl.BlockSpec((lp, 2 * HEAD_DIM), lambda i: (0, 0))
    return pl.pallas_call(
        body, name=name, grid=(nblk,),
        in_specs=([pl.BlockSpec((BLOCK, Q_HEADS * HEAD_DIM), lambda i: (i + 1, 0))] + _attn_specs(v_col)
                  + [qspec, pl.BlockSpec((BLOCK, Q_HEADS), lambda i: (i, 0)), qspec]),
        out_specs=[qspec, whole, whole, pl.BlockSpec((1, Q_HEADS), lambda i: (0, 0))],
        out_shape=[jax.ShapeDtypeStruct((nblk * BLOCK, Q_HEADS * HEAD_DIM), F32),
                   jax.ShapeDtypeStruct((lp, 2 * HEAD_DIM), F32), jax.ShapeDtypeStruct((lp, 2 * HEAD_DIM), F32),
                   jax.ShapeDtypeStruct((1, Q_HEADS), F32)],
        compiler_params=_cparams(("arbitrary",)),
    )(q, k, k, k, v, v, v, sinks, o, lse, do)


N_VEC = 5
VEC_ROWS = N_VEC * HEAD_DIM


def _selectors():
    t = SCAN_T
    shape = (t, 2 * LANES, RW_DIM)
    step, src, dst = [lax.broadcasted_iota(jnp.int32, shape, d) for d in range(3)]
    src = src % LANES
    spread = ((src // t == dst // HEAD_DIM) & (src % t == step)).astype(BF16)
    shape = (t, RW_DIM, LANES)
    step, src, dst = [lax.broadcasted_iota(jnp.int32, shape, d) for d in range(3)]
    collect = ((src // HEAD_DIM == dst // t) & (dst % t == step)).astype(BF16)
    return spread, collect


def _rows_to_xt(x):
    low = lax.broadcasted_iota(jnp.int32, (SCAN_T, LANES), 1) < HEAD_DIM
    pieces = []
    for m in range(RW_HEADS // 2):
        pair = x[:, m * LANES:(m + 1) * LANES]
        pieces += [jnp.where(low, pair, 0.0), jnp.where(low, pltpu.roll(pair, HEAD_DIM, 1), 0.0)]
    return jnp.concatenate(pieces, axis=0).T[:HEAD_DIM]


def _xt_to_rows(a):
    t = SCAN_T
    a_t = jnp.concatenate([a, jnp.zeros_like(a)], axis=0).T
    pairs = [a_t[2 * m * t:(2 * m + 1) * t] + pltpu.roll(a_t[(2 * m + 1) * t:(2 * m + 2) * t], HEAD_DIM, 1)
             for m in range(RW_HEADS // 2)]
    return jnp.concatenate(pairs, axis=1)


def _with_exchange(compute, n_in, n_out, n_scratch, scattered, shared, grid):
    n_sc = len(scattered)
    n_x = n_sc + len(shared)
    if n_x == 0:
        return compute

    def body(*refs):
        ins, x_in = refs[:n_in], refs[n_in:n_in + n_x]
        outs, x_out = refs[n_in + n_x:n_in + n_x + n_out], refs[n_in + n_x + n_out:n_in + 2 * n_x + n_out]
        scratch = refs[n_in + 2 * n_x + n_out:n_in + 2 * n_x + n_out + n_scratch]
        sems = refs[n_in + 2 * n_x + n_out + n_scratch:]

        first = last = True
        for d, size in enumerate(grid):
            first = first & (pl.program_id(d) == 0)
            last = last & (pl.program_id(d) == size - 1)

        @pl.when(first)
        def _():
            for cp in _exchange_copies(x_in, x_out, n_sc, *sems):
                cp.start()

        compute(*ins, *outs, *scratch)

        @pl.when(last)
        def _():
            for cp in _exchange_copies(x_in, x_out, n_sc, *sems):
                cp.wait()

    return body


def _wkv_fwd(xt, v, spread, name, shared=()):
    t_steps = SCAN_T
    nch = xt.shape[0]
    per = SCAN_CHUNKS
    grid = (nch // per,)
    rows = per * t_steps
    n_x = len(shared)

    def compute(xt_ref, v_ref, sel_ref, y_ref, hist_ref, st_ref):
        @pl.when(pl.program_id(0) == 0)
        def _():
            st_ref[...] = jnp.zeros_like(st_ref)

        st = st_ref[...]
        for c in range(per):
            x2 = _split2(xt_ref[c])
            for j in range(t_steps):
                row = c * t_steps + j
                cols = jnp.dot(x2, sel_ref[j], preferred_element_type=F32)
                a_c, w_c, b_c, k_c, r_c = [cols[n * HEAD_DIM:(n + 1) * HEAD_DIM] for n in range(N_VEC)]
                hist_ref[row] = st
                sa = jnp.sum(st * a_c, axis=0, keepdims=True)
                st = st * w_c + b_c * sa + k_c * v_ref[row:row + 1, :]
                y_ref[row:row + 1, :] = jnp.sum(st * r_c, axis=0, keepdims=True)
        st_ref[...] = st

    return pl.pallas_call(
        _with_exchange(compute, 3, 2, 1, (), shared, grid), name=name, grid=grid,
        in_specs=[pl.BlockSpec((per, VEC_ROWS, LANES), lambda c: (c, 0, 0)),
                  pl.BlockSpec((rows, RW_DIM), lambda c: (c, 0)),
                  pl.BlockSpec(spread.shape, lambda c: (0, 0, 0))] + [ANY] * n_x,
        out_specs=[pl.BlockSpec((rows, RW_DIM), lambda c: (c, 0)),
                   pl.BlockSpec((rows, HEAD_DIM, RW_DIM), lambda c: (c, 0, 0))] + [ANY] * n_x,
        out_shape=[jax.ShapeDtypeStruct((nch * t_steps, RW_DIM), F32),
                   jax.ShapeDtypeStruct((nch * t_steps, HEAD_DIM, RW_DIM), F32)] + _exchange_shapes((), shared),
        scratch_shapes=[pltpu.VMEM((HEAD_DIM, RW_DIM), F32)] + (_exchange_sems(n_x) if n_x else []),
        compiler_params=_cparams(("arbitrary",)),
    )(xt, v, spread, *shared)


def _wkv_bwd(xt, v, hist, dy, spread, collect, name, scattered=()):
    t_steps = SCAN_T
    nch = xt.shape[0]
    n_x = len(scattered)
    per = SCAN_CHUNKS
    nsteps = nch // per
    grid = (nsteps,)
    rows = per * t_steps
    lead = BLOCK // rows

    def compute(xt_ref, v_ref, hist_ref, dy_ref, sel_ref, col_ref, dxt_ref, dv_ref, g_ref):
        @pl.when(pl.program_id(0) == 0)
        def _():
            g_ref[...] = jnp.zeros_like(g_ref)

        has_dy = nsteps - 1 - pl.program_id(0) >= lead
        gst = g_ref[...]
        nxt = None
        for c in reversed(range(per)):
            x2 = _split2(xt_ref[c])
            acc = jnp.zeros((VEC_ROWS, LANES), F32)
            for j in reversed(range(t_steps)):
                row = c * t_steps + j
                cols = jnp.dot(x2, sel_ref[j], preferred_element_type=F32)
                a_c, w_c, b_c, k_c, r_c = [cols[n * HEAD_DIM:(n + 1) * HEAD_DIM] for n in range(N_VEC)]
                prev = hist_ref[row]
                v_row = v_ref[row:row + 1, :]
                dy_row = jnp.where(has_dy, dy_ref[row:row + 1, :], 0.0)
                sa = jnp.sum(prev * a_c, axis=0, keepdims=True)
                if nxt is None:
                    nxt = prev * w_c + b_c * sa + k_c * v_row
                gst = gst + r_c * dy_row
                dv_ref[row:row + 1, :] = jnp.sum(gst * k_c, axis=0, keepdims=True)
                dsa = jnp.sum(gst * b_c, axis=0, keepdims=True)
                prods = jnp.concatenate([p.astype(BF16) for p in
                                         (prev * dsa, gst * prev, gst * sa, gst * v_row, nxt * dy_row)], axis=0)
                acc = acc + jnp.dot(prods, col_ref[j], preferred_element_type=F32)
                gst = gst * w_c + a_c * dsa
                nxt = prev
            dxt_ref[c] = acc
        g_ref[...] = gst

    rev3 = lambda c: (nsteps - 1 - c, 0, 0)
    rev2 = lambda c: (nsteps - 1 - c, 0)
    rowspec = pl.BlockSpec((rows, RW_DIM), rev2)
    return pl.pallas_call(
        _with_exchange(compute, 6, 2, 1, scattered, (), grid), name=name, grid=grid,
        in_specs=[pl.BlockSpec((per, VEC_ROWS, LANES), rev3), rowspec,
                  pl.BlockSpec((rows, HEAD_DIM, RW_DIM), rev3),
                  pl.BlockSpec((rows, RW_DIM), lambda c: (jnp.maximum(nsteps - 1 - c - lead, 0), 0)),
                  pl.BlockSpec(spread.shape, lambda c: (0, 0, 0)),
                  pl.BlockSpec(collect.shape, lambda c: (0, 0, 0))] + [ANY] * n_x,
        out_specs=[pl.BlockSpec((per, VEC_ROWS, LANES), rev3), rowspec] + [ANY] * n_x,
        out_shape=[jax.ShapeDtypeStruct((nch, VEC_ROWS, LANES), F32),
                   jax.ShapeDtypeStruct((nch * t_steps, RW_DIM), F32)] + _exchange_shapes(scattered, ()),
        scratch_shapes=[pltpu.VMEM((HEAD_DIM, RW_DIM), F32)] + (_exchange_sems(n_x) if n_x else []),
        compiler_params=_cparams(("arbitrary",)),
    )(xt, v, hist, dy, spread, collect, *scattered)


MESH = pl.DeviceIdType.MESH
ANY = pl.BlockSpec(memory_space=pl.ANY)


def _all_gather(arrays, name):
    n_arr = len(arrays)
    per = N_DEV - 1

    def body(*refs):
        x_refs, out_refs = refs[:n_arr], refs[n_arr:2 * n_arr]
        send_sems, recv_sems, local_sems = refs[2 * n_arr:]
        xi, yi, ci = lax.axis_index("x"), lax.axis_index("y"), lax.axis_index("c")
        me, sibling = (xi, yi, ci), (xi, yi, 1 - ci)
        chips = [(1 - xi, yi), (xi, 1 - yi), (1 - xi, 1 - yi)]

        def slot(a, px, py, pc):
            return out_refs[a].at[4 * px + 2 * py + pc]

        def copy(a, sem, block, to, src=None):
            return pltpu.make_async_remote_copy(
                src_ref=slot(a, *block) if src is None else src, dst_ref=slot(a, *block),
                send_sem=send_sems.at[per * a + sem], recv_sem=recv_sems.at[per * a + sem],
                device_id=to, device_id_type=MESH)

        mine = [pltpu.make_async_copy(x_refs[a], slot(a, *me), local_sems.at[a]) for a in range(n_arr)]
        for cp in mine:
            cp.start()
        sent = []
        for a in range(n_arr):
            sent.append(copy(a, 0, me, sibling, src=x_refs[a]))
            sent += [copy(a, 1 + j, me, (*chip, ci), src=x_refs[a]) for j, chip in enumerate(chips)]
        for cp in sent:
            cp.start()
        for j, chip in enumerate(chips):
            for a in range(n_arr):
                copy(a, 1 + j, (*chip, ci), me).wait_recv()
                onward = copy(a, 4 + j, (*chip, ci), sibling)
                onward.start()
                sent.append(onward)
        for a in range(n_arr):
            copy(a, 0, sibling, me).wait_recv()
        for j, chip in enumerate(chips):
            for a in range(n_arr):
                copy(a, 4 + j, (*chip, 1 - ci), me).wait_recv()
        for cp in sent:
            cp.wait_send()
        for cp in mine:
            cp.wait()

    sems = pltpu.SemaphoreType.DMA((per * n_arr,))
    return pl.pallas_call(
        body, name=name, out_shape=[jax.ShapeDtypeStruct((N_DEV,) + a.shape, a.dtype) for a in arrays],
        in_specs=[ANY] * n_arr, out_specs=[ANY] * n_arr,
        scratch_shapes=[sems, sems, pltpu.SemaphoreType.DMA((n_arr,))],
    )(*arrays)


def _exchange_copies(in_refs, out_refs, n_scattered, send_sems, recv_sems, local_sems):
    n_arr = len(in_refs)
    per = N_DEV - 1
    xi, yi, ci = lax.axis_index("x"), lax.axis_index("y"), lax.axis_index("c")
    me = 4 * xi + 2 * yi + ci
    src_of = lambda a, peer: in_refs[a].at[peer] if a < n_scattered else in_refs[a]
    copies = []
    for d in range(1, N_DEV):
        px = 1 - xi if d & 4 else xi
        py = 1 - yi if d & 2 else yi
        pc = 1 - ci if d & 1 else ci
        for a in range(n_arr):
            copies.append(pltpu.make_async_remote_copy(
                src_ref=src_of(a, 4 * px + 2 * py + pc), dst_ref=out_refs[a].at[me],
                send_sem=send_sems.at[per * a + d - 1], recv_sem=recv_sems.at[per * a + d - 1],
                device_id=(px, py, pc), device_id_type=MESH))
    own = [pltpu.make_async_copy(src_of(a, me), out_refs[a].at[me], local_sems.at[a]) for a in range(n_arr)]
    return copies + own


def _exchange_shapes(scattered, shared):
    return ([jax.ShapeDtypeStruct(a.shape, a.dtype) for a in scattered]
            + [jax.ShapeDtypeStruct((N_DEV,) + a.shape, a.dtype) for a in shared])


def _exchange_sems(n_arr):
    sems = pltpu.SemaphoreType.DMA(((N_DEV - 1) * n_arr,))
    return [sems, sems, pltpu.SemaphoreType.DMA((n_arr,))]


def _exchange(scattered, shared, name):
    n_sc = len(scattered)
    n_arr = n_sc + len(shared)

    def body(*refs):
        copies = _exchange_copies(refs[:n_arr], refs[n_arr:2 * n_arr], n_sc, *refs[2 * n_arr:])
        for cp in copies:
            cp.start()
        for cp in copies:
            cp.wait()

    return pl.pallas_call(
        body, name=name, out_shape=_exchange_shapes(scattered, shared),
        in_specs=[ANY] * n_arr, out_specs=[ANY] * n_arr, scratch_shapes=_exchange_sems(n_arr),
    )(*scattered, *shared)


def _adamw(parts, w, m, v, name):
    rows, cols = w.shape[-2:]
    tile = PACK_ROWS if rows % PACK_ROWS == 0 else rows
    at = (0,) if w.ndim == 3 else (Ellipsis,)

    def body(p_ref, w_ref, m_ref, v_ref, g_out, d_out, m_out, v_out):
        g = p_ref[0].astype(F32)
        for s in range(1, N_DEV):
            g = g + p_ref[s].astype(F32)
        m_new = ADAM_B1 * m_ref[at] + (1.0 - ADAM_B1) * g
        v_new = ADAM_B2 * v_ref[at] + (1.0 - ADAM_B2) * (g * g)
        m_hat = m_new / (1.0 - ADAM_B1 ** ADAM_STEP)
        v_hat = v_new / (1.0 - ADAM_B2 ** ADAM_STEP)
        g_out[at] = g
        d_out[at] = -ADAM_LR * (m_hat / (jnp.sqrt(v_hat) + ADAM_EPS) + ADAM_WD * w_ref[at])
        m_out[at] = m_new
        v_out[at] = v_new

    spec = (pl.BlockSpec((1, tile, cols), lambda i: (0, i, 0)) if w.ndim == 3
            else pl.BlockSpec((tile, cols), lambda i: (i, 0)))
    return pl.pallas_call(
        body, name=name, grid=(rows // tile,),
        in_specs=[pl.BlockSpec((N_DEV, tile, cols), lambda i: (0, i, 0)), spec, spec, spec],
        out_specs=[spec] * 4, out_shape=[jax.ShapeDtypeStruct(w.shape, F32)] * 4,
        compiler_params=_cparams(("parallel",)),
    )(parts, w, m, v)


EARLY = [("meta_tokens", 1), ("rwkv_w2", 1), ("rwkv_a2", 1), ("rwkv_g2", 1)]
LATE = [("w_br_attn", 1), ("w_br_rwkv", 1), ("w_o", 0), ("w_ffn_gate", 1), ("w_ffn_up", 1), ("w_ffn_down", 0)]
REPLICATED = ["norm_mix_g", "b_in", "attn_sinks", "rwkv_mix", "rwkv_w0", "rwkv_a0", "rwkv_k_k", "rwkv_k_a",
              "rwkv_r_k", "rwkv_ln_w", "rwkv_ln_b", "norm_ffn_g", "norm_final_g"]
WEIGHTS = ["meta_tokens", "norm_mix_g", "w_in", "b_in", "attn_sinks", "rwkv_mix", "rwkv_w0", "rwkv_w2", "rwkv_a0",
           "rwkv_a2", "rwkv_g2", "rwkv_k_k", "rwkv_k_a", "rwkv_r_k", "rwkv_ln_w", "rwkv_ln_b", "w_br_attn",
           "w_br_rwkv", "w_o", "norm_ffn_g", "w_ffn_gate", "w_ffn_up", "w_ffn_down", "norm_final_g"]


def _pack(arrays, row_multiple):
    flat = jnp.concatenate([a.reshape(-1) for a in arrays])
    per = row_multiple * LANES
    total = -(-flat.shape[0] // per) * per
    return jnp.pad(flat, (0, total - flat.shape[0])).reshape(-1, LANES)


def _unpack(buf, shapes):
    flat = buf.reshape(-1)
    out, pos = [], 0
    for s in shapes:
        n = 1
        for d in s:
            n *= d
        out.append(flat[pos:pos + n].reshape(s))
        pos += n
    return out


def _strip(name, a):
    return a if name in ("meta_tokens", "norm_final_g") else a[0]


def _join(gathered, axis):
    if axis == 0:
        return gathered.reshape(-1, gathered.shape[2])
    return gathered.transpose(1, 0, 2).reshape(gathered.shape[1], -1)


def _split(g, axis):
    if axis == 0:
        return g.reshape(N_DEV, -1, g.shape[1])
    return g.reshape(g.shape[0], N_DEV, -1).transpose(1, 0, 2)


W_IN_LAYOUT = [(768, 2304), (0, 512), (2592, 4640), (2432, 2592), 256 - GATE_LORA, (512, 768), (2304, 2368),
               128 - DECAY_LORA, (2368, 2432), 128 - AAA_LORA, NP - C_DA - 128]


def _w_in_padded(w, shard_width=None):
    rows = w.shape[-2]
    width = D_IN if shard_width is None else shard_width
    parts = []
    for seg in W_IN_LAYOUT:
        if isinstance(seg, int):
            parts.append(jnp.zeros((rows, seg), w.dtype))
            continue
        lo, stop = seg
        while lo < stop:
            p = lo // width
            hi = min(stop, (p + 1) * width)
            src = w if shard_width is None else w[p]
            parts.append(src[:, lo - p * width:hi - p * width])
            lo = hi
    return jnp.concatenate(parts, axis=1)


def _w_in_unpadded(wp, lo=0, stop=D_IN):
    spans, pos = [], 0
    for seg in W_IN_LAYOUT:
        if isinstance(seg, int):
            pos += seg
        else:
            spans.append((seg[0], seg[1], pos))
            pos += seg[1] - seg[0]
    parts = []
    for a, b, at in sorted(spans):
        c, d = max(a, lo), min(b, stop)
        if c < d:
            parts.append(wp[:, at + c - a:at + d - a])
    return jnp.concatenate(parts, axis=1)


def _pad_rows(a, n):
    return jnp.pad(a, ((0, n - a.shape[0]), (0, 0)))


def _device_step(x, tgt, full, gather_late=None, scatter_early=None, scatter_last=None):
    seq = x.shape[0]
    nblk = seq // BLOCK
    lp = seq + BLOCK
    nall = nblk + 1

    w_in_p = full["w_in_p"]
    b_in_p = _w_in_padded(full["b_in"][None])
    mix = full["rwkv_mix"][None]
    mix_r, mix_k, mix_v = mix[:, 0:512], mix[:, 512:1024], mix[:, 1024:1536]
    mix_dw = jnp.pad(mix[:, 1536:1600], ((0, 0), (0, 64)))
    mix_da = jnp.pad(mix[:, 1600:1664], ((0, 0), (0, 64)))
    mix_dg = jnp.pad(mix[:, 1664:1824], ((0, 0), (0, 96)))
    w2_p = _pad_rows(full["rwkv_w2"].astype(F32), 128)
    a2_p = _pad_rows(full["rwkv_a2"].astype(F32), 128)
    g2_p = _pad_rows(full["rwkv_g2"].astype(F32), 256)
    row = lambda name: full[name].reshape(1, -1)
    sinks = row("attn_sinks")
    rope_c, rope_s1, rope_s2 = _rope_tables(lp)

    hpad = jnp.concatenate([jnp.zeros((PAD_ROWS, D_MODEL), F32), full["meta_tokens"].astype(F32), x], axis=0)
    (u,) = _rows_fwd(_rms_fn, [_view(hpad)], [row("norm_mix_g")], [D_MODEL], nblk=nall, name="norm_mix",
                     out_dtype=BF16)
    proj = _mm(u, w_in_p, bias=b_in_p, name="in_proj")
    (q_r,) = _rows_fwd(_rope_fwd_fn, [_view(proj, 512, C_Q // 512), _view(rope_c), _view(rope_s1), _view(rope_s2)],
                       [], [512], nblk=nall, name="rope_q")
    (k_r,) = _rows_fwd(_rope_fwd_fn, [_view(proj, 128, C_KA // 128), _view(rope_c), _view(rope_s1),
                                      _view(rope_s2)], [], [128], nblk=nall, name="rope_k")
    o_attn, lse = _attn_fwd(q_r, k_r, proj, C_VA // 128, sinks, nblk=nblk, name="attn_fwd")

    rw_cols = jnp.concatenate([proj[:, C_R:C_R + 1536], proj[:, C_DG:C_DG + 256], proj[:, C_DW:C_DW + 256]], axis=1)
    rw_prev = jnp.pad(rw_cols[:-1], ((1, 0), (0, 0)))
    pre_rows = [_view(proj, 512, 0), _view(proj, 512, 1), _view(proj, 512, 2), _view(proj, 128, C_DW // 128),
                _view(proj, 128, C_DA // 128), _view(proj, 256, C_DG // 256),
                _view(rw_prev, 512, 0), _view(rw_prev, 512, 1), _view(rw_prev, 512, 2), _view(rw_prev, 128, 14),
                _view(rw_prev, 128, 15), _view(rw_prev, 256, 6)]
    pre_consts = [mix_r, mix_k, mix_v, mix_dw, mix_da, mix_dg, row("rwkv_w0"), w2_p, row("rwkv_a0"), a2_p, g2_p,
                  row("rwkv_k_k"), row("rwkv_k_a")]
    xt_block = (BLOCK // SCAN_T * VEC_ROWS, LANES)
    r_t, k_mod, v_t, gate, xt = _rows_fwd(_rwkv_pre_xt_fn, pre_rows, pre_consts, [RW_DIM] * 4 + [xt_block],
                                          nblk=nall, name="rwkv_pre")
    xt = xt.reshape(-1, VEC_ROWS, LANES)
    spread, collect = _selectors()
    y_scan, hist, *late = _wkv_fwd(xt, v_t, spread, name="wkv_fwd", shared=gather_late[0] if gather_late else ())
    if gather_late:
        full = {**full, **gather_late[1](late)}
    post_rows = [_view(y_scan, off=1), _view(r_t, off=1), _view(k_mod, off=1), _view(v_t, off=1), _view(gate, off=1)]
    post_consts = [row("rwkv_ln_w"), row("rwkv_ln_b"), row("rwkv_r_k")]
    (y_rwkv,) = _rows_fwd(_rwkv_post_fn, post_rows, post_consts, [RW_DIM], nblk=nblk, name="rwkv_post",
                          out_dtype=BF16)

    ya = _mm(o_attn, full["w_br_attn"], name="br_attn")
    yr = _mm(y_rwkv, full["w_br_rwkv"], name="br_rwkv")
    merge_rows = [_view(ya), _view(yr), _view(proj, 1024, C_G1 // 1024, 1), _view(proj, 1024, C_G2 // 1024, 1)]
    (merged,) = _rows_fwd(_merge_fn, merge_rows, [], [D_MODEL], nblk=nblk, name="merge", out_dtype=BF16)
    h1 = _mm(merged, full["w_o"], residual=x, name="out_proj")
    (f,) = _rows_fwd(_rms_fn, [_view(h1)], [row("norm_ffn_g")], [D_MODEL], nblk=nblk, name="norm_ffn",
                     out_dtype=BF16)
    ff_gate = _mm(f, full["w_ffn_gate"], name="ffn_gate")
    ff_up = _mm(f, full["w_ffn_up"], name="ffn_up")
    (act,) = _rows_fwd(_swiglu_fn, [_view(ff_gate), _view(ff_up)], [], [D_FF], nblk=nblk, name="swiglu",
                       out_dtype=BF16)
    h2 = _mm(act, full["w_ffn_down"], residual=h1, name="ffn_down")

    grads = {}
    ones_col = jnp.ones((seq, 1), F32)
    loss_rows, dh2, grads["norm_final_g"] = _rows_bwd(
        _loss_fn, [_view(h2), _view(tgt)], [row("norm_final_g")], [_view(ones_col)], nblk=nblk, name="loss",
        diff_rows=[0], diff_consts=[0], fwd_widths=[1])
    loss = jnp.sum(loss_rows)

    dact = _mm(dh2, full["w_ffn_down"], tb=True, name="d_act")
    grads["w_ffn_down"] = _mm(act, dh2, ta=True, name="dw_ffn_down")
    dgate, dup = _rows_bwd(_swiglu_fn, [_view(ff_gate), _view(ff_up)], [], [_view(dact)], nblk=nblk,
                           name="swiglu_bwd", diff_rows=[0, 1], diff_consts=[], row_dtype=BF16)
    grads["w_ffn_gate"] = _mm(f, dgate, ta=True, name="dw_ffn_gate")
    grads["w_ffn_up"] = _mm(f, dup, ta=True, name="dw_ffn_up")
    df = _mm(dgate, full["w_ffn_gate"], tb=True, name="df_gate")
    df = _mm(dup, full["w_ffn_up"], tb=True, residual=df, name="df_up")
    dh1, grads["norm_ffn_g"] = _rows_bwd(_rms_fn, [_view(h1)], [row("norm_ffn_g")], [_view(df)], nblk=nblk,
                                         name="norm_ffn_bwd", diff_rows=[0], diff_consts=[0], acc=[_view(dh2)])
    dmerged = _mm(dh1, full["w_o"], tb=True, name="d_merged")
    grads["w_o"] = _mm(merged, dh1, ta=True, name="dw_o")
    dya, dyr, dg1, dg2 = _rows_bwd(_merge_fn, merge_rows, [], [_view(dmerged)], nblk=nblk, name="merge_bwd",
                                   diff_rows=[0, 1, 2, 3], diff_consts=[], row_dtype=BF16)
    grads["w_br_attn"] = _mm(o_attn, dya, ta=True, name="dw_br_attn")
    grads["w_br_rwkv"] = _mm(y_rwkv, dyr, ta=True, name="dw_br_rwkv")
    dy_attn = _mm(dya, full["w_br_attn"], tb=True, name="d_y_attn")
    dy_rwkv = _mm(dyr, full["w_br_rwkv"], tb=True, name="d_y_rwkv")

    post = _rows_bwd(_rwkv_post_fn, post_rows, post_consts, [_view(dy_rwkv)], nblk=nblk, name="rwkv_post_bwd",
                     diff_rows=[0, 1, 2, 3, 4], diff_consts=[0, 1, 2])
    dys, dr_post, dk_post, dv_post, dgate_post = post[:5]
    grads["rwkv_ln_w"], grads["rwkv_ln_b"], grads["rwkv_r_k"] = post[5:]
    dxt, dv_s, *early_parts = _wkv_bwd(xt, v_t, hist, dys, spread, collect, name="wkv_bwd",
                                       scattered=scatter_early(grads) if scatter_early else ())
    pre_cts = [_view(dxt.reshape(-1, LANES), rows=xt_block[0]), _view(dv_s)] + [
        _view(t, off=-1) for t in (dr_post, dk_post, dv_post, dgate_post)]
    pre = _rows_bwd(_rwkv_pre_fn, pre_rows, pre_consts, pre_cts, nblk=nall, name="rwkv_pre_bwd",
                    diff_rows=list(range(12)), diff_consts=list(range(13)), ct_map=_rwkv_pre_cts)
    d_cur, d_prev, d_par = pre[0:6], pre[6:12], pre[12:]
    up = lambda t: jnp.pad(t[1:], ((0, 1), (0, 0)))
    d_rw = [c + up(p) for c, p in zip(d_cur, d_prev)]
    grads["rwkv_mix"] = jnp.concatenate([d_par[0], d_par[1], d_par[2], d_par[3][:, :DECAY_LORA],
                                         d_par[4][:, :AAA_LORA], d_par[5][:, :GATE_LORA]], axis=1)
    grads["rwkv_w0"], grads["rwkv_w2"] = d_par[6], d_par[7][:DECAY_LORA]
    grads["rwkv_a0"], grads["rwkv_a2"] = d_par[8], d_par[9][:AAA_LORA]
    grads["rwkv_g2"] = d_par[10][:GATE_LORA]
    grads["rwkv_k_k"], grads["rwkv_k_a"] = d_par[11], d_par[12]

    dq_real, dk_r, dva, grads["attn_sinks"] = _attn_bwd(q_r, k_r, proj, C_VA // 128, sinks, o_attn, lse, dy_attn,
                                                        nblk=nblk, name="attn_bwd")
    dq_r = jnp.pad(dq_real, ((BLOCK, 0), (0, 0)))
    (dq,) = _rows_fwd(_rope_bwd_fn, [_view(dq_r), _view(rope_c), _view(rope_s1), _view(rope_s2)], [], [512],
                      nblk=nall, name="rope_q_bwd", out_dtype=BF16)
    (dka,) = _rows_fwd(_rope_bwd_fn, [_view(dk_r), _view(rope_c), _view(rope_s1),
                                      _view(rope_s2)], [], [128], nblk=nall, name="rope_k_bwd", out_dtype=BF16)

    lead = lambda t: jnp.pad(t, ((BLOCK, 0), (0, 0)))
    pieces = [d_rw[0], d_rw[1], d_rw[2], dq, lead(dg1), lead(dg2), d_rw[5], dka, dva, d_rw[3], d_rw[4],
              jnp.zeros((lp, NP - C_DA - 128), BF16)]
    dproj = jnp.concatenate([p.astype(BF16) for p in pieces], axis=1)
    grads["w_in_p"] = _mm(u, dproj, ta=True, name="dw_in")
    grads["b_in"] = _w_in_unpadded(_colsum(dproj, name="db_in"))
    du, *last_parts = _mm(dproj, w_in_p, tb=True, name="d_u", scattered=scatter_last(grads)) if scatter_last else (
        _mm(dproj, w_in_p, tb=True, name="d_u"),)
    dh, grads["norm_mix_g"] = _rows_bwd(_rms_fn, [_view(hpad)], [row("norm_mix_g")], [_view(du)], nblk=nall,
                                        name="norm_mix_bwd", diff_rows=[0], diff_consts=[0], acc=[_view(lead(dh1))])
    grads["meta_tokens"] = dh[PAD_ROWS:BLOCK]
    return loss, dh[BLOCK:], grads, early_parts, last_parts


def kernel(x, meta_tokens, norm_mix_g, w_in, b_in, attn_sinks, rwkv_mix, rwkv_w0, rwkv_w2, rwkv_a0, rwkv_a2, rwkv_g2, rwkv_k_k, rwkv_k_a, rwkv_r_k, rwkv_ln_w, rwkv_ln_b, w_br_attn, w_br_rwkv, w_o, norm_ffn_g, w_ffn_gate, w_ffn_up, w_ffn_down, norm_final_g, loss_target, m_meta_tokens, m_norm_mix_g, m_w_in, m_b_in, m_attn_sinks, m_rwkv_mix, m_rwkv_w0, m_rwkv_w2, m_rwkv_a0, m_rwkv_a2, m_rwkv_g2, m_rwkv_k_k, m_rwkv_k_a, m_rwkv_r_k, m_rwkv_ln_w, m_rwkv_ln_b, m_w_br_attn, m_w_br_rwkv, m_w_o, m_norm_ffn_g, m_w_ffn_gate, m_w_ffn_up, m_w_ffn_down, m_norm_final_g, v_meta_tokens, v_norm_mix_g, v_w_in, v_b_in, v_attn_sinks, v_rwkv_mix, v_rwkv_w0, v_rwkv_w2, v_rwkv_a0, v_rwkv_a2, v_rwkv_g2, v_rwkv_k_k, v_rwkv_k_a, v_rwkv_r_k, v_rwkv_ln_w, v_rwkv_ln_b, v_w_br_attn, v_w_br_rwkv, v_w_o, v_norm_ffn_g, v_w_ffn_gate, v_w_ffn_up, v_w_ffn_down, v_norm_final_g):
    given = dict(locals())
    wts = {n: _strip(n, given[n]) for n in WEIGHTS}
    mom = {n: _strip(n, given["m_" + n]) for n in WEIGHTS}
    var = {n: _strip(n, given["v_" + n]) for n in WEIGHTS}
    small_shapes = [wts[n].shape for n in REPLICATED]
    width = wts["w_in"].shape[1]
    wire = lambda table: [wts[n].astype(BF16) for n, _ in table]

    w_in_all, *early_all = _all_gather([wts["w_in"].astype(BF16)] + wire(EARLY), name="gather_weights")
    full = {n: wts[n] for n in REPLICATED}
    full.update({n: _join(g, axis) for (n, axis), g in zip(EARLY, early_all)})
    full["w_in_p"] = _w_in_padded(w_in_all, shard_width=width)
    gather_late = (wire(LATE), lambda got: {n: _join(g, axis) for (n, axis), g in zip(LATE, got)})
    scatter_early = lambda g: [_split(g[n], axis).astype(BF16) for n, axis in LATE]
    scatter_last = lambda g: [jnp.stack([_w_in_unpadded(g["w_in_p"], p * width, (p + 1) * width)
                                         for p in range(N_DEV)]).astype(BF16)]

    loss_part, grad_x, grads, parts_late, (parts_w_in,) = _device_step(
        x[0], loss_target[0], full, gather_late, scatter_early, scatter_last)

    g_early = [_split(grads[n], axis).astype(BF16) for n, axis in EARLY]
    zero = jnp.zeros((1,), F32)
    g_small = _pack([grads[n].reshape(wts[n].shape) for n in REPLICATED] + [loss_part.reshape(1)], 8)
    *parts_early, parts_small = _exchange(g_early, [g_small], name="exchange_grads")
    results = [{}, {}, {}, {}]
    for (n, _), parts in zip([("w_in", 1)] + EARLY + LATE, [parts_w_in] + parts_early + list(parts_late)):
        for kind, a in enumerate(_adamw(parts, given[n], given["m_" + n], given["v_" + n], name="adamw_" + n)):
            results[kind][n] = a
    small = _adamw(parts_small, _pack([wts[n] for n in REPLICATED] + [zero], 8),
                   _pack([mom[n] for n in REPLICATED] + [zero], 8), _pack([var[n] for n in REPLICATED] + [zero], 8),
                   name="adamw_replicated")
    for kind in range(4):
        for n, a in zip(REPLICATED, _unpack(small[kind], small_shapes)):
            results[kind][n] = a
    loss = _unpack(small[0], small_shapes + [(1,)])[-1][0]
    out = [loss, grad_x[None]]
    for kind in range(4):
        out += [results[kind][n].reshape(given[n].shape) for n in WEIGHTS]
    return tuple(out)
```

```python
import functools

import jax
import jax.numpy as jnp
from jax import lax
from jax.experimental import pallas as pl
from jax.experimental.pallas import tpu as pltpu

F32 = jnp.float32
BF16 = jnp.bfloat16

N_DEV = 8
D_MODEL = 1024
N_META = 16
BLOCK = 128
PAD_ROWS = BLOCK - N_META
HEAD_DIM = 64
Q_HEADS = 8
KV_HEADS = 2
GROUP = Q_HEADS // KV_HEADS
ROPE_DIM = HEAD_DIM // 4
ROPE_HALF = ROPE_DIM // 2
ROPE_THETA = 500000.0
RW_HEADS = 8
RW_DIM = 512
DECAY_LORA = 64
AAA_LORA = 64
GATE_LORA = 160
D_FF = 2816
D_IN = 4640
RMS_EPS = 1e-6
RWKV_LN_EPS = 64e-5
NEG_INF = -1e30
SCAN_T = 16
SCAN_CHUNKS = 4
LANES = 128
PACK_ROWS = 256
TILE_ALL = 384
TILE_REAL = 512
TILE_WIDE = 256

ADAM_LR = 0.001
ADAM_B1 = 0.9
ADAM_B2 = 0.999
ADAM_EPS = 1e-08
ADAM_WD = 0.01
ADAM_STEP = 10

C_R, C_K, C_V, C_Q = 0, 512, 1024, 1536
C_G1, C_G2 = 2048, 3072
C_DG, C_KA, C_VA, C_DW, C_DA = 4096, 4352, 4480, 4608, 4736
NP = 5120

VMEM_LIMIT = 48 * 1024 * 1024


def _cparams(sem):
    return pltpu.CompilerParams(dimension_semantics=sem, vmem_limit_bytes=VMEM_LIMIT)


def _pick(n, cands):
    for c in cands:
        if n % c == 0:
            return c
    raise ValueError(f"no tile for {n}")


def _mm(a, b, *, ta=False, tb=False, bias=None, residual=None, name, scattered=()):
    m = a.shape[1] if ta else a.shape[0]
    k = a.shape[0] if ta else a.shape[1]
    n = b.shape[0] if tb else b.shape[1]
    assert k == (b.shape[1] if tb else b.shape[0]), (a.shape, b.shape, ta, tb)
    tm = _pick(m, (512, 1408, 256, 128) if ta else (1024, 528, 512, 384, 256, 128))
    tn = _pick(n, (1024, 512, 1408, 256, 128))
    if k <= 1024:
        tk = k
    else:
        tk = _pick(k, (1024, 1056, 528, 512) if (ta and not tb) else (1024, 1408, 512, 256, 128))
    nk = k // tk
    has_bias = bias is not None
    has_res = residual is not None
    dn = (((0 if ta else 1,), (1 if tb else 0,)), ((), ()))

    def body(*refs):
        a_ref, b_ref = refs[0], refs[1]
        pos = 2
        bias_ref = res_ref = None
        if has_bias:
            bias_ref = refs[pos]
            pos += 1
        if has_res:
            res_ref = refs[pos]
            pos += 1
        o_ref, acc_ref = refs[pos], refs[pos + 1]
        kk = pl.program_id(2)
        part = lax.dot_general(a_ref[...].astype(BF16), b_ref[...].astype(BF16), dn, preferred_element_type=F32)

        def finish(out):
            if has_bias:
                out = out + bias_ref[...]
            if has_res:
                out = out + res_ref[...]
            o_ref[...] = out

        if nk == 1:
            finish(part)
        else:
            @pl.when(kk == 0)
            def _():
                acc_ref[...] = part

            @pl.when((kk > 0) & (kk < nk - 1))
            def _():
                acc_ref[...] += part

            @pl.when(kk == nk - 1)
            def _():
                finish(acc_ref[...] + part)

    in_specs = [
        pl.BlockSpec((tk, tm), lambda i, j, kk: (kk, i)) if ta else pl.BlockSpec((tm, tk), lambda i, j, kk: (i, kk)),
        pl.BlockSpec((tn, tk), lambda i, j, kk: (j, kk)) if tb else pl.BlockSpec((tk, tn), lambda i, j, kk: (kk, j)),
    ]
    args = [a, b]
    if has_bias:
        in_specs.append(pl.BlockSpec((1, tn), lambda i, j, kk: (0, j)))
        args.append(bias)
    if has_res:
        in_specs.append(pl.BlockSpec((tm, tn), lambda i, j, kk: (i, j)))
        args.append(residual)
    grid = (m // tm, n // tn, nk)
    n_x = len(scattered)
    out = pl.pallas_call(
        _with_exchange(body, len(args), 1, 1, scattered, (), grid), name=name, grid=grid,
        in_specs=in_specs + [ANY] * n_x,
        out_specs=[pl.BlockSpec((tm, tn), lambda i, j, kk: (i, j))] + [ANY] * n_x,
        out_shape=[jax.ShapeDtypeStruct((m, n), F32)] + _exchange_shapes(scattered, ()),
        scratch_shapes=[pltpu.VMEM((tm, tn) if nk > 1 else (8, LANES), F32)] + (_exchange_sems(n_x) if n_x else []),
        compiler_params=_cparams(("arbitrary",) * 3 if n_x else ("parallel", "parallel", "arbitrary")),
    )(*args, *scattered)
    return out if n_x else out[0]


def _colsum(x, name):
    m, n = x.shape
    tm = _pick(m, (512, 384, BLOCK))

    def body(x_ref, o_ref):
        i = pl.program_id(0)
        s = jnp.sum(x_ref[...].astype(F32), axis=0, keepdims=True)

        @pl.when(i == 0)
        def _():
            o_ref[...] = s

        @pl.when(i > 0)
        def _():
            o_ref[...] += s

    return pl.pallas_call(
        body, name=name, grid=(m // tm,),
        in_specs=[pl.BlockSpec((tm, n), lambda i: (i, 0))],
        out_specs=pl.BlockSpec((1, n), lambda i: (0, 0)),
        out_shape=jax.ShapeDtypeStruct((1, n), F32),
        compiler_params=_cparams(("arbitrary",)),
    )(x)


def _view(arr, width=None, col=0, off=0, rows=BLOCK):
    return (arr, arr.shape[1] if width is None else width, col, off, rows)


def _row_spec(view):
    _, width, col, off, rows = view
    if off < 0:
        return pl.BlockSpec((rows, width), lambda i, col=col, off=off: (jnp.maximum(i + off, 0), col))
    return pl.BlockSpec((rows, width), lambda i, col=col, off=off: (i + off, col))


def _const_spec(arr):
    return pl.BlockSpec(arr.shape, lambda i: (0,) * arr.ndim)


def _retile(views, tile):
    assert all(v[3] == 0 and v[4] == BLOCK for v in views)
    return [v[:4] + (tile,) for v in views]


def _rows_fwd(fn, rows, consts, out_widths, *, nblk, name, out_dtype=F32, tile=BLOCK):
    nr, nc = len(rows), len(consts)
    tile = tile if (nblk * BLOCK) % tile == 0 else BLOCK
    if tile != BLOCK:
        rows, nblk = _retile(rows, tile), nblk * BLOCK // tile
    out_blocks = [(tile, w) if isinstance(w, int) else w for w in out_widths]

    def body(*refs):
        i = pl.program_id(0)
        vals = [r[...] for r in refs[:nr + nc]]
        outs = fn(i, *vals)
        for o_ref, o in zip(refs[nr + nc:], outs):
            o_ref[...] = o.astype(o_ref.dtype)

    return pl.pallas_call(
        body, name=name, grid=(nblk,),
        in_specs=[_row_spec(v) for v in rows] + [_const_spec(c) for c in consts],
        out_specs=[pl.BlockSpec(b, lambda i: (i, 0)) for b in out_blocks],
        out_shape=[jax.ShapeDtypeStruct((nblk * r, w), out_dtype) for r, w in out_blocks],
        compiler_params=_cparams(("parallel",)),
    )(*[v[0] for v in rows], *consts)


def _rows_bwd(fn, rows, consts, cts, *, nblk, name, diff_rows, diff_consts, acc=None, fwd_widths=(), row_dtype=F32,
              ct_map=None, tile=BLOCK):
    nr, nc = len(rows), len(consts)
    acc = acc or [None] * len(diff_rows)
    tile = tile if (nblk * BLOCK) % tile == 0 else BLOCK
    if tile != BLOCK:
        rows, nblk = _retile(rows, tile), nblk * BLOCK // tile
        cts = [c if c is None else _retile([c], tile)[0] for c in cts]
        acc = [a if a is None else _retile([a], tile)[0] for a in acc]
    ct_views = [c for c in cts if c is not None]
    acc_views = [a for a in acc if a is not None]
    n_in = nr + nc + len(ct_views) + len(acc_views)
    n_fwd = len(fwd_widths)

    def body(*refs):
        i = pl.program_id(0)
        row_vals = [r[...] for r in refs[:nr]]
        const_vals = [r[...] for r in refs[nr:nr + nc]]
        ct_vals = [r[...] for r in refs[nr + nc:nr + nc + len(ct_views)]]
        acc_vals = [r[...] for r in refs[nr + nc + len(ct_views):n_in]]
        out_refs = refs[n_in:]

        def f(*dargs):
            rv = list(row_vals)
            cv = list(const_vals)
            for pos, idx in enumerate(diff_rows):
                rv[idx] = dargs[pos]
            for pos, idx in enumerate(diff_consts):
                cv[idx] = dargs[len(diff_rows) + pos]
            return tuple(fn(i, *rv, *cv))

        primals = [row_vals[idx] for idx in diff_rows] + [const_vals[idx] for idx in diff_consts]
        outs, pull = jax.vjp(f, *primals)
        full_ct, ci = [], 0
        if ct_map is not None:
            full_ct = ct_map(i, *ct_vals)
        else:
            for o, c in zip(outs, cts):
                if c is None:
                    full_ct.append(jnp.zeros_like(o))
                else:
                    full_ct.append(ct_vals[ci])
                    ci += 1
        grads = pull(tuple(full_ct))
        for o_ref, o in zip(out_refs[:n_fwd], outs):
            o_ref[...] = o
        ai = 0
        for pos in range(len(diff_rows)):
            g = grads[pos]
            if acc[pos] is not None:
                g = g + acc_vals[ai]
                ai += 1
            out_refs[n_fwd + pos][...] = g.astype(row_dtype)
        for pos in range(len(diff_consts)):
            g = grads[len(diff_rows) + pos]
            o_ref = out_refs[n_fwd + len(diff_rows) + pos]

            @pl.when(i == 0)
            def _(o_ref=o_ref, g=g):
                o_ref[...] = g

            @pl.when(i > 0)
            def _(o_ref=o_ref, g=g):
                o_ref[...] += g

    out_specs = [pl.BlockSpec((tile, w), lambda i: (i, 0)) for w in fwd_widths]
    out_shape = [jax.ShapeDtypeStruct((nblk * tile, w), F32) for w in fwd_widths]
    for idx in diff_rows:
        out_specs.append(pl.BlockSpec((tile, rows[idx][1]), lambda i: (i, 0)))
        out_shape.append(jax.ShapeDtypeStruct((nblk * tile, rows[idx][1]), row_dtype))
    for idx in diff_consts:
        out_specs.append(_const_spec(consts[idx]))
        out_shape.append(jax.ShapeDtypeStruct(consts[idx].shape, F32))
    return pl.pallas_call(
        body, name=name, grid=(nblk,),
        in_specs=([_row_spec(v) for v in rows] + [_const_spec(c) for c in consts]
                  + [_row_spec(v) for v in ct_views] + [_row_spec(v) for v in acc_views]),
        out_specs=out_specs, out_shape=out_shape,
        compiler_params=_cparams(("arbitrary",)),
    )(*[v[0] for v in rows], *consts, *[v[0] for v in ct_views], *[v[0] for v in acc_views])


def _rms_fn(i, x, g):
    return (x * lax.rsqrt(jnp.mean(x * x, axis=-1, keepdims=True) + RMS_EPS) * g,)


def _sigmoid(x):
    return 1.0 / (1.0 + jnp.exp(-x))


def _softplus(x):
    return jnp.maximum(x, 0.0) + jnp.log(1.0 + jnp.exp(-jnp.abs(x)))


def _split2(x):
    hi = x.astype(BF16)
    lo = (x - hi.astype(F32)).astype(BF16)
    return jnp.concatenate([hi, lo], axis=1)


@jax.custom_vjp
def _head_sum(x):
    r = lax.broadcasted_iota(jnp.int32, (2 * RW_DIM, RW_DIM), 0) % RW_DIM // HEAD_DIM
    c = lax.broadcasted_iota(jnp.int32, (2 * RW_DIM, RW_DIM), 1) // HEAD_DIM
    return jnp.dot(_split2(x), (r == c).astype(BF16), preferred_element_type=F32)


_head_sum.defvjp(lambda x: (_head_sum(x), None), lambda _, ct: (_head_sum(ct),))


@jax.custom_vjp
def _dot_bf16(x, w):
    return jnp.dot(x.astype(BF16), w.astype(BF16), preferred_element_type=F32)


def _dot_bf16_bwd(res, ct):
    x, w = res
    ct = ct.astype(BF16)
    dx = lax.dot_general(ct, w.astype(BF16), (((1,), (1,)), ((), ())), preferred_element_type=F32)
    dw = lax.dot_general(x.astype(BF16), ct, (((0,), (0,)), ((), ())), preferred_element_type=F32)
    return dx, dw


_dot_bf16.defvjp(lambda x, w: (_dot_bf16(x, w), (x, w)), _dot_bf16_bwd)


def _rwkv_pre_fn(i, r, k, v, dw, da, dg, r_p, k_p, v_p, dw_p, da_p, dg_p,
                 mix_r, mix_k, mix_v, mix_dw, mix_da, mix_dg, w0, w2, a0, a2, g2, k_k, k_a):
    row = i * BLOCK + lax.broadcasted_iota(jnp.int32, (BLOCK, 1), 0)
    live = row >= PAD_ROWS
    live_prev = row >= PAD_ROWS + 1

    def shift(cur, prev, mix):
        cur = jnp.where(live, cur, 0.0)
        prev = jnp.where(live_prev, prev, 0.0)
        return cur + (prev - cur) * mix

    r = shift(r, r_p, mix_r)
    k = shift(k, k_p, mix_k)
    v = shift(v, v_p, mix_v)
    dw = shift(dw, dw_p, mix_dw)
    da = shift(da, da_p, mix_da)
    dg = shift(dg, dg_p, mix_dg)
    wlog = -_softplus(-(w0 + _dot_bf16(jnp.tanh(dw), w2))) - 0.5
    decay = jnp.exp(-jnp.exp(wlog))
    a = _sigmoid(a0 + _dot_bf16(da, a2))
    g = _dot_bf16(_sigmoid(dg), g2)
    kk = k * k_k
    norm_sq = jnp.where(live, _head_sum(kk * kk), 1.0)
    kk = kk / jnp.maximum(jnp.sqrt(norm_sq), 1e-12)
    k_mod = k * (1.0 + (a - 1.0) * k_a)
    return r, decay, k_mod, v, -kk, kk * a, g


def _rwkv_pre_xt_fn(i, *args):
    r, decay, k_mod, v, a_neg, b, g = _rwkv_pre_fn(i, *args)
    t = SCAN_T
    xt = jnp.concatenate([_rows_to_xt(x[c * t:(c + 1) * t]) for c in range(BLOCK // t)
                          for x in (a_neg, decay, b, k_mod, r)], axis=0)
    return r, k_mod, v, g, xt


def _rwkv_pre_cts(i, dxt, dv_s, dr_p, dk_p, dv_p, dg_p):
    t = SCAN_T
    d_a, d_w, d_b, d_k, d_r = [
        jnp.concatenate([_xt_to_rows(dxt[c * VEC_ROWS + n * HEAD_DIM:c * VEC_ROWS + (n + 1) * HEAD_DIM])
                         for c in range(BLOCK // t)], axis=0) for n in range(N_VEC)]
    dr_p, dk_p, dv_p, dg_p = [jnp.where(i > 0, x, 0.0) for x in (dr_p, dk_p, dv_p, dg_p)]
    return d_r + dr_p, d_w, d_k + dk_p, dv_s + dv_p, d_a, d_b, dg_p


def _rwkv_post_fn(i, ys, r, k_mod, v, g, ln_w, ln_b, r_k):
    mean = _head_sum(ys) * (1.0 / HEAD_DIM)
    d = ys - mean
    var = _head_sum(d * d) * (1.0 / HEAD_DIM)
    yn = d * lax.rsqrt(var + RWKV_LN_EPS) * ln_w + ln_b
    bonus = _head_sum(r * k_mod * r_k) * v
    return ((yn + bonus) * g,)


def _merge_fn(i, ya, yr, g1, g2):
    return (_sigmoid(g1) * ya + _sigmoid(g2) * yr,)


def _swiglu_fn(i, gate, up):
    return (gate * _sigmoid(gate) * up,)


def _loss_fn(i, h, tgt, g):
    y = h * lax.rsqrt(jnp.mean(h * h, axis=-1, keepdims=True) + RMS_EPS) * g
    err = y - tgt
    return (0.5 * jnp.mean(err * err, axis=-1, keepdims=True),)


def _rope_tables(lp):
    pos = (jnp.arange(lp, dtype=jnp.int32) - PAD_ROWS).astype(F32)
    inv_freq = jnp.power(jnp.float32(ROPE_THETA), -jnp.arange(ROPE_HALF, dtype=F32) * (2.0 / ROPE_DIM))
    ang = pos[:, None] * inv_freq[None, :]
    cos, sin = jnp.cos(ang), jnp.sin(ang)
    one = jnp.ones((lp, HEAD_DIM - ROPE_DIM), F32)
    zero_h = jnp.zeros((lp, ROPE_HALF), F32)
    zero_r = jnp.zeros((lp, HEAD_DIM - ROPE_DIM), F32)
    c = jnp.concatenate([cos, cos, one], axis=1)
    s1 = jnp.concatenate([-sin, zero_h, zero_r], axis=1)
    s2 = jnp.concatenate([zero_h, sin, zero_r], axis=1)
    return tuple(jnp.tile(t, (1, LANES // HEAD_DIM)) for t in (c, s1, s2))


def _rope_fwd_fn(i, x, c, s1, s2):
    n = x.shape[1]
    c, s1, s2 = [jnp.tile(t, (1, n // LANES)) for t in (c, s1, s2)]
    return (x * c + pltpu.roll(x, n - ROPE_HALF, 1) * s1 + pltpu.roll(x, ROPE_HALF, 1) * s2,)


def _rope_bwd_fn(i, dy, c, s1, s2):
    n = dy.shape[1]
    c, s1, s2 = [jnp.tile(t, (1, n // LANES)) for t in (c, s1, s2)]
    return (dy * c + pltpu.roll(dy * s1, ROPE_HALF, 1) + pltpu.roll(dy * s2, n - ROPE_HALF, 1),)


def _attn_mask(i):
    r = lax.broadcasted_iota(jnp.int32, (BLOCK, 3 * BLOCK), 0)
    c = lax.broadcasted_iota(jnp.int32, (BLOCK, 3 * BLOCK), 1)
    meta = (c < BLOCK) & (c >= PAD_ROWS)
    prev = (c >= BLOCK) & (c < 2 * BLOCK) & ((c - BLOCK) > r) & (i >= 1)
    cur = (c >= 2 * BLOCK) & ((c - 2 * BLOCK) <= r)
    return meta | prev | cur


def _attn_rows(ref, g):
    return ref[:, g * HEAD_DIM:(g + 1) * HEAD_DIM]


def _attn_group(i, g, q_all, k_refs, v_refs, s_ref):
    heads = range(g * GROUP, (g + 1) * GROUP)
    kcat = jnp.concatenate([_attn_rows(r, g) for r in k_refs], axis=0).astype(BF16)
    vcat = jnp.concatenate([_attn_rows(r, g) for r in v_refs], axis=0).astype(BF16)
    qg = jnp.concatenate([q_all[:, h * HEAD_DIM:(h + 1) * HEAD_DIM] for h in heads], axis=0).astype(BF16)
    sink = jnp.concatenate([jnp.broadcast_to(s_ref[0:1, h:h + 1], (BLOCK, 1)) for h in heads], axis=0)
    s = lax.dot_general(qg, kcat, (((1,), (1,)), ((), ())), preferred_element_type=F32) * (HEAD_DIM ** -0.5)
    valid = jnp.concatenate([_attn_mask(i)] * GROUP, axis=0)
    return heads, qg, kcat, vcat, sink, jnp.where(valid, s, NEG_INF)


def _attn_specs(v_col):
    blk = lambda f: pl.BlockSpec((BLOCK, 2 * HEAD_DIM), f)
    keys = [blk(lambda i: (0, 0)), blk(lambda i: (i, 0)), blk(lambda i: (i + 1, 0))]
    vals = [blk(lambda i: (0, v_col)), blk(lambda i: (i, v_col)), blk(lambda i: (i + 1, v_col))]
    return keys + vals + [pl.BlockSpec((1, Q_HEADS), lambda i: (0, 0))]


def _attn_fwd(q, k, v, v_col, sinks, *, nblk, name):
    def body(q_ref, km_ref, kp_ref, kc_ref, vm_ref, vp_ref, vc_ref, s_ref, o_ref, lse_ref):
        i = pl.program_id(0)
        q_all = q_ref[...]
        for g in range(KV_HEADS):
            heads, _, _, vcat, sink, s = _attn_group(i, g, q_all, (km_ref, kp_ref, kc_ref), (vm_ref, vp_ref, vc_ref),
                                                     s_ref)
            m = jnp.maximum(jnp.max(s, axis=-1, keepdims=True), sink)
            p = jnp.exp(s - m)
            den = jnp.sum(p, axis=-1, keepdims=True) + jnp.exp(sink - m)
            o = jnp.dot(p.astype(BF16), vcat, preferred_element_type=F32) / den
            lse = m + jnp.log(den)
            for n, h in enumerate(heads):
                o_ref[:, h * HEAD_DIM:(h + 1) * HEAD_DIM] = o[n * BLOCK:(n + 1) * BLOCK]
                lse_ref[:, h:h + 1] = lse[n * BLOCK:(n + 1) * BLOCK]

    return pl.pallas_call(
        body, name=name, grid=(nblk,),
        in_specs=[pl.BlockSpec((BLOCK, Q_HEADS * HEAD_DIM), lambda i: (i + 1, 0))] + _attn_specs(v_col),
        out_specs=[pl.BlockSpec((BLOCK, Q_HEADS * HEAD_DIM), lambda i: (i, 0)),
                   pl.BlockSpec((BLOCK, Q_HEADS), lambda i: (i, 0))],
        out_shape=[jax.ShapeDtypeStruct((nblk * BLOCK, Q_HEADS * HEAD_DIM), F32),
                   jax.ShapeDtypeStruct((nblk * BLOCK, Q_HEADS), F32)],
        compiler_params=_cparams(("parallel",)),
    )(q, k, k, k, v, v, v, sinks)


def _attn_bwd(q, k, v, v_col, sinks, o, lse, do, *, nblk, name):
    lp = k.shape[0]

    def body(q_ref, km_ref, kp_ref, kc_ref, vm_ref, vp_ref, vc_ref, s_ref, o_ref, lse_ref, do_ref,
             dq_ref, dk_ref, dv_ref, ds_ref):
        i = pl.program_id(0)

        @pl.when(i == 0)
        def _():
            dk_ref[...] = jnp.zeros_like(dk_ref)
            dv_ref[...] = jnp.zeros_like(dv_ref)
            ds_ref[...] = jnp.zeros_like(ds_ref)

        lane = lax.broadcasted_iota(jnp.int32, (1, Q_HEADS), 1)
        prev_rows = pl.ds(pl.multiple_of(i * BLOCK, BLOCK), BLOCK)
        cur_rows = pl.ds(pl.multiple_of((i + 1) * BLOCK, BLOCK), BLOCK)
        q_all, o_all, do_all, lse_all = q_ref[...], o_ref[...], do_ref[...], lse_ref[...]
        for g in range(KV_HEADS):
            heads, qg, kcat, vcat, sink, s = _attn_group(i, g, q_all, (km_ref, kp_ref, kc_ref),
                                                         (vm_ref, vp_ref, vc_ref), s_ref)
            stack = lambda x: jnp.concatenate([x[:, h * HEAD_DIM:(h + 1) * HEAD_DIM] for h in heads], axis=0)
            lse_g = jnp.concatenate([lse_all[:, h:h + 1] for h in heads], axis=0)
            do_g = stack(do_all)
            p = jnp.exp(s - lse_g)
            delta = jnp.sum(do_g * stack(o_all), axis=-1, keepdims=True)
            dp = lax.dot_general(do_g.astype(BF16), vcat, (((1,), (1,)), ((), ())), preferred_element_type=F32)
            dsc = (p * (dp - delta) * (HEAD_DIM ** -0.5)).astype(BF16)
            dq = jnp.dot(dsc, kcat, preferred_element_type=F32)
            dk_all = lax.dot_general(dsc, qg, (((0,), (0,)), ((), ())), preferred_element_type=F32)
            dv_all = lax.dot_general(p.astype(BF16), do_g.astype(BF16), (((0,), (0,)), ((), ())),
                                     preferred_element_type=F32)
            cols = slice(g * HEAD_DIM, (g + 1) * HEAD_DIM)
            for ref, full in ((dk_ref, dk_all), (dv_ref, dv_all)):
                ref[0:BLOCK, cols] += full[0:BLOCK]
                ref[prev_rows, cols] += full[BLOCK:2 * BLOCK]
                ref[cur_rows, cols] += full[2 * BLOCK:]
            sink_part = jnp.exp(sink - lse_g) * delta
            for n, h in enumerate(heads):
                dq_ref[:, h * HEAD_DIM:(h + 1) * HEAD_DIM] = dq[n * BLOCK:(n + 1) * BLOCK]
                dsink = -jnp.sum(sink_part[n * BLOCK:(n + 1) * BLOCK], axis=0, keepdims=True)
                ds_ref[...] += jnp.where(lane == h, dsink, 0.0)

    qspec = pl.BlockSpec((BLOCK, Q_HEADS * HEAD_DIM), lambda i: (i, 0))
    whole = pl.BlockSpec((lp, 2 * HEAD_DIM), lambda i: (0, 0))
    return pl.pallas_call(
        body, name=name, grid=(nblk,),
        in_specs=([pl.BlockSpec((BLOCK, Q_HEADS * HEAD_DIM), lambda i: (i + 1, 0))] + _attn_specs(v_col)
                  + [qspec, pl.BlockSpec((BLOCK, Q_HEADS), lambda i: (i, 0)), qspec]),
        out_specs=[qspec, whole, whole, pl.BlockSpec((1, Q_HEADS), lambda i: (0, 0))],
        out_shape=[jax.ShapeDtypeStruct((nblk * BLOCK, Q_HEADS * HEAD_DIM), F32),
                   jax.ShapeDtypeStruct((lp, 2 * HEAD_DIM), F32), jax.ShapeDtypeStruct((lp, 2 * HEAD_DIM), F32),
                   jax.ShapeDtypeStruct((1, Q_HEADS), F32)],
        compiler_params=_cparams(("arbitrary",)),
    )(q, k, k, k, v, v, v, sinks, o, lse, do)


N_VEC = 5
VEC_ROWS = N_VEC * HEAD_DIM


def _selectors():
    t = SCAN_T
    shape = (t, 2 * LANES, RW_DIM)
    step, src, dst = [lax.broadcasted_iota(jnp.int32, shape, d) for d in range(3)]
    src = src % LANES
    spread = ((src // t == dst // HEAD_DIM) & (src % t == step)).astype(BF16)
    shape = (t, RW_DIM, LANES)
    step, src, dst = [lax.broadcasted_iota(jnp.int32, shape, d) for d in range(3)]
    collect = ((src // HEAD_DIM == dst // t) & (dst % t == step)).astype(BF16)
    return spread, collect


def _rows_to_xt(x):
    low = lax.broadcasted_iota(jnp.int32, (SCAN_T, LANES), 1) < HEAD_DIM
    pieces = []
    for m in range(RW_HEADS // 2):
        pair = x[:, m * LANES:(m + 1) * LANES]
        pieces += [jnp.where(low, pair, 0.0), jnp.where(low, pltpu.roll(pair, HEAD_DIM, 1), 0.0)]
    return jnp.concatenate(pieces, axis=0).T[:HEAD_DIM]


def _xt_to_rows(a):
    t = SCAN_T
    a_t = jnp.concatenate([a, jnp.zeros_like(a)], axis=0).T
    pairs = [a_t[2 * m * t:(2 * m + 1) * t] + pltpu.roll(a_t[(2 * m + 1) * t:(2 * m + 2) * t], HEAD_DIM, 1)
             for m in range(RW_HEADS // 2)]
    return jnp.concatenate(pairs, axis=1)


def _with_exchange(compute, n_in, n_out, n_scratch, scattered, shared, grid):
    n_sc = len(scattered)
    n_x = n_sc + len(shared)
    if n_x == 0:
        return compute

    def body(*refs):
        ins, x_in = refs[:n_in], refs[n_in:n_in + n_x]
        outs, x_out = refs[n_in + n_x:n_in + n_x + n_out], refs[n_in + n_x + n_out:n_in + 2 * n_x + n_out]
        scratch = refs[n_in + 2 * n_x + n_out:n_in + 2 * n_x + n_out + n_scratch]
        sems = refs[n_in + 2 * n_x + n_out + n_scratch:]

        first = last = True
        for d, size in enumerate(grid):
            first = first & (pl.program_id(d) == 0)
            last = last & (pl.program_id(d) == size - 1)

        @pl.when(first)
        def _():
            for cp in _exchange_copies(x_in, x_out, n_sc, *sems):
                cp.start()

        compute(*ins, *outs, *scratch)

        @pl.when(last)
        def _():
            for cp in _exchange_copies(x_in, x_out, n_sc, *sems):
                cp.wait()

    return body


def _wkv_fwd(xt, v, spread, name, shared=()):
    t_steps = SCAN_T
    nch = xt.shape[0]
    per = SCAN_CHUNKS
    grid = (nch // per,)
    rows = per * t_steps
    n_x = len(shared)

    def compute(xt_ref, v_ref, sel_ref, y_ref, hist_ref, st_ref):
        @pl.when(pl.program_id(0) == 0)
        def _():
            st_ref[...] = jnp.zeros_like(st_ref)

        st = st_ref[...]
        for c in range(per):
            x2 = _split2(xt_ref[c])
            for j in range(t_steps):
                row = c * t_steps + j
                cols = jnp.dot(x2, sel_ref[j], preferred_element_type=F32)
                a_c, w_c, b_c, k_c, r_c = [cols[n * HEAD_DIM:(n + 1) * HEAD_DIM] for n in range(N_VEC)]
                hist_ref[row] = st
                sa = jnp.sum(st * a_c, axis=0, keepdims=True)
                st = st * w_c + b_c * sa + k_c * v_ref[row:row + 1, :]
                y_ref[row:row + 1, :] = jnp.sum(st * r_c, axis=0, keepdims=True)
        st_ref[...] = st

    return pl.pallas_call(
        _with_exchange(compute, 3, 2, 1, (), shared, grid), name=name, grid=grid,
        in_specs=[pl.BlockSpec((per, VEC_ROWS, LANES), lambda c: (c, 0, 0)),
                  pl.BlockSpec((rows, RW_DIM), lambda c: (c, 0)),
                  pl.BlockSpec(spread.shape, lambda c: (0, 0, 0))] + [ANY] * n_x,
        out_specs=[pl.BlockSpec((rows, RW_DIM), lambda c: (c, 0)),
                   pl.BlockSpec((rows, HEAD_DIM, RW_DIM), lambda c: (c, 0, 0))] + [ANY] * n_x,
        out_shape=[jax.ShapeDtypeStruct((nch * t_steps, RW_DIM), F32),
                   jax.ShapeDtypeStruct((nch * t_steps, HEAD_DIM, RW_DIM), F32)] + _exchange_shapes((), shared),
        scratch_shapes=[pltpu.VMEM((HEAD_DIM, RW_DIM), F32)] + (_exchange_sems(n_x) if n_x else []),
        compiler_params=_cparams(("arbitrary",)),
    )(xt, v, spread, *shared)


def _wkv_bwd(xt, v, hist, dy, spread, collect, name, scattered=()):
    t_steps = SCAN_T
    nch = xt.shape[0]
    n_x = len(scattered)
    per = SCAN_CHUNKS
    nsteps = nch // per
    grid = (nsteps,)
    rows = per * t_steps
    lead = BLOCK // rows

    def compute(xt_ref, v_ref, hist_ref, dy_ref, sel_ref, col_ref, dxt_ref, dv_ref, g_ref):
        @pl.when(pl.program_id(0) == 0)
        def _():
            g_ref[...] = jnp.zeros_like(g_ref)

        has_dy = nsteps - 1 - pl.program_id(0) >= lead
        gst = g_ref[...]
        nxt = None
        for c in reversed(range(per)):
            x2 = _split2(xt_ref[c])
            acc = jnp.zeros((VEC_ROWS, LANES), F32)
            for j in reversed(range(t_steps)):
                row = c * t_steps + j
                cols = jnp.dot(x2, sel_ref[j], preferred_element_type=F32)
                a_c, w_c, b_c, k_c, r_c = [cols[n * HEAD_DIM:(n + 1) * HEAD_DIM] for n in range(N_VEC)]
                prev = hist_ref[row]
                v_row = v_ref[row:row + 1, :]
                dy_row = jnp.where(has_dy, dy_ref[row:row + 1, :], 0.0)
                sa = jnp.sum(prev * a_c, axis=0, keepdims=True)
                if nxt is None:
                    nxt = prev * w_c + b_c * sa + k_c * v_row
                gst = gst + r_c * dy_row
                dv_ref[row:row + 1, :] = jnp.sum(gst * k_c, axis=0, keepdims=True)
                dsa = jnp.sum(gst * b_c, axis=0, keepdims=True)
                prods = jnp.concatenate([p.astype(BF16) for p in
                                         (prev * dsa, gst * prev, gst * sa, gst * v_row, nxt * dy_row)], axis=0)
                acc = acc + jnp.dot(prods, col_ref[j], preferred_element_type=F32)
                gst = gst * w_c + a_c * dsa
                nxt = prev
            dxt_ref[c] = acc
        g_ref[...] = gst

    rev3 = lambda c: (nsteps - 1 - c, 0, 0)
    rev2 = lambda c: (nsteps - 1 - c, 0)
    rowspec = pl.BlockSpec((rows, RW_DIM), rev2)
    return pl.pallas_call(
        _with_exchange(compute, 6, 2, 1, scattered, (), grid), name=name, grid=grid,
        in_specs=[pl.BlockSpec((per, VEC_ROWS, LANES), rev3), rowspec,
                  pl.BlockSpec((rows, HEAD_DIM, RW_DIM), rev3),
                  pl.BlockSpec((rows, RW_DIM), lambda c: (jnp.maximum(nsteps - 1 - c - lead, 0), 0)),
                  pl.BlockSpec(spread.shape, lambda c: (0, 0, 0)),
                  pl.BlockSpec(collect.shape, lambda c: (0, 0, 0))] + [ANY] * n_x,
        out_specs=[pl.BlockSpec((per, VEC_ROWS, LANES), rev3), rowspec] + [ANY] * n_x,
        out_shape=[jax.ShapeDtypeStruct((nch, VEC_ROWS, LANES), F32),
                   jax.ShapeDtypeStruct((nch * t_steps, RW_DIM), F32)] + _exchange_shapes(scattered, ()),
        scratch_shapes=[pltpu.VMEM((HEAD_DIM, RW_DIM), F32)] + (_exchange_sems(n_x) if n_x else []),
        compiler_params=_cparams(("arbitrary",)),
    )(xt, v, hist, dy, spread, collect, *scattered)


MESH = pl.DeviceIdType.MESH
ANY = pl.BlockSpec(memory_space=pl.ANY)


def _all_gather(arrays, name):
    n_arr = len(arrays)
    per = N_DEV - 1

    def body(*refs):
        x_refs, out_refs = refs[:n_arr], refs[n_arr:2 * n_arr]
        send_sems, recv_sems, local_sems = refs[2 * n_arr:]
        xi, yi, ci = lax.axis_index("x"), lax.axis_index("y"), lax.axis_index("c")
        me, sibling = (xi, yi, ci), (xi, yi, 1 - ci)
        chips = [(1 - xi, yi), (xi, 1 - yi), (1 - xi, 1 - yi)]

        def slot(a, px, py, pc):
            return out_refs[a].at[4 * px + 2 * py + pc]

        def copy(a, sem, block, to, src=None):
            return pltpu.make_async_remote_copy(
                src_ref=slot(a, *block) if src is None else src, dst_ref=slot(a, *block),
                send_sem=send_sems.at[per * a + sem], recv_sem=recv_sems.at[per * a + sem],
                device_id=to, device_id_type=MESH)

        mine = [pltpu.make_async_copy(x_refs[a], slot(a, *me), local_sems.at[a]) for a in range(n_arr)]
        for cp in mine:
            cp.start()
        sent = []
        for a in range(n_arr):
            sent.append(copy(a, 0, me, sibling, src=x_refs[a]))
            sent += [copy(a, 1 + j, me, (*chip, ci), src=x_refs[a]) for j, chip in enumerate(chips)]
        for cp in sent:
            cp.start()
        for j, chip in enumerate(chips):
            for a in range(n_arr):
                copy(a, 1 + j, (*chip, ci), me).wait_recv()
                onward = copy(a, 4 + j, (*chip, ci), sibling)
                onward.start()
                sent.append(onward)
        for a in range(n_arr):
            copy(a, 0, sibling, me).wait_recv()
        for j, chip in enumerate(chips):
            for a in range(n_arr):
                copy(a, 4 + j, (*chip, 1 - ci), me).wait_recv()
        for cp in sent:
            cp.wait_send()
        for cp in mine:
            cp.wait()

    sems = pltpu.SemaphoreType.DMA((per * n_arr,))
    return pl.pallas_call(
        body, name=name, out_shape=[jax.ShapeDtypeStruct((N_DEV,) + a.shape, a.dtype) for a in arrays],
        in_specs=[ANY] * n_arr, out_specs=[ANY] * n_arr,
        scratch_shapes=[sems, sems, pltpu.SemaphoreType.DMA((n_arr,))],
    )(*arrays)


def _exchange_copies(in_refs, out_refs, n_scattered, send_sems, recv_sems, local_sems):
    n_arr = len(in_refs)
    per = N_DEV - 1
    xi, yi, ci = lax.axis_index("x"), lax.axis_index("y"), lax.axis_index("c")
    me = 4 * xi + 2 * yi + ci
    src_of = lambda a, peer: in_refs[a].at[peer] if a < n_scattered else in_refs[a]
    copies = []
    for d in range(1, N_DEV):
        px = 1 - xi if d & 4 else xi
        py = 1 - yi if d & 2 else yi
        pc = 1 - ci if d & 1 else ci
        for a in range(n_arr):
            copies.append(pltpu.make_async_remote_copy(
                src_ref=src_of(a, 4 * px + 2 * py + pc), dst_ref=out_refs[a].at[me],
                send_sem=send_sems.at[per * a + d - 1], recv_sem=recv_sems.at[per * a + d - 1],
                device_id=(px, py, pc), device_id_type=MESH))
    own = [pltpu.make_async_copy(src_of(a, me), out_refs[a].at[me], local_sems.at[a]) for a in range(n_arr)]
    return copies + own


def _exchange_shapes(scattered, shared):
    return ([jax.ShapeDtypeStruct(a.shape, a.dtype) for a in scattered]
            + [jax.ShapeDtypeStruct((N_DEV,) + a.shape, a.dtype) for a in shared])


def _exchange_sems(n_arr):
    sems = pltpu.SemaphoreType.DMA(((N_DEV - 1) * n_arr,))
    return [sems, sems, pltpu.SemaphoreType.DMA((n_arr,))]


def _exchange(scattered, shared, name):
    n_sc = len(scattered)
    n_arr = n_sc + len(shared)

    def body(*refs):
        copies = _exchange_copies(refs[:n_arr], refs[n_arr:2 * n_arr], n_sc, *refs[2 * n_arr:])
        for cp in copies:
            cp.start()
        for cp in copies:
            cp.wait()

    return pl.pallas_call(
        body, name=name, out_shape=_exchange_shapes(scattered, shared),
        in_specs=[ANY] * n_arr, out_specs=[ANY] * n_arr, scratch_shapes=_exchange_sems(n_arr),
    )(*scattered, *shared)


def _adamw(parts, w, m, v, name):
    rows, cols = w.shape[-2:]
    tile = PACK_ROWS if rows % PACK_ROWS == 0 else rows
    at = (0,) if w.ndim == 3 else (Ellipsis,)

    def body(p_ref, w_ref, m_ref, v_ref, g_out, d_out, m_out, v_out):
        g = p_ref[0].astype(F32)
        for s in range(1, N_DEV):
            g = g + p_ref[s].astype(F32)
        m_new = ADAM_B1 * m_ref[at] + (1.0 - ADAM_B1) * g
        v_new = ADAM_B2 * v_ref[at] + (1.0 - ADAM_B2) * (g * g)
        m_hat = m_new / (1.0 - ADAM_B1 ** ADAM_STEP)
        v_hat = v_new / (1.0 - ADAM_B2 ** ADAM_STEP)
        g_out[at] = g
        d_out[at] = -ADAM_LR * (m_hat / (jnp.sqrt(v_hat) + ADAM_EPS) + ADAM_WD * w_ref[at])
        m_out[at] = m_new
        v_out[at] = v_new

    spec = (pl.BlockSpec((1, tile, cols), lambda i: (0, i, 0)) if w.ndim == 3
            else pl.BlockSpec((tile, cols), lambda i: (i, 0)))
    return pl.pallas_call(
        body, name=name, grid=(rows // tile,),
        in_specs=[pl.BlockSpec((N_DEV, tile, cols), lambda i: (0, i, 0)), spec, spec, spec],
        out_specs=[spec] * 4, out_shape=[jax.ShapeDtypeStruct(w.shape, F32)] * 4,
        compiler_params=_cparams(("parallel",)),
    )(parts, w, m, v)


EARLY = [("meta_tokens", 1), ("rwkv_w2", 1), ("rwkv_a2", 1), ("rwkv_g2", 1)]
LATE = [("w_br_attn", 1), ("w_br_rwkv", 1), ("w_o", 0), ("w_ffn_gate", 1), ("w_ffn_up", 1), ("w_ffn_down", 0)]
REPLICATED = ["norm_mix_g", "b_in", "attn_sinks", "rwkv_mix", "rwkv_w0", "rwkv_a0", "rwkv_k_k", "rwkv_k_a",
              "rwkv_r_k", "rwkv_ln_w", "rwkv_ln_b", "norm_ffn_g", "norm_final_g"]
WEIGHTS = ["meta_tokens", "norm_mix_g", "w_in", "b_in", "attn_sinks", "rwkv_mix", "rwkv_w0", "rwkv_w2", "rwkv_a0",
           "rwkv_a2", "rwkv_g2", "rwkv_k_k", "rwkv_k_a", "rwkv_r_k", "rwkv_ln_w", "rwkv_ln_b", "w_br_attn",
           "w_br_rwkv", "w_o", "norm_ffn_g", "w_ffn_gate", "w_ffn_up", "w_ffn_down", "norm_final_g"]


def _pack(arrays, row_multiple):
    flat = jnp.concatenate([a.reshape(-1) for a in arrays])
    per = row_multiple * LANES
    total = -(-flat.shape[0] // per) * per
    return jnp.pad(flat, (0, total - flat.shape[0])).reshape(-1, LANES)


def _unpack(buf, shapes):
    flat = buf.reshape(-1)
    out, pos = [], 0
    for s in shapes:
        n = 1
        for d in s:
            n *= d
        out.append(flat[pos:pos + n].reshape(s))
        pos += n
    return out


def _strip(name, a):
    return a if name in ("meta_tokens", "norm_final_g") else a[0]


def _join(gathered, axis):
    if axis == 0:
        return gathered.reshape(-1, gathered.shape[2])
    return gathered.transpose(1, 0, 2).reshape(gathered.shape[1], -1)


def _split(g, axis):
    if axis == 0:
        return g.reshape(N_DEV, -1, g.shape[1])
    return g.reshape(g.shape[0], N_DEV, -1).transpose(1, 0, 2)


W_IN_LAYOUT = [(768, 2304), (0, 512), (2592, 4640), (2432, 2592), 256 - GATE_LORA, (512, 768), (2304, 2368),
               128 - DECAY_LORA, (2368, 2432), 128 - AAA_LORA, NP - C_DA - 128]


def _w_in_padded(w, shard_width=None):
    rows = w.shape[-2]
    width = D_IN if shard_width is None else shard_width
    parts = []
    for seg in W_IN_LAYOUT:
        if isinstance(seg, int):
            parts.append(jnp.zeros((rows, seg), w.dtype))
            continue
        lo, stop = seg
        while lo < stop:
            p = lo // width
            hi = min(stop, (p + 1) * width)
            src = w if shard_width is None else w[p]
            parts.append(src[:, lo - p * width:hi - p * width])
            lo = hi
    return jnp.concatenate(parts, axis=1)


def _w_in_unpadded(wp, lo=0, stop=D_IN):
    spans, pos = [], 0
    for seg in W_IN_LAYOUT:
        if isinstance(seg, int):
            pos += seg
        else:
            spans.append((seg[0], seg[1], pos))
            pos += seg[1] - seg[0]
    parts = []
    for a, b, at in sorted(spans):
        c, d = max(a, lo), min(b, stop)
        if c < d:
            parts.append(wp[:, at + c - a:at + d - a])
    return jnp.concatenate(parts, axis=1)


def _pad_rows(a, n):
    return jnp.pad(a, ((0, n - a.shape[0]), (0, 0)))


def _device_step(x, tgt, full, gather_late=None, scatter_early=None, scatter_last=None):
    seq = x.shape[0]
    nblk = seq // BLOCK
    lp = seq + BLOCK
    nall = nblk + 1

    w_in_p = full["w_in_p"]
    b_in_p = _w_in_padded(full["b_in"][None])
    mix = full["rwkv_mix"][None]
    mix_r, mix_k, mix_v = mix[:, 0:512], mix[:, 512:1024], mix[:, 1024:1536]
    mix_dw = jnp.pad(mix[:, 1536:1600], ((0, 0), (0, 64)))
    mix_da = jnp.pad(mix[:, 1600:1664], ((0, 0), (0, 64)))
    mix_dg = jnp.pad(mix[:, 1664:1824], ((0, 0), (0, 96)))
    w2_p = _pad_rows(full["rwkv_w2"].astype(F32), 128)
    a2_p = _pad_rows(full["rwkv_a2"].astype(F32), 128)
    g2_p = _pad_rows(full["rwkv_g2"].astype(F32), 256)
    row = lambda name: full[name].reshape(1, -1)
    sinks = row("attn_sinks")
    rope_c, rope_s1, rope_s2 = _rope_tables(lp)

    hpad = jnp.concatenate([jnp.zeros((PAD_ROWS, D_MODEL), F32), full["meta_tokens"].astype(F32), x], axis=0)
    (u,) = _rows_fwd(_rms_fn, [_view(hpad)], [row("norm_mix_g")], [D_MODEL], nblk=nall, name="norm_mix",
                     out_dtype=BF16, tile=TILE_ALL)
    proj = _mm(u, w_in_p, bias=b_in_p, name="in_proj")
    (q_r,) = _rows_fwd(_rope_fwd_fn, [_view(proj, 512, C_Q // 512), _view(rope_c), _view(rope_s1), _view(rope_s2)],
                       [], [512], nblk=nall, name="rope_q", tile=TILE_ALL)
    (k_r,) = _rows_fwd(_rope_fwd_fn, [_view(proj, 128, C_KA // 128), _view(rope_c), _view(rope_s1),
                                      _view(rope_s2)], [], [128], nblk=nall, name="rope_k", tile=TILE_ALL)
    o_attn, lse = _attn_fwd(q_r, k_r, proj, C_VA // 128, sinks, nblk=nblk, name="attn_fwd")

    rw_cols = jnp.concatenate([proj[:, C_R:C_R + 1536], proj[:, C_DG:C_DG + 256], proj[:, C_DW:C_DW + 256]], axis=1)
    rw_prev = jnp.pad(rw_cols[:-1], ((1, 0), (0, 0)))
    pre_rows = [_view(proj, 512, 0), _view(proj, 512, 1), _view(proj, 512, 2), _view(proj, 128, C_DW // 128),
                _view(proj, 128, C_DA // 128), _view(proj, 256, C_DG // 256),
                _view(rw_prev, 512, 0), _view(rw_prev, 512, 1), _view(rw_prev, 512, 2), _view(rw_prev, 128, 14),
                _view(rw_prev, 128, 15), _view(rw_prev, 256, 6)]
    pre_consts = [mix_r, mix_k, mix_v, mix_dw, mix_da, mix_dg, row("rwkv_w0"), w2_p, row("rwkv_a0"), a2_p, g2_p,
                  row("rwkv_k_k"), row("rwkv_k_a")]
    xt_block = (BLOCK // SCAN_T * VEC_ROWS, LANES)
    r_t, k_mod, v_t, gate, xt = _rows_fwd(_rwkv_pre_xt_fn, pre_rows, pre_consts, [RW_DIM] * 4 + [xt_block],
                                          nblk=nall, name="rwkv_pre")
    xt = xt.reshape(-1, VEC_ROWS, LANES)
    spread, collect = _selectors()
    y_scan, hist, *late = _wkv_fwd(xt, v_t, spread, name="wkv_fwd", shared=gather_late[0] if gather_late else ())
    if gather_late:
        full = {**full, **gather_late[1](late)}
    post_rows = [_view(y_scan, off=1), _view(r_t, off=1), _view(k_mod, off=1), _view(v_t, off=1), _view(gate, off=1)]
    post_consts = [row("rwkv_ln_w"), row("rwkv_ln_b"), row("rwkv_r_k")]
    (y_rwkv,) = _rows_fwd(_rwkv_post_fn, post_rows, post_consts, [RW_DIM], nblk=nblk, name="rwkv_post",
                          out_dtype=BF16)

    ya = _mm(o_attn, full["w_br_attn"], name="br_attn")
    yr = _mm(y_rwkv, full["w_br_rwkv"], name="br_rwkv")
    merge_rows = [_view(ya), _view(yr), _view(proj, 1024, C_G1 // 1024, 1), _view(proj, 1024, C_G2 // 1024, 1)]
    (merged,) = _rows_fwd(_merge_fn, merge_rows, [], [D_MODEL], nblk=nblk, name="merge", out_dtype=BF16)
    h1 = _mm(merged, full["w_o"], residual=x, name="out_proj")
    (f,) = _rows_fwd(_rms_fn, [_view(h1)], [row("norm_ffn_g")], [D_MODEL], nblk=nblk, name="norm_ffn",
                     out_dtype=BF16, tile=TILE_REAL)
    ff_gate = _mm(f, full["w_ffn_gate"], name="ffn_gate")
    ff_up = _mm(f, full["w_ffn_up"], name="ffn_up")
    (act,) = _rows_fwd(_swiglu_fn, [_view(ff_gate), _view(ff_up)], [], [D_FF], nblk=nblk, name="swiglu",
                       out_dtype=BF16, tile=TILE_WIDE)
    h2 = _mm(act, full["w_ffn_down"], residual=h1, name="ffn_down")

    grads = {}
    ones_col = jnp.ones((seq, 1), F32)
    loss_rows, dh2, grads["norm_final_g"] = _rows_bwd(
        _loss_fn, [_view(h2), _view(tgt)], [row("norm_final_g")], [_view(ones_col)], nblk=nblk, name="loss",
        diff_rows=[0], diff_consts=[0], fwd_widths=[1], tile=TILE_REAL)
    loss = jnp.sum(loss_rows)

    dact = _mm(dh2, full["w_ffn_down"], tb=True, name="d_act")
    grads["w_ffn_down"] = _mm(act, dh2, ta=True, name="dw_ffn_down")
    dgate, dup = _rows_bwd(_swiglu_fn, [_view(ff_gate), _view(ff_up)], [], [_view(dact)], nblk=nblk,
                           name="swiglu_bwd", diff_rows=[0, 1], diff_consts=[], row_dtype=BF16)
    grads["w_ffn_gate"] = _mm(f, dgate, ta=True, name="dw_ffn_gate")
    grads["w_ffn_up"] = _mm(f, dup, ta=True, name="dw_ffn_up")
    df = _mm(dgate, full["w_ffn_gate"], tb=True, name="df_gate")
    df = _mm(dup, full["w_ffn_up"], tb=True, residual=df, name="df_up")
    dh1, grads["norm_ffn_g"] = _rows_bwd(_rms_fn, [_view(h1)], [row("norm_ffn_g")], [_view(df)], nblk=nblk,
                                         name="norm_ffn_bwd", diff_rows=[0], diff_consts=[0], acc=[_view(dh2)],
                                         tile=TILE_REAL)
    dmerged = _mm(dh1, full["w_o"], tb=True, name="d_merged")
    grads["w_o"] = _mm(merged, dh1, ta=True, name="dw_o")
    dya, dyr, dg1, dg2 = _rows_bwd(_merge_fn, merge_rows, [], [_view(dmerged)], nblk=nblk, name="merge_bwd",
                                   diff_rows=[0, 1, 2, 3], diff_consts=[], row_dtype=BF16)
    grads["w_br_attn"] = _mm(o_attn, dya, ta=True, name="dw_br_attn")
    grads["w_br_rwkv"] = _mm(y_rwkv, dyr, ta=True, name="dw_br_rwkv")
    dy_attn = _mm(dya, full["w_br_attn"], tb=True, name="d_y_attn")
    dy_rwkv = _mm(dyr, full["w_br_rwkv"], tb=True, name="d_y_rwkv")

    post = _rows_bwd(_rwkv_post_fn, post_rows, post_consts, [_view(dy_rwkv)], nblk=nblk, name="rwkv_post_bwd",
                     diff_rows=[0, 1, 2, 3, 4], diff_consts=[0, 1, 2])
    dys, dr_post, dk_post, dv_post, dgate_post = post[:5]
    grads["rwkv_ln_w"], grads["rwkv_ln_b"], grads["rwkv_r_k"] = post[5:]
    dxt, dv_s, *early_parts = _wkv_bwd(xt, v_t, hist, dys, spread, collect, name="wkv_bwd",
                                       scattered=scatter_early(grads) if scatter_early else ())
    pre_cts = [_view(dxt.reshape(-1, LANES), rows=xt_block[0]), _view(dv_s)] + [
        _view(t, off=-1) for t in (dr_post, dk_post, dv_post, dgate_post)]
    pre = _rows_bwd(_rwkv_pre_fn, pre_rows, pre_consts, pre_cts, nblk=nall, name="rwkv_pre_bwd",
                    diff_rows=list(range(12)), diff_consts=list(range(13)), ct_map=_rwkv_pre_cts)
    d_cur, d_prev, d_par = pre[0:6], pre[6:12], pre[12:]
    up = lambda t: jnp.pad(t[1:], ((0, 1), (0, 0)))
    d_rw = [c + up(p) for c, p in zip(d_cur, d_prev)]
    grads["rwkv_mix"] = jnp.concatenate([d_par[0], d_par[1], d_par[2], d_par[3][:, :DECAY_LORA],
                                         d_par[4][:, :AAA_LORA], d_par[5][:, :GATE_LORA]], axis=1)
    grads["rwkv_w0"], grads["rwkv_w2"] = d_par[6], d_par[7][:DECAY_LORA]
    grads["rwkv_a0"], grads["rwkv_a2"] = d_par[8], d_par[9][:AAA_LORA]
    grads["rwkv_g2"] = d_par[10][:GATE_LORA]
    grads["rwkv_k_k"], grads["rwkv_k_a"] = d_par[11], d_par[12]

    dq_real, dk_r, dva, grads["attn_sinks"] = _attn_bwd(q_r, k_r, proj, C_VA // 128, sinks, o_attn, lse, dy_attn,
                                                        nblk=nblk, name="attn_bwd")
    dq_r = jnp.pad(dq_real, ((BLOCK, 0), (0, 0)))
    (dq,) = _rows_fwd(_rope_bwd_fn, [_view(dq_r), _view(rope_c), _view(rope_s1), _view(rope_s2)], [], [512],
                      nblk=nall, name="rope_q_bwd", out_dtype=BF16, tile=TILE_ALL)
    (dka,) = _rows_fwd(_rope_bwd_fn, [_view(dk_r), _view(rope_c), _view(rope_s1),
                                      _view(rope_s2)], [], [128], nblk=nall, name="rope_k_bwd", out_dtype=BF16,
                       tile=TILE_ALL)

    lead = lambda t: jnp.pad(t, ((BLOCK, 0), (0, 0)))
    pieces = [d_rw[0], d_rw[1], d_rw[2], dq, lead(dg1), lead(dg2), d_rw[5], dka, dva, d_rw[3], d_rw[4],
              jnp.zeros((lp, NP - C_DA - 128), BF16)]
    dproj = jnp.concatenate([p.astype(BF16) for p in pieces], axis=1)
    grads["w_in_p"] = _mm(u, dproj, ta=True, name="dw_in")
    grads["b_in"] = _w_in_unpadded(_colsum(dproj, name="db_in"))
    du, *last_parts = _mm(dproj, w_in_p, tb=True, name="d_u", scattered=scatter_last(grads)) if scatter_last else (
        _mm(dproj, w_in_p, tb=True, name="d_u"),)
    dh, grads["norm_mix_g"] = _rows_bwd(_rms_fn, [_view(hpad)], [row("norm_mix_g")], [_view(du)], nblk=nall,
                                        name="norm_mix_bwd", diff_rows=[0], diff_consts=[0], acc=[_view(lead(dh1))],
                                        tile=TILE_ALL)
    grads["meta_tokens"] = dh[PAD_ROWS:BLOCK]
    return loss, dh[BLOCK:], grads, early_parts, last_parts


def kernel(x, meta_tokens, norm_mix_g, w_in, b_in, attn_sinks, rwkv_mix, rwkv_w0, rwkv_w2, rwkv_a0, rwkv_a2, rwkv_g2, rwkv_k_k, rwkv_k_a, rwkv_r_k, rwkv_ln_w, rwkv_ln_b, w_br_attn, w_br_rwkv, w_o, norm_ffn_g, w_ffn_gate, w_ffn_up, w_ffn_down, norm_final_g, loss_target, m_meta_tokens, m_norm_mix_g, m_w_in, m_b_in, m_attn_sinks, m_rwkv_mix, m_rwkv_w0, m_rwkv_w2, m_rwkv_a0, m_rwkv_a2, m_rwkv_g2, m_rwkv_k_k, m_rwkv_k_a, m_rwkv_r_k, m_rwkv_ln_w, m_rwkv_ln_b, m_w_br_attn, m_w_br_rwkv, m_w_o, m_norm_ffn_g, m_w_ffn_gate, m_w_ffn_up, m_w_ffn_down, m_norm_final_g, v_meta_tokens, v_norm_mix_g, v_w_in, v_b_in, v_attn_sinks, v_rwkv_mix, v_rwkv_w0, v_rwkv_w2, v_rwkv_a0, v_rwkv_a2, v_rwkv_g2, v_rwkv_k_k, v_rwkv_k_a, v_rwkv_r_k, v_rwkv_ln_w, v_rwkv_ln_b, v_w_br_attn, v_w_br_rwkv, v_w_o, v_norm_ffn_g, v_w_ffn_gate, v_w_ffn_up, v_w_ffn_down, v_norm_final_g):
    given = dict(locals())
    wts = {n: _strip(n, given[n]) for n in WEIGHTS}
    mom = {n: _strip(n, given["m_" + n]) for n in WEIGHTS}
    var = {n: _strip(n, given["v_" + n]) for n in WEIGHTS}
    small_shapes = [wts[n].shape for n in REPLICATED]
    width = wts["w_in"].shape[1]
    wire = lambda table: [wts[n].astype(BF16) for n, _ in table]

    w_in_all, *early_all = _all_gather([wts["w_in"].astype(BF16)] + wire(EARLY), name="gather_weights")
    full = {n: wts[n] for n in REPLICATED}
    full.update({n: _join(g, axis) for (n, axis), g in zip(EARLY, early_all)})
    full["w_in_p"] = _w_in_padded(w_in_all, shard_width=width)
    gather_late = (wire(LATE), lambda got: {n: _join(g, axis) for (n, axis), g in zip(LATE, got)})
    scatter_early = lambda g: [_split(g[n], axis).astype(BF16) for n, axis in LATE]
    scatter_last = lambda g: [jnp.stack([_w_in_unpadded(g["w_in_p"], p * width, (p + 1) * width)
                                         for p in range(N_DEV)]).astype(BF16)]

    loss_part, grad_x, grads, parts_late, (parts_w_in,) = _device_step(
        x[0], loss_target[0], full, gather_late, scatter_early, scatter_last)

    g_early = [_split(grads[n], axis).astype(BF16) for n, axis in EARLY]
    zero = jnp.zeros((1,), F32)
    g_small = _pack([grads[n].reshape(wts[n].shape) for n in REPLICATED] + [loss_part.reshape(1)], 8)
    *parts_early, parts_small = _exchange(g_early, [g_small], name="exchange_grads")
    results = [{}, {}, {}, {}]
    for (n, _), parts in zip([("w_in", 1)] + EARLY + LATE, [parts_w_in] + parts_early + list(parts_late)):
        for kind, a in enumerate(_adamw(parts, given[n], given["m_" + n], given["v_" + n], name="adamw_" + n)):
            results[kind][n] = a
    small = _adamw(parts_small, _pack([wts[n] for n in REPLICATED] + [zero], 8),
                   _pack([mom[n] for n in REPLICATED] + [zero], 8), _pack([var[n] for n in REPLICATED] + [zero], 8),
                   name="adamw_replicated")
    for kind in range(4):
        for n, a in zip(REPLICATED, _unpack(small[kind], small_shapes)):
            results[kind][n] = a
    loss = _unpack(small[0], small_shapes + [(1,)])[-1][0]
    out = [loss, grad_x[None]]
    for kind in range(4):
        out += [results[kind][n].reshape(given[n].shape) for n in WEIGHTS]
    return tuple(out)
```

```python
import functools

import jax
import jax.numpy as jnp
from jax import lax
from jax.experimental import pallas as pl
from jax.experimental.pallas import tpu as pltpu

F32 = jnp.float32
BF16 = jnp.bfloat16

N_DEV = 8
D_MODEL = 1024
N_META = 16
BLOCK = 128
PAD_ROWS = BLOCK - N_META
HEAD_DIM = 64
Q_HEADS = 8
KV_HEADS = 2
GROUP = Q_HEADS // KV_HEADS
ROPE_DIM = HEAD_DIM // 4
ROPE_HALF = ROPE_DIM // 2
ROPE_THETA = 500000.0
RW_HEADS = 8
RW_DIM = 512
DECAY_LORA = 64
AAA_LORA = 64
GATE_LORA = 160
D_FF = 2816
D_IN = 4640
RMS_EPS = 1e-6
RWKV_LN_EPS = 64e-5
NEG_INF = -1e30
SCAN_T = 16
SCAN_CHUNKS = 4
LANES = 128
PACK_ROWS = 256
TILE_ALL = 384
TILE_REAL = 512
TILE_WIDE = 256

ADAM_LR = 0.001
ADAM_B1 = 0.9
ADAM_B2 = 0.999
ADAM_EPS = 1e-08
ADAM_WD = 0.01
ADAM_STEP = 10

C_R, C_K, C_V, C_Q = 0, 512, 1024, 1536
C_G1, C_G2 = 2048, 3072
C_DG, C_KA, C_VA, C_DW, C_DA = 4096, 4352, 4480, 4608, 4736
NP = 5120

VMEM_LIMIT = 48 * 1024 * 1024


def _cparams(sem):
    return pltpu.CompilerParams(dimension_semantics=sem, vmem_limit_bytes=VMEM_LIMIT)


def _pick(n, cands):
    for c in cands:
        if n % c == 0:
            return c
    raise ValueError(f"no tile for {n}")


def _mm(a, b, *, ta=False, tb=False, bias=None, residual=None, name, scattered=()):
    m = a.shape[1] if ta else a.shape[0]
    k = a.shape[0] if ta else a.shape[1]
    n = b.shape[0] if tb else b.shape[1]
    assert k == (b.shape[1] if tb else b.shape[0]), (a.shape, b.shape, ta, tb)
    tm = _pick(m, (512, 1408, 256, 128) if ta else (1056, 1024, 528, 512, 384, 256, 128))
    tn = _pick(n, (1024, 512, 1408, 256, 128))
    if k <= 1024:
        tk = k
    else:
        tk = _pick(k, (1024, 1056, 528, 512) if (ta and not tb) else (1024, 1408, 512, 256, 128))
    nk = k // tk
    has_bias = bias is not None
    has_res = residual is not None
    dn = (((0 if ta else 1,), (1 if tb else 0,)), ((), ()))

    def body(*refs):
        a_ref, b_ref = refs[0], refs[1]
        pos = 2
        bias_ref = res_ref = None
        if has_bias:
            bias_ref = refs[pos]
            pos += 1
        if has_res:
            res_ref = refs[pos]
            pos += 1
        o_ref, acc_ref = refs[pos], refs[pos + 1]
        kk = pl.program_id(2)
        part = lax.dot_general(a_ref[...].astype(BF16), b_ref[...].astype(BF16), dn, preferred_element_type=F32)

        def finish(out):
            if has_bias:
                out = out + bias_ref[...]
            if has_res:
                out = out + res_ref[...]
            o_ref[...] = out

        if nk == 1:
            finish(part)
        else:
            @pl.when(kk == 0)
            def _():
                acc_ref[...] = part

            @pl.when((kk > 0) & (kk < nk - 1))
            def _():
                acc_ref[...] += part

            @pl.when(kk == nk - 1)
            def _():
                finish(acc_ref[...] + part)

    in_specs = [
        pl.BlockSpec((tk, tm), lambda i, j, kk: (kk, i)) if ta else pl.BlockSpec((tm, tk), lambda i, j, kk: (i, kk)),
        pl.BlockSpec((tn, tk), lambda i, j, kk: (j, kk)) if tb else pl.BlockSpec((tk, tn), lambda i, j, kk: (kk, j)),
    ]
    args = [a, b]
    if has_bias:
        in_specs.append(pl.BlockSpec((1, tn), lambda i, j, kk: (0, j)))
        args.append(bias)
    if has_res:
        in_specs.append(pl.BlockSpec((tm, tn), lambda i, j, kk: (i, j)))
        args.append(residual)
    grid = (m // tm, n // tn, nk)
    n_x = len(scattered)
    out = pl.pallas_call(
        _with_exchange(body, len(args), 1, 1, scattered, (), grid), name=name, grid=grid,
        in_specs=in_specs + [ANY] * n_x,
        out_specs=[pl.BlockSpec((tm, tn), lambda i, j, kk: (i, j))] + [ANY] * n_x,
        out_shape=[jax.ShapeDtypeStruct((m, n), F32)] + _exchange_shapes(scattered, ()),
        scratch_shapes=[pltpu.VMEM((tm, tn) if nk > 1 else (8, LANES), F32)] + (_exchange_sems(n_x) if n_x else []),
        compiler_params=_cparams(("arbitrary",) * 3 if n_x else ("parallel", "parallel", "arbitrary")),
    )(*args, *scattered)
    return out if n_x else out[0]


def _colsum(x, name):
    m, n = x.shape
    tm = _pick(m, (512, 384, BLOCK))

    def body(x_ref, o_ref):
        i = pl.program_id(0)
        s = jnp.sum(x_ref[...].astype(F32), axis=0, keepdims=True)

        @pl.when(i == 0)
        def _():
            o_ref[...] = s

        @pl.when(i > 0)
        def _():
            o_ref[...] += s

    return pl.pallas_call(
        body, name=name, grid=(m // tm,),
        in_specs=[pl.BlockSpec((tm, n), lambda i: (i, 0))],
        out_specs=pl.BlockSpec((1, n), lambda i: (0, 0)),
        out_shape=jax.ShapeDtypeStruct((1, n), F32),
        compiler_params=_cparams(("arbitrary",)),
    )(x)


def _view(arr, width=None, col=0, off=0, rows=BLOCK):
    return (arr, arr.shape[1] if width is None else width, col, off, rows)


def _row_spec(view):
    _, width, col, off, rows = view
    if off < 0:
        return pl.BlockSpec((rows, width), lambda i, col=col, off=off: (jnp.maximum(i + off, 0), col))
    return pl.BlockSpec((rows, width), lambda i, col=col, off=off: (i + off, col))


def _const_spec(arr):
    return pl.BlockSpec(arr.shape, lambda i: (0,) * arr.ndim)


def _retile(views, tile):
    assert all(v[3] == 0 and v[4] == BLOCK for v in views)
    return [v[:4] + (tile,) for v in views]


def _rows_fwd(fn, rows, consts, out_widths, *, nblk, name, out_dtype=F32, tile=BLOCK):
    nr, nc = len(rows), len(consts)
    tile = tile if (nblk * BLOCK) % tile == 0 else BLOCK
    if tile != BLOCK:
        rows, nblk = _retile(rows, tile), nblk * BLOCK // tile
    out_blocks = [(tile, w) if isinstance(w, int) else w for w in out_widths]

    def body(*refs):
        i = pl.program_id(0)
        vals = [r[...] for r in refs[:nr + nc]]
        outs = fn(i, *vals)
        for o_ref, o in zip(refs[nr + nc:], outs):
            o_ref[...] = o.astype(o_ref.dtype)

    return pl.pallas_call(
        body, name=name, grid=(nblk,),
        in_specs=[_row_spec(v) for v in rows] + [_const_spec(c) for c in consts],
        out_specs=[pl.BlockSpec(b, lambda i: (i, 0)) for b in out_blocks],
        out_shape=[jax.ShapeDtypeStruct((nblk * r, w), out_dtype) for r, w in out_blocks],
        compiler_params=_cparams(("parallel",)),
    )(*[v[0] for v in rows], *consts)


def _rows_bwd(fn, rows, consts, cts, *, nblk, name, diff_rows, diff_consts, acc=None, fwd_widths=(), row_dtype=F32,
              ct_map=None, tile=BLOCK):
    nr, nc = len(rows), len(consts)
    acc = acc or [None] * len(diff_rows)
    tile = tile if (nblk * BLOCK) % tile == 0 else BLOCK
    if tile != BLOCK:
        rows, nblk = _retile(rows, tile), nblk * BLOCK // tile
        cts = [c if c is None else _retile([c], tile)[0] for c in cts]
        acc = [a if a is None else _retile([a], tile)[0] for a in acc]
    ct_views = [c for c in cts if c is not None]
    acc_views = [a for a in acc if a is not None]
    n_in = nr + nc + len(ct_views) + len(acc_views)
    n_fwd = len(fwd_widths)

    def body(*refs):
        i = pl.program_id(0)
        row_vals = [r[...] for r in refs[:nr]]
        const_vals = [r[...] for r in refs[nr:nr + nc]]
        ct_vals = [r[...] for r in refs[nr + nc:nr + nc + len(ct_views)]]
        acc_vals = [r[...] for r in refs[nr + nc + len(ct_views):n_in]]
        out_refs = refs[n_in:]

        def f(*dargs):
            rv = list(row_vals)
            cv = list(const_vals)
            for pos, idx in enumerate(diff_rows):
                rv[idx] = dargs[pos]
            for pos, idx in enumerate(diff_consts):
                cv[idx] = dargs[len(diff_rows) + pos]
            return tuple(fn(i, *rv, *cv))

        primals = [row_vals[idx] for idx in diff_rows] + [const_vals[idx] for idx in diff_consts]
        outs, pull = jax.vjp(f, *primals)
        full_ct, ci = [], 0
        if ct_map is not None:
            full_ct = ct_map(i, *ct_vals)
        else:
            for o, c in zip(outs, cts):
                if c is None:
                    full_ct.append(jnp.zeros_like(o))
                else:
                    full_ct.append(ct_vals[ci])
                    ci += 1
        grads = pull(tuple(full_ct))
        for o_ref, o in zip(out_refs[:n_fwd], outs):
            o_ref[...] = o
        ai = 0
        for pos in range(len(diff_rows)):
            g = grads[pos]
            if acc[pos] is not None:
                g = g + acc_vals[ai]
                ai += 1
            out_refs[n_fwd + pos][...] = g.astype(row_dtype)
        for pos in range(len(diff_consts)):
            g = grads[len(diff_rows) + pos]
            o_ref = out_refs[n_fwd + len(diff_rows) + pos]

            @pl.when(i == 0)
            def _(o_ref=o_ref, g=g):
                o_ref[...] = g

            @pl.when(i > 0)
            def _(o_ref=o_ref, g=g):
                o_ref[...] += g

    out_specs = [pl.BlockSpec((tile, w), lambda i: (i, 0)) for w in fwd_widths]
    out_shape = [jax.ShapeDtypeStruct((nblk * tile, w), F32) for w in fwd_widths]
    for idx in diff_rows:
        out_specs.append(pl.BlockSpec((tile, rows[idx][1]), lambda i: (i, 0)))
        out_shape.append(jax.ShapeDtypeStruct((nblk * tile, rows[idx][1]), row_dtype))
    for idx in diff_consts:
        out_specs.append(_const_spec(consts[idx]))
        out_shape.append(jax.ShapeDtypeStruct(consts[idx].shape, F32))
    return pl.pallas_call(
        body, name=name, grid=(nblk,),
        in_specs=([_row_spec(v) for v in rows] + [_const_spec(c) for c in consts]
                  + [_row_spec(v) for v in ct_views] + [_row_spec(v) for v in acc_views]),
        out_specs=out_specs, out_shape=out_shape,
        compiler_params=_cparams(("arbitrary",)),
    )(*[v[0] for v in rows], *consts, *[v[0] for v in ct_views], *[v[0] for v in acc_views])


def _rms_fn(i, x, g):
    return (x * lax.rsqrt(jnp.mean(x * x, axis=-1, keepdims=True) + RMS_EPS) * g,)


def _sigmoid(x):
    return 1.0 / (1.0 + jnp.exp(-x))


def _softplus(x):
    return jnp.maximum(x, 0.0) + jnp.log(1.0 + jnp.exp(-jnp.abs(x)))


def _split2(x):
    hi = x.astype(BF16)
    lo = (x - hi.astype(F32)).astype(BF16)
    return jnp.concatenate([hi, lo], axis=1)


@jax.custom_vjp
def _head_sum(x):
    r = lax.broadcasted_iota(jnp.int32, (2 * RW_DIM, RW_DIM), 0) % RW_DIM // HEAD_DIM
    c = lax.broadcasted_iota(jnp.int32, (2 * RW_DIM, RW_DIM), 1) // HEAD_DIM
    return jnp.dot(_split2(x), (r == c).astype(BF16), preferred_element_type=F32)


_head_sum.defvjp(lambda x: (_head_sum(x), None), lambda _, ct: (_head_sum(ct),))


@jax.custom_vjp
def _dot_bf16(x, w):
    return jnp.dot(x.astype(BF16), w.astype(BF16), preferred_element_type=F32)


def _dot_bf16_bwd(res, ct):
    x, w = res
    ct = ct.astype(BF16)
    dx = lax.dot_general(ct, w.astype(BF16), (((1,), (1,)), ((), ())), preferred_element_type=F32)
    dw = lax.dot_general(x.astype(BF16), ct, (((0,), (0,)), ((), ())), preferred_element_type=F32)
    return dx, dw


_dot_bf16.defvjp(lambda x, w: (_dot_bf16(x, w), (x, w)), _dot_bf16_bwd)


def _rwkv_pre_fn(i, r, k, v, dw, da, dg, r_p, k_p, v_p, dw_p, da_p, dg_p,
                 mix_r, mix_k, mix_v, mix_dw, mix_da, mix_dg, w0, w2, a0, a2, g2, k_k, k_a):
    row = i * BLOCK + lax.broadcasted_iota(jnp.int32, (BLOCK, 1), 0)
    live = row >= PAD_ROWS
    live_prev = row >= PAD_ROWS + 1

    def shift(cur, prev, mix):
        cur = jnp.where(live, cur, 0.0)
        prev = jnp.where(live_prev, prev, 0.0)
        return cur + (prev - cur) * mix

    r = shift(r, r_p, mix_r)
    k = shift(k, k_p, mix_k)
    v = shift(v, v_p, mix_v)
    dw = shift(dw, dw_p, mix_dw)
    da = shift(da, da_p, mix_da)
    dg = shift(dg, dg_p, mix_dg)
    wlog = -_softplus(-(w0 + _dot_bf16(jnp.tanh(dw), w2))) - 0.5
    decay = jnp.exp(-jnp.exp(wlog))
    a = _sigmoid(a0 + _dot_bf16(da, a2))
    g = _dot_bf16(_sigmoid(dg), g2)
    kk = k * k_k
    norm_sq = jnp.where(live, _head_sum(kk * kk), 1.0)
    kk = kk / jnp.maximum(jnp.sqrt(norm_sq), 1e-12)
    k_mod = k * (1.0 + (a - 1.0) * k_a)
    return r, decay, k_mod, v, -kk, kk * a, g


def _rwkv_pre_xt_fn(i, *args):
    r, decay, k_mod, v, a_neg, b, g = _rwkv_pre_fn(i, *args)
    t = SCAN_T
    xt = jnp.concatenate([_rows_to_xt(x[c * t:(c + 1) * t]) for c in range(BLOCK // t)
                          for x in (a_neg, decay, b, k_mod, r)], axis=0)
    return r, k_mod, v, g, xt


def _rwkv_pre_cts(i, dxt, dv_s, dr_p, dk_p, dv_p, dg_p):
    t = SCAN_T
    d_a, d_w, d_b, d_k, d_r = [
        jnp.concatenate([_xt_to_rows(dxt[c * VEC_ROWS + n * HEAD_DIM:c * VEC_ROWS + (n + 1) * HEAD_DIM])
                         for c in range(BLOCK // t)], axis=0) for n in range(N_VEC)]
    dr_p, dk_p, dv_p, dg_p = [jnp.where(i > 0, x, 0.0) for x in (dr_p, dk_p, dv_p, dg_p)]
    return d_r + dr_p, d_w, d_k + dk_p, dv_s + dv_p, d_a, d_b, dg_p


def _rwkv_post_fn(i, ys, r, k_mod, v, g, ln_w, ln_b, r_k):
    mean = _head_sum(ys) * (1.0 / HEAD_DIM)
    d = ys - mean
    var = _head_sum(d * d) * (1.0 / HEAD_DIM)
    yn = d * lax.rsqrt(var + RWKV_LN_EPS) * ln_w + ln_b
    bonus = _head_sum(r * k_mod * r_k) * v
    return ((yn + bonus) * g,)


def _merge_fn(i, ya, yr, g1, g2):
    return (_sigmoid(g1) * ya + _sigmoid(g2) * yr,)


def _swiglu_fn(i, gate, up):
    return (gate * _sigmoid(gate) * up,)


def _loss_fn(i, h, tgt, g):
    y = h * lax.rsqrt(jnp.mean(h * h, axis=-1, keepdims=True) + RMS_EPS) * g
    err = y - tgt
    return (0.5 * jnp.mean(err * err, axis=-1, keepdims=True),)


def _rope_tables(lp):
    pos = (jnp.arange(lp, dtype=jnp.int32) - PAD_ROWS).astype(F32)
    inv_freq = jnp.power(jnp.float32(ROPE_THETA), -jnp.arange(ROPE_HALF, dtype=F32) * (2.0 / ROPE_DIM))
    ang = pos[:, None] * inv_freq[None, :]
    cos, sin = jnp.cos(ang), jnp.sin(ang)
    one = jnp.ones((lp, HEAD_DIM - ROPE_DIM), F32)
    zero_h = jnp.zeros((lp, ROPE_HALF), F32)
    zero_r = jnp.zeros((lp, HEAD_DIM - ROPE_DIM), F32)
    c = jnp.concatenate([cos, cos, one], axis=1)
    s1 = jnp.concatenate([-sin, zero_h, zero_r], axis=1)
    s2 = jnp.concatenate([zero_h, sin, zero_r], axis=1)
    return tuple(jnp.tile(t, (1, LANES // HEAD_DIM)) for t in (c, s1, s2))


def _rope_fwd_fn(i, x, c, s1, s2):
    n = x.shape[1]
    c, s1, s2 = [jnp.tile(t, (1, n // LANES)) for t in (c, s1, s2)]
    return (x * c + pltpu.roll(x, n - ROPE_HALF, 1) * s1 + pltpu.roll(x, ROPE_HALF, 1) * s2,)


def _rope_bwd_fn(i, dy, c, s1, s2):
    n = dy.shape[1]
    c, s1, s2 = [jnp.tile(t, (1, n // LANES)) for t in (c, s1, s2)]
    return (dy * c + pltpu.roll(dy * s1, ROPE_HALF, 1) + pltpu.roll(dy * s2, n - ROPE_HALF, 1),)


def _attn_mask(i):
    r = lax.broadcasted_iota(jnp.int32, (BLOCK, 3 * BLOCK), 0)
    c = lax.broadcasted_iota(jnp.int32, (BLOCK, 3 * BLOCK), 1)
    meta = (c < BLOCK) & (c >= PAD_ROWS)
    prev = (c >= BLOCK) & (c < 2 * BLOCK) & ((c - BLOCK) > r) & (i >= 1)
    cur = (c >= 2 * BLOCK) & ((c - 2 * BLOCK) <= r)
    return meta | prev | cur


def _attn_rows(ref, g):
    return ref[:, g * HEAD_DIM:(g + 1) * HEAD_DIM]


def _attn_group(i, g, q_all, k_refs, v_refs, s_ref):
    heads = range(g * GROUP, (g + 1) * GROUP)
    kcat = jnp.concatenate([_attn_rows(r, g) for r in k_refs], axis=0).astype(BF16)
    vcat = jnp.concatenate([_attn_rows(r, g) for r in v_refs], axis=0).astype(BF16)
    qg = jnp.concatenate([q_all[:, h * HEAD_DIM:(h + 1) * HEAD_DIM] for h in heads], axis=0).astype(BF16)
    sink = jnp.concatenate([jnp.broadcast_to(s_ref[0:1, h:h + 1], (BLOCK, 1)) for h in heads], axis=0)
    s = lax.dot_general(qg, kcat, (((1,), (1,)), ((), ())), preferred_element_type=F32) * (HEAD_DIM ** -0.5)
    valid = jnp.concatenate([_attn_mask(i)] * GROUP, axis=0)
    return heads, qg, kcat, vcat, sink, jnp.where(valid, s, NEG_INF)


def _attn_specs(v_col):
    blk = lambda f: pl.BlockSpec((BLOCK, 2 * HEAD_DIM), f)
    keys = [blk(lambda i: (0, 0)), blk(lambda i: (i, 0)), blk(lambda i: (i + 1, 0))]
    vals = [blk(lambda i: (0, v_col)), blk(lambda i: (i, v_col)), blk(lambda i: (i + 1, v_col))]
    return keys + vals + [pl.BlockSpec((1, Q_HEADS), lambda i: (0, 0))]


def _attn_fwd(q, k, v, v_col, sinks, *, nblk, name):
    def body(q_ref, km_ref, kp_ref, kc_ref, vm_ref, vp_ref, vc_ref, s_ref, o_ref, lse_ref):
        i = pl.program_id(0)
        q_all = q_ref[...]
        for g in range(KV_HEADS):
            heads, _, _, vcat, sink, s = _attn_group(i, g, q_all, (km_ref, kp_ref, kc_ref), (vm_ref, vp_ref, vc_ref),
                                                     s_ref)
            m = jnp.maximum(jnp.max(s, axis=-1, keepdims=True), sink)
            p = jnp.exp(s - m)
            den = jnp.sum(p, axis=-1, keepdims=True) + jnp.exp(sink - m)
            o = jnp.dot(p.astype(BF16), vcat, preferred_element_type=F32) / den
            lse = m + jnp.log(den)
            for n, h in enumerate(heads):
                o_ref[:, h * HEAD_DIM:(h + 1) * HEAD_DIM] = o[n * BLOCK:(n + 1) * BLOCK]
                lse_ref[:, h:h + 1] = lse[n * BLOCK:(n + 1) * BLOCK]

    return pl.pallas_call(
        body, name=name, grid=(nblk,),
        in_specs=[pl.BlockSpec((BLOCK, Q_HEADS * HEAD_DIM), lambda i: (i + 1, 0))] + _attn_specs(v_col),
        out_specs=[pl.BlockSpec((BLOCK, Q_HEADS * HEAD_DIM), lambda i: (i, 0)),
                   pl.BlockSpec((BLOCK, Q_HEADS), lambda i: (i, 0))],
        out_shape=[jax.ShapeDtypeStruct((nblk * BLOCK, Q_HEADS * HEAD_DIM), F32),
                   jax.ShapeDtypeStruct((nblk * BLOCK, Q_HEADS), F32)],
        compiler_params=_cparams(("parallel",)),
    )(q, k, k, k, v, v, v, sinks)


def _attn_bwd(q, k, v, v_col, sinks, o, lse, do, *, nblk, name):
    lp = k.shape[0]

    def body(q_ref, km_ref, kp_ref, kc_ref, vm_ref, vp_ref, vc_ref, s_ref, o_ref, lse_ref, do_ref,
             dq_ref, dk_ref, dv_ref, ds_ref):
        i = pl.program_id(0)

        @pl.when(i == 0)
        def _():
            dk_ref[...] = jnp.zeros_like(dk_ref)
            dv_ref[...] = jnp.zeros_like(dv_ref)
            ds_ref[...] = jnp.zeros_like(ds_ref)

        lane = lax.broadcasted_iota(jnp.int32, (1, Q_HEADS), 1)
        prev_rows = pl.ds(pl.multiple_of(i * BLOCK, BLOCK), BLOCK)
        cur_rows = pl.ds(pl.multiple_of((i + 1) * BLOCK, BLOCK), BLOCK)
        q_all, o_all, do_all, lse_all = q_ref[...], o_ref[...], do_ref[...], lse_ref[...]
        for g in range(KV_HEADS):
            heads, qg, kcat, vcat, sink, s = _attn_group(i, g, q_all, (km_ref, kp_ref, kc_ref),
                                                         (vm_ref, vp_ref, vc_ref), s_ref)
            stack = lambda x: jnp.concatenate([x[:, h * HEAD_DIM:(h + 1) * HEAD_DIM] for h in heads], axis=0)
            lse_g = jnp.concatenate([lse_all[:, h:h + 1] for h in heads], axis=0)
            do_g = stack(do_all)
            p = jnp.exp(s - lse_g)
            delta = jnp.sum(do_g * stack(o_all), axis=-1, keepdims=True)
            dp = lax.dot_general(do_g.astype(BF16), vcat, (((1,), (1,)), ((), ())), preferred_element_type=F32)
            dsc = (p * (dp - delta) * (HEAD_DIM ** -0.5)).astype(BF16)
            dq = jnp.dot(dsc, kcat, preferred_element_type=F32)
            dk_all = lax.dot_general(dsc, qg, (((0,), (0,)), ((), ())), preferred_element_type=F32)
            dv_all = lax.dot_general(p.astype(BF16), do_g.astype(BF16), (((0,), (0,)), ((), ())),
                                     preferred_element_type=F32)
            cols = slice(g * HEAD_DIM, (g + 1) * HEAD_DIM)
            for ref, full in ((dk_ref, dk_all), (dv_ref, dv_all)):
                ref[0:BLOCK, cols] += full[0:BLOCK]
                ref[prev_rows, cols] += full[BLOCK:2 * BLOCK]
                ref[cur_rows, cols] += full[2 * BLOCK:]
            sink_part = jnp.exp(sink - lse_g) * delta
            for n, h in enumerate(heads):
                dq_ref[:, h * HEAD_DIM:(h + 1) * HEAD_DIM] = dq[n * BLOCK:(n + 1) * BLOCK]
                dsink = -jnp.sum(sink_part[n * BLOCK:(n + 1) * BLOCK], axis=0, keepdims=True)
                ds_ref[...] += jnp.where(lane == h, dsink, 0.0)

    qspec = pl.BlockSpec((BLOCK, Q_HEADS * HEAD_DIM), lambda i: (i, 0))
    whole = pl.BlockSpec((lp, 2 * HEAD_DIM), lambda i: (0, 0))
    return pl.pallas_call(
        body, name=name, grid=(nblk,),
        in_specs=([pl.BlockSpec((BLOCK, Q_HEADS * HEAD_DIM), lambda i: (i + 1, 0))] + _attn_specs(v_col)
                  + [qspec, pl.BlockSpec((BLOCK, Q_HEADS), lambda i: (i, 0)), qspec]),
        out_specs=[qspec, whole, whole, pl.BlockSpec((1, Q_HEADS), lambda i: (0, 0))],
        out_shape=[jax.ShapeDtypeStruct((nblk * BLOCK, Q_HEADS * HEAD_DIM), F32),
                   jax.ShapeDtypeStruct((lp, 2 * HEAD_DIM), F32), jax.ShapeDtypeStruct((lp, 2 * HEAD_DIM), F32),
                   jax.ShapeDtypeStruct((1, Q_HEADS), F32)],
        compiler_params=_cparams(("arbitrary",)),
    )(q, k, k, k, v, v, v, sinks, o, lse, do)


N_VEC = 5
VEC_ROWS = N_VEC * HEAD_DIM


def _selectors():
    t = SCAN_T
    shape = (t, 2 * LANES, RW_DIM)
    step, src, dst = [lax.broadcasted_iota(jnp.int32, shape, d) for d in range(3)]
    src = src % LANES
    spread = ((src // t == dst // HEAD_DIM) & (src % t == step)).astype(BF16)
    shape = (t, RW_DIM, LANES)
    step, src, dst = [lax.broadcasted_iota(jnp.int32, shape, d) for d in range(3)]
    collect = ((src // HEAD_DIM == dst // t) & (dst % t == step)).astype(BF16)
    return spread, collect


def _rows_to_xt(x):
    low = lax.broadcasted_iota(jnp.int32, (SCAN_T, LANES), 1) < HEAD_DIM
    pieces = []
    for m in range(RW_HEADS // 2):
        pair = x[:, m * LANES:(m + 1) * LANES]
        pieces += [jnp.where(low, pair, 0.0), jnp.where(low, pltpu.roll(pair, HEAD_DIM, 1), 0.0)]
    return jnp.concatenate(pieces, axis=0).T[:HEAD_DIM]


def _xt_to_rows(a):
    t = SCAN_T
    a_t = jnp.concatenate([a, jnp.zeros_like(a)], axis=0).T
    pairs = [a_t[2 * m * t:(2 * m + 1) * t] + pltpu.roll(a_t[(2 * m + 1) * t:(2 * m + 2) * t], HEAD_DIM, 1)
             for m in range(RW_HEADS // 2)]
    return jnp.concatenate(pairs, axis=1)


def _with_exchange(compute, n_in, n_out, n_scratch, scattered, shared, grid):
    n_sc = len(scattered)
    n_x = n_sc + len(shared)
    if n_x == 0:
        return compute

    def body(*refs):
        ins, x_in = refs[:n_in], refs[n_in:n_in + n_x]
        outs, x_out = refs[n_in + n_x:n_in + n_x + n_out], refs[n_in + n_x + n_out:n_in + 2 * n_x + n_out]
        scratch = refs[n_in + 2 * n_x + n_out:n_in + 2 * n_x + n_out + n_scratch]
        sems = refs[n_in + 2 * n_x + n_out + n_scratch:]

        first = last = True
        for d, size in enumerate(grid):
            first = first & (pl.program_id(d) == 0)
            last = last & (pl.program_id(d) == size - 1)

        @pl.when(first)
        def _():
            for cp in _exchange_copies(x_in, x_out, n_sc, *sems):
                cp.start()

        compute(*ins, *outs, *scratch)

        @pl.when(last)
        def _():
            for cp in _exchange_copies(x_in, x_out, n_sc, *sems):
                cp.wait()

    return body


def _wkv_fwd(xt, v, spread, name, shared=()):
    t_steps = SCAN_T
    nch = xt.shape[0]
    per = SCAN_CHUNKS
    grid = (nch // per,)
    rows = per * t_steps
    n_x = len(shared)

    def compute(xt_ref, v_ref, sel_ref, y_ref, hist_ref, st_ref):
        @pl.when(pl.program_id(0) == 0)
        def _():
            st_ref[...] = jnp.zeros_like(st_ref)

        st = st_ref[...]
        for c in range(per):
            x2 = _split2(xt_ref[c])
            for j in range(t_steps):
                row = c * t_steps + j
                cols = jnp.dot(x2, sel_ref[j], preferred_element_type=F32)
                a_c, w_c, b_c, k_c, r_c = [cols[n * HEAD_DIM:(n + 1) * HEAD_DIM] for n in range(N_VEC)]
                hist_ref[row] = st
                sa = jnp.sum(st * a_c, axis=0, keepdims=True)
                st = st * w_c + b_c * sa + k_c * v_ref[row:row + 1, :]
                y_ref[row:row + 1, :] = jnp.sum(st * r_c, axis=0, keepdims=True)
        st_ref[...] = st

    return pl.pallas_call(
        _with_exchange(compute, 3, 2, 1, (), shared, grid), name=name, grid=grid,
        in_specs=[pl.BlockSpec((per, VEC_ROWS, LANES), lambda c: (c, 0, 0)),
                  pl.BlockSpec((rows, RW_DIM), lambda c: (c, 0)),
                  pl.BlockSpec(spread.shape, lambda c: (0, 0, 0))] + [ANY] * n_x,
        out_specs=[pl.BlockSpec((rows, RW_DIM), lambda c: (c, 0)),
                   pl.BlockSpec((rows, HEAD_DIM, RW_DIM), lambda c: (c, 0, 0))] + [ANY] * n_x,
        out_shape=[jax.ShapeDtypeStruct((nch * t_steps, RW_DIM), F32),
                   jax.ShapeDtypeStruct((nch * t_steps, HEAD_DIM, RW_DIM), F32)] + _exchange_shapes((), shared),
        scratch_shapes=[pltpu.VMEM((HEAD_DIM, RW_DIM), F32)] + (_exchange_sems(n_x) if n_x else []),
        compiler_params=_cparams(("arbitrary",)),
    )(xt, v, spread, *shared)


def _wkv_bwd(xt, v, hist, dy, spread, collect, name, scattered=()):
    t_steps = SCAN_T
    nch = xt.shape[0]
    n_x = len(scattered)
    per = SCAN_CHUNKS
    nsteps = nch // per
    grid = (nsteps,)
    rows = per * t_steps
    lead = BLOCK // rows

    def compute(xt_ref, v_ref, hist_ref, dy_ref, sel_ref, col_ref, dxt_ref, dv_ref, g_ref):
        @pl.when(pl.program_id(0) == 0)
        def _():
            g_ref[...] = jnp.zeros_like(g_ref)

        has_dy = nsteps - 1 - pl.program_id(0) >= lead
        gst = g_ref[...]
        nxt = None
        for c in reversed(range(per)):
            x2 = _split2(xt_ref[c])
            acc = jnp.zeros((VEC_ROWS, LANES), F32)
            for j in reversed(range(t_steps)):
                row = c * t_steps + j
                cols = jnp.dot(x2, sel_ref[j], preferred_element_type=F32)
                a_c, w_c, b_c, k_c, r_c = [cols[n * HEAD_DIM:(n + 1) * HEAD_DIM] for n in range(N_VEC)]
                prev = hist_ref[row]
                v_row = v_ref[row:row + 1, :]
                dy_row = jnp.where(has_dy, dy_ref[row:row + 1, :], 0.0)
                sa = jnp.sum(prev * a_c, axis=0, keepdims=True)
                if nxt is None:
                    nxt = prev * w_c + b_c * sa + k_c * v_row
                gst = gst + r_c * dy_row
                dv_ref[row:row + 1, :] = jnp.sum(gst * k_c, axis=0, keepdims=True)
                dsa = jnp.sum(gst * b_c, axis=0, keepdims=True)
                prods = jnp.concatenate([p.astype(BF16) for p in
                                         (prev * dsa, gst * prev, gst * sa, gst * v_row, nxt * dy_row)], axis=0)
                acc = acc + jnp.dot(prods, col_ref[j], preferred_element_type=F32)
                gst = gst * w_c + a_c * dsa
                nxt = prev
            dxt_ref[c] = acc
        g_ref[...] = gst

    rev3 = lambda c: (nsteps - 1 - c, 0, 0)
    rev2 = lambda c: (nsteps - 1 - c, 0)
    rowspec = pl.BlockSpec((rows, RW_DIM), rev2)
    return pl.pallas_call(
        _with_exchange(compute, 6, 2, 1, scattered, (), grid), name=name, grid=grid,
        in_specs=[pl.BlockSpec((per, VEC_ROWS, LANES), rev3), rowspec,
                  pl.BlockSpec((rows, HEAD_DIM, RW_DIM), rev3),
                  pl.BlockSpec((rows, RW_DIM), lambda c: (jnp.maximum(nsteps - 1 - c - lead, 0), 0)),
                  pl.BlockSpec(spread.shape, lambda c: (0, 0, 0)),
                  pl.BlockSpec(collect.shape, lambda c: (0, 0, 0))] + [ANY] * n_x,
        out_specs=[pl.BlockSpec((per, VEC_ROWS, LANES), rev3), rowspec] + [ANY] * n_x,
        out_shape=[jax.ShapeDtypeStruct((nch, VEC_ROWS, LANES), F32),
                   jax.ShapeDtypeStruct((nch * t_steps, RW_DIM), F32)] + _exchange_shapes(scattered, ()),
        scratch_shapes=[pltpu.VMEM((HEAD_DIM, RW_DIM), F32)] + (_exchange_sems(n_x) if n_x else []),
        compiler_params=_cparams(("arbitrary",)),
    )(xt, v, hist, dy, spread, collect, *scattered)


MESH = pl.DeviceIdType.MESH
ANY = pl.BlockSpec(memory_space=pltpu.HBM)


def _all_gather(arrays, name):
    n_arr = len(arrays)
    per = N_DEV - 1

    def body(*refs):
        x_refs, out_refs = refs[:n_arr], refs[n_arr:2 * n_arr]
        send_sems, recv_sems, local_sems = refs[2 * n_arr:]
        xi, yi, ci = lax.axis_index("x"), lax.axis_index("y"), lax.axis_index("c")
        me, sibling = (xi, yi, ci), (xi, yi, 1 - ci)
        chips = [(1 - xi, yi), (xi, 1 - yi), (1 - xi, 1 - yi)]

        def slot(a, px, py, pc):
            return out_refs[a].at[4 * px + 2 * py + pc]

        def copy(a, sem, block, to, src=None):
            return pltpu.make_async_remote_copy(
                src_ref=slot(a, *block) if src is None else src, dst_ref=slot(a, *block),
                send_sem=send_sems.at[per * a + sem], recv_sem=recv_sems.at[per * a + sem],
                device_id=to, device_id_type=MESH)

        mine = [pltpu.make_async_copy(x_refs[a], slot(a, *me), local_sems.at[a]) for a in range(n_arr)]
        for cp in mine:
            cp.start()
        sent = []
        for a in range(n_arr):
            sent.append(copy(a, 0, me, sibling, src=x_refs[a]))
            sent += [copy(a, 1 + j, me, (*chip, ci), src=x_refs[a]) for j, chip in enumerate(chips)]
        for cp in sent:
            cp.start()
        for j, chip in enumerate(chips):
            for a in range(n_arr):
                copy(a, 1 + j, (*chip, ci), me).wait_recv()
                onward = copy(a, 4 + j, (*chip, ci), sibling)
                onward.start()
                sent.append(onward)
        for a in range(n_arr):
            copy(a, 0, sibling, me).wait_recv()
        for j, chip in enumerate(chips):
            for a in range(n_arr):
                copy(a, 4 + j, (*chip, 1 - ci), me).wait_recv()
        for cp in sent:
            cp.wait_send()
        for cp in mine:
            cp.wait()

    sems = pltpu.SemaphoreType.DMA((per * n_arr,))
    return pl.pallas_call(
        body, name=name, out_shape=[jax.ShapeDtypeStruct((N_DEV,) + a.shape, a.dtype) for a in arrays],
        in_specs=[ANY] * n_arr, out_specs=[ANY] * n_arr,
        scratch_shapes=[sems, sems, pltpu.SemaphoreType.DMA((n_arr,))],
    )(*arrays)


def _exchange_copies(in_refs, out_refs, n_scattered, send_sems, recv_sems, local_sems):
    n_arr = len(in_refs)
    per = N_DEV - 1
    xi, yi, ci = lax.axis_index("x"), lax.axis_index("y"), lax.axis_index("c")
    me = 4 * xi + 2 * yi + ci
    src_of = lambda a, peer: in_refs[a].at[peer] if a < n_scattered else in_refs[a]
    copies = []
    for d in range(1, N_DEV):
        px = 1 - xi if d & 4 else xi
        py = 1 - yi if d & 2 else yi
        pc = 1 - ci if d & 1 else ci
        for a in range(n_arr):
            copies.append(pltpu.make_async_remote_copy(
                src_ref=src_of(a, 4 * px + 2 * py + pc), dst_ref=out_refs[a].at[me],
                send_sem=send_sems.at[per * a + d - 1], recv_sem=recv_sems.at[per * a + d - 1],
                device_id=(px, py, pc), device_id_type=MESH))
    own = [pltpu.make_async_copy(src_of(a, me), out_refs[a].at[me], local_sems.at[a]) for a in range(n_arr)]
    return copies + own


def _exchange_shapes(scattered, shared):
    return ([jax.ShapeDtypeStruct(a.shape, a.dtype) for a in scattered]
            + [jax.ShapeDtypeStruct((N_DEV,) + a.shape, a.dtype) for a in shared])


def _exchange_sems(n_arr):
    sems = pltpu.SemaphoreType.DMA(((N_DEV - 1) * n_arr,))
    return [sems, sems, pltpu.SemaphoreType.DMA((n_arr,))]


def _exchange(scattered, shared, name):
    n_sc = len(scattered)
    n_arr = n_sc + len(shared)

    def body(*refs):
        copies = _exchange_copies(refs[:n_arr], refs[n_arr:2 * n_arr], n_sc, *refs[2 * n_arr:])
        for cp in copies:
            cp.start()
        for cp in copies:
            cp.wait()

    return pl.pallas_call(
        body, name=name, out_shape=_exchange_shapes(scattered, shared),
        in_specs=[ANY] * n_arr, out_specs=[ANY] * n_arr, scratch_shapes=_exchange_sems(n_arr),
    )(*scattered, *shared)


def _adam_math(g, w, m, v):
    m_new = ADAM_B1 * m + (1.0 - ADAM_B1) * g
    v_new = ADAM_B2 * v + (1.0 - ADAM_B2) * (g * g)
    m_hat = m_new / (1.0 - ADAM_B1 ** ADAM_STEP)
    v_hat = v_new / (1.0 - ADAM_B2 ** ADAM_STEP)
    return -ADAM_LR * (m_hat / (jnp.sqrt(v_hat) + ADAM_EPS) + ADAM_WD * w), m_new, v_new


def _slot_sum(p_ref):
    g = p_ref[0].astype(F32)
    for s in range(1, N_DEV):
        g = g + p_ref[s].astype(F32)
    return g


def _adamw_replicated(parts, ws, ms, vs, loss_parts, name):
    n = len(ws)

    def body(*refs):
        p_refs, w_refs, m_refs, v_refs = refs[:n], refs[n:2 * n], refs[2 * n:3 * n], refs[3 * n:4 * n]
        outs = refs[4 * n + 1:]
        for j in range(n):
            g = _slot_sum(p_refs[j])
            outs[4 * j][...] = g
            for o_ref, val in zip(outs[4 * j + 1:4 * j + 4], _adam_math(g, w_refs[j][...], m_refs[j][...],
                                                                        v_refs[j][...])):
                o_ref[...] = val
        outs[4 * n][...] = _slot_sum(refs[4 * n])

    whole = pl.BlockSpec(memory_space=pltpu.VMEM)
    out = pl.pallas_call(
        body, name=name, in_specs=[whole] * (4 * n + 1), out_specs=[whole] * (4 * n + 1),
        out_shape=[jax.ShapeDtypeStruct(w.shape, F32) for w in ws for _ in range(4)]
                  + [jax.ShapeDtypeStruct(loss_parts.shape[1:], F32)],
    )(*parts, *ws, *ms, *vs, loss_parts)
    return [out[4 * j:4 * j + 4] for j in range(n)], out[4 * n]


def _adamw(parts, w, m, v, name):
    rows, cols = w.shape[-2:]
    tile = PACK_ROWS if rows % PACK_ROWS == 0 else rows
    at = (0,) if w.ndim == 3 else (Ellipsis,)

    def body(p_ref, w_ref, m_ref, v_ref, g_out, d_out, m_out, v_out):
        g = _slot_sum(p_ref)
        g_out[at] = g
        d_out[at], m_out[at], v_out[at] = _adam_math(g, w_ref[at], m_ref[at], v_ref[at])

    spec = (pl.BlockSpec((1, tile, cols), lambda i: (0, i, 0)) if w.ndim == 3
            else pl.BlockSpec((tile, cols), lambda i: (i, 0)))
    return pl.pallas_call(
        body, name=name, grid=(rows // tile,),
        in_specs=[pl.BlockSpec((N_DEV, tile, cols), lambda i: (0, i, 0)), spec, spec, spec],
        out_specs=[spec] * 4, out_shape=[jax.ShapeDtypeStruct(w.shape, F32)] * 4,
        compiler_params=_cparams(("parallel",)),
    )(parts, w, m, v)


EARLY = [("meta_tokens", 1), ("rwkv_w2", 1), ("rwkv_a2", 1), ("rwkv_g2", 1)]
LATE = [("w_br_attn", 1), ("w_br_rwkv", 1), ("w_o", 0), ("w_ffn_gate", 1), ("w_ffn_up", 1), ("w_ffn_down", 0)]
REPLICATED = ["norm_mix_g", "b_in", "attn_sinks", "rwkv_mix", "rwkv_w0", "rwkv_a0", "rwkv_k_k", "rwkv_k_a",
              "rwkv_r_k", "rwkv_ln_w", "rwkv_ln_b", "norm_ffn_g", "norm_final_g"]
WEIGHTS = ["meta_tokens", "norm_mix_g", "w_in", "b_in", "attn_sinks", "rwkv_mix", "rwkv_w0", "rwkv_w2", "rwkv_a0",
           "rwkv_a2", "rwkv_g2", "rwkv_k_k", "rwkv_k_a", "rwkv_r_k", "rwkv_ln_w", "rwkv_ln_b", "w_br_attn",
           "w_br_rwkv", "w_o", "norm_ffn_g", "w_ffn_gate", "w_ffn_up", "w_ffn_down", "norm_final_g"]


def _strip(name, a):
    return a if name in ("meta_tokens", "norm_final_g") else a[0]


def _join(gathered, axis):
    if axis == 0:
        return gathered.reshape(-1, gathered.shape[2])
    return gathered.transpose(1, 0, 2).reshape(gathered.shape[1], -1)


def _split(g, axis):
    if axis == 0:
        return g.reshape(N_DEV, -1, g.shape[1])
    return g.reshape(g.shape[0], N_DEV, -1).transpose(1, 0, 2)


W_IN_LAYOUT = [(768, 2304), (0, 512), (2592, 4640), (2432, 2592), 256 - GATE_LORA, (512, 768), (2304, 2368),
               128 - DECAY_LORA, (2368, 2432), 128 - AAA_LORA, NP - C_DA - 128]


def _w_in_padded(w, shard_width=None):
    rows = w.shape[-2]
    width = D_IN if shard_width is None else shard_width
    parts = []
    for seg in W_IN_LAYOUT:
        if isinstance(seg, int):
            parts.append(jnp.zeros((rows, seg), w.dtype))
            continue
        lo, stop = seg
        while lo < stop:
            p = lo // width
            hi = min(stop, (p + 1) * width)
            src = w if shard_width is None else w[p]
            parts.append(src[:, lo - p * width:hi - p * width])
            lo = hi
    return jnp.concatenate(parts, axis=1)


def _w_in_unpadded(wp, lo=0, stop=D_IN):
    spans, pos = [], 0
    for seg in W_IN_LAYOUT:
        if isinstance(seg, int):
            pos += seg
        else:
            spans.append((seg[0], seg[1], pos))
            pos += seg[1] - seg[0]
    parts = []
    for a, b, at in sorted(spans):
        c, d = max(a, lo), min(b, stop)
        if c < d:
            parts.append(wp[:, at + c - a:at + d - a])
    return jnp.concatenate(parts, axis=1)


def _pad_rows(a, n):
    return jnp.pad(a, ((0, n - a.shape[0]), (0, 0)))


def _device_step(x, tgt, full, gather_late=None, scatter_early=None, scatter_last=None):
    seq = x.shape[0]
    nblk = seq // BLOCK
    lp = seq + BLOCK
    nall = nblk + 1

    w_in_p = full["w_in_p"]
    b_in_p = _w_in_padded(full["b_in"][None])
    mix = full["rwkv_mix"][None]
    mix_r, mix_k, mix_v = mix[:, 0:512], mix[:, 512:1024], mix[:, 1024:1536]
    mix_dw = jnp.pad(mix[:, 1536:1600], ((0, 0), (0, 64)))
    mix_da = jnp.pad(mix[:, 1600:1664], ((0, 0), (0, 64)))
    mix_dg = jnp.pad(mix[:, 1664:1824], ((0, 0), (0, 96)))
    w2_p = _pad_rows(full["rwkv_w2"].astype(F32), 128)
    a2_p = _pad_rows(full["rwkv_a2"].astype(F32), 128)
    g2_p = _pad_rows(full["rwkv_g2"].astype(F32), 256)
    row = lambda name: full[name].reshape(1, -1)
    sinks = row("attn_sinks")
    rope_c, rope_s1, rope_s2 = _rope_tables(lp)

    hpad = jnp.concatenate([jnp.zeros((PAD_ROWS, D_MODEL), F32), full["meta_tokens"].astype(F32), x], axis=0)
    (u,) = _rows_fwd(_rms_fn, [_view(hpad)], [row("norm_mix_g")], [D_MODEL], nblk=nall, name="norm_mix",
                     out_dtype=BF16, tile=TILE_ALL)
    proj = _mm(u, w_in_p, bias=b_in_p, name="in_proj")
    (q_r,) = _rows_fwd(_rope_fwd_fn, [_view(proj, 512, C_Q // 512), _view(rope_c), _view(rope_s1), _view(rope_s2)],
                       [], [512], nblk=nall, name="rope_q", tile=TILE_ALL)
    (k_r,) = _rows_fwd(_rope_fwd_fn, [_view(proj, 128, C_KA // 128), _view(rope_c), _view(rope_s1),
                                      _view(rope_s2)], [], [128], nblk=nall, name="rope_k", tile=TILE_ALL)
    o_attn, lse = _attn_fwd(q_r, k_r, proj, C_VA // 128, sinks, nblk=nblk, name="attn_fwd")

    rw_cols = jnp.concatenate([proj[:, C_R:C_R + 1536], proj[:, C_DG:C_DG + 256], proj[:, C_DW:C_DW + 256]], axis=1)
    rw_prev = jnp.pad(rw_cols[:-1], ((1, 0), (0, 0)))
    pre_rows = [_view(proj, 512, 0), _view(proj, 512, 1), _view(proj, 512, 2), _view(proj, 128, C_DW // 128),
                _view(proj, 128, C_DA // 128), _view(proj, 256, C_DG // 256),
                _view(rw_prev, 512, 0), _view(rw_prev, 512, 1), _view(rw_prev, 512, 2), _view(rw_prev, 128, 14),
                _view(rw_prev, 128, 15), _view(rw_prev, 256, 6)]
    pre_consts = [mix_r, mix_k, mix_v, mix_dw, mix_da, mix_dg, row("rwkv_w0"), w2_p, row("rwkv_a0"), a2_p, g2_p,
                  row("rwkv_k_k"), row("rwkv_k_a")]
    xt_block = (BLOCK // SCAN_T * VEC_ROWS, LANES)
    r_t, k_mod, v_t, gate, xt = _rows_fwd(_rwkv_pre_xt_fn, pre_rows, pre_consts, [RW_DIM] * 4 + [xt_block],
                                          nblk=nall, name="rwkv_pre")
    xt = xt.reshape(-1, VEC_ROWS, LANES)
    spread, collect = _selectors()
    y_scan, hist, *late = _wkv_fwd(xt, v_t, spread, name="wkv_fwd", shared=gather_late[0] if gather_late else ())
    if gather_late:
        full = {**full, **gather_late[1](late)}
    post_rows = [_view(y_scan, off=1), _view(r_t, off=1), _view(k_mod, off=1), _view(v_t, off=1), _view(gate, off=1)]
    post_consts = [row("rwkv_ln_w"), row("rwkv_ln_b"), row("rwkv_r_k")]
    (y_rwkv,) = _rows_fwd(_rwkv_post_fn, post_rows, post_consts, [RW_DIM], nblk=nblk, name="rwkv_post",
                          out_dtype=BF16)

    ya = _mm(o_attn, full["w_br_attn"], name="br_attn")
    yr = _mm(y_rwkv, full["w_br_rwkv"], name="br_rwkv")
    merge_rows = [_view(ya), _view(yr), _view(proj, 1024, C_G1 // 1024, 1), _view(proj, 1024, C_G2 // 1024, 1)]
    (merged,) = _rows_fwd(_merge_fn, merge_rows, [], [D_MODEL], nblk=nblk, name="merge", out_dtype=BF16)
    h1 = _mm(merged, full["w_o"], residual=x, name="out_proj")
    (f,) = _rows_fwd(_rms_fn, [_view(h1)], [row("norm_ffn_g")], [D_MODEL], nblk=nblk, name="norm_ffn",
                     out_dtype=BF16, tile=TILE_REAL)
    ff_gate = _mm(f, full["w_ffn_gate"], name="ffn_gate")
    ff_up = _mm(f, full["w_ffn_up"], name="ffn_up")
    (act,) = _rows_fwd(_swiglu_fn, [_view(ff_gate), _view(ff_up)], [], [D_FF], nblk=nblk, name="swiglu",
                       out_dtype=BF16, tile=TILE_WIDE)
    h2 = _mm(act, full["w_ffn_down"], residual=h1, name="ffn_down")

    grads = {}
    ones_col = jnp.ones((seq, 1), F32)
    loss_rows, dh2, grads["norm_final_g"] = _rows_bwd(
        _loss_fn, [_view(h2), _view(tgt)], [row("norm_final_g")], [_view(ones_col)], nblk=nblk, name="loss",
        diff_rows=[0], diff_consts=[0], fwd_widths=[1], tile=TILE_REAL)
    loss = jnp.sum(loss_rows)

    dact = _mm(dh2, full["w_ffn_down"], tb=True, name="d_act")
    grads["w_ffn_down"] = _mm(act, dh2, ta=True, name="dw_ffn_down")
    dgate, dup = _rows_bwd(_swiglu_fn, [_view(ff_gate), _view(ff_up)], [], [_view(dact)], nblk=nblk,
                           name="swiglu_bwd", diff_rows=[0, 1], diff_consts=[], row_dtype=BF16, tile=TILE_WIDE)
    grads["w_ffn_gate"] = _mm(f, dgate, ta=True, name="dw_ffn_gate")
    grads["w_ffn_up"] = _mm(f, dup, ta=True, name="dw_ffn_up")
    df = _mm(dgate, full["w_ffn_gate"], tb=True, name="df_gate")
    df = _mm(dup, full["w_ffn_up"], tb=True, residual=df, name="df_up")
    dh1, grads["norm_ffn_g"] = _rows_bwd(_rms_fn, [_view(h1)], [row("norm_ffn_g")], [_view(df)], nblk=nblk,
                                         name="norm_ffn_bwd", diff_rows=[0], diff_consts=[0], acc=[_view(dh2)],
                                         tile=TILE_REAL)
    dmerged = _mm(dh1, full["w_o"], tb=True, name="d_merged")
    grads["w_o"] = _mm(merged, dh1, ta=True, name="dw_o")
    dya, dyr, dg1, dg2 = _rows_bwd(_merge_fn, merge_rows, [], [_view(dmerged)], nblk=nblk, name="merge_bwd",
                                   diff_rows=[0, 1, 2, 3], diff_consts=[], row_dtype=BF16)
    grads["w_br_attn"] = _mm(o_attn, dya, ta=True, name="dw_br_attn")
    grads["w_br_rwkv"] = _mm(y_rwkv, dyr, ta=True, name="dw_br_rwkv")
    dy_attn = _mm(dya, full["w_br_attn"], tb=True, name="d_y_attn")
    dy_rwkv = _mm(dyr, full["w_br_rwkv"], tb=True, name="d_y_rwkv")

    post = _rows_bwd(_rwkv_post_fn, post_rows, post_consts, [_view(dy_rwkv)], nblk=nblk, name="rwkv_post_bwd",
                     diff_rows=[0, 1, 2, 3, 4], diff_consts=[0, 1, 2])
    dys, dr_post, dk_post, dv_post, dgate_post = post[:5]
    grads["rwkv_ln_w"], grads["rwkv_ln_b"], grads["rwkv_r_k"] = post[5:]
    dxt, dv_s, *early_parts = _wkv_bwd(xt, v_t, hist, dys, spread, collect, name="wkv_bwd",
                                       scattered=scatter_early(grads) if scatter_early else ())
    pre_cts = [_view(dxt.reshape(-1, LANES), rows=xt_block[0]), _view(dv_s)] + [
        _view(t, off=-1) for t in (dr_post, dk_post, dv_post, dgate_post)]
    pre = _rows_bwd(_rwkv_pre_fn, pre_rows, pre_consts, pre_cts, nblk=nall, name="rwkv_pre_bwd",
                    diff_rows=list(range(12)), diff_consts=list(range(13)), ct_map=_rwkv_pre_cts)
    d_cur, d_prev, d_par = pre[0:6], pre[6:12], pre[12:]
    up = lambda t: jnp.pad(t[1:], ((0, 1), (0, 0)))
    d_rw = [c + up(p) for c, p in zip(d_cur, d_prev)]
    grads["rwkv_mix"] = jnp.concatenate([d_par[0], d_par[1], d_par[2], d_par[3][:, :DECAY_LORA],
                                         d_par[4][:, :AAA_LORA], d_par[5][:, :GATE_LORA]], axis=1)
    grads["rwkv_w0"], grads["rwkv_w2"] = d_par[6], d_par[7][:DECAY_LORA]
    grads["rwkv_a0"], grads["rwkv_a2"] = d_par[8], d_par[9][:AAA_LORA]
    grads["rwkv_g2"] = d_par[10][:GATE_LORA]
    grads["rwkv_k_k"], grads["rwkv_k_a"] = d_par[11], d_par[12]

    dq_real, dk_r, dva, grads["attn_sinks"] = _attn_bwd(q_r, k_r, proj, C_VA // 128, sinks, o_attn, lse, dy_attn,
                                                        nblk=nblk, name="attn_bwd")
    dq_r = jnp.pad(dq_real, ((BLOCK, 0), (0, 0)))
    (dq,) = _rows_fwd(_rope_bwd_fn, [_view(dq_r), _view(rope_c), _view(rope_s1), _view(rope_s2)], [], [512],
                      nblk=nall, name="rope_q_bwd", out_dtype=BF16, tile=TILE_ALL)
    (dka,) = _rows_fwd(_rope_bwd_fn, [_view(dk_r), _view(rope_c), _view(rope_s1),
                                      _view(rope_s2)], [], [128], nblk=nall, name="rope_k_bwd", out_dtype=BF16,
                       tile=TILE_ALL)

    lead = lambda t: jnp.pad(t, ((BLOCK, 0), (0, 0)))
    pieces = [d_rw[0], d_rw[1], d_rw[2], dq, lead(dg1), lead(dg2), d_rw[5], dka, dva, d_rw[3], d_rw[4],
              jnp.zeros((lp, NP - C_DA - 128), BF16)]
    dproj = jnp.concatenate([p.astype(BF16) for p in pieces], axis=1)
    grads["w_in_p"] = _mm(u, dproj, ta=True, name="dw_in")
    grads["b_in"] = _w_in_unpadded(_colsum(dproj, name="db_in"))
    du, *last_parts = _mm(dproj, w_in_p, tb=True, name="d_u", scattered=scatter_last(grads)) if scatter_last else (
        _mm(dproj, w_in_p, tb=True, name="d_u"),)
    dh, grads["norm_mix_g"] = _rows_bwd(_rms_fn, [_view(hpad)], [row("norm_mix_g")], [_view(du)], nblk=nall,
                                        name="norm_mix_bwd", diff_rows=[0], diff_consts=[0], acc=[_view(lead(dh1))],
                                        tile=TILE_ALL)
    grads["meta_tokens"] = dh[PAD_ROWS:BLOCK]
    return loss, dh[BLOCK:], grads, early_parts, last_parts


def kernel(x, meta_tokens, norm_mix_g, w_in, b_in, attn_sinks, rwkv_mix, rwkv_w0, rwkv_w2, rwkv_a0, rwkv_a2, rwkv_g2, rwkv_k_k, rwkv_k_a, rwkv_r_k, rwkv_ln_w, rwkv_ln_b, w_br_attn, w_br_rwkv, w_o, norm_ffn_g, w_ffn_gate, w_ffn_up, w_ffn_down, norm_final_g, loss_target, m_meta_tokens, m_norm_mix_g, m_w_in, m_b_in, m_attn_sinks, m_rwkv_mix, m_rwkv_w0, m_rwkv_w2, m_rwkv_a0, m_rwkv_a2, m_rwkv_g2, m_rwkv_k_k, m_rwkv_k_a, m_rwkv_r_k, m_rwkv_ln_w, m_rwkv_ln_b, m_w_br_attn, m_w_br_rwkv, m_w_o, m_norm_ffn_g, m_w_ffn_gate, m_w_ffn_up, m_w_ffn_down, m_norm_final_g, v_meta_tokens, v_norm_mix_g, v_w_in, v_b_in, v_attn_sinks, v_rwkv_mix, v_rwkv_w0, v_rwkv_w2, v_rwkv_a0, v_rwkv_a2, v_rwkv_g2, v_rwkv_k_k, v_rwkv_k_a, v_rwkv_r_k, v_rwkv_ln_w, v_rwkv_ln_b, v_w_br_attn, v_w_br_rwkv, v_w_o, v_norm_ffn_g, v_w_ffn_gate, v_w_ffn_up, v_w_ffn_down, v_norm_final_g):
    given = dict(locals())
    wts = {n: _strip(n, given[n]) for n in WEIGHTS}
    as_rows = lambda a: a.reshape(1, -1) if a.ndim == 1 else a
    width = wts["w_in"].shape[1]
    wire = lambda table: [wts[n].astype(BF16) for n, _ in table]

    w_in_all, *early_all = _all_gather([wts["w_in"].astype(BF16)] + wire(EARLY), name="gather_weights")
    full = {n: wts[n] for n in REPLICATED}
    full.update({n: _join(g, axis) for (n, axis), g in zip(EARLY, early_all)})
    full["w_in_p"] = _w_in_padded(w_in_all, shard_width=width)
    gather_late = (wire(LATE), lambda got: {n: _join(g, axis) for (n, axis), g in zip(LATE, got)})
    scatter_early = lambda g: [_split(g[n], axis).astype(BF16) for n, axis in LATE]
    scatter_last = lambda g: [jnp.stack([_w_in_unpadded(g["w_in_p"], p * width, (p + 1) * width)
                                         for p in range(N_DEV)]).astype(BF16)]

    loss_part, grad_x, grads, parts_late, (parts_w_in,) = _device_step(
        x[0], loss_target[0], full, gather_late, scatter_early, scatter_last)

    g_early = [_split(grads[n], axis).astype(BF16) for n, axis in EARLY]
    g_small = [grads[n].reshape(as_rows(given[n]).shape) for n in REPLICATED] + [jnp.full((8, LANES), loss_part)]
    got = _exchange(g_early, g_small, name="exchange_grads")
    parts_early, parts_small, parts_loss = got[:len(EARLY)], got[len(EARLY):-1], got[-1]
    results = [{}, {}, {}, {}]
    for (n, _), parts in zip([("w_in", 1)] + EARLY + LATE, [parts_w_in] + list(parts_early) + list(parts_late)):
        for kind, a in enumerate(_adamw(parts, given[n], given["m_" + n], given["v_" + n], name="adamw_" + n)):
            results[kind][n] = a
    small, loss = _adamw_replicated(parts_small, *[[as_rows(given[pre + n]) for n in REPLICATED]
                                                   for pre in ("", "m_", "v_")], parts_loss, name="adamw_replicated")
    for n, four in zip(REPLICATED, small):
        for kind, a in enumerate(four):
            results[kind][n] = a
    loss = loss[0, 0]
    out = [loss, grad_x[None]]
    for kind in range(4):
        out += [results[kind][n].reshape(given[n].shape) for n in WEIGHTS]
    return tuple(out)
```

```python
import functools

import jax
import jax.numpy as jnp
from jax import lax
from jax.experimental import pallas as pl
from jax.experimental.pallas import tpu as pltpu

F32 = jnp.float32
BF16 = jnp.bfloat16

N_DEV = 8
D_MODEL = 1024
N_META = 16
BLOCK = 128
PAD_ROWS = BLOCK - N_META
HEAD_DIM = 64
Q_HEADS = 8
KV_HEADS = 2
GROUP = Q_HEADS // KV_HEADS
ROPE_DIM = HEAD_DIM // 4
ROPE_HALF = ROPE_DIM // 2
ROPE_THETA = 500000.0
RW_HEADS = 8
RW_DIM = 512
DECAY_LORA = 64
AAA_LORA = 64
GATE_LORA = 160
D_FF = 2816
D_IN = 4640
RMS_EPS = 1e-6
RWKV_LN_EPS = 64e-5
NEG_INF = -1e30
SCAN_T = 16
SCAN_CHUNKS = 4
LANES = 128
PACK_ROWS = 256
TILE_ALL = 384
TILE_REAL = 512
TILE_WIDE = 256

ADAM_LR = 0.001
ADAM_B1 = 0.9
ADAM_B2 = 0.999
ADAM_EPS = 1e-08
ADAM_WD = 0.01
ADAM_STEP = 10

C_R, C_K, C_V, C_Q = 0, 512, 1024, 1536
C_G1, C_G2 = 2048, 3072
C_DG, C_KA, C_VA, C_DW, C_DA = 4096, 4352, 4480, 4608, 4736
NP = 5120

VMEM_LIMIT = 48 * 1024 * 1024


def _cparams(sem):
    return pltpu.CompilerParams(dimension_semantics=sem, vmem_limit_bytes=VMEM_LIMIT)


def _pick(n, cands):
    for c in cands:
        if n % c == 0:
            return c
    raise ValueError(f"no tile for {n}")


def _mm(a, b, *, ta=False, tb=False, bias=None, residual=None, name, scattered=()):
    m = a.shape[1] if ta else a.shape[0]
    k = a.shape[0] if ta else a.shape[1]
    n = b.shape[0] if tb else b.shape[1]
    assert k == (b.shape[1] if tb else b.shape[0]), (a.shape, b.shape, ta, tb)
    tm = _pick(m, (512, 1408, 256, 128) if ta else (1056, 1024, 528, 512, 384, 256, 128))
    tn = _pick(n, (1024, 512, 1408, 256, 128))
    if k <= 1024:
        tk = k
    else:
        tk = _pick(k, (1024, 1056, 528, 512) if (ta and not tb) else (1024, 1408, 512, 256, 128))
    nk = k // tk
    has_bias = bias is not None
    has_res = residual is not None
    dn = (((0 if ta else 1,), (1 if tb else 0,)), ((), ()))

    def body(*refs):
        a_ref, b_ref = refs[0], refs[1]
        pos = 2
        bias_ref = res_ref = None
        if has_bias:
            bias_ref = refs[pos]
            pos += 1
        if has_res:
            res_ref = refs[pos]
            pos += 1
        o_ref, acc_ref = refs[pos], refs[pos + 1]
        kk = pl.program_id(2)
        part = lax.dot_general(a_ref[...].astype(BF16), b_ref[...].astype(BF16), dn, preferred_element_type=F32)

        def finish(out):
            if has_bias:
                out = out + bias_ref[...]
            if has_res:
                out = out + res_ref[...]
            o_ref[...] = out

        if nk == 1:
            finish(part)
        else:
            @pl.when(kk == 0)
            def _():
                acc_ref[...] = part

            @pl.when((kk > 0) & (kk < nk - 1))
            def _():
                acc_ref[...] += part

            @pl.when(kk == nk - 1)
            def _():
                finish(acc_ref[...] + part)

    in_specs = [
        pl.BlockSpec((tk, tm), lambda i, j, kk: (kk, i)) if ta else pl.BlockSpec((tm, tk), lambda i, j, kk: (i, kk)),
        pl.BlockSpec((tn, tk), lambda i, j, kk: (j, kk)) if tb else pl.BlockSpec((tk, tn), lambda i, j, kk: (kk, j)),
    ]
    args = [a, b]
    if has_bias:
        in_specs.append(pl.BlockSpec((1, tn), lambda i, j, kk: (0, j)))
        args.append(bias)
    if has_res:
        in_specs.append(pl.BlockSpec((tm, tn), lambda i, j, kk: (i, j)))
        args.append(residual)
    grid = (m // tm, n // tn, nk)
    n_x = len(scattered)
    out = pl.pallas_call(
        _with_exchange(body, len(args), 1, 1, scattered, (), grid), name=name, grid=grid,
        in_specs=in_specs + [ANY] * n_x,
        out_specs=[pl.BlockSpec((tm, tn), lambda i, j, kk: (i, j))] + [ANY] * n_x,
        out_shape=[jax.ShapeDtypeStruct((m, n), F32)] + _exchange_shapes(scattered, ()),
        scratch_shapes=[pltpu.VMEM((tm, tn) if nk > 1 else (8, LANES), F32)] + (_exchange_sems(n_x) if n_x else []),
        compiler_params=_cparams(("arbitrary",) * 3 if n_x else ("parallel", "parallel", "arbitrary")),
    )(*args, *scattered)
    return out if n_x else out[0]


def _colsum(x, name):
    m, n = x.shape
    tm = _pick(m, (512, 384, BLOCK))

    def body(x_ref, o_ref):
        i = pl.program_id(0)
        s = jnp.sum(x_ref[...].astype(F32), axis=0, keepdims=True)

        @pl.when(i == 0)
        def _():
            o_ref[...] = s

        @pl.when(i > 0)
        def _():
            o_ref[...] += s

    return pl.pallas_call(
        body, name=name, grid=(m // tm,),
        in_specs=[pl.BlockSpec((tm, n), lambda i: (i, 0))],
        out_specs=pl.BlockSpec((1, n), lambda i: (0, 0)),
        out_shape=jax.ShapeDtypeStruct((1, n), F32),
        compiler_params=_cparams(("arbitrary",)),
    )(x)


def _view(arr, width=None, col=0, off=0, rows=BLOCK):
    return (arr, arr.shape[1] if width is None else width, col, off, rows)


def _row_spec(view):
    _, width, col, off, rows = view
    if off < 0:
        return pl.BlockSpec((rows, width), lambda i, col=col, off=off: (jnp.maximum(i + off, 0), col))
    return pl.BlockSpec((rows, width), lambda i, col=col, off=off: (i + off, col))


def _const_spec(arr):
    return pl.BlockSpec(arr.shape, lambda i: (0,) * arr.ndim)


def _retile(views, tile):
    assert all(v[3] == 0 and v[4] == BLOCK for v in views)
    return [v[:4] + (tile,) for v in views]


def _rows_fwd(fn, rows, consts, out_widths, *, nblk, name, out_dtype=F32, tile=BLOCK):
    nr, nc = len(rows), len(consts)
    tile = tile if (nblk * BLOCK) % tile == 0 else BLOCK
    if tile != BLOCK:
        rows, nblk = _retile(rows, tile), nblk * BLOCK // tile
    out_blocks = [(tile, w) if isinstance(w, int) else w for w in out_widths]

    def body(*refs):
        i = pl.program_id(0)
        vals = [r[...] for r in refs[:nr + nc]]
        outs = fn(i, *vals)
        for o_ref, o in zip(refs[nr + nc:], outs):
            o_ref[...] = o.astype(o_ref.dtype)

    return pl.pallas_call(
        body, name=name, grid=(nblk,),
        in_specs=[_row_spec(v) for v in rows] + [_const_spec(c) for c in consts],
        out_specs=[pl.BlockSpec(b, lambda i: (i, 0)) for b in out_blocks],
        out_shape=[jax.ShapeDtypeStruct((nblk * r, w), out_dtype) for r, w in out_blocks],
        compiler_params=_cparams(("parallel",)),
    )(*[v[0] for v in rows], *consts)


def _rows_bwd(fn, rows, consts, cts, *, nblk, name, diff_rows, diff_consts, acc=None, fwd_widths=(), row_dtype=F32,
              ct_map=None, tile=BLOCK):
    nr, nc = len(rows), len(consts)
    acc = acc or [None] * len(diff_rows)
    tile = tile if (nblk * BLOCK) % tile == 0 else BLOCK
    if tile != BLOCK:
        rows, nblk = _retile(rows, tile), nblk * BLOCK // tile
        cts = [c if c is None else _retile([c], tile)[0] for c in cts]
        acc = [a if a is None else _retile([a], tile)[0] for a in acc]
    ct_views = [c for c in cts if c is not None]
    acc_views = [a for a in acc if a is not None]
    n_in = nr + nc + len(ct_views) + len(acc_views)
    n_fwd = len(fwd_widths)

    def body(*refs):
        i = pl.program_id(0)
        row_vals = [r[...] for r in refs[:nr]]
        const_vals = [r[...] for r in refs[nr:nr + nc]]
        ct_vals = [r[...] for r in refs[nr + nc:nr + nc + len(ct_views)]]
        acc_vals = [r[...] for r in refs[nr + nc + len(ct_views):n_in]]
        out_refs = refs[n_in:]

        def f(*dargs):
            rv = list(row_vals)
            cv = list(const_vals)
            for pos, idx in enumerate(diff_rows):
                rv[idx] = dargs[pos]
            for pos, idx in enumerate(diff_consts):
                cv[idx] = dargs[len(diff_rows) + pos]
            return tuple(fn(i, *rv, *cv))

        primals = [row_vals[idx] for idx in diff_rows] + [const_vals[idx] for idx in diff_consts]
        outs, pull = jax.vjp(f, *primals)
        full_ct, ci = [], 0
        if ct_map is not None:
            full_ct = ct_map(i, *ct_vals)
        else:
            for o, c in zip(outs, cts):
                if c is None:
                    full_ct.append(jnp.zeros_like(o))
                else:
                    full_ct.append(ct_vals[ci])
                    ci += 1
        grads = pull(tuple(full_ct))
        for o_ref, o in zip(out_refs[:n_fwd], outs):
            o_ref[...] = o
        ai = 0
        for pos in range(len(diff_rows)):
            g = grads[pos]
            if acc[pos] is not None:
                g = g + acc_vals[ai]
                ai += 1
            out_refs[n_fwd + pos][...] = g.astype(row_dtype)
        for pos in range(len(diff_consts)):
            g = grads[len(diff_rows) + pos]
            o_ref = out_refs[n_fwd + len(diff_rows) + pos]

            @pl.when(i == 0)
            def _(o_ref=o_ref, g=g):
                o_ref[...] = g

            @pl.when(i > 0)
            def _(o_ref=o_ref, g=g):
                o_ref[...] += g

    out_specs = [pl.BlockSpec((tile, w), lambda i: (i, 0)) for w in fwd_widths]
    out_shape = [jax.ShapeDtypeStruct((nblk * tile, w), F32) for w in fwd_widths]
    for idx in diff_rows:
        out_specs.append(pl.BlockSpec((tile, rows[idx][1]), lambda i: (i, 0)))
        out_shape.append(jax.ShapeDtypeStruct((nblk * tile, rows[idx][1]), row_dtype))
    for idx in diff_consts:
        out_specs.append(_const_spec(consts[idx]))
        out_shape.append(jax.ShapeDtypeStruct(consts[idx].shape, F32))
    return pl.pallas_call(
        body, name=name, grid=(nblk,),
        in_specs=([_row_spec(v) for v in rows] + [_const_spec(c) for c in consts]
                  + [_row_spec(v) for v in ct_views] + [_row_spec(v) for v in acc_views]),
        out_specs=out_specs, out_shape=out_shape,
        compiler_params=_cparams(("arbitrary",)),
    )(*[v[0] for v in rows], *consts, *[v[0] for v in ct_views], *[v[0] for v in acc_views])


def _rms_fn(i, x, g):
    return (x * lax.rsqrt(jnp.mean(x * x, axis=-1, keepdims=True) + RMS_EPS) * g,)


def _sigmoid(x):
    return 1.0 / (1.0 + jnp.exp(-x))


def _softplus(x):
    return jnp.maximum(x, 0.0) + jnp.log(1.0 + jnp.exp(-jnp.abs(x)))


def _split2(x):
    hi = x.astype(BF16)
    lo = (x - hi.astype(F32)).astype(BF16)
    return jnp.concatenate([hi, lo], axis=1)


@jax.custom_vjp
def _head_sum(x):
    r = lax.broadcasted_iota(jnp.int32, (2 * RW_DIM, RW_DIM), 0) % RW_DIM // HEAD_DIM
    c = lax.broadcasted_iota(jnp.int32, (2 * RW_DIM, RW_DIM), 1) // HEAD_DIM
    return jnp.dot(_split2(x), (r == c).astype(BF16), preferred_element_type=F32)


_head_sum.defvjp(lambda x: (_head_sum(x), None), lambda _, ct: (_head_sum(ct),))


@jax.custom_vjp
def _dot_bf16(x, w):
    return jnp.dot(x.astype(BF16), w.astype(BF16), preferred_element_type=F32)


def _dot_bf16_bwd(res, ct):
    x, w = res
    ct = ct.astype(BF16)
    dx = lax.dot_general(ct, w.astype(BF16), (((1,), (1,)), ((), ())), preferred_element_type=F32)
    dw = lax.dot_general(x.astype(BF16), ct, (((0,), (0,)), ((), ())), preferred_element_type=F32)
    return dx, dw


_dot_bf16.defvjp(lambda x, w: (_dot_bf16(x, w), (x, w)), _dot_bf16_bwd)


def _rwkv_pre_fn(i, r, k, v, dw, da, dg, r_p, k_p, v_p, dw_p, da_p, dg_p,
                 mix_r, mix_k, mix_v, mix_dw, mix_da, mix_dg, w0, w2, a0, a2, g2, k_k, k_a):
    row = i * BLOCK + lax.broadcasted_iota(jnp.int32, (BLOCK, 1), 0)
    live = row >= PAD_ROWS
    live_prev = row >= PAD_ROWS + 1

    def shift(cur, prev, mix):
        cur = jnp.where(live, cur, 0.0)
        prev = jnp.where(live_prev, prev, 0.0)
        return cur + (prev - cur) * mix

    r = shift(r, r_p, mix_r)
    k = shift(k, k_p, mix_k)
    v = shift(v, v_p, mix_v)
    dw = shift(dw, dw_p, mix_dw)
    da = shift(da, da_p, mix_da)
    dg = shift(dg, dg_p, mix_dg)
    wlog = -_softplus(-(w0 + _dot_bf16(jnp.tanh(dw), w2))) - 0.5
    decay = jnp.exp(-jnp.exp(wlog))
    a = _sigmoid(a0 + _dot_bf16(da, a2))
    g = _dot_bf16(_sigmoid(dg), g2)
    kk = k * k_k
    norm_sq = jnp.where(live, _head_sum(kk * kk), 1.0)
    kk = kk / jnp.maximum(jnp.sqrt(norm_sq), 1e-12)
    k_mod = k * (1.0 + (a - 1.0) * k_a)
    return r, decay, k_mod, v, -kk, kk * a, g


def _rwkv_pre_xt_fn(i, *args):
    r, decay, k_mod, v, a_neg, b, g = _rwkv_pre_fn(i, *args)
    t = SCAN_T
    xt = jnp.concatenate([_rows_to_xt(x[c * t:(c + 1) * t]) for c in range(BLOCK // t)
                          for x in (a_neg, decay, b, k_mod, r)], axis=0)
    return r, k_mod, v, g, xt


def _rwkv_pre_cts(i, dxt, dv_s, dr_p, dk_p, dv_p, dg_p):
    t = SCAN_T
    d_a, d_w, d_b, d_k, d_r = [
        jnp.concatenate([_xt_to_rows(dxt[c * VEC_ROWS + n * HEAD_DIM:c * VEC_ROWS + (n + 1) * HEAD_DIM])
                         for c in range(BLOCK // t)], axis=0) for n in range(N_VEC)]
    dr_p, dk_p, dv_p, dg_p = [jnp.where(i > 0, x, 0.0) for x in (dr_p, dk_p, dv_p, dg_p)]
    return d_r + dr_p, d_w, d_k + dk_p, dv_s + dv_p, d_a, d_b, dg_p


def _rwkv_post_fn(i, ys, r, k_mod, v, g, ln_w, ln_b, r_k):
    mean = _head_sum(ys) * (1.0 / HEAD_DIM)
    d = ys - mean
    var = _head_sum(d * d) * (1.0 / HEAD_DIM)
    yn = d * lax.rsqrt(var + RWKV_LN_EPS) * ln_w + ln_b
    bonus = _head_sum(r * k_mod * r_k) * v
    return ((yn + bonus) * g,)


def _merge_fn(i, ya, yr, g1, g2):
    return (_sigmoid(g1) * ya + _sigmoid(g2) * yr,)


def _swiglu_fn(i, gate, up):
    return (gate * _sigmoid(gate) * up,)


def _loss_fn(i, h, tgt, g):
    y = h * lax.rsqrt(jnp.mean(h * h, axis=-1, keepdims=True) + RMS_EPS) * g
    err = y - tgt
    return (0.5 * jnp.mean(err * err, axis=-1, keepdims=True),)


def _rope_tables(lp):
    pos = (jnp.arange(lp, dtype=jnp.int32) - PAD_ROWS).astype(F32)
    inv_freq = jnp.power(jnp.float32(ROPE_THETA), -jnp.arange(ROPE_HALF, dtype=F32) * (2.0 / ROPE_DIM))
    ang = pos[:, None] * inv_freq[None, :]
    cos, sin = jnp.cos(ang), jnp.sin(ang)
    one = jnp.ones((lp, HEAD_DIM - ROPE_DIM), F32)
    zero_h = jnp.zeros((lp, ROPE_HALF), F32)
    zero_r = jnp.zeros((lp, HEAD_DIM - ROPE_DIM), F32)
    c = jnp.concatenate([cos, cos, one], axis=1)
    s1 = jnp.concatenate([-sin, zero_h, zero_r], axis=1)
    s2 = jnp.concatenate([zero_h, sin, zero_r], axis=1)
    return tuple(jnp.tile(t, (1, LANES // HEAD_DIM)) for t in (c, s1, s2))


def _rope_fwd_fn(i, x, c, s1, s2):
    n = x.shape[1]
    c, s1, s2 = [jnp.tile(t, (1, n // LANES)) for t in (c, s1, s2)]
    return (x * c + pltpu.roll(x, n - ROPE_HALF, 1) * s1 + pltpu.roll(x, ROPE_HALF, 1) * s2,)


def _rope_bwd_fn(i, dy, c, s1, s2):
    n = dy.shape[1]
    c, s1, s2 = [jnp.tile(t, (1, n // LANES)) for t in (c, s1, s2)]
    return (dy * c + pltpu.roll(dy * s1, ROPE_HALF, 1) + pltpu.roll(dy * s2, n - ROPE_HALF, 1),)


def _attn_mask(i):
    r = lax.broadcasted_iota(jnp.int32, (BLOCK, 3 * BLOCK), 0)
    c = lax.broadcasted_iota(jnp.int32, (BLOCK, 3 * BLOCK), 1)
    meta = (c < BLOCK) & (c >= PAD_ROWS)
    prev = (c >= BLOCK) & (c < 2 * BLOCK) & ((c - BLOCK) > r) & (i >= 1)
    cur = (c >= 2 * BLOCK) & ((c - 2 * BLOCK) <= r)
    return meta | prev | cur


def _attn_rows(ref, g):
    return ref[:, g * HEAD_DIM:(g + 1) * HEAD_DIM]


def _attn_group(i, g, q_all, k_refs, v_refs, s_ref):
    heads = range(g * GROUP, (g + 1) * GROUP)
    kcat = jnp.concatenate([_attn_rows(r, g) for r in k_refs], axis=0).astype(BF16)
    vcat = jnp.concatenate([_attn_rows(r, g) for r in v_refs], axis=0).astype(BF16)
    qg = jnp.concatenate([q_all[:, h * HEAD_DIM:(h + 1) * HEAD_DIM] for h in heads], axis=0).astype(BF16)
    sink = jnp.concatenate([jnp.broadcast_to(s_ref[0:1, h:h + 1], (BLOCK, 1)) for h in heads], axis=0)
    s = lax.dot_general(qg, kcat, (((1,), (1,)), ((), ())), preferred_element_type=F32) * (HEAD_DIM ** -0.5)
    valid = jnp.concatenate([_attn_mask(i)] * GROUP, axis=0)
    return heads, qg, kcat, vcat, sink, jnp.where(valid, s, NEG_INF)


ATTN_SUB = 2


def _attn_specs(v_col):
    q = [pl.BlockSpec((BLOCK, Q_HEADS * HEAD_DIM), lambda i, n=n: (ATTN_SUB * i + 1 + n, 0)) for n in range(ATTN_SUB)]
    blk = lambda col: ([pl.BlockSpec((BLOCK, 2 * HEAD_DIM), lambda i: (0, col))]
                       + [pl.BlockSpec((BLOCK, 2 * HEAD_DIM), lambda i, n=n: (ATTN_SUB * i + n, col))
                          for n in range(ATTN_SUB + 1)])
    return q + blk(0) + blk(v_col) + [pl.BlockSpec((1, Q_HEADS), lambda i: (0, 0))]


def _attn_split(refs):
    n = ATTN_SUB
    q_refs, k_refs, v_refs = refs[:n], refs[n:2 * n + 2], refs[2 * n + 2:3 * n + 4]
    return q_refs, k_refs, v_refs, refs[3 * n + 4], refs[3 * n + 5:]


def _attn_fwd(q, k, v, v_col, sinks, *, nblk, name):
    rows = ATTN_SUB * BLOCK

    def body(*refs):
        q_refs, k_refs, v_refs, s_ref, (o_ref, lse_ref) = _attn_split(refs)
        i = pl.program_id(0)
        for sub in range(ATTN_SUB):
            at = slice(sub * BLOCK, (sub + 1) * BLOCK)
            keys = (k_refs[0], k_refs[1 + sub], k_refs[2 + sub])
            vals = (v_refs[0], v_refs[1 + sub], v_refs[2 + sub])
            q_all = q_refs[sub][...]
            for g in range(KV_HEADS):
                heads, _, _, vcat, sink, s = _attn_group(ATTN_SUB * i + sub, g, q_all, keys, vals, s_ref)
                m = jnp.maximum(jnp.max(s, axis=-1, keepdims=True), sink)
                p = jnp.exp(s - m)
                den = jnp.sum(p, axis=-1, keepdims=True) + jnp.exp(sink - m)
                o = jnp.dot(p.astype(BF16), vcat, preferred_element_type=F32) / den
                lse = m + jnp.log(den)
                for n, h in enumerate(heads):
                    o_ref[at, h * HEAD_DIM:(h + 1) * HEAD_DIM] = o[n * BLOCK:(n + 1) * BLOCK]
                    lse_ref[at, h:h + 1] = lse[n * BLOCK:(n + 1) * BLOCK]

    return pl.pallas_call(
        body, name=name, grid=(nblk // ATTN_SUB,),
        in_specs=_attn_specs(v_col),
        out_specs=[pl.BlockSpec((rows, Q_HEADS * HEAD_DIM), lambda i: (i, 0)),
                   pl.BlockSpec((rows, Q_HEADS), lambda i: (i, 0))],
        out_shape=[jax.ShapeDtypeStruct((nblk * BLOCK, Q_HEADS * HEAD_DIM), F32),
                   jax.ShapeDtypeStruct((nblk * BLOCK, Q_HEADS), F32)],
        compiler_params=_cparams(("parallel",)),
    )(*[q] * ATTN_SUB, *[k] * (ATTN_SUB + 2), *[v] * (ATTN_SUB + 2), sinks)


def _attn_bwd(q, k, v, v_col, sinks, o, lse, do, *, nblk, name):
    lp = k.shape[0]
    rows = ATTN_SUB * BLOCK

    def body(*refs):
        q_refs, k_refs, v_refs, s_ref, (o_ref, lse_ref, do_ref, dq_ref, dk_ref, dv_ref, ds_ref) = _attn_split(refs)
        i = pl.program_id(0)

        @pl.when(i == 0)
        def _():
            dk_ref[...] = jnp.zeros_like(dk_ref)
            dv_ref[...] = jnp.zeros_like(dv_ref)
            ds_ref[...] = jnp.zeros_like(ds_ref)

        lane = lax.broadcasted_iota(jnp.int32, (1, Q_HEADS), 1)
        for sub in range(ATTN_SUB):
            at = slice(sub * BLOCK, (sub + 1) * BLOCK)
            blk = ATTN_SUB * i + sub
            keys = (k_refs[0], k_refs[1 + sub], k_refs[2 + sub])
            vals = (v_refs[0], v_refs[1 + sub], v_refs[2 + sub])
            prev_rows = pl.ds(pl.multiple_of(blk * BLOCK, BLOCK), BLOCK)
            cur_rows = pl.ds(pl.multiple_of((blk + 1) * BLOCK, BLOCK), BLOCK)
            q_all, o_all, do_all, lse_all = q_refs[sub][...], o_ref[at, :], do_ref[at, :], lse_ref[at, :]
            for g in range(KV_HEADS):
                heads, qg, kcat, vcat, sink, s = _attn_group(blk, g, q_all, keys, vals, s_ref)
                stack = lambda x: jnp.concatenate([x[:, h * HEAD_DIM:(h + 1) * HEAD_DIM] for h in heads], axis=0)
                lse_g = jnp.concatenate([lse_all[:, h:h + 1] for h in heads], axis=0)
                do_g = stack(do_all)
                p = jnp.exp(s - lse_g)
                delta = jnp.sum(do_g * stack(o_all), axis=-1, keepdims=True)
                dp = lax.dot_general(do_g.astype(BF16), vcat, (((1,), (1,)), ((), ())), preferred_element_type=F32)
                dsc = (p * (dp - delta) * (HEAD_DIM ** -0.5)).astype(BF16)
                dq = jnp.dot(dsc, kcat, preferred_element_type=F32)
                dk_all = lax.dot_general(dsc, qg, (((0,), (0,)), ((), ())), preferred_element_type=F32)
                dv_all = lax.dot_general(p.astype(BF16), do_g.astype(BF16), (((0,), (0,)), ((), ())),
                                         preferred_element_type=F32)
                cols = slice(g * HEAD_DIM, (g + 1) * HEAD_DIM)
                for ref, full in ((dk_ref, dk_all), (dv_ref, dv_all)):
                    ref[0:BLOCK, cols] += full[0:BLOCK]
                    ref[prev_rows, cols] += full[BLOCK:2 * BLOCK]
                    ref[cur_rows, cols] += full[2 * BLOCK:]
                sink_part = jnp.exp(sink - lse_g) * delta
                for n, h in enumerate(heads):
                    dq_ref[at, h * HEAD_DIM:(h + 1) * HEAD_DIM] = dq[n * BLOCK:(n + 1) * BLOCK]
                    dsink = -jnp.sum(sink_part[n * BLOCK:(n + 1) * BLOCK], axis=0, keepdims=True)
                    ds_ref[...] += jnp.where(lane == h, dsink, 0.0)

    qspec = pl.BlockSpec((rows, Q_HEADS * HEAD_DIM), lambda i: (i, 0))
    whole = pl.BlockSpec((lp, 2 * HEAD_DIM), lambda i: (0, 0))
    return pl.pallas_call(
        body, name=name, grid=(nblk // ATTN_SUB,),
        in_specs=_attn_specs(v_col) + [qspec, pl.BlockSpec((rows, Q_HEADS), lambda i: (i, 0)), qspec],
        out_specs=[qspec, whole, whole, pl.BlockSpec((1, Q_HEADS), lambda i: (0, 0))],
        out_shape=[jax.ShapeDtypeStruct((nblk * BLOCK, Q_HEADS * HEAD_DIM), F32),
                   jax.ShapeDtypeStruct((lp, 2 * HEAD_DIM), F32), jax.ShapeDtypeStruct((lp, 2 * HEAD_DIM), F32),
                   jax.ShapeDtypeStruct((1, Q_HEADS), F32)],
        compiler_params=_cparams(("arbitrary",)),
    )(*[q] * ATTN_SUB, *[k] * (ATTN_SUB + 2), *[v] * (ATTN_SUB + 2), sinks, o, lse, do)


N_VEC = 5
VEC_ROWS = N_VEC * HEAD_DIM


def _selectors():
    t = SCAN_T
    shape = (t, 2 * LANES, RW_DIM)
    step, src, dst = [lax.broadcasted_iota(jnp.int32, shape, d) for d in range(3)]
    src = src % LANES
    spread = ((src // t == dst // HEAD_DIM) & (src % t == step)).astype(BF16)
    shape = (t, RW_DIM, LANES)
    step, src, dst = [lax.broadcasted_iota(jnp.int32, shape, d) for d in range(3)]
    collect = ((src // HEAD_DIM == dst // t) & (dst % t == step)).astype(BF16)
    return spread, collect


def _rows_to_xt(x):
    low = lax.broadcasted_iota(jnp.int32, (SCAN_T, LANES), 1) < HEAD_DIM
    pieces = []
    for m in range(RW_HEADS // 2):
        pair = x[:, m * LANES:(m + 1) * LANES]
        pieces += [jnp.where(low, pair, 0.0), jnp.where(low, pltpu.roll(pair, HEAD_DIM, 1), 0.0)]
    return jnp.concatenate(pieces, axis=0).T[:HEAD_DIM]


def _xt_to_rows(a):
    t = SCAN_T
    a_t = jnp.concatenate([a, jnp.zeros_like(a)], axis=0).T
    pairs = [a_t[2 * m * t:(2 * m + 1) * t] + pltpu.roll(a_t[(2 * m + 1) * t:(2 * m + 2) * t], HEAD_DIM, 1)
             for m in range(RW_HEADS // 2)]
    return jnp.concatenate(pairs, axis=1)


def _with_exchange(compute, n_in, n_out, n_scratch, scattered, shared, grid):
    n_sc = len(scattered)
    n_x = n_sc + len(shared)
    if n_x == 0:
        return compute

    def body(*refs):
        ins, x_in = refs[:n_in], refs[n_in:n_in + n_x]
        outs, x_out = refs[n_in + n_x:n_in + n_x + n_out], refs[n_in + n_x + n_out:n_in + 2 * n_x + n_out]
        scratch = refs[n_in + 2 * n_x + n_out:n_in + 2 * n_x + n_out + n_scratch]
        sems = refs[n_in + 2 * n_x + n_out + n_scratch:]

        first = last = True
        for d, size in enumerate(grid):
            first = first & (pl.program_id(d) == 0)
            last = last & (pl.program_id(d) == size - 1)

        @pl.when(first)
        def _():
            for cp in _exchange_copies(x_in, x_out, n_sc, *sems):
                cp.start()

        compute(*ins, *outs, *scratch)

        @pl.when(last)
        def _():
            for cp in _exchange_copies(x_in, x_out, n_sc, *sems):
                cp.wait()

    return body


def _wkv_fwd(xt, v, spread, name, shared=()):
    t_steps = SCAN_T
    nch = xt.shape[0]
    per = SCAN_CHUNKS
    grid = (nch // per,)
    rows = per * t_steps
    n_x = len(shared)

    def compute(xt_ref, v_ref, sel_ref, y_ref, hist_ref, st_ref):
        @pl.when(pl.program_id(0) == 0)
        def _():
            st_ref[...] = jnp.zeros_like(st_ref)

        st = st_ref[...]
        for c in range(per):
            x2 = _split2(xt_ref[c])
            for j in range(t_steps):
                row = c * t_steps + j
                cols = jnp.dot(x2, sel_ref[j], preferred_element_type=F32)
                a_c, w_c, b_c, k_c, r_c = [cols[n * HEAD_DIM:(n + 1) * HEAD_DIM] for n in range(N_VEC)]
                hist_ref[row] = st
                sa = jnp.sum(st * a_c, axis=0, keepdims=True)
                st = st * w_c + b_c * sa + k_c * v_ref[row:row + 1, :]
                y_ref[row:row + 1, :] = jnp.sum(st * r_c, axis=0, keepdims=True)
        st_ref[...] = st

    return pl.pallas_call(
        _with_exchange(compute, 3, 2, 1, (), shared, grid), name=name, grid=grid,
        in_specs=[pl.BlockSpec((per, VEC_ROWS, LANES), lambda c: (c, 0, 0)),
                  pl.BlockSpec((rows, RW_DIM), lambda c: (c, 0)),
                  pl.BlockSpec(spread.shape, lambda c: (0, 0, 0))] + [ANY] * n_x,
        out_specs=[pl.BlockSpec((rows, RW_DIM), lambda c: (c, 0)),
                   pl.BlockSpec((rows, HEAD_DIM, RW_DIM), lambda c: (c, 0, 0))] + [ANY] * n_x,
        out_shape=[jax.ShapeDtypeStruct((nch * t_steps, RW_DIM), F32),
                   jax.ShapeDtypeStruct((nch * t_steps, HEAD_DIM, RW_DIM), F32)] + _exchange_shapes((), shared),
        scratch_shapes=[pltpu.VMEM((HEAD_DIM, RW_DIM), F32)] + (_exchange_sems(n_x) if n_x else []),
        compiler_params=_cparams(("arbitrary",)),
    )(xt, v, spread, *shared)


def _wkv_bwd(xt, v, hist, dy, spread, collect, name, scattered=()):
    t_steps = SCAN_T
    nch = xt.shape[0]
    n_x = len(scattered)
    per = SCAN_CHUNKS
    nsteps = nch // per
    grid = (nsteps,)
    rows = per * t_steps
    lead = BLOCK // rows

    def compute(xt_ref, v_ref, hist_ref, dy_ref, sel_ref, col_ref, dxt_ref, dv_ref, g_ref):
        @pl.when(pl.program_id(0) == 0)
        def _():
            g_ref[...] = jnp.zeros_like(g_ref)

        has_dy = nsteps - 1 - pl.program_id(0) >= lead
        gst = g_ref[...]
        nxt = None
        for c in reversed(range(per)):
            x2 = _split2(xt_ref[c])
            acc = jnp.zeros((VEC_ROWS, LANES), F32)
            for j in reversed(range(t_steps)):
                row = c * t_steps + j
                cols = jnp.dot(x2, sel_ref[j], preferred_element_type=F32)
                a_c, w_c, b_c, k_c, r_c = [cols[n * HEAD_DIM:(n + 1) * HEAD_DIM] for n in range(N_VEC)]
                prev = hist_ref[row]
                v_row = v_ref[row:row + 1, :]
                dy_row = jnp.where(has_dy, dy_ref[row:row + 1, :], 0.0)
                sa = jnp.sum(prev * a_c, axis=0, keepdims=True)
                if nxt is None:
                    nxt = prev * w_c + b_c * sa + k_c * v_row
                gst = gst + r_c * dy_row
                dv_ref[row:row + 1, :] = jnp.sum(gst * k_c, axis=0, keepdims=True)
                dsa = jnp.sum(gst * b_c, axis=0, keepdims=True)
                prods = jnp.concatenate([p.astype(BF16) for p in
                                         (prev * dsa, gst * prev, gst * sa, gst * v_row, nxt * dy_row)], axis=0)
                acc = acc + jnp.dot(prods, col_ref[j], preferred_element_type=F32)
                gst = gst * w_c + a_c * dsa
                nxt = prev
            dxt_ref[c] = acc
        g_ref[...] = gst

    rev3 = lambda c: (nsteps - 1 - c, 0, 0)
    rev2 = lambda c: (nsteps - 1 - c, 0)
    rowspec = pl.BlockSpec((rows, RW_DIM), rev2)
    return pl.pallas_call(
        _with_exchange(compute, 6, 2, 1, scattered, (), grid), name=name, grid=grid,
        in_specs=[pl.BlockSpec((per, VEC_ROWS, LANES), rev3), rowspec,
                  pl.BlockSpec((rows, HEAD_DIM, RW_DIM), rev3),
                  pl.BlockSpec((rows, RW_DIM), lambda c: (jnp.maximum(nsteps - 1 - c - lead, 0), 0)),
                  pl.BlockSpec(spread.shape, lambda c: (0, 0, 0)),
                  pl.BlockSpec(collect.shape, lambda c: (0, 0, 0))] + [ANY] * n_x,
        out_specs=[pl.BlockSpec((per, VEC_ROWS, LANES), rev3), rowspec] + [ANY] * n_x,
        out_shape=[jax.ShapeDtypeStruct((nch, VEC_ROWS, LANES), F32),
                   jax.ShapeDtypeStruct((nch * t_steps, RW_DIM), F32)] + _exchange_shapes(scattered, ()),
        scratch_shapes=[pltpu.VMEM((HEAD_DIM, RW_DIM), F32)] + (_exchange_sems(n_x) if n_x else []),
        compiler_params=_cparams(("arbitrary",)),
    )(xt, v, hist, dy, spread, collect, *scattered)


MESH = pl.DeviceIdType.MESH
ANY = pl.BlockSpec(memory_space=pltpu.HBM)


def _all_gather(arrays, name):
    n_arr = len(arrays)
    per = N_DEV - 1

    def body(*refs):
        x_refs, out_refs = refs[:n_arr], refs[n_arr:2 * n_arr]
        send_sems, recv_sems, local_sems = refs[2 * n_arr:]
        xi, yi, ci = lax.axis_index("x"), lax.axis_index("y"), lax.axis_index("c")
        me, sibling = (xi, yi, ci), (xi, yi, 1 - ci)
        chips = [(1 - xi, yi), (xi, 1 - yi), (1 - xi, 1 - yi)]

        def slot(a, px, py, pc):
            return out_refs[a].at[4 * px + 2 * py + pc]

        def copy(a, sem, block, to, src=None):
            return pltpu.make_async_remote_copy(
                src_ref=slot(a, *block) if src is None else src, dst_ref=slot(a, *block),
                send_sem=send_sems.at[per * a + sem], recv_sem=recv_sems.at[per * a + sem],
                device_id=to, device_id_type=MESH)

        mine = [pltpu.make_async_copy(x_refs[a], slot(a, *me), local_sems.at[a]) for a in range(n_arr)]
        for cp in mine:
            cp.start()
        sent = []
        for a in range(n_arr):
            sent.append(copy(a, 0, me, sibling, src=x_refs[a]))
            sent += [copy(a, 1 + j, me, (*chip, ci), src=x_refs[a]) for j, chip in enumerate(chips)]
        for cp in sent:
            cp.start()
        for j, chip in enumerate(chips):
            for a in range(n_arr):
                copy(a, 1 + j, (*chip, ci), me).wait_recv()
                onward = copy(a, 4 + j, (*chip, ci), sibling)
                onward.start()
                sent.append(onward)
        for a in range(n_arr):
            copy(a, 0, sibling, me).wait_recv()
        for j, chip in enumerate(chips):
            for a in range(n_arr):
                copy(a, 4 + j, (*chip, 1 - ci), me).wait_recv()
        for cp in sent:
            cp.wait_send()
        for cp in mine:
            cp.wait()

    sems = pltpu.SemaphoreType.DMA((per * n_arr,))
    return pl.pallas_call(
        body, name=name, out_shape=[jax.ShapeDtypeStruct((N_DEV,) + a.shape, a.dtype) for a in arrays],
        in_specs=[ANY] * n_arr, out_specs=[ANY] * n_arr,
        scratch_shapes=[sems, sems, pltpu.SemaphoreType.DMA((n_arr,))],
    )(*arrays)


def _exchange_copies(in_refs, out_refs, n_scattered, send_sems, recv_sems, local_sems):
    n_arr = len(in_refs)
    per = N_DEV - 1
    xi, yi, ci = lax.axis_index("x"), lax.axis_index("y"), lax.axis_index("c")
    me = 4 * xi + 2 * yi + ci
    src_of = lambda a, peer: in_refs[a].at[peer] if a < n_scattered else in_refs[a]
    copies = []
    for d in range(1, N_DEV):
        px = 1 - xi if d & 4 else xi
        py = 1 - yi if d & 2 else yi
        pc = 1 - ci if d & 1 else ci
        for a in range(n_arr):
            copies.append(pltpu.make_async_remote_copy(
                src_ref=src_of(a, 4 * px + 2 * py + pc), dst_ref=out_refs[a].at[me],
                send_sem=send_sems.at[per * a + d - 1], recv_sem=recv_sems.at[per * a + d - 1],
                device_id=(px, py, pc), device_id_type=MESH))
    own = [pltpu.make_async_copy(src_of(a, me), out_refs[a].at[me], local_sems.at[a]) for a in range(n_arr)]
    return copies + own


def _exchange_shapes(scattered, shared):
    return ([jax.ShapeDtypeStruct(a.shape, a.dtype) for a in scattered]
            + [jax.ShapeDtypeStruct((N_DEV,) + a.shape, a.dtype) for a in shared])


def _exchange_sems(n_arr):
    sems = pltpu.SemaphoreType.DMA(((N_DEV - 1) * n_arr,))
    return [sems, sems, pltpu.SemaphoreType.DMA((n_arr,))]


def _exchange(scattered, shared, name):
    n_sc = len(scattered)
    n_arr = n_sc + len(shared)

    def body(*refs):
        copies = _exchange_copies(refs[:n_arr], refs[n_arr:2 * n_arr], n_sc, *refs[2 * n_arr:])
        for cp in copies:
            cp.start()
        for cp in copies:
            cp.wait()

    return pl.pallas_call(
        body, name=name, out_shape=_exchange_shapes(scattered, shared),
        in_specs=[ANY] * n_arr, out_specs=[ANY] * n_arr, scratch_shapes=_exchange_sems(n_arr),
    )(*scattered, *shared)


def _adam_math(g, w, m, v):
    m_new = ADAM_B1 * m + (1.0 - ADAM_B1) * g
    v_new = ADAM_B2 * v + (1.0 - ADAM_B2) * (g * g)
    m_hat = m_new / (1.0 - ADAM_B1 ** ADAM_STEP)
    v_hat = v_new / (1.0 - ADAM_B2 ** ADAM_STEP)
    return -ADAM_LR * (m_hat / (jnp.sqrt(v_hat) + ADAM_EPS) + ADAM_WD * w), m_new, v_new


def _slot_sum(p_ref):
    g = p_ref[0].astype(F32)
    for s in range(1, N_DEV):
        g = g + p_ref[s].astype(F32)
    return g


def _adamw_replicated(parts, ws, ms, vs, loss_parts, name):
    n = len(ws)

    def body(*refs):
        p_refs, w_refs, m_refs, v_refs = refs[:n], refs[n:2 * n], refs[2 * n:3 * n], refs[3 * n:4 * n]
        outs = refs[4 * n + 1:]
        for j in range(n):
            g = _slot_sum(p_refs[j])
            outs[4 * j][...] = g
            for o_ref, val in zip(outs[4 * j + 1:4 * j + 4], _adam_math(g, w_refs[j][...], m_refs[j][...],
                                                                        v_refs[j][...])):
                o_ref[...] = val
        outs[4 * n][...] = _slot_sum(refs[4 * n])

    whole = pl.BlockSpec(memory_space=pltpu.VMEM)
    out = pl.pallas_call(
        body, name=name, in_specs=[whole] * (4 * n + 1), out_specs=[whole] * (4 * n + 1),
        out_shape=[jax.ShapeDtypeStruct(w.shape, F32) for w in ws for _ in range(4)]
                  + [jax.ShapeDtypeStruct(loss_parts.shape[1:], F32)],
    )(*parts, *ws, *ms, *vs, loss_parts)
    return [out[4 * j:4 * j + 4] for j in range(n)], out[4 * n]


def _adamw(parts, w, m, v, name):
    rows, cols = w.shape[-2:]
    tile = PACK_ROWS if rows % PACK_ROWS == 0 else rows
    at = (0,) if w.ndim == 3 else (Ellipsis,)

    def body(p_ref, w_ref, m_ref, v_ref, g_out, d_out, m_out, v_out):
        g = _slot_sum(p_ref)
        g_out[at] = g
        d_out[at], m_out[at], v_out[at] = _adam_math(g, w_ref[at], m_ref[at], v_ref[at])

    spec = (pl.BlockSpec((1, tile, cols), lambda i: (0, i, 0)) if w.ndim == 3
            else pl.BlockSpec((tile, cols), lambda i: (i, 0)))
    return pl.pallas_call(
        body, name=name, grid=(rows // tile,),
        in_specs=[pl.BlockSpec((N_DEV, tile, cols), lambda i: (0, i, 0)), spec, spec, spec],
        out_specs=[spec] * 4, out_shape=[jax.ShapeDtypeStruct(w.shape, F32)] * 4,
        compiler_params=_cparams(("parallel",)),
    )(parts, w, m, v)


EARLY = [("meta_tokens", 1), ("rwkv_w2", 1), ("rwkv_a2", 1), ("rwkv_g2", 1)]
LATE = [("w_br_attn", 1), ("w_br_rwkv", 1), ("w_o", 0), ("w_ffn_gate", 1), ("w_ffn_up", 1), ("w_ffn_down", 0)]
REPLICATED = ["norm_mix_g", "b_in", "attn_sinks", "rwkv_mix", "rwkv_w0", "rwkv_a0", "rwkv_k_k", "rwkv_k_a",
              "rwkv_r_k", "rwkv_ln_w", "rwkv_ln_b", "norm_ffn_g", "norm_final_g"]
WEIGHTS = ["meta_tokens", "norm_mix_g", "w_in", "b_in", "attn_sinks", "rwkv_mix", "rwkv_w0", "rwkv_w2", "rwkv_a0",
           "rwkv_a2", "rwkv_g2", "rwkv_k_k", "rwkv_k_a", "rwkv_r_k", "rwkv_ln_w", "rwkv_ln_b", "w_br_attn",
           "w_br_rwkv", "w_o", "norm_ffn_g", "w_ffn_gate", "w_ffn_up", "w_ffn_down", "norm_final_g"]


def _strip(name, a):
    return a if name in ("meta_tokens", "norm_final_g") else a[0]


def _join(gathered, axis):
    if axis == 0:
        return gathered.reshape(-1, gathered.shape[2])
    return gathered.transpose(1, 0, 2).reshape(gathered.shape[1], -1)


def _split(g, axis):
    if axis == 0:
        return g.reshape(N_DEV, -1, g.shape[1])
    return g.reshape(g.shape[0], N_DEV, -1).transpose(1, 0, 2)


W_IN_LAYOUT = [(768, 2304), (0, 512), (2592, 4640), (2432, 2592), 256 - GATE_LORA, (512, 768), (2304, 2368),
               128 - DECAY_LORA, (2368, 2432), 128 - AAA_LORA, NP - C_DA - 128]


def _w_in_padded(w, shard_width=None):
    rows = w.shape[-2]
    width = D_IN if shard_width is None else shard_width
    parts = []
    for seg in W_IN_LAYOUT:
        if isinstance(seg, int):
            parts.append(jnp.zeros((rows, seg), w.dtype))
            continue
        lo, stop = seg
        while lo < stop:
            p = lo // width
            hi = min(stop, (p + 1) * width)
            src = w if shard_width is None else w[p]
            parts.append(src[:, lo - p * width:hi - p * width])
            lo = hi
    return jnp.concatenate(parts, axis=1)


def _w_in_unpadded(wp, lo=0, stop=D_IN):
    spans, pos = [], 0
    for seg in W_IN_LAYOUT:
        if isinstance(seg, int):
            pos += seg
        else:
            spans.append((seg[0], seg[1], pos))
            pos += seg[1] - seg[0]
    parts = []
    for a, b, at in sorted(spans):
        c, d = max(a, lo), min(b, stop)
        if c < d:
            parts.append(wp[:, at + c - a:at + d - a])
    return jnp.concatenate(parts, axis=1)


def _pad_rows(a, n):
    return jnp.pad(a, ((0, n - a.shape[0]), (0, 0)))


def _device_step(x, tgt, full, gather_late=None, scatter_early=None, scatter_last=None):
    seq = x.shape[0]
    nblk = seq // BLOCK
    lp = seq + BLOCK
    nall = nblk + 1

    w_in_p = full["w_in_p"]
    b_in_p = _w_in_padded(full["b_in"][None])
    mix = full["rwkv_mix"][None]
    mix_r, mix_k, mix_v = mix[:, 0:512], mix[:, 512:1024], mix[:, 1024:1536]
    mix_dw = jnp.pad(mix[:, 1536:1600], ((0, 0), (0, 64)))
    mix_da = jnp.pad(mix[:, 1600:1664], ((0, 0), (0, 64)))
    mix_dg = jnp.pad(mix[:, 1664:1824], ((0, 0), (0, 96)))
    w2_p = _pad_rows(full["rwkv_w2"].astype(F32), 128)
    a2_p = _pad_rows(full["rwkv_a2"].astype(F32), 128)
    g2_p = _pad_rows(full["rwkv_g2"].astype(F32), 256)
    row = lambda name: full[name].reshape(1, -1)
    sinks = row("attn_sinks")
    rope_c, rope_s1, rope_s2 = _rope_tables(lp)

    hpad = jnp.concatenate([jnp.zeros((PAD_ROWS, D_MODEL), F32), full["meta_tokens"].astype(F32), x], axis=0)
    (u,) = _rows_fwd(_rms_fn, [_view(hpad)], [row("norm_mix_g")], [D_MODEL], nblk=nall, name="norm_mix",
                     out_dtype=BF16, tile=TILE_ALL)
    proj = _mm(u, w_in_p, bias=b_in_p, name="in_proj")
    (q_r,) = _rows_fwd(_rope_fwd_fn, [_view(proj, 512, C_Q // 512), _view(rope_c), _view(rope_s1), _view(rope_s2)],
                       [], [512], nblk=nall, name="rope_q", tile=TILE_ALL)
    (k_r,) = _rows_fwd(_rope_fwd_fn, [_view(proj, 128, C_KA // 128), _view(rope_c), _view(rope_s1),
                                      _view(rope_s2)], [], [128], nblk=nall, name="rope_k", tile=TILE_ALL)
    o_attn, lse = _attn_fwd(q_r, k_r, proj, C_VA // 128, sinks, nblk=nblk, name="attn_fwd")

    rw_cols = jnp.concatenate([proj[:, C_R:C_R + 1536], proj[:, C_DG:C_DG + 256], proj[:, C_DW:C_DW + 256]], axis=1)
    rw_prev = jnp.pad(rw_cols[:-1], ((1, 0), (0, 0)))
    pre_rows = [_view(proj, 512, 0), _view(proj, 512, 1), _view(proj, 512, 2), _view(proj, 128, C_DW // 128),
                _view(proj, 128, C_DA // 128), _view(proj, 256, C_DG // 256),
                _view(rw_prev, 512, 0), _view(rw_prev, 512, 1), _view(rw_prev, 512, 2), _view(rw_prev, 128, 14),
                _view(rw_prev, 128, 15), _view(rw_prev, 256, 6)]
    pre_consts = [mix_r, mix_k, mix_v, mix_dw, mix_da, mix_dg, row("rwkv_w0"), w2_p, row("rwkv_a0"), a2_p, g2_p,
                  row("rwkv_k_k"), row("rwkv_k_a")]
    xt_block = (BLOCK // SCAN_T * VEC_ROWS, LANES)
    r_t, k_mod, v_t, gate, xt = _rows_fwd(_rwkv_pre_xt_fn, pre_rows, pre_consts, [RW_DIM] * 4 + [xt_block],
                                          nblk=nall, name="rwkv_pre")
    xt = xt.reshape(-1, VEC_ROWS, LANES)
    spread, collect = _selectors()
    y_scan, hist, *late = _wkv_fwd(xt, v_t, spread, name="wkv_fwd", shared=gather_late[0] if gather_late else ())
    if gather_late:
        full = {**full, **gather_late[1](late)}
    post_rows = [_view(y_scan, off=1), _view(r_t, off=1), _view(k_mod, off=1), _view(v_t, off=1), _view(gate, off=1)]
    post_consts = [row("rwkv_ln_w"), row("rwkv_ln_b"), row("rwkv_r_k")]
    (y_rwkv,) = _rows_fwd(_rwkv_post_fn, post_rows, post_consts, [RW_DIM], nblk=nblk, name="rwkv_post",
                          out_dtype=BF16)

    ya = _mm(o_attn, full["w_br_attn"], name="br_attn")
    yr = _mm(y_rwkv, full["w_br_rwkv"], name="br_rwkv")
    merge_rows = [_view(ya), _view(yr), _view(proj, 1024, C_G1 // 1024, 1), _view(proj, 1024, C_G2 // 1024, 1)]
    (merged,) = _rows_fwd(_merge_fn, merge_rows, [], [D_MODEL], nblk=nblk, name="merge", out_dtype=BF16)
    h1 = _mm(merged, full["w_o"], residual=x, name="out_proj")
    (f,) = _rows_fwd(_rms_fn, [_view(h1)], [row("norm_ffn_g")], [D_MODEL], nblk=nblk, name="norm_ffn",
                     out_dtype=BF16, tile=TILE_REAL)
    ff_gate = _mm(f, full["w_ffn_gate"], name="ffn_gate")
    ff_up = _mm(f, full["w_ffn_up"], name="ffn_up")
    (act,) = _rows_fwd(_swiglu_fn, [_view(ff_gate), _view(ff_up)], [], [D_FF], nblk=nblk, name="swiglu",
                       out_dtype=BF16, tile=TILE_WIDE)
    h2 = _mm(act, full["w_ffn_down"], residual=h1, name="ffn_down")

    grads = {}
    ones_col = jnp.ones((seq, 1), F32)
    loss_rows, dh2, grads["norm_final_g"] = _rows_bwd(
        _loss_fn, [_view(h2), _view(tgt)], [row("norm_final_g")], [_view(ones_col)], nblk=nblk, name="loss",
        diff_rows=[0], diff_consts=[0], fwd_widths=[1], tile=TILE_REAL)
    loss = jnp.sum(loss_rows)

    dact = _mm(dh2, full["w_ffn_down"], tb=True, name="d_act")
    grads["w_ffn_down"] = _mm(act, dh2, ta=True, name="dw_ffn_down")
    dgate, dup = _rows_bwd(_swiglu_fn, [_view(ff_gate), _view(ff_up)], [], [_view(dact)], nblk=nblk,
                           name="swiglu_bwd", diff_rows=[0, 1], diff_consts=[], row_dtype=BF16, tile=TILE_WIDE)
    grads["w_ffn_gate"] = _mm(f, dgate, ta=True, name="dw_ffn_gate")
    grads["w_ffn_up"] = _mm(f, dup, ta=True, name="dw_ffn_up")
    df = _mm(dgate, full["w_ffn_gate"], tb=True, name="df_gate")
    df = _mm(dup, full["w_ffn_up"], tb=True, residual=df, name="df_up")
    dh1, grads["norm_ffn_g"] = _rows_bwd(_rms_fn, [_view(h1)], [row("norm_ffn_g")], [_view(df)], nblk=nblk,
                                         name="norm_ffn_bwd", diff_rows=[0], diff_consts=[0], acc=[_view(dh2)],
                                         tile=TILE_REAL)
    dmerged = _mm(dh1, full["w_o"], tb=True, name="d_merged")
    grads["w_o"] = _mm(merged, dh1, ta=True, name="dw_o")
    dya, dyr, dg1, dg2 = _rows_bwd(_merge_fn, merge_rows, [], [_view(dmerged)], nblk=nblk, name="merge_bwd",
                                   diff_rows=[0, 1, 2, 3], diff_consts=[], row_dtype=BF16)
    grads["w_br_attn"] = _mm(o_attn, dya, ta=True, name="dw_br_attn")
    grads["w_br_rwkv"] = _mm(y_rwkv, dyr, ta=True, name="dw_br_rwkv")
    dy_attn = _mm(dya, full["w_br_attn"], tb=True, name="d_y_attn")
    dy_rwkv = _mm(dyr, full["w_br_rwkv"], tb=True, name="d_y_rwkv")

    post = _rows_bwd(_rwkv_post_fn, post_rows, post_consts, [_view(dy_rwkv)], nblk=nblk, name="rwkv_post_bwd",
                     diff_rows=[0, 1, 2, 3, 4], diff_consts=[0, 1, 2])
    dys, dr_post, dk_post, dv_post, dgate_post = post[:5]
    grads["rwkv_ln_w"], grads["rwkv_ln_b"], grads["rwkv_r_k"] = post[5:]
    dxt, dv_s, *early_parts = _wkv_bwd(xt, v_t, hist, dys, spread, collect, name="wkv_bwd",
                                       scattered=scatter_early(grads) if scatter_early else ())
    pre_cts = [_view(dxt.reshape(-1, LANES), rows=xt_block[0]), _view(dv_s)] + [
        _view(t, off=-1) for t in (dr_post, dk_post, dv_post, dgate_post)]
    pre = _rows_bwd(_rwkv_pre_fn, pre_rows, pre_consts, pre_cts, nblk=nall, name="rwkv_pre_bwd",
                    diff_rows=list(range(12)), diff_consts=list(range(13)), ct_map=_rwkv_pre_cts)
    d_cur, d_prev, d_par = pre[0:6], pre[6:12], pre[12:]
    up = lambda t: jnp.pad(t[1:], ((0, 1), (0, 0)))
    d_rw = [c + up(p) for c, p in zip(d_cur, d_prev)]
    grads["rwkv_mix"] = jnp.concatenate([d_par[0], d_par[1], d_par[2], d_par[3][:, :DECAY_LORA],
                                         d_par[4][:, :AAA_LORA], d_par[5][:, :GATE_LORA]], axis=1)
    grads["rwkv_w0"], grads["rwkv_w2"] = d_par[6], d_par[7][:DECAY_LORA]
    grads["rwkv_a0"], grads["rwkv_a2"] = d_par[8], d_par[9][:AAA_LORA]
    grads["rwkv_g2"] = d_par[10][:GATE_LORA]
    grads["rwkv_k_k"], grads["rwkv_k_a"] = d_par[11], d_par[12]

    dq_real, dk_r, dva, grads["attn_sinks"] = _attn_bwd(q_r, k_r, proj, C_VA // 128, sinks, o_attn, lse, dy_attn,
                                                        nblk=nblk, name="attn_bwd")
    dq_r = jnp.pad(dq_real, ((BLOCK, 0), (0, 0)))
    (dq,) = _rows_fwd(_rope_bwd_fn, [_view(dq_r), _view(rope_c), _view(rope_s1), _view(rope_s2)], [], [512],
                      nblk=nall, name="rope_q_bwd", out_dtype=BF16, tile=TILE_ALL)
    (dka,) = _rows_fwd(_rope_bwd_fn, [_view(dk_r), _view(rope_c), _view(rope_s1),
                                      _view(rope_s2)], [], [128], nblk=nall, name="rope_k_bwd", out_dtype=BF16,
                       tile=TILE_ALL)

    lead = lambda t: jnp.pad(t, ((BLOCK, 0), (0, 0)))
    pieces = [d_rw[0], d_rw[1], d_rw[2], dq, lead(dg1), lead(dg2), d_rw[5], dka, dva, d_rw[3], d_rw[4],
              jnp.zeros((lp, NP - C_DA - 128), BF16)]
    dproj = jnp.concatenate([p.astype(BF16) for p in pieces], axis=1)
    grads["w_in_p"] = _mm(u, dproj, ta=True, name="dw_in")
    grads["b_in"] = _w_in_unpadded(_colsum(dproj, name="db_in"))
    du, *last_parts = _mm(dproj, w_in_p, tb=True, name="d_u", scattered=scatter_last(grads)) if scatter_last else (
        _mm(dproj, w_in_p, tb=True, name="d_u"),)
    dh, grads["norm_mix_g"] = _rows_bwd(_rms_fn, [_view(hpad)], [row("norm_mix_g")], [_view(du)], nblk=nall,
                                        name="norm_mix_bwd", diff_rows=[0], diff_consts=[0], acc=[_view(lead(dh1))],
                                        tile=TILE_ALL)
    grads["meta_tokens"] = dh[PAD_ROWS:BLOCK]
    return loss, dh[BLOCK:], grads, early_parts, last_parts


def kernel(x, meta_tokens, norm_mix_g, w_in, b_in, attn_sinks, rwkv_mix, rwkv_w0, rwkv_w2, rwkv_a0, rwkv_a2, rwkv_g2, rwkv_k_k, rwkv_k_a, rwkv_r_k, rwkv_ln_w, rwkv_ln_b, w_br_attn, w_br_rwkv, w_o, norm_ffn_g, w_ffn_gate, w_ffn_up, w_ffn_down, norm_final_g, loss_target, m_meta_tokens, m_norm_mix_g, m_w_in, m_b_in, m_attn_sinks, m_rwkv_mix, m_rwkv_w0, m_rwkv_w2, m_rwkv_a0, m_rwkv_a2, m_rwkv_g2, m_rwkv_k_k, m_rwkv_k_a, m_rwkv_r_k, m_rwkv_ln_w, m_rwkv_ln_b, m_w_br_attn, m_w_br_rwkv, m_w_o, m_norm_ffn_g, m_w_ffn_gate, m_w_ffn_up, m_w_ffn_down, m_norm_final_g, v_meta_tokens, v_norm_mix_g, v_w_in, v_b_in, v_attn_sinks, v_rwkv_mix, v_rwkv_w0, v_rwkv_w2, v_rwkv_a0, v_rwkv_a2, v_rwkv_g2, v_rwkv_k_k, v_rwkv_k_a, v_rwkv_r_k, v_rwkv_ln_w, v_rwkv_ln_b, v_w_br_attn, v_w_br_rwkv, v_w_o, v_norm_ffn_g, v_w_ffn_gate, v_w_ffn_up, v_w_ffn_down, v_norm_final_g):
    given = dict(locals())
    wts = {n: _strip(n, given[n]) for n in WEIGHTS}
    as_rows = lambda a: a.reshape(1, -1) if a.ndim == 1 else a
    width = wts["w_in"].shape[1]
    wire = lambda table: [wts[n].astype(BF16) for n, _ in table]

    w_in_all, *early_all = _all_gather([wts["w_in"].astype(BF16)] + wire(EARLY), name="gather_weights")
    full = {n: wts[n] for n in REPLICATED}
    full.update({n: _join(g, axis) for (n, axis), g in zip(EARLY, early_all)})
    full["w_in_p"] = _w_in_padded(w_in_all, shard_width=width)
    gather_late = (wire(LATE), lambda got: {n: _join(g, axis) for (n, axis), g in zip(LATE, got)})
    scatter_early = lambda g: [_split(g[n], axis).astype(BF16) for n, axis in LATE]
    scatter_last = lambda g: [jnp.stack([_w_in_unpadded(g["w_in_p"], p * width, (p + 1) * width)
                                         for p in range(N_DEV)]).astype(BF16)]

    loss_part, grad_x, grads, parts_late, (parts_w_in,) = _device_step(
        x[0], loss_target[0], full, gather_late, scatter_early, scatter_last)

    g_early = [_split(grads[n], axis).astype(BF16) for n, axis in EARLY]
    g_small = [grads[n].reshape(as_rows(given[n]).shape) for n in REPLICATED] + [jnp.full((8, LANES), loss_part)]
    got = _exchange(g_early, g_small, name="exchange_grads")
    parts_early, parts_small, parts_loss = got[:len(EARLY)], got[len(EARLY):-1], got[-1]
    results = [{}, {}, {}, {}]
    for (n, _), parts in zip([("w_in", 1)] + EARLY + LATE, [parts_w_in] + list(parts_early) + list(parts_late)):
        for kind, a in enumerate(_adamw(parts, given[n], given["m_" + n], given["v_" + n], name="adamw_" + n)):
            results[kind][n] = a
    small, loss = _adamw_replicated(parts_small, *[[as_rows(given[pre + n]) for n in REPLICATED]
                                                   for pre in ("", "m_", "v_")], parts_loss, name="adamw_replicated")
    for n, four in zip(REPLICATED, small):
        for kind, a in enumerate(four):
            results[kind][n] = a
    loss = loss[0, 0]
    out = [loss, grad_x[None]]
    for kind in range(4):
        out += [results[kind][n].reshape(given[n].shape) for n in WEIGHTS]
    return tuple(out)
```

```python
import functools

import jax
import jax.numpy as jnp
from jax import lax
from jax.experimental import pallas as pl
from jax.experimental.pallas import tpu as pltpu

F32 = jnp.float32
BF16 = jnp.bfloat16

N_DEV = 8
D_MODEL = 1024
N_META = 16
BLOCK = 128
PAD_ROWS = BLOCK - N_META
HEAD_DIM = 64
Q_HEADS = 8
KV_HEADS = 2
GROUP = Q_HEADS // KV_HEADS
ROPE_DIM = HEAD_DIM // 4
ROPE_HALF = ROPE_DIM // 2
ROPE_THETA = 500000.0
RW_HEADS = 8
RW_DIM = 512
DECAY_LORA = 64
AAA_LORA = 64
GATE_LORA = 160
D_FF = 2816
D_IN = 4640
RMS_EPS = 1e-6
RWKV_LN_EPS = 64e-5
NEG_INF = -1e30
SCAN_T = 16
SCAN_CHUNKS = 8
LANES = 128
PACK_ROWS = 256
TILE_ALL = 384
TILE_REAL = 512
TILE_WIDE = 256

ADAM_LR = 0.001
ADAM_B1 = 0.9
ADAM_B2 = 0.999
ADAM_EPS = 1e-08
ADAM_WD = 0.01
ADAM_STEP = 10

C_R, C_K, C_V, C_Q = 0, 512, 1024, 1536
C_G1, C_G2 = 2048, 3072
C_DG, C_KA, C_VA, C_DW, C_DA = 4096, 4352, 4480, 4608, 4736
NP = 5120

VMEM_LIMIT = 48 * 1024 * 1024
SCAN_VMEM_LIMIT = 60 * 1024 * 1024


def _cparams(sem, vmem=VMEM_LIMIT):
    return pltpu.CompilerParams(dimension_semantics=sem, vmem_limit_bytes=vmem)


def _pick(n, cands):
    for c in cands:
        if n % c == 0:
            return c
    raise ValueError(f"no tile for {n}")


def _mm(a, b, *, ta=False, tb=False, bias=None, residual=None, name, scattered=()):
    m = a.shape[1] if ta else a.shape[0]
    k = a.shape[0] if ta else a.shape[1]
    n = b.shape[0] if tb else b.shape[1]
    assert k == (b.shape[1] if tb else b.shape[0]), (a.shape, b.shape, ta, tb)
    tm = _pick(m, (512, 1408, 256, 128) if ta else (1056, 1024, 528, 512, 384, 256, 128))
    tn = _pick(n, (1024, 512, 1408, 256, 128))
    if k <= 1024:
        tk = k
    else:
        tk = _pick(k, (1024, 1056, 528, 512) if (ta and not tb) else (1024, 1408, 512, 256, 128))
    nk = k // tk
    has_bias = bias is not None
    has_res = residual is not None
    dn = (((0 if ta else 1,), (1 if tb else 0,)), ((), ()))

    def body(*refs):
        a_ref, b_ref = refs[0], refs[1]
        pos = 2
        bias_ref = res_ref = None
        if has_bias:
            bias_ref = refs[pos]
            pos += 1
        if has_res:
            res_ref = refs[pos]
            pos += 1
        o_ref, acc_ref = refs[pos], refs[pos + 1]
        kk = pl.program_id(2)
        part = lax.dot_general(a_ref[...].astype(BF16), b_ref[...].astype(BF16), dn, preferred_element_type=F32)

        def finish(out):
            if has_bias:
                out = out + bias_ref[...]
            if has_res:
                out = out + res_ref[...]
            o_ref[...] = out

        if nk == 1:
            finish(part)
        else:
            @pl.when(kk == 0)
            def _():
                acc_ref[...] = part

            @pl.when((kk > 0) & (kk < nk - 1))
            def _():
                acc_ref[...] += part

            @pl.when(kk == nk - 1)
            def _():
                finish(acc_ref[...] + part)

    in_specs = [
        pl.BlockSpec((tk, tm), lambda i, j, kk: (kk, i)) if ta else pl.BlockSpec((tm, tk), lambda i, j, kk: (i, kk)),
        pl.BlockSpec((tn, tk), lambda i, j, kk: (j, kk)) if tb else pl.BlockSpec((tk, tn), lambda i, j, kk: (kk, j)),
    ]
    args = [a, b]
    if has_bias:
        in_specs.append(pl.BlockSpec((1, tn), lambda i, j, kk: (0, j)))
        args.append(bias)
    if has_res:
        in_specs.append(pl.BlockSpec((tm, tn), lambda i, j, kk: (i, j)))
        args.append(residual)
    grid = (m // tm, n // tn, nk)
    n_x = len(scattered)
    out = pl.pallas_call(
        _with_exchange(body, len(args), 1, 1, scattered, (), grid), name=name, grid=grid,
        in_specs=in_specs + [ANY] * n_x,
        out_specs=[pl.BlockSpec((tm, tn), lambda i, j, kk: (i, j))] + [ANY] * n_x,
        out_shape=[jax.ShapeDtypeStruct((m, n), F32)] + _exchange_shapes(scattered, ()),
        scratch_shapes=[pltpu.VMEM((tm, tn) if nk > 1 else (8, LANES), F32)] + (_exchange_sems(n_x) if n_x else []),
        compiler_params=_cparams(("arbitrary",) * 3 if n_x else ("parallel", "parallel", "arbitrary")),
    )(*args, *scattered)
    return out if n_x else out[0]


def _colsum(x, name):
    m, n = x.shape
    tm = _pick(m, (512, 384, BLOCK))

    def body(x_ref, o_ref):
        i = pl.program_id(0)
        s = jnp.sum(x_ref[...].astype(F32), axis=0, keepdims=True)

        @pl.when(i == 0)
        def _():
            o_ref[...] = s

        @pl.when(i > 0)
        def _():
            o_ref[...] += s

    return pl.pallas_call(
        body, name=name, grid=(m // tm,),
        in_specs=[pl.BlockSpec((tm, n), lambda i: (i, 0))],
        out_specs=pl.BlockSpec((1, n), lambda i: (0, 0)),
        out_shape=jax.ShapeDtypeStruct((1, n), F32),
        compiler_params=_cparams(("arbitrary",)),
    )(x)


def _view(arr, width=None, col=0, off=0, rows=BLOCK):
    return (arr, arr.shape[1] if width is None else width, col, off, rows)


def _row_spec(view):
    _, width, col, off, rows = view
    if off < 0:
        return pl.BlockSpec((rows, width), lambda i, col=col, off=off: (jnp.maximum(i + off, 0), col))
    return pl.BlockSpec((rows, width), lambda i, col=col, off=off: (i + off, col))


def _const_spec(arr):
    return pl.BlockSpec(arr.shape, lambda i: (0,) * arr.ndim)


def _retile(views, tile):
    assert all(v[3] == 0 and v[4] == BLOCK for v in views)
    return [v[:4] + (tile,) for v in views]


def _rows_fwd(fn, rows, consts, out_widths, *, nblk, name, out_dtype=F32, tile=BLOCK):
    nr, nc = len(rows), len(consts)
    tile = tile if (nblk * BLOCK) % tile == 0 else BLOCK
    if tile != BLOCK:
        rows, nblk = _retile(rows, tile), nblk * BLOCK // tile
    out_blocks = [(tile, w) if isinstance(w, int) else w for w in out_widths]

    def body(*refs):
        i = pl.program_id(0)
        vals = [r[...] for r in refs[:nr + nc]]
        outs = fn(i, *vals)
        for o_ref, o in zip(refs[nr + nc:], outs):
            o_ref[...] = o.astype(o_ref.dtype)

    return pl.pallas_call(
        body, name=name, grid=(nblk,),
        in_specs=[_row_spec(v) for v in rows] + [_const_spec(c) for c in consts],
        out_specs=[pl.BlockSpec(b, lambda i: (i, 0)) for b in out_blocks],
        out_shape=[jax.ShapeDtypeStruct((nblk * r, w), out_dtype) for r, w in out_blocks],
        compiler_params=_cparams(("parallel",)),
    )(*[v[0] for v in rows], *consts)


def _rows_bwd(fn, rows, consts, cts, *, nblk, name, diff_rows, diff_consts, acc=None, fwd_widths=(), row_dtype=F32,
              ct_map=None, tile=BLOCK):
    nr, nc = len(rows), len(consts)
    acc = acc or [None] * len(diff_rows)
    tile = tile if (nblk * BLOCK) % tile == 0 else BLOCK
    if tile != BLOCK:
        rows, nblk = _retile(rows, tile), nblk * BLOCK // tile
        cts = [c if c is None else _retile([c], tile)[0] for c in cts]
        acc = [a if a is None else _retile([a], tile)[0] for a in acc]
    ct_views = [c for c in cts if c is not None]
    acc_views = [a for a in acc if a is not None]
    n_in = nr + nc + len(ct_views) + len(acc_views)
    n_fwd = len(fwd_widths)

    def body(*refs):
        i = pl.program_id(0)
        row_vals = [r[...] for r in refs[:nr]]
        const_vals = [r[...] for r in refs[nr:nr + nc]]
        ct_vals = [r[...] for r in refs[nr + nc:nr + nc + len(ct_views)]]
        acc_vals = [r[...] for r in refs[nr + nc + len(ct_views):n_in]]
        out_refs = refs[n_in:]

        def f(*dargs):
            rv = list(row_vals)
            cv = list(const_vals)
            for pos, idx in enumerate(diff_rows):
                rv[idx] = dargs[pos]
            for pos, idx in enumerate(diff_consts):
                cv[idx] = dargs[len(diff_rows) + pos]
            return tuple(fn(i, *rv, *cv))

        primals = [row_vals[idx] for idx in diff_rows] + [const_vals[idx] for idx in diff_consts]
        outs, pull = jax.vjp(f, *primals)
        full_ct, ci = [], 0
        if ct_map is not None:
            full_ct = ct_map(i, *ct_vals)
        else:
            for o, c in zip(outs, cts):
                if c is None:
                    full_ct.append(jnp.zeros_like(o))
                else:
                    full_ct.append(ct_vals[ci])
                    ci += 1
        grads = pull(tuple(full_ct))
        for o_ref, o in zip(out_refs[:n_fwd], outs):
            o_ref[...] = o
        ai = 0
        for pos in range(len(diff_rows)):
            g = grads[pos]
            if acc[pos] is not None:
                g = g + acc_vals[ai]
                ai += 1
            out_refs[n_fwd + pos][...] = g.astype(row_dtype)
        for pos in range(len(diff_consts)):
            g = grads[len(diff_rows) + pos]
            o_ref = out_refs[n_fwd + len(diff_rows) + pos]

            @pl.when(i == 0)
            def _(o_ref=o_ref, g=g):
                o_ref[...] = g

            @pl.when(i > 0)
            def _(o_ref=o_ref, g=g):
                o_ref[...] += g

    out_specs = [pl.BlockSpec((tile, w), lambda i: (i, 0)) for w in fwd_widths]
    out_shape = [jax.ShapeDtypeStruct((nblk * tile, w), F32) for w in fwd_widths]
    for idx in diff_rows:
        out_specs.append(pl.BlockSpec((tile, rows[idx][1]), lambda i: (i, 0)))
        out_shape.append(jax.ShapeDtypeStruct((nblk * tile, rows[idx][1]), row_dtype))
    for idx in diff_consts:
        out_specs.append(_const_spec(consts[idx]))
        out_shape.append(jax.ShapeDtypeStruct(consts[idx].shape, F32))
    return pl.pallas_call(
        body, name=name, grid=(nblk,),
        in_specs=([_row_spec(v) for v in rows] + [_const_spec(c) for c in consts]
                  + [_row_spec(v) for v in ct_views] + [_row_spec(v) for v in acc_views]),
        out_specs=out_specs, out_shape=out_shape,
        compiler_params=_cparams(("arbitrary",)),
    )(*[v[0] for v in rows], *consts, *[v[0] for v in ct_views], *[v[0] for v in acc_views])


def _rms_fn(i, x, g):
    return (x * lax.rsqrt(jnp.mean(x * x, axis=-1, keepdims=True) + RMS_EPS) * g,)


def _sigmoid(x):
    return 1.0 / (1.0 + jnp.exp(-x))


def _softplus(x):
    return jnp.maximum(x, 0.0) + jnp.log(1.0 + jnp.exp(-jnp.abs(x)))


def _split2(x):
    hi = x.astype(BF16)
    lo = (x - hi.astype(F32)).astype(BF16)
    return jnp.concatenate([hi, lo], axis=1)


@jax.custom_vjp
def _head_sum(x):
    r = lax.broadcasted_iota(jnp.int32, (2 * RW_DIM, RW_DIM), 0) % RW_DIM // HEAD_DIM
    c = lax.broadcasted_iota(jnp.int32, (2 * RW_DIM, RW_DIM), 1) // HEAD_DIM
    return jnp.dot(_split2(x), (r == c).astype(BF16), preferred_element_type=F32)


_head_sum.defvjp(lambda x: (_head_sum(x), None), lambda _, ct: (_head_sum(ct),))


@jax.custom_vjp
def _dot_bf16(x, w):
    return jnp.dot(x.astype(BF16), w.astype(BF16), preferred_element_type=F32)


def _dot_bf16_bwd(res, ct):
    x, w = res
    ct = ct.astype(BF16)
    dx = lax.dot_general(ct, w.astype(BF16), (((1,), (1,)), ((), ())), preferred_element_type=F32)
    dw = lax.dot_general(x.astype(BF16), ct, (((0,), (0,)), ((), ())), preferred_element_type=F32)
    return dx, dw


_dot_bf16.defvjp(lambda x, w: (_dot_bf16(x, w), (x, w)), _dot_bf16_bwd)


def _rwkv_pre_fn(i, r, k, v, dw, da, dg, r_p, k_p, v_p, dw_p, da_p, dg_p,
                 mix_r, mix_k, mix_v, mix_dw, mix_da, mix_dg, w0, w2, a0, a2, g2, k_k, k_a):
    row = i * BLOCK + lax.broadcasted_iota(jnp.int32, (BLOCK, 1), 0)
    live = row >= PAD_ROWS
    live_prev = row >= PAD_ROWS + 1

    def shift(cur, prev, mix):
        cur = jnp.where(live, cur, 0.0)
        prev = jnp.where(live_prev, prev, 0.0)
        return cur + (prev - cur) * mix

    r = shift(r, r_p, mix_r)
    k = shift(k, k_p, mix_k)
    v = shift(v, v_p, mix_v)
    dw = shift(dw, dw_p, mix_dw)
    da = shift(da, da_p, mix_da)
    dg = shift(dg, dg_p, mix_dg)
    wlog = -_softplus(-(w0 + _dot_bf16(jnp.tanh(dw), w2))) - 0.5
    decay = jnp.exp(-jnp.exp(wlog))
    a = _sigmoid(a0 + _dot_bf16(da, a2))
    g = _dot_bf16(_sigmoid(dg), g2)
    kk = k * k_k
    norm_sq = jnp.where(live, _head_sum(kk * kk), 1.0)
    kk = kk / jnp.maximum(jnp.sqrt(norm_sq), 1e-12)
    k_mod = k * (1.0 + (a - 1.0) * k_a)
    return r, decay, k_mod, v, -kk, kk * a, g


def _rwkv_pre_xt_fn(i, *args):
    r, decay, k_mod, v, a_neg, b, g = _rwkv_pre_fn(i, *args)
    t = SCAN_T
    xt = jnp.concatenate([_rows_to_xt(x[c * t:(c + 1) * t]) for c in range(BLOCK // t)
                          for x in (a_neg, decay, b, k_mod, r)], axis=0)
    return r, k_mod, v, g, xt


def _rwkv_pre_cts(i, dxt, dv_s, dr_p, dk_p, dv_p, dg_p):
    t = SCAN_T
    d_a, d_w, d_b, d_k, d_r = [
        jnp.concatenate([_xt_to_rows(dxt[c * VEC_ROWS + n * HEAD_DIM:c * VEC_ROWS + (n + 1) * HEAD_DIM])
                         for c in range(BLOCK // t)], axis=0) for n in range(N_VEC)]
    dr_p, dk_p, dv_p, dg_p = [jnp.where(i > 0, x, 0.0) for x in (dr_p, dk_p, dv_p, dg_p)]
    return d_r + dr_p, d_w, d_k + dk_p, dv_s + dv_p, d_a, d_b, dg_p


def _rwkv_post_fn(i, ys, r, k_mod, v, g, ln_w, ln_b, r_k):
    mean = _head_sum(ys) * (1.0 / HEAD_DIM)
    d = ys - mean
    var = _head_sum(d * d) * (1.0 / HEAD_DIM)
    yn = d * lax.rsqrt(var + RWKV_LN_EPS) * ln_w + ln_b
    bonus = _head_sum(r * k_mod * r_k) * v
    return ((yn + bonus) * g,)


def _merge_fn(i, ya, yr, g1, g2):
    return (_sigmoid(g1) * ya + _sigmoid(g2) * yr,)


def _swiglu_fn(i, gate, up):
    return (gate * _sigmoid(gate) * up,)


def _loss_fn(i, h, tgt, g):
    y = h * lax.rsqrt(jnp.mean(h * h, axis=-1, keepdims=True) + RMS_EPS) * g
    err = y - tgt
    return (0.5 * jnp.mean(err * err, axis=-1, keepdims=True),)


def _rope_tables(lp):
    pos = (jnp.arange(lp, dtype=jnp.int32) - PAD_ROWS).astype(F32)
    inv_freq = jnp.power(jnp.float32(ROPE_THETA), -jnp.arange(ROPE_HALF, dtype=F32) * (2.0 / ROPE_DIM))
    ang = pos[:, None] * inv_freq[None, :]
    cos, sin = jnp.cos(ang), jnp.sin(ang)
    one = jnp.ones((lp, HEAD_DIM - ROPE_DIM), F32)
    zero_h = jnp.zeros((lp, ROPE_HALF), F32)
    zero_r = jnp.zeros((lp, HEAD_DIM - ROPE_DIM), F32)
    c = jnp.concatenate([cos, cos, one], axis=1)
    s1 = jnp.concatenate([-sin, zero_h, zero_r], axis=1)
    s2 = jnp.concatenate([zero_h, sin, zero_r], axis=1)
    return tuple(jnp.tile(t, (1, LANES // HEAD_DIM)) for t in (c, s1, s2))


def _rope_fwd_fn(i, x, c, s1, s2):
    n = x.shape[1]
    c, s1, s2 = [jnp.tile(t, (1, n // LANES)) for t in (c, s1, s2)]
    return (x * c + pltpu.roll(x, n - ROPE_HALF, 1) * s1 + pltpu.roll(x, ROPE_HALF, 1) * s2,)


def _rope_bwd_fn(i, dy, c, s1, s2):
    n = dy.shape[1]
    c, s1, s2 = [jnp.tile(t, (1, n // LANES)) for t in (c, s1, s2)]
    return (dy * c + pltpu.roll(dy * s1, ROPE_HALF, 1) + pltpu.roll(dy * s2, n - ROPE_HALF, 1),)


def _attn_mask(i):
    r = lax.broadcasted_iota(jnp.int32, (BLOCK, 3 * BLOCK), 0)
    c = lax.broadcasted_iota(jnp.int32, (BLOCK, 3 * BLOCK), 1)
    meta = (c < BLOCK) & (c >= PAD_ROWS)
    prev = (c >= BLOCK) & (c < 2 * BLOCK) & ((c - BLOCK) > r) & (i >= 1)
    cur = (c >= 2 * BLOCK) & ((c - 2 * BLOCK) <= r)
    return meta | prev | cur


def _attn_rows(ref, g):
    return ref[:, g * HEAD_DIM:(g + 1) * HEAD_DIM]


def _attn_group(i, g, q_all, k_refs, v_refs, s_ref):
    heads = range(g * GROUP, (g + 1) * GROUP)
    kcat = jnp.concatenate([_attn_rows(r, g) for r in k_refs], axis=0).astype(BF16)
    vcat = jnp.concatenate([_attn_rows(r, g) for r in v_refs], axis=0).astype(BF16)
    qg = jnp.concatenate([q_all[:, h * HEAD_DIM:(h + 1) * HEAD_DIM] for h in heads], axis=0).astype(BF16)
    sink = jnp.concatenate([jnp.broadcast_to(s_ref[0:1, h:h + 1], (BLOCK, 1)) for h in heads], axis=0)
    s = lax.dot_general(qg, kcat, (((1,), (1,)), ((), ())), preferred_element_type=F32) * (HEAD_DIM ** -0.5)
    valid = jnp.concatenate([_attn_mask(i)] * GROUP, axis=0)
    return heads, qg, kcat, vcat, sink, jnp.where(valid, s, NEG_INF)


ATTN_SUB = 2


def _attn_specs(v_col):
    q = [pl.BlockSpec((BLOCK, Q_HEADS * HEAD_DIM), lambda i, n=n: (ATTN_SUB * i + 1 + n, 0)) for n in range(ATTN_SUB)]
    blk = lambda col: ([pl.BlockSpec((BLOCK, 2 * HEAD_DIM), lambda i: (0, col))]
                       + [pl.BlockSpec((BLOCK, 2 * HEAD_DIM), lambda i, n=n: (ATTN_SUB * i + n, col))
                          for n in range(ATTN_SUB + 1)])
    return q + blk(0) + blk(v_col) + [pl.BlockSpec((1, Q_HEADS), lambda i: (0, 0))]


def _attn_split(refs):
    n = ATTN_SUB
    q_refs, k_refs, v_refs = refs[:n], refs[n:2 * n + 2], refs[2 * n + 2:3 * n + 4]
    return q_refs, k_refs, v_refs, refs[3 * n + 4], refs[3 * n + 5:]


def _attn_fwd(q, k, v, v_col, sinks, *, nblk, name):
    rows = ATTN_SUB * BLOCK

    def body(*refs):
        q_refs, k_refs, v_refs, s_ref, (o_ref, lse_ref) = _attn_split(refs)
        i = pl.program_id(0)
        for sub in range(ATTN_SUB):
            at = slice(sub * BLOCK, (sub + 1) * BLOCK)
            keys = (k_refs[0], k_refs[1 + sub], k_refs[2 + sub])
            vals = (v_refs[0], v_refs[1 + sub], v_refs[2 + sub])
            q_all = q_refs[sub][...]
            for g in range(KV_HEADS):
                heads, _, _, vcat, sink, s = _attn_group(ATTN_SUB * i + sub, g, q_all, keys, vals, s_ref)
                m = jnp.maximum(jnp.max(s, axis=-1, keepdims=True), sink)
                p = jnp.exp(s - m)
                den = jnp.sum(p, axis=-1, keepdims=True) + jnp.exp(sink - m)
                o = jnp.dot(p.astype(BF16), vcat, preferred_element_type=F32) / den
                lse = m + jnp.log(den)
                for n, h in enumerate(heads):
                    o_ref[at, h * HEAD_DIM:(h + 1) * HEAD_DIM] = o[n * BLOCK:(n + 1) * BLOCK]
                    lse_ref[at, h:h + 1] = lse[n * BLOCK:(n + 1) * BLOCK]

    return pl.pallas_call(
        body, name=name, grid=(nblk // ATTN_SUB,),
        in_specs=_attn_specs(v_col),
        out_specs=[pl.BlockSpec((rows, Q_HEADS * HEAD_DIM), lambda i: (i, 0)),
                   pl.BlockSpec((rows, Q_HEADS), lambda i: (i, 0))],
        out_shape=[jax.ShapeDtypeStruct((nblk * BLOCK, Q_HEADS * HEAD_DIM), F32),
                   jax.ShapeDtypeStruct((nblk * BLOCK, Q_HEADS), F32)],
        compiler_params=_cparams(("parallel",)),
    )(*[q] * ATTN_SUB, *[k] * (ATTN_SUB + 2), *[v] * (ATTN_SUB + 2), sinks)


def _attn_bwd(q, k, v, v_col, sinks, o, lse, do, *, nblk, name):
    lp = k.shape[0]
    rows = ATTN_SUB * BLOCK

    def body(*refs):
        q_refs, k_refs, v_refs, s_ref, (o_ref, lse_ref, do_ref, dq_ref, dk_ref, dv_ref, ds_ref) = _attn_split(refs)
        i = pl.program_id(0)

        @pl.when(i == 0)
        def _():
            dk_ref[...] = jnp.zeros_like(dk_ref)
            dv_ref[...] = jnp.zeros_like(dv_ref)
            ds_ref[...] = jnp.zeros_like(ds_ref)

        lane = lax.broadcasted_iota(jnp.int32, (1, Q_HEADS), 1)
        for sub in range(ATTN_SUB):
            at = slice(sub * BLOCK, (sub + 1) * BLOCK)
            blk = ATTN_SUB * i + sub
            keys = (k_refs[0], k_refs[1 + sub], k_refs[2 + sub])
            vals = (v_refs[0], v_refs[1 + sub], v_refs[2 + sub])
            prev_rows = pl.ds(pl.multiple_of(blk * BLOCK, BLOCK), BLOCK)
            cur_rows = pl.ds(pl.multiple_of((blk + 1) * BLOCK, BLOCK), BLOCK)
            q_all, o_all, do_all, lse_all = q_refs[sub][...], o_ref[at, :], do_ref[at, :], lse_ref[at, :]
            for g in range(KV_HEADS):
                heads, qg, kcat, vcat, sink, s = _attn_group(blk, g, q_all, keys, vals, s_ref)
                stack = lambda x: jnp.concatenate([x[:, h * HEAD_DIM:(h + 1) * HEAD_DIM] for h in heads], axis=0)
                lse_g = jnp.concatenate([lse_all[:, h:h + 1] for h in heads], axis=0)
                do_g = stack(do_all)
                p = jnp.exp(s - lse_g)
                delta = jnp.sum(do_g * stack(o_all), axis=-1, keepdims=True)
                dp = lax.dot_general(do_g.astype(BF16), vcat, (((1,), (1,)), ((), ())), preferred_element_type=F32)
                dsc = (p * (dp - delta) * (HEAD_DIM ** -0.5)).astype(BF16)
                dq = jnp.dot(dsc, kcat, preferred_element_type=F32)
                dk_all = lax.dot_general(dsc, qg, (((0,), (0,)), ((), ())), preferred_element_type=F32)
                dv_all = lax.dot_general(p.astype(BF16), do_g.astype(BF16), (((0,), (0,)), ((), ())),
                                         preferred_element_type=F32)
                cols = slice(g * HEAD_DIM, (g + 1) * HEAD_DIM)
                for ref, full in ((dk_ref, dk_all), (dv_ref, dv_all)):
                    ref[0:BLOCK, cols] += full[0:BLOCK]
                    ref[prev_rows, cols] += full[BLOCK:2 * BLOCK]
                    ref[cur_rows, cols] += full[2 * BLOCK:]
                sink_part = jnp.exp(sink - lse_g) * delta
                for n, h in enumerate(heads):
                    dq_ref[at, h * HEAD_DIM:(h + 1) * HEAD_DIM] = dq[n * BLOCK:(n + 1) * BLOCK]
                    dsink = -jnp.sum(sink_part[n * BLOCK:(n + 1) * BLOCK], axis=0, keepdims=True)
                    ds_ref[...] += jnp.where(lane == h, dsink, 0.0)

    qspec = pl.BlockSpec((rows, Q_HEADS * HEAD_DIM), lambda i: (i, 0))
    whole = pl.BlockSpec((lp, 2 * HEAD_DIM), lambda i: (0, 0))
    return pl.pallas_call(
        body, name=name, grid=(nblk // ATTN_SUB,),
        in_specs=_attn_specs(v_col) + [qspec, pl.BlockSpec((rows, Q_HEADS), lambda i: (i, 0)), qspec],
        out_specs=[qspec, whole, whole, pl.BlockSpec((1, Q_HEADS), lambda i: (0, 0))],
        out_shape=[jax.ShapeDtypeStruct((nblk * BLOCK, Q_HEADS * HEAD_DIM), F32),
                   jax.ShapeDtypeStruct((lp, 2 * HEAD_DIM), F32), jax.ShapeDtypeStruct((lp, 2 * HEAD_DIM), F32),
                   jax.ShapeDtypeStruct((1, Q_HEADS), F32)],
        compiler_params=_cparams(("arbitrary",)),
    )(*[q] * ATTN_SUB, *[k] * (ATTN_SUB + 2), *[v] * (ATTN_SUB + 2), sinks, o, lse, do)


N_VEC = 5
VEC_ROWS = N_VEC * HEAD_DIM


def _selectors():
    t = SCAN_T
    shape = (t, 2 * LANES, RW_DIM)
    step, src, dst = [lax.broadcasted_iota(jnp.int32, shape, d) for d in range(3)]
    src = src % LANES
    spread = ((src // t == dst // HEAD_DIM) & (src % t == step)).astype(BF16)
    shape = (t, RW_DIM, LANES)
    step, src, dst = [lax.broadcasted_iota(jnp.int32, shape, d) for d in range(3)]
    collect = ((src // HEAD_DIM == dst // t) & (dst % t == step)).astype(BF16)
    return spread, collect


def _rows_to_xt(x):
    low = lax.broadcasted_iota(jnp.int32, (SCAN_T, LANES), 1) < HEAD_DIM
    pieces = []
    for m in range(RW_HEADS // 2):
        pair = x[:, m * LANES:(m + 1) * LANES]
        pieces += [jnp.where(low, pair, 0.0), jnp.where(low, pltpu.roll(pair, HEAD_DIM, 1), 0.0)]
    return jnp.concatenate(pieces, axis=0).T[:HEAD_DIM]


def _xt_to_rows(a):
    t = SCAN_T
    a_t = jnp.concatenate([a, jnp.zeros_like(a)], axis=0).T
    pairs = [a_t[2 * m * t:(2 * m + 1) * t] + pltpu.roll(a_t[(2 * m + 1) * t:(2 * m + 2) * t], HEAD_DIM, 1)
             for m in range(RW_HEADS // 2)]
    return jnp.concatenate(pairs, axis=1)


def _with_exchange(compute, n_in, n_out, n_scratch, scattered, shared, grid):
    n_sc = len(scattered)
    n_x = n_sc + len(shared)
    if n_x == 0:
        return compute

    def body(*refs):
        ins, x_in = refs[:n_in], refs[n_in:n_in + n_x]
        outs, x_out = refs[n_in + n_x:n_in + n_x + n_out], refs[n_in + n_x + n_out:n_in + 2 * n_x + n_out]
        scratch = refs[n_in + 2 * n_x + n_out:n_in + 2 * n_x + n_out + n_scratch]
        sems = refs[n_in + 2 * n_x + n_out + n_scratch:]

        first = last = True
        for d, size in enumerate(grid):
            first = first & (pl.program_id(d) == 0)
            last = last & (pl.program_id(d) == size - 1)

        @pl.when(first)
        def _():
            for cp in _exchange_copies(x_in, x_out, n_sc, *sems):
                cp.start()

        compute(*ins, *outs, *scratch)

        @pl.when(last)
        def _():
            for cp in _exchange_copies(x_in, x_out, n_sc, *sems):
                cp.wait()

    return body


def _wkv_fwd(xt, v, spread, name, shared=()):
    t_steps = SCAN_T
    nch = xt.shape[0]
    per = SCAN_CHUNKS
    grid = (nch // per,)
    rows = per * t_steps
    n_x = len(shared)

    def compute(xt_ref, v_ref, sel_ref, y_ref, hist_ref, st_ref):
        @pl.when(pl.program_id(0) == 0)
        def _():
            st_ref[...] = jnp.zeros_like(st_ref)

        st = st_ref[...]
        for c in range(per):
            x2 = _split2(xt_ref[c])
            for j in range(t_steps):
                row = c * t_steps + j
                cols = jnp.dot(x2, sel_ref[j], preferred_element_type=F32)
                a_c, w_c, b_c, k_c, r_c = [cols[n * HEAD_DIM:(n + 1) * HEAD_DIM] for n in range(N_VEC)]
                hist_ref[row] = st
                sa = jnp.sum(st * a_c, axis=0, keepdims=True)
                st = st * w_c + b_c * sa + k_c * v_ref[row:row + 1, :]
                y_ref[row:row + 1, :] = jnp.sum(st * r_c, axis=0, keepdims=True)
        st_ref[...] = st

    return pl.pallas_call(
        _with_exchange(compute, 3, 2, 1, (), shared, grid), name=name, grid=grid,
        in_specs=[pl.BlockSpec((per, VEC_ROWS, LANES), lambda c: (c, 0, 0)),
                  pl.BlockSpec((rows, RW_DIM), lambda c: (c, 0)),
                  pl.BlockSpec(spread.shape, lambda c: (0, 0, 0))] + [ANY] * n_x,
        out_specs=[pl.BlockSpec((rows, RW_DIM), lambda c: (c, 0)),
                   pl.BlockSpec((rows, HEAD_DIM, RW_DIM), lambda c: (c, 0, 0))] + [ANY] * n_x,
        out_shape=[jax.ShapeDtypeStruct((nch * t_steps, RW_DIM), F32),
                   jax.ShapeDtypeStruct((nch * t_steps, HEAD_DIM, RW_DIM), F32)] + _exchange_shapes((), shared),
        scratch_shapes=[pltpu.VMEM((HEAD_DIM, RW_DIM), F32)] + (_exchange_sems(n_x) if n_x else []),
        compiler_params=_cparams(("arbitrary",), SCAN_VMEM_LIMIT),
    )(xt, v, spread, *shared)


def _wkv_bwd(xt, v, hist, dy, spread, collect, name, scattered=()):
    t_steps = SCAN_T
    nch = xt.shape[0]
    n_x = len(scattered)
    per = SCAN_CHUNKS
    nsteps = nch // per
    grid = (nsteps,)
    rows = per * t_steps
    lead = BLOCK // rows

    def compute(xt_ref, v_ref, hist_ref, dy_ref, sel_ref, col_ref, dxt_ref, dv_ref, g_ref):
        @pl.when(pl.program_id(0) == 0)
        def _():
            g_ref[...] = jnp.zeros_like(g_ref)

        has_dy = nsteps - 1 - pl.program_id(0) >= lead
        gst = g_ref[...]
        nxt = None
        for c in reversed(range(per)):
            x2 = _split2(xt_ref[c])
            acc = jnp.zeros((VEC_ROWS, LANES), F32)
            for j in reversed(range(t_steps)):
                row = c * t_steps + j
                cols = jnp.dot(x2, sel_ref[j], preferred_element_type=F32)
                a_c, w_c, b_c, k_c, r_c = [cols[n * HEAD_DIM:(n + 1) * HEAD_DIM] for n in range(N_VEC)]
                prev = hist_ref[row]
                v_row = v_ref[row:row + 1, :]
                dy_row = jnp.where(has_dy, dy_ref[row:row + 1, :], 0.0)
                sa = jnp.sum(prev * a_c, axis=0, keepdims=True)
                if nxt is None:
                    nxt = prev * w_c + b_c * sa + k_c * v_row
                gst = gst + r_c * dy_row
                dv_ref[row:row + 1, :] = jnp.sum(gst * k_c, axis=0, keepdims=True)
                dsa = jnp.sum(gst * b_c, axis=0, keepdims=True)
                prods = jnp.concatenate([p.astype(BF16) for p in
                                         (prev * dsa, gst * prev, gst * sa, gst * v_row, nxt * dy_row)], axis=0)
                acc = acc + jnp.dot(prods, col_ref[j], preferred_element_type=F32)
                gst = gst * w_c + a_c * dsa
                nxt = prev
            dxt_ref[c] = acc
        g_ref[...] = gst

    rev3 = lambda c: (nsteps - 1 - c, 0, 0)
    rev2 = lambda c: (nsteps - 1 - c, 0)
    rowspec = pl.BlockSpec((rows, RW_DIM), rev2)
    return pl.pallas_call(
        _with_exchange(compute, 6, 2, 1, scattered, (), grid), name=name, grid=grid,
        in_specs=[pl.BlockSpec((per, VEC_ROWS, LANES), rev3), rowspec,
                  pl.BlockSpec((rows, HEAD_DIM, RW_DIM), rev3),
                  pl.BlockSpec((rows, RW_DIM), lambda c: (jnp.maximum(nsteps - 1 - c - lead, 0), 0)),
                  pl.BlockSpec(spread.shape, lambda c: (0, 0, 0)),
                  pl.BlockSpec(collect.shape, lambda c: (0, 0, 0))] + [ANY] * n_x,
        out_specs=[pl.BlockSpec((per, VEC_ROWS, LANES), rev3), rowspec] + [ANY] * n_x,
        out_shape=[jax.ShapeDtypeStruct((nch, VEC_ROWS, LANES), F32),
                   jax.ShapeDtypeStruct((nch * t_steps, RW_DIM), F32)] + _exchange_shapes(scattered, ()),
        scratch_shapes=[pltpu.VMEM((HEAD_DIM, RW_DIM), F32)] + (_exchange_sems(n_x) if n_x else []),
        compiler_params=_cparams(("arbitrary",), SCAN_VMEM_LIMIT),
    )(xt, v, hist, dy, spread, collect, *scattered)


MESH = pl.DeviceIdType.MESH
ANY = pl.BlockSpec(memory_space=pltpu.HBM)


def _all_gather(arrays, name):
    n_arr = len(arrays)
    per = N_DEV - 1

    def body(*refs):
        x_refs, out_refs = refs[:n_arr], refs[n_arr:2 * n_arr]
        send_sems, recv_sems, local_sems = refs[2 * n_arr:]
        xi, yi, ci = lax.axis_index("x"), lax.axis_index("y"), lax.axis_index("c")
        me, sibling = (xi, yi, ci), (xi, yi, 1 - ci)
        chips = [(1 - xi, yi), (xi, 1 - yi), (1 - xi, 1 - yi)]

        def slot(a, px, py, pc):
            return out_refs[a].at[4 * px + 2 * py + pc]

        def copy(a, sem, block, to, src=None):
            return pltpu.make_async_remote_copy(
                src_ref=slot(a, *block) if src is None else src, dst_ref=slot(a, *block),
                send_sem=send_sems.at[per * a + sem], recv_sem=recv_sems.at[per * a + sem],
                device_id=to, device_id_type=MESH)

        mine = [pltpu.make_async_copy(x_refs[a], slot(a, *me), local_sems.at[a]) for a in range(n_arr)]
        for cp in mine:
            cp.start()
        sent = []
        for a in range(n_arr):
            sent.append(copy(a, 0, me, sibling, src=x_refs[a]))
            sent += [copy(a, 1 + j, me, (*chip, ci), src=x_refs[a]) for j, chip in enumerate(chips)]
        for cp in sent:
            cp.start()
        for j, chip in enumerate(chips):
            for a in range(n_arr):
                copy(a, 1 + j, (*chip, ci), me).wait_recv()
                onward = copy(a, 4 + j, (*chip, ci), sibling)
                onward.start()
                sent.append(onward)
        for a in range(n_arr):
            copy(a, 0, sibling, me).wait_recv()
        for j, chip in enumerate(chips):
            for a in range(n_arr):
                copy(a, 4 + j, (*chip, 1 - ci), me).wait_recv()
        for cp in sent:
            cp.wait_send()
        for cp in mine:
            cp.wait()

    sems = pltpu.SemaphoreType.DMA((per * n_arr,))
    return pl.pallas_call(
        body, name=name, out_shape=[jax.ShapeDtypeStruct((N_DEV,) + a.shape, a.dtype) for a in arrays],
        in_specs=[ANY] * n_arr, out_specs=[ANY] * n_arr,
        scratch_shapes=[sems, sems, pltpu.SemaphoreType.DMA((n_arr,))],
    )(*arrays)


def _exchange_copies(in_refs, out_refs, n_scattered, send_sems, recv_sems, local_sems):
    n_arr = len(in_refs)
    per = N_DEV - 1
    xi, yi, ci = lax.axis_index("x"), lax.axis_index("y"), lax.axis_index("c")
    me = 4 * xi + 2 * yi + ci
    src_of = lambda a, peer: in_refs[a].at[peer] if a < n_scattered else in_refs[a]
    copies = []
    for d in range(1, N_DEV):
        px = 1 - xi if d & 4 else xi
        py = 1 - yi if d & 2 else yi
        pc = 1 - ci if d & 1 else ci
        for a in range(n_arr):
            copies.append(pltpu.make_async_remote_copy(
                src_ref=src_of(a, 4 * px + 2 * py + pc), dst_ref=out_refs[a].at[me],
                send_sem=send_sems.at[per * a + d - 1], recv_sem=recv_sems.at[per * a + d - 1],
                device_id=(px, py, pc), device_id_type=MESH))
    own = [pltpu.make_async_copy(src_of(a, me), out_refs[a].at[me], local_sems.at[a]) for a in range(n_arr)]
    return copies + own


def _exchange_shapes(scattered, shared):
    return ([jax.ShapeDtypeStruct(a.shape, a.dtype) for a in scattered]
            + [jax.ShapeDtypeStruct((N_DEV,) + a.shape, a.dtype) for a in shared])


def _exchange_sems(n_arr):
    sems = pltpu.SemaphoreType.DMA(((N_DEV - 1) * n_arr,))
    return [sems, sems, pltpu.SemaphoreType.DMA((n_arr,))]


def _exchange(scattered, shared, name):
    n_sc = len(scattered)
    n_arr = n_sc + len(shared)

    def body(*refs):
        copies = _exchange_copies(refs[:n_arr], refs[n_arr:2 * n_arr], n_sc, *refs[2 * n_arr:])
        for cp in copies:
            cp.start()
        for cp in copies:
            cp.wait()

    return pl.pallas_call(
        body, name=name, out_shape=_exchange_shapes(scattered, shared),
        in_specs=[ANY] * n_arr, out_specs=[ANY] * n_arr, scratch_shapes=_exchange_sems(n_arr),
    )(*scattered, *shared)


def _adam_math(g, w, m, v):
    m_new = ADAM_B1 * m + (1.0 - ADAM_B1) * g
    v_new = ADAM_B2 * v + (1.0 - ADAM_B2) * (g * g)
    m_hat = m_new / (1.0 - ADAM_B1 ** ADAM_STEP)
    v_hat = v_new / (1.0 - ADAM_B2 ** ADAM_STEP)
    return -ADAM_LR * (m_hat / (jnp.sqrt(v_hat) + ADAM_EPS) + ADAM_WD * w), m_new, v_new


def _slot_sum(p_ref):
    g = p_ref[0].astype(F32)
    for s in range(1, N_DEV):
        g = g + p_ref[s].astype(F32)
    return g


def _adamw_replicated(parts, ws, ms, vs, loss_parts, name):
    n = len(ws)

    def body(*refs):
        p_refs, w_refs, m_refs, v_refs = refs[:n], refs[n:2 * n], refs[2 * n:3 * n], refs[3 * n:4 * n]
        outs = refs[4 * n + 1:]
        for j in range(n):
            g = _slot_sum(p_refs[j])
            outs[4 * j][...] = g
            for o_ref, val in zip(outs[4 * j + 1:4 * j + 4], _adam_math(g, w_refs[j][...], m_refs[j][...],
                                                                        v_refs[j][...])):
                o_ref[...] = val
        outs[4 * n][...] = _slot_sum(refs[4 * n])

    whole = pl.BlockSpec(memory_space=pltpu.VMEM)
    out = pl.pallas_call(
        body, name=name, in_specs=[whole] * (4 * n + 1), out_specs=[whole] * (4 * n + 1),
        out_shape=[jax.ShapeDtypeStruct(w.shape, F32) for w in ws for _ in range(4)]
                  + [jax.ShapeDtypeStruct(loss_parts.shape[1:], F32)],
    )(*parts, *ws, *ms, *vs, loss_parts)
    return [out[4 * j:4 * j + 4] for j in range(n)], out[4 * n]


def _adamw(parts, w, m, v, name):
    rows, cols = w.shape[-2:]
    tile = PACK_ROWS if rows % PACK_ROWS == 0 else rows
    at = (0,) if w.ndim == 3 else (Ellipsis,)

    def body(p_ref, w_ref, m_ref, v_ref, g_out, d_out, m_out, v_out):
        g = _slot_sum(p_ref)
        g_out[at] = g
        d_out[at], m_out[at], v_out[at] = _adam_math(g, w_ref[at], m_ref[at], v_ref[at])

    spec = (pl.BlockSpec((1, tile, cols), lambda i: (0, i, 0)) if w.ndim == 3
            else pl.BlockSpec((tile, cols), lambda i: (i, 0)))
    return pl.pallas_call(
        body, name=name, grid=(rows // tile,),
        in_specs=[pl.BlockSpec((N_DEV, tile, cols), lambda i: (0, i, 0)), spec, spec, spec],
        out_specs=[spec] * 4, out_shape=[jax.ShapeDtypeStruct(w.shape, F32)] * 4,
        compiler_params=_cparams(("parallel",)),
    )(parts, w, m, v)


EARLY = [("meta_tokens", 1), ("rwkv_w2", 1), ("rwkv_a2", 1), ("rwkv_g2", 1)]
LATE = [("w_br_attn", 1), ("w_br_rwkv", 1), ("w_o", 0), ("w_ffn_gate", 1), ("w_ffn_up", 1), ("w_ffn_down", 0)]
REPLICATED = ["norm_mix_g", "b_in", "attn_sinks", "rwkv_mix", "rwkv_w0", "rwkv_a0", "rwkv_k_k", "rwkv_k_a",
              "rwkv_r_k", "rwkv_ln_w", "rwkv_ln_b", "norm_ffn_g", "norm_final_g"]
WEIGHTS = ["meta_tokens", "norm_mix_g", "w_in", "b_in", "attn_sinks", "rwkv_mix", "rwkv_w0", "rwkv_w2", "rwkv_a0",
           "rwkv_a2", "rwkv_g2", "rwkv_k_k", "rwkv_k_a", "rwkv_r_k", "rwkv_ln_w", "rwkv_ln_b", "w_br_attn",
           "w_br_rwkv", "w_o", "norm_ffn_g", "w_ffn_gate", "w_ffn_up", "w_ffn_down", "norm_final_g"]


def _strip(name, a):
    return a if name in ("meta_tokens", "norm_final_g") else a[0]


def _join(gathered, axis):
    if axis == 0:
        return gathered.reshape(-1, gathered.shape[2])
    return gathered.transpose(1, 0, 2).reshape(gathered.shape[1], -1)


def _split(g, axis):
    if axis == 0:
        return g.reshape(N_DEV, -1, g.shape[1])
    return g.reshape(g.shape[0], N_DEV, -1).transpose(1, 0, 2)


W_IN_LAYOUT = [(768, 2304), (0, 512), (2592, 4640), (2432, 2592), 256 - GATE_LORA, (512, 768), (2304, 2368),
               128 - DECAY_LORA, (2368, 2432), 128 - AAA_LORA, NP - C_DA - 128]


def _w_in_padded(w, shard_width=None):
    rows = w.shape[-2]
    width = D_IN if shard_width is None else shard_width
    parts = []
    for seg in W_IN_LAYOUT:
        if isinstance(seg, int):
            parts.append(jnp.zeros((rows, seg), w.dtype))
            continue
        lo, stop = seg
        while lo < stop:
            p = lo // width
            hi = min(stop, (p + 1) * width)
            src = w if shard_width is None else w[p]
            parts.append(src[:, lo - p * width:hi - p * width])
            lo = hi
    return jnp.concatenate(parts, axis=1)


def _w_in_unpadded(wp, lo=0, stop=D_IN):
    spans, pos = [], 0
    for seg in W_IN_LAYOUT:
        if isinstance(seg, int):
            pos += seg
        else:
            spans.append((seg[0], seg[1], pos))
            pos += seg[1] - seg[0]
    parts = []
    for a, b, at in sorted(spans):
        c, d = max(a, lo), min(b, stop)
        if c < d:
            parts.append(wp[:, at + c - a:at + d - a])
    return jnp.concatenate(parts, axis=1)


def _pad_rows(a, n):
    return jnp.pad(a, ((0, n - a.shape[0]), (0, 0)))


def _device_step(x, tgt, full, gather_late=None, scatter_early=None, scatter_last=None):
    seq = x.shape[0]
    nblk = seq // BLOCK
    lp = seq + BLOCK
    nall = nblk + 1

    w_in_p = full["w_in_p"]
    b_in_p = _w_in_padded(full["b_in"][None])
    mix = full["rwkv_mix"][None]
    mix_r, mix_k, mix_v = mix[:, 0:512], mix[:, 512:1024], mix[:, 1024:1536]
    mix_dw = jnp.pad(mix[:, 1536:1600], ((0, 0), (0, 64)))
    mix_da = jnp.pad(mix[:, 1600:1664], ((0, 0), (0, 64)))
    mix_dg = jnp.pad(mix[:, 1664:1824], ((0, 0), (0, 96)))
    w2_p = _pad_rows(full["rwkv_w2"].astype(F32), 128)
    a2_p = _pad_rows(full["rwkv_a2"].astype(F32), 128)
    g2_p = _pad_rows(full["rwkv_g2"].astype(F32), 256)
    row = lambda name: full[name].reshape(1, -1)
    sinks = row("attn_sinks")
    rope_c, rope_s1, rope_s2 = _rope_tables(lp)

    hpad = jnp.concatenate([jnp.zeros((PAD_ROWS, D_MODEL), F32), full["meta_tokens"].astype(F32), x], axis=0)
    (u,) = _rows_fwd(_rms_fn, [_view(hpad)], [row("norm_mix_g")], [D_MODEL], nblk=nall, name="norm_mix",
                     out_dtype=BF16, tile=TILE_ALL)
    proj = _mm(u, w_in_p, bias=b_in_p, name="in_proj")
    (q_r,) = _rows_fwd(_rope_fwd_fn, [_view(proj, 512, C_Q // 512), _view(rope_c), _view(rope_s1), _view(rope_s2)],
                       [], [512], nblk=nall, name="rope_q", tile=TILE_ALL)
    (k_r,) = _rows_fwd(_rope_fwd_fn, [_view(proj, 128, C_KA // 128), _view(rope_c), _view(rope_s1),
                                      _view(rope_s2)], [], [128], nblk=nall, name="rope_k", tile=TILE_ALL)
    o_attn, lse = _attn_fwd(q_r, k_r, proj, C_VA // 128, sinks, nblk=nblk, name="attn_fwd")

    rw_cols = jnp.concatenate([proj[:, C_R:C_R + 1536], proj[:, C_DG:C_DG + 256], proj[:, C_DW:C_DW + 256]], axis=1)
    rw_prev = jnp.pad(rw_cols[:-1], ((1, 0), (0, 0)))
    pre_rows = [_view(proj, 512, 0), _view(proj, 512, 1), _view(proj, 512, 2), _view(proj, 128, C_DW // 128),
                _view(proj, 128, C_DA // 128), _view(proj, 256, C_DG // 256),
                _view(rw_prev, 512, 0), _view(rw_prev, 512, 1), _view(rw_prev, 512, 2), _view(rw_prev, 128, 14),
                _view(rw_prev, 128, 15), _view(rw_prev, 256, 6)]
    pre_consts = [mix_r, mix_k, mix_v, mix_dw, mix_da, mix_dg, row("rwkv_w0"), w2_p, row("rwkv_a0"), a2_p, g2_p,
                  row("rwkv_k_k"), row("rwkv_k_a")]
    xt_block = (BLOCK // SCAN_T * VEC_ROWS, LANES)
    r_t, k_mod, v_t, gate, xt = _rows_fwd(_rwkv_pre_xt_fn, pre_rows, pre_consts, [RW_DIM] * 4 + [xt_block],
                                          nblk=nall, name="rwkv_pre")
    xt = xt.reshape(-1, VEC_ROWS, LANES)
    spread, collect = _selectors()
    y_scan, hist, *late = _wkv_fwd(xt, v_t, spread, name="wkv_fwd", shared=gather_late[0] if gather_late else ())
    if gather_late:
        full = {**full, **gather_late[1](late)}
    post_rows = [_view(y_scan, off=1), _view(r_t, off=1), _view(k_mod, off=1), _view(v_t, off=1), _view(gate, off=1)]
    post_consts = [row("rwkv_ln_w"), row("rwkv_ln_b"), row("rwkv_r_k")]
    (y_rwkv,) = _rows_fwd(_rwkv_post_fn, post_rows, post_consts, [RW_DIM], nblk=nblk, name="rwkv_post",
                          out_dtype=BF16)

    ya = _mm(o_attn, full["w_br_attn"], name="br_attn")
    yr = _mm(y_rwkv, full["w_br_rwkv"], name="br_rwkv")
    merge_rows = [_view(ya), _view(yr), _view(proj, 1024, C_G1 // 1024, 1), _view(proj, 1024, C_G2 // 1024, 1)]
    (merged,) = _rows_fwd(_merge_fn, merge_rows, [], [D_MODEL], nblk=nblk, name="merge", out_dtype=BF16)
    h1 = _mm(merged, full["w_o"], residual=x, name="out_proj")
    (f,) = _rows_fwd(_rms_fn, [_view(h1)], [row("norm_ffn_g")], [D_MODEL], nblk=nblk, name="norm_ffn",
                     out_dtype=BF16, tile=TILE_REAL)
    ff_gate = _mm(f, full["w_ffn_gate"], name="ffn_gate")
    ff_up = _mm(f, full["w_ffn_up"], name="ffn_up")
    (act,) = _rows_fwd(_swiglu_fn, [_view(ff_gate), _view(ff_up)], [], [D_FF], nblk=nblk, name="swiglu",
                       out_dtype=BF16, tile=TILE_WIDE)
    h2 = _mm(act, full["w_ffn_down"], residual=h1, name="ffn_down")

    grads = {}
    ones_col = jnp.ones((seq, 1), F32)
    loss_rows, dh2, grads["norm_final_g"] = _rows_bwd(
        _loss_fn, [_view(h2), _view(tgt)], [row("norm_final_g")], [_view(ones_col)], nblk=nblk, name="loss",
        diff_rows=[0], diff_consts=[0], fwd_widths=[1], tile=TILE_REAL)
    loss = jnp.sum(loss_rows)

    dact = _mm(dh2, full["w_ffn_down"], tb=True, name="d_act")
    grads["w_ffn_down"] = _mm(act, dh2, ta=True, name="dw_ffn_down")
    dgate, dup = _rows_bwd(_swiglu_fn, [_view(ff_gate), _view(ff_up)], [], [_view(dact)], nblk=nblk,
                           name="swiglu_bwd", diff_rows=[0, 1], diff_consts=[], row_dtype=BF16, tile=TILE_WIDE)
    grads["w_ffn_gate"] = _mm(f, dgate, ta=True, name="dw_ffn_gate")
    grads["w_ffn_up"] = _mm(f, dup, ta=True, name="dw_ffn_up")
    df = _mm(dgate, full["w_ffn_gate"], tb=True, name="df_gate")
    df = _mm(dup, full["w_ffn_up"], tb=True, residual=df, name="df_up")
    dh1, grads["norm_ffn_g"] = _rows_bwd(_rms_fn, [_view(h1)], [row("norm_ffn_g")], [_view(df)], nblk=nblk,
                                         name="norm_ffn_bwd", diff_rows=[0], diff_consts=[0], acc=[_view(dh2)],
                                         tile=TILE_REAL)
    dmerged = _mm(dh1, full["w_o"], tb=True, name="d_merged")
    grads["w_o"] = _mm(merged, dh1, ta=True, name="dw_o")
    dya, dyr, dg1, dg2 = _rows_bwd(_merge_fn, merge_rows, [], [_view(dmerged)], nblk=nblk, name="merge_bwd",
                                   diff_rows=[0, 1, 2, 3], diff_consts=[], row_dtype=BF16)
    grads["w_br_attn"] = _mm(o_attn, dya, ta=True, name="dw_br_attn")
    grads["w_br_rwkv"] = _mm(y_rwkv, dyr, ta=True, name="dw_br_rwkv")
    dy_attn = _mm(dya, full["w_br_attn"], tb=True, name="d_y_attn")
    dy_rwkv = _mm(dyr, full["w_br_rwkv"], tb=True, name="d_y_rwkv")

    post = _rows_bwd(_rwkv_post_fn, post_rows, post_consts, [_view(dy_rwkv)], nblk=nblk, name="rwkv_post_bwd",
                     diff_rows=[0, 1, 2, 3, 4], diff_consts=[0, 1, 2])
    dys, dr_post, dk_post, dv_post, dgate_post = post[:5]
    grads["rwkv_ln_w"], grads["rwkv_ln_b"], grads["rwkv_r_k"] = post[5:]
    dxt, dv_s, *early_parts = _wkv_bwd(xt, v_t, hist, dys, spread, collect, name="wkv_bwd",
                                       scattered=scatter_early(grads) if scatter_early else ())
    pre_cts = [_view(dxt.reshape(-1, LANES), rows=xt_block[0]), _view(dv_s)] + [
        _view(t, off=-1) for t in (dr_post, dk_post, dv_post, dgate_post)]
    pre = _rows_bwd(_rwkv_pre_fn, pre_rows, pre_consts, pre_cts, nblk=nall, name="rwkv_pre_bwd",
                    diff_rows=list(range(12)), diff_consts=list(range(13)), ct_map=_rwkv_pre_cts)
    d_cur, d_prev, d_par = pre[0:6], pre[6:12], pre[12:]
    up = lambda t: jnp.pad(t[1:], ((0, 1), (0, 0)))
    d_rw = [c + up(p) for c, p in zip(d_cur, d_prev)]
    grads["rwkv_mix"] = jnp.concatenate([d_par[0], d_par[1], d_par[2], d_par[3][:, :DECAY_LORA],
                                         d_par[4][:, :AAA_LORA], d_par[5][:, :GATE_LORA]], axis=1)
    grads["rwkv_w0"], grads["rwkv_w2"] = d_par[6], d_par[7][:DECAY_LORA]
    grads["rwkv_a0"], grads["rwkv_a2"] = d_par[8], d_par[9][:AAA_LORA]
    grads["rwkv_g2"] = d_par[10][:GATE_LORA]
    grads["rwkv_k_k"], grads["rwkv_k_a"] = d_par[11], d_par[12]

    dq_real, dk_r, dva, grads["attn_sinks"] = _attn_bwd(q_r, k_r, proj, C_VA // 128, sinks, o_attn, lse, dy_attn,
                                                        nblk=nblk, name="attn_bwd")
    dq_r = jnp.pad(dq_real, ((BLOCK, 0), (0, 0)))
    (dq,) = _rows_fwd(_rope_bwd_fn, [_view(dq_r), _view(rope_c), _view(rope_s1), _view(rope_s2)], [], [512],
                      nblk=nall, name="rope_q_bwd", out_dtype=BF16, tile=TILE_ALL)
    (dka,) = _rows_fwd(_rope_bwd_fn, [_view(dk_r), _view(rope_c), _view(rope_s1),
                                      _view(rope_s2)], [], [128], nblk=nall, name="rope_k_bwd", out_dtype=BF16,
                       tile=TILE_ALL)

    lead = lambda t: jnp.pad(t, ((BLOCK, 0), (0, 0)))
    pieces = [d_rw[0], d_rw[1], d_rw[2], dq, lead(dg1), lead(dg2), d_rw[5], dka, dva, d_rw[3], d_rw[4],
              jnp.zeros((lp, NP - C_DA - 128), BF16)]
    dproj = jnp.concatenate([p.astype(BF16) for p in pieces], axis=1)
    grads["w_in_p"] = _mm(u, dproj, ta=True, name="dw_in")
    grads["b_in"] = _w_in_unpadded(_colsum(dproj, name="db_in"))
    du, *last_parts = _mm(dproj, w_in_p, tb=True, name="d_u", scattered=scatter_last(grads)) if scatter_last else (
        _mm(dproj, w_in_p, tb=True, name="d_u"),)
    dh, grads["norm_mix_g"] = _rows_bwd(_rms_fn, [_view(hpad)], [row("norm_mix_g")], [_view(du)], nblk=nall,
                                        name="norm_mix_bwd", diff_rows=[0], diff_consts=[0], acc=[_view(lead(dh1))],
                                        tile=TILE_ALL)
    grads["meta_tokens"] = dh[PAD_ROWS:BLOCK]
    return loss, dh[BLOCK:], grads, early_parts, last_parts


def kernel(x, meta_tokens, norm_mix_g, w_in, b_in, attn_sinks, rwkv_mix, rwkv_w0, rwkv_w2, rwkv_a0, rwkv_a2, rwkv_g2, rwkv_k_k, rwkv_k_a, rwkv_r_k, rwkv_ln_w, rwkv_ln_b, w_br_attn, w_br_rwkv, w_o, norm_ffn_g, w_ffn_gate, w_ffn_up, w_ffn_down, norm_final_g, loss_target, m_meta_tokens, m_norm_mix_g, m_w_in, m_b_in, m_attn_sinks, m_rwkv_mix, m_rwkv_w0, m_rwkv_w2, m_rwkv_a0, m_rwkv_a2, m_rwkv_g2, m_rwkv_k_k, m_rwkv_k_a, m_rwkv_r_k, m_rwkv_ln_w, m_rwkv_ln_b, m_w_br_attn, m_w_br_rwkv, m_w_o, m_norm_ffn_g, m_w_ffn_gate, m_w_ffn_up, m_w_ffn_down, m_norm_final_g, v_meta_tokens, v_norm_mix_g, v_w_in, v_b_in, v_attn_sinks, v_rwkv_mix, v_rwkv_w0, v_rwkv_w2, v_rwkv_a0, v_rwkv_a2, v_rwkv_g2, v_rwkv_k_k, v_rwkv_k_a, v_rwkv_r_k, v_rwkv_ln_w, v_rwkv_ln_b, v_w_br_attn, v_w_br_rwkv, v_w_o, v_norm_ffn_g, v_w_ffn_gate, v_w_ffn_up, v_w_ffn_down, v_norm_final_g):
    given = dict(locals())
    wts = {n: _strip(n, given[n]) for n in WEIGHTS}
    as_rows = lambda a: a.reshape(1, -1) if a.ndim == 1 else a
    width = wts["w_in"].shape[1]
    wire = lambda table: [wts[n].astype(BF16) for n, _ in table]

    w_in_all, *early_all = _all_gather([wts["w_in"].astype(BF16)] + wire(EARLY), name="gather_weights")
    full = {n: wts[n] for n in REPLICATED}
    full.update({n: _join(g, axis) for (n, axis), g in zip(EARLY, early_all)})
    full["w_in_p"] = _w_in_padded(w_in_all, shard_width=width)
    gather_late = (wire(LATE), lambda got: {n: _join(g, axis) for (n, axis), g in zip(LATE, got)})
    scatter_early = lambda g: [_split(g[n], axis).astype(BF16) for n, axis in LATE]
    scatter_last = lambda g: [jnp.stack([_w_in_unpadded(g["w_in_p"], p * width, (p + 1) * width)
                                         for p in range(N_DEV)]).astype(BF16)]

    loss_part, grad_x, grads, parts_late, (parts_w_in,) = _device_step(
        x[0], loss_target[0], full, gather_late, scatter_early, scatter_last)

    g_early = [_split(grads[n], axis).astype(BF16) for n, axis in EARLY]
    g_small = [grads[n].reshape(as_rows(given[n]).shape) for n in REPLICATED] + [jnp.full((8, LANES), loss_part)]
    got = _exchange(g_early, g_small, name="exchange_grads")
    parts_early, parts_small, parts_loss = got[:len(EARLY)], got[len(EARLY):-1], got[-1]
    results = [{}, {}, {}, {}]
    for (n, _), parts in zip([("w_in", 1)] + EARLY + LATE, [parts_w_in] + list(parts_early) + list(parts_late)):
        for kind, a in enumerate(_adamw(parts, given[n], given["m_" + n], given["v_" + n], name="adamw_" + n)):
            results[kind][n] = a
    small, loss = _adamw_replicated(parts_small, *[[as_rows(given[pre + n]) for n in REPLICATED]
                                                   for pre in ("", "m_", "v_")], parts_loss, name="adamw_replicated")
    for n, four in zip(REPLICATED, small):
        for kind, a in enumerate(four):
            results[kind][n] = a
    loss = loss[0, 0]
    out = [loss, grad_x[None]]
    for kind in range(4):
        out += [results[kind][n].reshape(given[n].shape) for n in WEIGHTS]
    return tuple(out)
```

```python
import functools

import jax
import jax.numpy as jnp
from jax import lax
from jax.experimental import pallas as pl
from jax.experimental.pallas import tpu as pltpu

F32 = jnp.float32
BF16 = jnp.bfloat16

N_DEV = 8
D_MODEL = 1024
N_META = 16
BLOCK = 128
PAD_ROWS = BLOCK - N_META
HEAD_DIM = 64
Q_HEADS = 8
KV_HEADS = 2
GROUP = Q_HEADS // KV_HEADS
ROPE_DIM = HEAD_DIM // 4
ROPE_HALF = ROPE_DIM // 2
ROPE_THETA = 500000.0
RW_HEADS = 8
RW_DIM = 512
DECAY_LORA = 64
AAA_LORA = 64
GATE_LORA = 160
D_FF = 2816
D_IN = 4640
RMS_EPS = 1e-6
RWKV_LN_EPS = 64e-5
NEG_INF = -1e30
SCAN_T = 16
SCAN_CHUNKS = 8
LANES = 128
PACK_ROWS = 256
TILE_ALL = 384
TILE_REAL = 512
TILE_WIDE = 256

ADAM_LR = 0.001
ADAM_B1 = 0.9
ADAM_B2 = 0.999
ADAM_EPS = 1e-08
ADAM_WD = 0.01
ADAM_STEP = 10

C_R, C_K, C_V, C_Q = 0, 512, 1024, 1536
C_G1, C_G2 = 2048, 3072
C_DG, C_KA, C_VA, C_DW, C_DA = 4096, 4352, 4480, 4608, 4736
NP = 5120

VMEM_LIMIT = 48 * 1024 * 1024
SCAN_VMEM_LIMIT = 60 * 1024 * 1024


def _cparams(sem, vmem=VMEM_LIMIT):
    return pltpu.CompilerParams(dimension_semantics=sem, vmem_limit_bytes=vmem)


def _pick(n, cands):
    for c in cands:
        if n % c == 0:
            return c
    raise ValueError(f"no tile for {n}")


def _mm(a, b, *, ta=False, tb=False, bias=None, residual=None, name, scattered=()):
    m = a.shape[1] if ta else a.shape[0]
    k = a.shape[0] if ta else a.shape[1]
    n = b.shape[0] if tb else b.shape[1]
    assert k == (b.shape[1] if tb else b.shape[0]), (a.shape, b.shape, ta, tb)
    tm = _pick(m, (512, 1408, 256, 128) if ta else (1056, 1024, 528, 512, 384, 256, 128))
    tn = _pick(n, (1024, 512, 1408, 256, 128))
    if k <= 1024:
        tk = k
    else:
        tk = _pick(k, (1024, 1056, 528, 512) if (ta and not tb) else (1024, 1408, 512, 256, 128))
    nk = k // tk
    has_bias = bias is not None
    has_res = residual is not None
    dn = (((0 if ta else 1,), (1 if tb else 0,)), ((), ()))

    def body(*refs):
        a_ref, b_ref = refs[0], refs[1]
        pos = 2
        bias_ref = res_ref = None
        if has_bias:
            bias_ref = refs[pos]
            pos += 1
        if has_res:
            res_ref = refs[pos]
            pos += 1
        o_ref, acc_ref = refs[pos], refs[pos + 1]
        kk = pl.program_id(2)
        part = lax.dot_general(a_ref[...].astype(BF16), b_ref[...].astype(BF16), dn, preferred_element_type=F32)

        def finish(out):
            if has_bias:
                out = out + bias_ref[...]
            if has_res:
                out = out + res_ref[...]
            o_ref[...] = out

        if nk == 1:
            finish(part)
        else:
            @pl.when(kk == 0)
            def _():
                acc_ref[...] = part

            @pl.when((kk > 0) & (kk < nk - 1))
            def _():
                acc_ref[...] += part

            @pl.when(kk == nk - 1)
            def _():
                finish(acc_ref[...] + part)

    in_specs = [
        pl.BlockSpec((tk, tm), lambda i, j, kk: (kk, i)) if ta else pl.BlockSpec((tm, tk), lambda i, j, kk: (i, kk)),
        pl.BlockSpec((tn, tk), lambda i, j, kk: (j, kk)) if tb else pl.BlockSpec((tk, tn), lambda i, j, kk: (kk, j)),
    ]
    args = [a, b]
    if has_bias:
        in_specs.append(pl.BlockSpec((1, tn), lambda i, j, kk: (0, j)))
        args.append(bias)
    if has_res:
        in_specs.append(pl.BlockSpec((tm, tn), lambda i, j, kk: (i, j)))
        args.append(residual)
    grid = (m // tm, n // tn, nk)
    n_x = len(scattered)
    out = pl.pallas_call(
        _with_exchange(body, len(args), 1, 1, scattered, (), grid), name=name, grid=grid,
        in_specs=in_specs + [ANY] * n_x,
        out_specs=[pl.BlockSpec((tm, tn), lambda i, j, kk: (i, j))] + [ANY] * n_x,
        out_shape=[jax.ShapeDtypeStruct((m, n), F32)] + _exchange_shapes(scattered, ()),
        scratch_shapes=[pltpu.VMEM((tm, tn) if nk > 1 else (8, LANES), F32)] + (_exchange_sems(n_x) if n_x else []),
        compiler_params=_cparams(("arbitrary",) * 3 if n_x else ("parallel", "parallel", "arbitrary")),
    )(*args, *scattered)
    return out if n_x else out[0]


def _colsum(x, name):
    m, n = x.shape
    tm = _pick(m, (512, 384, BLOCK))

    def body(x_ref, o_ref):
        i = pl.program_id(0)
        s = jnp.sum(x_ref[...].astype(F32), axis=0, keepdims=True)

        @pl.when(i == 0)
        def _():
            o_ref[...] = s

        @pl.when(i > 0)
        def _():
            o_ref[...] += s

    return pl.pallas_call(
        body, name=name, grid=(m // tm,),
        in_specs=[pl.BlockSpec((tm, n), lambda i: (i, 0))],
        out_specs=pl.BlockSpec((1, n), lambda i: (0, 0)),
        out_shape=jax.ShapeDtypeStruct((1, n), F32),
        compiler_params=_cparams(("arbitrary",)),
    )(x)


def _view(arr, width=None, col=0, off=0, rows=BLOCK):
    return (arr, arr.shape[1] if width is None else width, col, off, rows)


def _row_spec(view):
    _, width, col, off, rows = view
    if off < 0:
        return pl.BlockSpec((rows, width), lambda i, col=col, off=off: (jnp.maximum(i + off, 0), col))
    return pl.BlockSpec((rows, width), lambda i, col=col, off=off: (i + off, col))


def _const_spec(arr):
    return pl.BlockSpec(arr.shape, lambda i: (0,) * arr.ndim)


def _retile(views, tile):
    assert all(v[3] == 0 and v[4] == BLOCK for v in views)
    return [v[:4] + (tile,) for v in views]


def _rows_fwd(fn, rows, consts, out_widths, *, nblk, name, out_dtype=F32, tile=BLOCK):
    nr, nc = len(rows), len(consts)
    tile = tile if (nblk * BLOCK) % tile == 0 else BLOCK
    if tile != BLOCK:
        rows, nblk = _retile(rows, tile), nblk * BLOCK // tile
    out_blocks = [(tile, w) if isinstance(w, int) else w for w in out_widths]

    def body(*refs):
        i = pl.program_id(0)
        vals = [r[...] for r in refs[:nr + nc]]
        outs = fn(i, *vals)
        for o_ref, o in zip(refs[nr + nc:], outs):
            o_ref[...] = o.astype(o_ref.dtype)

    return pl.pallas_call(
        body, name=name, grid=(nblk,),
        in_specs=[_row_spec(v) for v in rows] + [_const_spec(c) for c in consts],
        out_specs=[pl.BlockSpec(b, lambda i: (i, 0)) for b in out_blocks],
        out_shape=[jax.ShapeDtypeStruct((nblk * r, w), out_dtype) for r, w in out_blocks],
        compiler_params=_cparams(("parallel",)),
    )(*[v[0] for v in rows], *consts)


def _rows_bwd(fn, rows, consts, cts, *, nblk, name, diff_rows, diff_consts, acc=None, fwd_widths=(), row_dtype=F32,
              ct_map=None, tile=BLOCK):
    nr, nc = len(rows), len(consts)
    acc = acc or [None] * len(diff_rows)
    tile = tile if (nblk * BLOCK) % tile == 0 else BLOCK
    if tile != BLOCK:
        rows, nblk = _retile(rows, tile), nblk * BLOCK // tile
        cts = [c if c is None else _retile([c], tile)[0] for c in cts]
        acc = [a if a is None else _retile([a], tile)[0] for a in acc]
    ct_views = [c for c in cts if c is not None]
    acc_views = [a for a in acc if a is not None]
    n_in = nr + nc + len(ct_views) + len(acc_views)
    n_fwd = len(fwd_widths)

    def body(*refs):
        i = pl.program_id(0)
        row_vals = [r[...] for r in refs[:nr]]
        const_vals = [r[...] for r in refs[nr:nr + nc]]
        ct_vals = [r[...] for r in refs[nr + nc:nr + nc + len(ct_views)]]
        acc_vals = [r[...] for r in refs[nr + nc + len(ct_views):n_in]]
        out_refs = refs[n_in:]

        def f(*dargs):
            rv = list(row_vals)
            cv = list(const_vals)
            for pos, idx in enumerate(diff_rows):
                rv[idx] = dargs[pos]
            for pos, idx in enumerate(diff_consts):
                cv[idx] = dargs[len(diff_rows) + pos]
            return tuple(fn(i, *rv, *cv))

        primals = [row_vals[idx] for idx in diff_rows] + [const_vals[idx] for idx in diff_consts]
        outs, pull = jax.vjp(f, *primals)
        full_ct, ci = [], 0
        if ct_map is not None:
            full_ct = ct_map(i, *ct_vals)
        else:
            for o, c in zip(outs, cts):
                if c is None:
                    full_ct.append(jnp.zeros_like(o))
                else:
                    full_ct.append(ct_vals[ci])
                    ci += 1
        grads = pull(tuple(full_ct))
        for o_ref, o in zip(out_refs[:n_fwd], outs):
            o_ref[...] = o
        ai = 0
        for pos in range(len(diff_rows)):
            g = grads[pos]
            if acc[pos] is not None:
                g = g + acc_vals[ai]
                ai += 1
            out_refs[n_fwd + pos][...] = g.astype(row_dtype)
        for pos in range(len(diff_consts)):
            g = grads[len(diff_rows) + pos]
            o_ref = out_refs[n_fwd + len(diff_rows) + pos]

            @pl.when(i == 0)
            def _(o_ref=o_ref, g=g):
                o_ref[...] = g

            @pl.when(i > 0)
            def _(o_ref=o_ref, g=g):
                o_ref[...] += g

    out_specs = [pl.BlockSpec((tile, w), lambda i: (i, 0)) for w in fwd_widths]
    out_shape = [jax.ShapeDtypeStruct((nblk * tile, w), F32) for w in fwd_widths]
    for idx in diff_rows:
        out_specs.append(pl.BlockSpec((tile, rows[idx][1]), lambda i: (i, 0)))
        out_shape.append(jax.ShapeDtypeStruct((nblk * tile, rows[idx][1]), row_dtype))
    for idx in diff_consts:
        out_specs.append(_const_spec(consts[idx]))
        out_shape.append(jax.ShapeDtypeStruct(consts[idx].shape, F32))
    return pl.pallas_call(
        body, name=name, grid=(nblk,),
        in_specs=([_row_spec(v) for v in rows] + [_const_spec(c) for c in consts]
                  + [_row_spec(v) for v in ct_views] + [_row_spec(v) for v in acc_views]),
        out_specs=out_specs, out_shape=out_shape,
        compiler_params=_cparams(("arbitrary",)),
    )(*[v[0] for v in rows], *consts, *[v[0] for v in ct_views], *[v[0] for v in acc_views])


def _rms_fn(i, x, g):
    return (x * lax.rsqrt(jnp.mean(x * x, axis=-1, keepdims=True) + RMS_EPS) * g,)


def _sigmoid(x):
    return 1.0 / (1.0 + jnp.exp(-x))


def _softplus(x):
    return jnp.maximum(x, 0.0) + jnp.log(1.0 + jnp.exp(-jnp.abs(x)))


def _split2(x):
    hi = x.astype(BF16)
    lo = (x - hi.astype(F32)).astype(BF16)
    return jnp.concatenate([hi, lo], axis=1)


@jax.custom_vjp
def _head_sum(x):
    r = lax.broadcasted_iota(jnp.int32, (2 * RW_DIM, RW_DIM), 0) % RW_DIM // HEAD_DIM
    c = lax.broadcasted_iota(jnp.int32, (2 * RW_DIM, RW_DIM), 1) // HEAD_DIM
    return jnp.dot(_split2(x), (r == c).astype(BF16), preferred_element_type=F32)


_head_sum.defvjp(lambda x: (_head_sum(x), None), lambda _, ct: (_head_sum(ct),))


@jax.custom_vjp
def _dot_bf16(x, w):
    return jnp.dot(x.astype(BF16), w.astype(BF16), preferred_element_type=F32)


def _dot_bf16_bwd(res, ct):
    x, w = res
    ct = ct.astype(BF16)
    dx = lax.dot_general(ct, w.astype(BF16), (((1,), (1,)), ((), ())), preferred_element_type=F32)
    dw = lax.dot_general(x.astype(BF16), ct, (((0,), (0,)), ((), ())), preferred_element_type=F32)
    return dx, dw


_dot_bf16.defvjp(lambda x, w: (_dot_bf16(x, w), (x, w)), _dot_bf16_bwd)


def _rwkv_pre_fn(i, r, k, v, dw, da, dg, r_p, k_p, v_p, dw_p, da_p, dg_p,
                 mix_r, mix_k, mix_v, mix_dw, mix_da, mix_dg, w0, w2, a0, a2, g2, k_k, k_a):
    row = i * BLOCK + lax.broadcasted_iota(jnp.int32, (BLOCK, 1), 0)
    live = row >= PAD_ROWS
    live_prev = row >= PAD_ROWS + 1

    def shift(cur, prev, mix):
        cur = jnp.where(live, cur, 0.0)
        prev = jnp.where(live_prev, prev, 0.0)
        return cur + (prev - cur) * mix

    r = shift(r, r_p, mix_r)
    k = shift(k, k_p, mix_k)
    v = shift(v, v_p, mix_v)
    dw = shift(dw, dw_p, mix_dw)
    da = shift(da, da_p, mix_da)
    dg = shift(dg, dg_p, mix_dg)
    wlog = -_softplus(-(w0 + _dot_bf16(jnp.tanh(dw), w2))) - 0.5
    decay = jnp.exp(-jnp.exp(wlog))
    a = _sigmoid(a0 + _dot_bf16(da, a2))
    g = _dot_bf16(_sigmoid(dg), g2)
    kk = k * k_k
    norm_sq = jnp.where(live, _head_sum(kk * kk), 1.0)
    kk = kk / jnp.maximum(jnp.sqrt(norm_sq), 1e-12)
    k_mod = k * (1.0 + (a - 1.0) * k_a)
    return r, decay, k_mod, v, -kk, kk * a, g


def _rwkv_pre_xt_fn(i, *args):
    r, decay, k_mod, v, a_neg, b, g = _rwkv_pre_fn(i, *args)
    t = SCAN_T
    xt = jnp.concatenate([_rows_to_xt(x[c * t:(c + 1) * t]) for c in range(BLOCK // t)
                          for x in (a_neg, decay, b, k_mod, r)], axis=0)
    return r, k_mod, v, g, xt


def _rwkv_pre_cts(i, dxt, dv_s, dr_p, dk_p, dv_p, dg_p):
    t = SCAN_T
    d_a, d_w, d_b, d_k, d_r = [
        jnp.concatenate([_xt_to_rows(dxt[c * VEC_ROWS + n * HEAD_DIM:c * VEC_ROWS + (n + 1) * HEAD_DIM])
                         for c in range(BLOCK // t)], axis=0) for n in range(N_VEC)]
    dr_p, dk_p, dv_p, dg_p = [jnp.where(i > 0, x, 0.0) for x in (dr_p, dk_p, dv_p, dg_p)]
    return d_r + dr_p, d_w, d_k + dk_p, dv_s + dv_p, d_a, d_b, dg_p


def _rwkv_post_fn(i, ys, r, k_mod, v, g, ln_w, ln_b, r_k):
    mean = _head_sum(ys) * (1.0 / HEAD_DIM)
    d = ys - mean
    var = _head_sum(d * d) * (1.0 / HEAD_DIM)
    yn = d * lax.rsqrt(var + RWKV_LN_EPS) * ln_w + ln_b
    bonus = _head_sum(r * k_mod * r_k) * v
    return ((yn + bonus) * g,)


def _merge_fn(i, ya, yr, g1, g2):
    return (_sigmoid(g1) * ya + _sigmoid(g2) * yr,)


def _swiglu_fn(i, gate, up):
    return (gate * _sigmoid(gate) * up,)


def _loss_fn(i, h, tgt, g):
    y = h * lax.rsqrt(jnp.mean(h * h, axis=-1, keepdims=True) + RMS_EPS) * g
    err = y - tgt
    return (0.5 * jnp.mean(err * err, axis=-1, keepdims=True),)


def _rope_tables(lp):
    pos = (jnp.arange(lp, dtype=jnp.int32) - PAD_ROWS).astype(F32)
    inv_freq = jnp.power(jnp.float32(ROPE_THETA), -jnp.arange(ROPE_HALF, dtype=F32) * (2.0 / ROPE_DIM))
    ang = pos[:, None] * inv_freq[None, :]
    cos, sin = jnp.cos(ang), jnp.sin(ang)
    one = jnp.ones((lp, HEAD_DIM - ROPE_DIM), F32)
    zero_h = jnp.zeros((lp, ROPE_HALF), F32)
    zero_r = jnp.zeros((lp, HEAD_DIM - ROPE_DIM), F32)
    c = jnp.concatenate([cos, cos, one], axis=1)
    s1 = jnp.concatenate([-sin, zero_h, zero_r], axis=1)
    s2 = jnp.concatenate([zero_h, sin, zero_r], axis=1)
    return tuple(jnp.tile(t, (1, LANES // HEAD_DIM)) for t in (c, s1, s2))


def _rope_fwd_fn(i, x, c, s1, s2):
    n = x.shape[1]
    c, s1, s2 = [jnp.tile(t, (1, n // LANES)) for t in (c, s1, s2)]
    return (x * c + pltpu.roll(x, n - ROPE_HALF, 1) * s1 + pltpu.roll(x, ROPE_HALF, 1) * s2,)


def _rope_bwd_fn(i, dy, c, s1, s2):
    n = dy.shape[1]
    c, s1, s2 = [jnp.tile(t, (1, n // LANES)) for t in (c, s1, s2)]
    return (dy * c + pltpu.roll(dy * s1, ROPE_HALF, 1) + pltpu.roll(dy * s2, n - ROPE_HALF, 1),)


def _attn_mask(i):
    r = lax.broadcasted_iota(jnp.int32, (BLOCK, 3 * BLOCK), 0)
    c = lax.broadcasted_iota(jnp.int32, (BLOCK, 3 * BLOCK), 1)
    meta = (c < BLOCK) & (c >= PAD_ROWS)
    prev = (c >= BLOCK) & (c < 2 * BLOCK) & ((c - BLOCK) > r) & (i >= 1)
    cur = (c >= 2 * BLOCK) & ((c - 2 * BLOCK) <= r)
    return meta | prev | cur


def _attn_rows(ref, g):
    return ref[:, g * HEAD_DIM:(g + 1) * HEAD_DIM]


def _attn_group(i, g, q_all, k_refs, v_refs, s_ref):
    heads = range(g * GROUP, (g + 1) * GROUP)
    kcat = jnp.concatenate([_attn_rows(r, g) for r in k_refs], axis=0).astype(BF16)
    vcat = jnp.concatenate([_attn_rows(r, g) for r in v_refs], axis=0).astype(BF16)
    qg = jnp.concatenate([q_all[:, h * HEAD_DIM:(h + 1) * HEAD_DIM] for h in heads], axis=0).astype(BF16)
    sink = jnp.concatenate([jnp.broadcast_to(s_ref[0:1, h:h + 1], (BLOCK, 1)) for h in heads], axis=0)
    s = lax.dot_general(qg, kcat, (((1,), (1,)), ((), ())), preferred_element_type=F32) * (HEAD_DIM ** -0.5)
    valid = jnp.concatenate([_attn_mask(i)] * GROUP, axis=0)
    return heads, qg, kcat, vcat, sink, jnp.where(valid, s, NEG_INF)


ATTN_SUB = 2


def _attn_specs(v_col):
    q = [pl.BlockSpec((BLOCK, Q_HEADS * HEAD_DIM), lambda i, n=n: (ATTN_SUB * i + 1 + n, 0)) for n in range(ATTN_SUB)]
    blk = lambda col: ([pl.BlockSpec((BLOCK, 2 * HEAD_DIM), lambda i: (0, col))]
                       + [pl.BlockSpec((BLOCK, 2 * HEAD_DIM), lambda i, n=n: (ATTN_SUB * i + n, col))
                          for n in range(ATTN_SUB + 1)])
    return q + blk(0) + blk(v_col) + [pl.BlockSpec((1, Q_HEADS), lambda i: (0, 0))]


def _attn_split(refs):
    n = ATTN_SUB
    q_refs, k_refs, v_refs = refs[:n], refs[n:2 * n + 2], refs[2 * n + 2:3 * n + 4]
    return q_refs, k_refs, v_refs, refs[3 * n + 4], refs[3 * n + 5:]


def _attn_fwd_block(blk, q_ref, keys, vals, s_ref, o_ref, lse_ref):
    q_all = q_ref[...]
    for g in range(KV_HEADS):
        heads, _, _, vcat, sink, s = _attn_group(blk, g, q_all, keys, vals, s_ref)
        m = jnp.maximum(jnp.max(s, axis=-1, keepdims=True), sink)
        p = jnp.exp(s - m)
        den = jnp.sum(p, axis=-1, keepdims=True) + jnp.exp(sink - m)
        o = jnp.dot(p.astype(BF16), vcat, preferred_element_type=F32) / den
        lse = m + jnp.log(den)
        for n, h in enumerate(heads):
            o_ref[:, h * HEAD_DIM:(h + 1) * HEAD_DIM] = o[n * BLOCK:(n + 1) * BLOCK]
            lse_ref[:, h:h + 1] = lse[n * BLOCK:(n + 1) * BLOCK]


def _attn_bwd(q, k, v, v_col, sinks, o, lse, do, *, nblk, name):
    lp = k.shape[0]
    rows = ATTN_SUB * BLOCK

    def body(*refs):
        q_refs, k_refs, v_refs, s_ref, (o_ref, lse_ref, do_ref, dq_ref, dk_ref, dv_ref, ds_ref) = _attn_split(refs)
        i = pl.program_id(0)

        @pl.when(i == 0)
        def _():
            dk_ref[...] = jnp.zeros_like(dk_ref)
            dv_ref[...] = jnp.zeros_like(dv_ref)
            ds_ref[...] = jnp.zeros_like(ds_ref)

        lane = lax.broadcasted_iota(jnp.int32, (1, Q_HEADS), 1)
        for sub in range(ATTN_SUB):
            at = slice(sub * BLOCK, (sub + 1) * BLOCK)
            blk = ATTN_SUB * i + sub
            keys = (k_refs[0], k_refs[1 + sub], k_refs[2 + sub])
            vals = (v_refs[0], v_refs[1 + sub], v_refs[2 + sub])
            prev_rows = pl.ds(pl.multiple_of(blk * BLOCK, BLOCK), BLOCK)
            cur_rows = pl.ds(pl.multiple_of((blk + 1) * BLOCK, BLOCK), BLOCK)
            q_all, o_all, do_all, lse_all = q_refs[sub][...], o_ref[at, :], do_ref[at, :], lse_ref[at, :]
            for g in range(KV_HEADS):
                heads, qg, kcat, vcat, sink, s = _attn_group(blk, g, q_all, keys, vals, s_ref)
                stack = lambda x: jnp.concatenate([x[:, h * HEAD_DIM:(h + 1) * HEAD_DIM] for h in heads], axis=0)
                lse_g = jnp.concatenate([lse_all[:, h:h + 1] for h in heads], axis=0)
                do_g = stack(do_all)
                p = jnp.exp(s - lse_g)
                delta = jnp.sum(do_g * stack(o_all), axis=-1, keepdims=True)
                dp = lax.dot_general(do_g.astype(BF16), vcat, (((1,), (1,)), ((), ())), preferred_element_type=F32)
                dsc = (p * (dp - delta) * (HEAD_DIM ** -0.5)).astype(BF16)
                dq = jnp.dot(dsc, kcat, preferred_element_type=F32)
                dk_all = lax.dot_general(dsc, qg, (((0,), (0,)), ((), ())), preferred_element_type=F32)
                dv_all = lax.dot_general(p.astype(BF16), do_g.astype(BF16), (((0,), (0,)), ((), ())),
                                         preferred_element_type=F32)
                cols = slice(g * HEAD_DIM, (g + 1) * HEAD_DIM)
                for ref, full in ((dk_ref, dk_all), (dv_ref, dv_all)):
                    ref[0:BLOCK, cols] += full[0:BLOCK]
                    ref[prev_rows, cols] += full[BLOCK:2 * BLOCK]
                    ref[cur_rows, cols] += full[2 * BLOCK:]
                sink_part = jnp.exp(sink - lse_g) * delta
                for n, h in enumerate(heads):
                    dq_ref[at, h * HEAD_DIM:(h + 1) * HEAD_DIM] = dq[n * BLOCK:(n + 1) * BLOCK]
                    dsink = -jnp.sum(sink_part[n * BLOCK:(n + 1) * BLOCK], axis=0, keepdims=True)
                    ds_ref[...] += jnp.where(lane == h, dsink, 0.0)

    qspec = pl.BlockSpec((rows, Q_HEADS * HEAD_DIM), lambda i: (i, 0))
    whole = pl.BlockSpec((lp, 2 * HEAD_DIM), lambda i: (0, 0))
    return pl.pallas_call(
        body, name=name, grid=(nblk // ATTN_SUB,),
        in_specs=_attn_specs(v_col) + [qspec, pl.BlockSpec((rows, Q_HEADS), lambda i: (i, 0)), qspec],
        out_specs=[qspec, whole, whole, pl.BlockSpec((1, Q_HEADS), lambda i: (0, 0))],
        out_shape=[jax.ShapeDtypeStruct((nblk * BLOCK, Q_HEADS * HEAD_DIM), F32),
                   jax.ShapeDtypeStruct((lp, 2 * HEAD_DIM), F32), jax.ShapeDtypeStruct((lp, 2 * HEAD_DIM), F32),
                   jax.ShapeDtypeStruct((1, Q_HEADS), F32)],
        compiler_params=_cparams(("arbitrary",)),
    )(*[q] * ATTN_SUB, *[k] * (ATTN_SUB + 2), *[v] * (ATTN_SUB + 2), sinks, o, lse, do)


N_VEC = 5
VEC_ROWS = N_VEC * HEAD_DIM


def _selectors():
    t = SCAN_T
    shape = (t, 2 * LANES, RW_DIM)
    step, src, dst = [lax.broadcasted_iota(jnp.int32, shape, d) for d in range(3)]
    src = src % LANES
    spread = ((src // t == dst // HEAD_DIM) & (src % t == step)).astype(BF16)
    shape = (t, RW_DIM, LANES)
    step, src, dst = [lax.broadcasted_iota(jnp.int32, shape, d) for d in range(3)]
    collect = ((src // HEAD_DIM == dst // t) & (dst % t == step)).astype(BF16)
    return spread, collect


def _rows_to_xt(x):
    low = lax.broadcasted_iota(jnp.int32, (SCAN_T, LANES), 1) < HEAD_DIM
    pieces = []
    for m in range(RW_HEADS // 2):
        pair = x[:, m * LANES:(m + 1) * LANES]
        pieces += [jnp.where(low, pair, 0.0), jnp.where(low, pltpu.roll(pair, HEAD_DIM, 1), 0.0)]
    return jnp.concatenate(pieces, axis=0).T[:HEAD_DIM]


def _xt_to_rows(a):
    t = SCAN_T
    a_t = jnp.concatenate([a, jnp.zeros_like(a)], axis=0).T
    pairs = [a_t[2 * m * t:(2 * m + 1) * t] + pltpu.roll(a_t[(2 * m + 1) * t:(2 * m + 2) * t], HEAD_DIM, 1)
             for m in range(RW_HEADS // 2)]
    return jnp.concatenate(pairs, axis=1)


def _with_exchange(compute, n_in, n_out, n_scratch, scattered, shared, grid):
    n_sc = len(scattered)
    n_x = n_sc + len(shared)
    if n_x == 0:
        return compute

    def body(*refs):
        ins, x_in = refs[:n_in], refs[n_in:n_in + n_x]
        outs, x_out = refs[n_in + n_x:n_in + n_x + n_out], refs[n_in + n_x + n_out:n_in + 2 * n_x + n_out]
        scratch = refs[n_in + 2 * n_x + n_out:n_in + 2 * n_x + n_out + n_scratch]
        sems = refs[n_in + 2 * n_x + n_out + n_scratch:]

        first = last = True
        for d, size in enumerate(grid):
            first = first & (pl.program_id(d) == 0)
            last = last & (pl.program_id(d) == size - 1)

        @pl.when(first)
        def _():
            for cp in _exchange_copies(x_in, x_out, n_sc, *sems):
                cp.start()

        compute(*ins, *outs, *scratch)

        @pl.when(last)
        def _():
            for cp in _exchange_copies(x_in, x_out, n_sc, *sems):
                cp.wait()

    return body


def _wkv_fwd(xt, v, spread, attn, name, shared=()):
    t_steps = SCAN_T
    nch = xt.shape[0]
    per = SCAN_CHUNKS
    grid = (nch // per,)
    rows = per * t_steps
    assert rows == BLOCK
    n_x = len(shared)
    q, k, v_arr, v_col, sinks = attn

    def compute(xt_ref, v_ref, sel_ref, q_ref, km_ref, kp_ref, kc_ref, vm_ref, vp_ref, vc_ref, s_ref,
                y_ref, hist_ref, o_ref, lse_ref, st_ref):
        @pl.when(pl.program_id(0) == 0)
        def _():
            st_ref[...] = jnp.zeros_like(st_ref)

        _attn_fwd_block(pl.program_id(0) - 1, q_ref, (km_ref, kp_ref, kc_ref), (vm_ref, vp_ref, vc_ref), s_ref,
                        o_ref, lse_ref)
        st = st_ref[...]
        for c in range(per):
            x2 = _split2(xt_ref[c])
            for j in range(t_steps):
                row = c * t_steps + j
                cols = jnp.dot(x2, sel_ref[j], preferred_element_type=F32)
                a_c, w_c, b_c, k_c, r_c = [cols[n * HEAD_DIM:(n + 1) * HEAD_DIM] for n in range(N_VEC)]
                hist_ref[row] = st
                sa = jnp.sum(st * a_c, axis=0, keepdims=True)
                st = st * w_c + b_c * sa + k_c * v_ref[row:row + 1, :]
                y_ref[row:row + 1, :] = jnp.sum(st * r_c, axis=0, keepdims=True)
        st_ref[...] = st

    before = lambda c: jnp.maximum(c - 1, 0)
    kv = lambda col: [pl.BlockSpec((BLOCK, 2 * HEAD_DIM), lambda c: (0, col)),
                      pl.BlockSpec((BLOCK, 2 * HEAD_DIM), lambda c: (before(c), col)),
                      pl.BlockSpec((BLOCK, 2 * HEAD_DIM), lambda c: (c, col))]
    seq = (nch * t_steps) - BLOCK
    return pl.pallas_call(
        _with_exchange(compute, 11, 4, 1, (), shared, grid), name=name, grid=grid,
        in_specs=[pl.BlockSpec((per, VEC_ROWS, LANES), lambda c: (c, 0, 0)),
                  pl.BlockSpec((rows, RW_DIM), lambda c: (c, 0)),
                  pl.BlockSpec(spread.shape, lambda c: (0, 0, 0)),
                  pl.BlockSpec((BLOCK, Q_HEADS * HEAD_DIM), lambda c: (c, 0))] + kv(0) + kv(v_col)
                 + [pl.BlockSpec((1, Q_HEADS), lambda c: (0, 0))] + [ANY] * n_x,
        out_specs=[pl.BlockSpec((rows, RW_DIM), lambda c: (c, 0)),
                   pl.BlockSpec((rows, HEAD_DIM, RW_DIM), lambda c: (c, 0, 0)),
                   pl.BlockSpec((BLOCK, Q_HEADS * HEAD_DIM), lambda c: (before(c), 0)),
                   pl.BlockSpec((BLOCK, Q_HEADS), lambda c: (before(c), 0))] + [ANY] * n_x,
        out_shape=[jax.ShapeDtypeStruct((nch * t_steps, RW_DIM), F32),
                   jax.ShapeDtypeStruct((nch * t_steps, HEAD_DIM, RW_DIM), F32),
                   jax.ShapeDtypeStruct((seq, Q_HEADS * HEAD_DIM), F32),
                   jax.ShapeDtypeStruct((seq, Q_HEADS), F32)] + _exchange_shapes((), shared),
        scratch_shapes=[pltpu.VMEM((HEAD_DIM, RW_DIM), F32)] + (_exchange_sems(n_x) if n_x else []),
        compiler_params=_cparams(("arbitrary",), SCAN_VMEM_LIMIT),
    )(xt, v, spread, q, k, k, k, v_arr, v_arr, v_arr, sinks, *shared)


def _wkv_bwd(xt, v, hist, dy, spread, collect, name, scattered=()):
    t_steps = SCAN_T
    nch = xt.shape[0]
    n_x = len(scattered)
    per = SCAN_CHUNKS
    nsteps = nch // per
    grid = (nsteps,)
    rows = per * t_steps
    lead = BLOCK // rows

    def compute(xt_ref, v_ref, hist_ref, dy_ref, sel_ref, col_ref, dxt_ref, dv_ref, g_ref):
        @pl.when(pl.program_id(0) == 0)
        def _():
            g_ref[...] = jnp.zeros_like(g_ref)

        has_dy = nsteps - 1 - pl.program_id(0) >= lead
        gst = g_ref[...]
        nxt = None
        for c in reversed(range(per)):
            x2 = _split2(xt_ref[c])
            acc = jnp.zeros((VEC_ROWS, LANES), F32)
            for j in reversed(range(t_steps)):
                row = c * t_steps + j
                cols = jnp.dot(x2, sel_ref[j], preferred_element_type=F32)
                a_c, w_c, b_c, k_c, r_c = [cols[n * HEAD_DIM:(n + 1) * HEAD_DIM] for n in range(N_VEC)]
                prev = hist_ref[row]
                v_row = v_ref[row:row + 1, :]
                dy_row = jnp.where(has_dy, dy_ref[row:row + 1, :], 0.0)
                sa = jnp.sum(prev * a_c, axis=0, keepdims=True)
                if nxt is None:
                    nxt = prev * w_c + b_c * sa + k_c * v_row
                gst = gst + r_c * dy_row
                dv_ref[row:row + 1, :] = jnp.sum(gst * k_c, axis=0, keepdims=True)
                dsa = jnp.sum(gst * b_c, axis=0, keepdims=True)
                prods = jnp.concatenate([p.astype(BF16) for p in
                                         (prev * dsa, gst * prev, gst * sa, gst * v_row, nxt * dy_row)], axis=0)
                acc = acc + jnp.dot(prods, col_ref[j], preferred_element_type=F32)
                gst = gst * w_c + a_c * dsa
                nxt = prev
            dxt_ref[c] = acc
        g_ref[...] = gst

    rev3 = lambda c: (nsteps - 1 - c, 0, 0)
    rev2 = lambda c: (nsteps - 1 - c, 0)
    rowspec = pl.BlockSpec((rows, RW_DIM), rev2)
    return pl.pallas_call(
        _with_exchange(compute, 6, 2, 1, scattered, (), grid), name=name, grid=grid,
        in_specs=[pl.BlockSpec((per, VEC_ROWS, LANES), rev3), rowspec,
                  pl.BlockSpec((rows, HEAD_DIM, RW_DIM), rev3),
                  pl.BlockSpec((rows, RW_DIM), lambda c: (jnp.maximum(nsteps - 1 - c - lead, 0), 0)),
                  pl.BlockSpec(spread.shape, lambda c: (0, 0, 0)),
                  pl.BlockSpec(collect.shape, lambda c: (0, 0, 0))] + [ANY] * n_x,
        out_specs=[pl.BlockSpec((per, VEC_ROWS, LANES), rev3), rowspec] + [ANY] * n_x,
        out_shape=[jax.ShapeDtypeStruct((nch, VEC_ROWS, LANES), F32),
                   jax.ShapeDtypeStruct((nch * t_steps, RW_DIM), F32)] + _exchange_shapes(scattered, ()),
        scratch_shapes=[pltpu.VMEM((HEAD_DIM, RW_DIM), F32)] + (_exchange_sems(n_x) if n_x else []),
        compiler_params=_cparams(("arbitrary",), SCAN_VMEM_LIMIT),
    )(xt, v, hist, dy, spread, collect, *scattered)


MESH = pl.DeviceIdType.MESH
ANY = pl.BlockSpec(memory_space=pltpu.HBM)


def _all_gather(arrays, name):
    n_arr = len(arrays)
    per = N_DEV - 1

    def body(*refs):
        x_refs, out_refs = refs[:n_arr], refs[n_arr:2 * n_arr]
        send_sems, recv_sems, local_sems = refs[2 * n_arr:]
        xi, yi, ci = lax.axis_index("x"), lax.axis_index("y"), lax.axis_index("c")
        me, sibling = (xi, yi, ci), (xi, yi, 1 - ci)
        chips = [(1 - xi, yi), (xi, 1 - yi), (1 - xi, 1 - yi)]

        def slot(a, px, py, pc):
            return out_refs[a].at[4 * px + 2 * py + pc]

        def copy(a, sem, block, to, src=None):
            return pltpu.make_async_remote_copy(
                src_ref=slot(a, *block) if src is None else src, dst_ref=slot(a, *block),
                send_sem=send_sems.at[per * a + sem], recv_sem=recv_sems.at[per * a + sem],
                device_id=to, device_id_type=MESH)

        mine = [pltpu.make_async_copy(x_refs[a], slot(a, *me), local_sems.at[a]) for a in range(n_arr)]
        for cp in mine:
            cp.start()
        sent = []
        for a in range(n_arr):
            sent.append(copy(a, 0, me, sibling, src=x_refs[a]))
            sent += [copy(a, 1 + j, me, (*chip, ci), src=x_refs[a]) for j, chip in enumerate(chips)]
        for cp in sent:
            cp.start()
        for j, chip in enumerate(chips):
            for a in range(n_arr):
                copy(a, 1 + j, (*chip, ci), me).wait_recv()
                onward = copy(a, 4 + j, (*chip, ci), sibling)
                onward.start()
                sent.append(onward)
        for a in range(n_arr):
            copy(a, 0, sibling, me).wait_recv()
        for j, chip in enumerate(chips):
            for a in range(n_arr):
                copy(a, 4 + j, (*chip, 1 - ci), me).wait_recv()
        for cp in sent:
            cp.wait_send()
        for cp in mine:
            cp.wait()

    sems = pltpu.SemaphoreType.DMA((per * n_arr,))
    return pl.pallas_call(
        body, name=name, out_shape=[jax.ShapeDtypeStruct((N_DEV,) + a.shape, a.dtype) for a in arrays],
        in_specs=[ANY] * n_arr, out_specs=[ANY] * n_arr,
        scratch_shapes=[sems, sems, pltpu.SemaphoreType.DMA((n_arr,))],
    )(*arrays)


def _exchange_copies(in_refs, out_refs, n_scattered, send_sems, recv_sems, local_sems):
    n_arr = len(in_refs)
    per = N_DEV - 1
    xi, yi, ci = lax.axis_index("x"), lax.axis_index("y"), lax.axis_index("c")
    me = 4 * xi + 2 * yi + ci
    src_of = lambda a, peer: in_refs[a].at[peer] if a < n_scattered else in_refs[a]
    copies = []
    for d in range(1, N_DEV):
        px = 1 - xi if d & 4 else xi
        py = 1 - yi if d & 2 else yi
        pc = 1 - ci if d & 1 else ci
        for a in range(n_arr):
            copies.append(pltpu.make_async_remote_copy(
                src_ref=src_of(a, 4 * px + 2 * py + pc), dst_ref=out_refs[a].at[me],
                send_sem=send_sems.at[per * a + d - 1], recv_sem=recv_sems.at[per * a + d - 1],
                device_id=(px, py, pc), device_id_type=MESH))
    own = [pltpu.make_async_copy(src_of(a, me), out_refs[a].at[me], local_sems.at[a]) for a in range(n_arr)]
    return copies + own


def _exchange_shapes(scattered, shared):
    return ([jax.ShapeDtypeStruct(a.shape, a.dtype) for a in scattered]
            + [jax.ShapeDtypeStruct((N_DEV,) + a.shape, a.dtype) for a in shared])


def _exchange_sems(n_arr):
    sems = pltpu.SemaphoreType.DMA(((N_DEV - 1) * n_arr,))
    return [sems, sems, pltpu.SemaphoreType.DMA((n_arr,))]


def _exchange(scattered, shared, name):
    n_sc = len(scattered)
    n_arr = n_sc + len(shared)

    def body(*refs):
        copies = _exchange_copies(refs[:n_arr], refs[n_arr:2 * n_arr], n_sc, *refs[2 * n_arr:])
        for cp in copies:
            cp.start()
        for cp in copies:
            cp.wait()

    return pl.pallas_call(
        body, name=name, out_shape=_exchange_shapes(scattered, shared),
        in_specs=[ANY] * n_arr, out_specs=[ANY] * n_arr, scratch_shapes=_exchange_sems(n_arr),
    )(*scattered, *shared)


def _adam_math(g, w, m, v):
    m_new = ADAM_B1 * m + (1.0 - ADAM_B1) * g
    v_new = ADAM_B2 * v + (1.0 - ADAM_B2) * (g * g)
    m_hat = m_new / (1.0 - ADAM_B1 ** ADAM_STEP)
    v_hat = v_new / (1.0 - ADAM_B2 ** ADAM_STEP)
    return -ADAM_LR * (m_hat / (jnp.sqrt(v_hat) + ADAM_EPS) + ADAM_WD * w), m_new, v_new


def _slot_sum(p_ref):
    g = p_ref[0].astype(F32)
    for s in range(1, N_DEV):
        g = g + p_ref[s].astype(F32)
    return g


def _adamw_replicated(parts, ws, ms, vs, loss_parts, name):
    n = len(ws)

    def body(*refs):
        p_refs, w_refs, m_refs, v_refs = refs[:n], refs[n:2 * n], refs[2 * n:3 * n], refs[3 * n:4 * n]
        outs = refs[4 * n + 1:]
        for j in range(n):
            g = _slot_sum(p_refs[j])
            outs[4 * j][...] = g
            for o_ref, val in zip(outs[4 * j + 1:4 * j + 4], _adam_math(g, w_refs[j][...], m_refs[j][...],
                                                                        v_refs[j][...])):
                o_ref[...] = val
        outs[4 * n][...] = _slot_sum(refs[4 * n])

    whole = pl.BlockSpec(memory_space=pltpu.VMEM)
    out = pl.pallas_call(
        body, name=name, in_specs=[whole] * (4 * n + 1), out_specs=[whole] * (4 * n + 1),
        out_shape=[jax.ShapeDtypeStruct(w.shape, F32) for w in ws for _ in range(4)]
                  + [jax.ShapeDtypeStruct(loss_parts.shape[1:], F32)],
    )(*parts, *ws, *ms, *vs, loss_parts)
    return [out[4 * j:4 * j + 4] for j in range(n)], out[4 * n]


def _adamw(parts, w, m, v, name):
    rows, cols = w.shape[-2:]
    tile = PACK_ROWS if rows % PACK_ROWS == 0 else rows
    at = (0,) if w.ndim == 3 else (Ellipsis,)

    def body(p_ref, w_ref, m_ref, v_ref, g_out, d_out, m_out, v_out):
        g = _slot_sum(p_ref)
        g_out[at] = g
        d_out[at], m_out[at], v_out[at] = _adam_math(g, w_ref[at], m_ref[at], v_ref[at])

    spec = (pl.BlockSpec((1, tile, cols), lambda i: (0, i, 0)) if w.ndim == 3
            else pl.BlockSpec((tile, cols), lambda i: (i, 0)))
    return pl.pallas_call(
        body, name=name, grid=(rows // tile,),
        in_specs=[pl.BlockSpec((N_DEV, tile, cols), lambda i: (0, i, 0)), spec, spec, spec],
        out_specs=[spec] * 4, out_shape=[jax.ShapeDtypeStruct(w.shape, F32)] * 4,
        compiler_params=_cparams(("parallel",)),
    )(parts, w, m, v)


EARLY = [("meta_tokens", 1), ("rwkv_w2", 1), ("rwkv_a2", 1), ("rwkv_g2", 1)]
LATE = [("w_br_attn", 1), ("w_br_rwkv", 1), ("w_o", 0), ("w_ffn_gate", 1), ("w_ffn_up", 1), ("w_ffn_down", 0)]
REPLICATED = ["norm_mix_g", "b_in", "attn_sinks", "rwkv_mix", "rwkv_w0", "rwkv_a0", "rwkv_k_k", "rwkv_k_a",
              "rwkv_r_k", "rwkv_ln_w", "rwkv_ln_b", "norm_ffn_g", "norm_final_g"]
WEIGHTS = ["meta_tokens", "norm_mix_g", "w_in", "b_in", "attn_sinks", "rwkv_mix", "rwkv_w0", "rwkv_w2", "rwkv_a0",
           "rwkv_a2", "rwkv_g2", "rwkv_k_k", "rwkv_k_a", "rwkv_r_k", "rwkv_ln_w", "rwkv_ln_b", "w_br_attn",
           "w_br_rwkv", "w_o", "norm_ffn_g", "w_ffn_gate", "w_ffn_up", "w_ffn_down", "norm_final_g"]


def _strip(name, a):
    return a if name in ("meta_tokens", "norm_final_g") else a[0]


def _join(gathered, axis):
    if axis == 0:
        return gathered.reshape(-1, gathered.shape[2])
    return gathered.transpose(1, 0, 2).reshape(gathered.shape[1], -1)


def _split(g, axis):
    if axis == 0:
        return g.reshape(N_DEV, -1, g.shape[1])
    return g.reshape(g.shape[0], N_DEV, -1).transpose(1, 0, 2)


W_IN_LAYOUT = [(768, 2304), (0, 512), (2592, 4640), (2432, 2592), 256 - GATE_LORA, (512, 768), (2304, 2368),
               128 - DECAY_LORA, (2368, 2432), 128 - AAA_LORA, NP - C_DA - 128]


def _w_in_padded(w, shard_width=None):
    rows = w.shape[-2]
    width = D_IN if shard_width is None else shard_width
    parts = []
    for seg in W_IN_LAYOUT:
        if isinstance(seg, int):
            parts.append(jnp.zeros((rows, seg), w.dtype))
            continue
        lo, stop = seg
        while lo < stop:
            p = lo // width
            hi = min(stop, (p + 1) * width)
            src = w if shard_width is None else w[p]
            parts.append(src[:, lo - p * width:hi - p * width])
            lo = hi
    return jnp.concatenate(parts, axis=1)


def _w_in_unpadded(wp, lo=0, stop=D_IN):
    spans, pos = [], 0
    for seg in W_IN_LAYOUT:
        if isinstance(seg, int):
            pos += seg
        else:
            spans.append((seg[0], seg[1], pos))
            pos += seg[1] - seg[0]
    parts = []
    for a, b, at in sorted(spans):
        c, d = max(a, lo), min(b, stop)
        if c < d:
            parts.append(wp[:, at + c - a:at + d - a])
    return jnp.concatenate(parts, axis=1)


def _pad_rows(a, n):
    return jnp.pad(a, ((0, n - a.shape[0]), (0, 0)))


def _device_step(x, tgt, full, gather_late=None, scatter_early=None, scatter_last=None):
    seq = x.shape[0]
    nblk = seq // BLOCK
    lp = seq + BLOCK
    nall = nblk + 1

    w_in_p = full["w_in_p"]
    b_in_p = _w_in_padded(full["b_in"][None])
    mix = full["rwkv_mix"][None]
    mix_r, mix_k, mix_v = mix[:, 0:512], mix[:, 512:1024], mix[:, 1024:1536]
    mix_dw = jnp.pad(mix[:, 1536:1600], ((0, 0), (0, 64)))
    mix_da = jnp.pad(mix[:, 1600:1664], ((0, 0), (0, 64)))
    mix_dg = jnp.pad(mix[:, 1664:1824], ((0, 0), (0, 96)))
    w2_p = _pad_rows(full["rwkv_w2"].astype(F32), 128)
    a2_p = _pad_rows(full["rwkv_a2"].astype(F32), 128)
    g2_p = _pad_rows(full["rwkv_g2"].astype(F32), 256)
    row = lambda name: full[name].reshape(1, -1)
    sinks = row("attn_sinks")
    rope_c, rope_s1, rope_s2 = _rope_tables(lp)

    hpad = jnp.concatenate([jnp.zeros((PAD_ROWS, D_MODEL), F32), full["meta_tokens"].astype(F32), x], axis=0)
    (u,) = _rows_fwd(_rms_fn, [_view(hpad)], [row("norm_mix_g")], [D_MODEL], nblk=nall, name="norm_mix",
                     out_dtype=BF16, tile=TILE_ALL)
    proj = _mm(u, w_in_p, bias=b_in_p, name="in_proj")
    (q_r,) = _rows_fwd(_rope_fwd_fn, [_view(proj, 512, C_Q // 512), _view(rope_c), _view(rope_s1), _view(rope_s2)],
                       [], [512], nblk=nall, name="rope_q", tile=TILE_ALL)
    (k_r,) = _rows_fwd(_rope_fwd_fn, [_view(proj, 128, C_KA // 128), _view(rope_c), _view(rope_s1),
                                      _view(rope_s2)], [], [128], nblk=nall, name="rope_k", tile=TILE_ALL)

    rw_cols = jnp.concatenate([proj[:, C_R:C_R + 1536], proj[:, C_DG:C_DG + 256], proj[:, C_DW:C_DW + 256]], axis=1)
    rw_prev = jnp.pad(rw_cols[:-1], ((1, 0), (0, 0)))
    pre_rows = [_view(proj, 512, 0), _view(proj, 512, 1), _view(proj, 512, 2), _view(proj, 128, C_DW // 128),
                _view(proj, 128, C_DA // 128), _view(proj, 256, C_DG // 256),
                _view(rw_prev, 512, 0), _view(rw_prev, 512, 1), _view(rw_prev, 512, 2), _view(rw_prev, 128, 14),
                _view(rw_prev, 128, 15), _view(rw_prev, 256, 6)]
    pre_consts = [mix_r, mix_k, mix_v, mix_dw, mix_da, mix_dg, row("rwkv_w0"), w2_p, row("rwkv_a0"), a2_p, g2_p,
                  row("rwkv_k_k"), row("rwkv_k_a")]
    xt_block = (BLOCK // SCAN_T * VEC_ROWS, LANES)
    r_t, k_mod, v_t, gate, xt = _rows_fwd(_rwkv_pre_xt_fn, pre_rows, pre_consts, [RW_DIM] * 4 + [xt_block],
                                          nblk=nall, name="rwkv_pre")
    xt = xt.reshape(-1, VEC_ROWS, LANES)
    spread, collect = _selectors()
    y_scan, hist, o_attn, lse, *late = _wkv_fwd(xt, v_t, spread, (q_r, k_r, proj, C_VA // 128, sinks), name="wkv_fwd",
                                                shared=gather_late[0] if gather_late else ())
    if gather_late:
        full = {**full, **gather_late[1](late)}
    post_rows = [_view(y_scan, off=1), _view(r_t, off=1), _view(k_mod, off=1), _view(v_t, off=1), _view(gate, off=1)]
    post_consts = [row("rwkv_ln_w"), row("rwkv_ln_b"), row("rwkv_r_k")]
    (y_rwkv,) = _rows_fwd(_rwkv_post_fn, post_rows, post_consts, [RW_DIM], nblk=nblk, name="rwkv_post",
                          out_dtype=BF16)

    ya = _mm(o_attn, full["w_br_attn"], name="br_attn")
    yr = _mm(y_rwkv, full["w_br_rwkv"], name="br_rwkv")
    merge_rows = [_view(ya), _view(yr), _view(proj, 1024, C_G1 // 1024, 1), _view(proj, 1024, C_G2 // 1024, 1)]
    (merged,) = _rows_fwd(_merge_fn, merge_rows, [], [D_MODEL], nblk=nblk, name="merge", out_dtype=BF16)
    h1 = _mm(merged, full["w_o"], residual=x, name="out_proj")
    (f,) = _rows_fwd(_rms_fn, [_view(h1)], [row("norm_ffn_g")], [D_MODEL], nblk=nblk, name="norm_ffn",
                     out_dtype=BF16, tile=TILE_REAL)
    ff_gate = _mm(f, full["w_ffn_gate"], name="ffn_gate")
    ff_up = _mm(f, full["w_ffn_up"], name="ffn_up")
    (act,) = _rows_fwd(_swiglu_fn, [_view(ff_gate), _view(ff_up)], [], [D_FF], nblk=nblk, name="swiglu",
                       out_dtype=BF16, tile=TILE_WIDE)
    h2 = _mm(act, full["w_ffn_down"], residual=h1, name="ffn_down")

    grads = {}
    ones_col = jnp.ones((seq, 1), F32)
    loss_rows, dh2, grads["norm_final_g"] = _rows_bwd(
        _loss_fn, [_view(h2), _view(tgt)], [row("norm_final_g")], [_view(ones_col)], nblk=nblk, name="loss",
        diff_rows=[0], diff_consts=[0], fwd_widths=[1], tile=TILE_REAL)
    loss = jnp.sum(loss_rows)

    dact = _mm(dh2, full["w_ffn_down"], tb=True, name="d_act")
    grads["w_ffn_down"] = _mm(act, dh2, ta=True, name="dw_ffn_down")
    dgate, dup = _rows_bwd(_swiglu_fn, [_view(ff_gate), _view(ff_up)], [], [_view(dact)], nblk=nblk,
                           name="swiglu_bwd", diff_rows=[0, 1], diff_consts=[], row_dtype=BF16, tile=TILE_WIDE)
    grads["w_ffn_gate"] = _mm(f, dgate, ta=True, name="dw_ffn_gate")
    grads["w_ffn_up"] = _mm(f, dup, ta=True, name="dw_ffn_up")
    df = _mm(dgate, full["w_ffn_gate"], tb=True, name="df_gate")
    df = _mm(dup, full["w_ffn_up"], tb=True, residual=df, name="df_up")
    dh1, grads["norm_ffn_g"] = _rows_bwd(_rms_fn, [_view(h1)], [row("norm_ffn_g")], [_view(df)], nblk=nblk,
                                         name="norm_ffn_bwd", diff_rows=[0], diff_consts=[0], acc=[_view(dh2)],
                                         tile=TILE_REAL)
    dmerged = _mm(dh1, full["w_o"], tb=True, name="d_merged")
    grads["w_o"] = _mm(merged, dh1, ta=True, name="dw_o")
    dya, dyr, dg1, dg2 = _rows_bwd(_merge_fn, merge_rows, [], [_view(dmerged)], nblk=nblk, name="merge_bwd",
                                   diff_rows=[0, 1, 2, 3], diff_consts=[], row_dtype=BF16)
    grads["w_br_attn"] = _mm(o_attn, dya, ta=True, name="dw_br_attn")
    grads["w_br_rwkv"] = _mm(y_rwkv, dyr, ta=True, name="dw_br_rwkv")
    dy_attn = _mm(dya, full["w_br_attn"], tb=True, name="d_y_attn")
    dy_rwkv = _mm(dyr, full["w_br_rwkv"], tb=True, name="d_y_rwkv")

    post = _rows_bwd(_rwkv_post_fn, post_rows, post_consts, [_view(dy_rwkv)], nblk=nblk, name="rwkv_post_bwd",
                     diff_rows=[0, 1, 2, 3, 4], diff_consts=[0, 1, 2])
    dys, dr_post, dk_post, dv_post, dgate_post = post[:5]
    grads["rwkv_ln_w"], grads["rwkv_ln_b"], grads["rwkv_r_k"] = post[5:]
    dxt, dv_s, *early_parts = _wkv_bwd(xt, v_t, hist, dys, spread, collect, name="wkv_bwd",
                                       scattered=scatter_early(grads) if scatter_early else ())
    pre_cts = [_view(dxt.reshape(-1, LANES), rows=xt_block[0]), _view(dv_s)] + [
        _view(t, off=-1) for t in (dr_post, dk_post, dv_post, dgate_post)]
    pre = _rows_bwd(_rwkv_pre_fn, pre_rows, pre_consts, pre_cts, nblk=nall, name="rwkv_pre_bwd",
                    diff_rows=list(range(12)), diff_consts=list(range(13)), ct_map=_rwkv_pre_cts)
    d_cur, d_prev, d_par = pre[0:6], pre[6:12], pre[12:]
    up = lambda t: jnp.pad(t[1:], ((0, 1), (0, 0)))
    d_rw = [c + up(p) for c, p in zip(d_cur, d_prev)]
    grads["rwkv_mix"] = jnp.concatenate([d_par[0], d_par[1], d_par[2], d_par[3][:, :DECAY_LORA],
                                         d_par[4][:, :AAA_LORA], d_par[5][:, :GATE_LORA]], axis=1)
    grads["rwkv_w0"], grads["rwkv_w2"] = d_par[6], d_par[7][:DECAY_LORA]
    grads["rwkv_a0"], grads["rwkv_a2"] = d_par[8], d_par[9][:AAA_LORA]
    grads["rwkv_g2"] = d_par[10][:GATE_LORA]
    grads["rwkv_k_k"], grads["rwkv_k_a"] = d_par[11], d_par[12]

    dq_real, dk_r, dva, grads["attn_sinks"] = _attn_bwd(q_r, k_r, proj, C_VA // 128, sinks, o_attn, lse, dy_attn,
                                                        nblk=nblk, name="attn_bwd")
    dq_r = jnp.pad(dq_real, ((BLOCK, 0), (0, 0)))
    (dq,) = _rows_fwd(_rope_bwd_fn, [_view(dq_r), _view(rope_c), _view(rope_s1), _view(rope_s2)], [], [512],
                      nblk=nall, name="rope_q_bwd", out_dtype=BF16, tile=TILE_ALL)
    (dka,) = _rows_fwd(_rope_bwd_fn, [_view(dk_r), _view(rope_c), _view(rope_s1),
                                      _view(rope_s2)], [], [128], nblk=nall, name="rope_k_bwd", out_dtype=BF16,
                       tile=TILE_ALL)

    lead = lambda t: jnp.pad(t, ((BLOCK, 0), (0, 0)))
    pieces = [d_rw[0], d_rw[1], d_rw[2], dq, lead(dg1), lead(dg2), d_rw[5], dka, dva, d_rw[3], d_rw[4],
              jnp.zeros((lp, NP - C_DA - 128), BF16)]
    dproj = jnp.concatenate([p.astype(BF16) for p in pieces], axis=1)
    grads["w_in_p"] = _mm(u, dproj, ta=True, name="dw_in")
    grads["b_in"] = _w_in_unpadded(_colsum(dproj, name="db_in"))
    du, *last_parts = _mm(dproj, w_in_p, tb=True, name="d_u", scattered=scatter_last(grads)) if scatter_last else (
        _mm(dproj, w_in_p, tb=True, name="d_u"),)
    dh, grads["norm_mix_g"] = _rows_bwd(_rms_fn, [_view(hpad)], [row("norm_mix_g")], [_view(du)], nblk=nall,
                                        name="norm_mix_bwd", diff_rows=[0], diff_consts=[0], acc=[_view(lead(dh1))],
                                        tile=TILE_ALL)
    grads["meta_tokens"] = dh[PAD_ROWS:BLOCK]
    return loss, dh[BLOCK:], grads, early_parts, last_parts


def kernel(x, meta_tokens, norm_mix_g, w_in, b_in, attn_sinks, rwkv_mix, rwkv_w0, rwkv_w2, rwkv_a0, rwkv_a2, rwkv_g2, rwkv_k_k, rwkv_k_a, rwkv_r_k, rwkv_ln_w, rwkv_ln_b, w_br_attn, w_br_rwkv, w_o, norm_ffn_g, w_ffn_gate, w_ffn_up, w_ffn_down, norm_final_g, loss_target, m_meta_tokens, m_norm_mix_g, m_w_in, m_b_in, m_attn_sinks, m_rwkv_mix, m_rwkv_w0, m_rwkv_w2, m_rwkv_a0, m_rwkv_a2, m_rwkv_g2, m_rwkv_k_k, m_rwkv_k_a, m_rwkv_r_k, m_rwkv_ln_w, m_rwkv_ln_b, m_w_br_attn, m_w_br_rwkv, m_w_o, m_norm_ffn_g, m_w_ffn_gate, m_w_ffn_up, m_w_ffn_down, m_norm_final_g, v_meta_tokens, v_norm_mix_g, v_w_in, v_b_in, v_attn_sinks, v_rwkv_mix, v_rwkv_w0, v_rwkv_w2, v_rwkv_a0, v_rwkv_a2, v_rwkv_g2, v_rwkv_k_k, v_rwkv_k_a, v_rwkv_r_k, v_rwkv_ln_w, v_rwkv_ln_b, v_w_br_attn, v_w_br_rwkv, v_w_o, v_norm_ffn_g, v_w_ffn_gate, v_w_ffn_up, v_w_ffn_down, v_norm_final_g):
    given = dict(locals())
    wts = {n: _strip(n, given[n]) for n in WEIGHTS}
    as_rows = lambda a: a.reshape(1, -1) if a.ndim == 1 else a
    width = wts["w_in"].shape[1]
    wire = lambda table: [wts[n].astype(BF16) for n, _ in table]

    w_in_all, *early_all = _all_gather([wts["w_in"].astype(BF16)] + wire(EARLY), name="gather_weights")
    full = {n: wts[n] for n in REPLICATED}
    full.update({n: _join(g, axis) for (n, axis), g in zip(EARLY, early_all)})
    full["w_in_p"] = _w_in_padded(w_in_all, shard_width=width)
    gather_late = (wire(LATE), lambda got: {n: _join(g, axis) for (n, axis), g in zip(LATE, got)})
    scatter_early = lambda g: [_split(g[n], axis).astype(BF16) for n, axis in LATE]
    scatter_last = lambda g: [jnp.stack([_w_in_unpadded(g["w_in_p"], p * width, (p + 1) * width)
                                         for p in range(N_DEV)]).astype(BF16)]

    loss_part, grad_x, grads, parts_late, (parts_w_in,) = _device_step(
        x[0], loss_target[0], full, gather_late, scatter_early, scatter_last)

    g_early = [_split(grads[n], axis).astype(BF16) for n, axis in EARLY]
    g_small = [grads[n].reshape(as_rows(given[n]).shape) for n in REPLICATED] + [jnp.full((8, LANES), loss_part)]
    got = _exchange(g_early, g_small, name="exchange_grads")
    parts_early, parts_small, parts_loss = got[:len(EARLY)], got[len(EARLY):-1], got[-1]
    results = [{}, {}, {}, {}]
    for (n, _), parts in zip([("w_in", 1)] + EARLY + LATE, [parts_w_in] + list(parts_early) + list(parts_late)):
        for kind, a in enumerate(_adamw(parts, given[n], given["m_" + n], given["v_" + n], name="adamw_" + n)):
            results[kind][n] = a
    small, loss = _adamw_replicated(parts_small, *[[as_rows(given[pre + n]) for n in REPLICATED]
                                                   for pre in ("", "m_", "v_")], parts_loss, name="adamw_replicated")
    for n, four in zip(REPLICATED, small):
        for kind, a in enumerate(four):
            results[kind][n] = a
    loss = loss[0, 0]
    out = [loss, grad_x[None]]
    for kind in range(4):
        out += [results[kind][n].reshape(given[n].shape) for n in WEIGHTS]
    return tuple(out)
```

```python
import functools

import jax
import jax.numpy as jnp
from jax import lax
from jax.experimental import pallas as pl
from jax.experimental.pallas import tpu as pltpu

F32 = jnp.float32
BF16 = jnp.bfloat16

N_DEV = 8
D_MODEL = 1024
N_META = 16
BLOCK = 128
PAD_ROWS = BLOCK - N_META
HEAD_DIM = 64
Q_HEADS = 8
KV_HEADS = 2
GROUP = Q_HEADS // KV_HEADS
ROPE_DIM = HEAD_DIM // 4
ROPE_HALF = ROPE_DIM // 2
ROPE_THETA = 500000.0
RW_HEADS = 8
RW_DIM = 512
DECAY_LORA = 64
AAA_LORA = 64
GATE_LORA = 160
D_FF = 2816
D_IN = 4640
RMS_EPS = 1e-6
RWKV_LN_EPS = 64e-5
NEG_INF = -1e30
SCAN_T = 16
SCAN_CHUNKS = 8
LANES = 128
PACK_ROWS = 256
TILE_ALL = 384
TILE_ALL_BIG = 1408
TILE_REAL = 512
TILE_WIDE = 256

ADAM_LR = 0.001
ADAM_B1 = 0.9
ADAM_B2 = 0.999
ADAM_EPS = 1e-08
ADAM_WD = 0.01
ADAM_STEP = 10

C_R, C_K, C_V, C_Q = 0, 512, 1024, 1536
C_G1, C_G2 = 2048, 3072
C_DG, C_KA, C_VA, C_DW, C_DA = 4096, 4352, 4480, 4608, 4736
NP = 5120

VMEM_LIMIT = 48 * 1024 * 1024
SCAN_VMEM_LIMIT = 60 * 1024 * 1024


def _cparams(sem, vmem=VMEM_LIMIT):
    return pltpu.CompilerParams(dimension_semantics=sem, vmem_limit_bytes=vmem)


def _pick(n, cands):
    for c in cands:
        if n % c == 0:
            return c
    raise ValueError(f"no tile for {n}")


def _mm(a, b, *, ta=False, tb=False, bias=None, residual=None, name, scattered=()):
    m = a.shape[1] if ta else a.shape[0]
    k = a.shape[0] if ta else a.shape[1]
    n = b.shape[0] if tb else b.shape[1]
    assert k == (b.shape[1] if tb else b.shape[0]), (a.shape, b.shape, ta, tb)
    tm = _pick(m, (512, 1408, 256, 128) if ta else (1056, 1024, 528, 512, 384, 256, 128))
    tn = _pick(n, (1024, 512, 1408, 256, 128))
    if k <= 1024:
        tk = k
    else:
        tk = _pick(k, (1024, 1056, 528, 512) if (ta and not tb) else (1024, 1408, 512, 256, 128))
    nk = k // tk
    has_bias = bias is not None
    has_res = residual is not None
    dn = (((0 if ta else 1,), (1 if tb else 0,)), ((), ()))

    def body(*refs):
        a_ref, b_ref = refs[0], refs[1]
        pos = 2
        bias_ref = res_ref = None
        if has_bias:
            bias_ref = refs[pos]
            pos += 1
        if has_res:
            res_ref = refs[pos]
            pos += 1
        o_ref, acc_ref = refs[pos], refs[pos + 1]
        kk = pl.program_id(2)
        part = lax.dot_general(a_ref[...].astype(BF16), b_ref[...].astype(BF16), dn, preferred_element_type=F32)

        def finish(out):
            if has_bias:
                out = out + bias_ref[...]
            if has_res:
                out = out + res_ref[...]
            o_ref[...] = out

        if nk == 1:
            finish(part)
        else:
            @pl.when(kk == 0)
            def _():
                acc_ref[...] = part

            @pl.when((kk > 0) & (kk < nk - 1))
            def _():
                acc_ref[...] += part

            @pl.when(kk == nk - 1)
            def _():
                finish(acc_ref[...] + part)

    in_specs = [
        pl.BlockSpec((tk, tm), lambda i, j, kk: (kk, i)) if ta else pl.BlockSpec((tm, tk), lambda i, j, kk: (i, kk)),
        pl.BlockSpec((tn, tk), lambda i, j, kk: (j, kk)) if tb else pl.BlockSpec((tk, tn), lambda i, j, kk: (kk, j)),
    ]
    args = [a, b]
    if has_bias:
        in_specs.append(pl.BlockSpec((1, tn), lambda i, j, kk: (0, j)))
        args.append(bias)
    if has_res:
        in_specs.append(pl.BlockSpec((tm, tn), lambda i, j, kk: (i, j)))
        args.append(residual)
    grid = (m // tm, n // tn, nk)
    n_x = len(scattered)
    out = pl.pallas_call(
        _with_exchange(body, len(args), 1, 1, scattered, (), grid), name=name, grid=grid,
        in_specs=in_specs + [ANY] * n_x,
        out_specs=[pl.BlockSpec((tm, tn), lambda i, j, kk: (i, j))] + [ANY] * n_x,
        out_shape=[jax.ShapeDtypeStruct((m, n), F32)] + _exchange_shapes(scattered, ()),
        scratch_shapes=[pltpu.VMEM((tm, tn) if nk > 1 else (8, LANES), F32)] + (_exchange_sems(n_x) if n_x else []),
        compiler_params=_cparams(("arbitrary",) * 3 if n_x else ("parallel", "parallel", "arbitrary")),
    )(*args, *scattered)
    return out if n_x else out[0]


def _colsum(x, name):
    m, n = x.shape
    tm = _pick(m, (1408, 512, 384, BLOCK))

    def body(x_ref, o_ref):
        i = pl.program_id(0)
        s = jnp.sum(x_ref[...].astype(F32), axis=0, keepdims=True)

        @pl.when(i == 0)
        def _():
            o_ref[...] = s

        @pl.when(i > 0)
        def _():
            o_ref[...] += s

    return pl.pallas_call(
        body, name=name, grid=(m // tm,),
        in_specs=[pl.BlockSpec((tm, n), lambda i: (i, 0))],
        out_specs=pl.BlockSpec((1, n), lambda i: (0, 0)),
        out_shape=jax.ShapeDtypeStruct((1, n), F32),
        compiler_params=_cparams(("arbitrary",)),
    )(x)


def _view(arr, width=None, col=0, off=0, rows=BLOCK):
    return (arr, arr.shape[1] if width is None else width, col, off, rows)


def _row_spec(view):
    _, width, col, off, rows = view
    if off < 0:
        return pl.BlockSpec((rows, width), lambda i, col=col, off=off: (jnp.maximum(i + off, 0), col))
    return pl.BlockSpec((rows, width), lambda i, col=col, off=off: (i + off, col))


def _const_spec(arr):
    return pl.BlockSpec(arr.shape, lambda i: (0,) * arr.ndim)


def _retile(views, tile):
    assert all(v[3] == 0 and v[4] == BLOCK for v in views)
    return [v[:4] + (tile,) for v in views]


def _rows_fwd(fn, rows, consts, out_widths, *, nblk, name, out_dtype=F32, tile=BLOCK):
    nr, nc = len(rows), len(consts)
    tile = tile if (nblk * BLOCK) % tile == 0 else BLOCK
    if tile != BLOCK:
        rows, nblk = _retile(rows, tile), nblk * BLOCK // tile
    out_blocks = [(tile, w) if isinstance(w, int) else w for w in out_widths]

    def body(*refs):
        i = pl.program_id(0)
        vals = [r[...] for r in refs[:nr + nc]]
        outs = fn(i, *vals)
        for o_ref, o in zip(refs[nr + nc:], outs):
            o_ref[...] = o.astype(o_ref.dtype)

    return pl.pallas_call(
        body, name=name, grid=(nblk,),
        in_specs=[_row_spec(v) for v in rows] + [_const_spec(c) for c in consts],
        out_specs=[pl.BlockSpec(b, lambda i: (i, 0)) for b in out_blocks],
        out_shape=[jax.ShapeDtypeStruct((nblk * r, w), out_dtype) for r, w in out_blocks],
        compiler_params=_cparams(("parallel",)),
    )(*[v[0] for v in rows], *consts)


def _rows_bwd(fn, rows, consts, cts, *, nblk, name, diff_rows, diff_consts, acc=None, fwd_widths=(), row_dtype=F32,
              ct_map=None, tile=BLOCK):
    nr, nc = len(rows), len(consts)
    acc = acc or [None] * len(diff_rows)
    tile = tile if (nblk * BLOCK) % tile == 0 else BLOCK
    if tile != BLOCK:
        rows, nblk = _retile(rows, tile), nblk * BLOCK // tile
        cts = [c if c is None else _retile([c], tile)[0] for c in cts]
        acc = [a if a is None else _retile([a], tile)[0] for a in acc]
    ct_views = [c for c in cts if c is not None]
    acc_views = [a for a in acc if a is not None]
    n_in = nr + nc + len(ct_views) + len(acc_views)
    n_fwd = len(fwd_widths)

    def body(*refs):
        i = pl.program_id(0)
        row_vals = [r[...] for r in refs[:nr]]
        const_vals = [r[...] for r in refs[nr:nr + nc]]
        ct_vals = [r[...] for r in refs[nr + nc:nr + nc + len(ct_views)]]
        acc_vals = [r[...] for r in refs[nr + nc + len(ct_views):n_in]]
        out_refs = refs[n_in:]

        def f(*dargs):
            rv = list(row_vals)
            cv = list(const_vals)
            for pos, idx in enumerate(diff_rows):
                rv[idx] = dargs[pos]
            for pos, idx in enumerate(diff_consts):
                cv[idx] = dargs[len(diff_rows) + pos]
            return tuple(fn(i, *rv, *cv))

        primals = [row_vals[idx] for idx in diff_rows] + [const_vals[idx] for idx in diff_consts]
        outs, pull = jax.vjp(f, *primals)
        full_ct, ci = [], 0
        if ct_map is not None:
            full_ct = ct_map(i, *ct_vals)
        else:
            for o, c in zip(outs, cts):
                if c is None:
                    full_ct.append(jnp.zeros_like(o))
                else:
                    full_ct.append(ct_vals[ci])
                    ci += 1
        grads = pull(tuple(full_ct))
        for o_ref, o in zip(out_refs[:n_fwd], outs):
            o_ref[...] = o
        ai = 0
        for pos in range(len(diff_rows)):
            g = grads[pos]
            if acc[pos] is not None:
                g = g + acc_vals[ai]
                ai += 1
            out_refs[n_fwd + pos][...] = g.astype(row_dtype)
        for pos in range(len(diff_consts)):
            g = grads[len(diff_rows) + pos]
            o_ref = out_refs[n_fwd + len(diff_rows) + pos]

            @pl.when(i == 0)
            def _(o_ref=o_ref, g=g):
                o_ref[...] = g

            @pl.when(i > 0)
            def _(o_ref=o_ref, g=g):
                o_ref[...] += g

    out_specs = [pl.BlockSpec((tile, w), lambda i: (i, 0)) for w in fwd_widths]
    out_shape = [jax.ShapeDtypeStruct((nblk * tile, w), F32) for w in fwd_widths]
    for idx in diff_rows:
        out_specs.append(pl.BlockSpec((tile, rows[idx][1]), lambda i: (i, 0)))
        out_shape.append(jax.ShapeDtypeStruct((nblk * tile, rows[idx][1]), row_dtype))
    for idx in diff_consts:
        out_specs.append(_const_spec(consts[idx]))
        out_shape.append(jax.ShapeDtypeStruct(consts[idx].shape, F32))
    return pl.pallas_call(
        body, name=name, grid=(nblk,),
        in_specs=([_row_spec(v) for v in rows] + [_const_spec(c) for c in consts]
                  + [_row_spec(v) for v in ct_views] + [_row_spec(v) for v in acc_views]),
        out_specs=out_specs, out_shape=out_shape,
        compiler_params=_cparams(("arbitrary",)),
    )(*[v[0] for v in rows], *consts, *[v[0] for v in ct_views], *[v[0] for v in acc_views])


def _rms_fn(i, x, g):
    return (x * lax.rsqrt(jnp.mean(x * x, axis=-1, keepdims=True) + RMS_EPS) * g,)


def _sigmoid(x):
    return 1.0 / (1.0 + jnp.exp(-x))


def _softplus(x):
    return jnp.maximum(x, 0.0) + jnp.log(1.0 + jnp.exp(-jnp.abs(x)))


def _split2(x):
    hi = x.astype(BF16)
    lo = (x - hi.astype(F32)).astype(BF16)
    return jnp.concatenate([hi, lo], axis=1)


@jax.custom_vjp
def _head_sum(x):
    r = lax.broadcasted_iota(jnp.int32, (2 * RW_DIM, RW_DIM), 0) % RW_DIM // HEAD_DIM
    c = lax.broadcasted_iota(jnp.int32, (2 * RW_DIM, RW_DIM), 1) // HEAD_DIM
    return jnp.dot(_split2(x), (r == c).astype(BF16), preferred_element_type=F32)


_head_sum.defvjp(lambda x: (_head_sum(x), None), lambda _, ct: (_head_sum(ct),))


@jax.custom_vjp
def _dot_bf16(x, w):
    return jnp.dot(x.astype(BF16), w.astype(BF16), preferred_element_type=F32)


def _dot_bf16_bwd(res, ct):
    x, w = res
    ct = ct.astype(BF16)
    dx = lax.dot_general(ct, w.astype(BF16), (((1,), (1,)), ((), ())), preferred_element_type=F32)
    dw = lax.dot_general(x.astype(BF16), ct, (((0,), (0,)), ((), ())), preferred_element_type=F32)
    return dx, dw


_dot_bf16.defvjp(lambda x, w: (_dot_bf16(x, w), (x, w)), _dot_bf16_bwd)


def _rwkv_pre_fn(i, r, k, v, dw, da, dg, r_p, k_p, v_p, dw_p, da_p, dg_p,
                 mix_r, mix_k, mix_v, mix_dw, mix_da, mix_dg, w0, w2, a0, a2, g2, k_k, k_a):
    row = i * BLOCK + lax.broadcasted_iota(jnp.int32, (BLOCK, 1), 0)
    live = row >= PAD_ROWS
    live_prev = row >= PAD_ROWS + 1

    def shift(cur, prev, mix):
        cur = jnp.where(live, cur, 0.0)
        prev = jnp.where(live_prev, prev, 0.0)
        return cur + (prev - cur) * mix

    r = shift(r, r_p, mix_r)
    k = shift(k, k_p, mix_k)
    v = shift(v, v_p, mix_v)
    dw = shift(dw, dw_p, mix_dw)
    da = shift(da, da_p, mix_da)
    dg = shift(dg, dg_p, mix_dg)
    wlog = -_softplus(-(w0 + _dot_bf16(jnp.tanh(dw), w2))) - 0.5
    decay = jnp.exp(-jnp.exp(wlog))
    a = _sigmoid(a0 + _dot_bf16(da, a2))
    g = _dot_bf16(_sigmoid(dg), g2)
    kk = k * k_k
    norm_sq = jnp.where(live, _head_sum(kk * kk), 1.0)
    kk = kk / jnp.maximum(jnp.sqrt(norm_sq), 1e-12)
    k_mod = k * (1.0 + (a - 1.0) * k_a)
    return r, decay, k_mod, v, -kk, kk * a, g


def _rwkv_pre_xt_fn(i, *args):
    r, decay, k_mod, v, a_neg, b, g = _rwkv_pre_fn(i, *args)
    t = SCAN_T
    xt = jnp.concatenate([_rows_to_xt(x[c * t:(c + 1) * t]) for c in range(BLOCK // t)
                          for x in (a_neg, decay, b, k_mod, r)], axis=0)
    return r, k_mod, v, g, xt


def _rwkv_pre_cts(i, dxt, dv_s, dr_p, dk_p, dv_p, dg_p):
    t = SCAN_T
    d_a, d_w, d_b, d_k, d_r = [
        jnp.concatenate([_xt_to_rows(dxt[c * VEC_ROWS + n * HEAD_DIM:c * VEC_ROWS + (n + 1) * HEAD_DIM])
                         for c in range(BLOCK // t)], axis=0) for n in range(N_VEC)]
    dr_p, dk_p, dv_p, dg_p = [jnp.where(i > 0, x, 0.0) for x in (dr_p, dk_p, dv_p, dg_p)]
    return d_r + dr_p, d_w, d_k + dk_p, dv_s + dv_p, d_a, d_b, dg_p


def _rwkv_post_fn(i, ys, r, k_mod, v, g, ln_w, ln_b, r_k):
    mean = _head_sum(ys) * (1.0 / HEAD_DIM)
    d = ys - mean
    var = _head_sum(d * d) * (1.0 / HEAD_DIM)
    yn = d * lax.rsqrt(var + RWKV_LN_EPS) * ln_w + ln_b
    bonus = _head_sum(r * k_mod * r_k) * v
    return ((yn + bonus) * g,)


def _merge_fn(i, ya, yr, g1, g2):
    return (_sigmoid(g1) * ya + _sigmoid(g2) * yr,)


def _swiglu_fn(i, gate, up):
    return (gate * _sigmoid(gate) * up,)


def _loss_fn(i, h, tgt, g):
    y = h * lax.rsqrt(jnp.mean(h * h, axis=-1, keepdims=True) + RMS_EPS) * g
    err = y - tgt
    return (0.5 * jnp.mean(err * err, axis=-1, keepdims=True),)


def _rope_tables(lp):
    pos = (jnp.arange(lp, dtype=jnp.int32) - PAD_ROWS).astype(F32)
    inv_freq = jnp.power(jnp.float32(ROPE_THETA), -jnp.arange(ROPE_HALF, dtype=F32) * (2.0 / ROPE_DIM))
    ang = pos[:, None] * inv_freq[None, :]
    cos, sin = jnp.cos(ang), jnp.sin(ang)
    one = jnp.ones((lp, HEAD_DIM - ROPE_DIM), F32)
    zero_h = jnp.zeros((lp, ROPE_HALF), F32)
    zero_r = jnp.zeros((lp, HEAD_DIM - ROPE_DIM), F32)
    c = jnp.concatenate([cos, cos, one], axis=1)
    s1 = jnp.concatenate([-sin, zero_h, zero_r], axis=1)
    s2 = jnp.concatenate([zero_h, sin, zero_r], axis=1)
    return tuple(jnp.tile(t, (1, LANES // HEAD_DIM)) for t in (c, s1, s2))


def _rope_fwd_fn(i, x, c, s1, s2):
    n = x.shape[1]
    c, s1, s2 = [jnp.tile(t, (1, n // LANES)) for t in (c, s1, s2)]
    return (x * c + pltpu.roll(x, n - ROPE_HALF, 1) * s1 + pltpu.roll(x, ROPE_HALF, 1) * s2,)


def _rope_bwd_fn(i, dy, c, s1, s2):
    n = dy.shape[1]
    c, s1, s2 = [jnp.tile(t, (1, n // LANES)) for t in (c, s1, s2)]
    return (dy * c + pltpu.roll(dy * s1, ROPE_HALF, 1) + pltpu.roll(dy * s2, n - ROPE_HALF, 1),)


def _attn_mask(i):
    r = lax.broadcasted_iota(jnp.int32, (BLOCK, 3 * BLOCK), 0)
    c = lax.broadcasted_iota(jnp.int32, (BLOCK, 3 * BLOCK), 1)
    meta = (c < BLOCK) & (c >= PAD_ROWS)
    prev = (c >= BLOCK) & (c < 2 * BLOCK) & ((c - BLOCK) > r) & (i >= 1)
    cur = (c >= 2 * BLOCK) & ((c - 2 * BLOCK) <= r)
    return meta | prev | cur


def _attn_rows(ref, g):
    return ref[:, g * HEAD_DIM:(g + 1) * HEAD_DIM]


def _attn_group(i, g, q_all, k_refs, v_refs, s_ref):
    heads = range(g * GROUP, (g + 1) * GROUP)
    kcat = jnp.concatenate([_attn_rows(r, g) for r in k_refs], axis=0).astype(BF16)
    vcat = jnp.concatenate([_attn_rows(r, g) for r in v_refs], axis=0).astype(BF16)
    qg = jnp.concatenate([q_all[:, h * HEAD_DIM:(h + 1) * HEAD_DIM] for h in heads], axis=0).astype(BF16)
    sink = jnp.concatenate([jnp.broadcast_to(s_ref[0:1, h:h + 1], (BLOCK, 1)) for h in heads], axis=0)
    s = lax.dot_general(qg, kcat, (((1,), (1,)), ((), ())), preferred_element_type=F32) * (HEAD_DIM ** -0.5)
    valid = jnp.concatenate([_attn_mask(i)] * GROUP, axis=0)
    return heads, qg, kcat, vcat, sink, jnp.where(valid, s, NEG_INF)


ATTN_SUB = 2


def _attn_specs(v_col):
    q = [pl.BlockSpec((BLOCK, Q_HEADS * HEAD_DIM), lambda i, n=n: (ATTN_SUB * i + 1 + n, 0)) for n in range(ATTN_SUB)]
    blk = lambda col: ([pl.BlockSpec((BLOCK, 2 * HEAD_DIM), lambda i: (0, col))]
                       + [pl.BlockSpec((BLOCK, 2 * HEAD_DIM), lambda i, n=n: (ATTN_SUB * i + n, col))
                          for n in range(ATTN_SUB + 1)])
    return q + blk(0) + blk(v_col) + [pl.BlockSpec((1, Q_HEADS), lambda i: (0, 0))]


def _attn_split(refs):
    n = ATTN_SUB
    q_refs, k_refs, v_refs = refs[:n], refs[n:2 * n + 2], refs[2 * n + 2:3 * n + 4]
    return q_refs, k_refs, v_refs, refs[3 * n + 4], refs[3 * n + 5:]


def _attn_fwd_block(blk, q_ref, keys, vals, s_ref, o_ref, lse_ref):
    q_all = q_ref[...]
    for g in range(KV_HEADS):
        heads, _, _, vcat, sink, s = _attn_group(blk, g, q_all, keys, vals, s_ref)
        m = jnp.maximum(jnp.max(s, axis=-1, keepdims=True), sink)
        p = jnp.exp(s - m)
        den = jnp.sum(p, axis=-1, keepdims=True) + jnp.exp(sink - m)
        o = jnp.dot(p.astype(BF16), vcat, preferred_element_type=F32) / den
        lse = m + jnp.log(den)
        for n, h in enumerate(heads):
            o_ref[:, h * HEAD_DIM:(h + 1) * HEAD_DIM] = o[n * BLOCK:(n + 1) * BLOCK]
            lse_ref[:, h:h + 1] = lse[n * BLOCK:(n + 1) * BLOCK]


def _attn_bwd(q, k, v, v_col, sinks, o, lse, do, *, nblk, name):
    lp = k.shape[0]
    rows = ATTN_SUB * BLOCK

    def body(*refs):
        q_refs, k_refs, v_refs, s_ref, (o_ref, lse_ref, do_ref, dq_ref, dk_ref, dv_ref, ds_ref) = _attn_split(refs)
        i = pl.program_id(0)

        @pl.when(i == 0)
        def _():
            dk_ref[...] = jnp.zeros_like(dk_ref)
            dv_ref[...] = jnp.zeros_like(dv_ref)
            ds_ref[...] = jnp.zeros_like(ds_ref)

        lane = lax.broadcasted_iota(jnp.int32, (1, Q_HEADS), 1)
        for sub in range(ATTN_SUB):
            at = slice(sub * BLOCK, (sub + 1) * BLOCK)
            blk = ATTN_SUB * i + sub
            keys = (k_refs[0], k_refs[1 + sub], k_refs[2 + sub])
            vals = (v_refs[0], v_refs[1 + sub], v_refs[2 + sub])
            prev_rows = pl.ds(pl.multiple_of(blk * BLOCK, BLOCK), BLOCK)
            cur_rows = pl.ds(pl.multiple_of((blk + 1) * BLOCK, BLOCK), BLOCK)
            q_all, o_all, do_all, lse_all = q_refs[sub][...], o_ref[at, :], do_ref[at, :], lse_ref[at, :]
            for g in range(KV_HEADS):
                heads, qg, kcat, vcat, sink, s = _attn_group(blk, g, q_all, keys, vals, s_ref)
                stack = lambda x: jnp.concatenate([x[:, h * HEAD_DIM:(h + 1) * HEAD_DIM] for h in heads], axis=0)
                lse_g = jnp.concatenate([lse_all[:, h:h + 1] for h in heads], axis=0)
                do_g = stack(do_all)
                p = jnp.exp(s - lse_g)
                delta = jnp.sum(do_g * stack(o_all), axis=-1, keepdims=True)
                dp = lax.dot_general(do_g.astype(BF16), vcat, (((1,), (1,)), ((), ())), preferred_element_type=F32)
                dsc = (p * (dp - delta) * (HEAD_DIM ** -0.5)).astype(BF16)
                dq = jnp.dot(dsc, kcat, preferred_element_type=F32)
                dk_all = lax.dot_general(dsc, qg, (((0,), (0,)), ((), ())), preferred_element_type=F32)
                dv_all = lax.dot_general(p.astype(BF16), do_g.astype(BF16), (((0,), (0,)), ((), ())),
                                         preferred_element_type=F32)
                cols = slice(g * HEAD_DIM, (g + 1) * HEAD_DIM)
                for ref, full in ((dk_ref, dk_all), (dv_ref, dv_all)):
                    ref[0:BLOCK, cols] += full[0:BLOCK]
                    ref[prev_rows, cols] += full[BLOCK:2 * BLOCK]
                    ref[cur_rows, cols] += full[2 * BLOCK:]
                sink_part = jnp.exp(sink - lse_g) * delta
                for n, h in enumerate(heads):
                    dq_ref[at, h * HEAD_DIM:(h + 1) * HEAD_DIM] = dq[n * BLOCK:(n + 1) * BLOCK]
                    dsink = -jnp.sum(sink_part[n * BLOCK:(n + 1) * BLOCK], axis=0, keepdims=True)
                    ds_ref[...] += jnp.where(lane == h, dsink, 0.0)

    qspec = pl.BlockSpec((rows, Q_HEADS * HEAD_DIM), lambda i: (i, 0))
    whole = pl.BlockSpec((lp, 2 * HEAD_DIM), lambda i: (0, 0))
    return pl.pallas_call(
        body, name=name, grid=(nblk // ATTN_SUB,),
        in_specs=_attn_specs(v_col) + [qspec, pl.BlockSpec((rows, Q_HEADS), lambda i: (i, 0)), qspec],
        out_specs=[qspec, whole, whole, pl.BlockSpec((1, Q_HEADS), lambda i: (0, 0))],
        out_shape=[jax.ShapeDtypeStruct((nblk * BLOCK, Q_HEADS * HEAD_DIM), F32),
                   jax.ShapeDtypeStruct((lp, 2 * HEAD_DIM), F32), jax.ShapeDtypeStruct((lp, 2 * HEAD_DIM), F32),
                   jax.ShapeDtypeStruct((1, Q_HEADS), F32)],
        compiler_params=_cparams(("arbitrary",)),
    )(*[q] * ATTN_SUB, *[k] * (ATTN_SUB + 2), *[v] * (ATTN_SUB + 2), sinks, o, lse, do)


N_VEC = 5
VEC_ROWS = N_VEC * HEAD_DIM


def _selectors():
    t = SCAN_T
    shape = (t, 2 * LANES, RW_DIM)
    step, src, dst = [lax.broadcasted_iota(jnp.int32, shape, d) for d in range(3)]
    src = src % LANES
    spread = ((src // t == dst // HEAD_DIM) & (src % t == step)).astype(BF16)
    shape = (t, RW_DIM, LANES)
    step, src, dst = [lax.broadcasted_iota(jnp.int32, shape, d) for d in range(3)]
    collect = ((src // HEAD_DIM == dst // t) & (dst % t == step)).astype(BF16)
    return spread, collect


def _rows_to_xt(x):
    low = lax.broadcasted_iota(jnp.int32, (SCAN_T, LANES), 1) < HEAD_DIM
    pieces = []
    for m in range(RW_HEADS // 2):
        pair = x[:, m * LANES:(m + 1) * LANES]
        pieces += [jnp.where(low, pair, 0.0), jnp.where(low, pltpu.roll(pair, HEAD_DIM, 1), 0.0)]
    return jnp.concatenate(pieces, axis=0).T[:HEAD_DIM]


def _xt_to_rows(a):
    t = SCAN_T
    a_t = jnp.concatenate([a, jnp.zeros_like(a)], axis=0).T
    pairs = [a_t[2 * m * t:(2 * m + 1) * t] + pltpu.roll(a_t[(2 * m + 1) * t:(2 * m + 2) * t], HEAD_DIM, 1)
             for m in range(RW_HEADS // 2)]
    return jnp.concatenate(pairs, axis=1)


def _with_exchange(compute, n_in, n_out, n_scratch, scattered, shared, grid):
    n_sc = len(scattered)
    n_x = n_sc + len(shared)
    if n_x == 0:
        return compute

    def body(*refs):
        ins, x_in = refs[:n_in], refs[n_in:n_in + n_x]
        outs, x_out = refs[n_in + n_x:n_in + n_x + n_out], refs[n_in + n_x + n_out:n_in + 2 * n_x + n_out]
        scratch = refs[n_in + 2 * n_x + n_out:n_in + 2 * n_x + n_out + n_scratch]
        sems = refs[n_in + 2 * n_x + n_out + n_scratch:]

        first = last = True
        for d, size in enumerate(grid):
            first = first & (pl.program_id(d) == 0)
            last = last & (pl.program_id(d) == size - 1)

        @pl.when(first)
        def _():
            for cp in _exchange_copies(x_in, x_out, n_sc, *sems):
                cp.start()

        compute(*ins, *outs, *scratch)

        @pl.when(last)
        def _():
            for cp in _exchange_copies(x_in, x_out, n_sc, *sems):
                cp.wait()

    return body


def _wkv_fwd(xt, v, spread, attn, name, shared=()):
    t_steps = SCAN_T
    nch = xt.shape[0]
    per = SCAN_CHUNKS
    grid = (nch // per,)
    rows = per * t_steps
    assert rows == BLOCK
    n_x = len(shared)
    q, k, v_arr, v_col, sinks = attn

    def compute(xt_ref, v_ref, sel_ref, q_ref, km_ref, kp_ref, kc_ref, vm_ref, vp_ref, vc_ref, s_ref,
                y_ref, hist_ref, o_ref, lse_ref, st_ref):
        @pl.when(pl.program_id(0) == 0)
        def _():
            st_ref[...] = jnp.zeros_like(st_ref)

        _attn_fwd_block(pl.program_id(0) - 1, q_ref, (km_ref, kp_ref, kc_ref), (vm_ref, vp_ref, vc_ref), s_ref,
                        o_ref, lse_ref)
        st = st_ref[...]
        for c in range(per):
            x2 = _split2(xt_ref[c])
            for j in range(t_steps):
                row = c * t_steps + j
                cols = jnp.dot(x2, sel_ref[j], preferred_element_type=F32)
                a_c, w_c, b_c, k_c, r_c = [cols[n * HEAD_DIM:(n + 1) * HEAD_DIM] for n in range(N_VEC)]
                hist_ref[row] = st
                sa = jnp.sum(st * a_c, axis=0, keepdims=True)
                st = st * w_c + b_c * sa + k_c * v_ref[row:row + 1, :]
                y_ref[row:row + 1, :] = jnp.sum(st * r_c, axis=0, keepdims=True)
        st_ref[...] = st

    before = lambda c: jnp.maximum(c - 1, 0)
    kv = lambda col: [pl.BlockSpec((BLOCK, 2 * HEAD_DIM), lambda c: (0, col)),
                      pl.BlockSpec((BLOCK, 2 * HEAD_DIM), lambda c: (before(c), col)),
                      pl.BlockSpec((BLOCK, 2 * HEAD_DIM), lambda c: (c, col))]
    seq = (nch * t_steps) - BLOCK
    return pl.pallas_call(
        _with_exchange(compute, 11, 4, 1, (), shared, grid), name=name, grid=grid,
        in_specs=[pl.BlockSpec((per, VEC_ROWS, LANES), lambda c: (c, 0, 0)),
                  pl.BlockSpec((rows, RW_DIM), lambda c: (c, 0)),
                  pl.BlockSpec(spread.shape, lambda c: (0, 0, 0)),
                  pl.BlockSpec((BLOCK, Q_HEADS * HEAD_DIM), lambda c: (c, 0))] + kv(0) + kv(v_col)
                 + [pl.BlockSpec((1, Q_HEADS), lambda c: (0, 0))] + [ANY] * n_x,
        out_specs=[pl.BlockSpec((rows, RW_DIM), lambda c: (c, 0)),
                   pl.BlockSpec((rows, HEAD_DIM, RW_DIM), lambda c: (c, 0, 0)),
                   pl.BlockSpec((BLOCK, Q_HEADS * HEAD_DIM), lambda c: (before(c), 0)),
                   pl.BlockSpec((BLOCK, Q_HEADS), lambda c: (before(c), 0))] + [ANY] * n_x,
        out_shape=[jax.ShapeDtypeStruct((nch * t_steps, RW_DIM), F32),
                   jax.ShapeDtypeStruct((nch * t_steps, HEAD_DIM, RW_DIM), F32),
                   jax.ShapeDtypeStruct((seq, Q_HEADS * HEAD_DIM), F32),
                   jax.ShapeDtypeStruct((seq, Q_HEADS), F32)] + _exchange_shapes((), shared),
        scratch_shapes=[pltpu.VMEM((HEAD_DIM, RW_DIM), F32)] + (_exchange_sems(n_x) if n_x else []),
        compiler_params=_cparams(("arbitrary",), SCAN_VMEM_LIMIT),
    )(xt, v, spread, q, k, k, k, v_arr, v_arr, v_arr, sinks, *shared)


def _wkv_bwd(xt, v, hist, dy, spread, collect, name, scattered=()):
    t_steps = SCAN_T
    nch = xt.shape[0]
    n_x = len(scattered)
    per = SCAN_CHUNKS
    nsteps = nch // per
    grid = (nsteps,)
    rows = per * t_steps
    lead = BLOCK // rows

    def compute(xt_ref, v_ref, hist_ref, dy_ref, sel_ref, col_ref, dxt_ref, dv_ref, g_ref):
        @pl.when(pl.program_id(0) == 0)
        def _():
            g_ref[...] = jnp.zeros_like(g_ref)

        has_dy = nsteps - 1 - pl.program_id(0) >= lead
        gst = g_ref[...]
        nxt = None
        for c in reversed(range(per)):
            x2 = _split2(xt_ref[c])
            acc = jnp.zeros((VEC_ROWS, LANES), F32)
            for j in reversed(range(t_steps)):
                row = c * t_steps + j
                cols = jnp.dot(x2, sel_ref[j], preferred_element_type=F32)
                a_c, w_c, b_c, k_c, r_c = [cols[n * HEAD_DIM:(n + 1) * HEAD_DIM] for n in range(N_VEC)]
                prev = hist_ref[row]
                v_row = v_ref[row:row + 1, :]
                dy_row = jnp.where(has_dy, dy_ref[row:row + 1, :], 0.0)
                sa = jnp.sum(prev * a_c, axis=0, keepdims=True)
                if nxt is None:
                    nxt = prev * w_c + b_c * sa + k_c * v_row
                gst = gst + r_c * dy_row
                dv_ref[row:row + 1, :] = jnp.sum(gst * k_c, axis=0, keepdims=True)
                dsa = jnp.sum(gst * b_c, axis=0, keepdims=True)
                prods = jnp.concatenate([p.astype(BF16) for p in
                                         (prev * dsa, gst * prev, gst * sa, gst * v_row, nxt * dy_row)], axis=0)
                acc = acc + jnp.dot(prods, col_ref[j], preferred_element_type=F32)
                gst = gst * w_c + a_c * dsa
                nxt = prev
            dxt_ref[c] = acc
        g_ref[...] = gst

    rev3 = lambda c: (nsteps - 1 - c, 0, 0)
    rev2 = lambda c: (nsteps - 1 - c, 0)
    rowspec = pl.BlockSpec((rows, RW_DIM), rev2)
    return pl.pallas_call(
        _with_exchange(compute, 6, 2, 1, scattered, (), grid), name=name, grid=grid,
        in_specs=[pl.BlockSpec((per, VEC_ROWS, LANES), rev3), rowspec,
                  pl.BlockSpec((rows, HEAD_DIM, RW_DIM), rev3),
                  pl.BlockSpec((rows, RW_DIM), lambda c: (jnp.maximum(nsteps - 1 - c - lead, 0), 0)),
                  pl.BlockSpec(spread.shape, lambda c: (0, 0, 0)),
                  pl.BlockSpec(collect.shape, lambda c: (0, 0, 0))] + [ANY] * n_x,
        out_specs=[pl.BlockSpec((per, VEC_ROWS, LANES), rev3), rowspec] + [ANY] * n_x,
        out_shape=[jax.ShapeDtypeStruct((nch, VEC_ROWS, LANES), F32),
                   jax.ShapeDtypeStruct((nch * t_steps, RW_DIM), F32)] + _exchange_shapes(scattered, ()),
        scratch_shapes=[pltpu.VMEM((HEAD_DIM, RW_DIM), F32)] + (_exchange_sems(n_x) if n_x else []),
        compiler_params=_cparams(("arbitrary",), SCAN_VMEM_LIMIT),
    )(xt, v, hist, dy, spread, collect, *scattered)


MESH = pl.DeviceIdType.MESH
ANY = pl.BlockSpec(memory_space=pltpu.HBM)


def _all_gather(arrays, name):
    n_arr = len(arrays)
    per = N_DEV - 1

    def body(*refs):
        x_refs, out_refs = refs[:n_arr], refs[n_arr:2 * n_arr]
        send_sems, recv_sems, local_sems = refs[2 * n_arr:]
        xi, yi, ci = lax.axis_index("x"), lax.axis_index("y"), lax.axis_index("c")
        me, sibling = (xi, yi, ci), (xi, yi, 1 - ci)
        chips = [(1 - xi, yi), (xi, 1 - yi), (1 - xi, 1 - yi)]

        def slot(a, px, py, pc):
            return out_refs[a].at[4 * px + 2 * py + pc]

        def copy(a, sem, block, to, src=None):
            return pltpu.make_async_remote_copy(
                src_ref=slot(a, *block) if src is None else src, dst_ref=slot(a, *block),
                send_sem=send_sems.at[per * a + sem], recv_sem=recv_sems.at[per * a + sem],
                device_id=to, device_id_type=MESH)

        mine = [pltpu.make_async_copy(x_refs[a], slot(a, *me), local_sems.at[a]) for a in range(n_arr)]
        for cp in mine:
            cp.start()
        sent = []
        for a in range(n_arr):
            sent.append(copy(a, 0, me, sibling, src=x_refs[a]))
            sent += [copy(a, 1 + j, me, (*chip, ci), src=x_refs[a]) for j, chip in enumerate(chips)]
        for cp in sent:
            cp.start()
        for j, chip in enumerate(chips):
            for a in range(n_arr):
                copy(a, 1 + j, (*chip, ci), me).wait_recv()
                onward = copy(a, 4 + j, (*chip, ci), sibling)
                onward.start()
                sent.append(onward)
        for a in range(n_arr):
            copy(a, 0, sibling, me).wait_recv()
        for j, chip in enumerate(chips):
            for a in range(n_arr):
                copy(a, 4 + j, (*chip, 1 - ci), me).wait_recv()
        for cp in sent:
            cp.wait_send()
        for cp in mine:
            cp.wait()

    sems = pltpu.SemaphoreType.DMA((per * n_arr,))
    return pl.pallas_call(
        body, name=name, out_shape=[jax.ShapeDtypeStruct((N_DEV,) + a.shape, a.dtype) for a in arrays],
        in_specs=[ANY] * n_arr, out_specs=[ANY] * n_arr,
        scratch_shapes=[sems, sems, pltpu.SemaphoreType.DMA((n_arr,))],
    )(*arrays)


def _exchange_copies(in_refs, out_refs, n_scattered, send_sems, recv_sems, local_sems):
    n_arr = len(in_refs)
    per = N_DEV - 1
    xi, yi, ci = lax.axis_index("x"), lax.axis_index("y"), lax.axis_index("c")
    me = 4 * xi + 2 * yi + ci
    src_of = lambda a, peer: in_refs[a].at[peer] if a < n_scattered else in_refs[a]
    copies = []
    for d in range(1, N_DEV):
        px = 1 - xi if d & 4 else xi
        py = 1 - yi if d & 2 else yi
        pc = 1 - ci if d & 1 else ci
        for a in range(n_arr):
            copies.append(pltpu.make_async_remote_copy(
                src_ref=src_of(a, 4 * px + 2 * py + pc), dst_ref=out_refs[a].at[me],
                send_sem=send_sems.at[per * a + d - 1], recv_sem=recv_sems.at[per * a + d - 1],
                device_id=(px, py, pc), device_id_type=MESH))
    own = [pltpu.make_async_copy(src_of(a, me), out_refs[a].at[me], local_sems.at[a]) for a in range(n_arr)]
    return copies + own


def _exchange_shapes(scattered, shared):
    return ([jax.ShapeDtypeStruct(a.shape, a.dtype) for a in scattered]
            + [jax.ShapeDtypeStruct((N_DEV,) + a.shape, a.dtype) for a in shared])


def _exchange_sems(n_arr):
    sems = pltpu.SemaphoreType.DMA(((N_DEV - 1) * n_arr,))
    return [sems, sems, pltpu.SemaphoreType.DMA((n_arr,))]


def _exchange(scattered, shared, name):
    n_sc = len(scattered)
    n_arr = n_sc + len(shared)

    def body(*refs):
        copies = _exchange_copies(refs[:n_arr], refs[n_arr:2 * n_arr], n_sc, *refs[2 * n_arr:])
        for cp in copies:
            cp.start()
        for cp in copies:
            cp.wait()

    return pl.pallas_call(
        body, name=name, out_shape=_exchange_shapes(scattered, shared),
        in_specs=[ANY] * n_arr, out_specs=[ANY] * n_arr, scratch_shapes=_exchange_sems(n_arr),
    )(*scattered, *shared)


def _adam_math(g, w, m, v):
    m_new = ADAM_B1 * m + (1.0 - ADAM_B1) * g
    v_new = ADAM_B2 * v + (1.0 - ADAM_B2) * (g * g)
    m_hat = m_new / (1.0 - ADAM_B1 ** ADAM_STEP)
    v_hat = v_new / (1.0 - ADAM_B2 ** ADAM_STEP)
    return -ADAM_LR * (m_hat / (jnp.sqrt(v_hat) + ADAM_EPS) + ADAM_WD * w), m_new, v_new


def _slot_sum(p_ref):
    g = p_ref[0].astype(F32)
    for s in range(1, N_DEV):
        g = g + p_ref[s].astype(F32)
    return g


def _adamw_replicated(parts, ws, ms, vs, loss_parts, name):
    n = len(ws)

    def body(*refs):
        p_refs, w_refs, m_refs, v_refs = refs[:n], refs[n:2 * n], refs[2 * n:3 * n], refs[3 * n:4 * n]
        outs = refs[4 * n + 1:]
        for j in range(n):
            g = _slot_sum(p_refs[j])
            outs[4 * j][...] = g
            for o_ref, val in zip(outs[4 * j + 1:4 * j + 4], _adam_math(g, w_refs[j][...], m_refs[j][...],
                                                                        v_refs[j][...])):
                o_ref[...] = val
        outs[4 * n][...] = _slot_sum(refs[4 * n])

    whole = pl.BlockSpec(memory_space=pltpu.VMEM)
    out = pl.pallas_call(
        body, name=name, in_specs=[whole] * (4 * n + 1), out_specs=[whole] * (4 * n + 1),
        out_shape=[jax.ShapeDtypeStruct(w.shape, F32) for w in ws for _ in range(4)]
                  + [jax.ShapeDtypeStruct(loss_parts.shape[1:], F32)],
    )(*parts, *ws, *ms, *vs, loss_parts)
    return [out[4 * j:4 * j + 4] for j in range(n)], out[4 * n]


def _adamw(parts, w, m, v, name):
    rows, cols = w.shape[-2:]
    tile = PACK_ROWS if rows % PACK_ROWS == 0 else rows
    at = (0,) if w.ndim == 3 else (Ellipsis,)

    def body(p_ref, w_ref, m_ref, v_ref, g_out, d_out, m_out, v_out):
        g = _slot_sum(p_ref)
        g_out[at] = g
        d_out[at], m_out[at], v_out[at] = _adam_math(g, w_ref[at], m_ref[at], v_ref[at])

    spec = (pl.BlockSpec((1, tile, cols), lambda i: (0, i, 0)) if w.ndim == 3
            else pl.BlockSpec((tile, cols), lambda i: (i, 0)))
    return pl.pallas_call(
        body, name=name, grid=(rows // tile,),
        in_specs=[pl.BlockSpec((N_DEV, tile, cols), lambda i: (0, i, 0)), spec, spec, spec],
        out_specs=[spec] * 4, out_shape=[jax.ShapeDtypeStruct(w.shape, F32)] * 4,
        compiler_params=_cparams(("parallel",)),
    )(parts, w, m, v)


EARLY = [("meta_tokens", 1), ("rwkv_w2", 1), ("rwkv_a2", 1), ("rwkv_g2", 1)]
LATE = [("w_br_attn", 1), ("w_br_rwkv", 1), ("w_o", 0), ("w_ffn_gate", 1), ("w_ffn_up", 1), ("w_ffn_down", 0)]
REPLICATED = ["norm_mix_g", "b_in", "attn_sinks", "rwkv_mix", "rwkv_w0", "rwkv_a0", "rwkv_k_k", "rwkv_k_a",
              "rwkv_r_k", "rwkv_ln_w", "rwkv_ln_b", "norm_ffn_g", "norm_final_g"]
WEIGHTS = ["meta_tokens", "norm_mix_g", "w_in", "b_in", "attn_sinks", "rwkv_mix", "rwkv_w0", "rwkv_w2", "rwkv_a0",
           "rwkv_a2", "rwkv_g2", "rwkv_k_k", "rwkv_k_a", "rwkv_r_k", "rwkv_ln_w", "rwkv_ln_b", "w_br_attn",
           "w_br_rwkv", "w_o", "norm_ffn_g", "w_ffn_gate", "w_ffn_up", "w_ffn_down", "norm_final_g"]


def _strip(name, a):
    return a if name in ("meta_tokens", "norm_final_g") else a[0]


def _join(gathered, axis):
    if axis == 0:
        return gathered.reshape(-1, gathered.shape[2])
    return gathered.transpose(1, 0, 2).reshape(gathered.shape[1], -1)


def _split(g, axis):
    if axis == 0:
        return g.reshape(N_DEV, -1, g.shape[1])
    return g.reshape(g.shape[0], N_DEV, -1).transpose(1, 0, 2)


W_IN_LAYOUT = [(768, 2304), (0, 512), (2592, 4640), (2432, 2592), 256 - GATE_LORA, (512, 768), (2304, 2368),
               128 - DECAY_LORA, (2368, 2432), 128 - AAA_LORA, NP - C_DA - 128]


def _w_in_padded(w, shard_width=None):
    rows = w.shape[-2]
    width = D_IN if shard_width is None else shard_width
    parts = []
    for seg in W_IN_LAYOUT:
        if isinstance(seg, int):
            parts.append(jnp.zeros((rows, seg), w.dtype))
            continue
        lo, stop = seg
        while lo < stop:
            p = lo // width
            hi = min(stop, (p + 1) * width)
            src = w if shard_width is None else w[p]
            parts.append(src[:, lo - p * width:hi - p * width])
            lo = hi
    return jnp.concatenate(parts, axis=1)


def _w_in_unpadded(wp, lo=0, stop=D_IN):
    spans, pos = [], 0
    for seg in W_IN_LAYOUT:
        if isinstance(seg, int):
            pos += seg
        else:
            spans.append((seg[0], seg[1], pos))
            pos += seg[1] - seg[0]
    parts = []
    for a, b, at in sorted(spans):
        c, d = max(a, lo), min(b, stop)
        if c < d:
            parts.append(wp[:, at + c - a:at + d - a])
    return jnp.concatenate(parts, axis=1)


def _pad_rows(a, n):
    return jnp.pad(a, ((0, n - a.shape[0]), (0, 0)))


def _device_step(x, tgt, full, gather_late=None, scatter_early=None, scatter_last=None):
    seq = x.shape[0]
    nblk = seq // BLOCK
    lp = seq + BLOCK
    nall = nblk + 1

    w_in_p = full["w_in_p"]
    b_in_p = _w_in_padded(full["b_in"][None])
    mix = full["rwkv_mix"][None]
    mix_r, mix_k, mix_v = mix[:, 0:512], mix[:, 512:1024], mix[:, 1024:1536]
    mix_dw = jnp.pad(mix[:, 1536:1600], ((0, 0), (0, 64)))
    mix_da = jnp.pad(mix[:, 1600:1664], ((0, 0), (0, 64)))
    mix_dg = jnp.pad(mix[:, 1664:1824], ((0, 0), (0, 96)))
    w2_p = _pad_rows(full["rwkv_w2"].astype(F32), 128)
    a2_p = _pad_rows(full["rwkv_a2"].astype(F32), 128)
    g2_p = _pad_rows(full["rwkv_g2"].astype(F32), 256)
    row = lambda name: full[name].reshape(1, -1)
    sinks = row("attn_sinks")
    rope_c, rope_s1, rope_s2 = _rope_tables(lp)

    hpad = jnp.concatenate([jnp.zeros((PAD_ROWS, D_MODEL), F32), full["meta_tokens"].astype(F32), x], axis=0)
    (u,) = _rows_fwd(_rms_fn, [_view(hpad)], [row("norm_mix_g")], [D_MODEL], nblk=nall, name="norm_mix",
                     out_dtype=BF16, tile=TILE_ALL_BIG)
    proj = _mm(u, w_in_p, bias=b_in_p, name="in_proj")
    (q_r,) = _rows_fwd(_rope_fwd_fn, [_view(proj, 512, C_Q // 512), _view(rope_c), _view(rope_s1), _view(rope_s2)],
                       [], [512], nblk=nall, name="rope_q", tile=TILE_ALL_BIG)
    (k_r,) = _rows_fwd(_rope_fwd_fn, [_view(proj, 128, C_KA // 128), _view(rope_c), _view(rope_s1),
                                      _view(rope_s2)], [], [128], nblk=nall, name="rope_k", tile=TILE_ALL_BIG)

    rw_cols = jnp.concatenate([proj[:, C_R:C_R + 1536], proj[:, C_DG:C_DG + 256], proj[:, C_DW:C_DW + 256]], axis=1)
    rw_prev = jnp.pad(rw_cols[:-1], ((1, 0), (0, 0)))
    pre_rows = [_view(proj, 512, 0), _view(proj, 512, 1), _view(proj, 512, 2), _view(proj, 128, C_DW // 128),
                _view(proj, 128, C_DA // 128), _view(proj, 256, C_DG // 256),
                _view(rw_prev, 512, 0), _view(rw_prev, 512, 1), _view(rw_prev, 512, 2), _view(rw_prev, 128, 14),
                _view(rw_prev, 128, 15), _view(rw_prev, 256, 6)]
    pre_consts = [mix_r, mix_k, mix_v, mix_dw, mix_da, mix_dg, row("rwkv_w0"), w2_p, row("rwkv_a0"), a2_p, g2_p,
                  row("rwkv_k_k"), row("rwkv_k_a")]
    xt_block = (BLOCK // SCAN_T * VEC_ROWS, LANES)
    r_t, k_mod, v_t, gate, xt = _rows_fwd(_rwkv_pre_xt_fn, pre_rows, pre_consts, [RW_DIM] * 4 + [xt_block],
                                          nblk=nall, name="rwkv_pre")
    xt = xt.reshape(-1, VEC_ROWS, LANES)
    spread, collect = _selectors()
    y_scan, hist, o_attn, lse, *late = _wkv_fwd(xt, v_t, spread, (q_r, k_r, proj, C_VA // 128, sinks), name="wkv_fwd",
                                                shared=gather_late[0] if gather_late else ())
    if gather_late:
        full = {**full, **gather_late[1](late)}
    post_rows = [_view(y_scan, off=1), _view(r_t, off=1), _view(k_mod, off=1), _view(v_t, off=1), _view(gate, off=1)]
    post_consts = [row("rwkv_ln_w"), row("rwkv_ln_b"), row("rwkv_r_k")]
    (y_rwkv,) = _rows_fwd(_rwkv_post_fn, post_rows, post_consts, [RW_DIM], nblk=nblk, name="rwkv_post",
                          out_dtype=BF16)

    ya = _mm(o_attn, full["w_br_attn"], name="br_attn")
    yr = _mm(y_rwkv, full["w_br_rwkv"], name="br_rwkv")
    merge_rows = [_view(ya), _view(yr), _view(proj, 1024, C_G1 // 1024, 1), _view(proj, 1024, C_G2 // 1024, 1)]
    (merged,) = _rows_fwd(_merge_fn, merge_rows, [], [D_MODEL], nblk=nblk, name="merge", out_dtype=BF16)
    h1 = _mm(merged, full["w_o"], residual=x, name="out_proj")
    (f,) = _rows_fwd(_rms_fn, [_view(h1)], [row("norm_ffn_g")], [D_MODEL], nblk=nblk, name="norm_ffn",
                     out_dtype=BF16, tile=TILE_REAL)
    ff_gate = _mm(f, full["w_ffn_gate"], name="ffn_gate")
    ff_up = _mm(f, full["w_ffn_up"], name="ffn_up")
    (act,) = _rows_fwd(_swiglu_fn, [_view(ff_gate), _view(ff_up)], [], [D_FF], nblk=nblk, name="swiglu",
                       out_dtype=BF16, tile=TILE_WIDE)
    h2 = _mm(act, full["w_ffn_down"], residual=h1, name="ffn_down")

    grads = {}
    ones_col = jnp.ones((seq, 1), F32)
    loss_rows, dh2, grads["norm_final_g"] = _rows_bwd(
        _loss_fn, [_view(h2), _view(tgt)], [row("norm_final_g")], [_view(ones_col)], nblk=nblk, name="loss",
        diff_rows=[0], diff_consts=[0], fwd_widths=[1], tile=TILE_REAL)
    loss = jnp.sum(loss_rows)

    dact = _mm(dh2, full["w_ffn_down"], tb=True, name="d_act")
    grads["w_ffn_down"] = _mm(act, dh2, ta=True, name="dw_ffn_down")
    dgate, dup = _rows_bwd(_swiglu_fn, [_view(ff_gate), _view(ff_up)], [], [_view(dact)], nblk=nblk,
                           name="swiglu_bwd", diff_rows=[0, 1], diff_consts=[], row_dtype=BF16, tile=TILE_WIDE)
    grads["w_ffn_gate"] = _mm(f, dgate, ta=True, name="dw_ffn_gate")
    grads["w_ffn_up"] = _mm(f, dup, ta=True, name="dw_ffn_up")
    df = _mm(dgate, full["w_ffn_gate"], tb=True, name="df_gate")
    df = _mm(dup, full["w_ffn_up"], tb=True, residual=df, name="df_up")
    dh1, grads["norm_ffn_g"] = _rows_bwd(_rms_fn, [_view(h1)], [row("norm_ffn_g")], [_view(df)], nblk=nblk,
                                         name="norm_ffn_bwd", diff_rows=[0], diff_consts=[0], acc=[_view(dh2)],
                                         tile=TILE_REAL)
    dmerged = _mm(dh1, full["w_o"], tb=True, name="d_merged")
    grads["w_o"] = _mm(merged, dh1, ta=True, name="dw_o")
    dya, dyr, dg1, dg2 = _rows_bwd(_merge_fn, merge_rows, [], [_view(dmerged)], nblk=nblk, name="merge_bwd",
                                   diff_rows=[0, 1, 2, 3], diff_consts=[], row_dtype=BF16)
    grads["w_br_attn"] = _mm(o_attn, dya, ta=True, name="dw_br_attn")
    grads["w_br_rwkv"] = _mm(y_rwkv, dyr, ta=True, name="dw_br_rwkv")
    dy_attn = _mm(dya, full["w_br_attn"], tb=True, name="d_y_attn")
    dy_rwkv = _mm(dyr, full["w_br_rwkv"], tb=True, name="d_y_rwkv")

    post = _rows_bwd(_rwkv_post_fn, post_rows, post_consts, [_view(dy_rwkv)], nblk=nblk, name="rwkv_post_bwd",
                     diff_rows=[0, 1, 2, 3, 4], diff_consts=[0, 1, 2])
    dys, dr_post, dk_post, dv_post, dgate_post = post[:5]
    grads["rwkv_ln_w"], grads["rwkv_ln_b"], grads["rwkv_r_k"] = post[5:]
    dxt, dv_s, *early_parts = _wkv_bwd(xt, v_t, hist, dys, spread, collect, name="wkv_bwd",
                                       scattered=scatter_early(grads) if scatter_early else ())
    pre_cts = [_view(dxt.reshape(-1, LANES), rows=xt_block[0]), _view(dv_s)] + [
        _view(t, off=-1) for t in (dr_post, dk_post, dv_post, dgate_post)]
    pre = _rows_bwd(_rwkv_pre_fn, pre_rows, pre_consts, pre_cts, nblk=nall, name="rwkv_pre_bwd",
                    diff_rows=list(range(12)), diff_consts=list(range(13)), ct_map=_rwkv_pre_cts)
    d_cur, d_prev, d_par = pre[0:6], pre[6:12], pre[12:]
    up = lambda t: jnp.pad(t[1:], ((0, 1), (0, 0)))
    d_rw = [c + up(p) for c, p in zip(d_cur, d_prev)]
    grads["rwkv_mix"] = jnp.concatenate([d_par[0], d_par[1], d_par[2], d_par[3][:, :DECAY_LORA],
                                         d_par[4][:, :AAA_LORA], d_par[5][:, :GATE_LORA]], axis=1)
    grads["rwkv_w0"], grads["rwkv_w2"] = d_par[6], d_par[7][:DECAY_LORA]
    grads["rwkv_a0"], grads["rwkv_a2"] = d_par[8], d_par[9][:AAA_LORA]
    grads["rwkv_g2"] = d_par[10][:GATE_LORA]
    grads["rwkv_k_k"], grads["rwkv_k_a"] = d_par[11], d_par[12]

    dq_real, dk_r, dva, grads["attn_sinks"] = _attn_bwd(q_r, k_r, proj, C_VA // 128, sinks, o_attn, lse, dy_attn,
                                                        nblk=nblk, name="attn_bwd")
    dq_r = jnp.pad(dq_real, ((BLOCK, 0), (0, 0)))
    (dq,) = _rows_fwd(_rope_bwd_fn, [_view(dq_r), _view(rope_c), _view(rope_s1), _view(rope_s2)], [], [512],
                      nblk=nall, name="rope_q_bwd", out_dtype=BF16, tile=TILE_ALL_BIG)
    (dka,) = _rows_fwd(_rope_bwd_fn, [_view(dk_r), _view(rope_c), _view(rope_s1),
                                      _view(rope_s2)], [], [128], nblk=nall, name="rope_k_bwd", out_dtype=BF16,
                       tile=TILE_ALL_BIG)

    lead = lambda t: jnp.pad(t, ((BLOCK, 0), (0, 0)))
    pieces = [d_rw[0], d_rw[1], d_rw[2], dq, lead(dg1), lead(dg2), d_rw[5], dka, dva, d_rw[3], d_rw[4],
              jnp.zeros((lp, NP - C_DA - 128), BF16)]
    dproj = jnp.concatenate([p.astype(BF16) for p in pieces], axis=1)
    grads["w_in_p"] = _mm(u, dproj, ta=True, name="dw_in")
    grads["b_in"] = _w_in_unpadded(_colsum(dproj, name="db_in"))
    du, *last_parts = _mm(dproj, w_in_p, tb=True, name="d_u", scattered=scatter_last(grads)) if scatter_last else (
        _mm(dproj, w_in_p, tb=True, name="d_u"),)
    dh, grads["norm_mix_g"] = _rows_bwd(_rms_fn, [_view(hpad)], [row("norm_mix_g")], [_view(du)], nblk=nall,
                                        name="norm_mix_bwd", diff_rows=[0], diff_consts=[0], acc=[_view(lead(dh1))],
                                        tile=TILE_ALL)
    grads["meta_tokens"] = dh[PAD_ROWS:BLOCK]
    return loss, dh[BLOCK:], grads, early_parts, last_parts


def kernel(x, meta_tokens, norm_mix_g, w_in, b_in, attn_sinks, rwkv_mix, rwkv_w0, rwkv_w2, rwkv_a0, rwkv_a2, rwkv_g2, rwkv_k_k, rwkv_k_a, rwkv_r_k, rwkv_ln_w, rwkv_ln_b, w_br_attn, w_br_rwkv, w_o, norm_ffn_g, w_ffn_gate, w_ffn_up, w_ffn_down, norm_final_g, loss_target, m_meta_tokens, m_norm_mix_g, m_w_in, m_b_in, m_attn_sinks, m_rwkv_mix, m_rwkv_w0, m_rwkv_w2, m_rwkv_a0, m_rwkv_a2, m_rwkv_g2, m_rwkv_k_k, m_rwkv_k_a, m_rwkv_r_k, m_rwkv_ln_w, m_rwkv_ln_b, m_w_br_attn, m_w_br_rwkv, m_w_o, m_norm_ffn_g, m_w_ffn_gate, m_w_ffn_up, m_w_ffn_down, m_norm_final_g, v_meta_tokens, v_norm_mix_g, v_w_in, v_b_in, v_attn_sinks, v_rwkv_mix, v_rwkv_w0, v_rwkv_w2, v_rwkv_a0, v_rwkv_a2, v_rwkv_g2, v_rwkv_k_k, v_rwkv_k_a, v_rwkv_r_k, v_rwkv_ln_w, v_rwkv_ln_b, v_w_br_attn, v_w_br_rwkv, v_w_o, v_norm_ffn_g, v_w_ffn_gate, v_w_ffn_up, v_w_ffn_down, v_norm_final_g):
    given = dict(locals())
    wts = {n: _strip(n, given[n]) for n in WEIGHTS}
    as_rows = lambda a: a.reshape(1, -1) if a.ndim == 1 else a
    width = wts["w_in"].shape[1]
    wire = lambda table: [wts[n].astype(BF16) for n, _ in table]

    w_in_all, *early_all = _all_gather([wts["w_in"].astype(BF16)] + wire(EARLY), name="gather_weights")
    full = {n: wts[n] for n in REPLICATED}
    full.update({n: _join(g, axis) for (n, axis), g in zip(EARLY, early_all)})
    full["w_in_p"] = _w_in_padded(w_in_all, shard_width=width)
    gather_late = (wire(LATE), lambda got: {n: _join(g, axis) for (n, axis), g in zip(LATE, got)})
    scatter_early = lambda g: [_split(g[n], axis).astype(BF16) for n, axis in LATE]
    scatter_last = lambda g: [jnp.stack([_w_in_unpadded(g["w_in_p"], p * width, (p + 1) * width)
                                         for p in range(N_DEV)]).astype(BF16)]

    loss_part, grad_x, grads, parts_late, (parts_w_in,) = _device_step(
        x[0], loss_target[0], full, gather_late, scatter_early, scatter_last)

    g_early = [_split(grads[n], axis).astype(BF16) for n, axis in EARLY]
    g_small = [grads[n].reshape(as_rows(given[n]).shape) for n in REPLICATED] + [jnp.full((8, LANES), loss_part)]
    got = _exchange(g_early, g_small, name="exchange_grads")
    parts_early, parts_small, parts_loss = got[:len(EARLY)], got[len(EARLY):-1], got[-1]
    results = [{}, {}, {}, {}]
    for (n, _), parts in zip([("w_in", 1)] + EARLY + LATE, [parts_w_in] + list(parts_early) + list(parts_late)):
        for kind, a in enumerate(_adamw(parts, given[n], given["m_" + n], given["v_" + n], name="adamw_" + n)):
            results[kind][n] = a
    small, loss = _adamw_replicated(parts_small, *[[as_rows(given[pre + n]) for n in REPLICATED]
                                                   for pre in ("", "m_", "v_")], parts_loss, name="adamw_replicated")
    for n, four in zip(REPLICATED, small):
        for kind, a in enumerate(four):
            results[kind][n] = a
    loss = loss[0, 0]
    out = [loss, grad_x[None]]
    for kind in range(4):
        out += [results[kind][n].reshape(given[n].shape) for n in WEIGHTS]
    return tuple(out)
```

```python
import functools

import jax
import jax.numpy as jnp
from jax import lax
from jax.experimental import pallas as pl
from jax.experimental.pallas import tpu as pltpu

F32 = jnp.float32
BF16 = jnp.bfloat16

N_DEV = 8
D_MODEL = 1024
N_META = 16
BLOCK = 128
PAD_ROWS = BLOCK - N_META
HEAD_DIM = 64
Q_HEADS = 8
KV_HEADS = 2
GROUP = Q_HEADS // KV_HEADS
ROPE_DIM = HEAD_DIM // 4
ROPE_HALF = ROPE_DIM // 2
ROPE_THETA = 500000.0
RW_HEADS = 8
RW_DIM = 512
DECAY_LORA = 64
AAA_LORA = 64
GATE_LORA = 160
D_FF = 2816
D_IN = 4640
RMS_EPS = 1e-6
RWKV_LN_EPS = 64e-5
NEG_INF = -1e30
SCAN_T = 16
SCAN_CHUNKS = 8
LANES = 128
PACK_ROWS = 256
TILE_ALL = 384
TILE_ALL_BIG = 1408
TILE_REAL = 512
TILE_WIDE = 256

ADAM_LR = 0.001
ADAM_B1 = 0.9
ADAM_B2 = 0.999
ADAM_EPS = 1e-08
ADAM_WD = 0.01
ADAM_STEP = 10

C_G1, C_G2 = 0, 1024
C_R, C_K, C_V, C_DG, C_DW, C_DA = 2048, 2560, 3072, 3584, 3840, 3968
C_Q, C_KA, C_VA = 4096, 4608, 4736
NP = 5120

VMEM_LIMIT = 48 * 1024 * 1024
SCAN_VMEM_LIMIT = 60 * 1024 * 1024


def _cparams(sem, vmem=VMEM_LIMIT):
    return pltpu.CompilerParams(dimension_semantics=sem, vmem_limit_bytes=vmem)


def _pick(n, cands):
    for c in cands:
        if n % c == 0:
            return c
    raise ValueError(f"no tile for {n}")


def _mm(a, b, *, ta=False, tb=False, bias=None, residual=None, name, scattered=()):
    m = a.shape[1] if ta else a.shape[0]
    k = a.shape[0] if ta else a.shape[1]
    n = b.shape[0] if tb else b.shape[1]
    assert k == (b.shape[1] if tb else b.shape[0]), (a.shape, b.shape, ta, tb)
    tm = _pick(m, (512, 1408, 256, 128) if ta else (1056, 1024, 528, 512, 384, 256, 128))
    tn = _pick(n, (1024, 512, 1408, 256, 128))
    if k <= 1024:
        tk = k
    else:
        tk = _pick(k, (1024, 1056, 528, 512) if (ta and not tb) else (1024, 1408, 512, 256, 128))
    nk = k // tk
    has_bias = bias is not None
    has_res = residual is not None
    dn = (((0 if ta else 1,), (1 if tb else 0,)), ((), ()))

    def body(*refs):
        a_ref, b_ref = refs[0], refs[1]
        pos = 2
        bias_ref = res_ref = None
        if has_bias:
            bias_ref = refs[pos]
            pos += 1
        if has_res:
            res_ref = refs[pos]
            pos += 1
        o_ref, acc_ref = refs[pos], refs[pos + 1]
        kk = pl.program_id(2)
        part = lax.dot_general(a_ref[...].astype(BF16), b_ref[...].astype(BF16), dn, preferred_element_type=F32)

        def finish(out):
            if has_bias:
                out = out + bias_ref[...]
            if has_res:
                out = out + res_ref[...]
            o_ref[...] = out

        if nk == 1:
            finish(part)
        else:
            @pl.when(kk == 0)
            def _():
                acc_ref[...] = part

            @pl.when((kk > 0) & (kk < nk - 1))
            def _():
                acc_ref[...] += part

            @pl.when(kk == nk - 1)
            def _():
                finish(acc_ref[...] + part)

    in_specs = [
        pl.BlockSpec((tk, tm), lambda i, j, kk: (kk, i)) if ta else pl.BlockSpec((tm, tk), lambda i, j, kk: (i, kk)),
        pl.BlockSpec((tn, tk), lambda i, j, kk: (j, kk)) if tb else pl.BlockSpec((tk, tn), lambda i, j, kk: (kk, j)),
    ]
    args = [a, b]
    if has_bias:
        in_specs.append(pl.BlockSpec((1, tn), lambda i, j, kk: (0, j)))
        args.append(bias)
    if has_res:
        in_specs.append(pl.BlockSpec((tm, tn), lambda i, j, kk: (i, j)))
        args.append(residual)
    grid = (m // tm, n // tn, nk)
    n_x = len(scattered)
    out = pl.pallas_call(
        _with_exchange(body, len(args), 1, 1, scattered, (), grid), name=name, grid=grid,
        in_specs=in_specs + [ANY] * n_x,
        out_specs=[pl.BlockSpec((tm, tn), lambda i, j, kk: (i, j))] + [ANY] * n_x,
        out_shape=[jax.ShapeDtypeStruct((m, n), F32)] + _exchange_shapes(scattered, ()),
        scratch_shapes=[pltpu.VMEM((tm, tn) if nk > 1 else (8, LANES), F32)] + (_exchange_sems(n_x) if n_x else []),
        compiler_params=_cparams(("arbitrary",) * 3 if n_x else ("parallel", "parallel", "arbitrary")),
    )(*args, *scattered)
    return out if n_x else out[0]


def _colsum(x, name):
    m, n = x.shape
    tm = _pick(m, (1408, 512, 384, BLOCK))

    def body(x_ref, o_ref):
        i = pl.program_id(0)
        s = jnp.sum(x_ref[...].astype(F32), axis=0, keepdims=True)

        @pl.when(i == 0)
        def _():
            o_ref[...] = s

        @pl.when(i > 0)
        def _():
            o_ref[...] += s

    return pl.pallas_call(
        body, name=name, grid=(m // tm,),
        in_specs=[pl.BlockSpec((tm, n), lambda i: (i, 0))],
        out_specs=pl.BlockSpec((1, n), lambda i: (0, 0)),
        out_shape=jax.ShapeDtypeStruct((1, n), F32),
        compiler_params=_cparams(("arbitrary",)),
    )(x)


def _view(arr, width=None, col=0, off=0, rows=BLOCK):
    return (arr, arr.shape[1] if width is None else width, col, off, rows)


def _row_spec(view):
    _, width, col, off, rows = view
    if off < 0:
        return pl.BlockSpec((rows, width), lambda i, col=col, off=off: (jnp.maximum(i + off, 0), col))
    return pl.BlockSpec((rows, width), lambda i, col=col, off=off: (i + off, col))


def _const_spec(arr):
    return pl.BlockSpec(arr.shape, lambda i: (0,) * arr.ndim)


def _retile(views, tile):
    assert all(v[3] == 0 and v[4] == BLOCK for v in views)
    return [v[:4] + (tile,) for v in views]


def _rows_fwd(fn, rows, consts, out_widths, *, nblk, name, out_dtype=F32, tile=BLOCK):
    nr, nc = len(rows), len(consts)
    tile = tile if (nblk * BLOCK) % tile == 0 else BLOCK
    if tile != BLOCK:
        rows, nblk = _retile(rows, tile), nblk * BLOCK // tile
    out_blocks = [(tile, w) if isinstance(w, int) else w for w in out_widths]

    def body(*refs):
        i = pl.program_id(0)
        vals = [r[...] for r in refs[:nr + nc]]
        outs = fn(i, *vals)
        for o_ref, o in zip(refs[nr + nc:], outs):
            o_ref[...] = o.astype(o_ref.dtype)

    return pl.pallas_call(
        body, name=name, grid=(nblk,),
        in_specs=[_row_spec(v) for v in rows] + [_const_spec(c) for c in consts],
        out_specs=[pl.BlockSpec(b, lambda i: (i, 0)) for b in out_blocks],
        out_shape=[jax.ShapeDtypeStruct((nblk * r, w), out_dtype) for r, w in out_blocks],
        compiler_params=_cparams(("parallel",)),
    )(*[v[0] for v in rows], *consts)


def _rows_bwd(fn, rows, consts, cts, *, nblk, name, diff_rows, diff_consts, acc=None, fwd_widths=(), row_dtype=F32,
              ct_map=None, tile=BLOCK):
    nr, nc = len(rows), len(consts)
    acc = acc or [None] * len(diff_rows)
    tile = tile if (nblk * BLOCK) % tile == 0 else BLOCK
    if tile != BLOCK:
        rows, nblk = _retile(rows, tile), nblk * BLOCK // tile
        cts = [c if c is None else _retile([c], tile)[0] for c in cts]
        acc = [a if a is None else _retile([a], tile)[0] for a in acc]
    ct_views = [c for c in cts if c is not None]
    acc_views = [a for a in acc if a is not None]
    n_in = nr + nc + len(ct_views) + len(acc_views)
    n_fwd = len(fwd_widths)

    def body(*refs):
        i = pl.program_id(0)
        row_vals = [r[...] for r in refs[:nr]]
        const_vals = [r[...] for r in refs[nr:nr + nc]]
        ct_vals = [r[...] for r in refs[nr + nc:nr + nc + len(ct_views)]]
        acc_vals = [r[...] for r in refs[nr + nc + len(ct_views):n_in]]
        out_refs = refs[n_in:]

        def f(*dargs):
            rv = list(row_vals)
            cv = list(const_vals)
            for pos, idx in enumerate(diff_rows):
                rv[idx] = dargs[pos]
            for pos, idx in enumerate(diff_consts):
                cv[idx] = dargs[len(diff_rows) + pos]
            return tuple(fn(i, *rv, *cv))

        primals = [row_vals[idx] for idx in diff_rows] + [const_vals[idx] for idx in diff_consts]
        outs, pull = jax.vjp(f, *primals)
        full_ct, ci = [], 0
        if ct_map is not None:
            full_ct = ct_map(i, *ct_vals)
        else:
            for o, c in zip(outs, cts):
                if c is None:
                    full_ct.append(jnp.zeros_like(o))
                else:
                    full_ct.append(ct_vals[ci])
                    ci += 1
        grads = pull(tuple(full_ct))
        for o_ref, o in zip(out_refs[:n_fwd], outs):
            o_ref[...] = o
        ai = 0
        for pos in range(len(diff_rows)):
            g = grads[pos]
            if acc[pos] is not None:
                g = g + acc_vals[ai]
                ai += 1
            out_refs[n_fwd + pos][...] = g.astype(row_dtype)
        for pos in range(len(diff_consts)):
            g = grads[len(diff_rows) + pos]
            o_ref = out_refs[n_fwd + len(diff_rows) + pos]

            @pl.when(i == 0)
            def _(o_ref=o_ref, g=g):
                o_ref[...] = g

            @pl.when(i > 0)
            def _(o_ref=o_ref, g=g):
                o_ref[...] += g

    out_specs = [pl.BlockSpec((tile, w), lambda i: (i, 0)) for w in fwd_widths]
    out_shape = [jax.ShapeDtypeStruct((nblk * tile, w), F32) for w in fwd_widths]
    for idx in diff_rows:
        out_specs.append(pl.BlockSpec((tile, rows[idx][1]), lambda i: (i, 0)))
        out_shape.append(jax.ShapeDtypeStruct((nblk * tile, rows[idx][1]), row_dtype))
    for idx in diff_consts:
        out_specs.append(_const_spec(consts[idx]))
        out_shape.append(jax.ShapeDtypeStruct(consts[idx].shape, F32))
    return pl.pallas_call(
        body, name=name, grid=(nblk,),
        in_specs=([_row_spec(v) for v in rows] + [_const_spec(c) for c in consts]
                  + [_row_spec(v) for v in ct_views] + [_row_spec(v) for v in acc_views]),
        out_specs=out_specs, out_shape=out_shape,
        compiler_params=_cparams(("arbitrary",)),
    )(*[v[0] for v in rows], *consts, *[v[0] for v in ct_views], *[v[0] for v in acc_views])


def _rms_fn(i, x, g):
    return (x * lax.rsqrt(jnp.mean(x * x, axis=-1, keepdims=True) + RMS_EPS) * g,)


def _sigmoid(x):
    return 1.0 / (1.0 + jnp.exp(-x))


def _softplus(x):
    return jnp.maximum(x, 0.0) + jnp.log(1.0 + jnp.exp(-jnp.abs(x)))


def _split2(x):
    hi = x.astype(BF16)
    lo = (x - hi.astype(F32)).astype(BF16)
    return jnp.concatenate([hi, lo], axis=1)


@jax.custom_vjp
def _head_sum(x):
    r = lax.broadcasted_iota(jnp.int32, (2 * RW_DIM, RW_DIM), 0) % RW_DIM // HEAD_DIM
    c = lax.broadcasted_iota(jnp.int32, (2 * RW_DIM, RW_DIM), 1) // HEAD_DIM
    return jnp.dot(_split2(x), (r == c).astype(BF16), preferred_element_type=F32)


_head_sum.defvjp(lambda x: (_head_sum(x), None), lambda _, ct: (_head_sum(ct),))


@jax.custom_vjp
def _dot_bf16(x, w):
    return jnp.dot(x.astype(BF16), w.astype(BF16), preferred_element_type=F32)


def _dot_bf16_bwd(res, ct):
    x, w = res
    ct = ct.astype(BF16)
    dx = lax.dot_general(ct, w.astype(BF16), (((1,), (1,)), ((), ())), preferred_element_type=F32)
    dw = lax.dot_general(x.astype(BF16), ct, (((0,), (0,)), ((), ())), preferred_element_type=F32)
    return dx, dw


_dot_bf16.defvjp(lambda x, w: (_dot_bf16(x, w), (x, w)), _dot_bf16_bwd)


def _rwkv_pre_fn(i, r, k, v, dw, da, dg, r_p, k_p, v_p, dw_p, da_p, dg_p,
                 mix_r, mix_k, mix_v, mix_dw, mix_da, mix_dg, w0, w2, a0, a2, g2, k_k, k_a):
    row = i * BLOCK + lax.broadcasted_iota(jnp.int32, (BLOCK, 1), 0)
    live = row >= PAD_ROWS
    live_prev = row >= PAD_ROWS + 1

    def shift(cur, prev, mix):
        cur = jnp.where(live, cur, 0.0)
        prev = jnp.where(live_prev, prev, 0.0)
        return cur + (prev - cur) * mix

    r = shift(r, r_p, mix_r)
    k = shift(k, k_p, mix_k)
    v = shift(v, v_p, mix_v)
    dw = shift(dw, dw_p, mix_dw)
    da = shift(da, da_p, mix_da)
    dg = shift(dg, dg_p, mix_dg)
    wlog = -_softplus(-(w0 + _dot_bf16(jnp.tanh(dw), w2))) - 0.5
    decay = jnp.exp(-jnp.exp(wlog))
    a = _sigmoid(a0 + _dot_bf16(da, a2))
    g = _dot_bf16(_sigmoid(dg), g2)
    kk = k * k_k
    norm_sq = jnp.where(live, _head_sum(kk * kk), 1.0)
    kk = kk / jnp.maximum(jnp.sqrt(norm_sq), 1e-12)
    k_mod = k * (1.0 + (a - 1.0) * k_a)
    return r, decay, k_mod, v, -kk, kk * a, g


def _rwkv_pre_xt_fn(i, *args):
    r, decay, k_mod, v, a_neg, b, g = _rwkv_pre_fn(i, *args)
    t = SCAN_T
    xt = jnp.concatenate([_rows_to_xt(x[c * t:(c + 1) * t]) for c in range(BLOCK // t)
                          for x in (a_neg, decay, b, k_mod, r)], axis=0)
    return r, k_mod, v, g, xt


def _rwkv_pre_cts(i, dxt, dv_s, dr_p, dk_p, dv_p, dg_p):
    t = SCAN_T
    d_a, d_w, d_b, d_k, d_r = [
        jnp.concatenate([_xt_to_rows(dxt[c * VEC_ROWS + n * HEAD_DIM:c * VEC_ROWS + (n + 1) * HEAD_DIM])
                         for c in range(BLOCK // t)], axis=0) for n in range(N_VEC)]
    dr_p, dk_p, dv_p, dg_p = [jnp.where(i > 0, x, 0.0) for x in (dr_p, dk_p, dv_p, dg_p)]
    return d_r + dr_p, d_w, d_k + dk_p, dv_s + dv_p, d_a, d_b, dg_p


def _rwkv_post_fn(i, ys, r, k_mod, v, g, ln_w, ln_b, r_k):
    mean = _head_sum(ys) * (1.0 / HEAD_DIM)
    d = ys - mean
    var = _head_sum(d * d) * (1.0 / HEAD_DIM)
    yn = d * lax.rsqrt(var + RWKV_LN_EPS) * ln_w + ln_b
    bonus = _head_sum(r * k_mod * r_k) * v
    return ((yn + bonus) * g,)


def _merge_fn(i, ya, yr, g1, g2):
    return (_sigmoid(g1) * ya + _sigmoid(g2) * yr,)


def _swiglu_fn(i, gate, up):
    return (gate * _sigmoid(gate) * up,)


def _loss_fn(i, h, tgt, g):
    y = h * lax.rsqrt(jnp.mean(h * h, axis=-1, keepdims=True) + RMS_EPS) * g
    err = y - tgt
    return (0.5 * jnp.mean(err * err, axis=-1, keepdims=True),)


def _rope_tables(lp):
    pos = (jnp.arange(lp, dtype=jnp.int32) - PAD_ROWS).astype(F32)
    inv_freq = jnp.power(jnp.float32(ROPE_THETA), -jnp.arange(ROPE_HALF, dtype=F32) * (2.0 / ROPE_DIM))
    ang = pos[:, None] * inv_freq[None, :]
    cos, sin = jnp.cos(ang), jnp.sin(ang)
    one = jnp.ones((lp, HEAD_DIM - ROPE_DIM), F32)
    zero_h = jnp.zeros((lp, ROPE_HALF), F32)
    zero_r = jnp.zeros((lp, HEAD_DIM - ROPE_DIM), F32)
    c = jnp.concatenate([cos, cos, one], axis=1)
    s1 = jnp.concatenate([-sin, zero_h, zero_r], axis=1)
    s2 = jnp.concatenate([zero_h, sin, zero_r], axis=1)
    return tuple(jnp.tile(t, (1, LANES // HEAD_DIM)) for t in (c, s1, s2))


def _rope_fwd_fn(i, x, c, s1, s2):
    n = x.shape[1]
    c, s1, s2 = [jnp.tile(t, (1, n // LANES)) for t in (c, s1, s2)]
    return (x * c + pltpu.roll(x, n - ROPE_HALF, 1) * s1 + pltpu.roll(x, ROPE_HALF, 1) * s2,)


def _rope_bwd_fn(i, dy, c, s1, s2):
    n = dy.shape[1]
    c, s1, s2 = [jnp.tile(t, (1, n // LANES)) for t in (c, s1, s2)]
    return (dy * c + pltpu.roll(dy * s1, ROPE_HALF, 1) + pltpu.roll(dy * s2, n - ROPE_HALF, 1),)


def _attn_mask(i):
    r = lax.broadcasted_iota(jnp.int32, (BLOCK, 3 * BLOCK), 0)
    c = lax.broadcasted_iota(jnp.int32, (BLOCK, 3 * BLOCK), 1)
    meta = (c < BLOCK) & (c >= PAD_ROWS)
    prev = (c >= BLOCK) & (c < 2 * BLOCK) & ((c - BLOCK) > r) & (i >= 1)
    cur = (c >= 2 * BLOCK) & ((c - 2 * BLOCK) <= r)
    return meta | prev | cur


def _attn_rows(ref, g):
    return ref[:, g * HEAD_DIM:(g + 1) * HEAD_DIM]


def _attn_group(i, g, q_all, k_refs, v_refs, s_ref):
    heads = range(g * GROUP, (g + 1) * GROUP)
    kcat = jnp.concatenate([_attn_rows(r, g) for r in k_refs], axis=0).astype(BF16)
    vcat = jnp.concatenate([_attn_rows(r, g) for r in v_refs], axis=0).astype(BF16)
    qg = jnp.concatenate([q_all[:, h * HEAD_DIM:(h + 1) * HEAD_DIM] for h in heads], axis=0).astype(BF16)
    sink = jnp.concatenate([jnp.broadcast_to(s_ref[0:1, h:h + 1], (BLOCK, 1)) for h in heads], axis=0)
    s = lax.dot_general(qg, kcat, (((1,), (1,)), ((), ())), preferred_element_type=F32) * (HEAD_DIM ** -0.5)
    valid = jnp.concatenate([_attn_mask(i)] * GROUP, axis=0)
    return heads, qg, kcat, vcat, sink, jnp.where(valid, s, NEG_INF)


ATTN_SUB = 2


def _attn_specs(v_col):
    q = [pl.BlockSpec((BLOCK, Q_HEADS * HEAD_DIM), lambda i, n=n: (ATTN_SUB * i + 1 + n, 0)) for n in range(ATTN_SUB)]
    blk = lambda col: ([pl.BlockSpec((BLOCK, 2 * HEAD_DIM), lambda i: (0, col))]
                       + [pl.BlockSpec((BLOCK, 2 * HEAD_DIM), lambda i, n=n: (ATTN_SUB * i + n, col))
                          for n in range(ATTN_SUB + 1)])
    return q + blk(0) + blk(v_col) + [pl.BlockSpec((1, Q_HEADS), lambda i: (0, 0))]


def _attn_split(refs):
    n = ATTN_SUB
    q_refs, k_refs, v_refs = refs[:n], refs[n:2 * n + 2], refs[2 * n + 2:3 * n + 4]
    return q_refs, k_refs, v_refs, refs[3 * n + 4], refs[3 * n + 5:]


def _attn_fwd_block(blk, q_ref, keys, vals, s_ref, o_ref, lse_ref):
    q_all = q_ref[...]
    for g in range(KV_HEADS):
        heads, _, _, vcat, sink, s = _attn_group(blk, g, q_all, keys, vals, s_ref)
        m = jnp.maximum(jnp.max(s, axis=-1, keepdims=True), sink)
        p = jnp.exp(s - m)
        den = jnp.sum(p, axis=-1, keepdims=True) + jnp.exp(sink - m)
        o = jnp.dot(p.astype(BF16), vcat, preferred_element_type=F32) / den
        lse = m + jnp.log(den)
        for n, h in enumerate(heads):
            o_ref[:, h * HEAD_DIM:(h + 1) * HEAD_DIM] = o[n * BLOCK:(n + 1) * BLOCK]
            lse_ref[:, h:h + 1] = lse[n * BLOCK:(n + 1) * BLOCK]


def _attn_bwd(q, k, v, v_col, sinks, o, lse, do, *, nblk, name):
    lp = k.shape[0]
    rows = ATTN_SUB * BLOCK

    def body(*refs):
        q_refs, k_refs, v_refs, s_ref, (o_ref, lse_ref, do_ref, dq_ref, dk_ref, dv_ref, ds_ref) = _attn_split(refs)
        i = pl.program_id(0)

        @pl.when(i == 0)
        def _():
            dk_ref[...] = jnp.zeros_like(dk_ref)
            dv_ref[...] = jnp.zeros_like(dv_ref)
            ds_ref[...] = jnp.zeros_like(ds_ref)

        lane = lax.broadcasted_iota(jnp.int32, (1, Q_HEADS), 1)
        for sub in range(ATTN_SUB):
            at = slice(sub * BLOCK, (sub + 1) * BLOCK)
            blk = ATTN_SUB * i + sub
            keys = (k_refs[0], k_refs[1 + sub], k_refs[2 + sub])
            vals = (v_refs[0], v_refs[1 + sub], v_refs[2 + sub])
            prev_rows = pl.ds(pl.multiple_of(blk * BLOCK, BLOCK), BLOCK)
            cur_rows = pl.ds(pl.multiple_of((blk + 1) * BLOCK, BLOCK), BLOCK)
            q_all, o_all, do_all, lse_all = q_refs[sub][...], o_ref[at, :], do_ref[at, :], lse_ref[at, :]
            for g in range(KV_HEADS):
                heads, qg, kcat, vcat, sink, s = _attn_group(blk, g, q_all, keys, vals, s_ref)
                stack = lambda x: jnp.concatenate([x[:, h * HEAD_DIM:(h + 1) * HEAD_DIM] for h in heads], axis=0)
                lse_g = jnp.concatenate([lse_all[:, h:h + 1] for h in heads], axis=0)
                do_g = stack(do_all)
                p = jnp.exp(s - lse_g)
                delta = jnp.sum(do_g * stack(o_all), axis=-1, keepdims=True)
                dp = lax.dot_general(do_g.astype(BF16), vcat, (((1,), (1,)), ((), ())), preferred_element_type=F32)
                dsc = (p * (dp - delta) * (HEAD_DIM ** -0.5)).astype(BF16)
                dq = jnp.dot(dsc, kcat, preferred_element_type=F32)
                dk_all = lax.dot_general(dsc, qg, (((0,), (0,)), ((), ())), preferred_element_type=F32)
                dv_all = lax.dot_general(p.astype(BF16), do_g.astype(BF16), (((0,), (0,)), ((), ())),
                                         preferred_element_type=F32)
                cols = slice(g * HEAD_DIM, (g + 1) * HEAD_DIM)
                for ref, full in ((dk_ref, dk_all), (dv_ref, dv_all)):
                    ref[0:BLOCK, cols] += full[0:BLOCK]
                    ref[prev_rows, cols] += full[BLOCK:2 * BLOCK]
                    ref[cur_rows, cols] += full[2 * BLOCK:]
                sink_part = jnp.exp(sink - lse_g) * delta
                for n, h in enumerate(heads):
                    dq_ref[at, h * HEAD_DIM:(h + 1) * HEAD_DIM] = dq[n * BLOCK:(n + 1) * BLOCK]
                    dsink = -jnp.sum(sink_part[n * BLOCK:(n + 1) * BLOCK], axis=0, keepdims=True)
                    ds_ref[...] += jnp.where(lane == h, dsink, 0.0)

    qspec = pl.BlockSpec((rows, Q_HEADS * HEAD_DIM), lambda i: (i, 0))
    whole = pl.BlockSpec((lp, 2 * HEAD_DIM), lambda i: (0, 0))
    return pl.pallas_call(
        body, name=name, grid=(nblk // ATTN_SUB,),
        in_specs=_attn_specs(v_col) + [qspec, pl.BlockSpec((rows, Q_HEADS), lambda i: (i, 0)), qspec],
        out_specs=[qspec, whole, whole, pl.BlockSpec((1, Q_HEADS), lambda i: (0, 0))],
        out_shape=[jax.ShapeDtypeStruct((nblk * BLOCK, Q_HEADS * HEAD_DIM), F32),
                   jax.ShapeDtypeStruct((lp, 2 * HEAD_DIM), F32), jax.ShapeDtypeStruct((lp, 2 * HEAD_DIM), F32),
                   jax.ShapeDtypeStruct((1, Q_HEADS), F32)],
        compiler_params=_cparams(("arbitrary",)),
    )(*[q] * ATTN_SUB, *[k] * (ATTN_SUB + 2), *[v] * (ATTN_SUB + 2), sinks, o, lse, do)


N_VEC = 5
VEC_ROWS = N_VEC * HEAD_DIM


def _selectors():
    t = SCAN_T
    shape = (t, 2 * LANES, RW_DIM)
    step, src, dst = [lax.broadcasted_iota(jnp.int32, shape, d) for d in range(3)]
    src = src % LANES
    spread = ((src // t == dst // HEAD_DIM) & (src % t == step)).astype(BF16)
    shape = (t, RW_DIM, LANES)
    step, src, dst = [lax.broadcasted_iota(jnp.int32, shape, d) for d in range(3)]
    collect = ((src // HEAD_DIM == dst // t) & (dst % t == step)).astype(BF16)
    return spread, collect


def _rows_to_xt(x):
    low = lax.broadcasted_iota(jnp.int32, (SCAN_T, LANES), 1) < HEAD_DIM
    pieces = []
    for m in range(RW_HEADS // 2):
        pair = x[:, m * LANES:(m + 1) * LANES]
        pieces += [jnp.where(low, pair, 0.0), jnp.where(low, pltpu.roll(pair, HEAD_DIM, 1), 0.0)]
    return jnp.concatenate(pieces, axis=0).T[:HEAD_DIM]


def _xt_to_rows(a):
    t = SCAN_T
    a_t = jnp.concatenate([a, jnp.zeros_like(a)], axis=0).T
    pairs = [a_t[2 * m * t:(2 * m + 1) * t] + pltpu.roll(a_t[(2 * m + 1) * t:(2 * m + 2) * t], HEAD_DIM, 1)
             for m in range(RW_HEADS // 2)]
    return jnp.concatenate(pairs, axis=1)


def _with_exchange(compute, n_in, n_out, n_scratch, scattered, shared, grid):
    n_sc = len(scattered)
    n_x = n_sc + len(shared)
    if n_x == 0:
        return compute

    def body(*refs):
        ins, x_in = refs[:n_in], refs[n_in:n_in + n_x]
        outs, x_out = refs[n_in + n_x:n_in + n_x + n_out], refs[n_in + n_x + n_out:n_in + 2 * n_x + n_out]
        scratch = refs[n_in + 2 * n_x + n_out:n_in + 2 * n_x + n_out + n_scratch]
        sems = refs[n_in + 2 * n_x + n_out + n_scratch:]

        first = last = True
        for d, size in enumerate(grid):
            first = first & (pl.program_id(d) == 0)
            last = last & (pl.program_id(d) == size - 1)

        @pl.when(first)
        def _():
            for cp in _exchange_copies(x_in, x_out, n_sc, *sems):
                cp.start()

        compute(*ins, *outs, *scratch)

        @pl.when(last)
        def _():
            for cp in _exchange_copies(x_in, x_out, n_sc, *sems):
                cp.wait()

    return body


def _wkv_fwd(xt, v, spread, attn, name, shared=()):
    t_steps = SCAN_T
    nch = xt.shape[0]
    per = SCAN_CHUNKS
    grid = (nch // per,)
    rows = per * t_steps
    assert rows == BLOCK
    n_x = len(shared)
    q, k, v_arr, v_col, sinks = attn

    def compute(xt_ref, v_ref, sel_ref, q_ref, km_ref, kp_ref, kc_ref, vm_ref, vp_ref, vc_ref, s_ref,
                y_ref, hist_ref, o_ref, lse_ref, st_ref):
        @pl.when(pl.program_id(0) == 0)
        def _():
            st_ref[...] = jnp.zeros_like(st_ref)

        _attn_fwd_block(pl.program_id(0) - 1, q_ref, (km_ref, kp_ref, kc_ref), (vm_ref, vp_ref, vc_ref), s_ref,
                        o_ref, lse_ref)
        st = st_ref[...]
        for c in range(per):
            x2 = _split2(xt_ref[c])
            for j in range(t_steps):
                row = c * t_steps + j
                cols = jnp.dot(x2, sel_ref[j], preferred_element_type=F32)
                a_c, w_c, b_c, k_c, r_c = [cols[n * HEAD_DIM:(n + 1) * HEAD_DIM] for n in range(N_VEC)]
                hist_ref[row] = st
                sa = jnp.sum(st * a_c, axis=0, keepdims=True)
                st = st * w_c + b_c * sa + k_c * v_ref[row:row + 1, :]
                y_ref[row:row + 1, :] = jnp.sum(st * r_c, axis=0, keepdims=True)
        st_ref[...] = st

    before = lambda c: jnp.maximum(c - 1, 0)
    kv = lambda col: [pl.BlockSpec((BLOCK, 2 * HEAD_DIM), lambda c: (0, col)),
                      pl.BlockSpec((BLOCK, 2 * HEAD_DIM), lambda c: (before(c), col)),
                      pl.BlockSpec((BLOCK, 2 * HEAD_DIM), lambda c: (c, col))]
    seq = (nch * t_steps) - BLOCK
    return pl.pallas_call(
        _with_exchange(compute, 11, 4, 1, (), shared, grid), name=name, grid=grid,
        in_specs=[pl.BlockSpec((per, VEC_ROWS, LANES), lambda c: (c, 0, 0)),
                  pl.BlockSpec((rows, RW_DIM), lambda c: (c, 0)),
                  pl.BlockSpec(spread.shape, lambda c: (0, 0, 0)),
                  pl.BlockSpec((BLOCK, Q_HEADS * HEAD_DIM), lambda c: (c, 0))] + kv(0) + kv(v_col)
                 + [pl.BlockSpec((1, Q_HEADS), lambda c: (0, 0))] + [ANY] * n_x,
        out_specs=[pl.BlockSpec((rows, RW_DIM), lambda c: (c, 0)),
                   pl.BlockSpec((rows, HEAD_DIM, RW_DIM), lambda c: (c, 0, 0)),
                   pl.BlockSpec((BLOCK, Q_HEADS * HEAD_DIM), lambda c: (before(c), 0)),
                   pl.BlockSpec((BLOCK, Q_HEADS), lambda c: (before(c), 0))] + [ANY] * n_x,
        out_shape=[jax.ShapeDtypeStruct((nch * t_steps, RW_DIM), F32),
                   jax.ShapeDtypeStruct((nch * t_steps, HEAD_DIM, RW_DIM), F32),
                   jax.ShapeDtypeStruct((seq, Q_HEADS * HEAD_DIM), F32),
                   jax.ShapeDtypeStruct((seq, Q_HEADS), F32)] + _exchange_shapes((), shared),
        scratch_shapes=[pltpu.VMEM((HEAD_DIM, RW_DIM), F32)] + (_exchange_sems(n_x) if n_x else []),
        compiler_params=_cparams(("arbitrary",), SCAN_VMEM_LIMIT),
    )(xt, v, spread, q, k, k, k, v_arr, v_arr, v_arr, sinks, *shared)


def _wkv_bwd(xt, v, hist, dy, spread, collect, name, scattered=()):
    t_steps = SCAN_T
    nch = xt.shape[0]
    n_x = len(scattered)
    per = SCAN_CHUNKS
    nsteps = nch // per
    grid = (nsteps,)
    rows = per * t_steps
    lead = BLOCK // rows

    def compute(xt_ref, v_ref, hist_ref, dy_ref, sel_ref, col_ref, dxt_ref, dv_ref, g_ref):
        @pl.when(pl.program_id(0) == 0)
        def _():
            g_ref[...] = jnp.zeros_like(g_ref)

        has_dy = nsteps - 1 - pl.program_id(0) >= lead
        gst = g_ref[...]
        nxt = None
        for c in reversed(range(per)):
            x2 = _split2(xt_ref[c])
            acc = jnp.zeros((VEC_ROWS, LANES), F32)
            for j in reversed(range(t_steps)):
                row = c * t_steps + j
                cols = jnp.dot(x2, sel_ref[j], preferred_element_type=F32)
                a_c, w_c, b_c, k_c, r_c = [cols[n * HEAD_DIM:(n + 1) * HEAD_DIM] for n in range(N_VEC)]
                prev = hist_ref[row]
                v_row = v_ref[row:row + 1, :]
                dy_row = jnp.where(has_dy, dy_ref[row:row + 1, :], 0.0)
                sa = jnp.sum(prev * a_c, axis=0, keepdims=True)
                if nxt is None:
                    nxt = prev * w_c + b_c * sa + k_c * v_row
                gst = gst + r_c * dy_row
                dv_ref[row:row + 1, :] = jnp.sum(gst * k_c, axis=0, keepdims=True)
                dsa = jnp.sum(gst * b_c, axis=0, keepdims=True)
                prods = jnp.concatenate([p.astype(BF16) for p in
                                         (prev * dsa, gst * prev, gst * sa, gst * v_row, nxt * dy_row)], axis=0)
                acc = acc + jnp.dot(prods, col_ref[j], preferred_element_type=F32)
                gst = gst * w_c + a_c * dsa
                nxt = prev
            dxt_ref[c] = acc
        g_ref[...] = gst

    rev3 = lambda c: (nsteps - 1 - c, 0, 0)
    rev2 = lambda c: (nsteps - 1 - c, 0)
    rowspec = pl.BlockSpec((rows, RW_DIM), rev2)
    return pl.pallas_call(
        _with_exchange(compute, 6, 2, 1, scattered, (), grid), name=name, grid=grid,
        in_specs=[pl.BlockSpec((per, VEC_ROWS, LANES), rev3), rowspec,
                  pl.BlockSpec((rows, HEAD_DIM, RW_DIM), rev3),
                  pl.BlockSpec((rows, RW_DIM), lambda c: (jnp.maximum(nsteps - 1 - c - lead, 0), 0)),
                  pl.BlockSpec(spread.shape, lambda c: (0, 0, 0)),
                  pl.BlockSpec(collect.shape, lambda c: (0, 0, 0))] + [ANY] * n_x,
        out_specs=[pl.BlockSpec((per, VEC_ROWS, LANES), rev3), rowspec] + [ANY] * n_x,
        out_shape=[jax.ShapeDtypeStruct((nch, VEC_ROWS, LANES), F32),
                   jax.ShapeDtypeStruct((nch * t_steps, RW_DIM), F32)] + _exchange_shapes(scattered, ()),
        scratch_shapes=[pltpu.VMEM((HEAD_DIM, RW_DIM), F32)] + (_exchange_sems(n_x) if n_x else []),
        compiler_params=_cparams(("arbitrary",), SCAN_VMEM_LIMIT),
    )(xt, v, hist, dy, spread, collect, *scattered)


MESH = pl.DeviceIdType.MESH
ANY = pl.BlockSpec(memory_space=pltpu.HBM)


def _all_gather(arrays, name):
    n_arr = len(arrays)
    per = N_DEV - 1

    def body(*refs):
        x_refs, out_refs = refs[:n_arr], refs[n_arr:2 * n_arr]
        send_sems, recv_sems, local_sems = refs[2 * n_arr:]
        xi, yi, ci = lax.axis_index("x"), lax.axis_index("y"), lax.axis_index("c")
        me, sibling = (xi, yi, ci), (xi, yi, 1 - ci)
        chips = [(1 - xi, yi), (xi, 1 - yi), (1 - xi, 1 - yi)]

        def slot(a, px, py, pc):
            return out_refs[a].at[4 * px + 2 * py + pc]

        def copy(a, sem, block, to, src=None):
            return pltpu.make_async_remote_copy(
                src_ref=slot(a, *block) if src is None else src, dst_ref=slot(a, *block),
                send_sem=send_sems.at[per * a + sem], recv_sem=recv_sems.at[per * a + sem],
                device_id=to, device_id_type=MESH)

        mine = [pltpu.make_async_copy(x_refs[a], slot(a, *me), local_sems.at[a]) for a in range(n_arr)]
        for cp in mine:
            cp.start()
        sent = []
        for a in range(n_arr):
            sent.append(copy(a, 0, me, sibling, src=x_refs[a]))
            sent += [copy(a, 1 + j, me, (*chip, ci), src=x_refs[a]) for j, chip in enumerate(chips)]
        for cp in sent:
            cp.start()
        for j, chip in enumerate(chips):
            for a in range(n_arr):
                copy(a, 1 + j, (*chip, ci), me).wait_recv()
                onward = copy(a, 4 + j, (*chip, ci), sibling)
                onward.start()
                sent.append(onward)
        for a in range(n_arr):
            copy(a, 0, sibling, me).wait_recv()
        for j, chip in enumerate(chips):
            for a in range(n_arr):
                copy(a, 4 + j, (*chip, 1 - ci), me).wait_recv()
        for cp in sent:
            cp.wait_send()
        for cp in mine:
            cp.wait()

    sems = pltpu.SemaphoreType.DMA((per * n_arr,))
    return pl.pallas_call(
        body, name=name, out_shape=[jax.ShapeDtypeStruct((N_DEV,) + a.shape, a.dtype) for a in arrays],
        in_specs=[ANY] * n_arr, out_specs=[ANY] * n_arr,
        scratch_shapes=[sems, sems, pltpu.SemaphoreType.DMA((n_arr,))],
    )(*arrays)


def _exchange_copies(in_refs, out_refs, n_scattered, send_sems, recv_sems, local_sems):
    n_arr = len(in_refs)
    per = N_DEV - 1
    xi, yi, ci = lax.axis_index("x"), lax.axis_index("y"), lax.axis_index("c")
    me = 4 * xi + 2 * yi + ci
    src_of = lambda a, peer: in_refs[a].at[peer] if a < n_scattered else in_refs[a]
    copies = []
    for d in range(1, N_DEV):
        px = 1 - xi if d & 4 else xi
        py = 1 - yi if d & 2 else yi
        pc = 1 - ci if d & 1 else ci
        for a in range(n_arr):
            copies.append(pltpu.make_async_remote_copy(
                src_ref=src_of(a, 4 * px + 2 * py + pc), dst_ref=out_refs[a].at[me],
                send_sem=send_sems.at[per * a + d - 1], recv_sem=recv_sems.at[per * a + d - 1],
                device_id=(px, py, pc), device_id_type=MESH))
    own = [pltpu.make_async_copy(src_of(a, me), out_refs[a].at[me], local_sems.at[a]) for a in range(n_arr)]
    return copies + own


def _exchange_shapes(scattered, shared):
    return ([jax.ShapeDtypeStruct(a.shape, a.dtype) for a in scattered]
            + [jax.ShapeDtypeStruct((N_DEV,) + a.shape, a.dtype) for a in shared])


def _exchange_sems(n_arr):
    sems = pltpu.SemaphoreType.DMA(((N_DEV - 1) * n_arr,))
    return [sems, sems, pltpu.SemaphoreType.DMA((n_arr,))]


def _exchange(scattered, shared, name):
    n_sc = len(scattered)
    n_arr = n_sc + len(shared)

    def body(*refs):
        copies = _exchange_copies(refs[:n_arr], refs[n_arr:2 * n_arr], n_sc, *refs[2 * n_arr:])
        for cp in copies:
            cp.start()
        for cp in copies:
            cp.wait()

    return pl.pallas_call(
        body, name=name, out_shape=_exchange_shapes(scattered, shared),
        in_specs=[ANY] * n_arr, out_specs=[ANY] * n_arr, scratch_shapes=_exchange_sems(n_arr),
    )(*scattered, *shared)


def _adam_math(g, w, m, v):
    m_new = ADAM_B1 * m + (1.0 - ADAM_B1) * g
    v_new = ADAM_B2 * v + (1.0 - ADAM_B2) * (g * g)
    m_hat = m_new / (1.0 - ADAM_B1 ** ADAM_STEP)
    v_hat = v_new / (1.0 - ADAM_B2 ** ADAM_STEP)
    return -ADAM_LR * (m_hat / (jnp.sqrt(v_hat) + ADAM_EPS) + ADAM_WD * w), m_new, v_new


def _slot_sum(p_ref):
    g = p_ref[0].astype(F32)
    for s in range(1, N_DEV):
        g = g + p_ref[s].astype(F32)
    return g


def _adamw_replicated(parts, ws, ms, vs, loss_parts, name):
    n = len(ws)

    def body(*refs):
        p_refs, w_refs, m_refs, v_refs = refs[:n], refs[n:2 * n], refs[2 * n:3 * n], refs[3 * n:4 * n]
        outs = refs[4 * n + 1:]
        for j in range(n):
            g = _slot_sum(p_refs[j])
            outs[4 * j][...] = g
            for o_ref, val in zip(outs[4 * j + 1:4 * j + 4], _adam_math(g, w_refs[j][...], m_refs[j][...],
                                                                        v_refs[j][...])):
                o_ref[...] = val
        outs[4 * n][...] = _slot_sum(refs[4 * n])

    whole = pl.BlockSpec(memory_space=pltpu.VMEM)
    out = pl.pallas_call(
        body, name=name, in_specs=[whole] * (4 * n + 1), out_specs=[whole] * (4 * n + 1),
        out_shape=[jax.ShapeDtypeStruct(w.shape, F32) for w in ws for _ in range(4)]
                  + [jax.ShapeDtypeStruct(loss_parts.shape[1:], F32)],
    )(*parts, *ws, *ms, *vs, loss_parts)
    return [out[4 * j:4 * j + 4] for j in range(n)], out[4 * n]


def _adamw(parts, w, m, v, name):
    rows, cols = w.shape[-2:]
    tile = PACK_ROWS if rows % PACK_ROWS == 0 else rows
    at = (0,) if w.ndim == 3 else (Ellipsis,)

    def body(p_ref, w_ref, m_ref, v_ref, g_out, d_out, m_out, v_out):
        g = _slot_sum(p_ref)
        g_out[at] = g
        d_out[at], m_out[at], v_out[at] = _adam_math(g, w_ref[at], m_ref[at], v_ref[at])

    spec = (pl.BlockSpec((1, tile, cols), lambda i: (0, i, 0)) if w.ndim == 3
            else pl.BlockSpec((tile, cols), lambda i: (i, 0)))
    return pl.pallas_call(
        body, name=name, grid=(rows // tile,),
        in_specs=[pl.BlockSpec((N_DEV, tile, cols), lambda i: (0, i, 0)), spec, spec, spec],
        out_specs=[spec] * 4, out_shape=[jax.ShapeDtypeStruct(w.shape, F32)] * 4,
        compiler_params=_cparams(("parallel",)),
    )(parts, w, m, v)


EARLY = [("meta_tokens", 1), ("rwkv_w2", 1), ("rwkv_a2", 1), ("rwkv_g2", 1)]
LATE = [("w_br_attn", 1), ("w_br_rwkv", 1), ("w_o", 0), ("w_ffn_gate", 1), ("w_ffn_up", 1), ("w_ffn_down", 0)]
REPLICATED = ["norm_mix_g", "b_in", "attn_sinks", "rwkv_mix", "rwkv_w0", "rwkv_a0", "rwkv_k_k", "rwkv_k_a",
              "rwkv_r_k", "rwkv_ln_w", "rwkv_ln_b", "norm_ffn_g", "norm_final_g"]
WEIGHTS = ["meta_tokens", "norm_mix_g", "w_in", "b_in", "attn_sinks", "rwkv_mix", "rwkv_w0", "rwkv_w2", "rwkv_a0",
           "rwkv_a2", "rwkv_g2", "rwkv_k_k", "rwkv_k_a", "rwkv_r_k", "rwkv_ln_w", "rwkv_ln_b", "w_br_attn",
           "w_br_rwkv", "w_o", "norm_ffn_g", "w_ffn_gate", "w_ffn_up", "w_ffn_down", "norm_final_g"]


def _strip(name, a):
    return a if name in ("meta_tokens", "norm_final_g") else a[0]


def _join(gathered, axis):
    if axis == 0:
        return gathered.reshape(-1, gathered.shape[2])
    return gathered.transpose(1, 0, 2).reshape(gathered.shape[1], -1)


def _split(g, axis):
    if axis == 0:
        return g.reshape(N_DEV, -1, g.shape[1])
    return g.reshape(g.shape[0], N_DEV, -1).transpose(1, 0, 2)


W_IN_LAYOUT = [(2592, 4640), (768, 2304), (2432, 2592), 256 - GATE_LORA, (2304, 2368), 128 - DECAY_LORA,
               (2368, 2432), 128 - AAA_LORA, (0, 512), (512, 768), NP - C_VA - 128]


def _w_in_padded(w, shard_width=None):
    rows = w.shape[-2]
    width = D_IN if shard_width is None else shard_width
    parts = []
    for seg in W_IN_LAYOUT:
        if isinstance(seg, int):
            parts.append(jnp.zeros((rows, seg), w.dtype))
            continue
        lo, stop = seg
        while lo < stop:
            p = lo // width
            hi = min(stop, (p + 1) * width)
            src = w if shard_width is None else w[p]
            parts.append(src[:, lo - p * width:hi - p * width])
            lo = hi
    return jnp.concatenate(parts, axis=1)


def _w_in_unpadded(wp, lo=0, stop=D_IN):
    spans, pos = [], 0
    for seg in W_IN_LAYOUT:
        if isinstance(seg, int):
            pos += seg
        else:
            spans.append((seg[0], seg[1], pos))
            pos += seg[1] - seg[0]
    parts = []
    for a, b, at in sorted(spans):
        c, d = max(a, lo), min(b, stop)
        if c < d:
            parts.append(wp[:, at + c - a:at + d - a])
    return jnp.concatenate(parts, axis=1)


def _pad_rows(a, n):
    return jnp.pad(a, ((0, n - a.shape[0]), (0, 0)))


def _device_step(x, tgt, full, gather_late=None, scatter_early=None, scatter_last=None):
    seq = x.shape[0]
    nblk = seq // BLOCK
    lp = seq + BLOCK
    nall = nblk + 1

    w_in_p = full["w_in_p"]
    b_in_p = _w_in_padded(full["b_in"][None])
    mix = full["rwkv_mix"][None]
    mix_r, mix_k, mix_v = mix[:, 0:512], mix[:, 512:1024], mix[:, 1024:1536]
    mix_dw = jnp.pad(mix[:, 1536:1600], ((0, 0), (0, 64)))
    mix_da = jnp.pad(mix[:, 1600:1664], ((0, 0), (0, 64)))
    mix_dg = jnp.pad(mix[:, 1664:1824], ((0, 0), (0, 96)))
    w2_p = _pad_rows(full["rwkv_w2"].astype(F32), 128)
    a2_p = _pad_rows(full["rwkv_a2"].astype(F32), 128)
    g2_p = _pad_rows(full["rwkv_g2"].astype(F32), 256)
    row = lambda name: full[name].reshape(1, -1)
    sinks = row("attn_sinks")
    rope_c, rope_s1, rope_s2 = _rope_tables(lp)

    hpad = jnp.concatenate([jnp.zeros((PAD_ROWS, D_MODEL), F32), full["meta_tokens"].astype(F32), x], axis=0)
    (u,) = _rows_fwd(_rms_fn, [_view(hpad)], [row("norm_mix_g")], [D_MODEL], nblk=nall, name="norm_mix",
                     out_dtype=BF16, tile=TILE_ALL_BIG)
    proj = _mm(u, w_in_p, bias=b_in_p, name="in_proj")
    (q_r,) = _rows_fwd(_rope_fwd_fn, [_view(proj, 512, C_Q // 512), _view(rope_c), _view(rope_s1), _view(rope_s2)],
                       [], [512], nblk=nall, name="rope_q", tile=TILE_ALL_BIG)
    (k_r,) = _rows_fwd(_rope_fwd_fn, [_view(proj, 128, C_KA // 128), _view(rope_c), _view(rope_s1),
                                      _view(rope_s2)], [], [128], nblk=nall, name="rope_k", tile=TILE_ALL_BIG)

    rw_prev = jnp.pad(proj[:-1, C_R:C_R + 2048], ((1, 0), (0, 0)))
    pre_rows = [_view(proj, 512, C_R // 512), _view(proj, 512, C_K // 512), _view(proj, 512, C_V // 512),
                _view(proj, 128, C_DW // 128),
                _view(proj, 128, C_DA // 128), _view(proj, 256, C_DG // 256),
                _view(rw_prev, 512, 0), _view(rw_prev, 512, 1), _view(rw_prev, 512, 2), _view(rw_prev, 128, 14),
                _view(rw_prev, 128, 15), _view(rw_prev, 256, 6)]
    pre_consts = [mix_r, mix_k, mix_v, mix_dw, mix_da, mix_dg, row("rwkv_w0"), w2_p, row("rwkv_a0"), a2_p, g2_p,
                  row("rwkv_k_k"), row("rwkv_k_a")]
    xt_block = (BLOCK // SCAN_T * VEC_ROWS, LANES)
    r_t, k_mod, v_t, gate, xt = _rows_fwd(_rwkv_pre_xt_fn, pre_rows, pre_consts, [RW_DIM] * 4 + [xt_block],
                                          nblk=nall, name="rwkv_pre")
    xt = xt.reshape(-1, VEC_ROWS, LANES)
    spread, collect = _selectors()
    y_scan, hist, o_attn, lse, *late = _wkv_fwd(xt, v_t, spread, (q_r, k_r, proj, C_VA // 128, sinks), name="wkv_fwd",
                                                shared=gather_late[0] if gather_late else ())
    if gather_late:
        full = {**full, **gather_late[1](late)}
    post_rows = [_view(y_scan, off=1), _view(r_t, off=1), _view(k_mod, off=1), _view(v_t, off=1), _view(gate, off=1)]
    post_consts = [row("rwkv_ln_w"), row("rwkv_ln_b"), row("rwkv_r_k")]
    (y_rwkv,) = _rows_fwd(_rwkv_post_fn, post_rows, post_consts, [RW_DIM], nblk=nblk, name="rwkv_post",
                          out_dtype=BF16)

    ya = _mm(o_attn, full["w_br_attn"], name="br_attn")
    yr = _mm(y_rwkv, full["w_br_rwkv"], name="br_rwkv")
    merge_rows = [_view(ya), _view(yr), _view(proj, 1024, C_G1 // 1024, 1), _view(proj, 1024, C_G2 // 1024, 1)]
    (merged,) = _rows_fwd(_merge_fn, merge_rows, [], [D_MODEL], nblk=nblk, name="merge", out_dtype=BF16)
    h1 = _mm(merged, full["w_o"], residual=x, name="out_proj")
    (f,) = _rows_fwd(_rms_fn, [_view(h1)], [row("norm_ffn_g")], [D_MODEL], nblk=nblk, name="norm_ffn",
                     out_dtype=BF16, tile=TILE_REAL)
    ff_gate = _mm(f, full["w_ffn_gate"], name="ffn_gate")
    ff_up = _mm(f, full["w_ffn_up"], name="ffn_up")
    (act,) = _rows_fwd(_swiglu_fn, [_view(ff_gate), _view(ff_up)], [], [D_FF], nblk=nblk, name="swiglu",
                       out_dtype=BF16, tile=TILE_WIDE)
    h2 = _mm(act, full["w_ffn_down"], residual=h1, name="ffn_down")

    grads = {}
    ones_col = jnp.ones((seq, 1), F32)
    loss_rows, dh2, grads["norm_final_g"] = _rows_bwd(
        _loss_fn, [_view(h2), _view(tgt)], [row("norm_final_g")], [_view(ones_col)], nblk=nblk, name="loss",
        diff_rows=[0], diff_consts=[0], fwd_widths=[1], tile=TILE_REAL)
    loss = jnp.sum(loss_rows)

    dact = _mm(dh2, full["w_ffn_down"], tb=True, name="d_act")
    grads["w_ffn_down"] = _mm(act, dh2, ta=True, name="dw_ffn_down")
    dgate, dup = _rows_bwd(_swiglu_fn, [_view(ff_gate), _view(ff_up)], [], [_view(dact)], nblk=nblk,
                           name="swiglu_bwd", diff_rows=[0, 1], diff_consts=[], row_dtype=BF16, tile=TILE_WIDE)
    grads["w_ffn_gate"] = _mm(f, dgate, ta=True, name="dw_ffn_gate")
    grads["w_ffn_up"] = _mm(f, dup, ta=True, name="dw_ffn_up")
    df = _mm(dgate, full["w_ffn_gate"], tb=True, name="df_gate")
    df = _mm(dup, full["w_ffn_up"], tb=True, residual=df, name="df_up")
    dh1, grads["norm_ffn_g"] = _rows_bwd(_rms_fn, [_view(h1)], [row("norm_ffn_g")], [_view(df)], nblk=nblk,
                                         name="norm_ffn_bwd", diff_rows=[0], diff_consts=[0], acc=[_view(dh2)],
                                         tile=TILE_REAL)
    dmerged = _mm(dh1, full["w_o"], tb=True, name="d_merged")
    grads["w_o"] = _mm(merged, dh1, ta=True, name="dw_o")
    dya, dyr, dg1, dg2 = _rows_bwd(_merge_fn, merge_rows, [], [_view(dmerged)], nblk=nblk, name="merge_bwd",
                                   diff_rows=[0, 1, 2, 3], diff_consts=[], row_dtype=BF16)
    grads["w_br_attn"] = _mm(o_attn, dya, ta=True, name="dw_br_attn")
    grads["w_br_rwkv"] = _mm(y_rwkv, dyr, ta=True, name="dw_br_rwkv")
    dy_attn = _mm(dya, full["w_br_attn"], tb=True, name="d_y_attn")
    dy_rwkv = _mm(dyr, full["w_br_rwkv"], tb=True, name="d_y_rwkv")

    post = _rows_bwd(_rwkv_post_fn, post_rows, post_consts, [_view(dy_rwkv)], nblk=nblk, name="rwkv_post_bwd",
                     diff_rows=[0, 1, 2, 3, 4], diff_consts=[0, 1, 2])
    dys, dr_post, dk_post, dv_post, dgate_post = post[:5]
    grads["rwkv_ln_w"], grads["rwkv_ln_b"], grads["rwkv_r_k"] = post[5:]
    dxt, dv_s, *early_parts = _wkv_bwd(xt, v_t, hist, dys, spread, collect, name="wkv_bwd",
                                       scattered=scatter_early(grads) if scatter_early else ())
    pre_cts = [_view(dxt.reshape(-1, LANES), rows=xt_block[0]), _view(dv_s)] + [
        _view(t, off=-1) for t in (dr_post, dk_post, dv_post, dgate_post)]
    pre = _rows_bwd(_rwkv_pre_fn, pre_rows, pre_consts, pre_cts, nblk=nall, name="rwkv_pre_bwd",
                    diff_rows=list(range(12)), diff_consts=list(range(13)), ct_map=_rwkv_pre_cts)
    d_cur, d_prev, d_par = pre[0:6], pre[6:12], pre[12:]
    up = lambda t: jnp.pad(t[1:], ((0, 1), (0, 0)))
    d_rw = [c + up(p) for c, p in zip(d_cur, d_prev)]
    grads["rwkv_mix"] = jnp.concatenate([d_par[0], d_par[1], d_par[2], d_par[3][:, :DECAY_LORA],
                                         d_par[4][:, :AAA_LORA], d_par[5][:, :GATE_LORA]], axis=1)
    grads["rwkv_w0"], grads["rwkv_w2"] = d_par[6], d_par[7][:DECAY_LORA]
    grads["rwkv_a0"], grads["rwkv_a2"] = d_par[8], d_par[9][:AAA_LORA]
    grads["rwkv_g2"] = d_par[10][:GATE_LORA]
    grads["rwkv_k_k"], grads["rwkv_k_a"] = d_par[11], d_par[12]

    dq_real, dk_r, dva, grads["attn_sinks"] = _attn_bwd(q_r, k_r, proj, C_VA // 128, sinks, o_attn, lse, dy_attn,
                                                        nblk=nblk, name="attn_bwd")
    dq_r = jnp.pad(dq_real, ((BLOCK, 0), (0, 0)))
    (dq,) = _rows_fwd(_rope_bwd_fn, [_view(dq_r), _view(rope_c), _view(rope_s1), _view(rope_s2)], [], [512],
                      nblk=nall, name="rope_q_bwd", out_dtype=BF16, tile=TILE_ALL_BIG)
    (dka,) = _rows_fwd(_rope_bwd_fn, [_view(dk_r), _view(rope_c), _view(rope_s1),
                                      _view(rope_s2)], [], [128], nblk=nall, name="rope_k_bwd", out_dtype=BF16,
                       tile=TILE_ALL_BIG)

    lead = lambda t: jnp.pad(t, ((BLOCK, 0), (0, 0)))
    pieces = [lead(dg1), lead(dg2), d_rw[0], d_rw[1], d_rw[2], d_rw[5], d_rw[3], d_rw[4], dq, dka, dva,
              jnp.zeros((lp, NP - C_VA - 128), BF16)]
    dproj = jnp.concatenate([p.astype(BF16) for p in pieces], axis=1)
    grads["w_in_p"] = _mm(u, dproj, ta=True, name="dw_in")
    grads["b_in"] = _w_in_unpadded(_colsum(dproj, name="db_in"))
    du, *last_parts = _mm(dproj, w_in_p, tb=True, name="d_u", scattered=scatter_last(grads)) if scatter_last else (
        _mm(dproj, w_in_p, tb=True, name="d_u"),)
    dh, grads["norm_mix_g"] = _rows_bwd(_rms_fn, [_view(hpad)], [row("norm_mix_g")], [_view(du)], nblk=nall,
                                        name="norm_mix_bwd", diff_rows=[0], diff_consts=[0], acc=[_view(lead(dh1))],
                                        tile=TILE_ALL)
    grads["meta_tokens"] = dh[PAD_ROWS:BLOCK]
    return loss, dh[BLOCK:], grads, early_parts, last_parts


def kernel(x, meta_tokens, norm_mix_g, w_in, b_in, attn_sinks, rwkv_mix, rwkv_w0, rwkv_w2, rwkv_a0, rwkv_a2, rwkv_g2, rwkv_k_k, rwkv_k_a, rwkv_r_k, rwkv_ln_w, rwkv_ln_b, w_br_attn, w_br_rwkv, w_o, norm_ffn_g, w_ffn_gate, w_ffn_up, w_ffn_down, norm_final_g, loss_target, m_meta_tokens, m_norm_mix_g, m_w_in, m_b_in, m_attn_sinks, m_rwkv_mix, m_rwkv_w0, m_rwkv_w2, m_rwkv_a0, m_rwkv_a2, m_rwkv_g2, m_rwkv_k_k, m_rwkv_k_a, m_rwkv_r_k, m_rwkv_ln_w, m_rwkv_ln_b, m_w_br_attn, m_w_br_rwkv, m_w_o, m_norm_ffn_g, m_w_ffn_gate, m_w_ffn_up, m_w_ffn_down, m_norm_final_g, v_meta_tokens, v_norm_mix_g, v_w_in, v_b_in, v_attn_sinks, v_rwkv_mix, v_rwkv_w0, v_rwkv_w2, v_rwkv_a0, v_rwkv_a2, v_rwkv_g2, v_rwkv_k_k, v_rwkv_k_a, v_rwkv_r_k, v_rwkv_ln_w, v_rwkv_ln_b, v_w_br_attn, v_w_br_rwkv, v_w_o, v_norm_ffn_g, v_w_ffn_gate, v_w_ffn_up, v_w_ffn_down, v_norm_final_g):
    given = dict(locals())
    wts = {n: _strip(n, given[n]) for n in WEIGHTS}
    as_rows = lambda a: a.reshape(1, -1) if a.ndim == 1 else a
    width = wts["w_in"].shape[1]
    wire = lambda table: [wts[n].astype(BF16) for n, _ in table]

    w_in_all, *early_all = _all_gather([wts["w_in"].astype(BF16)] + wire(EARLY), name="gather_weights")
    full = {n: wts[n] for n in REPLICATED}
    full.update({n: _join(g, axis) for (n, axis), g in zip(EARLY, early_all)})
    full["w_in_p"] = _w_in_padded(w_in_all, shard_width=width)
    gather_late = (wire(LATE), lambda got: {n: _join(g, axis) for (n, axis), g in zip(LATE, got)})
    scatter_early = lambda g: [_split(g[n], axis).astype(BF16) for n, axis in LATE]
    scatter_last = lambda g: [jnp.stack([_w_in_unpadded(g["w_in_p"], p * width, (p + 1) * width)
                                         for p in range(N_DEV)]).astype(BF16)]

    loss_part, grad_x, grads, parts_late, (parts_w_in,) = _device_step(
        x[0], loss_target[0], full, gather_late, scatter_early, scatter_last)

    g_early = [_split(grads[n], axis).astype(BF16) for n, axis in EARLY]
    g_small = [grads[n].reshape(as_rows(given[n]).shape) for n in REPLICATED] + [jnp.full((8, LANES), loss_part)]
    got = _exchange(g_early, g_small, name="exchange_grads")
    parts_early, parts_small, parts_loss = got[:len(EARLY)], got[len(EARLY):-1], got[-1]
    results = [{}, {}, {}, {}]
    for (n, _), parts in zip([("w_in", 1)] + EARLY + LATE, [parts_w_in] + list(parts_early) + list(parts_late)):
        for kind, a in enumerate(_adamw(parts, given[n], given["m_" + n], given["v_" + n], name="adamw_" + n)):
            results[kind][n] = a
    small, loss = _adamw_replicated(parts_small, *[[as_rows(given[pre + n]) for n in REPLICATED]
                                                   for pre in ("", "m_", "v_")], parts_loss, name="adamw_replicated")
    for n, four in zip(REPLICATED, small):
        for kind, a in enumerate(four):
            results[kind][n] = a
    loss = loss[0, 0]
    out = [loss, grad_x[None]]
    for kind in range(4):
        out += [results[kind][n].reshape(given[n].shape) for n in WEIGHTS]
    return tuple(out)
```

```python
import jax
import jax.numpy as jnp
from jax import lax
from jax.experimental import pallas as pl
from jax.experimental.pallas import tpu as pltpu

F32 = jnp.float32
BF16 = jnp.bfloat16

N_DEV = 8
D_MODEL = 1024
N_META = 16
BLOCK = 128
PAD_ROWS = BLOCK - N_META
HEAD_DIM = 64
Q_HEADS = 8
KV_HEADS = 2
GROUP = Q_HEADS // KV_HEADS
ROPE_DIM = HEAD_DIM // 4
ROPE_HALF = ROPE_DIM // 2
ROPE_THETA = 500000.0
RW_HEADS = 8
RW_DIM = 512
DECAY_LORA = 64
AAA_LORA = 64
GATE_LORA = 160
D_FF = 2816
D_IN = 4640
RMS_EPS = 1e-6
RWKV_LN_EPS = 64e-5
NEG_INF = -1e30
SCAN_T = 16
SCAN_CHUNKS = 8
LANES = 128
PACK_ROWS = 256
TILE_ALL = 384
TILE_ALL_BIG = 1408
TILE_REAL = 512
TILE_WIDE = 256
TILE_REAL_BIG = 1024
TILE_WIDE_BIG = 512

ADAM_LR = 0.001
ADAM_B1 = 0.9
ADAM_B2 = 0.999
ADAM_EPS = 1e-08
ADAM_WD = 0.01
ADAM_STEP = 10

C_G1, C_G2 = 0, 1024
C_R, C_K, C_V, C_DG, C_DW, C_DA = 2048, 2560, 3072, 3584, 3840, 3968
C_Q, C_KA, C_VA = 4096, 4608, 4736
NP = 5120

VMEM_LIMIT = 48 * 1024 * 1024
SCAN_VMEM_LIMIT = 60 * 1024 * 1024


def _cparams(sem, vmem=VMEM_LIMIT):
    return pltpu.CompilerParams(dimension_semantics=sem, vmem_limit_bytes=vmem)


def _pick(n, cands):
    for c in cands:
        if n % c == 0:
            return c
    raise ValueError(f"no tile for {n}")


def _mm(a, b, *, ta=False, tb=False, bias=None, residual=None, name, scattered=()):
    m = a.shape[1] if ta else a.shape[0]
    k = a.shape[0] if ta else a.shape[1]
    n = b.shape[0] if tb else b.shape[1]
    assert k == (b.shape[1] if tb else b.shape[0]), (a.shape, b.shape, ta, tb)
    tm = _pick(m, (512, 1408, 256, 128) if ta else (1056, 1024, 528, 512, 384, 256, 128))
    tn = _pick(n, (1024, 512, 1408, 256, 128))
    if k <= 1024:
        tk = k
    else:
        tk = _pick(k, (1024, 1056, 528, 512) if (ta and not tb) else (1024, 1408, 512, 256, 128))
    nk = k // tk
    has_bias = bias is not None
    has_res = residual is not None
    dn = (((0 if ta else 1,), (1 if tb else 0,)), ((), ()))

    def body(*refs):
        a_ref, b_ref = refs[0], refs[1]
        pos = 2
        bias_ref = res_ref = None
        if has_bias:
            bias_ref = refs[pos]
            pos += 1
        if has_res:
            res_ref = refs[pos]
            pos += 1
        o_ref, acc_ref = refs[pos], refs[pos + 1]
        kk = pl.program_id(2)
        part = lax.dot_general(a_ref[...].astype(BF16), b_ref[...].astype(BF16), dn, preferred_element_type=F32)

        def finish(out):
            if has_bias:
                out = out + bias_ref[...]
            if has_res:
                out = out + res_ref[...]
            o_ref[...] = out

        if nk == 1:
            finish(part)
        else:
            @pl.when(kk == 0)
            def _():
                acc_ref[...] = part

            @pl.when((kk > 0) & (kk < nk - 1))
            def _():
                acc_ref[...] += part

            @pl.when(kk == nk - 1)
            def _():
                finish(acc_ref[...] + part)

    in_specs = [
        pl.BlockSpec((tk, tm), lambda i, j, kk: (kk, i)) if ta else pl.BlockSpec((tm, tk), lambda i, j, kk: (i, kk)),
        pl.BlockSpec((tn, tk), lambda i, j, kk: (j, kk)) if tb else pl.BlockSpec((tk, tn), lambda i, j, kk: (kk, j)),
    ]
    args = [a, b]
    if has_bias:
        in_specs.append(pl.BlockSpec((1, tn), lambda i, j, kk: (0, j)))
        args.append(bias)
    if has_res:
        in_specs.append(pl.BlockSpec((tm, tn), lambda i, j, kk: (i, j)))
        args.append(residual)
    grid = (m // tm, n // tn, nk)
    n_x = len(scattered)
    out = pl.pallas_call(
        _with_exchange(body, len(args), 1, 1, scattered, (), grid), name=name, grid=grid,
        in_specs=in_specs + [ANY] * n_x,
        out_specs=[pl.BlockSpec((tm, tn), lambda i, j, kk: (i, j))] + [ANY] * n_x,
        out_shape=[jax.ShapeDtypeStruct((m, n), F32)] + _exchange_shapes(scattered, ()),
        scratch_shapes=[pltpu.VMEM((tm, tn) if nk > 1 else (8, LANES), F32)] + (_exchange_sems(n_x) if n_x else []),
        compiler_params=_cparams(("arbitrary",) * 3 if n_x else ("parallel", "parallel", "arbitrary")),
    )(*args, *scattered)
    return out if n_x else out[0]


def _colsum(x, name):
    m, n = x.shape
    tm = _pick(m, (1408, 512, 384, BLOCK))

    def body(x_ref, o_ref):
        i = pl.program_id(0)
        s = jnp.sum(x_ref[...].astype(F32), axis=0, keepdims=True)

        @pl.when(i == 0)
        def _():
            o_ref[...] = s

        @pl.when(i > 0)
        def _():
            o_ref[...] += s

    return pl.pallas_call(
        body, name=name, grid=(m // tm,),
        in_specs=[pl.BlockSpec((tm, n), lambda i: (i, 0))],
        out_specs=pl.BlockSpec((1, n), lambda i: (0, 0)),
        out_shape=jax.ShapeDtypeStruct((1, n), F32),
        compiler_params=_cparams(("arbitrary",)),
    )(x)


def _view(arr, width=None, col=0, off=0, rows=BLOCK):
    return (arr, arr.shape[1] if width is None else width, col, off, rows)


def _row_spec(view):
    _, width, col, off, rows = view
    if off < 0:
        return pl.BlockSpec((rows, width), lambda i, col=col, off=off: (jnp.maximum(i + off, 0), col))
    return pl.BlockSpec((rows, width), lambda i, col=col, off=off: (i + off, col))


def _const_spec(arr):
    return pl.BlockSpec(arr.shape, lambda i: (0,) * arr.ndim)


def _retile(views, tile):
    assert all(v[3] == 0 and v[4] == BLOCK for v in views)
    return [v[:4] + (tile,) for v in views]


def _rows_fwd(fn, rows, consts, out_widths, *, nblk, name, out_dtype=F32, tile=BLOCK):
    nr, nc = len(rows), len(consts)
    tile = tile if (nblk * BLOCK) % tile == 0 else BLOCK
    if tile != BLOCK:
        rows, nblk = _retile(rows, tile), nblk * BLOCK // tile
    out_blocks = [(tile, w) if isinstance(w, int) else w for w in out_widths]

    def body(*refs):
        i = pl.program_id(0)
        vals = [r[...] for r in refs[:nr + nc]]
        outs = fn(i, *vals)
        for o_ref, o in zip(refs[nr + nc:], outs):
            o_ref[...] = o.astype(o_ref.dtype)

    return pl.pallas_call(
        body, name=name, grid=(nblk,),
        in_specs=[_row_spec(v) for v in rows] + [_const_spec(c) for c in consts],
        out_specs=[pl.BlockSpec(b, lambda i: (i, 0)) for b in out_blocks],
        out_shape=[jax.ShapeDtypeStruct((nblk * r, w), out_dtype) for r, w in out_blocks],
        compiler_params=_cparams(("parallel",)),
    )(*[v[0] for v in rows], *consts)


def _rows_bwd(fn, rows, consts, cts, *, nblk, name, diff_rows, diff_consts, acc=None, fwd_widths=(), row_dtype=F32,
              ct_map=None, tile=BLOCK):
    nr, nc = len(rows), len(consts)
    acc = acc or [None] * len(diff_rows)
    tile = tile if (nblk * BLOCK) % tile == 0 else BLOCK
    if tile != BLOCK:
        rows, nblk = _retile(rows, tile), nblk * BLOCK // tile
        cts = [c if c is None else _retile([c], tile)[0] for c in cts]
        acc = [a if a is None else _retile([a], tile)[0] for a in acc]
    ct_views = [c for c in cts if c is not None]
    acc_views = [a for a in acc if a is not None]
    n_in = nr + nc + len(ct_views) + len(acc_views)
    n_fwd = len(fwd_widths)

    def body(*refs):
        i = pl.program_id(0)
        row_vals = [r[...] for r in refs[:nr]]
        const_vals = [r[...] for r in refs[nr:nr + nc]]
        ct_vals = [r[...] for r in refs[nr + nc:nr + nc + len(ct_views)]]
        acc_vals = [r[...] for r in refs[nr + nc + len(ct_views):n_in]]
        out_refs = refs[n_in:]

        def f(*dargs):
            rv = list(row_vals)
            cv = list(const_vals)
            for pos, idx in enumerate(diff_rows):
                rv[idx] = dargs[pos]
            for pos, idx in enumerate(diff_consts):
                cv[idx] = dargs[len(diff_rows) + pos]
            return tuple(fn(i, *rv, *cv))

        primals = [row_vals[idx] for idx in diff_rows] + [const_vals[idx] for idx in diff_consts]
        outs, pull = jax.vjp(f, *primals)
        full_ct, ci = [], 0
        if ct_map is not None:
            full_ct = ct_map(i, *ct_vals)
        else:
            for o, c in zip(outs, cts):
                if c is None:
                    full_ct.append(jnp.zeros_like(o))
                else:
                    full_ct.append(ct_vals[ci])
                    ci += 1
        grads = pull(tuple(full_ct))
        for o_ref, o in zip(out_refs[:n_fwd], outs):
            o_ref[...] = o
        ai = 0
        for pos in range(len(diff_rows)):
            g = grads[pos]
            if acc[pos] is not None:
                g = g + acc_vals[ai]
                ai += 1
            out_refs[n_fwd + pos][...] = g.astype(row_dtype)
        for pos in range(len(diff_consts)):
            g = grads[len(diff_rows) + pos]
            o_ref = out_refs[n_fwd + len(diff_rows) + pos]

            @pl.when(i == 0)
            def _(o_ref=o_ref, g=g):
                o_ref[...] = g

            @pl.when(i > 0)
            def _(o_ref=o_ref, g=g):
                o_ref[...] += g

    out_specs = [pl.BlockSpec((tile, w), lambda i: (i, 0)) for w in fwd_widths]
    out_shape = [jax.ShapeDtypeStruct((nblk * tile, w), F32) for w in fwd_widths]
    for idx in diff_rows:
        out_specs.append(pl.BlockSpec((tile, rows[idx][1]), lambda i: (i, 0)))
        out_shape.append(jax.ShapeDtypeStruct((nblk * tile, rows[idx][1]), row_dtype))
    for idx in diff_consts:
        out_specs.append(_const_spec(consts[idx]))
        out_shape.append(jax.ShapeDtypeStruct(consts[idx].shape, F32))
    return pl.pallas_call(
        body, name=name, grid=(nblk,),
        in_specs=([_row_spec(v) for v in rows] + [_const_spec(c) for c in consts]
                  + [_row_spec(v) for v in ct_views] + [_row_spec(v) for v in acc_views]),
        out_specs=out_specs, out_shape=out_shape,
        compiler_params=_cparams(("arbitrary",)),
    )(*[v[0] for v in rows], *consts, *[v[0] for v in ct_views], *[v[0] for v in acc_views])


def _rms_fn(i, x, g):
    return (x * lax.rsqrt(jnp.mean(x * x, axis=-1, keepdims=True) + RMS_EPS) * g,)


def _sigmoid(x):
    return 1.0 / (1.0 + jnp.exp(-x))


def _softplus(x):
    return jnp.maximum(x, 0.0) + jnp.log(1.0 + jnp.exp(-jnp.abs(x)))


def _split2(x):
    hi = x.astype(BF16)
    lo = (x - hi.astype(F32)).astype(BF16)
    return jnp.concatenate([hi, lo], axis=1)


@jax.custom_vjp
def _head_sum(x):
    r = lax.broadcasted_iota(jnp.int32, (2 * RW_DIM, RW_DIM), 0) % RW_DIM // HEAD_DIM
    c = lax.broadcasted_iota(jnp.int32, (2 * RW_DIM, RW_DIM), 1) // HEAD_DIM
    return jnp.dot(_split2(x), (r == c).astype(BF16), preferred_element_type=F32)


_head_sum.defvjp(lambda x: (_head_sum(x), None), lambda _, ct: (_head_sum(ct),))


@jax.custom_vjp
def _dot_bf16(x, w):
    return jnp.dot(x.astype(BF16), w.astype(BF16), preferred_element_type=F32)


def _dot_bf16_bwd(res, ct):
    x, w = res
    ct = ct.astype(BF16)
    dx = lax.dot_general(ct, w.astype(BF16), (((1,), (1,)), ((), ())), preferred_element_type=F32)
    dw = lax.dot_general(x.astype(BF16), ct, (((0,), (0,)), ((), ())), preferred_element_type=F32)
    return dx, dw


_dot_bf16.defvjp(lambda x, w: (_dot_bf16(x, w), (x, w)), _dot_bf16_bwd)


def _rwkv_pre_fn(i, r, k, v, dw, da, dg, r_p, k_p, v_p, dw_p, da_p, dg_p,
                 mix_r, mix_k, mix_v, mix_dw, mix_da, mix_dg, w0, w2, a0, a2, g2, k_k, k_a):
    row = i * BLOCK + lax.broadcasted_iota(jnp.int32, (BLOCK, 1), 0)
    live = row >= PAD_ROWS
    live_prev = row >= PAD_ROWS + 1

    def shift(cur, prev, mix):
        cur = jnp.where(live, cur, 0.0)
        prev = jnp.where(live_prev, prev, 0.0)
        return cur + (prev - cur) * mix

    r = shift(r, r_p, mix_r)
    k = shift(k, k_p, mix_k)
    v = shift(v, v_p, mix_v)
    dw = shift(dw, dw_p, mix_dw)
    da = shift(da, da_p, mix_da)
    dg = shift(dg, dg_p, mix_dg)
    wlog = -_softplus(-(w0 + _dot_bf16(jnp.tanh(dw), w2))) - 0.5
    decay = jnp.exp(-jnp.exp(wlog))
    a = _sigmoid(a0 + _dot_bf16(da, a2))
    g = _dot_bf16(_sigmoid(dg), g2)
    kk = k * k_k
    norm_sq = jnp.where(live, _head_sum(kk * kk), 1.0)
    kk = kk / jnp.maximum(jnp.sqrt(norm_sq), 1e-12)
    k_mod = k * (1.0 + (a - 1.0) * k_a)
    return r, decay, k_mod, v, -kk, kk * a, g


def _rwkv_pre_xt_fn(i, *args):
    r, decay, k_mod, v, a_neg, b, g = _rwkv_pre_fn(i, *args)
    t = SCAN_T
    xt = jnp.concatenate([_rows_to_xt(x[c * t:(c + 1) * t]) for c in range(BLOCK // t)
                          for x in (a_neg, decay, b, k_mod, r)], axis=0)
    return r, k_mod, v, g, xt


def _rwkv_pre_cts(i, dxt, dv_s, dr_p, dk_p, dv_p, dg_p):
    t = SCAN_T
    d_a, d_w, d_b, d_k, d_r = [
        jnp.concatenate([_xt_to_rows(dxt[c * VEC_ROWS + n * HEAD_DIM:c * VEC_ROWS + (n + 1) * HEAD_DIM])
                         for c in range(BLOCK // t)], axis=0) for n in range(N_VEC)]
    dr_p, dk_p, dv_p, dg_p = [jnp.where(i > 0, x, 0.0) for x in (dr_p, dk_p, dv_p, dg_p)]
    return d_r + dr_p, d_w, d_k + dk_p, dv_s + dv_p, d_a, d_b, dg_p


def _rwkv_post_fn(i, ys, r, k_mod, v, g, ln_w, ln_b, r_k):
    mean = _head_sum(ys) * (1.0 / HEAD_DIM)
    d = ys - mean
    var = _head_sum(d * d) * (1.0 / HEAD_DIM)
    yn = d * lax.rsqrt(var + RWKV_LN_EPS) * ln_w + ln_b
    bonus = _head_sum(r * k_mod * r_k) * v
    return ((yn + bonus) * g,)


def _merge_fn(i, ya, yr, g1, g2):
    return (_sigmoid(g1) * ya + _sigmoid(g2) * yr,)


def _swiglu_fn(i, gate, up):
    return (gate * _sigmoid(gate) * up,)


def _loss_fn(i, h, tgt, g):
    y = h * lax.rsqrt(jnp.mean(h * h, axis=-1, keepdims=True) + RMS_EPS) * g
    err = y - tgt
    return (0.5 * jnp.mean(err * err, axis=-1, keepdims=True),)


def _rope_tables(lp):
    pos = (jnp.arange(lp, dtype=jnp.int32) - PAD_ROWS).astype(F32)
    inv_freq = jnp.power(jnp.float32(ROPE_THETA), -jnp.arange(ROPE_HALF, dtype=F32) * (2.0 / ROPE_DIM))
    ang = pos[:, None] * inv_freq[None, :]
    cos, sin = jnp.cos(ang), jnp.sin(ang)
    one = jnp.ones((lp, HEAD_DIM - ROPE_DIM), F32)
    zero_h = jnp.zeros((lp, ROPE_HALF), F32)
    zero_r = jnp.zeros((lp, HEAD_DIM - ROPE_DIM), F32)
    c = jnp.concatenate([cos, cos, one], axis=1)
    s1 = jnp.concatenate([-sin, zero_h, zero_r], axis=1)
    s2 = jnp.concatenate([zero_h, sin, zero_r], axis=1)
    return tuple(jnp.tile(t, (1, LANES // HEAD_DIM)) for t in (c, s1, s2))


def _rope_fwd_fn(i, x, c, s1, s2):
    n = x.shape[1]
    c, s1, s2 = [jnp.tile(t, (1, n // LANES)) for t in (c, s1, s2)]
    return (x * c + pltpu.roll(x, n - ROPE_HALF, 1) * s1 + pltpu.roll(x, ROPE_HALF, 1) * s2,)


def _rope_bwd_fn(i, dy, c, s1, s2):
    n = dy.shape[1]
    c, s1, s2 = [jnp.tile(t, (1, n // LANES)) for t in (c, s1, s2)]
    return (dy * c + pltpu.roll(dy * s1, ROPE_HALF, 1) + pltpu.roll(dy * s2, n - ROPE_HALF, 1),)


def _attn_mask(i):
    r = lax.broadcasted_iota(jnp.int32, (BLOCK, 3 * BLOCK), 0)
    c = lax.broadcasted_iota(jnp.int32, (BLOCK, 3 * BLOCK), 1)
    meta = (c < BLOCK) & (c >= PAD_ROWS)
    prev = (c >= BLOCK) & (c < 2 * BLOCK) & ((c - BLOCK) > r) & (i >= 1)
    cur = (c >= 2 * BLOCK) & ((c - 2 * BLOCK) <= r)
    return meta | prev | cur


def _attn_rows(ref, g):
    return ref[:, g * HEAD_DIM:(g + 1) * HEAD_DIM]


def _attn_group(i, g, q_all, k_refs, v_refs, s_ref):
    heads = range(g * GROUP, (g + 1) * GROUP)
    kcat = jnp.concatenate([_attn_rows(r, g) for r in k_refs], axis=0).astype(BF16)
    vcat = jnp.concatenate([_attn_rows(r, g) for r in v_refs], axis=0).astype(BF16)
    qg = jnp.concatenate([q_all[:, h * HEAD_DIM:(h + 1) * HEAD_DIM] for h in heads], axis=0).astype(BF16)
    sink = jnp.concatenate([jnp.broadcast_to(s_ref[0:1, h:h + 1], (BLOCK, 1)) for h in heads], axis=0)
    s = lax.dot_general(qg, kcat, (((1,), (1,)), ((), ())), preferred_element_type=F32) * (HEAD_DIM ** -0.5)
    valid = jnp.concatenate([_attn_mask(i)] * GROUP, axis=0)
    return heads, qg, kcat, vcat, sink, jnp.where(valid, s, NEG_INF)


ATTN_SUB = 4


def _attn_specs(v_col):
    q = [pl.BlockSpec((BLOCK, Q_HEADS * HEAD_DIM), lambda i, n=n: (ATTN_SUB * i + 1 + n, 0)) for n in range(ATTN_SUB)]
    blk = lambda col: ([pl.BlockSpec((BLOCK, 2 * HEAD_DIM), lambda i: (0, col))]
                       + [pl.BlockSpec((BLOCK, 2 * HEAD_DIM), lambda i, n=n: (ATTN_SUB * i + n, col))
                          for n in range(ATTN_SUB + 1)])
    return q + blk(0) + blk(v_col) + [pl.BlockSpec((1, Q_HEADS), lambda i: (0, 0))]


def _attn_split(refs):
    n = ATTN_SUB
    q_refs, k_refs, v_refs = refs[:n], refs[n:2 * n + 2], refs[2 * n + 2:3 * n + 4]
    return q_refs, k_refs, v_refs, refs[3 * n + 4], refs[3 * n + 5:]


def _attn_fwd_block(blk, q_ref, keys, vals, s_ref, o_ref, lse_ref):
    q_all = q_ref[...]
    for g in range(KV_HEADS):
        heads, _, _, vcat, sink, s = _attn_group(blk, g, q_all, keys, vals, s_ref)
        m = jnp.maximum(jnp.max(s, axis=-1, keepdims=True), sink)
        p = jnp.exp(s - m)
        den = jnp.sum(p, axis=-1, keepdims=True) + jnp.exp(sink - m)
        o = jnp.dot(p.astype(BF16), vcat, preferred_element_type=F32) / den
        lse = m + jnp.log(den)
        for n, h in enumerate(heads):
            o_ref[:, h * HEAD_DIM:(h + 1) * HEAD_DIM] = o[n * BLOCK:(n + 1) * BLOCK]
            lse_ref[:, h:h + 1] = lse[n * BLOCK:(n + 1) * BLOCK]


def _attn_bwd(q, k, v, v_col, sinks, o, lse, do, *, nblk, name):
    lp = k.shape[0]
    rows = ATTN_SUB * BLOCK

    def body(*refs):
        q_refs, k_refs, v_refs, s_ref, (o_ref, lse_ref, do_ref, dq_ref, dk_ref, dv_ref, ds_ref) = _attn_split(refs)
        i = pl.program_id(0)

        @pl.when(i == 0)
        def _():
            dk_ref[...] = jnp.zeros_like(dk_ref)
            dv_ref[...] = jnp.zeros_like(dv_ref)
            ds_ref[...] = jnp.zeros_like(ds_ref)

        lane = lax.broadcasted_iota(jnp.int32, (1, Q_HEADS), 1)
        for sub in range(ATTN_SUB):
            at = slice(sub * BLOCK, (sub + 1) * BLOCK)
            blk = ATTN_SUB * i + sub
            keys = (k_refs[0], k_refs[1 + sub], k_refs[2 + sub])
            vals = (v_refs[0], v_refs[1 + sub], v_refs[2 + sub])
            prev_rows = pl.ds(pl.multiple_of(blk * BLOCK, BLOCK), BLOCK)
            cur_rows = pl.ds(pl.multiple_of((blk + 1) * BLOCK, BLOCK), BLOCK)
            q_all, o_all, do_all, lse_all = q_refs[sub][...], o_ref[at, :], do_ref[at, :], lse_ref[at, :]
            for g in range(KV_HEADS):
                heads, qg, kcat, vcat, sink, s = _attn_group(blk, g, q_all, keys, vals, s_ref)
                stack = lambda x: jnp.concatenate([x[:, h * HEAD_DIM:(h + 1) * HEAD_DIM] for h in heads], axis=0)
                lse_g = jnp.concatenate([lse_all[:, h:h + 1] for h in heads], axis=0)
                do_g = stack(do_all)
                p = jnp.exp(s - lse_g)
                delta = jnp.sum(do_g * stack(o_all), axis=-1, keepdims=True)
                dp = lax.dot_general(do_g.astype(BF16), vcat, (((1,), (1,)), ((), ())), preferred_element_type=F32)
                dsc = (p * (dp - delta) * (HEAD_DIM ** -0.5)).astype(BF16)
                dq = jnp.dot(dsc, kcat, preferred_element_type=F32)
                dk_all = lax.dot_general(dsc, qg, (((0,), (0,)), ((), ())), preferred_element_type=F32)
                dv_all = lax.dot_general(p.astype(BF16), do_g.astype(BF16), (((0,), (0,)), ((), ())),
                                         preferred_element_type=F32)
                cols = slice(g * HEAD_DIM, (g + 1) * HEAD_DIM)
                for ref, full in ((dk_ref, dk_all), (dv_ref, dv_all)):
                    ref[0:BLOCK, cols] += full[0:BLOCK]
                    ref[prev_rows, cols] += full[BLOCK:2 * BLOCK]
                    ref[cur_rows, cols] += full[2 * BLOCK:]
                sink_part = jnp.exp(sink - lse_g) * delta
                for n, h in enumerate(heads):
                    dq_ref[at, h * HEAD_DIM:(h + 1) * HEAD_DIM] = dq[n * BLOCK:(n + 1) * BLOCK]
                    dsink = -jnp.sum(sink_part[n * BLOCK:(n + 1) * BLOCK], axis=0, keepdims=True)
                    ds_ref[...] += jnp.where(lane == h, dsink, 0.0)

    qspec = pl.BlockSpec((rows, Q_HEADS * HEAD_DIM), lambda i: (i, 0))
    whole = pl.BlockSpec((lp, 2 * HEAD_DIM), lambda i: (0, 0))
    return pl.pallas_call(
        body, name=name, grid=(nblk // ATTN_SUB,),
        in_specs=_attn_specs(v_col) + [qspec, pl.BlockSpec((rows, Q_HEADS), lambda i: (i, 0)), qspec],
        out_specs=[qspec, whole, whole, pl.BlockSpec((1, Q_HEADS), lambda i: (0, 0))],
        out_shape=[jax.ShapeDtypeStruct((nblk * BLOCK, Q_HEADS * HEAD_DIM), F32),
                   jax.ShapeDtypeStruct((lp, 2 * HEAD_DIM), F32), jax.ShapeDtypeStruct((lp, 2 * HEAD_DIM), F32),
                   jax.ShapeDtypeStruct((1, Q_HEADS), F32)],
        compiler_params=_cparams(("arbitrary",)),
    )(*[q] * ATTN_SUB, *[k] * (ATTN_SUB + 2), *[v] * (ATTN_SUB + 2), sinks, o, lse, do)


N_VEC = 5
VEC_ROWS = N_VEC * HEAD_DIM


def _selectors():
    t = SCAN_T
    shape = (t, 2 * LANES, RW_DIM)
    step, src, dst = [lax.broadcasted_iota(jnp.int32, shape, d) for d in range(3)]
    src = src % LANES
    spread = ((src // t == dst // HEAD_DIM) & (src % t == step)).astype(BF16)
    shape = (t, RW_DIM, LANES)
    step, src, dst = [lax.broadcasted_iota(jnp.int32, shape, d) for d in range(3)]
    collect = ((src // HEAD_DIM == dst // t) & (dst % t == step)).astype(BF16)
    return spread, collect


def _rows_to_xt(x):
    low = lax.broadcasted_iota(jnp.int32, (SCAN_T, LANES), 1) < HEAD_DIM
    pieces = []
    for m in range(RW_HEADS // 2):
        pair = x[:, m * LANES:(m + 1) * LANES]
        pieces += [jnp.where(low, pair, 0.0), jnp.where(low, pltpu.roll(pair, HEAD_DIM, 1), 0.0)]
    return jnp.concatenate(pieces, axis=0).T[:HEAD_DIM]


def _xt_to_rows(a):
    t = SCAN_T
    a_t = jnp.concatenate([a, jnp.zeros_like(a)], axis=0).T
    pairs = [a_t[2 * m * t:(2 * m + 1) * t] + pltpu.roll(a_t[(2 * m + 1) * t:(2 * m + 2) * t], HEAD_DIM, 1)
             for m in range(RW_HEADS // 2)]
    return jnp.concatenate(pairs, axis=1)


def _with_exchange(compute, n_in, n_out, n_scratch, scattered, shared, grid):
    n_sc = len(scattered)
    n_x = n_sc + len(shared)
    if n_x == 0:
        return compute

    def body(*refs):
        ins, x_in = refs[:n_in], refs[n_in:n_in + n_x]
        outs, x_out = refs[n_in + n_x:n_in + n_x + n_out], refs[n_in + n_x + n_out:n_in + 2 * n_x + n_out]
        scratch = refs[n_in + 2 * n_x + n_out:n_in + 2 * n_x + n_out + n_scratch]
        sems = refs[n_in + 2 * n_x + n_out + n_scratch:]

        first = last = True
        for d, size in enumerate(grid):
            first = first & (pl.program_id(d) == 0)
            last = last & (pl.program_id(d) == size - 1)

        @pl.when(first)
        def _():
            for cp in _exchange_copies(x_in, x_out, n_sc, *sems):
                cp.start()

        compute(*ins, *outs, *scratch)

        @pl.when(last)
        def _():
            for cp in _exchange_copies(x_in, x_out, n_sc, *sems):
                cp.wait()

    return body


def _wkv_fwd(xt, v, spread, attn, name, shared=()):
    t_steps = SCAN_T
    nch = xt.shape[0]
    per = SCAN_CHUNKS
    grid = (nch // per,)
    rows = per * t_steps
    assert rows == BLOCK
    n_x = len(shared)
    q, k, v_arr, v_col, sinks = attn

    def compute(xt_ref, v_ref, sel_ref, q_ref, km_ref, kp_ref, kc_ref, vm_ref, vp_ref, vc_ref, s_ref,
                y_ref, hist_ref, o_ref, lse_ref, st_ref):
        @pl.when(pl.program_id(0) == 0)
        def _():
            st_ref[...] = jnp.zeros_like(st_ref)

        _attn_fwd_block(pl.program_id(0) - 1, q_ref, (km_ref, kp_ref, kc_ref), (vm_ref, vp_ref, vc_ref), s_ref,
                        o_ref, lse_ref)
        st = st_ref[...]
        for c in range(per):
            x2 = _split2(xt_ref[c])
            for j in range(t_steps):
                row = c * t_steps + j
                cols = jnp.dot(x2, sel_ref[j], preferred_element_type=F32)
                a_c, w_c, b_c, k_c, r_c = [cols[n * HEAD_DIM:(n + 1) * HEAD_DIM] for n in range(N_VEC)]
                hist_ref[row] = st
                sa = jnp.sum(st * a_c, axis=0, keepdims=True)
                st = st * w_c + b_c * sa + k_c * v_ref[row:row + 1, :]
                y_ref[row:row + 1, :] = jnp.sum(st * r_c, axis=0, keepdims=True)
        st_ref[...] = st

    before = lambda c: jnp.maximum(c - 1, 0)
    kv = lambda col: [pl.BlockSpec((BLOCK, 2 * HEAD_DIM), lambda c: (0, col)),
                      pl.BlockSpec((BLOCK, 2 * HEAD_DIM), lambda c: (before(c), col)),
                      pl.BlockSpec((BLOCK, 2 * HEAD_DIM), lambda c: (c, col))]
    seq = (nch * t_steps) - BLOCK
    return pl.pallas_call(
        _with_exchange(compute, 11, 4, 1, (), shared, grid), name=name, grid=grid,
        in_specs=[pl.BlockSpec((per, VEC_ROWS, LANES), lambda c: (c, 0, 0)),
                  pl.BlockSpec((rows, RW_DIM), lambda c: (c, 0)),
                  pl.BlockSpec(spread.shape, lambda c: (0, 0, 0)),
                  pl.BlockSpec((BLOCK, Q_HEADS * HEAD_DIM), lambda c: (c, 0))] + kv(0) + kv(v_col)
                 + [pl.BlockSpec((1, Q_HEADS), lambda c: (0, 0))] + [ANY] * n_x,
        out_specs=[pl.BlockSpec((rows, RW_DIM), lambda c: (c, 0)),
                   pl.BlockSpec((rows, HEAD_DIM, RW_DIM), lambda c: (c, 0, 0)),
                   pl.BlockSpec((BLOCK, Q_HEADS * HEAD_DIM), lambda c: (before(c), 0)),
                   pl.BlockSpec((BLOCK, Q_HEADS), lambda c: (before(c), 0))] + [ANY] * n_x,
        out_shape=[jax.ShapeDtypeStruct((nch * t_steps, RW_DIM), F32),
                   jax.ShapeDtypeStruct((nch * t_steps, HEAD_DIM, RW_DIM), F32),
                   jax.ShapeDtypeStruct((seq, Q_HEADS * HEAD_DIM), F32),
                   jax.ShapeDtypeStruct((seq, Q_HEADS), F32)] + _exchange_shapes((), shared),
        scratch_shapes=[pltpu.VMEM((HEAD_DIM, RW_DIM), F32)] + (_exchange_sems(n_x) if n_x else []),
        compiler_params=_cparams(("arbitrary",), SCAN_VMEM_LIMIT),
    )(xt, v, spread, q, k, k, k, v_arr, v_arr, v_arr, sinks, *shared)


def _wkv_bwd(xt, v, hist, dy, spread, collect, name, scattered=()):
    t_steps = SCAN_T
    nch = xt.shape[0]
    n_x = len(scattered)
    per = SCAN_CHUNKS
    nsteps = nch // per
    grid = (nsteps,)
    rows = per * t_steps
    lead = BLOCK // rows

    def compute(xt_ref, v_ref, hist_ref, dy_ref, sel_ref, col_ref, dxt_ref, dv_ref, g_ref):
        @pl.when(pl.program_id(0) == 0)
        def _():
            g_ref[...] = jnp.zeros_like(g_ref)

        has_dy = nsteps - 1 - pl.program_id(0) >= lead
        gst = g_ref[...]
        nxt = None
        for c in reversed(range(per)):
            x2 = _split2(xt_ref[c])
            acc = jnp.zeros((VEC_ROWS, LANES), F32)
            for j in reversed(range(t_steps)):
                row = c * t_steps + j
                cols = jnp.dot(x2, sel_ref[j], preferred_element_type=F32)
                a_c, w_c, b_c, k_c, r_c = [cols[n * HEAD_DIM:(n + 1) * HEAD_DIM] for n in range(N_VEC)]
                prev = hist_ref[row]
                v_row = v_ref[row:row + 1, :]
                dy_row = jnp.where(has_dy, dy_ref[row:row + 1, :], 0.0)
                sa = jnp.sum(prev * a_c, axis=0, keepdims=True)
                if nxt is None:
                    nxt = prev * w_c + b_c * sa + k_c * v_row
                gst = gst + r_c * dy_row
                dv_ref[row:row + 1, :] = jnp.sum(gst * k_c, axis=0, keepdims=True)
                dsa = jnp.sum(gst * b_c, axis=0, keepdims=True)
                prods = jnp.concatenate([p.astype(BF16) for p in
                                         (prev * dsa, gst * prev, gst * sa, gst * v_row, nxt * dy_row)], axis=0)
                acc = acc + jnp.dot(prods, col_ref[j], preferred_element_type=F32)
                gst = gst * w_c + a_c * dsa
                nxt = prev
            dxt_ref[c] = acc
        g_ref[...] = gst

    rev3 = lambda c: (nsteps - 1 - c, 0, 0)
    rev2 = lambda c: (nsteps - 1 - c, 0)
    rowspec = pl.BlockSpec((rows, RW_DIM), rev2)
    return pl.pallas_call(
        _with_exchange(compute, 6, 2, 1, scattered, (), grid), name=name, grid=grid,
        in_specs=[pl.BlockSpec((per, VEC_ROWS, LANES), rev3), rowspec,
                  pl.BlockSpec((rows, HEAD_DIM, RW_DIM), rev3),
                  pl.BlockSpec((rows, RW_DIM), lambda c: (jnp.maximum(nsteps - 1 - c - lead, 0), 0)),
                  pl.BlockSpec(spread.shape, lambda c: (0, 0, 0)),
                  pl.BlockSpec(collect.shape, lambda c: (0, 0, 0))] + [ANY] * n_x,
        out_specs=[pl.BlockSpec((per, VEC_ROWS, LANES), rev3), rowspec] + [ANY] * n_x,
        out_shape=[jax.ShapeDtypeStruct((nch, VEC_ROWS, LANES), F32),
                   jax.ShapeDtypeStruct((nch * t_steps, RW_DIM), F32)] + _exchange_shapes(scattered, ()),
        scratch_shapes=[pltpu.VMEM((HEAD_DIM, RW_DIM), F32)] + (_exchange_sems(n_x) if n_x else []),
        compiler_params=_cparams(("arbitrary",), SCAN_VMEM_LIMIT),
    )(xt, v, hist, dy, spread, collect, *scattered)


MESH = pl.DeviceIdType.MESH
ANY = pl.BlockSpec(memory_space=pltpu.HBM)


def _all_gather(arrays, name):
    n_arr = len(arrays)
    per = N_DEV - 1

    def body(*refs):
        x_refs, out_refs = refs[:n_arr], refs[n_arr:2 * n_arr]
        send_sems, recv_sems, local_sems = refs[2 * n_arr:]
        xi, yi, ci = lax.axis_index("x"), lax.axis_index("y"), lax.axis_index("c")
        me, sibling = (xi, yi, ci), (xi, yi, 1 - ci)
        chips = [(1 - xi, yi), (xi, 1 - yi), (1 - xi, 1 - yi)]

        def slot(a, px, py, pc):
            return out_refs[a].at[4 * px + 2 * py + pc]

        def copy(a, sem, block, to, src=None):
            return pltpu.make_async_remote_copy(
                src_ref=slot(a, *block) if src is None else src, dst_ref=slot(a, *block),
                send_sem=send_sems.at[per * a + sem], recv_sem=recv_sems.at[per * a + sem],
                device_id=to, device_id_type=MESH)

        mine = [pltpu.make_async_copy(x_refs[a], slot(a, *me), local_sems.at[a]) for a in range(n_arr)]
        for cp in mine:
            cp.start()
        sent = []
        for a in range(n_arr):
            sent.append(copy(a, 0, me, sibling, src=x_refs[a]))
            sent += [copy(a, 1 + j, me, (*chip, ci), src=x_refs[a]) for j, chip in enumerate(chips)]
        for cp in sent:
            cp.start()
        for j, chip in enumerate(chips):
            for a in range(n_arr):
                copy(a, 1 + j, (*chip, ci), me).wait_recv()
                onward = copy(a, 4 + j, (*chip, ci), sibling)
                onward.start()
                sent.append(onward)
        for a in range(n_arr):
            copy(a, 0, sibling, me).wait_recv()
        for j, chip in enumerate(chips):
            for a in range(n_arr):
                copy(a, 4 + j, (*chip, 1 - ci), me).wait_recv()
        for cp in sent:
            cp.wait_send()
        for cp in mine:
            cp.wait()

    sems = pltpu.SemaphoreType.DMA((per * n_arr,))
    return pl.pallas_call(
        body, name=name, out_shape=[jax.ShapeDtypeStruct((N_DEV,) + a.shape, a.dtype) for a in arrays],
        in_specs=[ANY] * n_arr, out_specs=[ANY] * n_arr,
        scratch_shapes=[sems, sems, pltpu.SemaphoreType.DMA((n_arr,))],
    )(*arrays)


def _exchange_copies(in_refs, out_refs, n_scattered, send_sems, recv_sems, local_sems):
    n_arr = len(in_refs)
    per = N_DEV - 1
    xi, yi, ci = lax.axis_index("x"), lax.axis_index("y"), lax.axis_index("c")
    me = 4 * xi + 2 * yi + ci
    src_of = lambda a, peer: in_refs[a].at[peer] if a < n_scattered else in_refs[a]
    copies = []
    for d in range(1, N_DEV):
        px = 1 - xi if d & 4 else xi
        py = 1 - yi if d & 2 else yi
        pc = 1 - ci if d & 1 else ci
        for a in range(n_arr):
            copies.append(pltpu.make_async_remote_copy(
                src_ref=src_of(a, 4 * px + 2 * py + pc), dst_ref=out_refs[a].at[me],
                send_sem=send_sems.at[per * a + d - 1], recv_sem=recv_sems.at[per * a + d - 1],
                device_id=(px, py, pc), device_id_type=MESH))
    own = [pltpu.make_async_copy(src_of(a, me), out_refs[a].at[me], local_sems.at[a]) for a in range(n_arr)]
    return copies + own


def _exchange_shapes(scattered, shared):
    return ([jax.ShapeDtypeStruct(a.shape, a.dtype) for a in scattered]
            + [jax.ShapeDtypeStruct((N_DEV,) + a.shape, a.dtype) for a in shared])


def _exchange_sems(n_arr):
    sems = pltpu.SemaphoreType.DMA(((N_DEV - 1) * n_arr,))
    return [sems, sems, pltpu.SemaphoreType.DMA((n_arr,))]


def _exchange(scattered, shared, name):
    n_sc = len(scattered)
    n_arr = n_sc + len(shared)

    def body(*refs):
        copies = _exchange_copies(refs[:n_arr], refs[n_arr:2 * n_arr], n_sc, *refs[2 * n_arr:])
        for cp in copies:
            cp.start()
        for cp in copies:
            cp.wait()

    return pl.pallas_call(
        body, name=name, out_shape=_exchange_shapes(scattered, shared),
        in_specs=[ANY] * n_arr, out_specs=[ANY] * n_arr, scratch_shapes=_exchange_sems(n_arr),
    )(*scattered, *shared)


def _adam_math(g, w, m, v):
    m_new = ADAM_B1 * m + (1.0 - ADAM_B1) * g
    v_new = ADAM_B2 * v + (1.0 - ADAM_B2) * (g * g)
    m_hat = m_new / (1.0 - ADAM_B1 ** ADAM_STEP)
    v_hat = v_new / (1.0 - ADAM_B2 ** ADAM_STEP)
    return -ADAM_LR * (m_hat / (jnp.sqrt(v_hat) + ADAM_EPS) + ADAM_WD * w), m_new, v_new


def _slot_sum(p_ref):
    g = p_ref[0].astype(F32)
    for s in range(1, N_DEV):
        g = g + p_ref[s].astype(F32)
    return g


def _adamw_replicated(parts, ws, ms, vs, loss_parts, name):
    n = len(ws)

    def body(*refs):
        p_refs, w_refs, m_refs, v_refs = refs[:n], refs[n:2 * n], refs[2 * n:3 * n], refs[3 * n:4 * n]
        outs = refs[4 * n + 1:]
        for j in range(n):
            g = _slot_sum(p_refs[j])
            outs[4 * j][...] = g
            for o_ref, val in zip(outs[4 * j + 1:4 * j + 4], _adam_math(g, w_refs[j][...], m_refs[j][...],
                                                                        v_refs[j][...])):
                o_ref[...] = val
        outs[4 * n][...] = _slot_sum(refs[4 * n])

    whole = pl.BlockSpec(memory_space=pltpu.VMEM)
    out = pl.pallas_call(
        body, name=name, in_specs=[whole] * (4 * n + 1), out_specs=[whole] * (4 * n + 1),
        out_shape=[jax.ShapeDtypeStruct(w.shape, F32) for w in ws for _ in range(4)]
                  + [jax.ShapeDtypeStruct(loss_parts.shape[1:], F32)],
    )(*parts, *ws, *ms, *vs, loss_parts)
    return [out[4 * j:4 * j + 4] for j in range(n)], out[4 * n]


def _adamw(parts, w, m, v, name):
    rows, cols = w.shape[-2:]
    tile = PACK_ROWS if rows % PACK_ROWS == 0 else rows
    at = (0,) if w.ndim == 3 else (Ellipsis,)

    def body(p_ref, w_ref, m_ref, v_ref, g_out, d_out, m_out, v_out):
        g = _slot_sum(p_ref)
        g_out[at] = g
        d_out[at], m_out[at], v_out[at] = _adam_math(g, w_ref[at], m_ref[at], v_ref[at])

    spec = (pl.BlockSpec((1, tile, cols), lambda i: (0, i, 0)) if w.ndim == 3
            else pl.BlockSpec((tile, cols), lambda i: (i, 0)))
    return pl.pallas_call(
        body, name=name, grid=(rows // tile,),
        in_specs=[pl.BlockSpec((N_DEV, tile, cols), lambda i: (0, i, 0)), spec, spec, spec],
        out_specs=[spec] * 4, out_shape=[jax.ShapeDtypeStruct(w.shape, F32)] * 4,
        compiler_params=_cparams(("parallel",)),
    )(parts, w, m, v)


EARLY = [("meta_tokens", 1), ("rwkv_w2", 1), ("rwkv_a2", 1), ("rwkv_g2", 1)]
LATE = [("w_br_attn", 1), ("w_br_rwkv", 1), ("w_o", 0), ("w_ffn_gate", 1), ("w_ffn_up", 1), ("w_ffn_down", 0)]
REPLICATED = ["norm_mix_g", "b_in", "attn_sinks", "rwkv_mix", "rwkv_w0", "rwkv_a0", "rwkv_k_k", "rwkv_k_a",
              "rwkv_r_k", "rwkv_ln_w", "rwkv_ln_b", "norm_ffn_g", "norm_final_g"]
WEIGHTS = ["meta_tokens", "norm_mix_g", "w_in", "b_in", "attn_sinks", "rwkv_mix", "rwkv_w0", "rwkv_w2", "rwkv_a0",
           "rwkv_a2", "rwkv_g2", "rwkv_k_k", "rwkv_k_a", "rwkv_r_k", "rwkv_ln_w", "rwkv_ln_b", "w_br_attn",
           "w_br_rwkv", "w_o", "norm_ffn_g", "w_ffn_gate", "w_ffn_up", "w_ffn_down", "norm_final_g"]


def _strip(name, a):
    return a if name in ("meta_tokens", "norm_final_g") else a[0]


def _join(gathered, axis):
    if axis == 0:
        return gathered.reshape(-1, gathered.shape[2])
    return gathered.transpose(1, 0, 2).reshape(gathered.shape[1], -1)


def _split(g, axis):
    if axis == 0:
        return g.reshape(N_DEV, -1, g.shape[1])
    return g.reshape(g.shape[0], N_DEV, -1).transpose(1, 0, 2)


W_IN_LAYOUT = [(2592, 4640), (768, 2304), (2432, 2592), 256 - GATE_LORA, (2304, 2368), 128 - DECAY_LORA,
               (2368, 2432), 128 - AAA_LORA, (0, 512), (512, 768), NP - C_VA - 128]


def _w_in_padded(w, shard_width=None):
    rows = w.shape[-2]
    width = D_IN if shard_width is None else shard_width
    parts = []
    for seg in W_IN_LAYOUT:
        if isinstance(seg, int):
            parts.append(jnp.zeros((rows, seg), w.dtype))
            continue
        lo, stop = seg
        while lo < stop:
            p = lo // width
            hi = min(stop, (p + 1) * width)
            src = w if shard_width is None else w[p]
            parts.append(src[:, lo - p * width:hi - p * width])
            lo = hi
    return jnp.concatenate(parts, axis=1)


def _w_in_unpadded(wp, lo=0, stop=D_IN):
    spans, pos = [], 0
    for seg in W_IN_LAYOUT:
        if isinstance(seg, int):
            pos += seg
        else:
            spans.append((seg[0], seg[1], pos))
            pos += seg[1] - seg[0]
    parts = []
    for a, b, at in sorted(spans):
        c, d = max(a, lo), min(b, stop)
        if c < d:
            parts.append(wp[:, at + c - a:at + d - a])
    return jnp.concatenate(parts, axis=1)


def _pad_rows(a, n):
    return jnp.pad(a, ((0, n - a.shape[0]), (0, 0)))


def _device_step(x, tgt, full, gather_late=None, scatter_early=None, scatter_last=None):
    seq = x.shape[0]
    nblk = seq // BLOCK
    lp = seq + BLOCK
    nall = nblk + 1

    w_in_p = full["w_in_p"]
    b_in_p = _w_in_padded(full["b_in"][None])
    mix = full["rwkv_mix"][None]
    mix_r, mix_k, mix_v = mix[:, 0:512], mix[:, 512:1024], mix[:, 1024:1536]
    mix_dw = jnp.pad(mix[:, 1536:1600], ((0, 0), (0, 64)))
    mix_da = jnp.pad(mix[:, 1600:1664], ((0, 0), (0, 64)))
    mix_dg = jnp.pad(mix[:, 1664:1824], ((0, 0), (0, 96)))
    w2_p = _pad_rows(full["rwkv_w2"].astype(F32), 128)
    a2_p = _pad_rows(full["rwkv_a2"].astype(F32), 128)
    g2_p = _pad_rows(full["rwkv_g2"].astype(F32), 256)
    row = lambda name: full[name].reshape(1, -1)
    sinks = row("attn_sinks")
    rope_c, rope_s1, rope_s2 = _rope_tables(lp)

    hpad = jnp.concatenate([jnp.zeros((PAD_ROWS, D_MODEL), F32), full["meta_tokens"].astype(F32), x], axis=0)
    (u,) = _rows_fwd(_rms_fn, [_view(hpad)], [row("norm_mix_g")], [D_MODEL], nblk=nall, name="norm_mix",
                     out_dtype=BF16, tile=TILE_ALL_BIG)
    proj = _mm(u, w_in_p, bias=b_in_p, name="in_proj")
    (q_r,) = _rows_fwd(_rope_fwd_fn, [_view(proj, 512, C_Q // 512), _view(rope_c), _view(rope_s1), _view(rope_s2)],
                       [], [512], nblk=nall, name="rope_q", tile=TILE_ALL_BIG)
    (k_r,) = _rows_fwd(_rope_fwd_fn, [_view(proj, 128, C_KA // 128), _view(rope_c), _view(rope_s1),
                                      _view(rope_s2)], [], [128], nblk=nall, name="rope_k", tile=TILE_ALL_BIG)

    rw_prev = jnp.pad(proj[:-1, C_R:C_R + 2048], ((1, 0), (0, 0)))
    pre_rows = [_view(proj, 512, C_R // 512), _view(proj, 512, C_K // 512), _view(proj, 512, C_V // 512),
                _view(proj, 128, C_DW // 128),
                _view(proj, 128, C_DA // 128), _view(proj, 256, C_DG // 256),
                _view(rw_prev, 512, 0), _view(rw_prev, 512, 1), _view(rw_prev, 512, 2), _view(rw_prev, 128, 14),
                _view(rw_prev, 128, 15), _view(rw_prev, 256, 6)]
    pre_consts = [mix_r, mix_k, mix_v, mix_dw, mix_da, mix_dg, row("rwkv_w0"), w2_p, row("rwkv_a0"), a2_p, g2_p,
                  row("rwkv_k_k"), row("rwkv_k_a")]
    xt_block = (BLOCK // SCAN_T * VEC_ROWS, LANES)
    r_t, k_mod, v_t, gate, xt = _rows_fwd(_rwkv_pre_xt_fn, pre_rows, pre_consts, [RW_DIM] * 4 + [xt_block],
                                          nblk=nall, name="rwkv_pre")
    xt = xt.reshape(-1, VEC_ROWS, LANES)
    spread, collect = _selectors()
    y_scan, hist, o_attn, lse, *late = _wkv_fwd(xt, v_t, spread, (q_r, k_r, proj, C_VA // 128, sinks), name="wkv_fwd",
                                                shared=gather_late[0] if gather_late else ())
    if gather_late:
        full = {**full, **gather_late[1](late)}
    post_rows = [_view(y_scan, off=1), _view(r_t, off=1), _view(k_mod, off=1), _view(v_t, off=1), _view(gate, off=1)]
    post_consts = [row("rwkv_ln_w"), row("rwkv_ln_b"), row("rwkv_r_k")]
    (y_rwkv,) = _rows_fwd(_rwkv_post_fn, post_rows, post_consts, [RW_DIM], nblk=nblk, name="rwkv_post",
                          out_dtype=BF16)

    ya = _mm(o_attn, full["w_br_attn"], name="br_attn")
    yr = _mm(y_rwkv, full["w_br_rwkv"], name="br_rwkv")
    merge_rows = [_view(ya), _view(yr), _view(proj, 1024, C_G1 // 1024, 1), _view(proj, 1024, C_G2 // 1024, 1)]
    (merged,) = _rows_fwd(_merge_fn, merge_rows, [], [D_MODEL], nblk=nblk, name="merge", out_dtype=BF16)
    h1 = _mm(merged, full["w_o"], residual=x, name="out_proj")
    (f,) = _rows_fwd(_rms_fn, [_view(h1)], [row("norm_ffn_g")], [D_MODEL], nblk=nblk, name="norm_ffn",
                     out_dtype=BF16, tile=TILE_REAL_BIG)
    ff_gate = _mm(f, full["w_ffn_gate"], name="ffn_gate")
    ff_up = _mm(f, full["w_ffn_up"], name="ffn_up")
    (act,) = _rows_fwd(_swiglu_fn, [_view(ff_gate), _view(ff_up)], [], [D_FF], nblk=nblk, name="swiglu",
                       out_dtype=BF16, tile=TILE_WIDE_BIG)
    h2 = _mm(act, full["w_ffn_down"], residual=h1, name="ffn_down")

    grads = {}
    ones_col = jnp.ones((seq, 1), F32)
    loss_rows, dh2, grads["norm_final_g"] = _rows_bwd(
        _loss_fn, [_view(h2), _view(tgt)], [row("norm_final_g")], [_view(ones_col)], nblk=nblk, name="loss",
        diff_rows=[0], diff_consts=[0], fwd_widths=[1], tile=TILE_REAL_BIG)
    loss = jnp.sum(loss_rows)

    dact = _mm(dh2, full["w_ffn_down"], tb=True, name="d_act")
    grads["w_ffn_down"] = _mm(act, dh2, ta=True, name="dw_ffn_down")
    dgate, dup = _rows_bwd(_swiglu_fn, [_view(ff_gate), _view(ff_up)], [], [_view(dact)], nblk=nblk,
                           name="swiglu_bwd", diff_rows=[0, 1], diff_consts=[], row_dtype=BF16, tile=TILE_WIDE)
    grads["w_ffn_gate"] = _mm(f, dgate, ta=True, name="dw_ffn_gate")
    grads["w_ffn_up"] = _mm(f, dup, ta=True, name="dw_ffn_up")
    df = _mm(dgate, full["w_ffn_gate"], tb=True, name="df_gate")
    df = _mm(dup, full["w_ffn_up"], tb=True, residual=df, name="df_up")
    dh1, grads["norm_ffn_g"] = _rows_bwd(_rms_fn, [_view(h1)], [row("norm_ffn_g")], [_view(df)], nblk=nblk,
                                         name="norm_ffn_bwd", diff_rows=[0], diff_consts=[0], acc=[_view(dh2)],
                                         tile=TILE_REAL)
    dmerged = _mm(dh1, full["w_o"], tb=True, name="d_merged")
    grads["w_o"] = _mm(merged, dh1, ta=True, name="dw_o")
    dya, dyr, dg1, dg2 = _rows_bwd(_merge_fn, merge_rows, [], [_view(dmerged)], nblk=nblk, name="merge_bwd",
                                   diff_rows=[0, 1, 2, 3], diff_consts=[], row_dtype=BF16)
    grads["w_br_attn"] = _mm(o_attn, dya, ta=True, name="dw_br_attn")
    grads["w_br_rwkv"] = _mm(y_rwkv, dyr, ta=True, name="dw_br_rwkv")
    dy_attn = _mm(dya, full["w_br_attn"], tb=True, name="d_y_attn")
    dy_rwkv = _mm(dyr, full["w_br_rwkv"], tb=True, name="d_y_rwkv")

    post = _rows_bwd(_rwkv_post_fn, post_rows, post_consts, [_view(dy_rwkv)], nblk=nblk, name="rwkv_post_bwd",
                     diff_rows=[0, 1, 2, 3, 4], diff_consts=[0, 1, 2])
    dys, dr_post, dk_post, dv_post, dgate_post = post[:5]
    grads["rwkv_ln_w"], grads["rwkv_ln_b"], grads["rwkv_r_k"] = post[5:]
    dxt, dv_s, *early_parts = _wkv_bwd(xt, v_t, hist, dys, spread, collect, name="wkv_bwd",
                                       scattered=scatter_early(grads) if scatter_early else ())
    pre_cts = [_view(dxt.reshape(-1, LANES), rows=xt_block[0]), _view(dv_s)] + [
        _view(t, off=-1) for t in (dr_post, dk_post, dv_post, dgate_post)]
    pre = _rows_bwd(_rwkv_pre_fn, pre_rows, pre_consts, pre_cts, nblk=nall, name="rwkv_pre_bwd",
                    diff_rows=list(range(12)), diff_consts=list(range(13)), ct_map=_rwkv_pre_cts)
    d_cur, d_prev, d_par = pre[0:6], pre[6:12], pre[12:]
    up = lambda t: jnp.pad(t[1:], ((0, 1), (0, 0)))
    d_rw = [c + up(p) for c, p in zip(d_cur, d_prev)]
    grads["rwkv_mix"] = jnp.concatenate([d_par[0], d_par[1], d_par[2], d_par[3][:, :DECAY_LORA],
                                         d_par[4][:, :AAA_LORA], d_par[5][:, :GATE_LORA]], axis=1)
    grads["rwkv_w0"], grads["rwkv_w2"] = d_par[6], d_par[7][:DECAY_LORA]
    grads["rwkv_a0"], grads["rwkv_a2"] = d_par[8], d_par[9][:AAA_LORA]
    grads["rwkv_g2"] = d_par[10][:GATE_LORA]
    grads["rwkv_k_k"], grads["rwkv_k_a"] = d_par[11], d_par[12]

    dq_real, dk_r, dva, grads["attn_sinks"] = _attn_bwd(q_r, k_r, proj, C_VA // 128, sinks, o_attn, lse, dy_attn,
                                                        nblk=nblk, name="attn_bwd")
    dq_r = jnp.pad(dq_real, ((BLOCK, 0), (0, 0)))
    (dq,) = _rows_fwd(_rope_bwd_fn, [_view(dq_r), _view(rope_c), _view(rope_s1), _view(rope_s2)], [], [512],
                      nblk=nall, name="rope_q_bwd", out_dtype=BF16, tile=TILE_ALL_BIG)
    (dka,) = _rows_fwd(_rope_bwd_fn, [_view(dk_r), _view(rope_c), _view(rope_s1),
                                      _view(rope_s2)], [], [128], nblk=nall, name="rope_k_bwd", out_dtype=BF16,
                       tile=TILE_ALL_BIG)

    lead = lambda t: jnp.pad(t, ((BLOCK, 0), (0, 0)))
    pieces = [lead(dg1), lead(dg2), d_rw[0], d_rw[1], d_rw[2], d_rw[5], d_rw[3], d_rw[4], dq, dka, dva,
              jnp.zeros((lp, NP - C_VA - 128), BF16)]
    dproj = jnp.concatenate([p.astype(BF16) for p in pieces], axis=1)
    grads["w_in_p"] = _mm(u, dproj, ta=True, name="dw_in")
    grads["b_in"] = _w_in_unpadded(_colsum(dproj, name="db_in"))
    du, *last_parts = _mm(dproj, w_in_p, tb=True, name="d_u", scattered=scatter_last(grads)) if scatter_last else (
        _mm(dproj, w_in_p, tb=True, name="d_u"),)
    dh, grads["norm_mix_g"] = _rows_bwd(_rms_fn, [_view(hpad)], [row("norm_mix_g")], [_view(du)], nblk=nall,
                                        name="norm_mix_bwd", diff_rows=[0], diff_consts=[0], acc=[_view(lead(dh1))],
                                        tile=TILE_ALL)
    grads["meta_tokens"] = dh[PAD_ROWS:BLOCK]
    return loss, dh[BLOCK:], grads, early_parts, last_parts


def kernel(x, meta_tokens, norm_mix_g, w_in, b_in, attn_sinks, rwkv_mix, rwkv_w0, rwkv_w2, rwkv_a0, rwkv_a2, rwkv_g2, rwkv_k_k, rwkv_k_a, rwkv_r_k, rwkv_ln_w, rwkv_ln_b, w_br_attn, w_br_rwkv, w_o, norm_ffn_g, w_ffn_gate, w_ffn_up, w_ffn_down, norm_final_g, loss_target, m_meta_tokens, m_norm_mix_g, m_w_in, m_b_in, m_attn_sinks, m_rwkv_mix, m_rwkv_w0, m_rwkv_w2, m_rwkv_a0, m_rwkv_a2, m_rwkv_g2, m_rwkv_k_k, m_rwkv_k_a, m_rwkv_r_k, m_rwkv_ln_w, m_rwkv_ln_b, m_w_br_attn, m_w_br_rwkv, m_w_o, m_norm_ffn_g, m_w_ffn_gate, m_w_ffn_up, m_w_ffn_down, m_norm_final_g, v_meta_tokens, v_norm_mix_g, v_w_in, v_b_in, v_attn_sinks, v_rwkv_mix, v_rwkv_w0, v_rwkv_w2, v_rwkv_a0, v_rwkv_a2, v_rwkv_g2, v_rwkv_k_k, v_rwkv_k_a, v_rwkv_r_k, v_rwkv_ln_w, v_rwkv_ln_b, v_w_br_attn, v_w_br_rwkv, v_w_o, v_norm_ffn_g, v_w_ffn_gate, v_w_ffn_up, v_w_ffn_down, v_norm_final_g):
    given = dict(locals())
    wts = {n: _strip(n, given[n]) for n in WEIGHTS}
    as_rows = lambda a: a.reshape(1, -1) if a.ndim == 1 else a
    width = wts["w_in"].shape[1]
    wire = lambda table: [wts[n].astype(BF16) for n, _ in table]

    w_in_all, *early_all = _all_gather([wts["w_in"].astype(BF16)] + wire(EARLY), name="gather_weights")
    full = {n: wts[n] for n in REPLICATED}
    full.update({n: _join(g, axis) for (n, axis), g in zip(EARLY, early_all)})
    full["w_in_p"] = _w_in_padded(w_in_all, shard_width=width)
    gather_late = (wire(LATE), lambda got: {n: _join(g, axis) for (n, axis), g in zip(LATE, got)})
    scatter_early = lambda g: [_split(g[n], axis).astype(BF16) for n, axis in LATE]
    scatter_last = lambda g: [jnp.stack([_w_in_unpadded(g["w_in_p"], p * width, (p + 1) * width)
                                         for p in range(N_DEV)]).astype(BF16)]

    loss_part, grad_x, grads, parts_late, (parts_w_in,) = _device_step(
        x[0], loss_target[0], full, gather_late, scatter_early, scatter_last)

    g_early = [_split(grads[n], axis).astype(BF16) for n, axis in EARLY]
    g_small = [grads[n].reshape(as_rows(given[n]).shape) for n in REPLICATED] + [jnp.full((8, LANES), loss_part)]
    got = _exchange(g_early, g_small, name="exchange_grads")
    parts_early, parts_small, parts_loss = got[:len(EARLY)], got[len(EARLY):-1], got[-1]
    results = [{}, {}, {}, {}]
    for (n, _), parts in zip([("w_in", 1)] + EARLY + LATE, [parts_w_in] + list(parts_early) + list(parts_late)):
        for kind, a in enumerate(_adamw(parts, given[n], given["m_" + n], given["v_" + n], name="adamw_" + n)):
            results[kind][n] = a
    small, loss = _adamw_replicated(parts_small, *[[as_rows(given[pre + n]) for n in REPLICATED]
                                                   for pre in ("", "m_", "v_")], parts_loss, name="adamw_replicated")
    for n, four in zip(REPLICATED, small):
        for kind, a in enumerate(four):
            results[kind][n] = a
    loss = loss[0, 0]
    out = [loss, grad_x[None]]
    for kind in range(4):
        out += [results[kind][n].reshape(given[n].shape) for n in WEIGHTS]
    return tuple(out)
```

```python
import jax
import jax.numpy as jnp
from jax import lax
from jax.experimental import pallas as pl
from jax.experimental.pallas import tpu as pltpu

F32 = jnp.float32
BF16 = jnp.bfloat16

N_DEV = 8
D_MODEL = 1024
N_META = 16
BLOCK = 128
PAD_ROWS = BLOCK - N_META
HEAD_DIM = 64
Q_HEADS = 8
KV_HEADS = 2
GROUP = Q_HEADS // KV_HEADS
ROPE_DIM = HEAD_DIM // 4
ROPE_HALF = ROPE_DIM // 2
ROPE_THETA = 500000.0
RW_HEADS = 8
RW_DIM = 512
DECAY_LORA = 64
AAA_LORA = 64
GATE_LORA = 160
D_FF = 2816
D_IN = 4640
RMS_EPS = 1e-6
RWKV_LN_EPS = 64e-5
NEG_INF = -1e30
SCAN_T = 16
SCAN_CHUNKS = 8
LANES = 128
PACK_ROWS = 256
TILE_ALL = 384
TILE_ALL_BIG = 1408
TILE_REAL = 512
TILE_WIDE = 256

ADAM_LR = 0.001
ADAM_B1 = 0.9
ADAM_B2 = 0.999
ADAM_EPS = 1e-08
ADAM_WD = 0.01
ADAM_STEP = 10

C_G1, C_G2 = 0, 1024
C_R, C_K, C_V, C_DG, C_DW, C_DA = 2048, 2560, 3072, 3584, 3840, 3968
C_Q, C_KA, C_VA = 4096, 4608, 4736
NP = 5120

VMEM_LIMIT = 48 * 1024 * 1024
SCAN_VMEM_LIMIT = 60 * 1024 * 1024


def _cparams(sem, vmem=VMEM_LIMIT):
    return pltpu.CompilerParams(dimension_semantics=sem, vmem_limit_bytes=vmem)


def _pick(n, cands):
    for c in cands:
        if n % c == 0:
            return c
    raise ValueError(f"no tile for {n}")


def _mm(a, b, *, ta=False, tb=False, bias=None, residual=None, name, scattered=(), colsum_a=False):
    m = a.shape[1] if ta else a.shape[0]
    k = a.shape[0] if ta else a.shape[1]
    n = b.shape[0] if tb else b.shape[1]
    assert k == (b.shape[1] if tb else b.shape[0]), (a.shape, b.shape, ta, tb)
    tm = _pick(m, (512, 1408, 256, 128) if ta else (1056, 1024, 528, 512, 384, 256, 128))
    tn = _pick(n, (1024, 512, 1408, 256, 128))
    if k <= 1024:
        tk = k
    else:
        tk = _pick(k, (1024, 1056, 528, 512) if (ta and not tb) else (1024, 1408, 512, 256, 128))
    nk = k // tk
    has_bias = bias is not None
    has_res = residual is not None
    dn = (((0 if ta else 1,), (1 if tb else 0,)), ((), ()))

    def body(*refs):
        a_ref, b_ref = refs[0], refs[1]
        pos = 2
        bias_ref = res_ref = None
        if has_bias:
            bias_ref = refs[pos]
            pos += 1
        if has_res:
            res_ref = refs[pos]
            pos += 1
        o_ref, acc_ref = refs[pos], refs[-1]
        kk = pl.program_id(2)
        if colsum_a:
            cs_ref = refs[pos + 1]

            @pl.when((pl.program_id(1) == 0) & (pl.program_id(0) == 0))
            def _():
                cs_ref[kk] = jnp.sum(a_ref[...].astype(F32), axis=0, keepdims=True)

            @pl.when((pl.program_id(1) == 0) & (pl.program_id(0) > 0))
            def _():
                cs_ref[kk] += jnp.sum(a_ref[...].astype(F32), axis=0, keepdims=True)
        part = lax.dot_general(a_ref[...].astype(BF16), b_ref[...].astype(BF16), dn, preferred_element_type=F32)

        def finish(out):
            if has_bias:
                out = out + bias_ref[...]
            if has_res:
                out = out + res_ref[...]
            o_ref[...] = out

        if nk == 1:
            finish(part)
        else:
            @pl.when(kk == 0)
            def _():
                acc_ref[...] = part

            @pl.when((kk > 0) & (kk < nk - 1))
            def _():
                acc_ref[...] += part

            @pl.when(kk == nk - 1)
            def _():
                finish(acc_ref[...] + part)

    in_specs = [
        pl.BlockSpec((tk, tm), lambda i, j, kk: (kk, i)) if ta else pl.BlockSpec((tm, tk), lambda i, j, kk: (i, kk)),
        pl.BlockSpec((tn, tk), lambda i, j, kk: (j, kk)) if tb else pl.BlockSpec((tk, tn), lambda i, j, kk: (kk, j)),
    ]
    args = [a, b]
    if has_bias:
        in_specs.append(pl.BlockSpec((1, tn), lambda i, j, kk: (0, j)))
        args.append(bias)
    if has_res:
        in_specs.append(pl.BlockSpec((tm, tn), lambda i, j, kk: (i, j)))
        args.append(residual)
    grid = (m // tm, n // tn, nk)
    n_x = len(scattered)
    assert not (colsum_a and ta)
    sums_spec = [pl.BlockSpec((nk, 1, tk), lambda i, j, kk: (0, 0, 0))] if colsum_a else []
    sums_shape = [jax.ShapeDtypeStruct((nk, 1, tk), F32)] if colsum_a else []
    serial = n_x or colsum_a
    out = pl.pallas_call(
        _with_exchange(body, len(args), 1 + len(sums_spec), 1, scattered, (), grid), name=name, grid=grid,
        in_specs=in_specs + [ANY] * n_x,
        out_specs=[pl.BlockSpec((tm, tn), lambda i, j, kk: (i, j))] + sums_spec + [ANY] * n_x,
        out_shape=[jax.ShapeDtypeStruct((m, n), F32)] + sums_shape + _exchange_shapes(scattered, ()),
        scratch_shapes=[pltpu.VMEM((tm, tn) if nk > 1 else (8, LANES), F32)] + (_exchange_sems(n_x) if n_x else []),
        compiler_params=_cparams(("arbitrary",) * 3 if serial else ("parallel", "parallel", "arbitrary")),
    )(*args, *scattered)
    out = list(out)
    if colsum_a:
        out[1] = out[1].reshape(1, k)
    return out if (n_x or colsum_a) else out[0]


def _view(arr, width=None, col=0, off=0, rows=BLOCK):
    return (arr, arr.shape[1] if width is None else width, col, off, rows)


def _row_spec(view):
    _, width, col, off, rows = view
    if off < 0:
        return pl.BlockSpec((rows, width), lambda i, col=col, off=off: (jnp.maximum(i + off, 0), col))
    return pl.BlockSpec((rows, width), lambda i, col=col, off=off: (i + off, col))


def _const_spec(arr):
    return pl.BlockSpec(arr.shape, lambda i: (0,) * arr.ndim)


def _retile(views, tile):
    assert all(v[3] == 0 and v[4] == BLOCK for v in views)
    return [v[:4] + (tile,) for v in views]


def _rows_fwd(fn, rows, consts, out_widths, *, nblk, name, out_dtype=F32, tile=BLOCK):
    nr, nc = len(rows), len(consts)
    tile = tile if (nblk * BLOCK) % tile == 0 else BLOCK
    if tile != BLOCK:
        rows, nblk = _retile(rows, tile), nblk * BLOCK // tile
    out_blocks = [(tile, w) if isinstance(w, int) else w for w in out_widths]

    def body(*refs):
        i = pl.program_id(0)
        vals = [r[...] for r in refs[:nr + nc]]
        outs = fn(i, *vals)
        for o_ref, o in zip(refs[nr + nc:], outs):
            o_ref[...] = o.astype(o_ref.dtype)

    return pl.pallas_call(
        body, name=name, grid=(nblk,),
        in_specs=[_row_spec(v) for v in rows] + [_const_spec(c) for c in consts],
        out_specs=[pl.BlockSpec(b, lambda i: (i, 0)) for b in out_blocks],
        out_shape=[jax.ShapeDtypeStruct((nblk * r, w), out_dtype) for r, w in out_blocks],
        compiler_params=_cparams(("parallel",)),
    )(*[v[0] for v in rows], *consts)


def _rows_bwd(fn, rows, consts, cts, *, nblk, name, diff_rows, diff_consts, acc=None, fwd_widths=(), row_dtype=F32,
              ct_map=None, tile=BLOCK):
    nr, nc = len(rows), len(consts)
    acc = acc or [None] * len(diff_rows)
    tile = tile if (nblk * BLOCK) % tile == 0 else BLOCK
    if tile != BLOCK:
        rows, nblk = _retile(rows, tile), nblk * BLOCK // tile
        cts = [c if c is None else _retile([c], tile)[0] for c in cts]
        acc = [a if a is None else _retile([a], tile)[0] for a in acc]
    ct_views = [c for c in cts if c is not None]
    acc_views = [a for a in acc if a is not None]
    n_in = nr + nc + len(ct_views) + len(acc_views)
    n_fwd = len(fwd_widths)

    def body(*refs):
        i = pl.program_id(0)
        row_vals = [r[...] for r in refs[:nr]]
        const_vals = [r[...] for r in refs[nr:nr + nc]]
        ct_vals = [r[...] for r in refs[nr + nc:nr + nc + len(ct_views)]]
        acc_vals = [r[...] for r in refs[nr + nc + len(ct_views):n_in]]
        out_refs = refs[n_in:]

        def f(*dargs):
            rv = list(row_vals)
            cv = list(const_vals)
            for pos, idx in enumerate(diff_rows):
                rv[idx] = dargs[pos]
            for pos, idx in enumerate(diff_consts):
                cv[idx] = dargs[len(diff_rows) + pos]
            return tuple(fn(i, *rv, *cv))

        primals = [row_vals[idx] for idx in diff_rows] + [const_vals[idx] for idx in diff_consts]
        outs, pull = jax.vjp(f, *primals)
        full_ct, ci = [], 0
        if ct_map is not None:
            full_ct = ct_map(i, *ct_vals)
        else:
            for o, c in zip(outs, cts):
                if c is None:
                    full_ct.append(jnp.zeros_like(o))
                else:
                    full_ct.append(ct_vals[ci])
                    ci += 1
        grads = pull(tuple(full_ct))
        for o_ref, o in zip(out_refs[:n_fwd], outs):
            o_ref[...] = o
        ai = 0
        for pos in range(len(diff_rows)):
            g = grads[pos]
            if acc[pos] is not None:
                g = g + acc_vals[ai]
                ai += 1
            out_refs[n_fwd + pos][...] = g.astype(row_dtype)
        for pos in range(len(diff_consts)):
            g = grads[len(diff_rows) + pos]
            o_ref = out_refs[n_fwd + len(diff_rows) + pos]

            @pl.when(i == 0)
            def _(o_ref=o_ref, g=g):
                o_ref[...] = g

            @pl.when(i > 0)
            def _(o_ref=o_ref, g=g):
                o_ref[...] += g

    out_specs = [pl.BlockSpec((tile, w), lambda i: (i, 0)) for w in fwd_widths]
    out_shape = [jax.ShapeDtypeStruct((nblk * tile, w), F32) for w in fwd_widths]
    for idx in diff_rows:
        out_specs.append(pl.BlockSpec((tile, rows[idx][1]), lambda i: (i, 0)))
        out_shape.append(jax.ShapeDtypeStruct((nblk * tile, rows[idx][1]), row_dtype))
    for idx in diff_consts:
        out_specs.append(_const_spec(consts[idx]))
        out_shape.append(jax.ShapeDtypeStruct(consts[idx].shape, F32))
    return pl.pallas_call(
        body, name=name, grid=(nblk,),
        in_specs=([_row_spec(v) for v in rows] + [_const_spec(c) for c in consts]
                  + [_row_spec(v) for v in ct_views] + [_row_spec(v) for v in acc_views]),
        out_specs=out_specs, out_shape=out_shape,
        compiler_params=_cparams(("arbitrary",)),
    )(*[v[0] for v in rows], *consts, *[v[0] for v in ct_views], *[v[0] for v in acc_views])


def _rms_fn(i, x, g):
    return (x * lax.rsqrt(jnp.mean(x * x, axis=-1, keepdims=True) + RMS_EPS) * g,)


def _sigmoid(x):
    return 1.0 / (1.0 + jnp.exp(-x))


def _softplus(x):
    return jnp.maximum(x, 0.0) + jnp.log(1.0 + jnp.exp(-jnp.abs(x)))


def _split2(x):
    hi = x.astype(BF16)
    lo = (x - hi.astype(F32)).astype(BF16)
    return jnp.concatenate([hi, lo], axis=1)


@jax.custom_vjp
def _head_sum(x):
    r = lax.broadcasted_iota(jnp.int32, (2 * RW_DIM, RW_DIM), 0) % RW_DIM // HEAD_DIM
    c = lax.broadcasted_iota(jnp.int32, (2 * RW_DIM, RW_DIM), 1) // HEAD_DIM
    return jnp.dot(_split2(x), (r == c).astype(BF16), preferred_element_type=F32)


_head_sum.defvjp(lambda x: (_head_sum(x), None), lambda _, ct: (_head_sum(ct),))


@jax.custom_vjp
def _dot_bf16(x, w):
    return jnp.dot(x.astype(BF16), w.astype(BF16), preferred_element_type=F32)


def _dot_bf16_bwd(res, ct):
    x, w = res
    ct = ct.astype(BF16)
    dx = lax.dot_general(ct, w.astype(BF16), (((1,), (1,)), ((), ())), preferred_element_type=F32)
    dw = lax.dot_general(x.astype(BF16), ct, (((0,), (0,)), ((), ())), preferred_element_type=F32)
    return dx, dw


_dot_bf16.defvjp(lambda x, w: (_dot_bf16(x, w), (x, w)), _dot_bf16_bwd)


def _rwkv_pre_fn(i, r, k, v, dw, da, dg, r_p, k_p, v_p, dw_p, da_p, dg_p,
                 mix_r, mix_k, mix_v, mix_dw, mix_da, mix_dg, w0, w2, a0, a2, g2, k_k, k_a):
    row = i * BLOCK + lax.broadcasted_iota(jnp.int32, (BLOCK, 1), 0)
    live = row >= PAD_ROWS
    live_prev = row >= PAD_ROWS + 1

    def shift(cur, prev, mix):
        cur = jnp.where(live, cur, 0.0)
        prev = jnp.where(live_prev, prev, 0.0)
        return cur + (prev - cur) * mix

    r = shift(r, r_p, mix_r)
    k = shift(k, k_p, mix_k)
    v = shift(v, v_p, mix_v)
    dw = shift(dw, dw_p, mix_dw)
    da = shift(da, da_p, mix_da)
    dg = shift(dg, dg_p, mix_dg)
    wlog = -_softplus(-(w0 + _dot_bf16(jnp.tanh(dw), w2))) - 0.5
    decay = jnp.exp(-jnp.exp(wlog))
    a = _sigmoid(a0 + _dot_bf16(da, a2))
    g = _dot_bf16(_sigmoid(dg), g2)
    kk = k * k_k
    norm_sq = jnp.where(live, _head_sum(kk * kk), 1.0)
    kk = kk / jnp.maximum(jnp.sqrt(norm_sq), 1e-12)
    k_mod = k * (1.0 + (a - 1.0) * k_a)
    return r, decay, k_mod, v, -kk, kk * a, g


def _rwkv_pre_xt_fn(i, *args):
    r, decay, k_mod, v, a_neg, b, g = _rwkv_pre_fn(i, *args)
    t = SCAN_T
    xt = jnp.concatenate([_rows_to_xt(x[c * t:(c + 1) * t]) for c in range(BLOCK // t)
                          for x in (a_neg, decay, b, k_mod, r)], axis=0)
    return r, k_mod, v, g, xt


def _rwkv_pre_cts(i, dxt, dv_s, dr_p, dk_p, dv_p, dg_p):
    t = SCAN_T
    d_a, d_w, d_b, d_k, d_r = [
        jnp.concatenate([_xt_to_rows(dxt[c * VEC_ROWS + n * HEAD_DIM:c * VEC_ROWS + (n + 1) * HEAD_DIM])
                         for c in range(BLOCK // t)], axis=0) for n in range(N_VEC)]
    dr_p, dk_p, dv_p, dg_p = [jnp.where(i > 0, x, 0.0) for x in (dr_p, dk_p, dv_p, dg_p)]
    return d_r + dr_p, d_w, d_k + dk_p, dv_s + dv_p, d_a, d_b, dg_p


def _rwkv_post_fn(i, ys, r, k_mod, v, g, ln_w, ln_b, r_k):
    mean = _head_sum(ys) * (1.0 / HEAD_DIM)
    d = ys - mean
    var = _head_sum(d * d) * (1.0 / HEAD_DIM)
    yn = d * lax.rsqrt(var + RWKV_LN_EPS) * ln_w + ln_b
    bonus = _head_sum(r * k_mod * r_k) * v
    return ((yn + bonus) * g,)


def _merge_fn(i, ya, yr, g1, g2):
    return (_sigmoid(g1) * ya + _sigmoid(g2) * yr,)


def _swiglu_fn(i, gate, up):
    return (gate * _sigmoid(gate) * up,)


def _loss_fn(i, h, tgt, g):
    y = h * lax.rsqrt(jnp.mean(h * h, axis=-1, keepdims=True) + RMS_EPS) * g
    err = y - tgt
    return (0.5 * jnp.mean(err * err, axis=-1, keepdims=True),)


def _rope_tables(lp):
    pos = (jnp.arange(lp, dtype=jnp.int32) - PAD_ROWS).astype(F32)
    inv_freq = jnp.power(jnp.float32(ROPE_THETA), -jnp.arange(ROPE_HALF, dtype=F32) * (2.0 / ROPE_DIM))
    ang = pos[:, None] * inv_freq[None, :]
    cos, sin = jnp.cos(ang), jnp.sin(ang)
    one = jnp.ones((lp, HEAD_DIM - ROPE_DIM), F32)
    zero_h = jnp.zeros((lp, ROPE_HALF), F32)
    zero_r = jnp.zeros((lp, HEAD_DIM - ROPE_DIM), F32)
    c = jnp.concatenate([cos, cos, one], axis=1)
    s1 = jnp.concatenate([-sin, zero_h, zero_r], axis=1)
    s2 = jnp.concatenate([zero_h, sin, zero_r], axis=1)
    return tuple(jnp.tile(t, (1, LANES // HEAD_DIM)) for t in (c, s1, s2))


def _rope_fwd_fn(i, x, c, s1, s2):
    n = x.shape[1]
    c, s1, s2 = [jnp.tile(t, (1, n // LANES)) for t in (c, s1, s2)]
    return (x * c + pltpu.roll(x, n - ROPE_HALF, 1) * s1 + pltpu.roll(x, ROPE_HALF, 1) * s2,)


def _rope_bwd_fn(i, dy, c, s1, s2):
    n = dy.shape[1]
    c, s1, s2 = [jnp.tile(t, (1, n // LANES)) for t in (c, s1, s2)]
    return (dy * c + pltpu.roll(dy * s1, ROPE_HALF, 1) + pltpu.roll(dy * s2, n - ROPE_HALF, 1),)


def _attn_mask(i):
    r = lax.broadcasted_iota(jnp.int32, (BLOCK, 3 * BLOCK), 0)
    c = lax.broadcasted_iota(jnp.int32, (BLOCK, 3 * BLOCK), 1)
    meta = (c < BLOCK) & (c >= PAD_ROWS)
    prev = (c >= BLOCK) & (c < 2 * BLOCK) & ((c - BLOCK) > r) & (i >= 1)
    cur = (c >= 2 * BLOCK) & ((c - 2 * BLOCK) <= r)
    return meta | prev | cur


def _attn_rows(ref, g):
    return ref[:, g * HEAD_DIM:(g + 1) * HEAD_DIM]


def _attn_group(i, g, q_all, k_refs, v_refs, s_ref):
    heads = range(g * GROUP, (g + 1) * GROUP)
    kcat = jnp.concatenate([_attn_rows(r, g) for r in k_refs], axis=0).astype(BF16)
    vcat = jnp.concatenate([_attn_rows(r, g) for r in v_refs], axis=0).astype(BF16)
    qg = jnp.concatenate([q_all[:, h * HEAD_DIM:(h + 1) * HEAD_DIM] for h in heads], axis=0).astype(BF16)
    sink = jnp.concatenate([jnp.broadcast_to(s_ref[0:1, h:h + 1], (BLOCK, 1)) for h in heads], axis=0)
    s = lax.dot_general(qg, kcat, (((1,), (1,)), ((), ())), preferred_element_type=F32) * (HEAD_DIM ** -0.5)
    valid = jnp.concatenate([_attn_mask(i)] * GROUP, axis=0)
    return heads, qg, kcat, vcat, sink, jnp.where(valid, s, NEG_INF)


ATTN_SUB = 2


def _attn_specs(v_col):
    q = [pl.BlockSpec((BLOCK, Q_HEADS * HEAD_DIM), lambda i, n=n: (ATTN_SUB * i + 1 + n, 0)) for n in range(ATTN_SUB)]
    blk = lambda col: ([pl.BlockSpec((BLOCK, 2 * HEAD_DIM), lambda i: (0, col))]
                       + [pl.BlockSpec((BLOCK, 2 * HEAD_DIM), lambda i, n=n: (ATTN_SUB * i + n, col))
                          for n in range(ATTN_SUB + 1)])
    return q + blk(0) + blk(v_col) + [pl.BlockSpec((1, Q_HEADS), lambda i: (0, 0))]


def _attn_split(refs):
    n = ATTN_SUB
    q_refs, k_refs, v_refs = refs[:n], refs[n:2 * n + 2], refs[2 * n + 2:3 * n + 4]
    return q_refs, k_refs, v_refs, refs[3 * n + 4], refs[3 * n + 5:]


def _attn_fwd_block(blk, q_ref, keys, vals, s_ref, o_ref, lse_ref):
    q_all = q_ref[...]
    for g in range(KV_HEADS):
        heads, _, _, vcat, sink, s = _attn_group(blk, g, q_all, keys, vals, s_ref)
        m = jnp.maximum(jnp.max(s, axis=-1, keepdims=True), sink)
        p = jnp.exp(s - m)
        den = jnp.sum(p, axis=-1, keepdims=True) + jnp.exp(sink - m)
        o = jnp.dot(p.astype(BF16), vcat, preferred_element_type=F32) / den
        lse = m + jnp.log(den)
        for n, h in enumerate(heads):
            o_ref[:, h * HEAD_DIM:(h + 1) * HEAD_DIM] = o[n * BLOCK:(n + 1) * BLOCK]
            lse_ref[:, h:h + 1] = lse[n * BLOCK:(n + 1) * BLOCK]


def _attn_bwd(q, k, v, v_col, sinks, o, lse, do, *, nblk, name):
    lp = k.shape[0]
    rows = ATTN_SUB * BLOCK

    def body(*refs):
        q_refs, k_refs, v_refs, s_ref, (o_ref, lse_ref, do_ref, dq_ref, dk_ref, dv_ref, ds_ref) = _attn_split(refs)
        i = pl.program_id(0)

        @pl.when(i == 0)
        def _():
            dk_ref[...] = jnp.zeros_like(dk_ref)
            dv_ref[...] = jnp.zeros_like(dv_ref)
            ds_ref[...] = jnp.zeros_like(ds_ref)

        lane = lax.broadcasted_iota(jnp.int32, (1, Q_HEADS), 1)
        for sub in range(ATTN_SUB):
            at = slice(sub * BLOCK, (sub + 1) * BLOCK)
            blk = ATTN_SUB * i + sub
            keys = (k_refs[0], k_refs[1 + sub], k_refs[2 + sub])
            vals = (v_refs[0], v_refs[1 + sub], v_refs[2 + sub])
            prev_rows = pl.ds(pl.multiple_of(blk * BLOCK, BLOCK), BLOCK)
            cur_rows = pl.ds(pl.multiple_of((blk + 1) * BLOCK, BLOCK), BLOCK)
            q_all, o_all, do_all, lse_all = q_refs[sub][...], o_ref[at, :], do_ref[at, :], lse_ref[at, :]
            for g in range(KV_HEADS):
                heads, qg, kcat, vcat, sink, s = _attn_group(blk, g, q_all, keys, vals, s_ref)
                stack = lambda x: jnp.concatenate([x[:, h * HEAD_DIM:(h + 1) * HEAD_DIM] for h in heads], axis=0)
                lse_g = jnp.concatenate([lse_all[:, h:h + 1] for h in heads], axis=0)
                do_g = stack(do_all)
                p = jnp.exp(s - lse_g)
                delta = jnp.sum(do_g * stack(o_all), axis=-1, keepdims=True)
                dp = lax.dot_general(do_g.astype(BF16), vcat, (((1,), (1,)), ((), ())), preferred_element_type=F32)
                dsc = (p * (dp - delta) * (HEAD_DIM ** -0.5)).astype(BF16)
                dq = jnp.dot(dsc, kcat, preferred_element_type=F32)
                dk_all = lax.dot_general(dsc, qg, (((0,), (0,)), ((), ())), preferred_element_type=F32)
                dv_all = lax.dot_general(p.astype(BF16), do_g.astype(BF16), (((0,), (0,)), ((), ())),
                                         preferred_element_type=F32)
                cols = slice(g * HEAD_DIM, (g + 1) * HEAD_DIM)
                for ref, full in ((dk_ref, dk_all), (dv_ref, dv_all)):
                    ref[0:BLOCK, cols] += full[0:BLOCK]
                    ref[prev_rows, cols] += full[BLOCK:2 * BLOCK]
                    ref[cur_rows, cols] += full[2 * BLOCK:]
                sink_part = jnp.exp(sink - lse_g) * delta
                for n, h in enumerate(heads):
                    dq_ref[at, h * HEAD_DIM:(h + 1) * HEAD_DIM] = dq[n * BLOCK:(n + 1) * BLOCK]
                    dsink = -jnp.sum(sink_part[n * BLOCK:(n + 1) * BLOCK], axis=0, keepdims=True)
                    ds_ref[...] += jnp.where(lane == h, dsink, 0.0)

    qspec = pl.BlockSpec((rows, Q_HEADS * HEAD_DIM), lambda i: (i, 0))
    whole = pl.BlockSpec((lp, 2 * HEAD_DIM), lambda i: (0, 0))
    return pl.pallas_call(
        body, name=name, grid=(nblk // ATTN_SUB,),
        in_specs=_attn_specs(v_col) + [qspec, pl.BlockSpec((rows, Q_HEADS), lambda i: (i, 0)), qspec],
        out_specs=[qspec, whole, whole, pl.BlockSpec((1, Q_HEADS), lambda i: (0, 0))],
        out_shape=[jax.ShapeDtypeStruct((nblk * BLOCK, Q_HEADS * HEAD_DIM), F32),
                   jax.ShapeDtypeStruct((lp, 2 * HEAD_DIM), F32), jax.ShapeDtypeStruct((lp, 2 * HEAD_DIM), F32),
                   jax.ShapeDtypeStruct((1, Q_HEADS), F32)],
        compiler_params=_cparams(("arbitrary",)),
    )(*[q] * ATTN_SUB, *[k] * (ATTN_SUB + 2), *[v] * (ATTN_SUB + 2), sinks, o, lse, do)


N_VEC = 5
VEC_ROWS = N_VEC * HEAD_DIM


def _selectors():
    t = SCAN_T
    shape = (t, 2 * LANES, RW_DIM)
    step, src, dst = [lax.broadcasted_iota(jnp.int32, shape, d) for d in range(3)]
    src = src % LANES
    spread = ((src // t == dst // HEAD_DIM) & (src % t == step)).astype(BF16)
    shape = (t, RW_DIM, LANES)
    step, src, dst = [lax.broadcasted_iota(jnp.int32, shape, d) for d in range(3)]
    collect = ((src // HEAD_DIM == dst // t) & (dst % t == step)).astype(BF16)
    return spread, collect


def _rows_to_xt(x):
    low = lax.broadcasted_iota(jnp.int32, (SCAN_T, LANES), 1) < HEAD_DIM
    pieces = []
    for m in range(RW_HEADS // 2):
        pair = x[:, m * LANES:(m + 1) * LANES]
        pieces += [jnp.where(low, pair, 0.0), jnp.where(low, pltpu.roll(pair, HEAD_DIM, 1), 0.0)]
    return jnp.concatenate(pieces, axis=0).T[:HEAD_DIM]


def _xt_to_rows(a):
    t = SCAN_T
    a_t = jnp.concatenate([a, jnp.zeros_like(a)], axis=0).T
    pairs = [a_t[2 * m * t:(2 * m + 1) * t] + pltpu.roll(a_t[(2 * m + 1) * t:(2 * m + 2) * t], HEAD_DIM, 1)
             for m in range(RW_HEADS // 2)]
    return jnp.concatenate(pairs, axis=1)


def _with_exchange(compute, n_in, n_out, n_scratch, scattered, shared, grid):
    n_sc = len(scattered)
    n_x = n_sc + len(shared)
    if n_x == 0:
        return compute

    def body(*refs):
        ins, x_in = refs[:n_in], refs[n_in:n_in + n_x]
        outs, x_out = refs[n_in + n_x:n_in + n_x + n_out], refs[n_in + n_x + n_out:n_in + 2 * n_x + n_out]
        scratch = refs[n_in + 2 * n_x + n_out:n_in + 2 * n_x + n_out + n_scratch]
        sems = refs[n_in + 2 * n_x + n_out + n_scratch:]

        first = last = True
        for d, size in enumerate(grid):
            first = first & (pl.program_id(d) == 0)
            last = last & (pl.program_id(d) == size - 1)

        @pl.when(first)
        def _():
            for cp in _exchange_copies(x_in, x_out, n_sc, *sems):
                cp.start()

        compute(*ins, *outs, *scratch)

        @pl.when(last)
        def _():
            for cp in _exchange_copies(x_in, x_out, n_sc, *sems):
                cp.wait()

    return body


def _wkv_fwd(xt, v, spread, attn, name, shared=()):
    t_steps = SCAN_T
    nch = xt.shape[0]
    per = SCAN_CHUNKS
    grid = (nch // per,)
    rows = per * t_steps
    assert rows == BLOCK
    n_x = len(shared)
    q, k, v_arr, v_col, sinks = attn

    def compute(xt_ref, v_ref, sel_ref, q_ref, km_ref, kp_ref, kc_ref, vm_ref, vp_ref, vc_ref, s_ref,
                y_ref, hist_ref, o_ref, lse_ref, st_ref):
        @pl.when(pl.program_id(0) == 0)
        def _():
            st_ref[...] = jnp.zeros_like(st_ref)

        _attn_fwd_block(pl.program_id(0) - 1, q_ref, (km_ref, kp_ref, kc_ref), (vm_ref, vp_ref, vc_ref), s_ref,
                        o_ref, lse_ref)
        st = st_ref[...]
        for c in range(per):
            x2 = _split2(xt_ref[c])
            for j in range(t_steps):
                row = c * t_steps + j
                cols = jnp.dot(x2, sel_ref[j], preferred_element_type=F32)
                a_c, w_c, b_c, k_c, r_c = [cols[n * HEAD_DIM:(n + 1) * HEAD_DIM] for n in range(N_VEC)]
                hist_ref[row] = st
                sa = jnp.sum(st * a_c, axis=0, keepdims=True)
                st = st * w_c + b_c * sa + k_c * v_ref[row:row + 1, :]
                y_ref[row:row + 1, :] = jnp.sum(st * r_c, axis=0, keepdims=True)
        st_ref[...] = st

    before = lambda c: jnp.maximum(c - 1, 0)
    kv = lambda col: [pl.BlockSpec((BLOCK, 2 * HEAD_DIM), lambda c: (0, col)),
                      pl.BlockSpec((BLOCK, 2 * HEAD_DIM), lambda c: (before(c), col)),
                      pl.BlockSpec((BLOCK, 2 * HEAD_DIM), lambda c: (c, col))]
    seq = (nch * t_steps) - BLOCK
    return pl.pallas_call(
        _with_exchange(compute, 11, 4, 1, (), shared, grid), name=name, grid=grid,
        in_specs=[pl.BlockSpec((per, VEC_ROWS, LANES), lambda c: (c, 0, 0)),
                  pl.BlockSpec((rows, RW_DIM), lambda c: (c, 0)),
                  pl.BlockSpec(spread.shape, lambda c: (0, 0, 0)),
                  pl.BlockSpec((BLOCK, Q_HEADS * HEAD_DIM), lambda c: (c, 0))] + kv(0) + kv(v_col)
                 + [pl.BlockSpec((1, Q_HEADS), lambda c: (0, 0))] + [ANY] * n_x,
        out_specs=[pl.BlockSpec((rows, RW_DIM), lambda c: (c, 0)),
                   pl.BlockSpec((rows, HEAD_DIM, RW_DIM), lambda c: (c, 0, 0)),
                   pl.BlockSpec((BLOCK, Q_HEADS * HEAD_DIM), lambda c: (before(c), 0)),
                   pl.BlockSpec((BLOCK, Q_HEADS), lambda c: (before(c), 0))] + [ANY] * n_x,
        out_shape=[jax.ShapeDtypeStruct((nch * t_steps, RW_DIM), F32),
                   jax.ShapeDtypeStruct((nch * t_steps, HEAD_DIM, RW_DIM), F32),
                   jax.ShapeDtypeStruct((seq, Q_HEADS * HEAD_DIM), F32),
                   jax.ShapeDtypeStruct((seq, Q_HEADS), F32)] + _exchange_shapes((), shared),
        scratch_shapes=[pltpu.VMEM((HEAD_DIM, RW_DIM), F32)] + (_exchange_sems(n_x) if n_x else []),
        compiler_params=_cparams(("arbitrary",), SCAN_VMEM_LIMIT),
    )(xt, v, spread, q, k, k, k, v_arr, v_arr, v_arr, sinks, *shared)


def _wkv_bwd(xt, v, hist, dy, spread, collect, name, scattered=()):
    t_steps = SCAN_T
    nch = xt.shape[0]
    n_x = len(scattered)
    per = SCAN_CHUNKS
    nsteps = nch // per
    grid = (nsteps,)
    rows = per * t_steps
    lead = BLOCK // rows

    def compute(xt_ref, v_ref, hist_ref, dy_ref, sel_ref, col_ref, dxt_ref, dv_ref, g_ref):
        @pl.when(pl.program_id(0) == 0)
        def _():
            g_ref[...] = jnp.zeros_like(g_ref)

        has_dy = nsteps - 1 - pl.program_id(0) >= lead
        gst = g_ref[...]
        nxt = None
        for c in reversed(range(per)):
            x2 = _split2(xt_ref[c])
            acc = jnp.zeros((VEC_ROWS, LANES), F32)
            for j in reversed(range(t_steps)):
                row = c * t_steps + j
                cols = jnp.dot(x2, sel_ref[j], preferred_element_type=F32)
                a_c, w_c, b_c, k_c, r_c = [cols[n * HEAD_DIM:(n + 1) * HEAD_DIM] for n in range(N_VEC)]
                prev = hist_ref[row]
                v_row = v_ref[row:row + 1, :]
                dy_row = jnp.where(has_dy, dy_ref[row:row + 1, :], 0.0)
                sa = jnp.sum(prev * a_c, axis=0, keepdims=True)
                if nxt is None:
                    nxt = prev * w_c + b_c * sa + k_c * v_row
                gst = gst + r_c * dy_row
                dv_ref[row:row + 1, :] = jnp.sum(gst * k_c, axis=0, keepdims=True)
                dsa = jnp.sum(gst * b_c, axis=0, keepdims=True)
                prods = jnp.concatenate([p.astype(BF16) for p in
                                         (prev * dsa, gst * prev, gst * sa, gst * v_row, nxt * dy_row)], axis=0)
                acc = acc + jnp.dot(prods, col_ref[j], preferred_element_type=F32)
                gst = gst * w_c + a_c * dsa
                nxt = prev
            dxt_ref[c] = acc
        g_ref[...] = gst

    rev3 = lambda c: (nsteps - 1 - c, 0, 0)
    rev2 = lambda c: (nsteps - 1 - c, 0)
    rowspec = pl.BlockSpec((rows, RW_DIM), rev2)
    return pl.pallas_call(
        _with_exchange(compute, 6, 2, 1, scattered, (), grid), name=name, grid=grid,
        in_specs=[pl.BlockSpec((per, VEC_ROWS, LANES), rev3), rowspec,
                  pl.BlockSpec((rows, HEAD_DIM, RW_DIM), rev3),
                  pl.BlockSpec((rows, RW_DIM), lambda c: (jnp.maximum(nsteps - 1 - c - lead, 0), 0)),
                  pl.BlockSpec(spread.shape, lambda c: (0, 0, 0)),
                  pl.BlockSpec(collect.shape, lambda c: (0, 0, 0))] + [ANY] * n_x,
        out_specs=[pl.BlockSpec((per, VEC_ROWS, LANES), rev3), rowspec] + [ANY] * n_x,
        out_shape=[jax.ShapeDtypeStruct((nch, VEC_ROWS, LANES), F32),
                   jax.ShapeDtypeStruct((nch * t_steps, RW_DIM), F32)] + _exchange_shapes(scattered, ()),
        scratch_shapes=[pltpu.VMEM((HEAD_DIM, RW_DIM), F32)] + (_exchange_sems(n_x) if n_x else []),
        compiler_params=_cparams(("arbitrary",), SCAN_VMEM_LIMIT),
    )(xt, v, hist, dy, spread, collect, *scattered)


MESH = pl.DeviceIdType.MESH
ANY = pl.BlockSpec(memory_space=pltpu.HBM)


def _all_gather(arrays, name):
    n_arr = len(arrays)
    per = N_DEV - 1

    def body(*refs):
        x_refs, out_refs = refs[:n_arr], refs[n_arr:2 * n_arr]
        send_sems, recv_sems, local_sems = refs[2 * n_arr:]
        xi, yi, ci = lax.axis_index("x"), lax.axis_index("y"), lax.axis_index("c")
        me, sibling = (xi, yi, ci), (xi, yi, 1 - ci)
        chips = [(1 - xi, yi), (xi, 1 - yi), (1 - xi, 1 - yi)]

        def slot(a, px, py, pc):
            return out_refs[a].at[4 * px + 2 * py + pc]

        def copy(a, sem, block, to, src=None):
            return pltpu.make_async_remote_copy(
                src_ref=slot(a, *block) if src is None else src, dst_ref=slot(a, *block),
                send_sem=send_sems.at[per * a + sem], recv_sem=recv_sems.at[per * a + sem],
                device_id=to, device_id_type=MESH)

        mine = [pltpu.make_async_copy(x_refs[a], slot(a, *me), local_sems.at[a]) for a in range(n_arr)]
        for cp in mine:
            cp.start()
        sent = []
        for a in range(n_arr):
            sent.append(copy(a, 0, me, sibling, src=x_refs[a]))
            sent += [copy(a, 1 + j, me, (*chip, ci), src=x_refs[a]) for j, chip in enumerate(chips)]
        for cp in sent:
            cp.start()
        for j, chip in enumerate(chips):
            for a in range(n_arr):
                copy(a, 1 + j, (*chip, ci), me).wait_recv()
                onward = copy(a, 4 + j, (*chip, ci), sibling)
                onward.start()
                sent.append(onward)
        for a in range(n_arr):
            copy(a, 0, sibling, me).wait_recv()
        for j, chip in enumerate(chips):
            for a in range(n_arr):
                copy(a, 4 + j, (*chip, 1 - ci), me).wait_recv()
        for cp in sent:
            cp.wait_send()
        for cp in mine:
            cp.wait()

    sems = pltpu.SemaphoreType.DMA((per * n_arr,))
    return pl.pallas_call(
        body, name=name, out_shape=[jax.ShapeDtypeStruct((N_DEV,) + a.shape, a.dtype) for a in arrays],
        in_specs=[ANY] * n_arr, out_specs=[ANY] * n_arr,
        scratch_shapes=[sems, sems, pltpu.SemaphoreType.DMA((n_arr,))],
    )(*arrays)


def _exchange_copies(in_refs, out_refs, n_scattered, send_sems, recv_sems, local_sems):
    n_arr = len(in_refs)
    per = N_DEV - 1
    xi, yi, ci = lax.axis_index("x"), lax.axis_index("y"), lax.axis_index("c")
    me = 4 * xi + 2 * yi + ci
    src_of = lambda a, peer: in_refs[a].at[peer] if a < n_scattered else in_refs[a]
    copies = []
    for d in range(1, N_DEV):
        px = 1 - xi if d & 4 else xi
        py = 1 - yi if d & 2 else yi
        pc = 1 - ci if d & 1 else ci
        for a in range(n_arr):
            copies.append(pltpu.make_async_remote_copy(
                src_ref=src_of(a, 4 * px + 2 * py + pc), dst_ref=out_refs[a].at[me],
                send_sem=send_sems.at[per * a + d - 1], recv_sem=recv_sems.at[per * a + d - 1],
                device_id=(px, py, pc), device_id_type=MESH))
    own = [pltpu.make_async_copy(src_of(a, me), out_refs[a].at[me], local_sems.at[a]) for a in range(n_arr)]
    return copies + own


def _exchange_shapes(scattered, shared):
    return ([jax.ShapeDtypeStruct(a.shape, a.dtype) for a in scattered]
            + [jax.ShapeDtypeStruct((N_DEV,) + a.shape, a.dtype) for a in shared])


def _exchange_sems(n_arr):
    sems = pltpu.SemaphoreType.DMA(((N_DEV - 1) * n_arr,))
    return [sems, sems, pltpu.SemaphoreType.DMA((n_arr,))]


def _exchange(scattered, shared, name):
    n_sc = len(scattered)
    n_arr = n_sc + len(shared)

    def body(*refs):
        copies = _exchange_copies(refs[:n_arr], refs[n_arr:2 * n_arr], n_sc, *refs[2 * n_arr:])
        for cp in copies:
            cp.start()
        for cp in copies:
            cp.wait()

    return pl.pallas_call(
        body, name=name, out_shape=_exchange_shapes(scattered, shared),
        in_specs=[ANY] * n_arr, out_specs=[ANY] * n_arr, scratch_shapes=_exchange_sems(n_arr),
    )(*scattered, *shared)


def _adam_math(g, w, m, v):
    m_new = ADAM_B1 * m + (1.0 - ADAM_B1) * g
    v_new = ADAM_B2 * v + (1.0 - ADAM_B2) * (g * g)
    m_hat = m_new / (1.0 - ADAM_B1 ** ADAM_STEP)
    v_hat = v_new / (1.0 - ADAM_B2 ** ADAM_STEP)
    return -ADAM_LR * (m_hat / (jnp.sqrt(v_hat) + ADAM_EPS) + ADAM_WD * w), m_new, v_new


def _slot_sum(p_ref):
    g = p_ref[0].astype(F32)
    for s in range(1, N_DEV):
        g = g + p_ref[s].astype(F32)
    return g


def _adamw_replicated(parts, ws, ms, vs, loss_parts, name):
    n = len(ws)

    def body(*refs):
        p_refs, w_refs, m_refs, v_refs = refs[:n], refs[n:2 * n], refs[2 * n:3 * n], refs[3 * n:4 * n]
        outs = refs[4 * n + 1:]
        for j in range(n):
            g = _slot_sum(p_refs[j])
            outs[4 * j][...] = g
            for o_ref, val in zip(outs[4 * j + 1:4 * j + 4], _adam_math(g, w_refs[j][...], m_refs[j][...],
                                                                        v_refs[j][...])):
                o_ref[...] = val
        outs[4 * n][...] = _slot_sum(refs[4 * n])

    whole = pl.BlockSpec(memory_space=pltpu.VMEM)
    out = pl.pallas_call(
        body, name=name, in_specs=[whole] * (4 * n + 1), out_specs=[whole] * (4 * n + 1),
        out_shape=[jax.ShapeDtypeStruct(w.shape, F32) for w in ws for _ in range(4)]
                  + [jax.ShapeDtypeStruct(loss_parts.shape[1:], F32)],
    )(*parts, *ws, *ms, *vs, loss_parts)
    return [out[4 * j:4 * j + 4] for j in range(n)], out[4 * n]


def _adamw(parts, w, m, v, name):
    rows, cols = w.shape[-2:]
    tile = PACK_ROWS if rows % PACK_ROWS == 0 else rows
    at = (0,) if w.ndim == 3 else (Ellipsis,)

    def body(p_ref, w_ref, m_ref, v_ref, g_out, d_out, m_out, v_out):
        g = _slot_sum(p_ref)
        g_out[at] = g
        d_out[at], m_out[at], v_out[at] = _adam_math(g, w_ref[at], m_ref[at], v_ref[at])

    spec = (pl.BlockSpec((1, tile, cols), lambda i: (0, i, 0)) if w.ndim == 3
            else pl.BlockSpec((tile, cols), lambda i: (i, 0)))
    return pl.pallas_call(
        body, name=name, grid=(rows // tile,),
        in_specs=[pl.BlockSpec((N_DEV, tile, cols), lambda i: (0, i, 0)), spec, spec, spec],
        out_specs=[spec] * 4, out_shape=[jax.ShapeDtypeStruct(w.shape, F32)] * 4,
        compiler_params=_cparams(("parallel",)),
    )(parts, w, m, v)


EARLY = [("meta_tokens", 1), ("rwkv_w2", 1), ("rwkv_a2", 1), ("rwkv_g2", 1)]
LATE = [("w_br_attn", 1), ("w_br_rwkv", 1), ("w_o", 0), ("w_ffn_gate", 1), ("w_ffn_up", 1), ("w_ffn_down", 0)]
REPLICATED = ["norm_mix_g", "b_in", "attn_sinks", "rwkv_mix", "rwkv_w0", "rwkv_a0", "rwkv_k_k", "rwkv_k_a",
              "rwkv_r_k", "rwkv_ln_w", "rwkv_ln_b", "norm_ffn_g", "norm_final_g"]
WEIGHTS = ["meta_tokens", "norm_mix_g", "w_in", "b_in", "attn_sinks", "rwkv_mix", "rwkv_w0", "rwkv_w2", "rwkv_a0",
           "rwkv_a2", "rwkv_g2", "rwkv_k_k", "rwkv_k_a", "rwkv_r_k", "rwkv_ln_w", "rwkv_ln_b", "w_br_attn",
           "w_br_rwkv", "w_o", "norm_ffn_g", "w_ffn_gate", "w_ffn_up", "w_ffn_down", "norm_final_g"]


def _strip(name, a):
    return a if name in ("meta_tokens", "norm_final_g") else a[0]


def _join(gathered, axis):
    if axis == 0:
        return gathered.reshape(-1, gathered.shape[2])
    return gathered.transpose(1, 0, 2).reshape(gathered.shape[1], -1)


def _split(g, axis):
    if axis == 0:
        return g.reshape(N_DEV, -1, g.shape[1])
    return g.reshape(g.shape[0], N_DEV, -1).transpose(1, 0, 2)


W_IN_LAYOUT = [(2592, 4640), (768, 2304), (2432, 2592), 256 - GATE_LORA, (2304, 2368), 128 - DECAY_LORA,
               (2368, 2432), 128 - AAA_LORA, (0, 512), (512, 768), NP - C_VA - 128]


def _w_in_padded(w, shard_width=None):
    rows = w.shape[-2]
    width = D_IN if shard_width is None else shard_width
    parts = []
    for seg in W_IN_LAYOUT:
        if isinstance(seg, int):
            parts.append(jnp.zeros((rows, seg), w.dtype))
            continue
        lo, stop = seg
        while lo < stop:
            p = lo // width
            hi = min(stop, (p + 1) * width)
            src = w if shard_width is None else w[p]
            parts.append(src[:, lo - p * width:hi - p * width])
            lo = hi
    return jnp.concatenate(parts, axis=1)


def _w_in_unpadded(wp, lo=0, stop=D_IN):
    spans, pos = [], 0
    for seg in W_IN_LAYOUT:
        if isinstance(seg, int):
            pos += seg
        else:
            spans.append((seg[0], seg[1], pos))
            pos += seg[1] - seg[0]
    parts = []
    for a, b, at in sorted(spans):
        c, d = max(a, lo), min(b, stop)
        if c < d:
            parts.append(wp[:, at + c - a:at + d - a])
    return jnp.concatenate(parts, axis=1)


def _pad_rows(a, n):
    return jnp.pad(a, ((0, n - a.shape[0]), (0, 0)))


def _device_step(x, tgt, full, gather_late=None, scatter_early=None, scatter_last=None):
    seq = x.shape[0]
    nblk = seq // BLOCK
    lp = seq + BLOCK
    nall = nblk + 1

    w_in_p = full["w_in_p"]
    b_in_p = _w_in_padded(full["b_in"][None])
    mix = full["rwkv_mix"][None]
    mix_r, mix_k, mix_v = mix[:, 0:512], mix[:, 512:1024], mix[:, 1024:1536]
    mix_dw = jnp.pad(mix[:, 1536:1600], ((0, 0), (0, 64)))
    mix_da = jnp.pad(mix[:, 1600:1664], ((0, 0), (0, 64)))
    mix_dg = jnp.pad(mix[:, 1664:1824], ((0, 0), (0, 96)))
    w2_p = _pad_rows(full["rwkv_w2"].astype(F32), 128)
    a2_p = _pad_rows(full["rwkv_a2"].astype(F32), 128)
    g2_p = _pad_rows(full["rwkv_g2"].astype(F32), 256)
    row = lambda name: full[name].reshape(1, -1)
    sinks = row("attn_sinks")
    rope_c, rope_s1, rope_s2 = _rope_tables(lp)

    hpad = jnp.concatenate([jnp.zeros((PAD_ROWS, D_MODEL), F32), full["meta_tokens"].astype(F32), x], axis=0)
    (u,) = _rows_fwd(_rms_fn, [_view(hpad)], [row("norm_mix_g")], [D_MODEL], nblk=nall, name="norm_mix",
                     out_dtype=BF16, tile=TILE_ALL_BIG)
    proj = _mm(u, w_in_p, bias=b_in_p, name="in_proj")
    (q_r,) = _rows_fwd(_rope_fwd_fn, [_view(proj, 512, C_Q // 512), _view(rope_c), _view(rope_s1), _view(rope_s2)],
                       [], [512], nblk=nall, name="rope_q", tile=TILE_ALL_BIG)
    (k_r,) = _rows_fwd(_rope_fwd_fn, [_view(proj, 128, C_KA // 128), _view(rope_c), _view(rope_s1),
                                      _view(rope_s2)], [], [128], nblk=nall, name="rope_k", tile=TILE_ALL_BIG)

    rw_prev = jnp.pad(proj[:-1, C_R:C_R + 2048], ((1, 0), (0, 0)))
    pre_rows = [_view(proj, 512, C_R // 512), _view(proj, 512, C_K // 512), _view(proj, 512, C_V // 512),
                _view(proj, 128, C_DW // 128),
                _view(proj, 128, C_DA // 128), _view(proj, 256, C_DG // 256),
                _view(rw_prev, 512, 0), _view(rw_prev, 512, 1), _view(rw_prev, 512, 2), _view(rw_prev, 128, 14),
                _view(rw_prev, 128, 15), _view(rw_prev, 256, 6)]
    pre_consts = [mix_r, mix_k, mix_v, mix_dw, mix_da, mix_dg, row("rwkv_w0"), w2_p, row("rwkv_a0"), a2_p, g2_p,
                  row("rwkv_k_k"), row("rwkv_k_a")]
    xt_block = (BLOCK // SCAN_T * VEC_ROWS, LANES)
    r_t, k_mod, v_t, gate, xt = _rows_fwd(_rwkv_pre_xt_fn, pre_rows, pre_consts, [RW_DIM] * 4 + [xt_block],
                                          nblk=nall, name="rwkv_pre")
    xt = xt.reshape(-1, VEC_ROWS, LANES)
    spread, collect = _selectors()
    y_scan, hist, o_attn, lse, *late = _wkv_fwd(xt, v_t, spread, (q_r, k_r, proj, C_VA // 128, sinks), name="wkv_fwd",
                                                shared=gather_late[0] if gather_late else ())
    if gather_late:
        full = {**full, **gather_late[1](late)}
    post_rows = [_view(y_scan, off=1), _view(r_t, off=1), _view(k_mod, off=1), _view(v_t, off=1), _view(gate, off=1)]
    post_consts = [row("rwkv_ln_w"), row("rwkv_ln_b"), row("rwkv_r_k")]
    (y_rwkv,) = _rows_fwd(_rwkv_post_fn, post_rows, post_consts, [RW_DIM], nblk=nblk, name="rwkv_post",
                          out_dtype=BF16)

    ya = _mm(o_attn, full["w_br_attn"], name="br_attn")
    yr = _mm(y_rwkv, full["w_br_rwkv"], name="br_rwkv")
    merge_rows = [_view(ya), _view(yr), _view(proj, 1024, C_G1 // 1024, 1), _view(proj, 1024, C_G2 // 1024, 1)]
    (merged,) = _rows_fwd(_merge_fn, merge_rows, [], [D_MODEL], nblk=nblk, name="merge", out_dtype=BF16)
    h1 = _mm(merged, full["w_o"], residual=x, name="out_proj")
    (f,) = _rows_fwd(_rms_fn, [_view(h1)], [row("norm_ffn_g")], [D_MODEL], nblk=nblk, name="norm_ffn",
                     out_dtype=BF16, tile=TILE_REAL)
    ff_gate = _mm(f, full["w_ffn_gate"], name="ffn_gate")
    ff_up = _mm(f, full["w_ffn_up"], name="ffn_up")
    (act,) = _rows_fwd(_swiglu_fn, [_view(ff_gate), _view(ff_up)], [], [D_FF], nblk=nblk, name="swiglu",
                       out_dtype=BF16, tile=TILE_WIDE)
    h2 = _mm(act, full["w_ffn_down"], residual=h1, name="ffn_down")

    grads = {}
    ones_col = jnp.ones((seq, 1), F32)
    loss_rows, dh2, grads["norm_final_g"] = _rows_bwd(
        _loss_fn, [_view(h2), _view(tgt)], [row("norm_final_g")], [_view(ones_col)], nblk=nblk, name="loss",
        diff_rows=[0], diff_consts=[0], fwd_widths=[1], tile=TILE_REAL)
    loss = jnp.sum(loss_rows)

    dact = _mm(dh2, full["w_ffn_down"], tb=True, name="d_act")
    grads["w_ffn_down"] = _mm(act, dh2, ta=True, name="dw_ffn_down")
    dgate, dup = _rows_bwd(_swiglu_fn, [_view(ff_gate), _view(ff_up)], [], [_view(dact)], nblk=nblk,
                           name="swiglu_bwd", diff_rows=[0, 1], diff_consts=[], row_dtype=BF16, tile=TILE_WIDE)
    grads["w_ffn_gate"] = _mm(f, dgate, ta=True, name="dw_ffn_gate")
    grads["w_ffn_up"] = _mm(f, dup, ta=True, name="dw_ffn_up")
    df = _mm(dgate, full["w_ffn_gate"], tb=True, name="df_gate")
    df = _mm(dup, full["w_ffn_up"], tb=True, residual=df, name="df_up")
    dh1, grads["norm_ffn_g"] = _rows_bwd(_rms_fn, [_view(h1)], [row("norm_ffn_g")], [_view(df)], nblk=nblk,
                                         name="norm_ffn_bwd", diff_rows=[0], diff_consts=[0], acc=[_view(dh2)],
                                         tile=TILE_REAL)
    dmerged = _mm(dh1, full["w_o"], tb=True, name="d_merged")
    grads["w_o"] = _mm(merged, dh1, ta=True, name="dw_o")
    dya, dyr, dg1, dg2 = _rows_bwd(_merge_fn, merge_rows, [], [_view(dmerged)], nblk=nblk, name="merge_bwd",
                                   diff_rows=[0, 1, 2, 3], diff_consts=[], row_dtype=BF16)
    grads["w_br_attn"] = _mm(o_attn, dya, ta=True, name="dw_br_attn")
    grads["w_br_rwkv"] = _mm(y_rwkv, dyr, ta=True, name="dw_br_rwkv")
    dy_attn = _mm(dya, full["w_br_attn"], tb=True, name="d_y_attn")
    dy_rwkv = _mm(dyr, full["w_br_rwkv"], tb=True, name="d_y_rwkv")

    post = _rows_bwd(_rwkv_post_fn, post_rows, post_consts, [_view(dy_rwkv)], nblk=nblk, name="rwkv_post_bwd",
                     diff_rows=[0, 1, 2, 3, 4], diff_consts=[0, 1, 2])
    dys, dr_post, dk_post, dv_post, dgate_post = post[:5]
    grads["rwkv_ln_w"], grads["rwkv_ln_b"], grads["rwkv_r_k"] = post[5:]
    dxt, dv_s, *early_parts = _wkv_bwd(xt, v_t, hist, dys, spread, collect, name="wkv_bwd",
                                       scattered=scatter_early(grads) if scatter_early else ())
    pre_cts = [_view(dxt.reshape(-1, LANES), rows=xt_block[0]), _view(dv_s)] + [
        _view(t, off=-1) for t in (dr_post, dk_post, dv_post, dgate_post)]
    pre = _rows_bwd(_rwkv_pre_fn, pre_rows, pre_consts, pre_cts, nblk=nall, name="rwkv_pre_bwd",
                    diff_rows=list(range(12)), diff_consts=list(range(13)), ct_map=_rwkv_pre_cts)
    d_cur, d_prev, d_par = pre[0:6], pre[6:12], pre[12:]
    up = lambda t: jnp.pad(t[1:], ((0, 1), (0, 0)))
    d_rw = [c + up(p) for c, p in zip(d_cur, d_prev)]
    grads["rwkv_mix"] = jnp.concatenate([d_par[0], d_par[1], d_par[2], d_par[3][:, :DECAY_LORA],
                                         d_par[4][:, :AAA_LORA], d_par[5][:, :GATE_LORA]], axis=1)
    grads["rwkv_w0"], grads["rwkv_w2"] = d_par[6], d_par[7][:DECAY_LORA]
    grads["rwkv_a0"], grads["rwkv_a2"] = d_par[8], d_par[9][:AAA_LORA]
    grads["rwkv_g2"] = d_par[10][:GATE_LORA]
    grads["rwkv_k_k"], grads["rwkv_k_a"] = d_par[11], d_par[12]

    dq_real, dk_r, dva, grads["attn_sinks"] = _attn_bwd(q_r, k_r, proj, C_VA // 128, sinks, o_attn, lse, dy_attn,
                                                        nblk=nblk, name="attn_bwd")
    dq_r = jnp.pad(dq_real, ((BLOCK, 0), (0, 0)))
    (dq,) = _rows_fwd(_rope_bwd_fn, [_view(dq_r), _view(rope_c), _view(rope_s1), _view(rope_s2)], [], [512],
                      nblk=nall, name="rope_q_bwd", out_dtype=BF16, tile=TILE_ALL_BIG)
    (dka,) = _rows_fwd(_rope_bwd_fn, [_view(dk_r), _view(rope_c), _view(rope_s1),
                                      _view(rope_s2)], [], [128], nblk=nall, name="rope_k_bwd", out_dtype=BF16,
                       tile=TILE_ALL_BIG)

    lead = lambda t: jnp.pad(t, ((BLOCK, 0), (0, 0)))
    pieces = [lead(dg1), lead(dg2), d_rw[0], d_rw[1], d_rw[2], d_rw[5], d_rw[3], d_rw[4], dq, dka, dva,
              jnp.zeros((lp, NP - C_VA - 128), BF16)]
    dproj = jnp.concatenate([p.astype(BF16) for p in pieces], axis=1)
    grads["w_in_p"] = _mm(u, dproj, ta=True, name="dw_in")
    du, db_in_p, *last_parts = _mm(dproj, w_in_p, tb=True, name="d_u", colsum_a=True,
                                   scattered=scatter_last(grads) if scatter_last else ())
    grads["b_in"] = _w_in_unpadded(db_in_p)
    dh, grads["norm_mix_g"] = _rows_bwd(_rms_fn, [_view(hpad)], [row("norm_mix_g")], [_view(du)], nblk=nall,
                                        name="norm_mix_bwd", diff_rows=[0], diff_consts=[0], acc=[_view(lead(dh1))],
                                        tile=TILE_ALL)
    grads["meta_tokens"] = dh[PAD_ROWS:BLOCK]
    return loss, dh[BLOCK:], grads, early_parts, last_parts


def kernel(x, meta_tokens, norm_mix_g, w_in, b_in, attn_sinks, rwkv_mix, rwkv_w0, rwkv_w2, rwkv_a0, rwkv_a2, rwkv_g2, rwkv_k_k, rwkv_k_a, rwkv_r_k, rwkv_ln_w, rwkv_ln_b, w_br_attn, w_br_rwkv, w_o, norm_ffn_g, w_ffn_gate, w_ffn_up, w_ffn_down, norm_final_g, loss_target, m_meta_tokens, m_norm_mix_g, m_w_in, m_b_in, m_attn_sinks, m_rwkv_mix, m_rwkv_w0, m_rwkv_w2, m_rwkv_a0, m_rwkv_a2, m_rwkv_g2, m_rwkv_k_k, m_rwkv_k_a, m_rwkv_r_k, m_rwkv_ln_w, m_rwkv_ln_b, m_w_br_attn, m_w_br_rwkv, m_w_o, m_norm_ffn_g, m_w_ffn_gate, m_w_ffn_up, m_w_ffn_down, m_norm_final_g, v_meta_tokens, v_norm_mix_g, v_w_in, v_b_in, v_attn_sinks, v_rwkv_mix, v_rwkv_w0, v_rwkv_w2, v_rwkv_a0, v_rwkv_a2, v_rwkv_g2, v_rwkv_k_k, v_rwkv_k_a, v_rwkv_r_k, v_rwkv_ln_w, v_rwkv_ln_b, v_w_br_attn, v_w_br_rwkv, v_w_o, v_norm_ffn_g, v_w_ffn_gate, v_w_ffn_up, v_w_ffn_down, v_norm_final_g):
    given = dict(locals())
    wts = {n: _strip(n, given[n]) for n in WEIGHTS}
    as_rows = lambda a: a.reshape(1, -1) if a.ndim == 1 else a
    width = wts["w_in"].shape[1]
    wire = lambda table: [wts[n].astype(BF16) for n, _ in table]

    w_in_all, *early_all = _all_gather([wts["w_in"].astype(BF16)] + wire(EARLY), name="gather_weights")
    full = {n: wts[n] for n in REPLICATED}
    full.update({n: _join(g, axis) for (n, axis), g in zip(EARLY, early_all)})
    full["w_in_p"] = _w_in_padded(w_in_all, shard_width=width)
    gather_late = (wire(LATE), lambda got: {n: _join(g, axis) for (n, axis), g in zip(LATE, got)})
    scatter_early = lambda g: [_split(g[n], axis).astype(BF16) for n, axis in LATE]
    scatter_last = lambda g: [jnp.stack([_w_in_unpadded(g["w_in_p"], p * width, (p + 1) * width)
                                         for p in range(N_DEV)]).astype(BF16)]

    loss_part, grad_x, grads, parts_late, (parts_w_in,) = _device_step(
        x[0], loss_target[0], full, gather_late, scatter_early, scatter_last)

    g_early = [_split(grads[n], axis).astype(BF16) for n, axis in EARLY]
    g_small = [grads[n].reshape(as_rows(given[n]).shape) for n in REPLICATED] + [jnp.full((8, LANES), loss_part)]
    got = _exchange(g_early, g_small, name="exchange_grads")
    parts_early, parts_small, parts_loss = got[:len(EARLY)], got[len(EARLY):-1], got[-1]
    results = [{}, {}, {}, {}]
    for (n, _), parts in zip([("w_in", 1)] + EARLY + LATE, [parts_w_in] + list(parts_early) + list(parts_late)):
        for kind, a in enumerate(_adamw(parts, given[n], given["m_" + n], given["v_" + n], name="adamw_" + n)):
            results[kind][n] = a
    small, loss = _adamw_replicated(parts_small, *[[as_rows(given[pre + n]) for n in REPLICATED]
                                                   for pre in ("", "m_", "v_")], parts_loss, name="adamw_replicated")
    for n, four in zip(REPLICATED, small):
        for kind, a in enumerate(four):
            results[kind][n] = a
    loss = loss[0, 0]
    out = [loss, grad_x[None]]
    for kind in range(4):
        out += [results[kind][n].reshape(given[n].shape) for n in WEIGHTS]
    return tuple(out)
```

```python
import jax
import jax.numpy as jnp
from jax import lax
from jax.experimental import pallas as pl
from jax.experimental.pallas import tpu as pltpu

F32 = jnp.float32
BF16 = jnp.bfloat16

N_DEV = 8
D_MODEL = 1024
N_META = 16
BLOCK = 128
PAD_ROWS = BLOCK - N_META
HEAD_DIM = 64
Q_HEADS = 8
KV_HEADS = 2
GROUP = Q_HEADS // KV_HEADS
ROPE_DIM = HEAD_DIM // 4
ROPE_HALF = ROPE_DIM // 2
ROPE_THETA = 500000.0
RW_HEADS = 8
RW_DIM = 512
DECAY_LORA = 64
AAA_LORA = 64
GATE_LORA = 160
D_FF = 2816
D_IN = 4640
RMS_EPS = 1e-6
RWKV_LN_EPS = 64e-5
NEG_INF = -1e30
SCAN_T = 16
SCAN_CHUNKS = 8
LANES = 128
PACK_ROWS = 256
TILE_ALL = 384
TILE_ALL_BIG = 1408
TILE_REAL = 512
TILE_WIDE = 256

ADAM_LR = 0.001
ADAM_B1 = 0.9
ADAM_B2 = 0.999
ADAM_EPS = 1e-08
ADAM_WD = 0.01
ADAM_STEP = 10

C_G1, C_G2 = 0, 1024
C_R, C_K, C_V, C_DG, C_DW, C_DA = 2048, 2560, 3072, 3584, 3840, 3968
C_Q, C_KA, C_VA = 4096, 4608, 4736
NP = 5120

VMEM_LIMIT = 48 * 1024 * 1024
SCAN_VMEM_LIMIT = 60 * 1024 * 1024


def _cparams(sem, vmem=VMEM_LIMIT):
    return pltpu.CompilerParams(dimension_semantics=sem, vmem_limit_bytes=vmem)


def _pick(n, cands):
    for c in cands:
        if n % c == 0:
            return c
    raise ValueError(f"no tile for {n}")


def _mm(a, b, *, ta=False, tb=False, bias=None, residual=None, name, scattered=(), colsum_a=False):
    m = a.shape[1] if ta else a.shape[0]
    k = a.shape[0] if ta else a.shape[1]
    n = b.shape[0] if tb else b.shape[1]
    assert k == (b.shape[1] if tb else b.shape[0]), (a.shape, b.shape, ta, tb)
    tm = _pick(m, (512, 1408, 256, 128) if ta else (1056, 1024, 528, 512, 384, 256, 128))
    tn = _pick(n, (1024, 512, 1408, 256, 128))
    if k <= 1024:
        tk = k
    else:
        tk = _pick(k, (1024, 1056, 528, 512) if (ta and not tb) else (1024, 1408, 512, 256, 128))
    nk = k // tk
    has_bias = bias is not None
    has_res = residual is not None
    dn = (((0 if ta else 1,), (1 if tb else 0,)), ((), ()))

    def body(*refs):
        a_ref, b_ref = refs[0], refs[1]
        pos = 2
        bias_ref = res_ref = None
        if has_bias:
            bias_ref = refs[pos]
            pos += 1
        if has_res:
            res_ref = refs[pos]
            pos += 1
        o_ref, acc_ref = refs[pos], refs[-1]
        kk = pl.program_id(2)
        if colsum_a:
            cs_ref = refs[pos + 1]

            @pl.when((pl.program_id(1) == 0) & (pl.program_id(0) == 0))
            def _():
                cs_ref[kk] = jnp.sum(a_ref[...].astype(F32), axis=0, keepdims=True)

            @pl.when((pl.program_id(1) == 0) & (pl.program_id(0) > 0))
            def _():
                cs_ref[kk] += jnp.sum(a_ref[...].astype(F32), axis=0, keepdims=True)
        part = lax.dot_general(a_ref[...].astype(BF16), b_ref[...].astype(BF16), dn, preferred_element_type=F32)

        def finish(out):
            if has_bias:
                out = out + bias_ref[...]
            if has_res:
                out = out + res_ref[...]
            o_ref[...] = out

        if nk == 1:
            finish(part)
        else:
            @pl.when(kk == 0)
            def _():
                acc_ref[...] = part

            @pl.when((kk > 0) & (kk < nk - 1))
            def _():
                acc_ref[...] += part

            @pl.when(kk == nk - 1)
            def _():
                finish(acc_ref[...] + part)

    in_specs = [
        pl.BlockSpec((tk, tm), lambda i, j, kk: (kk, i)) if ta else pl.BlockSpec((tm, tk), lambda i, j, kk: (i, kk)),
        pl.BlockSpec((tn, tk), lambda i, j, kk: (j, kk)) if tb else pl.BlockSpec((tk, tn), lambda i, j, kk: (kk, j)),
    ]
    args = [a, b]
    if has_bias:
        in_specs.append(pl.BlockSpec((1, tn), lambda i, j, kk: (0, j)))
        args.append(bias)
    if has_res:
        in_specs.append(pl.BlockSpec((tm, tn), lambda i, j, kk: (i, j)))
        args.append(residual)
    grid = (m // tm, n // tn, nk)
    n_x = len(scattered)
    assert not (colsum_a and ta)
    sums_spec = [pl.BlockSpec((nk, 1, tk), lambda i, j, kk: (0, 0, 0))] if colsum_a else []
    sums_shape = [jax.ShapeDtypeStruct((nk, 1, tk), F32)] if colsum_a else []
    serial = n_x or colsum_a
    out = pl.pallas_call(
        _with_exchange(body, len(args), 1 + len(sums_spec), 1, scattered, (), grid), name=name, grid=grid,
        in_specs=in_specs + [ANY] * n_x,
        out_specs=[pl.BlockSpec((tm, tn), lambda i, j, kk: (i, j))] + sums_spec + [ANY] * n_x,
        out_shape=[jax.ShapeDtypeStruct((m, n), F32)] + sums_shape + _exchange_shapes(scattered, ()),
        scratch_shapes=[pltpu.VMEM((tm, tn) if nk > 1 else (8, LANES), F32)] + (_exchange_sems(n_x) if n_x else []),
        compiler_params=_cparams(("arbitrary",) * 3 if serial else ("parallel", "parallel", "arbitrary")),
    )(*args, *scattered)
    out = list(out)
    if colsum_a:
        out[1] = out[1].reshape(1, k)
    return out if (n_x or colsum_a) else out[0]


def _mm_swiglu(a, w_gate, w_up, *, name):
    m, k = a.shape
    n = w_gate.shape[1]
    assert w_gate.shape == (k, n) and w_up.shape == (k, n) and k <= 1024, (a.shape, w_gate.shape, w_up.shape)
    tm = _pick(m, (512, 256, 128))
    tn = _pick(n, (1024, 512, 1408, 256, 128))

    def body(a_ref, gate_w_ref, up_w_ref, gate_ref, up_ref, act_ref):
        lhs = a_ref[...].astype(BF16)
        gate = jnp.dot(lhs, gate_w_ref[...].astype(BF16), preferred_element_type=F32)
        up = jnp.dot(lhs, up_w_ref[...].astype(BF16), preferred_element_type=F32)
        gate_ref[...] = gate
        up_ref[...] = up
        (act,) = _swiglu_fn(None, gate, up)
        act_ref[...] = act.astype(BF16)

    w_spec = pl.BlockSpec((k, tn), lambda i, j: (0, j))
    o_spec = pl.BlockSpec((tm, tn), lambda i, j: (i, j))
    return pl.pallas_call(
        body, name=name, grid=(m // tm, n // tn),
        in_specs=[pl.BlockSpec((tm, k), lambda i, j: (i, 0)), w_spec, w_spec],
        out_specs=[o_spec, o_spec, o_spec],
        out_shape=[jax.ShapeDtypeStruct((m, n), F32), jax.ShapeDtypeStruct((m, n), F32),
                   jax.ShapeDtypeStruct((m, n), BF16)],
        compiler_params=_cparams(("parallel", "parallel")),
    )(a, w_gate, w_up)


def _view(arr, width=None, col=0, off=0, rows=BLOCK):
    return (arr, arr.shape[1] if width is None else width, col, off, rows)


def _row_spec(view):
    _, width, col, off, rows = view
    if off < 0:
        return pl.BlockSpec((rows, width), lambda i, col=col, off=off: (jnp.maximum(i + off, 0), col))
    return pl.BlockSpec((rows, width), lambda i, col=col, off=off: (i + off, col))


def _const_spec(arr):
    return pl.BlockSpec(arr.shape, lambda i: (0,) * arr.ndim)


def _retile(views, tile):
    assert all(v[3] == 0 and v[4] == BLOCK for v in views)
    return [v[:4] + (tile,) for v in views]


def _rows_fwd(fn, rows, consts, out_widths, *, nblk, name, out_dtype=F32, tile=BLOCK):
    nr, nc = len(rows), len(consts)
    tile = tile if (nblk * BLOCK) % tile == 0 else BLOCK
    if tile != BLOCK:
        rows, nblk = _retile(rows, tile), nblk * BLOCK // tile
    out_blocks = [(tile, w) if isinstance(w, int) else w for w in out_widths]

    def body(*refs):
        i = pl.program_id(0)
        vals = [r[...] for r in refs[:nr + nc]]
        outs = fn(i, *vals)
        for o_ref, o in zip(refs[nr + nc:], outs):
            o_ref[...] = o.astype(o_ref.dtype)

    return pl.pallas_call(
        body, name=name, grid=(nblk,),
        in_specs=[_row_spec(v) for v in rows] + [_const_spec(c) for c in consts],
        out_specs=[pl.BlockSpec(b, lambda i: (i, 0)) for b in out_blocks],
        out_shape=[jax.ShapeDtypeStruct((nblk * r, w), out_dtype) for r, w in out_blocks],
        compiler_params=_cparams(("parallel",)),
    )(*[v[0] for v in rows], *consts)


def _rows_bwd(fn, rows, consts, cts, *, nblk, name, diff_rows, diff_consts, acc=None, fwd_widths=(), row_dtype=F32,
              ct_map=None, tile=BLOCK):
    nr, nc = len(rows), len(consts)
    acc = acc or [None] * len(diff_rows)
    tile = tile if (nblk * BLOCK) % tile == 0 else BLOCK
    if tile != BLOCK:
        rows, nblk = _retile(rows, tile), nblk * BLOCK // tile
        cts = [c if c is None else _retile([c], tile)[0] for c in cts]
        acc = [a if a is None else _retile([a], tile)[0] for a in acc]
    ct_views = [c for c in cts if c is not None]
    acc_views = [a for a in acc if a is not None]
    n_in = nr + nc + len(ct_views) + len(acc_views)
    n_fwd = len(fwd_widths)

    def body(*refs):
        i = pl.program_id(0)
        row_vals = [r[...] for r in refs[:nr]]
        const_vals = [r[...] for r in refs[nr:nr + nc]]
        ct_vals = [r[...] for r in refs[nr + nc:nr + nc + len(ct_views)]]
        acc_vals = [r[...] for r in refs[nr + nc + len(ct_views):n_in]]
        out_refs = refs[n_in:]

        def f(*dargs):
            rv = list(row_vals)
            cv = list(const_vals)
            for pos, idx in enumerate(diff_rows):
                rv[idx] = dargs[pos]
            for pos, idx in enumerate(diff_consts):
                cv[idx] = dargs[len(diff_rows) + pos]
            return tuple(fn(i, *rv, *cv))

        primals = [row_vals[idx] for idx in diff_rows] + [const_vals[idx] for idx in diff_consts]
        outs, pull = jax.vjp(f, *primals)
        full_ct, ci = [], 0
        if ct_map is not None:
            full_ct = ct_map(i, *ct_vals)
        else:
            for o, c in zip(outs, cts):
                if c is None:
                    full_ct.append(jnp.zeros_like(o))
                else:
                    full_ct.append(ct_vals[ci])
                    ci += 1
        grads = pull(tuple(full_ct))
        for o_ref, o in zip(out_refs[:n_fwd], outs):
            o_ref[...] = o
        ai = 0
        for pos in range(len(diff_rows)):
            g = grads[pos]
            if acc[pos] is not None:
                g = g + acc_vals[ai]
                ai += 1
            out_refs[n_fwd + pos][...] = g.astype(row_dtype)
        for pos in range(len(diff_consts)):
            g = grads[len(diff_rows) + pos]
            o_ref = out_refs[n_fwd + len(diff_rows) + pos]

            @pl.when(i == 0)
            def _(o_ref=o_ref, g=g):
                o_ref[...] = g

            @pl.when(i > 0)
            def _(o_ref=o_ref, g=g):
                o_ref[...] += g

    out_specs = [pl.BlockSpec((tile, w), lambda i: (i, 0)) for w in fwd_widths]
    out_shape = [jax.ShapeDtypeStruct((nblk * tile, w), F32) for w in fwd_widths]
    for idx in diff_rows:
        out_specs.append(pl.BlockSpec((tile, rows[idx][1]), lambda i: (i, 0)))
        out_shape.append(jax.ShapeDtypeStruct((nblk * tile, rows[idx][1]), row_dtype))
    for idx in diff_consts:
        out_specs.append(_const_spec(consts[idx]))
        out_shape.append(jax.ShapeDtypeStruct(consts[idx].shape, F32))
    return pl.pallas_call(
        body, name=name, grid=(nblk,),
        in_specs=([_row_spec(v) for v in rows] + [_const_spec(c) for c in consts]
                  + [_row_spec(v) for v in ct_views] + [_row_spec(v) for v in acc_views]),
        out_specs=out_specs, out_shape=out_shape,
        compiler_params=_cparams(("arbitrary",)),
    )(*[v[0] for v in rows], *consts, *[v[0] for v in ct_views], *[v[0] for v in acc_views])


def _rms_fn(i, x, g):
    return (x * lax.rsqrt(jnp.mean(x * x, axis=-1, keepdims=True) + RMS_EPS) * g,)


def _sigmoid(x):
    return 1.0 / (1.0 + jnp.exp(-x))


def _softplus(x):
    return jnp.maximum(x, 0.0) + jnp.log(1.0 + jnp.exp(-jnp.abs(x)))


def _split2(x):
    hi = x.astype(BF16)
    lo = (x - hi.astype(F32)).astype(BF16)
    return jnp.concatenate([hi, lo], axis=1)


@jax.custom_vjp
def _head_sum(x):
    r = lax.broadcasted_iota(jnp.int32, (2 * RW_DIM, RW_DIM), 0) % RW_DIM // HEAD_DIM
    c = lax.broadcasted_iota(jnp.int32, (2 * RW_DIM, RW_DIM), 1) // HEAD_DIM
    return jnp.dot(_split2(x), (r == c).astype(BF16), preferred_element_type=F32)


_head_sum.defvjp(lambda x: (_head_sum(x), None), lambda _, ct: (_head_sum(ct),))


@jax.custom_vjp
def _dot_bf16(x, w):
    return jnp.dot(x.astype(BF16), w.astype(BF16), preferred_element_type=F32)


def _dot_bf16_bwd(res, ct):
    x, w = res
    ct = ct.astype(BF16)
    dx = lax.dot_general(ct, w.astype(BF16), (((1,), (1,)), ((), ())), preferred_element_type=F32)
    dw = lax.dot_general(x.astype(BF16), ct, (((0,), (0,)), ((), ())), preferred_element_type=F32)
    return dx, dw


_dot_bf16.defvjp(lambda x, w: (_dot_bf16(x, w), (x, w)), _dot_bf16_bwd)


def _rwkv_pre_fn(i, r, k, v, dw, da, dg, r_p, k_p, v_p, dw_p, da_p, dg_p,
                 mix_r, mix_k, mix_v, mix_dw, mix_da, mix_dg, w0, w2, a0, a2, g2, k_k, k_a):
    row = i * BLOCK + lax.broadcasted_iota(jnp.int32, (BLOCK, 1), 0)
    live = row >= PAD_ROWS
    live_prev = row >= PAD_ROWS + 1

    def shift(cur, prev, mix):
        cur = jnp.where(live, cur, 0.0)
        prev = jnp.where(live_prev, prev, 0.0)
        return cur + (prev - cur) * mix

    r = shift(r, r_p, mix_r)
    k = shift(k, k_p, mix_k)
    v = shift(v, v_p, mix_v)
    dw = shift(dw, dw_p, mix_dw)
    da = shift(da, da_p, mix_da)
    dg = shift(dg, dg_p, mix_dg)
    wlog = -_softplus(-(w0 + _dot_bf16(jnp.tanh(dw), w2))) - 0.5
    decay = jnp.exp(-jnp.exp(wlog))
    a = _sigmoid(a0 + _dot_bf16(da, a2))
    g = _dot_bf16(_sigmoid(dg), g2)
    kk = k * k_k
    norm_sq = jnp.where(live, _head_sum(kk * kk), 1.0)
    kk = kk / jnp.maximum(jnp.sqrt(norm_sq), 1e-12)
    k_mod = k * (1.0 + (a - 1.0) * k_a)
    return r, decay, k_mod, v, -kk, kk * a, g


def _rwkv_pre_xt_fn(i, *args):
    r, decay, k_mod, v, a_neg, b, g = _rwkv_pre_fn(i, *args)
    t = SCAN_T
    xt = jnp.concatenate([_rows_to_xt(x[c * t:(c + 1) * t]) for c in range(BLOCK // t)
                          for x in (a_neg, decay, b, k_mod, r)], axis=0)
    return r, k_mod, v, g, xt


def _rwkv_pre_cts(i, dxt, dv_s, dr_p, dk_p, dv_p, dg_p):
    t = SCAN_T
    d_a, d_w, d_b, d_k, d_r = [
        jnp.concatenate([_xt_to_rows(dxt[c * VEC_ROWS + n * HEAD_DIM:c * VEC_ROWS + (n + 1) * HEAD_DIM])
                         for c in range(BLOCK // t)], axis=0) for n in range(N_VEC)]
    dr_p, dk_p, dv_p, dg_p = [jnp.where(i > 0, x, 0.0) for x in (dr_p, dk_p, dv_p, dg_p)]
    return d_r + dr_p, d_w, d_k + dk_p, dv_s + dv_p, d_a, d_b, dg_p


def _rwkv_post_fn(i, ys, r, k_mod, v, g, ln_w, ln_b, r_k):
    mean = _head_sum(ys) * (1.0 / HEAD_DIM)
    d = ys - mean
    var = _head_sum(d * d) * (1.0 / HEAD_DIM)
    yn = d * lax.rsqrt(var + RWKV_LN_EPS) * ln_w + ln_b
    bonus = _head_sum(r * k_mod * r_k) * v
    return ((yn + bonus) * g,)


def _merge_fn(i, ya, yr, g1, g2):
    return (_sigmoid(g1) * ya + _sigmoid(g2) * yr,)


def _swiglu_fn(i, gate, up):
    return (gate * _sigmoid(gate) * up,)


def _loss_fn(i, h, tgt, g):
    y = h * lax.rsqrt(jnp.mean(h * h, axis=-1, keepdims=True) + RMS_EPS) * g
    err = y - tgt
    return (0.5 * jnp.mean(err * err, axis=-1, keepdims=True),)


def _rope_tables(lp):
    pos = (jnp.arange(lp, dtype=jnp.int32) - PAD_ROWS).astype(F32)
    inv_freq = jnp.power(jnp.float32(ROPE_THETA), -jnp.arange(ROPE_HALF, dtype=F32) * (2.0 / ROPE_DIM))
    ang = pos[:, None] * inv_freq[None, :]
    cos, sin = jnp.cos(ang), jnp.sin(ang)
    one = jnp.ones((lp, HEAD_DIM - ROPE_DIM), F32)
    zero_h = jnp.zeros((lp, ROPE_HALF), F32)
    zero_r = jnp.zeros((lp, HEAD_DIM - ROPE_DIM), F32)
    c = jnp.concatenate([cos, cos, one], axis=1)
    s1 = jnp.concatenate([-sin, zero_h, zero_r], axis=1)
    s2 = jnp.concatenate([zero_h, sin, zero_r], axis=1)
    return tuple(jnp.tile(t, (1, LANES // HEAD_DIM)) for t in (c, s1, s2))


def _rope_fwd_fn(i, x, c, s1, s2):
    n = x.shape[1]
    c, s1, s2 = [jnp.tile(t, (1, n // LANES)) for t in (c, s1, s2)]
    return (x * c + pltpu.roll(x, n - ROPE_HALF, 1) * s1 + pltpu.roll(x, ROPE_HALF, 1) * s2,)


def _rope_bwd_fn(i, dy, c, s1, s2):
    n = dy.shape[1]
    c, s1, s2 = [jnp.tile(t, (1, n // LANES)) for t in (c, s1, s2)]
    return (dy * c + pltpu.roll(dy * s1, ROPE_HALF, 1) + pltpu.roll(dy * s2, n - ROPE_HALF, 1),)


def _attn_mask(i):
    r = lax.broadcasted_iota(jnp.int32, (BLOCK, 3 * BLOCK), 0)
    c = lax.broadcasted_iota(jnp.int32, (BLOCK, 3 * BLOCK), 1)
    meta = (c < BLOCK) & (c >= PAD_ROWS)
    prev = (c >= BLOCK) & (c < 2 * BLOCK) & ((c - BLOCK) > r) & (i >= 1)
    cur = (c >= 2 * BLOCK) & ((c - 2 * BLOCK) <= r)
    return meta | prev | cur


def _attn_rows(ref, g):
    return ref[:, g * HEAD_DIM:(g + 1) * HEAD_DIM]


def _attn_group(i, g, q_all, k_refs, v_refs, s_ref):
    heads = range(g * GROUP, (g + 1) * GROUP)
    kcat = jnp.concatenate([_attn_rows(r, g) for r in k_refs], axis=0).astype(BF16)
    vcat = jnp.concatenate([_attn_rows(r, g) for r in v_refs], axis=0).astype(BF16)
    qg = jnp.concatenate([q_all[:, h * HEAD_DIM:(h + 1) * HEAD_DIM] for h in heads], axis=0).astype(BF16)
    sink = jnp.concatenate([jnp.broadcast_to(s_ref[0:1, h:h + 1], (BLOCK, 1)) for h in heads], axis=0)
    s = lax.dot_general(qg, kcat, (((1,), (1,)), ((), ())), preferred_element_type=F32) * (HEAD_DIM ** -0.5)
    valid = jnp.concatenate([_attn_mask(i)] * GROUP, axis=0)
    return heads, qg, kcat, vcat, sink, jnp.where(valid, s, NEG_INF)


ATTN_SUB = 2


def _attn_specs(v_col):
    q = [pl.BlockSpec((BLOCK, Q_HEADS * HEAD_DIM), lambda i, n=n: (ATTN_SUB * i + 1 + n, 0)) for n in range(ATTN_SUB)]
    blk = lambda col: ([pl.BlockSpec((BLOCK, 2 * HEAD_DIM), lambda i: (0, col))]
                       + [pl.BlockSpec((BLOCK, 2 * HEAD_DIM), lambda i, n=n: (ATTN_SUB * i + n, col))
                          for n in range(ATTN_SUB + 1)])
    return q + blk(0) + blk(v_col) + [pl.BlockSpec((1, Q_HEADS), lambda i: (0, 0))]


def _attn_split(refs):
    n = ATTN_SUB
    q_refs, k_refs, v_refs = refs[:n], refs[n:2 * n + 2], refs[2 * n + 2:3 * n + 4]
    return q_refs, k_refs, v_refs, refs[3 * n + 4], refs[3 * n + 5:]


def _attn_fwd_block(blk, q_ref, keys, vals, s_ref, o_ref, lse_ref):
    q_all = q_ref[...]
    for g in range(KV_HEADS):
        heads, _, _, vcat, sink, s = _attn_group(blk, g, q_all, keys, vals, s_ref)
        m = jnp.maximum(jnp.max(s, axis=-1, keepdims=True), sink)
        p = jnp.exp(s - m)
        den = jnp.sum(p, axis=-1, keepdims=True) + jnp.exp(sink - m)
        o = jnp.dot(p.astype(BF16), vcat, preferred_element_type=F32) / den
        lse = m + jnp.log(den)
        for n, h in enumerate(heads):
            o_ref[:, h * HEAD_DIM:(h + 1) * HEAD_DIM] = o[n * BLOCK:(n + 1) * BLOCK]
            lse_ref[:, h:h + 1] = lse[n * BLOCK:(n + 1) * BLOCK]


def _attn_bwd(q, k, v, v_col, sinks, o, lse, do, *, nblk, name):
    lp = k.shape[0]
    rows = ATTN_SUB * BLOCK

    def body(*refs):
        q_refs, k_refs, v_refs, s_ref, (o_ref, lse_ref, do_ref, dq_ref, dk_ref, dv_ref, ds_ref) = _attn_split(refs)
        i = pl.program_id(0)

        @pl.when(i == 0)
        def _():
            dk_ref[...] = jnp.zeros_like(dk_ref)
            dv_ref[...] = jnp.zeros_like(dv_ref)
            ds_ref[...] = jnp.zeros_like(ds_ref)

        lane = lax.broadcasted_iota(jnp.int32, (1, Q_HEADS), 1)
        for sub in range(ATTN_SUB):
            at = slice(sub * BLOCK, (sub + 1) * BLOCK)
            blk = ATTN_SUB * i + sub
            keys = (k_refs[0], k_refs[1 + sub], k_refs[2 + sub])
            vals = (v_refs[0], v_refs[1 + sub], v_refs[2 + sub])
            prev_rows = pl.ds(pl.multiple_of(blk * BLOCK, BLOCK), BLOCK)
            cur_rows = pl.ds(pl.multiple_of((blk + 1) * BLOCK, BLOCK), BLOCK)
            q_all, o_all, do_all, lse_all = q_refs[sub][...], o_ref[at, :], do_ref[at, :], lse_ref[at, :]
            for g in range(KV_HEADS):
                heads, qg, kcat, vcat, sink, s = _attn_group(blk, g, q_all, keys, vals, s_ref)
                stack = lambda x: jnp.concatenate([x[:, h * HEAD_DIM:(h + 1) * HEAD_DIM] for h in heads], axis=0)
                lse_g = jnp.concatenate([lse_all[:, h:h + 1] for h in heads], axis=0)
                do_g = stack(do_all)
                p = jnp.exp(s - lse_g)
                delta = jnp.sum(do_g * stack(o_all), axis=-1, keepdims=True)
                dp = lax.dot_general(do_g.astype(BF16), vcat, (((1,), (1,)), ((), ())), preferred_element_type=F32)
                dsc = (p * (dp - delta) * (HEAD_DIM ** -0.5)).astype(BF16)
                dq = jnp.dot(dsc, kcat, preferred_element_type=F32)
                dk_all = lax.dot_general(dsc, qg, (((0,), (0,)), ((), ())), preferred_element_type=F32)
                dv_all = lax.dot_general(p.astype(BF16), do_g.astype(BF16), (((0,), (0,)), ((), ())),
                                         preferred_element_type=F32)
                cols = slice(g * HEAD_DIM, (g + 1) * HEAD_DIM)
                for ref, full in ((dk_ref, dk_all), (dv_ref, dv_all)):
                    ref[0:BLOCK, cols] += full[0:BLOCK]
                    ref[prev_rows, cols] += full[BLOCK:2 * BLOCK]
                    ref[cur_rows, cols] += full[2 * BLOCK:]
                sink_part = jnp.exp(sink - lse_g) * delta
                for n, h in enumerate(heads):
                    dq_ref[at, h * HEAD_DIM:(h + 1) * HEAD_DIM] = dq[n * BLOCK:(n + 1) * BLOCK]
                    dsink = -jnp.sum(sink_part[n * BLOCK:(n + 1) * BLOCK], axis=0, keepdims=True)
                    ds_ref[...] += jnp.where(lane == h, dsink, 0.0)

    qspec = pl.BlockSpec((rows, Q_HEADS * HEAD_DIM), lambda i: (i, 0))
    whole = pl.BlockSpec((lp, 2 * HEAD_DIM), lambda i: (0, 0))
    return pl.pallas_call(
        body, name=name, grid=(nblk // ATTN_SUB,),
        in_specs=_attn_specs(v_col) + [qspec, pl.BlockSpec((rows, Q_HEADS), lambda i: (i, 0)), qspec],
        out_specs=[qspec, whole, whole, pl.BlockSpec((1, Q_HEADS), lambda i: (0, 0))],
        out_shape=[jax.ShapeDtypeStruct((nblk * BLOCK, Q_HEADS * HEAD_DIM), F32),
                   jax.ShapeDtypeStruct((lp, 2 * HEAD_DIM), F32), jax.ShapeDtypeStruct((lp, 2 * HEAD_DIM), F32),
                   jax.ShapeDtypeStruct((1, Q_HEADS), F32)],
        compiler_params=_cparams(("arbitrary",)),
    )(*[q] * ATTN_SUB, *[k] * (ATTN_SUB + 2), *[v] * (ATTN_SUB + 2), sinks, o, lse, do)


N_VEC = 5
VEC_ROWS = N_VEC * HEAD_DIM


def _selectors():
    t = SCAN_T
    shape = (t, 2 * LANES, RW_DIM)
    step, src, dst = [lax.broadcasted_iota(jnp.int32, shape, d) for d in range(3)]
    src = src % LANES
    spread = ((src // t == dst // HEAD_DIM) & (src % t == step)).astype(BF16)
    shape = (t, RW_DIM, LANES)
    step, src, dst = [lax.broadcasted_iota(jnp.int32, shape, d) for d in range(3)]
    collect = ((src // HEAD_DIM == dst // t) & (dst % t == step)).astype(BF16)
    return spread, collect


def _rows_to_xt(x):
    low = lax.broadcasted_iota(jnp.int32, (SCAN_T, LANES), 1) < HEAD_DIM
    pieces = []
    for m in range(RW_HEADS // 2):
        pair = x[:, m * LANES:(m + 1) * LANES]
        pieces += [jnp.where(low, pair, 0.0), jnp.where(low, pltpu.roll(pair, HEAD_DIM, 1), 0.0)]
    return jnp.concatenate(pieces, axis=0).T[:HEAD_DIM]


def _xt_to_rows(a):
    t = SCAN_T
    a_t = jnp.concatenate([a, jnp.zeros_like(a)], axis=0).T
    pairs = [a_t[2 * m * t:(2 * m + 1) * t] + pltpu.roll(a_t[(2 * m + 1) * t:(2 * m + 2) * t], HEAD_DIM, 1)
             for m in range(RW_HEADS // 2)]
    return jnp.concatenate(pairs, axis=1)


def _with_exchange(compute, n_in, n_out, n_scratch, scattered, shared, grid):
    n_sc = len(scattered)
    n_x = n_sc + len(shared)
    if n_x == 0:
        return compute

    def body(*refs):
        ins, x_in = refs[:n_in], refs[n_in:n_in + n_x]
        outs, x_out = refs[n_in + n_x:n_in + n_x + n_out], refs[n_in + n_x + n_out:n_in + 2 * n_x + n_out]
        scratch = refs[n_in + 2 * n_x + n_out:n_in + 2 * n_x + n_out + n_scratch]
        sems = refs[n_in + 2 * n_x + n_out + n_scratch:]

        first = last = True
        for d, size in enumerate(grid):
            first = first & (pl.program_id(d) == 0)
            last = last & (pl.program_id(d) == size - 1)

        @pl.when(first)
        def _():
            for cp in _exchange_copies(x_in, x_out, n_sc, *sems):
                cp.start()

        compute(*ins, *outs, *scratch)

        @pl.when(last)
        def _():
            for cp in _exchange_copies(x_in, x_out, n_sc, *sems):
                cp.wait()

    return body


def _wkv_fwd(xt, v, spread, attn, name, shared=()):
    t_steps = SCAN_T
    nch = xt.shape[0]
    per = SCAN_CHUNKS
    grid = (nch // per,)
    rows = per * t_steps
    assert rows == BLOCK
    n_x = len(shared)
    q, k, v_arr, v_col, sinks = attn

    def compute(xt_ref, v_ref, sel_ref, q_ref, km_ref, kp_ref, kc_ref, vm_ref, vp_ref, vc_ref, s_ref,
                y_ref, hist_ref, o_ref, lse_ref, st_ref):
        @pl.when(pl.program_id(0) == 0)
        def _():
            st_ref[...] = jnp.zeros_like(st_ref)

        _attn_fwd_block(pl.program_id(0) - 1, q_ref, (km_ref, kp_ref, kc_ref), (vm_ref, vp_ref, vc_ref), s_ref,
                        o_ref, lse_ref)
        st = st_ref[...]
        for c in range(per):
            x2 = _split2(xt_ref[c])
            for j in range(t_steps):
                row = c * t_steps + j
                cols = jnp.dot(x2, sel_ref[j], preferred_element_type=F32)
                a_c, w_c, b_c, k_c, r_c = [cols[n * HEAD_DIM:(n + 1) * HEAD_DIM] for n in range(N_VEC)]
                hist_ref[row] = st
                sa = jnp.sum(st * a_c, axis=0, keepdims=True)
                st = st * w_c + b_c * sa + k_c * v_ref[row:row + 1, :]
                y_ref[row:row + 1, :] = jnp.sum(st * r_c, axis=0, keepdims=True)
        st_ref[...] = st

    before = lambda c: jnp.maximum(c - 1, 0)
    kv = lambda col: [pl.BlockSpec((BLOCK, 2 * HEAD_DIM), lambda c: (0, col)),
                      pl.BlockSpec((BLOCK, 2 * HEAD_DIM), lambda c: (before(c), col)),
                      pl.BlockSpec((BLOCK, 2 * HEAD_DIM), lambda c: (c, col))]
    seq = (nch * t_steps) - BLOCK
    return pl.pallas_call(
        _with_exchange(compute, 11, 4, 1, (), shared, grid), name=name, grid=grid,
        in_specs=[pl.BlockSpec((per, VEC_ROWS, LANES), lambda c: (c, 0, 0)),
                  pl.BlockSpec((rows, RW_DIM), lambda c: (c, 0)),
                  pl.BlockSpec(spread.shape, lambda c: (0, 0, 0)),
                  pl.BlockSpec((BLOCK, Q_HEADS * HEAD_DIM), lambda c: (c, 0))] + kv(0) + kv(v_col)
                 + [pl.BlockSpec((1, Q_HEADS), lambda c: (0, 0))] + [ANY] * n_x,
        out_specs=[pl.BlockSpec((rows, RW_DIM), lambda c: (c, 0)),
                   pl.BlockSpec((rows, HEAD_DIM, RW_DIM), lambda c: (c, 0, 0)),
                   pl.BlockSpec((BLOCK, Q_HEADS * HEAD_DIM), lambda c: (before(c), 0)),
                   pl.BlockSpec((BLOCK, Q_HEADS), lambda c: (before(c), 0))] + [ANY] * n_x,
        out_shape=[jax.ShapeDtypeStruct((nch * t_steps, RW_DIM), F32),
                   jax.ShapeDtypeStruct((nch * t_steps, HEAD_DIM, RW_DIM), F32),
                   jax.ShapeDtypeStruct((seq, Q_HEADS * HEAD_DIM), F32),
                   jax.ShapeDtypeStruct((seq, Q_HEADS), F32)] + _exchange_shapes((), shared),
        scratch_shapes=[pltpu.VMEM((HEAD_DIM, RW_DIM), F32)] + (_exchange_sems(n_x) if n_x else []),
        compiler_params=_cparams(("arbitrary",), SCAN_VMEM_LIMIT),
    )(xt, v, spread, q, k, k, k, v_arr, v_arr, v_arr, sinks, *shared)


def _wkv_bwd(xt, v, hist, dy, spread, collect, name, scattered=()):
    t_steps = SCAN_T
    nch = xt.shape[0]
    n_x = len(scattered)
    per = SCAN_CHUNKS
    nsteps = nch // per
    grid = (nsteps,)
    rows = per * t_steps
    lead = BLOCK // rows

    def compute(xt_ref, v_ref, hist_ref, dy_ref, sel_ref, col_ref, dxt_ref, dv_ref, g_ref):
        @pl.when(pl.program_id(0) == 0)
        def _():
            g_ref[...] = jnp.zeros_like(g_ref)

        has_dy = nsteps - 1 - pl.program_id(0) >= lead
        gst = g_ref[...]
        nxt = None
        for c in reversed(range(per)):
            x2 = _split2(xt_ref[c])
            acc = jnp.zeros((VEC_ROWS, LANES), F32)
            for j in reversed(range(t_steps)):
                row = c * t_steps + j
                cols = jnp.dot(x2, sel_ref[j], preferred_element_type=F32)
                a_c, w_c, b_c, k_c, r_c = [cols[n * HEAD_DIM:(n + 1) * HEAD_DIM] for n in range(N_VEC)]
                prev = hist_ref[row]
                v_row = v_ref[row:row + 1, :]
                dy_row = jnp.where(has_dy, dy_ref[row:row + 1, :], 0.0)
                sa = jnp.sum(prev * a_c, axis=0, keepdims=True)
                if nxt is None:
                    nxt = prev * w_c + b_c * sa + k_c * v_row
                gst = gst + r_c * dy_row
                dv_ref[row:row + 1, :] = jnp.sum(gst * k_c, axis=0, keepdims=True)
                dsa = jnp.sum(gst * b_c, axis=0, keepdims=True)
                prods = jnp.concatenate([p.astype(BF16) for p in
                                         (prev * dsa, gst * prev, gst * sa, gst * v_row, nxt * dy_row)], axis=0)
                acc = acc + jnp.dot(prods, col_ref[j], preferred_element_type=F32)
                gst = gst * w_c + a_c * dsa
                nxt = prev
            dxt_ref[c] = acc
        g_ref[...] = gst

    rev3 = lambda c: (nsteps - 1 - c, 0, 0)
    rev2 = lambda c: (nsteps - 1 - c, 0)
    rowspec = pl.BlockSpec((rows, RW_DIM), rev2)
    return pl.pallas_call(
        _with_exchange(compute, 6, 2, 1, scattered, (), grid), name=name, grid=grid,
        in_specs=[pl.BlockSpec((per, VEC_ROWS, LANES), rev3), rowspec,
                  pl.BlockSpec((rows, HEAD_DIM, RW_DIM), rev3),
                  pl.BlockSpec((rows, RW_DIM), lambda c: (jnp.maximum(nsteps - 1 - c - lead, 0), 0)),
                  pl.BlockSpec(spread.shape, lambda c: (0, 0, 0)),
                  pl.BlockSpec(collect.shape, lambda c: (0, 0, 0))] + [ANY] * n_x,
        out_specs=[pl.BlockSpec((per, VEC_ROWS, LANES), rev3), rowspec] + [ANY] * n_x,
        out_shape=[jax.ShapeDtypeStruct((nch, VEC_ROWS, LANES), F32),
                   jax.ShapeDtypeStruct((nch * t_steps, RW_DIM), F32)] + _exchange_shapes(scattered, ()),
        scratch_shapes=[pltpu.VMEM((HEAD_DIM, RW_DIM), F32)] + (_exchange_sems(n_x) if n_x else []),
        compiler_params=_cparams(("arbitrary",), SCAN_VMEM_LIMIT),
    )(xt, v, hist, dy, spread, collect, *scattered)


MESH = pl.DeviceIdType.MESH
ANY = pl.BlockSpec(memory_space=pltpu.HBM)


def _all_gather(arrays, name):
    n_arr = len(arrays)
    per = N_DEV - 1

    def body(*refs):
        x_refs, out_refs = refs[:n_arr], refs[n_arr:2 * n_arr]
        send_sems, recv_sems, local_sems = refs[2 * n_arr:]
        xi, yi, ci = lax.axis_index("x"), lax.axis_index("y"), lax.axis_index("c")
        me, sibling = (xi, yi, ci), (xi, yi, 1 - ci)
        chips = [(1 - xi, yi), (xi, 1 - yi), (1 - xi, 1 - yi)]

        def slot(a, px, py, pc):
            return out_refs[a].at[4 * px + 2 * py + pc]

        def copy(a, sem, block, to, src=None):
            return pltpu.make_async_remote_copy(
                src_ref=slot(a, *block) if src is None else src, dst_ref=slot(a, *block),
                send_sem=send_sems.at[per * a + sem], recv_sem=recv_sems.at[per * a + sem],
                device_id=to, device_id_type=MESH)

        mine = [pltpu.make_async_copy(x_refs[a], slot(a, *me), local_sems.at[a]) for a in range(n_arr)]
        for cp in mine:
            cp.start()
        sent = []
        for a in range(n_arr):
            sent.append(copy(a, 0, me, sibling, src=x_refs[a]))
            sent += [copy(a, 1 + j, me, (*chip, ci), src=x_refs[a]) for j, chip in enumerate(chips)]
        for cp in sent:
            cp.start()
        for j, chip in enumerate(chips):
            for a in range(n_arr):
                copy(a, 1 + j, (*chip, ci), me).wait_recv()
                onward = copy(a, 4 + j, (*chip, ci), sibling)
                onward.start()
                sent.append(onward)
        for a in range(n_arr):
            copy(a, 0, sibling, me).wait_recv()
        for j, chip in enumerate(chips):
            for a in range(n_arr):
                copy(a, 4 + j, (*chip, 1 - ci), me).wait_recv()
        for cp in sent:
            cp.wait_send()
        for cp in mine:
            cp.wait()

    sems = pltpu.SemaphoreType.DMA((per * n_arr,))
    return pl.pallas_call(
        body, name=name, out_shape=[jax.ShapeDtypeStruct((N_DEV,) + a.shape, a.dtype) for a in arrays],
        in_specs=[ANY] * n_arr, out_specs=[ANY] * n_arr,
        scratch_shapes=[sems, sems, pltpu.SemaphoreType.DMA((n_arr,))],
    )(*arrays)


def _exchange_copies(in_refs, out_refs, n_scattered, send_sems, recv_sems, local_sems):
    n_arr = len(in_refs)
    per = N_DEV - 1
    xi, yi, ci = lax.axis_index("x"), lax.axis_index("y"), lax.axis_index("c")
    me = 4 * xi + 2 * yi + ci
    src_of = lambda a, peer: in_refs[a].at[peer] if a < n_scattered else in_refs[a]
    copies = []
    for d in range(1, N_DEV):
        px = 1 - xi if d & 4 else xi
        py = 1 - yi if d & 2 else yi
        pc = 1 - ci if d & 1 else ci
        for a in range(n_arr):
            copies.append(pltpu.make_async_remote_copy(
                src_ref=src_of(a, 4 * px + 2 * py + pc), dst_ref=out_refs[a].at[me],
                send_sem=send_sems.at[per * a + d - 1], recv_sem=recv_sems.at[per * a + d - 1],
                device_id=(px, py, pc), device_id_type=MESH))
    own = [pltpu.make_async_copy(src_of(a, me), out_refs[a].at[me], local_sems.at[a]) for a in range(n_arr)]
    return copies + own


def _exchange_shapes(scattered, shared):
    return ([jax.ShapeDtypeStruct(a.shape, a.dtype) for a in scattered]
            + [jax.ShapeDtypeStruct((N_DEV,) + a.shape, a.dtype) for a in shared])


def _exchange_sems(n_arr):
    sems = pltpu.SemaphoreType.DMA(((N_DEV - 1) * n_arr,))
    return [sems, sems, pltpu.SemaphoreType.DMA((n_arr,))]


def _exchange(scattered, shared, name):
    n_sc = len(scattered)
    n_arr = n_sc + len(shared)

    def body(*refs):
        copies = _exchange_copies(refs[:n_arr], refs[n_arr:2 * n_arr], n_sc, *refs[2 * n_arr:])
        for cp in copies:
            cp.start()
        for cp in copies:
            cp.wait()

    return pl.pallas_call(
        body, name=name, out_shape=_exchange_shapes(scattered, shared),
        in_specs=[ANY] * n_arr, out_specs=[ANY] * n_arr, scratch_shapes=_exchange_sems(n_arr),
    )(*scattered, *shared)


def _adam_math(g, w, m, v):
    m_new = ADAM_B1 * m + (1.0 - ADAM_B1) * g
    v_new = ADAM_B2 * v + (1.0 - ADAM_B2) * (g * g)
    m_hat = m_new / (1.0 - ADAM_B1 ** ADAM_STEP)
    v_hat = v_new / (1.0 - ADAM_B2 ** ADAM_STEP)
    return -ADAM_LR * (m_hat / (jnp.sqrt(v_hat) + ADAM_EPS) + ADAM_WD * w), m_new, v_new


def _slot_sum(p_ref):
    g = p_ref[0].astype(F32)
    for s in range(1, N_DEV):
        g = g + p_ref[s].astype(F32)
    return g


def _adamw_replicated(parts, ws, ms, vs, loss_parts, name):
    n = len(ws)

    def body(*refs):
        p_refs, w_refs, m_refs, v_refs = refs[:n], refs[n:2 * n], refs[2 * n:3 * n], refs[3 * n:4 * n]
        outs = refs[4 * n + 1:]
        for j in range(n):
            g = _slot_sum(p_refs[j])
            outs[4 * j][...] = g
            for o_ref, val in zip(outs[4 * j + 1:4 * j + 4], _adam_math(g, w_refs[j][...], m_refs[j][...],
                                                                        v_refs[j][...])):
                o_ref[...] = val
        outs[4 * n][...] = _slot_sum(refs[4 * n])

    whole = pl.BlockSpec(memory_space=pltpu.VMEM)
    out = pl.pallas_call(
        body, name=name, in_specs=[whole] * (4 * n + 1), out_specs=[whole] * (4 * n + 1),
        out_shape=[jax.ShapeDtypeStruct(w.shape, F32) for w in ws for _ in range(4)]
                  + [jax.ShapeDtypeStruct(loss_parts.shape[1:], F32)],
    )(*parts, *ws, *ms, *vs, loss_parts)
    return [out[4 * j:4 * j + 4] for j in range(n)], out[4 * n]


def _adamw(parts, w, m, v, name):
    rows, cols = w.shape[-2:]
    tile = PACK_ROWS if rows % PACK_ROWS == 0 else rows
    at = (0,) if w.ndim == 3 else (Ellipsis,)

    def body(p_ref, w_ref, m_ref, v_ref, g_out, d_out, m_out, v_out):
        g = _slot_sum(p_ref)
        g_out[at] = g
        d_out[at], m_out[at], v_out[at] = _adam_math(g, w_ref[at], m_ref[at], v_ref[at])

    spec = (pl.BlockSpec((1, tile, cols), lambda i: (0, i, 0)) if w.ndim == 3
            else pl.BlockSpec((tile, cols), lambda i: (i, 0)))
    return pl.pallas_call(
        body, name=name, grid=(rows // tile,),
        in_specs=[pl.BlockSpec((N_DEV, tile, cols), lambda i: (0, i, 0)), spec, spec, spec],
        out_specs=[spec] * 4, out_shape=[jax.ShapeDtypeStruct(w.shape, F32)] * 4,
        compiler_params=_cparams(("parallel",)),
    )(parts, w, m, v)


EARLY = [("meta_tokens", 1), ("rwkv_w2", 1), ("rwkv_a2", 1), ("rwkv_g2", 1)]
LATE = [("w_br_attn", 1), ("w_br_rwkv", 1), ("w_o", 0), ("w_ffn_gate", 1), ("w_ffn_up", 1), ("w_ffn_down", 0)]
REPLICATED = ["norm_mix_g", "b_in", "attn_sinks", "rwkv_mix", "rwkv_w0", "rwkv_a0", "rwkv_k_k", "rwkv_k_a",
              "rwkv_r_k", "rwkv_ln_w", "rwkv_ln_b", "norm_ffn_g", "norm_final_g"]
WEIGHTS = ["meta_tokens", "norm_mix_g", "w_in", "b_in", "attn_sinks", "rwkv_mix", "rwkv_w0", "rwkv_w2", "rwkv_a0",
           "rwkv_a2", "rwkv_g2", "rwkv_k_k", "rwkv_k_a", "rwkv_r_k", "rwkv_ln_w", "rwkv_ln_b", "w_br_attn",
           "w_br_rwkv", "w_o", "norm_ffn_g", "w_ffn_gate", "w_ffn_up", "w_ffn_down", "norm_final_g"]


def _strip(name, a):
    return a if name in ("meta_tokens", "norm_final_g") else a[0]


def _join(gathered, axis):
    if axis == 0:
        return gathered.reshape(-1, gathered.shape[2])
    return gathered.transpose(1, 0, 2).reshape(gathered.shape[1], -1)


def _split(g, axis):
    if axis == 0:
        return g.reshape(N_DEV, -1, g.shape[1])
    return g.reshape(g.shape[0], N_DEV, -1).transpose(1, 0, 2)


W_IN_LAYOUT = [(2592, 4640), (768, 2304), (2432, 2592), 256 - GATE_LORA, (2304, 2368), 128 - DECAY_LORA,
               (2368, 2432), 128 - AAA_LORA, (0, 512), (512, 768), NP - C_VA - 128]


def _w_in_padded(w, shard_width=None):
    rows = w.shape[-2]
    width = D_IN if shard_width is None else shard_width
    parts = []
    for seg in W_IN_LAYOUT:
        if isinstance(seg, int):
            parts.append(jnp.zeros((rows, seg), w.dtype))
            continue
        lo, stop = seg
        while lo < stop:
            p = lo // width
            hi = min(stop, (p + 1) * width)
            src = w if shard_width is None else w[p]
            parts.append(src[:, lo - p * width:hi - p * width])
            lo = hi
    return jnp.concatenate(parts, axis=1)


def _w_in_unpadded(wp, lo=0, stop=D_IN):
    spans, pos = [], 0
    for seg in W_IN_LAYOUT:
        if isinstance(seg, int):
            pos += seg
        else:
            spans.append((seg[0], seg[1], pos))
            pos += seg[1] - seg[0]
    parts = []
    for a, b, at in sorted(spans):
        c, d = max(a, lo), min(b, stop)
        if c < d:
            parts.append(wp[:, at + c - a:at + d - a])
    return jnp.concatenate(parts, axis=1)


def _pad_rows(a, n):
    return jnp.pad(a, ((0, n - a.shape[0]), (0, 0)))


def _device_step(x, tgt, full, gather_late=None, scatter_early=None, scatter_last=None):
    seq = x.shape[0]
    nblk = seq // BLOCK
    lp = seq + BLOCK
    nall = nblk + 1

    w_in_p = full["w_in_p"]
    b_in_p = _w_in_padded(full["b_in"][None])
    mix = full["rwkv_mix"][None]
    mix_r, mix_k, mix_v = mix[:, 0:512], mix[:, 512:1024], mix[:, 1024:1536]
    mix_dw = jnp.pad(mix[:, 1536:1600], ((0, 0), (0, 64)))
    mix_da = jnp.pad(mix[:, 1600:1664], ((0, 0), (0, 64)))
    mix_dg = jnp.pad(mix[:, 1664:1824], ((0, 0), (0, 96)))
    w2_p = _pad_rows(full["rwkv_w2"].astype(F32), 128)
    a2_p = _pad_rows(full["rwkv_a2"].astype(F32), 128)
    g2_p = _pad_rows(full["rwkv_g2"].astype(F32), 256)
    row = lambda name: full[name].reshape(1, -1)
    sinks = row("attn_sinks")
    rope_c, rope_s1, rope_s2 = _rope_tables(lp)

    hpad = jnp.concatenate([jnp.zeros((PAD_ROWS, D_MODEL), F32), full["meta_tokens"].astype(F32), x], axis=0)
    (u,) = _rows_fwd(_rms_fn, [_view(hpad)], [row("norm_mix_g")], [D_MODEL], nblk=nall, name="norm_mix",
                     out_dtype=BF16, tile=TILE_ALL_BIG)
    proj = _mm(u, w_in_p, bias=b_in_p, name="in_proj")
    (q_r,) = _rows_fwd(_rope_fwd_fn, [_view(proj, 512, C_Q // 512), _view(rope_c), _view(rope_s1), _view(rope_s2)],
                       [], [512], nblk=nall, name="rope_q", tile=TILE_ALL_BIG)
    (k_r,) = _rows_fwd(_rope_fwd_fn, [_view(proj, 128, C_KA // 128), _view(rope_c), _view(rope_s1),
                                      _view(rope_s2)], [], [128], nblk=nall, name="rope_k", tile=TILE_ALL_BIG)

    rw_prev = jnp.pad(proj[:-1, C_R:C_R + 2048], ((1, 0), (0, 0)))
    pre_rows = [_view(proj, 512, C_R // 512), _view(proj, 512, C_K // 512), _view(proj, 512, C_V // 512),
                _view(proj, 128, C_DW // 128),
                _view(proj, 128, C_DA // 128), _view(proj, 256, C_DG // 256),
                _view(rw_prev, 512, 0), _view(rw_prev, 512, 1), _view(rw_prev, 512, 2), _view(rw_prev, 128, 14),
                _view(rw_prev, 128, 15), _view(rw_prev, 256, 6)]
    pre_consts = [mix_r, mix_k, mix_v, mix_dw, mix_da, mix_dg, row("rwkv_w0"), w2_p, row("rwkv_a0"), a2_p, g2_p,
                  row("rwkv_k_k"), row("rwkv_k_a")]
    xt_block = (BLOCK // SCAN_T * VEC_ROWS, LANES)
    r_t, k_mod, v_t, gate, xt = _rows_fwd(_rwkv_pre_xt_fn, pre_rows, pre_consts, [RW_DIM] * 4 + [xt_block],
                                          nblk=nall, name="rwkv_pre")
    xt = xt.reshape(-1, VEC_ROWS, LANES)
    spread, collect = _selectors()
    y_scan, hist, o_attn, lse, *late = _wkv_fwd(xt, v_t, spread, (q_r, k_r, proj, C_VA // 128, sinks), name="wkv_fwd",
                                                shared=gather_late[0] if gather_late else ())
    if gather_late:
        full = {**full, **gather_late[1](late)}
    post_rows = [_view(y_scan, off=1), _view(r_t, off=1), _view(k_mod, off=1), _view(v_t, off=1), _view(gate, off=1)]
    post_consts = [row("rwkv_ln_w"), row("rwkv_ln_b"), row("rwkv_r_k")]
    (y_rwkv,) = _rows_fwd(_rwkv_post_fn, post_rows, post_consts, [RW_DIM], nblk=nblk, name="rwkv_post",
                          out_dtype=BF16)

    ya = _mm(o_attn, full["w_br_attn"], name="br_attn")
    yr = _mm(y_rwkv, full["w_br_rwkv"], name="br_rwkv")
    merge_rows = [_view(ya), _view(yr), _view(proj, 1024, C_G1 // 1024, 1), _view(proj, 1024, C_G2 // 1024, 1)]
    (merged,) = _rows_fwd(_merge_fn, merge_rows, [], [D_MODEL], nblk=nblk, name="merge", out_dtype=BF16)
    h1 = _mm(merged, full["w_o"], residual=x, name="out_proj")
    (f,) = _rows_fwd(_rms_fn, [_view(h1)], [row("norm_ffn_g")], [D_MODEL], nblk=nblk, name="norm_ffn",
                     out_dtype=BF16, tile=TILE_REAL)
    ff_gate, ff_up, act = _mm_swiglu(f, full["w_ffn_gate"], full["w_ffn_up"], name="ffn_gate_up")
    h2 = _mm(act, full["w_ffn_down"], residual=h1, name="ffn_down")

    grads = {}
    ones_col = jnp.ones((seq, 1), F32)
    loss_rows, dh2, grads["norm_final_g"] = _rows_bwd(
        _loss_fn, [_view(h2), _view(tgt)], [row("norm_final_g")], [_view(ones_col)], nblk=nblk, name="loss",
        diff_rows=[0], diff_consts=[0], fwd_widths=[1], tile=TILE_REAL)
    loss = jnp.sum(loss_rows)

    dact = _mm(dh2, full["w_ffn_down"], tb=True, name="d_act")
    grads["w_ffn_down"] = _mm(act, dh2, ta=True, name="dw_ffn_down")
    dgate, dup = _rows_bwd(_swiglu_fn, [_view(ff_gate), _view(ff_up)], [], [_view(dact)], nblk=nblk,
                           name="swiglu_bwd", diff_rows=[0, 1], diff_consts=[], row_dtype=BF16, tile=TILE_WIDE)
    grads["w_ffn_gate"] = _mm(f, dgate, ta=True, name="dw_ffn_gate")
    grads["w_ffn_up"] = _mm(f, dup, ta=True, name="dw_ffn_up")
    df = _mm(dgate, full["w_ffn_gate"], tb=True, name="df_gate")
    df = _mm(dup, full["w_ffn_up"], tb=True, residual=df, name="df_up")
    dh1, grads["norm_ffn_g"] = _rows_bwd(_rms_fn, [_view(h1)], [row("norm_ffn_g")], [_view(df)], nblk=nblk,
                                         name="norm_ffn_bwd", diff_rows=[0], diff_consts=[0], acc=[_view(dh2)],
                                         tile=TILE_REAL)
    dmerged = _mm(dh1, full["w_o"], tb=True, name="d_merged")
    grads["w_o"] = _mm(merged, dh1, ta=True, name="dw_o")
    dya, dyr, dg1, dg2 = _rows_bwd(_merge_fn, merge_rows, [], [_view(dmerged)], nblk=nblk, name="merge_bwd",
                                   diff_rows=[0, 1, 2, 3], diff_consts=[], row_dtype=BF16)
    grads["w_br_attn"] = _mm(o_attn, dya, ta=True, name="dw_br_attn")
    grads["w_br_rwkv"] = _mm(y_rwkv, dyr, ta=True, name="dw_br_rwkv")
    dy_attn = _mm(dya, full["w_br_attn"], tb=True, name="d_y_attn")
    dy_rwkv = _mm(dyr, full["w_br_rwkv"], tb=True, name="d_y_rwkv")

    post = _rows_bwd(_rwkv_post_fn, post_rows, post_consts, [_view(dy_rwkv)], nblk=nblk, name="rwkv_post_bwd",
                     diff_rows=[0, 1, 2, 3, 4], diff_consts=[0, 1, 2])
    dys, dr_post, dk_post, dv_post, dgate_post = post[:5]
    grads["rwkv_ln_w"], grads["rwkv_ln_b"], grads["rwkv_r_k"] = post[5:]
    dxt, dv_s, *early_parts = _wkv_bwd(xt, v_t, hist, dys, spread, collect, name="wkv_bwd",
                                       scattered=scatter_early(grads) if scatter_early else ())
    pre_cts = [_view(dxt.reshape(-1, LANES), rows=xt_block[0]), _view(dv_s)] + [
        _view(t, off=-1) for t in (dr_post, dk_post, dv_post, dgate_post)]
    pre = _rows_bwd(_rwkv_pre_fn, pre_rows, pre_consts, pre_cts, nblk=nall, name="rwkv_pre_bwd",
                    diff_rows=list(range(12)), diff_consts=list(range(13)), ct_map=_rwkv_pre_cts)
    d_cur, d_prev, d_par = pre[0:6], pre[6:12], pre[12:]
    up = lambda t: jnp.pad(t[1:], ((0, 1), (0, 0)))
    d_rw = [c + up(p) for c, p in zip(d_cur, d_prev)]
    grads["rwkv_mix"] = jnp.concatenate([d_par[0], d_par[1], d_par[2], d_par[3][:, :DECAY_LORA],
                                         d_par[4][:, :AAA_LORA], d_par[5][:, :GATE_LORA]], axis=1)
    grads["rwkv_w0"], grads["rwkv_w2"] = d_par[6], d_par[7][:DECAY_LORA]
    grads["rwkv_a0"], grads["rwkv_a2"] = d_par[8], d_par[9][:AAA_LORA]
    grads["rwkv_g2"] = d_par[10][:GATE_LORA]
    grads["rwkv_k_k"], grads["rwkv_k_a"] = d_par[11], d_par[12]

    dq_real, dk_r, dva, grads["attn_sinks"] = _attn_bwd(q_r, k_r, proj, C_VA // 128, sinks, o_attn, lse, dy_attn,
                                                        nblk=nblk, name="attn_bwd")
    dq_r = jnp.pad(dq_real, ((BLOCK, 0), (0, 0)))
    (dq,) = _rows_fwd(_rope_bwd_fn, [_view(dq_r), _view(rope_c), _view(rope_s1), _view(rope_s2)], [], [512],
                      nblk=nall, name="rope_q_bwd", out_dtype=BF16, tile=TILE_ALL_BIG)
    (dka,) = _rows_fwd(_rope_bwd_fn, [_view(dk_r), _view(rope_c), _view(rope_s1),
                                      _view(rope_s2)], [], [128], nblk=nall, name="rope_k_bwd", out_dtype=BF16,
                       tile=TILE_ALL_BIG)

    lead = lambda t: jnp.pad(t, ((BLOCK, 0), (0, 0)))
    pieces = [lead(dg1), lead(dg2), d_rw[0], d_rw[1], d_rw[2], d_rw[5], d_rw[3], d_rw[4], dq, dka, dva,
              jnp.zeros((lp, NP - C_VA - 128), BF16)]
    dproj = jnp.concatenate([p.astype(BF16) for p in pieces], axis=1)
    grads["w_in_p"] = _mm(u, dproj, ta=True, name="dw_in")
    du, db_in_p, *last_parts = _mm(dproj, w_in_p, tb=True, name="d_u", colsum_a=True,
                                   scattered=scatter_last(grads) if scatter_last else ())
    grads["b_in"] = _w_in_unpadded(db_in_p)
    dh, grads["norm_mix_g"] = _rows_bwd(_rms_fn, [_view(hpad)], [row("norm_mix_g")], [_view(du)], nblk=nall,
                                        name="norm_mix_bwd", diff_rows=[0], diff_consts=[0], acc=[_view(lead(dh1))],
                                        tile=TILE_ALL)
    grads["meta_tokens"] = dh[PAD_ROWS:BLOCK]
    return loss, dh[BLOCK:], grads, early_parts, last_parts


def kernel(x, meta_tokens, norm_mix_g, w_in, b_in, attn_sinks, rwkv_mix, rwkv_w0, rwkv_w2, rwkv_a0, rwkv_a2, rwkv_g2, rwkv_k_k, rwkv_k_a, rwkv_r_k, rwkv_ln_w, rwkv_ln_b, w_br_attn, w_br_rwkv, w_o, norm_ffn_g, w_ffn_gate, w_ffn_up, w_ffn_down, norm_final_g, loss_target, m_meta_tokens, m_norm_mix_g, m_w_in, m_b_in, m_attn_sinks, m_rwkv_mix, m_rwkv_w0, m_rwkv_w2, m_rwkv_a0, m_rwkv_a2, m_rwkv_g2, m_rwkv_k_k, m_rwkv_k_a, m_rwkv_r_k, m_rwkv_ln_w, m_rwkv_ln_b, m_w_br_attn, m_w_br_rwkv, m_w_o, m_norm_ffn_g, m_w_ffn_gate, m_w_ffn_up, m_w_ffn_down, m_norm_final_g, v_meta_tokens, v_norm_mix_g, v_w_in, v_b_in, v_attn_sinks, v_rwkv_mix, v_rwkv_w0, v_rwkv_w2, v_rwkv_a0, v_rwkv_a2, v_rwkv_g2, v_rwkv_k_k, v_rwkv_k_a, v_rwkv_r_k, v_rwkv_ln_w, v_rwkv_ln_b, v_w_br_attn, v_w_br_rwkv, v_w_o, v_norm_ffn_g, v_w_ffn_gate, v_w_ffn_up, v_w_ffn_down, v_norm_final_g):
    given = dict(locals())
    wts = {n: _strip(n, given[n]) for n in WEIGHTS}
    as_rows = lambda a: a.reshape(1, -1) if a.ndim == 1 else a
    width = wts["w_in"].shape[1]
    wire = lambda table: [wts[n].astype(BF16) for n, _ in table]

    w_in_all, *early_all = _all_gather([wts["w_in"].astype(BF16)] + wire(EARLY), name="gather_weights")
    full = {n: wts[n] for n in REPLICATED}
    full.update({n: _join(g, axis) for (n, axis), g in zip(EARLY, early_all)})
    full["w_in_p"] = _w_in_padded(w_in_all, shard_width=width)
    gather_late = (wire(LATE), lambda got: {n: _join(g, axis) for (n, axis), g in zip(LATE, got)})
    scatter_early = lambda g: [_split(g[n], axis).astype(BF16) for n, axis in LATE]
    scatter_last = lambda g: [jnp.stack([_w_in_unpadded(g["w_in_p"], p * width, (p + 1) * width)
                                         for p in range(N_DEV)]).astype(BF16)]

    loss_part, grad_x, grads, parts_late, (parts_w_in,) = _device_step(
        x[0], loss_target[0], full, gather_late, scatter_early, scatter_last)

    g_early = [_split(grads[n], axis).astype(BF16) for n, axis in EARLY]
    g_small = [grads[n].reshape(as_rows(given[n]).shape) for n in REPLICATED] + [jnp.full((8, LANES), loss_part)]
    got = _exchange(g_early, g_small, name="exchange_grads")
    parts_early, parts_small, parts_loss = got[:len(EARLY)], got[len(EARLY):-1], got[-1]
    results = [{}, {}, {}, {}]
    for (n, _), parts in zip([("w_in", 1)] + EARLY + LATE, [parts_w_in] + list(parts_early) + list(parts_late)):
        for kind, a in enumerate(_adamw(parts, given[n], given["m_" + n], given["v_" + n], name="adamw_" + n)):
            results[kind][n] = a
    small, loss = _adamw_replicated(parts_small, *[[as_rows(given[pre + n]) for n in REPLICATED]
                                                   for pre in ("", "m_", "v_")], parts_loss, name="adamw_replicated")
    for n, four in zip(REPLICATED, small):
        for kind, a in enumerate(four):
            results[kind][n] = a
    loss = loss[0, 0]
    out = [loss, grad_x[None]]
    for kind in range(4):
        out += [results[kind][n].reshape(given[n].shape) for n in WEIGHTS]
    return tuple(out)
```

```python
import jax
import jax.numpy as jnp
from jax import lax
from jax.experimental import pallas as pl
from jax.experimental.pallas import tpu as pltpu

F32 = jnp.float32
BF16 = jnp.bfloat16

N_DEV = 8
D_MODEL = 1024
N_META = 16
BLOCK = 128
PAD_ROWS = BLOCK - N_META
HEAD_DIM = 64
Q_HEADS = 8
KV_HEADS = 2
GROUP = Q_HEADS // KV_HEADS
ROPE_DIM = HEAD_DIM // 4
ROPE_HALF = ROPE_DIM // 2
ROPE_THETA = 500000.0
RW_HEADS = 8
RW_DIM = 512
DECAY_LORA = 64
AAA_LORA = 64
GATE_LORA = 160
D_FF = 2816
D_IN = 4640
RMS_EPS = 1e-6
RWKV_LN_EPS = 64e-5
NEG_INF = -1e30
SCAN_T = 16
SCAN_CHUNKS = 8
LANES = 128
PACK_ROWS = 256
TILE_ALL = 384
TILE_ALL_BIG = 1408
TILE_REAL = 512

ADAM_LR = 0.001
ADAM_B1 = 0.9
ADAM_B2 = 0.999
ADAM_EPS = 1e-08
ADAM_WD = 0.01
ADAM_STEP = 10

C_G1, C_G2 = 0, 1024
C_R, C_K, C_V, C_DG, C_DW, C_DA = 2048, 2560, 3072, 3584, 3840, 3968
C_Q, C_KA, C_VA = 4096, 4608, 4736
NP = 5120

VMEM_LIMIT = 48 * 1024 * 1024
SCAN_VMEM_LIMIT = 60 * 1024 * 1024


def _cparams(sem, vmem=VMEM_LIMIT):
    return pltpu.CompilerParams(dimension_semantics=sem, vmem_limit_bytes=vmem)


def _pick(n, cands):
    for c in cands:
        if n % c == 0:
            return c
    raise ValueError(f"no tile for {n}")


def _mm(a, b, *, ta=False, tb=False, bias=None, residual=None, name, scattered=(), colsum_a=False):
    m = a.shape[1] if ta else a.shape[0]
    k = a.shape[0] if ta else a.shape[1]
    n = b.shape[0] if tb else b.shape[1]
    assert k == (b.shape[1] if tb else b.shape[0]), (a.shape, b.shape, ta, tb)
    tm = _pick(m, (512, 1408, 256, 128) if ta else (1056, 1024, 528, 512, 384, 256, 128))
    tn = _pick(n, (1024, 512, 1408, 256, 128))
    if k <= 1024:
        tk = k
    else:
        tk = _pick(k, (1024, 1056, 528, 512) if (ta and not tb) else (1024, 1408, 512, 256, 128))
    nk = k // tk
    has_bias = bias is not None
    has_res = residual is not None
    dn = (((0 if ta else 1,), (1 if tb else 0,)), ((), ()))

    def body(*refs):
        a_ref, b_ref = refs[0], refs[1]
        pos = 2
        bias_ref = res_ref = None
        if has_bias:
            bias_ref = refs[pos]
            pos += 1
        if has_res:
            res_ref = refs[pos]
            pos += 1
        o_ref, acc_ref = refs[pos], refs[-1]
        kk = pl.program_id(2)
        if colsum_a:
            cs_ref = refs[pos + 1]

            @pl.when((pl.program_id(1) == 0) & (pl.program_id(0) == 0))
            def _():
                cs_ref[kk] = jnp.sum(a_ref[...].astype(F32), axis=0, keepdims=True)

            @pl.when((pl.program_id(1) == 0) & (pl.program_id(0) > 0))
            def _():
                cs_ref[kk] += jnp.sum(a_ref[...].astype(F32), axis=0, keepdims=True)
        part = lax.dot_general(a_ref[...].astype(BF16), b_ref[...].astype(BF16), dn, preferred_element_type=F32)

        def finish(out):
            if has_bias:
                out = out + bias_ref[...]
            if has_res:
                out = out + res_ref[...]
            o_ref[...] = out

        if nk == 1:
            finish(part)
        else:
            @pl.when(kk == 0)
            def _():
                acc_ref[...] = part

            @pl.when((kk > 0) & (kk < nk - 1))
            def _():
                acc_ref[...] += part

            @pl.when(kk == nk - 1)
            def _():
                finish(acc_ref[...] + part)

    in_specs = [
        pl.BlockSpec((tk, tm), lambda i, j, kk: (kk, i)) if ta else pl.BlockSpec((tm, tk), lambda i, j, kk: (i, kk)),
        pl.BlockSpec((tn, tk), lambda i, j, kk: (j, kk)) if tb else pl.BlockSpec((tk, tn), lambda i, j, kk: (kk, j)),
    ]
    args = [a, b]
    if has_bias:
        in_specs.append(pl.BlockSpec((1, tn), lambda i, j, kk: (0, j)))
        args.append(bias)
    if has_res:
        in_specs.append(pl.BlockSpec((tm, tn), lambda i, j, kk: (i, j)))
        args.append(residual)
    grid = (m // tm, n // tn, nk)
    n_x = len(scattered)
    assert not (colsum_a and ta)
    sums_spec = [pl.BlockSpec((nk, 1, tk), lambda i, j, kk: (0, 0, 0))] if colsum_a else []
    sums_shape = [jax.ShapeDtypeStruct((nk, 1, tk), F32)] if colsum_a else []
    serial = n_x or colsum_a
    out = pl.pallas_call(
        _with_exchange(body, len(args), 1 + len(sums_spec), 1, scattered, (), grid), name=name, grid=grid,
        in_specs=in_specs + [ANY] * n_x,
        out_specs=[pl.BlockSpec((tm, tn), lambda i, j, kk: (i, j))] + sums_spec + [ANY] * n_x,
        out_shape=[jax.ShapeDtypeStruct((m, n), F32)] + sums_shape + _exchange_shapes(scattered, ()),
        scratch_shapes=[pltpu.VMEM((tm, tn) if nk > 1 else (8, LANES), F32)] + (_exchange_sems(n_x) if n_x else []),
        compiler_params=_cparams(("arbitrary",) * 3 if serial else ("parallel", "parallel", "arbitrary")),
    )(*args, *scattered)
    out = list(out)
    if colsum_a:
        out[1] = out[1].reshape(1, k)
    return out if (n_x or colsum_a) else out[0]


def _mm_swiglu(a, w_gate, w_up, *, name):
    m, k = a.shape
    n = w_gate.shape[1]
    assert w_gate.shape == (k, n) and w_up.shape == (k, n) and k <= 1024, (a.shape, w_gate.shape, w_up.shape)
    tm = _pick(m, (512, 256, 128))
    tn = _pick(n, (1024, 512, 1408, 256, 128))

    def body(a_ref, gate_w_ref, up_w_ref, gate_ref, up_ref, act_ref):
        lhs = a_ref[...].astype(BF16)
        gate = jnp.dot(lhs, gate_w_ref[...].astype(BF16), preferred_element_type=F32)
        up = jnp.dot(lhs, up_w_ref[...].astype(BF16), preferred_element_type=F32)
        gate_ref[...] = gate
        up_ref[...] = up
        (act,) = _swiglu_fn(None, gate, up)
        act_ref[...] = act.astype(BF16)

    w_spec = pl.BlockSpec((k, tn), lambda i, j: (0, j))
    o_spec = pl.BlockSpec((tm, tn), lambda i, j: (i, j))
    return pl.pallas_call(
        body, name=name, grid=(m // tm, n // tn),
        in_specs=[pl.BlockSpec((tm, k), lambda i, j: (i, 0)), w_spec, w_spec],
        out_specs=[o_spec, o_spec, o_spec],
        out_shape=[jax.ShapeDtypeStruct((m, n), F32), jax.ShapeDtypeStruct((m, n), F32),
                   jax.ShapeDtypeStruct((m, n), BF16)],
        compiler_params=_cparams(("parallel", "parallel")),
    )(a, w_gate, w_up)


def _mm_swiglu_bwd(d_out, w_down, gate, up, *, name):
    m, k = d_out.shape
    n = w_down.shape[0]
    assert w_down.shape == (n, k) and gate.shape == (m, n) and up.shape == (m, n) and k <= 1024
    tm = _pick(m, (512, 256, 128))
    tn = _pick(n, (1024, 512, 1408, 256, 128))

    def body(d_ref, w_ref, gate_ref, up_ref, dgate_ref, dup_ref):
        d_act = lax.dot_general(d_ref[...].astype(BF16), w_ref[...].astype(BF16), (((1,), (1,)), ((), ())),
                                preferred_element_type=F32)
        _, pull = jax.vjp(lambda g, u: _swiglu_fn(None, g, u), gate_ref[...], up_ref[...])
        d_gate, d_up = pull((d_act,))
        dgate_ref[...] = d_gate.astype(BF16)
        dup_ref[...] = d_up.astype(BF16)

    tile = pl.BlockSpec((tm, tn), lambda j, i: (i, j))
    return pl.pallas_call(
        body, name=name, grid=(n // tn, m // tm),
        in_specs=[pl.BlockSpec((tm, k), lambda j, i: (i, 0)), pl.BlockSpec((tn, k), lambda j, i: (j, 0)), tile, tile],
        out_specs=[tile, tile],
        out_shape=[jax.ShapeDtypeStruct((m, n), BF16), jax.ShapeDtypeStruct((m, n), BF16)],
        compiler_params=_cparams(("parallel", "parallel")),
    )(d_out, w_down, gate, up)


def _view(arr, width=None, col=0, off=0, rows=BLOCK):
    return (arr, arr.shape[1] if width is None else width, col, off, rows)


def _row_spec(view):
    _, width, col, off, rows = view
    if off < 0:
        return pl.BlockSpec((rows, width), lambda i, col=col, off=off: (jnp.maximum(i + off, 0), col))
    return pl.BlockSpec((rows, width), lambda i, col=col, off=off: (i + off, col))


def _const_spec(arr):
    return pl.BlockSpec(arr.shape, lambda i: (0,) * arr.ndim)


def _retile(views, tile):
    assert all(v[3] == 0 and v[4] == BLOCK for v in views)
    return [v[:4] + (tile,) for v in views]


def _rows_fwd(fn, rows, consts, out_widths, *, nblk, name, out_dtype=F32, tile=BLOCK):
    nr, nc = len(rows), len(consts)
    tile = tile if (nblk * BLOCK) % tile == 0 else BLOCK
    if tile != BLOCK:
        rows, nblk = _retile(rows, tile), nblk * BLOCK // tile
    out_blocks = [(tile, w) if isinstance(w, int) else w for w in out_widths]

    def body(*refs):
        i = pl.program_id(0)
        vals = [r[...] for r in refs[:nr + nc]]
        outs = fn(i, *vals)
        for o_ref, o in zip(refs[nr + nc:], outs):
            o_ref[...] = o.astype(o_ref.dtype)

    return pl.pallas_call(
        body, name=name, grid=(nblk,),
        in_specs=[_row_spec(v) for v in rows] + [_const_spec(c) for c in consts],
        out_specs=[pl.BlockSpec(b, lambda i: (i, 0)) for b in out_blocks],
        out_shape=[jax.ShapeDtypeStruct((nblk * r, w), out_dtype) for r, w in out_blocks],
        compiler_params=_cparams(("parallel",)),
    )(*[v[0] for v in rows], *consts)


def _rows_bwd(fn, rows, consts, cts, *, nblk, name, diff_rows, diff_consts, acc=None, fwd_widths=(), row_dtype=F32,
              ct_map=None, tile=BLOCK):
    nr, nc = len(rows), len(consts)
    acc = acc or [None] * len(diff_rows)
    tile = tile if (nblk * BLOCK) % tile == 0 else BLOCK
    if tile != BLOCK:
        rows, nblk = _retile(rows, tile), nblk * BLOCK // tile
        cts = [c if c is None else _retile([c], tile)[0] for c in cts]
        acc = [a if a is None else _retile([a], tile)[0] for a in acc]
    ct_views = [c for c in cts if c is not None]
    acc_views = [a for a in acc if a is not None]
    n_in = nr + nc + len(ct_views) + len(acc_views)
    n_fwd = len(fwd_widths)

    def body(*refs):
        i = pl.program_id(0)
        row_vals = [r[...] for r in refs[:nr]]
        const_vals = [r[...] for r in refs[nr:nr + nc]]
        ct_vals = [r[...] for r in refs[nr + nc:nr + nc + len(ct_views)]]
        acc_vals = [r[...] for r in refs[nr + nc + len(ct_views):n_in]]
        out_refs = refs[n_in:]

        def f(*dargs):
            rv = list(row_vals)
            cv = list(const_vals)
            for pos, idx in enumerate(diff_rows):
                rv[idx] = dargs[pos]
            for pos, idx in enumerate(diff_consts):
                cv[idx] = dargs[len(diff_rows) + pos]
            return tuple(fn(i, *rv, *cv))

        primals = [row_vals[idx] for idx in diff_rows] + [const_vals[idx] for idx in diff_consts]
        outs, pull = jax.vjp(f, *primals)
        full_ct, ci = [], 0
        if ct_map is not None:
            full_ct = ct_map(i, *ct_vals)
        else:
            for o, c in zip(outs, cts):
                if c is None:
                    full_ct.append(jnp.zeros_like(o))
                else:
                    full_ct.append(ct_vals[ci])
                    ci += 1
        grads = pull(tuple(full_ct))
        for o_ref, o in zip(out_refs[:n_fwd], outs):
            o_ref[...] = o
        ai = 0
        for pos in range(len(diff_rows)):
            g = grads[pos]
            if acc[pos] is not None:
                g = g + acc_vals[ai]
                ai += 1
            out_refs[n_fwd + pos][...] = g.astype(row_dtype)
        for pos in range(len(diff_consts)):
            g = grads[len(diff_rows) + pos]
            o_ref = out_refs[n_fwd + len(diff_rows) + pos]

            @pl.when(i == 0)
            def _(o_ref=o_ref, g=g):
                o_ref[...] = g

            @pl.when(i > 0)
            def _(o_ref=o_ref, g=g):
                o_ref[...] += g

    out_specs = [pl.BlockSpec((tile, w), lambda i: (i, 0)) for w in fwd_widths]
    out_shape = [jax.ShapeDtypeStruct((nblk * tile, w), F32) for w in fwd_widths]
    for idx in diff_rows:
        out_specs.append(pl.BlockSpec((tile, rows[idx][1]), lambda i: (i, 0)))
        out_shape.append(jax.ShapeDtypeStruct((nblk * tile, rows[idx][1]), row_dtype))
    for idx in diff_consts:
        out_specs.append(_const_spec(consts[idx]))
        out_shape.append(jax.ShapeDtypeStruct(consts[idx].shape, F32))
    return pl.pallas_call(
        body, name=name, grid=(nblk,),
        in_specs=([_row_spec(v) for v in rows] + [_const_spec(c) for c in consts]
                  + [_row_spec(v) for v in ct_views] + [_row_spec(v) for v in acc_views]),
        out_specs=out_specs, out_shape=out_shape,
        compiler_params=_cparams(("arbitrary",)),
    )(*[v[0] for v in rows], *consts, *[v[0] for v in ct_views], *[v[0] for v in acc_views])


def _rms_fn(i, x, g):
    return (x * lax.rsqrt(jnp.mean(x * x, axis=-1, keepdims=True) + RMS_EPS) * g,)


def _sigmoid(x):
    return 1.0 / (1.0 + jnp.exp(-x))


def _softplus(x):
    return jnp.maximum(x, 0.0) + jnp.log(1.0 + jnp.exp(-jnp.abs(x)))


def _split2(x):
    hi = x.astype(BF16)
    lo = (x - hi.astype(F32)).astype(BF16)
    return jnp.concatenate([hi, lo], axis=1)


@jax.custom_vjp
def _head_sum(x):
    r = lax.broadcasted_iota(jnp.int32, (2 * RW_DIM, RW_DIM), 0) % RW_DIM // HEAD_DIM
    c = lax.broadcasted_iota(jnp.int32, (2 * RW_DIM, RW_DIM), 1) // HEAD_DIM
    return jnp.dot(_split2(x), (r == c).astype(BF16), preferred_element_type=F32)


_head_sum.defvjp(lambda x: (_head_sum(x), None), lambda _, ct: (_head_sum(ct),))


@jax.custom_vjp
def _dot_bf16(x, w):
    return jnp.dot(x.astype(BF16), w.astype(BF16), preferred_element_type=F32)


def _dot_bf16_bwd(res, ct):
    x, w = res
    ct = ct.astype(BF16)
    dx = lax.dot_general(ct, w.astype(BF16), (((1,), (1,)), ((), ())), preferred_element_type=F32)
    dw = lax.dot_general(x.astype(BF16), ct, (((0,), (0,)), ((), ())), preferred_element_type=F32)
    return dx, dw


_dot_bf16.defvjp(lambda x, w: (_dot_bf16(x, w), (x, w)), _dot_bf16_bwd)


def _rwkv_pre_fn(i, r, k, v, dw, da, dg, r_p, k_p, v_p, dw_p, da_p, dg_p,
                 mix_r, mix_k, mix_v, mix_dw, mix_da, mix_dg, w0, w2, a0, a2, g2, k_k, k_a):
    row = i * BLOCK + lax.broadcasted_iota(jnp.int32, (BLOCK, 1), 0)
    live = row >= PAD_ROWS
    live_prev = row >= PAD_ROWS + 1

    def shift(cur, prev, mix):
        cur = jnp.where(live, cur, 0.0)
        prev = jnp.where(live_prev, prev, 0.0)
        return cur + (prev - cur) * mix

    r = shift(r, r_p, mix_r)
    k = shift(k, k_p, mix_k)
    v = shift(v, v_p, mix_v)
    dw = shift(dw, dw_p, mix_dw)
    da = shift(da, da_p, mix_da)
    dg = shift(dg, dg_p, mix_dg)
    wlog = -_softplus(-(w0 + _dot_bf16(jnp.tanh(dw), w2))) - 0.5
    decay = jnp.exp(-jnp.exp(wlog))
    a = _sigmoid(a0 + _dot_bf16(da, a2))
    g = _dot_bf16(_sigmoid(dg), g2)
    kk = k * k_k
    norm_sq = jnp.where(live, _head_sum(kk * kk), 1.0)
    kk = kk / jnp.maximum(jnp.sqrt(norm_sq), 1e-12)
    k_mod = k * (1.0 + (a - 1.0) * k_a)
    return r, decay, k_mod, v, -kk, kk * a, g


def _rwkv_pre_xt_fn(i, *args):
    r, decay, k_mod, v, a_neg, b, g = _rwkv_pre_fn(i, *args)
    t = SCAN_T
    xt = jnp.concatenate([_rows_to_xt(x[c * t:(c + 1) * t]) for c in range(BLOCK // t)
                          for x in (a_neg, decay, b, k_mod, r)], axis=0)
    return r, k_mod, v, g, xt


def _rwkv_pre_cts(i, dxt, dv_s, dr_p, dk_p, dv_p, dg_p):
    t = SCAN_T
    d_a, d_w, d_b, d_k, d_r = [
        jnp.concatenate([_xt_to_rows(dxt[c * VEC_ROWS + n * HEAD_DIM:c * VEC_ROWS + (n + 1) * HEAD_DIM])
                         for c in range(BLOCK // t)], axis=0) for n in range(N_VEC)]
    dr_p, dk_p, dv_p, dg_p = [jnp.where(i > 0, x, 0.0) for x in (dr_p, dk_p, dv_p, dg_p)]
    return d_r + dr_p, d_w, d_k + dk_p, dv_s + dv_p, d_a, d_b, dg_p


def _rwkv_post_fn(i, ys, r, k_mod, v, g, ln_w, ln_b, r_k):
    mean = _head_sum(ys) * (1.0 / HEAD_DIM)
    d = ys - mean
    var = _head_sum(d * d) * (1.0 / HEAD_DIM)
    yn = d * lax.rsqrt(var + RWKV_LN_EPS) * ln_w + ln_b
    bonus = _head_sum(r * k_mod * r_k) * v
    return ((yn + bonus) * g,)


def _merge_fn(i, ya, yr, g1, g2):
    return (_sigmoid(g1) * ya + _sigmoid(g2) * yr,)


def _swiglu_fn(i, gate, up):
    return (gate * _sigmoid(gate) * up,)


def _loss_fn(i, h, tgt, g):
    y = h * lax.rsqrt(jnp.mean(h * h, axis=-1, keepdims=True) + RMS_EPS) * g
    err = y - tgt
    return (0.5 * jnp.mean(err * err, axis=-1, keepdims=True),)


def _rope_tables(lp):
    pos = (jnp.arange(lp, dtype=jnp.int32) - PAD_ROWS).astype(F32)
    inv_freq = jnp.power(jnp.float32(ROPE_THETA), -jnp.arange(ROPE_HALF, dtype=F32) * (2.0 / ROPE_DIM))
    ang = pos[:, None] * inv_freq[None, :]
    cos, sin = jnp.cos(ang), jnp.sin(ang)
    one = jnp.ones((lp, HEAD_DIM - ROPE_DIM), F32)
    zero_h = jnp.zeros((lp, ROPE_HALF), F32)
    zero_r = jnp.zeros((lp, HEAD_DIM - ROPE_DIM), F32)
    c = jnp.concatenate([cos, cos, one], axis=1)
    s1 = jnp.concatenate([-sin, zero_h, zero_r], axis=1)
    s2 = jnp.concatenate([zero_h, sin, zero_r], axis=1)
    return tuple(jnp.tile(t, (1, LANES // HEAD_DIM)) for t in (c, s1, s2))


def _rope_fwd_fn(i, x, c, s1, s2):
    n = x.shape[1]
    c, s1, s2 = [jnp.tile(t, (1, n // LANES)) for t in (c, s1, s2)]
    return (x * c + pltpu.roll(x, n - ROPE_HALF, 1) * s1 + pltpu.roll(x, ROPE_HALF, 1) * s2,)


def _rope_bwd_fn(i, dy, c, s1, s2):
    n = dy.shape[1]
    c, s1, s2 = [jnp.tile(t, (1, n // LANES)) for t in (c, s1, s2)]
    return (dy * c + pltpu.roll(dy * s1, ROPE_HALF, 1) + pltpu.roll(dy * s2, n - ROPE_HALF, 1),)


def _attn_mask(i):
    r = lax.broadcasted_iota(jnp.int32, (BLOCK, 3 * BLOCK), 0)
    c = lax.broadcasted_iota(jnp.int32, (BLOCK, 3 * BLOCK), 1)
    meta = (c < BLOCK) & (c >= PAD_ROWS)
    prev = (c >= BLOCK) & (c < 2 * BLOCK) & ((c - BLOCK) > r) & (i >= 1)
    cur = (c >= 2 * BLOCK) & ((c - 2 * BLOCK) <= r)
    return meta | prev | cur


def _attn_rows(ref, g):
    return ref[:, g * HEAD_DIM:(g + 1) * HEAD_DIM]


def _attn_group(i, g, q_all, k_refs, v_refs, s_ref):
    heads = range(g * GROUP, (g + 1) * GROUP)
    kcat = jnp.concatenate([_attn_rows(r, g) for r in k_refs], axis=0).astype(BF16)
    vcat = jnp.concatenate([_attn_rows(r, g) for r in v_refs], axis=0).astype(BF16)
    qg = jnp.concatenate([q_all[:, h * HEAD_DIM:(h + 1) * HEAD_DIM] for h in heads], axis=0).astype(BF16)
    sink = jnp.concatenate([jnp.broadcast_to(s_ref[0:1, h:h + 1], (BLOCK, 1)) for h in heads], axis=0)
    s = lax.dot_general(qg, kcat, (((1,), (1,)), ((), ())), preferred_element_type=F32) * (HEAD_DIM ** -0.5)
    valid = jnp.concatenate([_attn_mask(i)] * GROUP, axis=0)
    return heads, qg, kcat, vcat, sink, jnp.where(valid, s, NEG_INF)


ATTN_SUB = 2


def _attn_specs(v_col):
    q = [pl.BlockSpec((BLOCK, Q_HEADS * HEAD_DIM), lambda i, n=n: (ATTN_SUB * i + 1 + n, 0)) for n in range(ATTN_SUB)]
    blk = lambda col: ([pl.BlockSpec((BLOCK, 2 * HEAD_DIM), lambda i: (0, col))]
                       + [pl.BlockSpec((BLOCK, 2 * HEAD_DIM), lambda i, n=n: (ATTN_SUB * i + n, col))
                          for n in range(ATTN_SUB + 1)])
    return q + blk(0) + blk(v_col) + [pl.BlockSpec((1, Q_HEADS), lambda i: (0, 0))]


def _attn_split(refs):
    n = ATTN_SUB
    q_refs, k_refs, v_refs = refs[:n], refs[n:2 * n + 2], refs[2 * n + 2:3 * n + 4]
    return q_refs, k_refs, v_refs, refs[3 * n + 4], refs[3 * n + 5:]


def _attn_fwd_block(blk, q_ref, keys, vals, s_ref, o_ref, lse_ref):
    q_all = q_ref[...]
    for g in range(KV_HEADS):
        heads, _, _, vcat, sink, s = _attn_group(blk, g, q_all, keys, vals, s_ref)
        m = jnp.maximum(jnp.max(s, axis=-1, keepdims=True), sink)
        p = jnp.exp(s - m)
        den = jnp.sum(p, axis=-1, keepdims=True) + jnp.exp(sink - m)
        o = jnp.dot(p.astype(BF16), vcat, preferred_element_type=F32) / den
        lse = m + jnp.log(den)
        for n, h in enumerate(heads):
            o_ref[:, h * HEAD_DIM:(h + 1) * HEAD_DIM] = o[n * BLOCK:(n + 1) * BLOCK]
            lse_ref[:, h:h + 1] = lse[n * BLOCK:(n + 1) * BLOCK]


def _attn_bwd(q, k, v, v_col, sinks, o, lse, do, *, nblk, name):
    lp = k.shape[0]
    rows = ATTN_SUB * BLOCK

    def body(*refs):
        q_refs, k_refs, v_refs, s_ref, (o_ref, lse_ref, do_ref, dq_ref, dk_ref, dv_ref, ds_ref) = _attn_split(refs)
        i = pl.program_id(0)

        @pl.when(i == 0)
        def _():
            dk_ref[...] = jnp.zeros_like(dk_ref)
            dv_ref[...] = jnp.zeros_like(dv_ref)
            ds_ref[...] = jnp.zeros_like(ds_ref)

        lane = lax.broadcasted_iota(jnp.int32, (1, Q_HEADS), 1)
        for sub in range(ATTN_SUB):
            at = slice(sub * BLOCK, (sub + 1) * BLOCK)
            blk = ATTN_SUB * i + sub
            keys = (k_refs[0], k_refs[1 + sub], k_refs[2 + sub])
            vals = (v_refs[0], v_refs[1 + sub], v_refs[2 + sub])
            prev_rows = pl.ds(pl.multiple_of(blk * BLOCK, BLOCK), BLOCK)
            cur_rows = pl.ds(pl.multiple_of((blk + 1) * BLOCK, BLOCK), BLOCK)
            q_all, o_all, do_all, lse_all = q_refs[sub][...], o_ref[at, :], do_ref[at, :], lse_ref[at, :]
            for g in range(KV_HEADS):
                heads, qg, kcat, vcat, sink, s = _attn_group(blk, g, q_all, keys, vals, s_ref)
                stack = lambda x: jnp.concatenate([x[:, h * HEAD_DIM:(h + 1) * HEAD_DIM] for h in heads], axis=0)
                lse_g = jnp.concatenate([lse_all[:, h:h + 1] for h in heads], axis=0)
                do_g = stack(do_all)
                p = jnp.exp(s - lse_g)
                delta = jnp.sum(do_g * stack(o_all), axis=-1, keepdims=True)
                dp = lax.dot_general(do_g.astype(BF16), vcat, (((1,), (1,)), ((), ())), preferred_element_type=F32)
                dsc = (p * (dp - delta) * (HEAD_DIM ** -0.5)).astype(BF16)
                dq = jnp.dot(dsc, kcat, preferred_element_type=F32)
                dk_all = lax.dot_general(dsc, qg, (((0,), (0,)), ((), ())), preferred_element_type=F32)
                dv_all = lax.dot_general(p.astype(BF16), do_g.astype(BF16), (((0,), (0,)), ((), ())),
                                         preferred_element_type=F32)
                cols = slice(g * HEAD_DIM, (g + 1) * HEAD_DIM)
                for ref, full in ((dk_ref, dk_all), (dv_ref, dv_all)):
                    ref[0:BLOCK, cols] += full[0:BLOCK]
                    ref[prev_rows, cols] += full[BLOCK:2 * BLOCK]
                    ref[cur_rows, cols] += full[2 * BLOCK:]
                sink_part = jnp.exp(sink - lse_g) * delta
                for n, h in enumerate(heads):
                    dq_ref[at, h * HEAD_DIM:(h + 1) * HEAD_DIM] = dq[n * BLOCK:(n + 1) * BLOCK]
                    dsink = -jnp.sum(sink_part[n * BLOCK:(n + 1) * BLOCK], axis=0, keepdims=True)
                    ds_ref[...] += jnp.where(lane == h, dsink, 0.0)

    qspec = pl.BlockSpec((rows, Q_HEADS * HEAD_DIM), lambda i: (i, 0))
    whole = pl.BlockSpec((lp, 2 * HEAD_DIM), lambda i: (0, 0))
    return pl.pallas_call(
        body, name=name, grid=(nblk // ATTN_SUB,),
        in_specs=_attn_specs(v_col) + [qspec, pl.BlockSpec((rows, Q_HEADS), lambda i: (i, 0)), qspec],
        out_specs=[qspec, whole, whole, pl.BlockSpec((1, Q_HEADS), lambda i: (0, 0))],
        out_shape=[jax.ShapeDtypeStruct((nblk * BLOCK, Q_HEADS * HEAD_DIM), F32),
                   jax.ShapeDtypeStruct((lp, 2 * HEAD_DIM), F32), jax.ShapeDtypeStruct((lp, 2 * HEAD_DIM), F32),
                   jax.ShapeDtypeStruct((1, Q_HEADS), F32)],
        compiler_params=_cparams(("arbitrary",)),
    )(*[q] * ATTN_SUB, *[k] * (ATTN_SUB + 2), *[v] * (ATTN_SUB + 2), sinks, o, lse, do)


N_VEC = 5
VEC_ROWS = N_VEC * HEAD_DIM


def _selectors():
    t = SCAN_T
    shape = (t, 2 * LANES, RW_DIM)
    step, src, dst = [lax.broadcasted_iota(jnp.int32, shape, d) for d in range(3)]
    src = src % LANES
    spread = ((src // t == dst // HEAD_DIM) & (src % t == step)).astype(BF16)
    shape = (t, RW_DIM, LANES)
    step, src, dst = [lax.broadcasted_iota(jnp.int32, shape, d) for d in range(3)]
    collect = ((src // HEAD_DIM == dst // t) & (dst % t == step)).astype(BF16)
    return spread, collect


def _rows_to_xt(x):
    low = lax.broadcasted_iota(jnp.int32, (SCAN_T, LANES), 1) < HEAD_DIM
    pieces = []
    for m in range(RW_HEADS // 2):
        pair = x[:, m * LANES:(m + 1) * LANES]
        pieces += [jnp.where(low, pair, 0.0), jnp.where(low, pltpu.roll(pair, HEAD_DIM, 1), 0.0)]
    return jnp.concatenate(pieces, axis=0).T[:HEAD_DIM]


def _xt_to_rows(a):
    t = SCAN_T
    a_t = jnp.concatenate([a, jnp.zeros_like(a)], axis=0).T
    pairs = [a_t[2 * m * t:(2 * m + 1) * t] + pltpu.roll(a_t[(2 * m + 1) * t:(2 * m + 2) * t], HEAD_DIM, 1)
             for m in range(RW_HEADS // 2)]
    return jnp.concatenate(pairs, axis=1)


def _with_exchange(compute, n_in, n_out, n_scratch, scattered, shared, grid):
    n_sc = len(scattered)
    n_x = n_sc + len(shared)
    if n_x == 0:
        return compute

    def body(*refs):
        ins, x_in = refs[:n_in], refs[n_in:n_in + n_x]
        outs, x_out = refs[n_in + n_x:n_in + n_x + n_out], refs[n_in + n_x + n_out:n_in + 2 * n_x + n_out]
        scratch = refs[n_in + 2 * n_x + n_out:n_in + 2 * n_x + n_out + n_scratch]
        sems = refs[n_in + 2 * n_x + n_out + n_scratch:]

        first = last = True
        for d, size in enumerate(grid):
            first = first & (pl.program_id(d) == 0)
            last = last & (pl.program_id(d) == size - 1)

        @pl.when(first)
        def _():
            for cp in _exchange_copies(x_in, x_out, n_sc, *sems):
                cp.start()

        compute(*ins, *outs, *scratch)

        @pl.when(last)
        def _():
            for cp in _exchange_copies(x_in, x_out, n_sc, *sems):
                cp.wait()

    return body


def _wkv_fwd(xt, v, spread, attn, name, shared=()):
    t_steps = SCAN_T
    nch = xt.shape[0]
    per = SCAN_CHUNKS
    grid = (nch // per,)
    rows = per * t_steps
    assert rows == BLOCK
    n_x = len(shared)
    q, k, v_arr, v_col, sinks = attn

    def compute(xt_ref, v_ref, sel_ref, q_ref, km_ref, kp_ref, kc_ref, vm_ref, vp_ref, vc_ref, s_ref,
                y_ref, hist_ref, o_ref, lse_ref, st_ref):
        @pl.when(pl.program_id(0) == 0)
        def _():
            st_ref[...] = jnp.zeros_like(st_ref)

        _attn_fwd_block(pl.program_id(0) - 1, q_ref, (km_ref, kp_ref, kc_ref), (vm_ref, vp_ref, vc_ref), s_ref,
                        o_ref, lse_ref)
        st = st_ref[...]
        for c in range(per):
            x2 = _split2(xt_ref[c])
            for j in range(t_steps):
                row = c * t_steps + j
                cols = jnp.dot(x2, sel_ref[j], preferred_element_type=F32)
                a_c, w_c, b_c, k_c, r_c = [cols[n * HEAD_DIM:(n + 1) * HEAD_DIM] for n in range(N_VEC)]
                hist_ref[row] = st
                sa = jnp.sum(st * a_c, axis=0, keepdims=True)
                st = st * w_c + b_c * sa + k_c * v_ref[row:row + 1, :]
                y_ref[row:row + 1, :] = jnp.sum(st * r_c, axis=0, keepdims=True)
        st_ref[...] = st

    before = lambda c: jnp.maximum(c - 1, 0)
    kv = lambda col: [pl.BlockSpec((BLOCK, 2 * HEAD_DIM), lambda c: (0, col)),
                      pl.BlockSpec((BLOCK, 2 * HEAD_DIM), lambda c: (before(c), col)),
                      pl.BlockSpec((BLOCK, 2 * HEAD_DIM), lambda c: (c, col))]
    seq = (nch * t_steps) - BLOCK
    return pl.pallas_call(
        _with_exchange(compute, 11, 4, 1, (), shared, grid), name=name, grid=grid,
        in_specs=[pl.BlockSpec((per, VEC_ROWS, LANES), lambda c: (c, 0, 0)),
                  pl.BlockSpec((rows, RW_DIM), lambda c: (c, 0)),
                  pl.BlockSpec(spread.shape, lambda c: (0, 0, 0)),
                  pl.BlockSpec((BLOCK, Q_HEADS * HEAD_DIM), lambda c: (c, 0))] + kv(0) + kv(v_col)
                 + [pl.BlockSpec((1, Q_HEADS), lambda c: (0, 0))] + [ANY] * n_x,
        out_specs=[pl.BlockSpec((rows, RW_DIM), lambda c: (c, 0)),
                   pl.BlockSpec((rows, HEAD_DIM, RW_DIM), lambda c: (c, 0, 0)),
                   pl.BlockSpec((BLOCK, Q_HEADS * HEAD_DIM), lambda c: (before(c), 0)),
                   pl.BlockSpec((BLOCK, Q_HEADS), lambda c: (before(c), 0))] + [ANY] * n_x,
        out_shape=[jax.ShapeDtypeStruct((nch * t_steps, RW_DIM), F32),
                   jax.ShapeDtypeStruct((nch * t_steps, HEAD_DIM, RW_DIM), F32),
                   jax.ShapeDtypeStruct((seq, Q_HEADS * HEAD_DIM), F32),
                   jax.ShapeDtypeStruct((seq, Q_HEADS), F32)] + _exchange_shapes((), shared),
        scratch_shapes=[pltpu.VMEM((HEAD_DIM, RW_DIM), F32)] + (_exchange_sems(n_x) if n_x else []),
        compiler_params=_cparams(("arbitrary",), SCAN_VMEM_LIMIT),
    )(xt, v, spread, q, k, k, k, v_arr, v_arr, v_arr, sinks, *shared)


def _wkv_bwd(xt, v, hist, dy, spread, collect, name, scattered=()):
    t_steps = SCAN_T
    nch = xt.shape[0]
    n_x = len(scattered)
    per = SCAN_CHUNKS
    nsteps = nch // per
    grid = (nsteps,)
    rows = per * t_steps
    lead = BLOCK // rows

    def compute(xt_ref, v_ref, hist_ref, dy_ref, sel_ref, col_ref, dxt_ref, dv_ref, g_ref):
        @pl.when(pl.program_id(0) == 0)
        def _():
            g_ref[...] = jnp.zeros_like(g_ref)

        has_dy = nsteps - 1 - pl.program_id(0) >= lead
        gst = g_ref[...]
        nxt = None
        for c in reversed(range(per)):
            x2 = _split2(xt_ref[c])
            acc = jnp.zeros((VEC_ROWS, LANES), F32)
            for j in reversed(range(t_steps)):
                row = c * t_steps + j
                cols = jnp.dot(x2, sel_ref[j], preferred_element_type=F32)
                a_c, w_c, b_c, k_c, r_c = [cols[n * HEAD_DIM:(n + 1) * HEAD_DIM] for n in range(N_VEC)]
                prev = hist_ref[row]
                v_row = v_ref[row:row + 1, :]
                dy_row = jnp.where(has_dy, dy_ref[row:row + 1, :], 0.0)
                sa = jnp.sum(prev * a_c, axis=0, keepdims=True)
                if nxt is None:
                    nxt = prev * w_c + b_c * sa + k_c * v_row
                gst = gst + r_c * dy_row
                dv_ref[row:row + 1, :] = jnp.sum(gst * k_c, axis=0, keepdims=True)
                dsa = jnp.sum(gst * b_c, axis=0, keepdims=True)
                prods = jnp.concatenate([p.astype(BF16) for p in
                                         (prev * dsa, gst * prev, gst * sa, gst * v_row, nxt * dy_row)], axis=0)
                acc = acc + jnp.dot(prods, col_ref[j], preferred_element_type=F32)
                gst = gst * w_c + a_c * dsa
                nxt = prev
            dxt_ref[c] = acc
        g_ref[...] = gst

    rev3 = lambda c: (nsteps - 1 - c, 0, 0)
    rev2 = lambda c: (nsteps - 1 - c, 0)
    rowspec = pl.BlockSpec((rows, RW_DIM), rev2)
    return pl.pallas_call(
        _with_exchange(compute, 6, 2, 1, scattered, (), grid), name=name, grid=grid,
        in_specs=[pl.BlockSpec((per, VEC_ROWS, LANES), rev3), rowspec,
                  pl.BlockSpec((rows, HEAD_DIM, RW_DIM), rev3),
                  pl.BlockSpec((rows, RW_DIM), lambda c: (jnp.maximum(nsteps - 1 - c - lead, 0), 0)),
                  pl.BlockSpec(spread.shape, lambda c: (0, 0, 0)),
                  pl.BlockSpec(collect.shape, lambda c: (0, 0, 0))] + [ANY] * n_x,
        out_specs=[pl.BlockSpec((per, VEC_ROWS, LANES), rev3), rowspec] + [ANY] * n_x,
        out_shape=[jax.ShapeDtypeStruct((nch, VEC_ROWS, LANES), F32),
                   jax.ShapeDtypeStruct((nch * t_steps, RW_DIM), F32)] + _exchange_shapes(scattered, ()),
        scratch_shapes=[pltpu.VMEM((HEAD_DIM, RW_DIM), F32)] + (_exchange_sems(n_x) if n_x else []),
        compiler_params=_cparams(("arbitrary",), SCAN_VMEM_LIMIT),
    )(xt, v, hist, dy, spread, collect, *scattered)


MESH = pl.DeviceIdType.MESH
ANY = pl.BlockSpec(memory_space=pltpu.HBM)


def _all_gather(arrays, name):
    n_arr = len(arrays)
    per = N_DEV - 1

    def body(*refs):
        x_refs, out_refs = refs[:n_arr], refs[n_arr:2 * n_arr]
        send_sems, recv_sems, local_sems = refs[2 * n_arr:]
        xi, yi, ci = lax.axis_index("x"), lax.axis_index("y"), lax.axis_index("c")
        me, sibling = (xi, yi, ci), (xi, yi, 1 - ci)
        chips = [(1 - xi, yi), (xi, 1 - yi), (1 - xi, 1 - yi)]

        def slot(a, px, py, pc):
            return out_refs[a].at[4 * px + 2 * py + pc]

        def copy(a, sem, block, to, src=None):
            return pltpu.make_async_remote_copy(
                src_ref=slot(a, *block) if src is None else src, dst_ref=slot(a, *block),
                send_sem=send_sems.at[per * a + sem], recv_sem=recv_sems.at[per * a + sem],
                device_id=to, device_id_type=MESH)

        mine = [pltpu.make_async_copy(x_refs[a], slot(a, *me), local_sems.at[a]) for a in range(n_arr)]
        for cp in mine:
            cp.start()
        sent = []
        for a in range(n_arr):
            sent.append(copy(a, 0, me, sibling, src=x_refs[a]))
            sent += [copy(a, 1 + j, me, (*chip, ci), src=x_refs[a]) for j, chip in enumerate(chips)]
        for cp in sent:
            cp.start()
        for j, chip in enumerate(chips):
            for a in range(n_arr):
                copy(a, 1 + j, (*chip, ci), me).wait_recv()
                onward = copy(a, 4 + j, (*chip, ci), sibling)
                onward.start()
                sent.append(onward)
        for a in range(n_arr):
            copy(a, 0, sibling, me).wait_recv()
        for j, chip in enumerate(chips):
            for a in range(n_arr):
                copy(a, 4 + j, (*chip, 1 - ci), me).wait_recv()
        for cp in sent:
            cp.wait_send()
        for cp in mine:
            cp.wait()

    sems = pltpu.SemaphoreType.DMA((per * n_arr,))
    return pl.pallas_call(
        body, name=name, out_shape=[jax.ShapeDtypeStruct((N_DEV,) + a.shape, a.dtype) for a in arrays],
        in_specs=[ANY] * n_arr, out_specs=[ANY] * n_arr,
        scratch_shapes=[sems, sems, pltpu.SemaphoreType.DMA((n_arr,))],
    )(*arrays)


def _exchange_copies(in_refs, out_refs, n_scattered, send_sems, recv_sems, local_sems):
    n_arr = len(in_refs)
    per = N_DEV - 1
    xi, yi, ci = lax.axis_index("x"), lax.axis_index("y"), lax.axis_index("c")
    me = 4 * xi + 2 * yi + ci
    src_of = lambda a, peer: in_refs[a].at[peer] if a < n_scattered else in_refs[a]
    copies = []
    for d in range(1, N_DEV):
        px = 1 - xi if d & 4 else xi
        py = 1 - yi if d & 2 else yi
        pc = 1 - ci if d & 1 else ci
        for a in range(n_arr):
            copies.append(pltpu.make_async_remote_copy(
                src_ref=src_of(a, 4 * px + 2 * py + pc), dst_ref=out_refs[a].at[me],
                send_sem=send_sems.at[per * a + d - 1], recv_sem=recv_sems.at[per * a + d - 1],
                device_id=(px, py, pc), device_id_type=MESH))
    own = [pltpu.make_async_copy(src_of(a, me), out_refs[a].at[me], local_sems.at[a]) for a in range(n_arr)]
    return copies + own


def _exchange_shapes(scattered, shared):
    return ([jax.ShapeDtypeStruct(a.shape, a.dtype) for a in scattered]
            + [jax.ShapeDtypeStruct((N_DEV,) + a.shape, a.dtype) for a in shared])


def _exchange_sems(n_arr):
    sems = pltpu.SemaphoreType.DMA(((N_DEV - 1) * n_arr,))
    return [sems, sems, pltpu.SemaphoreType.DMA((n_arr,))]


def _exchange(scattered, shared, name):
    n_sc = len(scattered)
    n_arr = n_sc + len(shared)

    def body(*refs):
        copies = _exchange_copies(refs[:n_arr], refs[n_arr:2 * n_arr], n_sc, *refs[2 * n_arr:])
        for cp in copies:
            cp.start()
        for cp in copies:
            cp.wait()

    return pl.pallas_call(
        body, name=name, out_shape=_exchange_shapes(scattered, shared),
        in_specs=[ANY] * n_arr, out_specs=[ANY] * n_arr, scratch_shapes=_exchange_sems(n_arr),
    )(*scattered, *shared)


def _adam_math(g, w, m, v):
    m_new = ADAM_B1 * m + (1.0 - ADAM_B1) * g
    v_new = ADAM_B2 * v + (1.0 - ADAM_B2) * (g * g)
    m_hat = m_new / (1.0 - ADAM_B1 ** ADAM_STEP)
    v_hat = v_new / (1.0 - ADAM_B2 ** ADAM_STEP)
    return -ADAM_LR * (m_hat / (jnp.sqrt(v_hat) + ADAM_EPS) + ADAM_WD * w), m_new, v_new


def _slot_sum(p_ref):
    g = p_ref[0].astype(F32)
    for s in range(1, N_DEV):
        g = g + p_ref[s].astype(F32)
    return g


def _adamw_replicated(parts, ws, ms, vs, loss_parts, name):
    n = len(ws)

    def body(*refs):
        p_refs, w_refs, m_refs, v_refs = refs[:n], refs[n:2 * n], refs[2 * n:3 * n], refs[3 * n:4 * n]
        outs = refs[4 * n + 1:]
        for j in range(n):
            g = _slot_sum(p_refs[j])
            outs[4 * j][...] = g
            for o_ref, val in zip(outs[4 * j + 1:4 * j + 4], _adam_math(g, w_refs[j][...], m_refs[j][...],
                                                                        v_refs[j][...])):
                o_ref[...] = val
        outs[4 * n][...] = _slot_sum(refs[4 * n])

    whole = pl.BlockSpec(memory_space=pltpu.VMEM)
    out = pl.pallas_call(
        body, name=name, in_specs=[whole] * (4 * n + 1), out_specs=[whole] * (4 * n + 1),
        out_shape=[jax.ShapeDtypeStruct(w.shape, F32) for w in ws for _ in range(4)]
                  + [jax.ShapeDtypeStruct(loss_parts.shape[1:], F32)],
    )(*parts, *ws, *ms, *vs, loss_parts)
    return [out[4 * j:4 * j + 4] for j in range(n)], out[4 * n]


def _adamw(parts, w, m, v, name):
    rows, cols = w.shape[-2:]
    tile = PACK_ROWS if rows % PACK_ROWS == 0 else rows
    at = (0,) if w.ndim == 3 else (Ellipsis,)

    def body(p_ref, w_ref, m_ref, v_ref, g_out, d_out, m_out, v_out):
        g = _slot_sum(p_ref)
        g_out[at] = g
        d_out[at], m_out[at], v_out[at] = _adam_math(g, w_ref[at], m_ref[at], v_ref[at])

    spec = (pl.BlockSpec((1, tile, cols), lambda i: (0, i, 0)) if w.ndim == 3
            else pl.BlockSpec((tile, cols), lambda i: (i, 0)))
    return pl.pallas_call(
        body, name=name, grid=(rows // tile,),
        in_specs=[pl.BlockSpec((N_DEV, tile, cols), lambda i: (0, i, 0)), spec, spec, spec],
        out_specs=[spec] * 4, out_shape=[jax.ShapeDtypeStruct(w.shape, F32)] * 4,
        compiler_params=_cparams(("parallel",)),
    )(parts, w, m, v)


EARLY = [("meta_tokens", 1), ("rwkv_w2", 1), ("rwkv_a2", 1), ("rwkv_g2", 1)]
LATE = [("w_br_attn", 1), ("w_br_rwkv", 1), ("w_o", 0), ("w_ffn_gate", 1), ("w_ffn_up", 1), ("w_ffn_down", 0)]
REPLICATED = ["norm_mix_g", "b_in", "attn_sinks", "rwkv_mix", "rwkv_w0", "rwkv_a0", "rwkv_k_k", "rwkv_k_a",
              "rwkv_r_k", "rwkv_ln_w", "rwkv_ln_b", "norm_ffn_g", "norm_final_g"]
WEIGHTS = ["meta_tokens", "norm_mix_g", "w_in", "b_in", "attn_sinks", "rwkv_mix", "rwkv_w0", "rwkv_w2", "rwkv_a0",
           "rwkv_a2", "rwkv_g2", "rwkv_k_k", "rwkv_k_a", "rwkv_r_k", "rwkv_ln_w", "rwkv_ln_b", "w_br_attn",
           "w_br_rwkv", "w_o", "norm_ffn_g", "w_ffn_gate", "w_ffn_up", "w_ffn_down", "norm_final_g"]


def _strip(name, a):
    return a if name in ("meta_tokens", "norm_final_g") else a[0]


def _join(gathered, axis):
    if axis == 0:
        return gathered.reshape(-1, gathered.shape[2])
    return gathered.transpose(1, 0, 2).reshape(gathered.shape[1], -1)


def _split(g, axis):
    if axis == 0:
        return g.reshape(N_DEV, -1, g.shape[1])
    return g.reshape(g.shape[0], N_DEV, -1).transpose(1, 0, 2)


W_IN_LAYOUT = [(2592, 4640), (768, 2304), (2432, 2592), 256 - GATE_LORA, (2304, 2368), 128 - DECAY_LORA,
               (2368, 2432), 128 - AAA_LORA, (0, 512), (512, 768), NP - C_VA - 128]


def _w_in_padded(w, shard_width=None):
    rows = w.shape[-2]
    width = D_IN if shard_width is None else shard_width
    parts = []
    for seg in W_IN_LAYOUT:
        if isinstance(seg, int):
            parts.append(jnp.zeros((rows, seg), w.dtype))
            continue
        lo, stop = seg
        while lo < stop:
            p = lo // width
            hi = min(stop, (p + 1) * width)
            src = w if shard_width is None else w[p]
            parts.append(src[:, lo - p * width:hi - p * width])
            lo = hi
    return jnp.concatenate(parts, axis=1)


def _w_in_unpadded(wp, lo=0, stop=D_IN):
    spans, pos = [], 0
    for seg in W_IN_LAYOUT:
        if isinstance(seg, int):
            pos += seg
        else:
            spans.append((seg[0], seg[1], pos))
            pos += seg[1] - seg[0]
    parts = []
    for a, b, at in sorted(spans):
        c, d = max(a, lo), min(b, stop)
        if c < d:
            parts.append(wp[:, at + c - a:at + d - a])
    return jnp.concatenate(parts, axis=1)


def _pad_rows(a, n):
    return jnp.pad(a, ((0, n - a.shape[0]), (0, 0)))


def _device_step(x, tgt, full, gather_late=None, scatter_early=None, scatter_last=None):
    seq = x.shape[0]
    nblk = seq // BLOCK
    lp = seq + BLOCK
    nall = nblk + 1

    w_in_p = full["w_in_p"]
    b_in_p = _w_in_padded(full["b_in"][None])
    mix = full["rwkv_mix"][None]
    mix_r, mix_k, mix_v = mix[:, 0:512], mix[:, 512:1024], mix[:, 1024:1536]
    mix_dw = jnp.pad(mix[:, 1536:1600], ((0, 0), (0, 64)))
    mix_da = jnp.pad(mix[:, 1600:1664], ((0, 0), (0, 64)))
    mix_dg = jnp.pad(mix[:, 1664:1824], ((0, 0), (0, 96)))
    w2_p = _pad_rows(full["rwkv_w2"].astype(F32), 128)
    a2_p = _pad_rows(full["rwkv_a2"].astype(F32), 128)
    g2_p = _pad_rows(full["rwkv_g2"].astype(F32), 256)
    row = lambda name: full[name].reshape(1, -1)
    sinks = row("attn_sinks")
    rope_c, rope_s1, rope_s2 = _rope_tables(lp)

    hpad = jnp.concatenate([jnp.zeros((PAD_ROWS, D_MODEL), F32), full["meta_tokens"].astype(F32), x], axis=0)
    (u,) = _rows_fwd(_rms_fn, [_view(hpad)], [row("norm_mix_g")], [D_MODEL], nblk=nall, name="norm_mix",
                     out_dtype=BF16, tile=TILE_ALL_BIG)
    proj = _mm(u, w_in_p, bias=b_in_p, name="in_proj")
    (q_r,) = _rows_fwd(_rope_fwd_fn, [_view(proj, 512, C_Q // 512), _view(rope_c), _view(rope_s1), _view(rope_s2)],
                       [], [512], nblk=nall, name="rope_q", tile=TILE_ALL_BIG)
    (k_r,) = _rows_fwd(_rope_fwd_fn, [_view(proj, 128, C_KA // 128), _view(rope_c), _view(rope_s1),
                                      _view(rope_s2)], [], [128], nblk=nall, name="rope_k", tile=TILE_ALL_BIG)

    rw_prev = jnp.pad(proj[:-1, C_R:C_R + 2048], ((1, 0), (0, 0)))
    pre_rows = [_view(proj, 512, C_R // 512), _view(proj, 512, C_K // 512), _view(proj, 512, C_V // 512),
                _view(proj, 128, C_DW // 128),
                _view(proj, 128, C_DA // 128), _view(proj, 256, C_DG // 256),
                _view(rw_prev, 512, 0), _view(rw_prev, 512, 1), _view(rw_prev, 512, 2), _view(rw_prev, 128, 14),
                _view(rw_prev, 128, 15), _view(rw_prev, 256, 6)]
    pre_consts = [mix_r, mix_k, mix_v, mix_dw, mix_da, mix_dg, row("rwkv_w0"), w2_p, row("rwkv_a0"), a2_p, g2_p,
                  row("rwkv_k_k"), row("rwkv_k_a")]
    xt_block = (BLOCK // SCAN_T * VEC_ROWS, LANES)
    r_t, k_mod, v_t, gate, xt = _rows_fwd(_rwkv_pre_xt_fn, pre_rows, pre_consts, [RW_DIM] * 4 + [xt_block],
                                          nblk=nall, name="rwkv_pre")
    xt = xt.reshape(-1, VEC_ROWS, LANES)
    spread, collect = _selectors()
    y_scan, hist, o_attn, lse, *late = _wkv_fwd(xt, v_t, spread, (q_r, k_r, proj, C_VA // 128, sinks), name="wkv_fwd",
                                                shared=gather_late[0] if gather_late else ())
    if gather_late:
        full = {**full, **gather_late[1](late)}
    post_rows = [_view(y_scan, off=1), _view(r_t, off=1), _view(k_mod, off=1), _view(v_t, off=1), _view(gate, off=1)]
    post_consts = [row("rwkv_ln_w"), row("rwkv_ln_b"), row("rwkv_r_k")]
    (y_rwkv,) = _rows_fwd(_rwkv_post_fn, post_rows, post_consts, [RW_DIM], nblk=nblk, name="rwkv_post",
                          out_dtype=BF16)

    ya = _mm(o_attn, full["w_br_attn"], name="br_attn")
    yr = _mm(y_rwkv, full["w_br_rwkv"], name="br_rwkv")
    merge_rows = [_view(ya), _view(yr), _view(proj, 1024, C_G1 // 1024, 1), _view(proj, 1024, C_G2 // 1024, 1)]
    (merged,) = _rows_fwd(_merge_fn, merge_rows, [], [D_MODEL], nblk=nblk, name="merge", out_dtype=BF16)
    h1 = _mm(merged, full["w_o"], residual=x, name="out_proj")
    (f,) = _rows_fwd(_rms_fn, [_view(h1)], [row("norm_ffn_g")], [D_MODEL], nblk=nblk, name="norm_ffn",
                     out_dtype=BF16, tile=TILE_REAL)
    ff_gate, ff_up, act = _mm_swiglu(f, full["w_ffn_gate"], full["w_ffn_up"], name="ffn_gate_up")
    h2 = _mm(act, full["w_ffn_down"], residual=h1, name="ffn_down")

    grads = {}
    ones_col = jnp.ones((seq, 1), F32)
    loss_rows, dh2, grads["norm_final_g"] = _rows_bwd(
        _loss_fn, [_view(h2), _view(tgt)], [row("norm_final_g")], [_view(ones_col)], nblk=nblk, name="loss",
        diff_rows=[0], diff_consts=[0], fwd_widths=[1], tile=TILE_REAL)
    loss = jnp.sum(loss_rows)

    grads["w_ffn_down"] = _mm(act, dh2, ta=True, name="dw_ffn_down")
    dgate, dup = _mm_swiglu_bwd(dh2, full["w_ffn_down"], ff_gate, ff_up, name="d_act_swiglu")
    grads["w_ffn_gate"] = _mm(f, dgate, ta=True, name="dw_ffn_gate")
    grads["w_ffn_up"] = _mm(f, dup, ta=True, name="dw_ffn_up")
    df = _mm(dgate, full["w_ffn_gate"], tb=True, name="df_gate")
    df = _mm(dup, full["w_ffn_up"], tb=True, residual=df, name="df_up")
    dh1, grads["norm_ffn_g"] = _rows_bwd(_rms_fn, [_view(h1)], [row("norm_ffn_g")], [_view(df)], nblk=nblk,
                                         name="norm_ffn_bwd", diff_rows=[0], diff_consts=[0], acc=[_view(dh2)],
                                         tile=TILE_REAL)
    dmerged = _mm(dh1, full["w_o"], tb=True, name="d_merged")
    grads["w_o"] = _mm(merged, dh1, ta=True, name="dw_o")
    dya, dyr, dg1, dg2 = _rows_bwd(_merge_fn, merge_rows, [], [_view(dmerged)], nblk=nblk, name="merge_bwd",
                                   diff_rows=[0, 1, 2, 3], diff_consts=[], row_dtype=BF16)
    grads["w_br_attn"] = _mm(o_attn, dya, ta=True, name="dw_br_attn")
    grads["w_br_rwkv"] = _mm(y_rwkv, dyr, ta=True, name="dw_br_rwkv")
    dy_attn = _mm(dya, full["w_br_attn"], tb=True, name="d_y_attn")
    dy_rwkv = _mm(dyr, full["w_br_rwkv"], tb=True, name="d_y_rwkv")

    post = _rows_bwd(_rwkv_post_fn, post_rows, post_consts, [_view(dy_rwkv)], nblk=nblk, name="rwkv_post_bwd",
                     diff_rows=[0, 1, 2, 3, 4], diff_consts=[0, 1, 2])
    dys, dr_post, dk_post, dv_post, dgate_post = post[:5]
    grads["rwkv_ln_w"], grads["rwkv_ln_b"], grads["rwkv_r_k"] = post[5:]
    dxt, dv_s, *early_parts = _wkv_bwd(xt, v_t, hist, dys, spread, collect, name="wkv_bwd",
                                       scattered=scatter_early(grads) if scatter_early else ())
    pre_cts = [_view(dxt.reshape(-1, LANES), rows=xt_block[0]), _view(dv_s)] + [
        _view(t, off=-1) for t in (dr_post, dk_post, dv_post, dgate_post)]
    pre = _rows_bwd(_rwkv_pre_fn, pre_rows, pre_consts, pre_cts, nblk=nall, name="rwkv_pre_bwd",
                    diff_rows=list(range(12)), diff_consts=list(range(13)), ct_map=_rwkv_pre_cts)
    d_cur, d_prev, d_par = pre[0:6], pre[6:12], pre[12:]
    up = lambda t: jnp.pad(t[1:], ((0, 1), (0, 0)))
    d_rw = [c + up(p) for c, p in zip(d_cur, d_prev)]
    grads["rwkv_mix"] = jnp.concatenate([d_par[0], d_par[1], d_par[2], d_par[3][:, :DECAY_LORA],
                                         d_par[4][:, :AAA_LORA], d_par[5][:, :GATE_LORA]], axis=1)
    grads["rwkv_w0"], grads["rwkv_w2"] = d_par[6], d_par[7][:DECAY_LORA]
    grads["rwkv_a0"], grads["rwkv_a2"] = d_par[8], d_par[9][:AAA_LORA]
    grads["rwkv_g2"] = d_par[10][:GATE_LORA]
    grads["rwkv_k_k"], grads["rwkv_k_a"] = d_par[11], d_par[12]

    dq_real, dk_r, dva, grads["attn_sinks"] = _attn_bwd(q_r, k_r, proj, C_VA // 128, sinks, o_attn, lse, dy_attn,
                                                        nblk=nblk, name="attn_bwd")
    dq_r = jnp.pad(dq_real, ((BLOCK, 0), (0, 0)))
    (dq,) = _rows_fwd(_rope_bwd_fn, [_view(dq_r), _view(rope_c), _view(rope_s1), _view(rope_s2)], [], [512],
                      nblk=nall, name="rope_q_bwd", out_dtype=BF16, tile=TILE_ALL_BIG)
    (dka,) = _rows_fwd(_rope_bwd_fn, [_view(dk_r), _view(rope_c), _view(rope_s1),
                                      _view(rope_s2)], [], [128], nblk=nall, name="rope_k_bwd", out_dtype=BF16,
                       tile=TILE_ALL_BIG)

    lead = lambda t: jnp.pad(t, ((BLOCK, 0), (0, 0)))
    pieces = [lead(dg1), lead(dg2), d_rw[0], d_rw[1], d_rw[2], d_rw[5], d_rw[3], d_rw[4], dq, dka, dva,
              jnp.zeros((lp, NP - C_VA - 128), BF16)]
    dproj = jnp.concatenate([p.astype(BF16) for p in pieces], axis=1)
    grads["w_in_p"] = _mm(u, dproj, ta=True, name="dw_in")
    du, db_in_p, *last_parts = _mm(dproj, w_in_p, tb=True, name="d_u", colsum_a=True,
                                   scattered=scatter_last(grads) if scatter_last else ())
    grads["b_in"] = _w_in_unpadded(db_in_p)
    dh, grads["norm_mix_g"] = _rows_bwd(_rms_fn, [_view(hpad)], [row("norm_mix_g")], [_view(du)], nblk=nall,
                                        name="norm_mix_bwd", diff_rows=[0], diff_consts=[0], acc=[_view(lead(dh1))],
                                        tile=TILE_ALL)
    grads["meta_tokens"] = dh[PAD_ROWS:BLOCK]
    return loss, dh[BLOCK:], grads, early_parts, last_parts


def kernel(x, meta_tokens, norm_mix_g, w_in, b_in, attn_sinks, rwkv_mix, rwkv_w0, rwkv_w2, rwkv_a0, rwkv_a2, rwkv_g2, rwkv_k_k, rwkv_k_a, rwkv_r_k, rwkv_ln_w, rwkv_ln_b, w_br_attn, w_br_rwkv, w_o, norm_ffn_g, w_ffn_gate, w_ffn_up, w_ffn_down, norm_final_g, loss_target, m_meta_tokens, m_norm_mix_g, m_w_in, m_b_in, m_attn_sinks, m_rwkv_mix, m_rwkv_w0, m_rwkv_w2, m_rwkv_a0, m_rwkv_a2, m_rwkv_g2, m_rwkv_k_k, m_rwkv_k_a, m_rwkv_r_k, m_rwkv_ln_w, m_rwkv_ln_b, m_w_br_attn, m_w_br_rwkv, m_w_o, m_norm_ffn_g, m_w_ffn_gate, m_w_ffn_up, m_w_ffn_down, m_norm_final_g, v_meta_tokens, v_norm_mix_g, v_w_in, v_b_in, v_attn_sinks, v_rwkv_mix, v_rwkv_w0, v_rwkv_w2, v_rwkv_a0, v_rwkv_a2, v_rwkv_g2, v_rwkv_k_k, v_rwkv_k_a, v_rwkv_r_k, v_rwkv_ln_w, v_rwkv_ln_b, v_w_br_attn, v_w_br_rwkv, v_w_o, v_norm_ffn_g, v_w_ffn_gate, v_w_ffn_up, v_w_ffn_down, v_norm_final_g):
    given = dict(locals())
    wts = {n: _strip(n, given[n]) for n in WEIGHTS}
    as_rows = lambda a: a.reshape(1, -1) if a.ndim == 1 else a
    width = wts["w_in"].shape[1]
    wire = lambda table: [wts[n].astype(BF16) for n, _ in table]

    w_in_all, *early_all = _all_gather([wts["w_in"].astype(BF16)] + wire(EARLY), name="gather_weights")
    full = {n: wts[n] for n in REPLICATED}
    full.update({n: _join(g, axis) for (n, axis), g in zip(EARLY, early_all)})
    full["w_in_p"] = _w_in_padded(w_in_all, shard_width=width)
    gather_late = (wire(LATE), lambda got: {n: _join(g, axis) for (n, axis), g in zip(LATE, got)})
    scatter_early = lambda g: [_split(g[n], axis).astype(BF16) for n, axis in LATE]
    scatter_last = lambda g: [jnp.stack([_w_in_unpadded(g["w_in_p"], p * width, (p + 1) * width)
                                         for p in range(N_DEV)]).astype(BF16)]

    loss_part, grad_x, grads, parts_late, (parts_w_in,) = _device_step(
        x[0], loss_target[0], full, gather_late, scatter_early, scatter_last)

    g_early = [_split(grads[n], axis).astype(BF16) for n, axis in EARLY]
    g_small = [grads[n].reshape(as_rows(given[n]).shape) for n in REPLICATED] + [jnp.full((8, LANES), loss_part)]
    got = _exchange(g_early, g_small, name="exchange_grads")
    parts_early, parts_small, parts_loss = got[:len(EARLY)], got[len(EARLY):-1], got[-1]
    results = [{}, {}, {}, {}]
    for (n, _), parts in zip([("w_in", 1)] + EARLY + LATE, [parts_w_in] + list(parts_early) + list(parts_late)):
        for kind, a in enumerate(_adamw(parts, given[n], given["m_" + n], given["v_" + n], name="adamw_" + n)):
            results[kind][n] = a
    small, loss = _adamw_replicated(parts_small, *[[as_rows(given[pre + n]) for n in REPLICATED]
                                                   for pre in ("", "m_", "v_")], parts_loss, name="adamw_replicated")
    for n, four in zip(REPLICATED, small):
        for kind, a in enumerate(four):
            results[kind][n] = a
    loss = loss[0, 0]
    out = [loss, grad_x[None]]
    for kind in range(4):
        out += [results[kind][n].reshape(given[n].shape) for n in WEIGHTS]
    return tuple(out)
```
